```python
import jax, jax.numpy as jnp
from jax import lax
import numpy as np

D_MODEL = 1024
BATCH = 8
SEQ = 8192
DEPTH = 4

CHUNK = 64
N_A = DEPTH // 2
N_B = DEPTH - N_A
N_MEM = 256
MEM_HEADS = 4
MEM_HEAD_DIM = D_MODEL // 16
MEM_W = MEM_HEADS * MEM_HEAD_DIM
GM_W = D_MODEL - MEM_W
GM_GROUPS = 6
GM_GC = GM_W // GM_GROUPS
GM_CHUNK = 128
SB_W = D_MODEL - MEM_W
SB_HEAD_DIM = 64
SB_HEADS = SB_W // SB_HEAD_DIM
SB_BLOCK = 128
D_FF = 2816
EPS = 1e-6

kernel_name = "yoco_gmlp_stickbreaking_macaron_memory"


def rms_norm(x, g):
    xf = x.astype(jnp.float32)
    r = lax.rsqrt(jnp.mean(xf * xf, axis=-1, keepdims=True) + EPS)
    return (xf * r).astype(x.dtype) * g


def half_swiglu(x, g, w_gate, w_up, w_down):
    h = rms_norm(x, g)
    return (jax.nn.silu(h @ w_gate) * (h @ w_up)) @ w_down


def memory_attention(q, mem_kv):
    b, s, _ = q.shape
    k, v = jnp.split(mem_kv, 2, axis=-1)
    q = q.reshape(b, s, MEM_HEADS, MEM_HEAD_DIM)
    k = k.reshape(b, N_MEM, MEM_HEADS, MEM_HEAD_DIM)
    v = v.reshape(b, N_MEM, MEM_HEADS, MEM_HEAD_DIM)
    scores = jnp.einsum('bshd,bmhd->bhsm', q, k).astype(jnp.float32) * (MEM_HEAD_DIM ** -0.5)
    p = jax.nn.softmax(scores, axis=-1).astype(v.dtype)
    return jnp.einsum('bhsm,bmhd->bshd', p, v).reshape(b, s, MEM_W)


def gmlp_chunk_mask():
    pos = np.arange(GM_CHUNK)
    return jnp.asarray((pos[None, :] // CHUNK) <= (pos[:, None] // CHUNK))


def gmlp_spatial_gating(u, v, v_gain, w_s, b_s):
    b, s, _ = u.shape
    v = rms_norm(v, v_gain)
    vb = v.reshape(b, s // GM_CHUNK, GM_CHUNK, GM_GROUPS, GM_GC)
    w = jnp.where(gmlp_chunk_mask()[None], w_s, 0.0).astype(v.dtype)
    mixed = jnp.einsum('gts,bnsgc->bntgc', w, vb) + b_s.T[None, None, :, :, None]
    return u * mixed.reshape(b, s, GM_W)


def stick_breaking_attention(q, k, v):
    b, s, h, d = q.shape
    nblk = s // SB_BLOCK
    qb = q.reshape(b, nblk, SB_BLOCK, h, d).transpose(1, 0, 2, 3, 4)
    starts = jnp.arange(nblk, dtype=jnp.int32) * SB_BLOCK
    key_pos = jnp.arange(s, dtype=jnp.int32)
    scale = d ** -0.5

    def block(args):
        qi, start = args
        z = jnp.einsum('bqhd,bkhd->bhqk', qi, k).astype(jnp.float32) * scale
        qpos = start + jnp.arange(SB_BLOCK, dtype=jnp.int32)
        causal = key_pos[None, :] < qpos[:, None]
        log_beta = jax.nn.log_sigmoid(z)
        log_one_minus = jnp.where(causal, jax.nn.log_sigmoid(-z), 0.0)
        suffix = lax.cumsum(log_one_minus, axis=3, reverse=True) - log_one_minus
        a = jnp.where(causal, jnp.exp(log_beta + suffix), 0.0).astype(v.dtype)
        return jnp.einsum('bhqk,bkhd->bqhd', a, v)

    out = lax.map(block, (qb, starts))
    return out.transpose(1, 0, 2, 3, 4).reshape(b, s, h * d)


def _fwd_setup_inputs(seed: int = 0) -> dict:
    key = jax.random.key(seed)
    ks = iter(jax.random.split(key, 32))
    f32 = jnp.float32

    def w(shape, fan_in):
        return jax.random.normal(next(ks), shape, f32) * (fan_in ** -0.5)

    def gain(shape):
        return 1.0 + 0.02 * jax.random.normal(next(ks), shape, f32)

    return {
        "x": jax.random.normal(next(ks), (BATCH, SEQ, D_MODEL), f32),
        "mem": jax.random.normal(next(ks), (BATCH, N_MEM, D_MODEL), f32),
        "ffn1_norm": gain((DEPTH, D_MODEL)),
        "ffn1_w_gate": w((DEPTH, D_MODEL, D_FF), D_MODEL),
        "ffn1_w_up": w((DEPTH, D_MODEL, D_FF), D_MODEL),
        "ffn1_w_down": w((DEPTH, D_FF, D_MODEL), D_FF),
        "mix_norm": gain((DEPTH, D_MODEL)),
        "ffn2_norm": gain((DEPTH, D_MODEL)),
        "ffn2_w_gate": w((DEPTH, D_MODEL, D_FF), D_MODEL),
        "ffn2_w_up": w((DEPTH, D_MODEL, D_FF), D_MODEL),
        "ffn2_w_down": w((DEPTH, D_FF, D_MODEL), D_FF),
        "mem_norm": gain((D_MODEL,)),
        "w_mem_kv": w((DEPTH, D_MODEL, 2 * MEM_W), D_MODEL),
        "a_w_in": w((N_A, D_MODEL, 2 * GM_W + MEM_W), D_MODEL),
        "a_v_norm": gain((N_A, GM_W)),
        "a_w_spatial": w((N_A, GM_GROUPS, GM_CHUNK, GM_CHUNK), GM_CHUNK),
        "a_b_spatial": gain((N_A, GM_GROUPS, GM_CHUNK)),
        "a_w_out": w((N_A, GM_W + MEM_W, D_MODEL), GM_W + MEM_W),
        "kv_norm": gain((D_MODEL,)),
        "w_kv": w((D_MODEL, 2 * SB_W), D_MODEL),
        "b_w_in": w((N_B, D_MODEL, SB_W + MEM_W), D_MODEL),
        "b_w_out": w((N_B, SB_W + MEM_W, D_MODEL), SB_W + MEM_W),
        "final_norm": gain((D_MODEL,)),
    }


def _fwd_reference(x, mem, ffn1_norm, ffn1_w_gate, ffn1_w_up, ffn1_w_down, mix_norm,
              ffn2_norm, ffn2_w_gate, ffn2_w_up, ffn2_w_down, mem_norm, w_mem_kv,
              a_w_in, a_v_norm, a_w_spatial, a_b_spatial, a_w_out,
              kv_norm, w_kv, b_w_in, b_w_out, final_norm):
    b, s, _ = x.shape
    mem_h = rms_norm(mem, mem_norm)
    shared_k = shared_v = None
    for l in range(DEPTH):
        if l == N_A:
            kv = rms_norm(x, kv_norm) @ w_kv
            shared_k, shared_v = jnp.split(kv, 2, axis=-1)
            shared_k = shared_k.reshape(b, s, SB_HEADS, SB_HEAD_DIM)
            shared_v = shared_v.reshape(b, s, SB_HEADS, SB_HEAD_DIM)

        x = x + 0.5 * half_swiglu(x, ffn1_norm[l], ffn1_w_gate[l], ffn1_w_up[l], ffn1_w_down[l])

        h = rms_norm(x, mix_norm[l])
        mem_kv = mem_h @ w_mem_kv[l]
        if l < N_A:
            i = l
            proj = h @ a_w_in[i]
            uv = jax.nn.gelu(proj[..., :2 * GM_W])
            u, v = uv[..., :GM_W], uv[..., GM_W:]
            q_mem = proj[..., 2 * GM_W:]
            y_tok = gmlp_spatial_gating(u, v, a_v_norm[i], a_w_spatial[i], a_b_spatial[i])
            y = jnp.concatenate([y_tok, memory_attention(q_mem, mem_kv)], axis=-1)
            x = x + y @ a_w_out[i]
        else:
            j = l - N_A
            proj = h @ b_w_in[j]
            q_sb = proj[..., :SB_W].reshape(b, s, SB_HEADS, SB_HEAD_DIM)
            q_mem = proj[..., SB_W:]
            y_tok = stick_breaking_attention(q_sb, shared_k, shared_v)
            y = jnp.concatenate([y_tok, memory_attention(q_mem, mem_kv)], axis=-1)
            x = x + y @ b_w_out[j]

        x = x + 0.5 * half_swiglu(x, ffn2_norm[l], ffn2_w_gate[l], ffn2_w_up[l], ffn2_w_down[l])
    return rms_norm(x, final_norm)


import jax as _jax
import jax.numpy as _jnp

TWIN_FORMAT = 'train_step'
FWD_PARAMS = ['x', 'mem', 'ffn1_norm', 'ffn1_w_gate', 'ffn1_w_up', 'ffn1_w_down', 'mix_norm', 'ffn2_norm', 'ffn2_w_gate', 'ffn2_w_up', 'ffn2_w_down', 'mem_norm', 'w_mem_kv', 'a_w_in', 'a_v_norm', 'a_w_spatial', 'a_b_spatial', 'a_w_out', 'kv_norm', 'w_kv', 'b_w_in', 'b_w_out', 'final_norm']
TWIN_WEIGHTS = ['ffn1_norm', 'ffn1_w_gate', 'ffn1_w_up', 'ffn1_w_down', 'mix_norm', 'ffn2_norm', 'ffn2_w_gate', 'ffn2_w_up', 'ffn2_w_down', 'mem_norm', 'w_mem_kv', 'a_w_in', 'a_v_norm', 'a_w_spatial', 'a_b_spatial', 'a_w_out', 'kv_norm', 'w_kv', 'b_w_in', 'b_w_out', 'final_norm']
TWIN_DIFF_INPUT = 'x'
TWIN_INPUTS = ['x', 'mem', 'ffn1_norm', 'ffn1_w_gate', 'ffn1_w_up', 'ffn1_w_down', 'mix_norm', 'ffn2_norm', 'ffn2_w_gate', 'ffn2_w_up', 'ffn2_w_down', 'mem_norm', 'w_mem_kv', 'a_w_in', 'a_v_norm', 'a_w_spatial', 'a_b_spatial', 'a_w_out', 'kv_norm', 'w_kv', 'b_w_in', 'b_w_out', 'final_norm', 'loss_target', 'm_ffn1_norm', 'm_ffn1_w_gate', 'm_ffn1_w_up', 'm_ffn1_w_down', 'm_mix_norm', 'm_ffn2_norm', 'm_ffn2_w_gate', 'm_ffn2_w_up', 'm_ffn2_w_down', 'm_mem_norm', 'm_w_mem_kv', 'm_a_w_in', 'm_a_v_norm', 'm_a_w_spatial', 'm_a_b_spatial', 'm_a_w_out', 'm_kv_norm', 'm_w_kv', 'm_b_w_in', 'm_b_w_out', 'm_final_norm', 'v_ffn1_norm', 'v_ffn1_w_gate', 'v_ffn1_w_up', 'v_ffn1_w_down', 'v_mix_norm', 'v_ffn2_norm', 'v_ffn2_w_gate', 'v_ffn2_w_up', 'v_ffn2_w_down', 'v_mem_norm', 'v_w_mem_kv', 'v_a_w_in', 'v_a_v_norm', 'v_a_w_spatial', 'v_a_b_spatial', 'v_a_w_out', 'v_kv_norm', 'v_w_kv', 'v_b_w_in', 'v_b_w_out', 'v_final_norm']
TWIN_OUTPUTS = ['loss', 'grad_x', 'grad_ffn1_norm', 'grad_ffn1_w_gate', 'grad_ffn1_w_up', 'grad_ffn1_w_down', 'grad_mix_norm', 'grad_ffn2_norm', 'grad_ffn2_w_gate', 'grad_ffn2_w_up', 'grad_ffn2_w_down', 'grad_mem_norm', 'grad_w_mem_kv', 'grad_a_w_in', 'grad_a_v_norm', 'grad_a_w_spatial', 'grad_a_b_spatial', 'grad_a_w_out', 'grad_kv_norm', 'grad_w_kv', 'grad_b_w_in', 'grad_b_w_out', 'grad_final_norm', 'delta_ffn1_norm', 'delta_ffn1_w_gate', 'delta_ffn1_w_up', 'delta_ffn1_w_down', 'delta_mix_norm', 'delta_ffn2_norm', 'delta_ffn2_w_gate', 'delta_ffn2_w_up', 'delta_ffn2_w_down', 'delta_mem_norm', 'delta_w_mem_kv', 'delta_a_w_in', 'delta_a_v_norm', 'delta_a_w_spatial', 'delta_a_b_spatial', 'delta_a_w_out', 'delta_kv_norm', 'delta_w_kv', 'delta_b_w_in', 'delta_b_w_out', 'delta_final_norm', 'new_m_ffn1_norm', 'new_m_ffn1_w_gate', 'new_m_ffn1_w_up', 'new_m_ffn1_w_down', 'new_m_mix_norm', 'new_m_ffn2_norm', 'new_m_ffn2_w_gate', 'new_m_ffn2_w_up', 'new_m_ffn2_w_down', 'new_m_mem_norm', 'new_m_w_mem_kv', 'new_m_a_w_in', 'new_m_a_v_norm', 'new_m_a_w_spatial', 'new_m_a_b_spatial', 'new_m_a_w_out', 'new_m_kv_norm', 'new_m_w_kv', 'new_m_b_w_in', 'new_m_b_w_out', 'new_m_final_norm', 'new_v_ffn1_norm', 'new_v_ffn1_w_gate', 'new_v_ffn1_w_up', 'new_v_ffn1_w_down', 'new_v_mix_norm', 'new_v_ffn2_norm', 'new_v_ffn2_w_gate', 'new_v_ffn2_w_up', 'new_v_ffn2_w_down', 'new_v_mem_norm', 'new_v_w_mem_kv', 'new_v_a_w_in', 'new_v_a_v_norm', 'new_v_a_w_spatial', 'new_v_a_b_spatial', 'new_v_a_w_out', 'new_v_kv_norm', 'new_v_w_kv', 'new_v_b_w_in', 'new_v_b_w_out', 'new_v_final_norm']
TWIN_LEAF_KINDS = {'loss': 'loss', 'grad_x': 'grad_x', 'grad_ffn1_norm': 'grad_w', 'grad_ffn1_w_gate': 'grad_w', 'grad_ffn1_w_up': 'grad_w', 'grad_ffn1_w_down': 'grad_w', 'grad_mix_norm': 'grad_w', 'grad_ffn2_norm': 'grad_w', 'grad_ffn2_w_gate': 'grad_w', 'grad_ffn2_w_up': 'grad_w', 'grad_ffn2_w_down': 'grad_w', 'grad_mem_norm': 'grad_w', 'grad_w_mem_kv': 'grad_w', 'grad_a_w_in': 'grad_w', 'grad_a_v_norm': 'grad_w', 'grad_a_w_spatial': 'grad_w', 'grad_a_b_spatial': 'grad_w', 'grad_a_w_out': 'grad_w', 'grad_kv_norm': 'grad_w', 'grad_w_kv': 'grad_w', 'grad_b_w_in': 'grad_w', 'grad_b_w_out': 'grad_w', 'grad_final_norm': 'grad_w', 'delta_ffn1_norm': 'delta_w', 'delta_ffn1_w_gate': 'delta_w', 'delta_ffn1_w_up': 'delta_w', 'delta_ffn1_w_down': 'delta_w', 'delta_mix_norm': 'delta_w', 'delta_ffn2_norm': 'delta_w', 'delta_ffn2_w_gate': 'delta_w', 'delta_ffn2_w_up': 'delta_w', 'delta_ffn2_w_down': 'delta_w', 'delta_mem_norm': 'delta_w', 'delta_w_mem_kv': 'delta_w', 'delta_a_w_in': 'delta_w', 'delta_a_v_norm': 'delta_w', 'delta_a_w_spatial': 'delta_w', 'delta_a_b_spatial': 'delta_w', 'delta_a_w_out': 'delta_w', 'delta_kv_norm': 'delta_w', 'delta_w_kv': 'delta_w', 'delta_b_w_in': 'delta_w', 'delta_b_w_out': 'delta_w', 'delta_final_norm': 'delta_w', 'new_m_ffn1_norm': 'new_m', 'new_m_ffn1_w_gate': 'new_m', 'new_m_ffn1_w_up': 'new_m', 'new_m_ffn1_w_down': 'new_m', 'new_m_mix_norm': 'new_m', 'new_m_ffn2_norm': 'new_m', 'new_m_ffn2_w_gate': 'new_m', 'new_m_ffn2_w_up': 'new_m', 'new_m_ffn2_w_down': 'new_m', 'new_m_mem_norm': 'new_m', 'new_m_w_mem_kv': 'new_m', 'new_m_a_w_in': 'new_m', 'new_m_a_v_norm': 'new_m', 'new_m_a_w_spatial': 'new_m', 'new_m_a_b_spatial': 'new_m', 'new_m_a_w_out': 'new_m', 'new_m_kv_norm': 'new_m', 'new_m_w_kv': 'new_m', 'new_m_b_w_in': 'new_m', 'new_m_b_w_out': 'new_m', 'new_m_final_norm': 'new_m', 'new_v_ffn1_norm': 'new_v', 'new_v_ffn1_w_gate': 'new_v', 'new_v_ffn1_w_up': 'new_v', 'new_v_ffn1_w_down': 'new_v', 'new_v_mix_norm': 'new_v', 'new_v_ffn2_norm': 'new_v', 'new_v_ffn2_w_gate': 'new_v', 'new_v_ffn2_w_up': 'new_v', 'new_v_ffn2_w_down': 'new_v', 'new_v_mem_norm': 'new_v', 'new_v_w_mem_kv': 'new_v', 'new_v_a_w_in': 'new_v', 'new_v_a_v_norm': 'new_v', 'new_v_a_w_spatial': 'new_v', 'new_v_a_b_spatial': 'new_v', 'new_v_a_w_out': 'new_v', 'new_v_kv_norm': 'new_v', 'new_v_w_kv': 'new_v', 'new_v_b_w_in': 'new_v', 'new_v_b_w_out': 'new_v', 'new_v_final_norm': 'new_v'}


def _forward(args):
    return _fwd_reference(*[args[k] for k in FWD_PARAMS])


def _output_shape():
    def fwd():
        inp = _fwd_setup_inputs(0)
        return _fwd_reference(*[inp[k] for k in FWD_PARAMS])
    out = _jax.eval_shape(fwd)
    return out.shape, out.dtype

N_MICROBATCH = 1
ADAM_LR = 0.001
ADAM_B1 = 0.9
ADAM_B2 = 0.999
ADAM_EPS = 1e-08
ADAM_WD = 0.01
ADAM_STEP = 10
PER_EXAMPLE_BATCH_AXIS = {'x': 0, 'mem': 0, 'loss_target': 0}
SHARED_INPUTS = []
_WEIGHT_DTYPES = {'ffn1_norm': _jnp.float32, 'ffn1_w_gate': _jnp.float32, 'ffn1_w_up': _jnp.float32, 'ffn1_w_down': _jnp.float32, 'mix_norm': _jnp.float32, 'ffn2_norm': _jnp.float32, 'ffn2_w_gate': _jnp.float32, 'ffn2_w_up': _jnp.float32, 'ffn2_w_down': _jnp.float32, 'mem_norm': _jnp.float32, 'w_mem_kv': _jnp.float32, 'a_w_in': _jnp.float32, 'a_v_norm': _jnp.float32, 'a_w_spatial': _jnp.float32, 'a_b_spatial': _jnp.float32, 'a_w_out': _jnp.float32, 'kv_norm': _jnp.float32, 'w_kv': _jnp.float32, 'b_w_in': _jnp.float32, 'b_w_out': _jnp.float32, 'final_norm': _jnp.float32}
MOMENT_SCALE = {'ffn1_norm': 9.553506e-02, 'ffn1_w_gate': 4.098637e-02, 'ffn1_w_up': 3.978642e-02, 'ffn1_w_down': 6.600693e-02, 'mix_norm': 1.391988e-01, 'ffn2_norm': 7.557528e-02, 'ffn2_w_gate': 3.246465e-02, 'ffn2_w_up': 3.163287e-02, 'ffn2_w_down': 5.251755e-02, 'mem_norm': 2.956717e-02, 'w_mem_kv': 1.959307e-02, 'a_w_in': 1.481900e-01, 'a_v_norm': 1.188041e-01, 'a_w_spatial': 1.202544e-01, 'a_b_spatial': 1.455804e-01, 'a_w_out': 1.712739e-01, 'kv_norm': 1.306426e-01, 'w_kv': 1.108069e-01, 'b_w_in': 3.684100e-02, 'b_w_out': 9.105945e-02, 'final_norm': 6.414025e+01}


def _to_microbatches(a, axis):
    t = _jnp.moveaxis(a, axis, 0)
    t = t.reshape((N_MICROBATCH, t.shape[0] // N_MICROBATCH) + t.shape[1:])
    return _jnp.moveaxis(t, 1, axis + 1)


def setup_inputs(seed: int = 0) -> dict:
    inp = _fwd_setup_inputs(seed)
    key = _jax.random.fold_in(_jax.random.key(seed), 7919)
    shape, _ = _output_shape()
    out = dict(inp)
    out["loss_target"] = _jax.random.normal(_jax.random.fold_in(key, 0), shape, _jnp.float32)
    for i, name in enumerate(TWIN_WEIGHTS):
        w = inp[name].astype(_jnp.float32)
        if MOMENT_SCALE is None:
            s = _jnp.sqrt(_jnp.mean(_jnp.square(w)) + 1e-30)
        else:
            s = MOMENT_SCALE[name]
        km, kv = _jax.random.split(_jax.random.fold_in(key, i + 1))
        out[name] = w
        out["m_" + name] = s * _jax.random.normal(km, w.shape, _jnp.float32)
        out["v_" + name] = (s * s) * _jax.random.uniform(kv, w.shape, _jnp.float32, 0.5, 1.5)
    if N_MICROBATCH > 1:
        for name, axis in PER_EXAMPLE_BATCH_AXIS.items():
            out[name] = _to_microbatches(out[name], axis)
    return {'x': out['x'], 'mem': out['mem'], 'ffn1_norm': out['ffn1_norm'], 'ffn1_w_gate': out['ffn1_w_gate'], 'ffn1_w_up': out['ffn1_w_up'], 'ffn1_w_down': out['ffn1_w_down'], 'mix_norm': out['mix_norm'], 'ffn2_norm': out['ffn2_norm'], 'ffn2_w_gate': out['ffn2_w_gate'], 'ffn2_w_up': out['ffn2_w_up'], 'ffn2_w_down': out['ffn2_w_down'], 'mem_norm': out['mem_norm'], 'w_mem_kv': out['w_mem_kv'], 'a_w_in': out['a_w_in'], 'a_v_norm': out['a_v_norm'], 'a_w_spatial': out['a_w_spatial'], 'a_b_spatial': out['a_b_spatial'], 'a_w_out': out['a_w_out'], 'kv_norm': out['kv_norm'], 'w_kv': out['w_kv'], 'b_w_in': out['b_w_in'], 'b_w_out': out['b_w_out'], 'final_norm': out['final_norm'], 'loss_target': out['loss_target'], 'm_ffn1_norm': out['m_ffn1_norm'], 'm_ffn1_w_gate': out['m_ffn1_w_gate'], 'm_ffn1_w_up': out['m_ffn1_w_up'], 'm_ffn1_w_down': out['m_ffn1_w_down'], 'm_mix_norm': out['m_mix_norm'], 'm_ffn2_norm': out['m_ffn2_norm'], 'm_ffn2_w_gate': out['m_ffn2_w_gate'], 'm_ffn2_w_up': out['m_ffn2_w_up'], 'm_ffn2_w_down': out['m_ffn2_w_down'], 'm_mem_norm': out['m_mem_norm'], 'm_w_mem_kv': out['m_w_mem_kv'], 'm_a_w_in': out['m_a_w_in'], 'm_a_v_norm': out['m_a_v_norm'], 'm_a_w_spatial': out['m_a_w_spatial'], 'm_a_b_spatial': out['m_a_b_spatial'], 'm_a_w_out': out['m_a_w_out'], 'm_kv_norm': out['m_kv_norm'], 'm_w_kv': out['m_w_kv'], 'm_b_w_in': out['m_b_w_in'], 'm_b_w_out': out['m_b_w_out'], 'm_final_norm': out['m_final_norm'], 'v_ffn1_norm': out['v_ffn1_norm'], 'v_ffn1_w_gate': out['v_ffn1_w_gate'], 'v_ffn1_w_up': out['v_ffn1_w_up'], 'v_ffn1_w_down': out['v_ffn1_w_down'], 'v_mix_norm': out['v_mix_norm'], 'v_ffn2_norm': out['v_ffn2_norm'], 'v_ffn2_w_gate': out['v_ffn2_w_gate'], 'v_ffn2_w_up': out['v_ffn2_w_up'], 'v_ffn2_w_down': out['v_ffn2_w_down'], 'v_mem_norm': out['v_mem_norm'], 'v_w_mem_kv': out['v_w_mem_kv'], 'v_a_w_in': out['v_a_w_in'], 'v_a_v_norm': out['v_a_v_norm'], 'v_a_w_spatial': out['v_a_w_spatial'], 'v_a_b_spatial': out['v_a_b_spatial'], 'v_a_w_out': out['v_a_w_out'], 'v_kv_norm': out['v_kv_norm'], 'v_w_kv': out['v_w_kv'], 'v_b_w_in': out['v_b_w_in'], 'v_b_w_out': out['v_b_w_out'], 'v_final_norm': out['v_final_norm']}


def _loss(weights, diff, rest, loss_target):
    with _jax.named_scope("forward"):
        args = {**rest, TWIN_DIFF_INPUT: diff, **{k: w.astype(_WEIGHT_DTYPES[k]) for k, w in weights.items()}}
        y = _forward(args)
    with _jax.named_scope("loss_head"):
        err = _jnp.square(y.astype(_jnp.float32) - loss_target)
        return 0.5 * _jnp.sum(_jnp.mean(err, axis=-1)) if err.ndim else 0.5 * err


def _adamw(w, g, m, v):
    m = ADAM_B1 * m + (1.0 - ADAM_B1) * g
    v = ADAM_B2 * v + (1.0 - ADAM_B2) * _jnp.square(g)
    m_hat = m / (1.0 - ADAM_B1 ** ADAM_STEP)
    v_hat = v / (1.0 - ADAM_B2 ** ADAM_STEP)
    delta = -ADAM_LR * (m_hat / (_jnp.sqrt(v_hat) + ADAM_EPS) + ADAM_WD * w)
    return delta, m, v


def reference(x, mem, ffn1_norm, ffn1_w_gate, ffn1_w_up, ffn1_w_down, mix_norm, ffn2_norm, ffn2_w_gate, ffn2_w_up, ffn2_w_down, mem_norm, w_mem_kv, a_w_in, a_v_norm, a_w_spatial, a_b_spatial, a_w_out, kv_norm, w_kv, b_w_in, b_w_out, final_norm, loss_target, m_ffn1_norm, m_ffn1_w_gate, m_ffn1_w_up, m_ffn1_w_down, m_mix_norm, m_ffn2_norm, m_ffn2_w_gate, m_ffn2_w_up, m_ffn2_w_down, m_mem_norm, m_w_mem_kv, m_a_w_in, m_a_v_norm, m_a_w_spatial, m_a_b_spatial, m_a_w_out, m_kv_norm, m_w_kv, m_b_w_in, m_b_w_out, m_final_norm, v_ffn1_norm, v_ffn1_w_gate, v_ffn1_w_up, v_ffn1_w_down, v_mix_norm, v_ffn2_norm, v_ffn2_w_gate, v_ffn2_w_up, v_ffn2_w_down, v_mem_norm, v_w_mem_kv, v_a_w_in, v_a_v_norm, v_a_w_spatial, v_a_b_spatial, v_a_w_out, v_kv_norm, v_w_kv, v_b_w_in, v_b_w_out, v_final_norm):
    given = dict(x=x, mem=mem, ffn1_norm=ffn1_norm, ffn1_w_gate=ffn1_w_gate, ffn1_w_up=ffn1_w_up, ffn1_w_down=ffn1_w_down, mix_norm=mix_norm, ffn2_norm=ffn2_norm, ffn2_w_gate=ffn2_w_gate, ffn2_w_up=ffn2_w_up, ffn2_w_down=ffn2_w_down, mem_norm=mem_norm, w_mem_kv=w_mem_kv, a_w_in=a_w_in, a_v_norm=a_v_norm, a_w_spatial=a_w_spatial, a_b_spatial=a_b_spatial, a_w_out=a_w_out, kv_norm=kv_norm, w_kv=w_kv, b_w_in=b_w_in, b_w_out=b_w_out, final_norm=final_norm, loss_target=loss_target, m_ffn1_norm=m_ffn1_norm, m_ffn1_w_gate=m_ffn1_w_gate, m_ffn1_w_up=m_ffn1_w_up, m_ffn1_w_down=m_ffn1_w_down, m_mix_norm=m_mix_norm, m_ffn2_norm=m_ffn2_norm, m_ffn2_w_gate=m_ffn2_w_gate, m_ffn2_w_up=m_ffn2_w_up, m_ffn2_w_down=m_ffn2_w_down, m_mem_norm=m_mem_norm, m_w_mem_kv=m_w_mem_kv, m_a_w_in=m_a_w_in, m_a_v_norm=m_a_v_norm, m_a_w_spatial=m_a_w_spatial, m_a_b_spatial=m_a_b_spatial, m_a_w_out=m_a_w_out, m_kv_norm=m_kv_norm, m_w_kv=m_w_kv, m_b_w_in=m_b_w_in, m_b_w_out=m_b_w_out, m_final_norm=m_final_norm, v_ffn1_norm=v_ffn1_norm, v_ffn1_w_gate=v_ffn1_w_gate, v_ffn1_w_up=v_ffn1_w_up, v_ffn1_w_down=v_ffn1_w_down, v_mix_norm=v_mix_norm, v_ffn2_norm=v_ffn2_norm, v_ffn2_w_gate=v_ffn2_w_gate, v_ffn2_w_up=v_ffn2_w_up, v_ffn2_w_down=v_ffn2_w_down, v_mem_norm=v_mem_norm, v_w_mem_kv=v_w_mem_kv, v_a_w_in=v_a_w_in, v_a_v_norm=v_a_v_norm, v_a_w_spatial=v_a_w_spatial, v_a_b_spatial=v_a_b_spatial, v_a_w_out=v_a_w_out, v_kv_norm=v_kv_norm, v_w_kv=v_w_kv, v_b_w_in=v_b_w_in, v_b_w_out=v_b_w_out, v_final_norm=v_final_norm)
    weights = {n: given[n] for n in TWIN_WEIGHTS}
    shared = {n: given[n] for n in SHARED_INPUTS}
    per_example = {n: given[n] for n in ['x', 'mem']}
    grad_fn = _jax.value_and_grad(_loss, argnums=(0, 1))

    def one_microbatch(ex, loss_target):
        ex = dict(ex)
        diff = ex.pop(TWIN_DIFF_INPUT)
        return grad_fn(weights, diff, {**shared, **ex}, loss_target)

    if N_MICROBATCH == 1:
        loss, (grad_w, grad_x) = one_microbatch(per_example, given["loss_target"])
    else:
        def body(carry, xs):
            loss_sum, grad_sum = carry
            l_k, (gw_k, gx_k) = one_microbatch(xs[0], xs[1])
            with _jax.named_scope("update"):
                return (loss_sum + l_k, _jax.tree.map(_jnp.add, grad_sum, gw_k)), gx_k

        init = (_jnp.zeros((), _jnp.float32), _jax.tree.map(_jnp.zeros_like, weights))
        (loss, grad_w), grad_x = _jax.lax.scan(body, init, (per_example, given["loss_target"]))
    with _jax.named_scope("update"):
        delta_w, new_m, new_v = {}, {}, {}
        for n in TWIN_WEIGHTS:
            delta_w[n], new_m[n], new_v[n] = _adamw(weights[n], grad_w[n], given["m_" + n], given["v_" + n])
    return (loss, grad_x, *[grad_w[n] for n in TWIN_WEIGHTS], *[delta_w[n] for n in TWIN_WEIGHTS],
            *[new_m[n] for n in TWIN_WEIGHTS], *[new_v[n] for n in TWIN_WEIGHTS])
```

```python
import functools

import jax
import jax.numpy as jnp
from jax import lax
from jax.experimental import pallas as pl
from jax.experimental.pallas import tpu as pltpu

F32, BF16 = jnp.float32, jnp.bfloat16
MESH_ID = pl.DeviceIdType.MESH
AXES = ("x", "y", "c")
N_DEV = 8

EPS = 1e-6
DEPTH, N_A = 4, 2
GM_W, GM_GROUPS, GM_P = 768, 6, 128
MEM_W, MEM_HEADS, HEAD_DIM = 256, 4, 64
SB_W, SB_BLK = 768, 128
LANES = 128
QK_SCALE = HEAD_DIM ** -0.5
GELU_C, GELU_A = 0.7978845608028654, 0.044715

ADAM_LR, ADAM_B1, ADAM_B2, ADAM_EPS, ADAM_WD, ADAM_STEP = 0.001, 0.9, 0.999, 1e-08, 0.01, 10

VMEM_LIMIT = 56 * 1024 * 1024
PACK_COLS = 512

NT = (((1,), (1,)), ((), ()))
TN = (((0,), (0,)), ((), ()))


def _params(*sem):
    return pltpu.CompilerParams(dimension_semantics=sem, vmem_limit_bytes=VMEM_LIMIT)


def _tile(n, target, mult=LANES):
    best = None
    for t in range(mult, min(n, target) + 1, mult):
        if n % t == 0:
            best = t
    return best if best is not None else n


def _dot(a, b, dims=None):
    if dims is None:
        return jnp.dot(a, b, preferred_element_type=F32)
    return lax.dot_general(a, b, dims, preferred_element_type=F32)


def exchange(src, group, same_src, name):
    size = {"pair": 2, "quad": 4, "all": 8}[group]
    chunk_shape = src.shape if same_src else src.shape[1:]

    def body(src_ref, out_ref, send_sems, recv_sems, local_sem):
        x, y, c = lax.axis_index("x"), lax.axis_index("y"), lax.axis_index("c")
        if group == "pair":
            me, dev = c, lambda p: (x, y, p)
        elif group == "quad":
            me, dev = 2 * x + y, lambda p: (p // 2, p % 2, c)
        else:
            me, dev = 4 * x + 2 * y + c, lambda p: (p // 4, (p // 2) % 2, p % 2)

        def chunk(idx):
            return src_ref if same_src else src_ref.at[idx]

        def copy(k, idx, slot, peer):
            return pltpu.make_async_remote_copy(
                src_ref=chunk(idx), dst_ref=out_ref.at[slot], send_sem=send_sems.at[k], recv_sem=recv_sems.at[k],
                device_id=dev(peer), device_id_type=MESH_ID)

        local = pltpu.make_async_copy(chunk(me), out_ref.at[me], local_sem)
        local.start()
        sends = []
        for k in range(1, size):
            peer = (me + k) % size
            sends.append(copy(k, peer, me, peer))
            sends[-1].start()
        for k in range(1, size):
            sender = (me + size - k) % size
            copy(k, me, sender, sender).wait_recv()
        for cp in sends:
            cp.wait_send()
        local.wait()

    return pl.pallas_call(
        body, name=name,
        out_shape=jax.ShapeDtypeStruct((size,) + tuple(chunk_shape), src.dtype),
        in_specs=[pl.BlockSpec(memory_space=pltpu.HBM)],
        out_specs=pl.BlockSpec(memory_space=pltpu.HBM),
        scratch_shapes=[pltpu.SemaphoreType.DMA((size,)), pltpu.SemaphoreType.DMA((size,)), pltpu.SemaphoreType.DMA],
    )(src)


def sum_leading(parts, out_dtype, name):
    k, rows, cols = parts.shape
    tr = _tile(rows, 512, 16)

    def body(p_ref, o_ref):
        acc = p_ref[0].astype(F32)
        for s in range(1, k):
            acc = acc + p_ref[s].astype(F32)
        o_ref[...] = acc.astype(o_ref.dtype)

    return pl.pallas_call(
        body, name=name, grid=(rows // tr,),
        in_specs=[pl.BlockSpec((k, tr, cols), lambda i: (0, i, 0))],
        out_specs=pl.BlockSpec((tr, cols), lambda i: (i, 0)),
        out_shape=jax.ShapeDtypeStruct((rows, cols), out_dtype),
        compiler_params=_params("arbitrary"),
    )(parts)


def _rms(xf):
    return lax.rsqrt(jnp.mean(xf * xf, axis=-1, keepdims=True) + EPS)


def norm_mm(x, g, w, out_dtype, name, emit_h=False):
    m, d = x.shape
    n = w.shape[1]
    tm, tn = _tile(m, 512, 8), _tile(n, 512)

    def body(x_ref, g_ref, w_ref, o_ref, *rest):
        h_ref = rest[-1]

        @pl.when(pl.program_id(1) == 0)
        def _():
            xf = x_ref[...]
            hb = ((xf * _rms(xf)) * g_ref[...]).astype(BF16)
            h_ref[...] = hb
            if emit_h:
                rest[0][...] = hb

        o_ref[...] = _dot(h_ref[...], w_ref[...]).astype(o_ref.dtype)

    out_shape = [jax.ShapeDtypeStruct((m, n), out_dtype)]
    out_specs = [pl.BlockSpec((tm, tn), lambda i, j: (i, j))]
    if emit_h:
        out_shape.append(jax.ShapeDtypeStruct((m, d), BF16))
        out_specs.append(pl.BlockSpec((tm, d), lambda i, j: (i, 0)))
    res = pl.pallas_call(
        body, name=name, grid=(m // tm, n // tn),
        in_specs=[pl.BlockSpec((tm, d), lambda i, j: (i, 0)), pl.BlockSpec((1, d), lambda i, j: (0, 0)),
                  pl.BlockSpec((d, tn), lambda i, j: (0, j))],
        out_specs=out_specs, out_shape=out_shape,
        scratch_shapes=[pltpu.VMEM((tm, d), BF16)],
        compiler_params=_params("arbitrary", "arbitrary"),
    )(x, g, w)
    return res if emit_h else res[0]


def mm_res(a, w, res, alpha, name):
    m, k = a.shape
    n = w.shape[1]
    tm, tn = _tile(m, 512, 8), _tile(n, 512)

    def body(a_ref, w_ref, r_ref, o_ref):
        o_ref[...] = r_ref[...] + alpha * _dot(a_ref[...], w_ref[...])

    return pl.pallas_call(
        body, name=name, grid=(m // tm, n // tn),
        in_specs=[pl.BlockSpec((tm, k), lambda i, j: (i, 0)), pl.BlockSpec((k, tn), lambda i, j: (0, j)),
                  pl.BlockSpec((tm, tn), lambda i, j: (i, j))],
        out_specs=pl.BlockSpec((tm, tn), lambda i, j: (i, j)),
        out_shape=jax.ShapeDtypeStruct((m, n), F32),
        compiler_params=_params("arbitrary", "arbitrary"),
    )(a, w, res)


def mm_nt(x, w, alpha, name):
    m, d = x.shape
    n = w.shape[0]
    tm, tn = _tile(m, 512, 8), _tile(n, 512)

    def body(x_ref, w_ref, o_ref, xb_ref):
        @pl.when(pl.program_id(1) == 0)
        def _():
            xb_ref[...] = x_ref[...].astype(BF16)

        o_ref[...] = (alpha * _dot(xb_ref[...], w_ref[...], NT)).astype(o_ref.dtype)

    return pl.pallas_call(
        body, name=name, grid=(m // tm, n // tn),
        in_specs=[pl.BlockSpec((tm, d), lambda i, j: (i, 0)), pl.BlockSpec((tn, d), lambda i, j: (j, 0))],
        out_specs=pl.BlockSpec((tm, tn), lambda i, j: (i, j)),
        out_shape=jax.ShapeDtypeStruct((m, n), BF16),
        scratch_shapes=[pltpu.VMEM((tm, d), BF16)],
        compiler_params=_params("arbitrary", "arbitrary"),
    )(x, w)


def mm_tn(a, b, alpha, name, ta_target=1024, tb_target=512):
    s, ka = a.shape
    nb = b.shape[1]
    ta, tb, ts = _tile(ka, ta_target), _tile(nb, tb_target), _tile(s, 512, 16)
    steps = s // ts

    def body(a_ref, b_ref, o_ref, acc_ref):
        t = pl.program_id(2)

        @pl.when(t == 0)
        def _():
            acc_ref[...] = jnp.zeros_like(acc_ref)

        acc_ref[...] += _dot(a_ref[...].astype(BF16), b_ref[...].astype(BF16), TN)

        @pl.when(t == steps - 1)
        def _():
            o_ref[...] = alpha * acc_ref[...]

    return pl.pallas_call(
        body, name=name, grid=(ka // ta, nb // tb, steps),
        in_specs=[pl.BlockSpec((ts, ta), lambda i, j, t: (t, i)), pl.BlockSpec((ts, tb), lambda i, j, t: (t, j))],
        out_specs=pl.BlockSpec((ta, tb), lambda i, j, t: (i, j)),
        out_shape=jax.ShapeDtypeStruct((ka, nb), F32),
        scratch_shapes=[pltpu.VMEM((ta, tb), F32)],
        compiler_params=_params("arbitrary", "arbitrary", "arbitrary"),
    )(a, b)


def mm_nt_normbwd(dy, w, x, g, res, name):
    m, n = dy.shape
    d = w.shape[0]
    tm, tk = _tile(m, 512, 8), _tile(n, 512)
    steps = n // tk
    has_res = res is not None

    def body(*refs):
        if has_res:
            dy_ref, w_ref, x_ref, g_ref, r_ref, dx_ref, dg_ref, h_ref, acc_ref = refs
        else:
            dy_ref, w_ref, x_ref, g_ref, dx_ref, dg_ref, h_ref, acc_ref = refs
        i, t = pl.program_id(0), pl.program_id(1)

        @pl.when(t == 0)
        def _():
            acc_ref[...] = jnp.zeros_like(acc_ref)

        @pl.when((t == 0) & (i == 0))
        def _():
            dg_ref[...] = jnp.zeros_like(dg_ref)

        acc_ref[...] += _dot(dy_ref[...], w_ref[...], NT)

        @pl.when(t == steps - 1)
        def _():
            xf = x_ref[...]
            r = _rms(xf)
            xhat = xf * r
            dh = acc_ref[...]
            gain = g_ref[...]
            dg_ref[...] += jnp.sum(dh * xhat, axis=0, keepdims=True)
            dxhat = dh * gain
            dx = r * (dxhat - xhat * jnp.mean(dxhat * xhat, axis=-1, keepdims=True))
            dx_ref[...] = (r_ref[...] + dx) if has_res else dx
            h_ref[...] = (xhat * gain).astype(BF16)

    row = lambda i, t: (i, 0)
    in_specs = [pl.BlockSpec((tm, tk), lambda i, t: (i, t)), pl.BlockSpec((d, tk), lambda i, t: (0, t)),
                pl.BlockSpec((tm, d), row), pl.BlockSpec((1, d), lambda i, t: (0, 0))]
    args = [dy, w, x, g]
    if has_res:
        in_specs.append(pl.BlockSpec((tm, d), row))
        args.append(res)
    return pl.pallas_call(
        body, name=name, grid=(m // tm, steps),
        in_specs=in_specs,
        out_specs=[pl.BlockSpec((tm, d), row), pl.BlockSpec((1, d), lambda i, t: (0, 0)), pl.BlockSpec((tm, d), row)],
        out_shape=[jax.ShapeDtypeStruct((m, d), F32), jax.ShapeDtypeStruct((1, d), F32), jax.ShapeDtypeStruct((m, d), BF16)],
        scratch_shapes=[pltpu.VMEM((tm, d), F32)],
        compiler_params=_params("arbitrary", "arbitrary"),
    )(*args)


def _sigmoid(z):
    return 1.0 / (1.0 + jnp.exp(-z))


def swiglu_fwd(gu, name):
    m, f2 = gu.shape
    f = f2 // 2
    tm = _tile(m, 256, 16)

    def body(gu_ref, o_ref):
        gate, up = gu_ref[:, :f].astype(F32), gu_ref[:, f:].astype(F32)
        o_ref[...] = (gate * _sigmoid(gate) * up).astype(BF16)

    return pl.pallas_call(
        body, name=name, grid=(m // tm,),
        in_specs=[pl.BlockSpec((tm, f2), lambda i: (i, 0))], out_specs=pl.BlockSpec((tm, f), lambda i: (i, 0)),
        out_shape=jax.ShapeDtypeStruct((m, f), BF16), compiler_params=_params("arbitrary"),
    )(gu)


def swiglu_bwd(gu, da, name):
    m, f2 = gu.shape
    f = f2 // 2
    tm = _tile(m, 256, 16)

    def body(gu_ref, da_ref, o_ref):
        gate, up = gu_ref[:, :f].astype(F32), gu_ref[:, f:].astype(F32)
        d = da_ref[...].astype(F32)
        sg = _sigmoid(gate)
        o_ref[:, :f] = (d * up * (sg * (1.0 + gate * (1.0 - sg)))).astype(BF16)
        o_ref[:, f:] = (d * (gate * sg)).astype(BF16)

    return pl.pallas_call(
        body, name=name, grid=(m // tm,),
        in_specs=[pl.BlockSpec((tm, f2), lambda i: (i, 0)), pl.BlockSpec((tm, f), lambda i: (i, 0))],
        out_specs=pl.BlockSpec((tm, f2), lambda i: (i, 0)),
        out_shape=jax.ShapeDtypeStruct((m, f2), BF16), compiler_params=_params("arbitrary"),
    )(gu, da)


def _gelu(x):
    return 0.5 * x * (1.0 + jnp.tanh(GELU_C * (x + GELU_A * x * x * x)))


def _gelu_grad(x):
    t = jnp.tanh(GELU_C * (x + GELU_A * x * x * x))
    return 0.5 * (1.0 + t) + 0.5 * x * (1.0 - t * t) * (GELU_C * (1.0 + 3.0 * GELU_A * x * x))


def _chunk_mask():
    row = lax.broadcasted_iota(jnp.int32, (GM_P, GM_P), 0)
    col = lax.broadcasted_iota(jnp.int32, (GM_P, GM_P), 1)
    return (col < GM_P // 2) | (row >= GM_P // 2)


def gmlp_fwd(proj, gain, w_s, bias, name):
    s, pw = proj.shape
    tm = _tile(s, 256, GM_P)

    def body(p_ref, gain_ref, w_ref, b_ref, o_ref):
        mask = _chunk_mask()
        u = _gelu(p_ref[:, :GM_W])
        v = _gelu(p_ref[:, GM_W:2 * GM_W])
        vn = ((v * _rms(v)) * gain_ref[...]).astype(BF16)
        for g in range(GM_GROUPS):
            wg = jnp.where(mask, w_ref[g], 0.0).astype(BF16)
            cols = slice(g * GM_P, (g + 1) * GM_P)
            for n in range(tm // GM_P):
                rows = slice(n * GM_P, (n + 1) * GM_P)
                mixed = _dot(wg, vn[rows, cols]) + b_ref[:, cols]
                o_ref[rows, cols] = (u[rows, cols] * mixed).astype(BF16)

    return pl.pallas_call(
        body, name=name, grid=(s // tm,),
        in_specs=[pl.BlockSpec((tm, pw), lambda i: (i, 0)), pl.BlockSpec((1, GM_W), lambda i: (0, 0)),
                  pl.BlockSpec((GM_GROUPS, GM_P, GM_P), lambda i: (0, 0, 0)), pl.BlockSpec((GM_P, GM_W), lambda i: (0, 0))],
        out_specs=pl.BlockSpec((tm, GM_W), lambda i: (i, 0)),
        out_shape=jax.ShapeDtypeStruct((s, GM_W), BF16), compiler_params=_params("arbitrary"),
    )(proj, gain, w_s, bias)


def gmlp_bwd(proj, dy, gain, w_s, bias, name):
    s, pw = proj.shape
    dw_total = dy.shape[1]
    tm = _tile(s, 256, GM_P)

    def body(p_ref, dy_ref, gain_ref, w_ref, b_ref, dp_ref, dw_ref, db_ref, dgain_ref, dvn_ref):
        @pl.when(pl.program_id(0) == 0)
        def _():
            dw_ref[...] = jnp.zeros_like(dw_ref)
            db_ref[...] = jnp.zeros_like(db_ref)
            dgain_ref[...] = jnp.zeros_like(dgain_ref)

        mask = _chunk_mask()
        pu = p_ref[:, :GM_W]
        pv = p_ref[:, GM_W:2 * GM_W]
        u = _gelu(pu)
        v = _gelu(pv)
        r = _rms(v)
        vhat = v * r
        gain = gain_ref[...]
        vn = (vhat * gain).astype(BF16)
        gu_grad = _gelu_grad(pu)
        for g in range(GM_GROUPS):
            wg = jnp.where(mask, w_ref[g], 0.0).astype(BF16)
            cols = slice(g * GM_P, (g + 1) * GM_P)
            dw_acc = jnp.zeros((GM_P, GM_P), F32)
            db_acc = jnp.zeros((GM_P, 1), F32)
            for n in range(tm // GM_P):
                rows = slice(n * GM_P, (n + 1) * GM_P)
                dyb = dy_ref[rows, cols].astype(F32)
                vnb = vn[rows, cols]
                mixed = _dot(wg, vnb) + b_ref[:, cols]
                dmixed = dyb * u[rows, cols]
                dmb = dmixed.astype(BF16)
                dp_ref[rows, cols] = (dyb * mixed * gu_grad[rows, cols]).astype(BF16)
                dw_acc = dw_acc + _dot(dmb, vnb, NT)
                db_acc = db_acc + jnp.sum(dmixed, axis=1, keepdims=True)
                dvn_ref[rows, cols] = _dot(wg, dmb, TN)
            dw_ref[g] += jnp.where(mask, dw_acc, 0.0)
            db_ref[g] += jnp.broadcast_to(db_acc, (GM_P, GM_P))
        dvn = dvn_ref[...]
        dgain_ref[...] += jnp.sum(dvn * vhat, axis=0, keepdims=True)
        dvhat = dvn * gain
        dv = r * (dvhat - vhat * jnp.mean(dvhat * vhat, axis=-1, keepdims=True))
        dp_ref[:, GM_W:] = (dv * _gelu_grad(pv)).astype(BF16)

    const3 = lambda i: (0, 0, 0)
    return pl.pallas_call(
        body, name=name, grid=(s // tm,),
        in_specs=[pl.BlockSpec((tm, pw), lambda i: (i, 0)), pl.BlockSpec((tm, dw_total), lambda i: (i, 0)),
                  pl.BlockSpec((1, GM_W), lambda i: (0, 0)), pl.BlockSpec((GM_GROUPS, GM_P, GM_P), const3),
                  pl.BlockSpec((GM_P, GM_W), lambda i: (0, 0))],
        out_specs=[pl.BlockSpec((tm, 2 * GM_W), lambda i: (i, 0)), pl.BlockSpec((GM_GROUPS, GM_P, GM_P), const3),
                   pl.BlockSpec((GM_GROUPS, GM_P, GM_P), const3), pl.BlockSpec((1, GM_W), lambda i: (0, 0))],
        out_shape=[jax.ShapeDtypeStruct((s, 2 * GM_W), BF16), jax.ShapeDtypeStruct((GM_GROUPS, GM_P, GM_P), F32),
                   jax.ShapeDtypeStruct((GM_GROUPS, GM_P, GM_P), F32), jax.ShapeDtypeStruct((1, GM_W), F32)],
        scratch_shapes=[pltpu.VMEM((tm, GM_W), F32)],
        compiler_params=_params("arbitrary"),
    )(proj, dy, gain, w_s, bias)


def _keep(mask, xb):
    return jnp.where(mask, xb.astype(F32), 0.0).astype(BF16)


def _head_masks(rows, width, heads):
    lane = lax.broadcasted_iota(jnp.int32, (rows, width), 1)
    return [(lane >= HEAD_DIM * h) & (lane < HEAD_DIM * (h + 1)) for h in range(heads)]


def _mem_probs(qh, k):
    sc = _dot(qh, k, NT) * QK_SCALE
    e = jnp.exp(sc - jnp.max(sc, axis=-1, keepdims=True))
    return e / jnp.sum(e, axis=-1, keepdims=True)


def mem_fwd(proj, q_blk, mem_kv, layer, name):
    s = proj.shape[0]
    n_mem = mem_kv.shape[0]
    tm = _tile(s, 512, 16)

    def body(q_ref, k_ref, v_ref, o_ref):
        q = q_ref[...].astype(BF16)
        k, v = k_ref[...], v_ref[...]
        out = jnp.zeros((tm, MEM_W), F32)
        for hm in _head_masks(tm, MEM_W, MEM_HEADS):
            p = _mem_probs(_keep(hm, q), k)
            out = out + jnp.where(hm, _dot(p.astype(BF16), v), 0.0)
        o_ref[...] = out.astype(BF16)

    return pl.pallas_call(
        body, name=name, grid=(s // tm,),
        in_specs=[pl.BlockSpec((tm, MEM_W), lambda i: (i, q_blk)), pl.BlockSpec((n_mem, MEM_W), lambda i: (0, 2 * layer)),
                  pl.BlockSpec((n_mem, MEM_W), lambda i: (0, 2 * layer + 1))],
        out_specs=pl.BlockSpec((tm, MEM_W), lambda i: (i, 0)),
        out_shape=jax.ShapeDtypeStruct((s, MEM_W), BF16), compiler_params=_params("arbitrary"),
    )(proj, mem_kv, mem_kv)


def mem_bwd(proj, q_blk, mem_kv, layer, dy, dy_blk, name):
    s = proj.shape[0]
    n_mem = mem_kv.shape[0]
    tm = _tile(s, 512, 16)

    def body(q_ref, k_ref, v_ref, dy_ref, dq_ref, dk_ref, dv_ref):
        @pl.when(pl.program_id(0) == 0)
        def _():
            dk_ref[...] = jnp.zeros_like(dk_ref)
            dv_ref[...] = jnp.zeros_like(dv_ref)

        q = q_ref[...].astype(BF16)
        k, v = k_ref[...], v_ref[...]
        dy = dy_ref[...]
        dq = jnp.zeros((tm, MEM_W), F32)
        dk = jnp.zeros((n_mem, MEM_W), F32)
        dv = jnp.zeros((n_mem, MEM_W), F32)
        for hm in _head_masks(tm, MEM_W, MEM_HEADS):
            qh = _keep(hm, q)
            dyh = _keep(hm, dy)
            p = _mem_probs(qh, k)
            dp = _dot(dyh, v, NT)
            dv = dv + _dot(p.astype(BF16), dyh, TN)
            ds = (p * (dp - jnp.sum(dp * p, axis=-1, keepdims=True)) * QK_SCALE).astype(BF16)
            dq = dq + jnp.where(hm, _dot(ds, k), 0.0)
            dk = dk + _dot(ds, qh, TN)
        dq_ref[...] = dq.astype(BF16)
        dk_ref[...] += dk
        dv_ref[...] += dv

    const = lambda i: (0, 0)
    return pl.pallas_call(
        body, name=name, grid=(s // tm,),
        in_specs=[pl.BlockSpec((tm, MEM_W), lambda i: (i, q_blk)), pl.BlockSpec((n_mem, MEM_W), lambda i: (0, 2 * layer)),
                  pl.BlockSpec((n_mem, MEM_W), lambda i: (0, 2 * layer + 1)), pl.BlockSpec((tm, MEM_W), lambda i: (i, dy_blk))],
        out_specs=[pl.BlockSpec((tm, MEM_W), lambda i: (i, 0)), pl.BlockSpec((n_mem, MEM_W), const),
                   pl.BlockSpec((n_mem, MEM_W), const)],
        out_shape=[jax.ShapeDtypeStruct((s, MEM_W), BF16), jax.ShapeDtypeStruct((n_mem, MEM_W), F32),
                   jax.ShapeDtypeStruct((n_mem, MEM_W), F32)],
        compiler_params=_params("arbitrary"),
    )(proj, mem_kv, mem_kv, dy)


def _split(xf):
    hi = xf.astype(BF16)
    return hi, (xf - hi.astype(F32)).astype(BF16)


def _sb_consts():
    row = lax.broadcasted_iota(jnp.int32, (SB_BLK, SB_BLK), 0)
    col = lax.broadcasted_iota(jnp.int32, (SB_BLK, SB_BLK), 1)
    after = jnp.where(row > col, 1.0, 0.0).astype(BF16)
    from_ = jnp.where(row >= col, 1.0, 0.0).astype(BF16)
    causal = col < row
    heads = [col < HEAD_DIM, col >= HEAD_DIM]
    return after, from_, causal, heads


def _sb_weights(qh, kb, after, carry, causal):
    z = _dot(qh, kb, NT)
    e = jnp.exp(-jnp.abs(z))
    l1 = jnp.log(1.0 + e)
    log_one_minus = jnp.minimum(-z, 0.0) - l1
    log_beta = jnp.minimum(z, 0.0) - l1
    if causal is not None:
        log_one_minus = jnp.where(causal, log_one_minus, 0.0)
    hi, lo = _split(log_one_minus)
    inner = _dot(hi, after) + _dot(lo, after)
    a = jnp.exp(log_beta + inner + carry)
    if causal is not None:
        a = jnp.where(causal, a, 0.0)
    carry = carry + jnp.sum(log_one_minus, axis=1, keepdims=True)
    return z, e, a, carry


def sb_fwd(proj, kv, name):
    s = proj.shape[0]

    def body(q_ref, k_ref, v_ref, o_ref):
        i = pl.program_id(1)
        after, _, causal, heads = _sb_consts()
        q = (q_ref[...].astype(F32) * QK_SCALE).astype(BF16)
        outs = []
        for hm in heads:
            qh = _keep(hm, q)

            def block(j, state, masked):
                carry, acc = state
                rows = pl.ds(pl.multiple_of(j * SB_BLK, SB_BLK), SB_BLK)
                kb, vb = k_ref[rows, :], v_ref[rows, :]
                _, _, a, carry = _sb_weights(qh, kb, after, carry, causal if masked else None)
                hi, lo = _split(a)
                return carry, acc + _dot(hi, vb) + _dot(lo, vb)

            state = block(i, (jnp.zeros((SB_BLK, 1), F32), jnp.zeros((SB_BLK, LANES), F32)), True)
            state = lax.fori_loop(0, i, lambda t, st: block(i - 1 - t, st, False), state)
            outs.append(state[1])
        o_ref[...] = jnp.where(heads[0], outs[0], outs[1])

    pairs = SB_W // LANES
    return pl.pallas_call(
        body, name=name, grid=(pairs, s // SB_BLK),
        in_specs=[pl.BlockSpec((SB_BLK, LANES), lambda p, i: (i, p)), pl.BlockSpec((s, LANES), lambda p, i: (0, p)),
                  pl.BlockSpec((s, LANES), lambda p, i: (0, pairs + p))],
        out_specs=pl.BlockSpec((SB_BLK, LANES), lambda p, i: (i, p)),
        out_shape=jax.ShapeDtypeStruct((s, SB_W), F32),
        compiler_params=_params("arbitrary", "arbitrary"),
    )(proj, kv, kv)


def sb_bwd(proj, kv, out, dy, name):
    s = proj.shape[0]

    def body(q_ref, k_ref, v_ref, o_ref, do_ref, dq_ref, dk_ref, dv_ref):
        i = pl.program_id(1)

        @pl.when(i == 0)
        def _():
            dk_ref[...] = jnp.zeros_like(dk_ref)
            dv_ref[...] = jnp.zeros_like(dv_ref)

        after, from_, causal, heads = _sb_consts()
        q = (q_ref[...].astype(F32) * QK_SCALE).astype(BF16)
        d_out = do_ref[...]
        prod = d_out.astype(F32) * o_ref[...]
        dqs = []
        for hm in heads:
            qh = _keep(hm, q)
            doh = _keep(hm, d_out)
            total = jnp.sum(jnp.where(hm, prod, 0.0), axis=1, keepdims=True)

            def block(j, state, masked):
                carry, seen, dq = state
                rows = pl.ds(pl.multiple_of(j * SB_BLK, SB_BLK), SB_BLK)
                kb, vb = k_ref[rows, :], v_ref[rows, :]
                z, e, a, carry = _sb_weights(qh, kb, after, carry, causal if masked else None)
                dl = a * _dot(doh, vb, NT)
                dv_ref[rows, :] += _dot(a.astype(BF16), doh, TN)
                hi, lo = _split(dl)
                inner = _dot(hi, from_) + _dot(lo, from_)
                d_lom = total - (inner + seen)
                if masked:
                    d_lom = jnp.where(causal, d_lom, 0.0)
                rinv = 1.0 / (1.0 + e)
                small = e * rinv
                beta = jnp.where(z >= 0.0, rinv, small)
                one_minus = jnp.where(z >= 0.0, small, rinv)
                dz = (dl * one_minus - d_lom * beta).astype(BF16)
                dk_ref[rows, :] += _dot(dz, qh, TN)
                return carry, seen + jnp.sum(dl, axis=1, keepdims=True), dq + _dot(dz, kb)

            zero = jnp.zeros((SB_BLK, 1), F32)
            state = block(i, (zero, zero, jnp.zeros((SB_BLK, LANES), F32)), True)
            state = lax.fori_loop(0, i, lambda t, st: block(i - 1 - t, st, False), state)
            dqs.append(state[2])
        dq_ref[...] = (jnp.where(heads[0], dqs[0], dqs[1]) * QK_SCALE).astype(BF16)

    pairs = SB_W // LANES
    blk = lambda p, i: (i, p)
    col = lambda p, i: (0, p)
    return pl.pallas_call(
        body, name=name, grid=(pairs, s // SB_BLK),
        in_specs=[pl.BlockSpec((SB_BLK, LANES), blk), pl.BlockSpec((s, LANES), col),
                  pl.BlockSpec((s, LANES), lambda p, i: (0, pairs + p)), pl.BlockSpec((SB_BLK, LANES), blk),
                  pl.BlockSpec((SB_BLK, LANES), blk)],
        out_specs=[pl.BlockSpec((SB_BLK, LANES), blk), pl.BlockSpec((s, LANES), col), pl.BlockSpec((s, LANES), col)],
        out_shape=[jax.ShapeDtypeStruct((s, SB_W), BF16), jax.ShapeDtypeStruct((s, SB_W), F32),
                   jax.ShapeDtypeStruct((s, SB_W), F32)],
        compiler_params=_params("arbitrary", "arbitrary"),
    )(proj, kv, kv, out, dy)


def final_loss(x, g, target, name):
    s, d = x.shape
    tm = _tile(s, 256, 8)

    def body(x_ref, g_ref, t_ref, loss_ref, dx_ref, dg_ref):
        @pl.when(pl.program_id(0) == 0)
        def _():
            loss_ref[...] = jnp.zeros_like(loss_ref)
            dg_ref[...] = jnp.zeros_like(dg_ref)

        xf = x_ref[...]
        r = _rms(xf)
        xhat = xf * r
        gain = g_ref[...]
        diff = xhat * gain - t_ref[...]
        sq = jnp.sum(jnp.sum(diff * diff, axis=1, keepdims=True), axis=0, keepdims=True)
        loss_ref[...] += jnp.broadcast_to(sq, loss_ref.shape)
        dy = diff * (1.0 / d)
        dg_ref[...] += jnp.sum(dy * xhat, axis=0, keepdims=True)
        dxhat = dy * gain
        dx_ref[...] = r * (dxhat - xhat * jnp.mean(dxhat * xhat, axis=-1, keepdims=True))

    row = lambda i: (i, 0)
    const = lambda i: (0, 0)
    return pl.pallas_call(
        body, name=name, grid=(s // tm,),
        in_specs=[pl.BlockSpec((tm, d), row), pl.BlockSpec((1, d), const), pl.BlockSpec((tm, d), row)],
        out_specs=[pl.BlockSpec((8, LANES), const), pl.BlockSpec((tm, d), row), pl.BlockSpec((1, d), const)],
        out_shape=[jax.ShapeDtypeStruct((8, LANES), F32), jax.ShapeDtypeStruct((s, d), F32), jax.ShapeDtypeStruct((1, d), F32)],
        compiler_params=_params("arbitrary"),
    )(x, g, target)


def adamw(w, g, m, v, name):
    rows, cols = w.shape
    tr = _tile(rows, 512, 8)
    c1, c2 = 1.0 - ADAM_B1 ** ADAM_STEP, 1.0 - ADAM_B2 ** ADAM_STEP

    def body(w_ref, g_ref, m_ref, v_ref, d_ref, nm_ref, nv_ref):
        grad = g_ref[...]
        nm = ADAM_B1 * m_ref[...] + (1.0 - ADAM_B1) * grad
        nv = ADAM_B2 * v_ref[...] + (1.0 - ADAM_B2) * (grad * grad)
        d_ref[...] = -ADAM_LR * ((nm / c1) / (jnp.sqrt(nv / c2) + ADAM_EPS) + ADAM_WD * w_ref[...])
        nm_ref[...] = nm
        nv_ref[...] = nv

    spec = pl.BlockSpec((tr, cols), lambda i: (i, 0))
    shape = jax.ShapeDtypeStruct((rows, cols), F32)
    return pl.pallas_call(
        body, name=name, grid=(rows // tr,), in_specs=[spec] * 4, out_specs=[spec] * 3, out_shape=[shape] * 3,
        compiler_params=_params("arbitrary"),
    )(w, g, m, v)


SHARDED = {"ffn1_w_gate": 2, "ffn1_w_up": 2, "ffn1_w_down": 1, "ffn2_w_gate": 2, "ffn2_w_up": 2, "ffn2_w_down": 1,
           "w_mem_kv": 1, "a_w_in": 2, "a_w_out": 1, "w_kv": 1, "b_w_in": 1, "b_w_out": 1}
SMALL = ["ffn1_norm", "mix_norm", "ffn2_norm", "mem_norm", "kv_norm", "final_norm", "a_v_norm", "a_w_spatial", "a_b_spatial"]
WEIGHTS = ["ffn1_norm", "ffn1_w_gate", "ffn1_w_up", "ffn1_w_down", "mix_norm", "ffn2_norm", "ffn2_w_gate", "ffn2_w_up",
           "ffn2_w_down", "mem_norm", "w_mem_kv", "a_w_in", "a_v_norm", "a_w_spatial", "a_b_spatial", "a_w_out", "kv_norm",
           "w_kv", "b_w_in", "b_w_out", "final_norm"]


def _gather_weights(shards):
    flat = jnp.concatenate([shards[n].astype(BF16).reshape(-1) for n in SHARDED])
    mine = flat.reshape(-1, PACK_COLS)
    by_chip = exchange(mine, "quad", True, "gather_chips")
    by_core = exchange(by_chip.reshape(-1, PACK_COLS), "pair", True, "gather_cores")
    per_dev = by_core.reshape(2, 4, -1).transpose(1, 0, 2).reshape(N_DEV, -1)
    full, off = {}, 0
    for n, axis in SHARDED.items():
        size = shards[n].size
        blocks = per_dev[:, off:off + size].reshape((N_DEV,) + shards[n].shape)
        full[n] = jnp.concatenate([blocks[d] for d in range(N_DEV)], axis=axis)
        off += size
    return full


def _scatter_grads(grads, shards):
    per_dev = []
    for n, axis in SHARDED.items():
        parts = jnp.split(grads[n].astype(BF16), N_DEV, axis=axis)
        per_dev.append(jnp.stack([p.reshape(-1) for p in parts]))
    flat = jnp.concatenate(per_dev, axis=1)
    rows = flat.shape[1] // PACK_COLS
    by_core = flat.reshape(4, 2, rows, PACK_COLS).transpose(1, 0, 2, 3).reshape(2, 4 * rows, PACK_COLS)
    pair = exchange(by_core, "pair", False, "scatter_cores")
    chip_sum = sum_leading(pair, BF16, "scatter_cores_sum").reshape(4, rows, PACK_COLS)
    quad = exchange(chip_sum, "quad", False, "scatter_chips")
    mine = sum_leading(quad, F32, "scatter_chips_sum").reshape(-1)
    out, off = {}, 0
    for n in SHARDED:
        size = shards[n].size
        out[n] = mine[off:off + size].reshape(shards[n].shape)
        off += size
    return out


def _all_sum(parts, name):
    flat = jnp.concatenate([p.reshape(-1) for p in parts])
    pad = (-flat.size) % (16 * LANES)
    buf = jnp.pad(flat, (0, pad)).reshape(-1, LANES)
    total = sum_leading(exchange(buf, "all", True, name), F32, name + "_sum").reshape(-1)
    out, off = [], 0
    for p in parts:
        out.append(total[off:off + p.size].reshape(p.shape))
        off += p.size
    return out


def _device_index():
    return 4 * lax.axis_index("x") + 2 * lax.axis_index("y") + lax.axis_index("c")


def kernel(x, mem, ffn1_norm, ffn1_w_gate, ffn1_w_up, ffn1_w_down, mix_norm, ffn2_norm, ffn2_w_gate, ffn2_w_up, ffn2_w_down, mem_norm, w_mem_kv, a_w_in, a_v_norm, a_w_spatial, a_b_spatial, a_w_out, kv_norm, w_kv, b_w_in, b_w_out, final_norm, loss_target, m_ffn1_norm, m_ffn1_w_gate, m_ffn1_w_up, m_ffn1_w_down, m_mix_norm, m_ffn2_norm, m_ffn2_w_gate, m_ffn2_w_up, m_ffn2_w_down, m_mem_norm, m_w_mem_kv, m_a_w_in, m_a_v_norm, m_a_w_spatial, m_a_b_spatial, m_a_w_out, m_kv_norm, m_w_kv, m_b_w_in, m_b_w_out, m_final_norm, v_ffn1_norm, v_ffn1_w_gate, v_ffn1_w_up, v_ffn1_w_down, v_mix_norm, v_ffn2_norm, v_ffn2_w_gate, v_ffn2_w_up, v_ffn2_w_down, v_mem_norm, v_w_mem_kv, v_a_w_in, v_a_v_norm, v_a_w_spatial, v_a_b_spatial, v_a_w_out, v_kv_norm, v_w_kv, v_b_w_in, v_b_w_out, v_final_norm):
    weights = dict(ffn1_norm=ffn1_norm, ffn1_w_gate=ffn1_w_gate, ffn1_w_up=ffn1_w_up, ffn1_w_down=ffn1_w_down, mix_norm=mix_norm, ffn2_norm=ffn2_norm, ffn2_w_gate=ffn2_w_gate, ffn2_w_up=ffn2_w_up, ffn2_w_down=ffn2_w_down, mem_norm=mem_norm, w_mem_kv=w_mem_kv, a_w_in=a_w_in, a_v_norm=a_v_norm, a_w_spatial=a_w_spatial, a_b_spatial=a_b_spatial, a_w_out=a_w_out, kv_norm=kv_norm, w_kv=w_kv, b_w_in=b_w_in, b_w_out=b_w_out, final_norm=final_norm)
    mom1 = dict(ffn1_norm=m_ffn1_norm, ffn1_w_gate=m_ffn1_w_gate, ffn1_w_up=m_ffn1_w_up, ffn1_w_down=m_ffn1_w_down, mix_norm=m_mix_norm, ffn2_norm=m_ffn2_norm, ffn2_w_gate=m_ffn2_w_gate, ffn2_w_up=m_ffn2_w_up, ffn2_w_down=m_ffn2_w_down, mem_norm=m_mem_norm, w_mem_kv=m_w_mem_kv, a_w_in=m_a_w_in, a_v_norm=m_a_v_norm, a_w_spatial=m_a_w_spatial, a_b_spatial=m_a_b_spatial, a_w_out=m_a_w_out, kv_norm=m_kv_norm, w_kv=m_w_kv, b_w_in=m_b_w_in, b_w_out=m_b_w_out, final_norm=m_final_norm)
    mom2 = dict(ffn1_norm=v_ffn1_norm, ffn1_w_gate=v_ffn1_w_gate, ffn1_w_up=v_ffn1_w_up, ffn1_w_down=v_ffn1_w_down, mix_norm=v_mix_norm, ffn2_norm=v_ffn2_norm, ffn2_w_gate=v_ffn2_w_gate, ffn2_w_up=v_ffn2_w_up, ffn2_w_down=v_ffn2_w_down, mem_norm=v_mem_norm, w_mem_kv=v_w_mem_kv, a_w_in=v_a_w_in, a_v_norm=v_a_v_norm, a_w_spatial=v_a_w_spatial, a_b_spatial=v_a_b_spatial, a_w_out=v_a_w_out, kv_norm=v_kv_norm, w_kv=v_w_kv, b_w_in=v_b_w_in, b_w_out=v_b_w_out, final_norm=v_final_norm)

    dev = _device_index()
    xs, mem_in, target = x[0], mem[0], loss_target[0]
    d_model = xs.shape[1]
    shards = {n: weights[n] for n in SHARDED}
    full = _gather_weights(shards)
    vn_width = a_v_norm.shape[1]
    a_v_full = _all_sum([lax.dynamic_update_slice(jnp.zeros((N_A, N_DEV * vn_width), F32), a_v_norm, (0, dev * vn_width))],
                        "gather_v_norm")[0]

    row = lambda v: v.reshape(1, -1)
    w_gu = {f: jnp.concatenate([full[f + "_w_gate"], full[f + "_w_up"]], axis=2) for f in ("ffn1", "ffn2")}
    w_mem_cat = full["w_mem_kv"].transpose(1, 0, 2).reshape(d_model, -1)
    bias = [jnp.repeat(a_b_spatial[i].T, GM_P, axis=1) for i in range(N_A)]

    mem_kv, mem_h = norm_mm(mem_in, row(mem_norm), w_mem_cat, BF16, "mem_kv", emit_h=True)

    def ffn_fwd(xin, f, l):
        gu = norm_mm(xin, row(weights[f + "_norm"][l]), w_gu[f][l], BF16, "ffn_gu")
        act = swiglu_fwd(gu, "ffn_act")
        return mm_res(act, full[f + "_w_down"][l], xin, 0.5, "ffn_down"), gu, act

    saved = []
    kv = x_kv = None
    cur = xs
    for l in range(DEPTH):
        st = {"x0": cur}
        if l == N_A:
            x_kv = cur
            kv = norm_mm(cur, row(kv_norm), full["w_kv"], BF16, "kv_proj")
        st["x1"], st["gu1"], st["act1"] = ffn_fwd(cur, "ffn1", l)
        if l < N_A:
            proj = norm_mm(st["x1"], row(mix_norm[l]), full["a_w_in"][l], F32, "a_proj")
            y_tok = gmlp_fwd(proj, row(a_v_full[l]), a_w_spatial[l], bias[l], "gmlp_fwd")
            y_mem = mem_fwd(proj, 2 * GM_W // MEM_W, mem_kv, l, "mem_fwd_a")
            w_out = full["a_w_out"][l]
        else:
            proj = norm_mm(st["x1"], row(mix_norm[l]), full["b_w_in"][l - N_A], BF16, "b_proj")
            st["sb_out"] = sb_fwd(proj, kv, "sb_fwd")
            y_tok = st["sb_out"].astype(BF16)
            y_mem = mem_fwd(proj, SB_W // MEM_W, mem_kv, l, "mem_fwd_b")
            w_out = full["b_w_out"][l - N_A]
        st["proj"] = proj
        st["y"] = jnp.concatenate([y_tok, y_mem], axis=1)
        st["x2"] = mm_res(st["y"], w_out, st["x1"], 1.0, "mix_out")
        cur, st["gu2"], st["act2"] = ffn_fwd(st["x2"], "ffn2", l)
        saved.append(st)

    loss_blk, dx, d_final = final_loss(cur, row(final_norm), target, "final_loss")
    loss = lax.psum(loss_blk[0, 0] * (0.5 / d_model), AXES)

    grads = {n: [None] * weights[n].shape[0] for n in WEIGHTS if weights[n].ndim >= 2 and n not in ("w_kv",)}
    grads["final_norm"] = d_final.reshape(-1)
    d_mem_kv = [None] * DEPTH
    d_kv = []

    def ffn_bwd(dx, xin, gu, act, f, l):
        d_act = mm_nt(dx, full[f + "_w_down"][l], 0.5, "ffn_dact")
        d_gu = swiglu_bwd(gu, d_act, "ffn_dgu")
        dx_new, d_gain, h = mm_nt_normbwd(d_gu, w_gu[f][l], xin, row(weights[f + "_norm"][l]), dx, "ffn_dx")
        d_wgu = mm_tn(h, d_gu, 1.0, "ffn_dwgu")
        half = d_wgu.shape[1] // 2
        grads[f + "_w_gate"][l], grads[f + "_w_up"][l] = d_wgu[:, :half], d_wgu[:, half:]
        grads[f + "_w_down"][l] = mm_tn(act, dx, 0.5, "ffn_dwdown", ta_target=1408, tb_target=1024)
        grads[f + "_norm"][l] = d_gain.reshape(-1)
        return dx_new

    for l in reversed(range(DEPTH)):
        st = saved[l]
        dx = ffn_bwd(dx, st["x2"], st["gu2"], st["act2"], "ffn2", l)
        proj = st["proj"]
        if l < N_A:
            w_in, w_out, key_in, key_out, idx = full["a_w_in"][l], full["a_w_out"][l], "a_w_in", "a_w_out", l
        else:
            w_in, w_out, key_in, key_out, idx = full["b_w_in"][l - N_A], full["b_w_out"][l - N_A], "b_w_in", "b_w_out", l - N_A
        dy = mm_nt(dx, w_out, 1.0, "mix_dy")
        grads[key_out][idx] = mm_tn(st["y"], dx, 1.0, "mix_dwout", tb_target=1024)
        if l < N_A:
            d_uv, d_ws, d_bs, d_vgain = gmlp_bwd(proj, dy, row(a_v_full[l]), a_w_spatial[l], bias[l], "gmlp_bwd")
            grads["a_w_spatial"][l], grads["a_b_spatial"][l], grads["a_v_norm"][l] = d_ws, d_bs[:, :, 0], d_vgain.reshape(-1)
            d_q, d_k, d_v = mem_bwd(proj, 2 * GM_W // MEM_W, mem_kv, l, dy, GM_W // MEM_W, "mem_bwd_a")
            d_proj = jnp.concatenate([d_uv, d_q], axis=1)
        else:
            d_qsb, d_ksb, d_vsb = sb_bwd(proj, kv, st["sb_out"], dy, "sb_bwd")
            d_kv.append(jnp.concatenate([d_ksb, d_vsb], axis=1))
            d_q, d_k, d_v = mem_bwd(proj, SB_W // MEM_W, mem_kv, l, dy, SB_W // MEM_W, "mem_bwd_b")
            d_proj = jnp.concatenate([d_qsb, d_q], axis=1)
        d_mem_kv[l] = jnp.concatenate([d_k, d_v], axis=1)
        dx, d_gain, h = mm_nt_normbwd(d_proj, w_in, st["x1"], row(mix_norm[l]), dx, "mix_dx")
        grads["mix_norm"][l] = d_gain.reshape(-1)
        grads[key_in][idx] = mm_tn(h, d_proj, 1.0, "mix_dwin")
        dx = ffn_bwd(dx, st["x0"], st["gu1"], st["act1"], "ffn1", l)
        if l == N_A:
            d_kv_b = sum_leading(jnp.stack(d_kv), BF16, "kv_dsum")
            dx, d_gain, h = mm_nt_normbwd(d_kv_b, full["w_kv"], x_kv, row(kv_norm), dx, "kv_dx")
            grads["kv_norm"] = d_gain.reshape(-1)
            grads["w_kv"] = mm_tn(h, d_kv_b, 1.0, "kv_dw")

    d_mem_all = jnp.concatenate(d_mem_kv, axis=1).astype(BF16)
    _, d_gain, _ = mm_nt_normbwd(d_mem_all, w_mem_cat, mem_in, row(mem_norm), None, "mem_dnorm")
    grads["mem_norm"] = d_gain.reshape(-1)
    d_wmem = mm_tn(mem_h, d_mem_all, 1.0, "mem_dw")
    grads["w_mem_kv"] = d_wmem.reshape(d_model, DEPTH, -1).transpose(1, 0, 2)
    grads = {n: (jnp.stack(g) if isinstance(g, list) else g) for n, g in grads.items()}

    reduced = _scatter_grads({n: grads[n] for n in SHARDED}, shards)
    for n, g in zip(SMALL, _all_sum([grads[n] for n in SMALL], "sum_small")):
        reduced[n] = g
    reduced["a_v_norm"] = lax.dynamic_slice(reduced["a_v_norm"], (0, dev * vn_width), a_v_norm.shape)

    deltas, new_m, new_v = {}, {}, {}
    for n in WEIGHTS:
        w = weights[n]
        view = (lambda a: a.reshape(-1, a.shape[-1]))
        d, nm, nv = adamw(view(w), view(reduced[n]), view(mom1[n]), view(mom2[n]), "adamw")
        deltas[n], new_m[n], new_v[n] = d.reshape(w.shape), nm.reshape(w.shape), nv.reshape(w.shape)

    return (loss, dx[None], *[reduced[n] for n in WEIGHTS], *[deltas[n] for n in WEIGHTS],
            *[new_m[n] for n in WEIGHTS], *[new_v[n] for n in WEIGHTS])
```

```python
import functools

import jax
import jax.numpy as jnp
from jax import lax
from jax.experimental import pallas as pl
from jax.experimental.pallas import tpu as pltpu

F32, BF16 = jnp.float32, jnp.bfloat16
MESH_ID = pl.DeviceIdType.MESH
AXES = ("x", "y", "c")
N_DEV = 8

EPS = 1e-6
DEPTH, N_A = 4, 2
GM_W, GM_GROUPS, GM_P = 768, 6, 128
MEM_W, MEM_HEADS, HEAD_DIM = 256, 4, 64
SB_W, SB_BLK = 768, 128
LANES = 128
QK_SCALE = HEAD_DIM ** -0.5
GELU_C, GELU_A = 0.7978845608028654, 0.044715

ADAM_LR, ADAM_B1, ADAM_B2, ADAM_EPS, ADAM_WD, ADAM_STEP = 0.001, 0.9, 0.999, 1e-08, 0.01, 10

VMEM_LIMIT = 56 * 1024 * 1024
PACK_COLS = 512

NT = (((1,), (1,)), ((), ()))
TN = (((0,), (0,)), ((), ()))


def _params(*sem):
    return pltpu.CompilerParams(dimension_semantics=sem, vmem_limit_bytes=VMEM_LIMIT)


def _tile(n, target, mult=LANES):
    best = None
    for t in range(mult, min(n, target) + 1, mult):
        if n % t == 0:
            best = t
    return best if best is not None else n


def _dot(a, b, dims=None):
    if dims is None:
        return jnp.dot(a, b, preferred_element_type=F32)
    return lax.dot_general(a, b, dims, preferred_element_type=F32)


def exchange(srcs, group, same_src, name, split=False):
    size = {"pair": 2, "quad": 4, "all": 8}[group]
    n = len(srcs)
    chunk_shapes = [tuple(s.shape) if same_src else tuple(s.shape[1:]) for s in srcs]
    pieces = [cs[0] if split else 1 for cs in chunk_shapes]
    n_dma = sum(pieces)

    def body(*refs):
        src_refs, out_refs = refs[:n], refs[n:2 * n]
        send_sems, recv_sems, local_sems = refs[2 * n:]
        x, y, c = lax.axis_index("x"), lax.axis_index("y"), lax.axis_index("c")
        if group == "pair":
            me, dev = c, lambda p: (x, y, p)
        elif group == "quad":
            me, dev = 2 * x + y, lambda p: (p // 2, p % 2, c)
        else:
            me, dev = 4 * x + 2 * y + c, lambda p: (p // 4, (p // 2) % 2, p % 2)

        def chunk(t, idx):
            return src_refs[t] if same_src else src_refs[t].at[idx]

        def copies(k, idx, slot, peer):
            out, w = [], k * n_dma
            for t in range(n):
                src, dst = chunk(t, idx), out_refs[t].at[slot]
                for s_ref, d_ref in ([(src.at[u], dst.at[u]) for u in range(pieces[t])] if split else [(src, dst)]):
                    out.append(pltpu.make_async_remote_copy(
                        src_ref=s_ref, dst_ref=d_ref, send_sem=send_sems.at[w], recv_sem=recv_sems.at[w],
                        device_id=dev(peer), device_id_type=MESH_ID))
                    w += 1
            return out

        local = [pltpu.make_async_copy(chunk(t, me), out_refs[t].at[me], local_sems.at[t]) for t in range(n)]
        for cp in local:
            cp.start()
        sends = []
        for k in range(1, size):
            peer = (me + k) % size
            sends += copies(k, peer, me, peer)
        for cp in sends:
            cp.start()
        for k in range(1, size):
            sender = (me + size - k) % size
            for cp in copies(k, me, sender, sender):
                cp.wait_recv()
        for cp in sends:
            cp.wait_send()
        for cp in local:
            cp.wait()

    hbm = pl.BlockSpec(memory_space=pltpu.HBM)
    return pl.pallas_call(
        body, name=name,
        out_shape=[jax.ShapeDtypeStruct((size,) + cs, s.dtype) for cs, s in zip(chunk_shapes, srcs)],
        in_specs=[hbm] * n, out_specs=[hbm] * n,
        scratch_shapes=[pltpu.SemaphoreType.DMA((size * n_dma,)), pltpu.SemaphoreType.DMA((size * n_dma,)),
                        pltpu.SemaphoreType.DMA((n,))],
    )(*srcs)


def sum_leading(parts, out_dtype, name):
    k, rows, cols = parts.shape
    tr = _tile(rows, 512, 16)

    def body(p_ref, o_ref):
        acc = p_ref[0].astype(F32)
        for s in range(1, k):
            acc = acc + p_ref[s].astype(F32)
        o_ref[...] = acc.astype(o_ref.dtype)

    return pl.pallas_call(
        body, name=name, grid=(rows // tr,),
        in_specs=[pl.BlockSpec((k, tr, cols), lambda i: (0, i, 0))],
        out_specs=pl.BlockSpec((tr, cols), lambda i: (i, 0)),
        out_shape=jax.ShapeDtypeStruct((rows, cols), out_dtype),
        compiler_params=_params("arbitrary"),
    )(parts)


def _rms(xf):
    return lax.rsqrt(jnp.mean(xf * xf, axis=-1, keepdims=True) + EPS)


def norm_mm(x, g, w, out_dtype, name, emit_h=False):
    m, d = x.shape
    n = w.shape[1]
    tm, tn = _tile(m, 512, 8), _tile(n, 512)

    def body(x_ref, g_ref, w_ref, o_ref, *rest):
        h_ref = rest[-1]

        @pl.when(pl.program_id(1) == 0)
        def _():
            xf = x_ref[...]
            hb = ((xf * _rms(xf)) * g_ref[...]).astype(BF16)
            h_ref[...] = hb
            if emit_h:
                rest[0][...] = hb

        o_ref[...] = _dot(h_ref[...], w_ref[...]).astype(o_ref.dtype)

    out_shape = [jax.ShapeDtypeStruct((m, n), out_dtype)]
    out_specs = [pl.BlockSpec((tm, tn), lambda i, j: (i, j))]
    if emit_h:
        out_shape.append(jax.ShapeDtypeStruct((m, d), BF16))
        out_specs.append(pl.BlockSpec((tm, d), lambda i, j: (i, 0)))
    res = pl.pallas_call(
        body, name=name, grid=(m // tm, n // tn),
        in_specs=[pl.BlockSpec((tm, d), lambda i, j: (i, 0)), pl.BlockSpec((1, d), lambda i, j: (0, 0)),
                  pl.BlockSpec((d, tn), lambda i, j: (0, j))],
        out_specs=out_specs, out_shape=out_shape,
        scratch_shapes=[pltpu.VMEM((tm, d), BF16)],
        compiler_params=_params("arbitrary", "arbitrary"),
    )(x, g, w)
    return res if emit_h else res[0]


def mm_res(a, w, res, alpha, name):
    m, k = a.shape
    n = w.shape[1]
    tm, tn = _tile(m, 512, 8), _tile(n, 512)

    def body(a_ref, w_ref, r_ref, o_ref):
        o_ref[...] = r_ref[...] + alpha * _dot(a_ref[...], w_ref[...])

    return pl.pallas_call(
        body, name=name, grid=(m // tm, n // tn),
        in_specs=[pl.BlockSpec((tm, k), lambda i, j: (i, 0)), pl.BlockSpec((k, tn), lambda i, j: (0, j)),
                  pl.BlockSpec((tm, tn), lambda i, j: (i, j))],
        out_specs=pl.BlockSpec((tm, tn), lambda i, j: (i, j)),
        out_shape=jax.ShapeDtypeStruct((m, n), F32),
        compiler_params=_params("arbitrary", "arbitrary"),
    )(a, w, res)


def mm_nt(x, w, alpha, name):
    m, d = x.shape
    n = w.shape[0]
    tm, tn = _tile(m, 512, 8), _tile(n, 512)

    def body(x_ref, w_ref, o_ref, xb_ref):
        @pl.when(pl.program_id(1) == 0)
        def _():
            xb_ref[...] = x_ref[...].astype(BF16)

        o_ref[...] = (alpha * _dot(xb_ref[...], w_ref[...], NT)).astype(o_ref.dtype)

    return pl.pallas_call(
        body, name=name, grid=(m // tm, n // tn),
        in_specs=[pl.BlockSpec((tm, d), lambda i, j: (i, 0)), pl.BlockSpec((tn, d), lambda i, j: (j, 0))],
        out_specs=pl.BlockSpec((tm, tn), lambda i, j: (i, j)),
        out_shape=jax.ShapeDtypeStruct((m, n), BF16),
        scratch_shapes=[pltpu.VMEM((tm, d), BF16)],
        compiler_params=_params("arbitrary", "arbitrary"),
    )(x, w)


def mm_tn(a, b, alpha, name, ta_target=1024, tb_target=512):
    s, ka = a.shape
    nb = b.shape[1]
    ta, tb, ts = _tile(ka, ta_target), _tile(nb, tb_target), _tile(s, 512, 16)
    steps = s // ts

    def body(a_ref, b_ref, o_ref, acc_ref):
        t = pl.program_id(2)

        @pl.when(t == 0)
        def _():
            acc_ref[...] = jnp.zeros_like(acc_ref)

        acc_ref[...] += _dot(a_ref[...].astype(BF16), b_ref[...].astype(BF16), TN)

        @pl.when(t == steps - 1)
        def _():
            o_ref[...] = alpha * acc_ref[...]

    return pl.pallas_call(
        body, name=name, grid=(ka // ta, nb // tb, steps),
        in_specs=[pl.BlockSpec((ts, ta), lambda i, j, t: (t, i)), pl.BlockSpec((ts, tb), lambda i, j, t: (t, j))],
        out_specs=pl.BlockSpec((ta, tb), lambda i, j, t: (i, j)),
        out_shape=jax.ShapeDtypeStruct((ka, nb), F32),
        scratch_shapes=[pltpu.VMEM((ta, tb), F32)],
        compiler_params=_params("arbitrary", "arbitrary", "arbitrary"),
    )(a, b)


def mm_nt_normbwd(dy, w, x, g, res, name):
    m, n = dy.shape
    d = w.shape[0]
    tm, tk = _tile(m, 512, 8), _tile(n, 512)
    steps = n // tk
    has_res = res is not None

    def body(*refs):
        if has_res:
            dy_ref, w_ref, x_ref, g_ref, r_ref, dx_ref, dg_ref, h_ref, acc_ref = refs
        else:
            dy_ref, w_ref, x_ref, g_ref, dx_ref, dg_ref, h_ref, acc_ref = refs
        i, t = pl.program_id(0), pl.program_id(1)

        @pl.when(t == 0)
        def _():
            acc_ref[...] = jnp.zeros_like(acc_ref)

        @pl.when((t == 0) & (i == 0))
        def _():
            dg_ref[...] = jnp.zeros_like(dg_ref)

        acc_ref[...] += _dot(dy_ref[...], w_ref[...], NT)

        @pl.when(t == steps - 1)
        def _():
            xf = x_ref[...]
            r = _rms(xf)
            xhat = xf * r
            dh = acc_ref[...]
            gain = g_ref[...]
            dg_ref[...] += jnp.sum(dh * xhat, axis=0, keepdims=True)
            dxhat = dh * gain
            dx = r * (dxhat - xhat * jnp.mean(dxhat * xhat, axis=-1, keepdims=True))
            dx_ref[...] = (r_ref[...] + dx) if has_res else dx
            h_ref[...] = (xhat * gain).astype(BF16)

    row = lambda i, t: (i, 0)
    in_specs = [pl.BlockSpec((tm, tk), lambda i, t: (i, t)), pl.BlockSpec((d, tk), lambda i, t: (0, t)),
                pl.BlockSpec((tm, d), row), pl.BlockSpec((1, d), lambda i, t: (0, 0))]
    args = [dy, w, x, g]
    if has_res:
        in_specs.append(pl.BlockSpec((tm, d), row))
        args.append(res)
    return pl.pallas_call(
        body, name=name, grid=(m // tm, steps),
        in_specs=in_specs,
        out_specs=[pl.BlockSpec((tm, d), row), pl.BlockSpec((1, d), lambda i, t: (0, 0)), pl.BlockSpec((tm, d), row)],
        out_shape=[jax.ShapeDtypeStruct((m, d), F32), jax.ShapeDtypeStruct((1, d), F32), jax.ShapeDtypeStruct((m, d), BF16)],
        scratch_shapes=[pltpu.VMEM((tm, d), F32)],
        compiler_params=_params("arbitrary", "arbitrary"),
    )(*args)


def _sigmoid(z):
    return 1.0 / (1.0 + jnp.exp(-z))


def swiglu_fwd(gu, name):
    m, f2 = gu.shape
    f = f2 // 2
    tm = _tile(m, 256, 16)

    def body(gu_ref, o_ref):
        gate, up = gu_ref[:, :f].astype(F32), gu_ref[:, f:].astype(F32)
        o_ref[...] = (gate * _sigmoid(gate) * up).astype(BF16)

    return pl.pallas_call(
        body, name=name, grid=(m // tm,),
        in_specs=[pl.BlockSpec((tm, f2), lambda i: (i, 0))], out_specs=pl.BlockSpec((tm, f), lambda i: (i, 0)),
        out_shape=jax.ShapeDtypeStruct((m, f), BF16), compiler_params=_params("arbitrary"),
    )(gu)


def swiglu_bwd(gu, da, name):
    m, f2 = gu.shape
    f = f2 // 2
    tm = _tile(m, 256, 16)

    def body(gu_ref, da_ref, o_ref):
        gate, up = gu_ref[:, :f].astype(F32), gu_ref[:, f:].astype(F32)
        d = da_ref[...].astype(F32)
        sg = _sigmoid(gate)
        o_ref[:, :f] = (d * up * (sg * (1.0 + gate * (1.0 - sg)))).astype(BF16)
        o_ref[:, f:] = (d * (gate * sg)).astype(BF16)

    return pl.pallas_call(
        body, name=name, grid=(m // tm,),
        in_specs=[pl.BlockSpec((tm, f2), lambda i: (i, 0)), pl.BlockSpec((tm, f), lambda i: (i, 0))],
        out_specs=pl.BlockSpec((tm, f2), lambda i: (i, 0)),
        out_shape=jax.ShapeDtypeStruct((m, f2), BF16), compiler_params=_params("arbitrary"),
    )(gu, da)


def _gelu(x):
    return 0.5 * x * (1.0 + jnp.tanh(GELU_C * (x + GELU_A * x * x * x)))


def _gelu_grad(x):
    t = jnp.tanh(GELU_C * (x + GELU_A * x * x * x))
    return 0.5 * (1.0 + t) + 0.5 * x * (1.0 - t * t) * (GELU_C * (1.0 + 3.0 * GELU_A * x * x))


def _chunk_mask():
    row = lax.broadcasted_iota(jnp.int32, (GM_P, GM_P), 0)
    col = lax.broadcasted_iota(jnp.int32, (GM_P, GM_P), 1)
    return (col < GM_P // 2) | (row >= GM_P // 2)


def gmlp_fwd(proj, gain, w_s, bias, name):
    s, pw = proj.shape
    tm = _tile(s, 256, GM_P)

    def body(p_ref, gain_ref, w_ref, b_ref, o_ref):
        mask = _chunk_mask()
        u = _gelu(p_ref[:, :GM_W])
        v = _gelu(p_ref[:, GM_W:2 * GM_W])
        vn = ((v * _rms(v)) * gain_ref[...]).astype(BF16)
        for g in range(GM_GROUPS):
            wg = jnp.where(mask, w_ref[g], 0.0).astype(BF16)
            cols = slice(g * GM_P, (g + 1) * GM_P)
            for n in range(tm // GM_P):
                rows = slice(n * GM_P, (n + 1) * GM_P)
                mixed = _dot(wg, vn[rows, cols]) + b_ref[:, cols]
                o_ref[rows, cols] = (u[rows, cols] * mixed).astype(BF16)

    return pl.pallas_call(
        body, name=name, grid=(s // tm,),
        in_specs=[pl.BlockSpec((tm, pw), lambda i: (i, 0)), pl.BlockSpec((1, GM_W), lambda i: (0, 0)),
                  pl.BlockSpec((GM_GROUPS, GM_P, GM_P), lambda i: (0, 0, 0)), pl.BlockSpec((GM_P, GM_W), lambda i: (0, 0))],
        out_specs=pl.BlockSpec((tm, GM_W), lambda i: (i, 0)),
        out_shape=jax.ShapeDtypeStruct((s, GM_W), BF16), compiler_params=_params("arbitrary"),
    )(proj, gain, w_s, bias)


def gmlp_bwd(proj, dy, gain, w_s, bias, name):
    s, pw = proj.shape
    dw_total = dy.shape[1]
    tm = _tile(s, 256, GM_P)

    def body(p_ref, dy_ref, gain_ref, w_ref, b_ref, dp_ref, dw_ref, db_ref, dgain_ref, dvn_ref):
        @pl.when(pl.program_id(0) == 0)
        def _():
            dw_ref[...] = jnp.zeros_like(dw_ref)
            db_ref[...] = jnp.zeros_like(db_ref)
            dgain_ref[...] = jnp.zeros_like(dgain_ref)

        mask = _chunk_mask()
        pu = p_ref[:, :GM_W]
        pv = p_ref[:, GM_W:2 * GM_W]
        u = _gelu(pu)
        v = _gelu(pv)
        r = _rms(v)
        vhat = v * r
        gain = gain_ref[...]
        vn = (vhat * gain).astype(BF16)
        gu_grad = _gelu_grad(pu)
        for g in range(GM_GROUPS):
            wg = jnp.where(mask, w_ref[g], 0.0).astype(BF16)
            cols = slice(g * GM_P, (g + 1) * GM_P)
            dw_acc = jnp.zeros((GM_P, GM_P), F32)
            db_acc = jnp.zeros((GM_P, 1), F32)
            for n in range(tm // GM_P):
                rows = slice(n * GM_P, (n + 1) * GM_P)
                dyb = dy_ref[rows, cols].astype(F32)
                vnb = vn[rows, cols]
                mixed = _dot(wg, vnb) + b_ref[:, cols]
                dmixed = dyb * u[rows, cols]
                dmb = dmixed.astype(BF16)
                dp_ref[rows, cols] = (dyb * mixed * gu_grad[rows, cols]).astype(BF16)
                dw_acc = dw_acc + _dot(dmb, vnb, NT)
                db_acc = db_acc + jnp.sum(dmixed, axis=1, keepdims=True)
                dvn_ref[rows, cols] = _dot(wg, dmb, TN)
            dw_ref[g] += jnp.where(mask, dw_acc, 0.0)
            db_ref[g] += jnp.broadcast_to(db_acc, (GM_P, GM_P))
        dvn = dvn_ref[...]
        dgain_ref[...] += jnp.sum(dvn * vhat, axis=0, keepdims=True)
        dvhat = dvn * gain
        dv = r * (dvhat - vhat * jnp.mean(dvhat * vhat, axis=-1, keepdims=True))
        dp_ref[:, GM_W:] = (dv * _gelu_grad(pv)).astype(BF16)

    const3 = lambda i: (0, 0, 0)
    return pl.pallas_call(
        body, name=name, grid=(s // tm,),
        in_specs=[pl.BlockSpec((tm, pw), lambda i: (i, 0)), pl.BlockSpec((tm, dw_total), lambda i: (i, 0)),
                  pl.BlockSpec((1, GM_W), lambda i: (0, 0)), pl.BlockSpec((GM_GROUPS, GM_P, GM_P), const3),
                  pl.BlockSpec((GM_P, GM_W), lambda i: (0, 0))],
        out_specs=[pl.BlockSpec((tm, 2 * GM_W), lambda i: (i, 0)), pl.BlockSpec((GM_GROUPS, GM_P, GM_P), const3),
                   pl.BlockSpec((GM_GROUPS, GM_P, GM_P), const3), pl.BlockSpec((1, GM_W), lambda i: (0, 0))],
        out_shape=[jax.ShapeDtypeStruct((s, 2 * GM_W), BF16), jax.ShapeDtypeStruct((GM_GROUPS, GM_P, GM_P), F32),
                   jax.ShapeDtypeStruct((GM_GROUPS, GM_P, GM_P), F32), jax.ShapeDtypeStruct((1, GM_W), F32)],
        scratch_shapes=[pltpu.VMEM((tm, GM_W), F32)],
        compiler_params=_params("arbitrary"),
    )(proj, dy, gain, w_s, bias)


def _keep(mask, xb):
    return jnp.where(mask, xb.astype(F32), 0.0).astype(BF16)


def _head_masks(rows, width, heads):
    lane = lax.broadcasted_iota(jnp.int32, (rows, width), 1)
    return [(lane >= HEAD_DIM * h) & (lane < HEAD_DIM * (h + 1)) for h in range(heads)]


def _mem_probs(qh, k):
    sc = _dot(qh, k, NT) * QK_SCALE
    e = jnp.exp(sc - jnp.max(sc, axis=-1, keepdims=True))
    return e / jnp.sum(e, axis=-1, keepdims=True)


def mem_fwd(proj, q_blk, mem_kv, layer, name):
    s = proj.shape[0]
    n_mem = mem_kv.shape[0]
    tm = _tile(s, 512, 16)

    def body(q_ref, k_ref, v_ref, o_ref):
        q = q_ref[...].astype(BF16)
        k, v = k_ref[...], v_ref[...]
        out = jnp.zeros((tm, MEM_W), F32)
        for hm in _head_masks(tm, MEM_W, MEM_HEADS):
            p = _mem_probs(_keep(hm, q), k)
            out = out + jnp.where(hm, _dot(p.astype(BF16), v), 0.0)
        o_ref[...] = out.astype(BF16)

    return pl.pallas_call(
        body, name=name, grid=(s // tm,),
        in_specs=[pl.BlockSpec((tm, MEM_W), lambda i: (i, q_blk)), pl.BlockSpec((n_mem, MEM_W), lambda i: (0, 2 * layer)),
                  pl.BlockSpec((n_mem, MEM_W), lambda i: (0, 2 * layer + 1))],
        out_specs=pl.BlockSpec((tm, MEM_W), lambda i: (i, 0)),
        out_shape=jax.ShapeDtypeStruct((s, MEM_W), BF16), compiler_params=_params("arbitrary"),
    )(proj, mem_kv, mem_kv)


def mem_bwd(proj, q_blk, mem_kv, layer, dy, dy_blk, name):
    s = proj.shape[0]
    n_mem = mem_kv.shape[0]
    tm = _tile(s, 512, 16)

    def body(q_ref, k_ref, v_ref, dy_ref, dq_ref, dk_ref, dv_ref):
        @pl.when(pl.program_id(0) == 0)
        def _():
            dk_ref[...] = jnp.zeros_like(dk_ref)
            dv_ref[...] = jnp.zeros_like(dv_ref)

        q = q_ref[...].astype(BF16)
        k, v = k_ref[...], v_ref[...]
        dy = dy_ref[...]
        dq = jnp.zeros((tm, MEM_W), F32)
        dk = jnp.zeros((n_mem, MEM_W), F32)
        dv = jnp.zeros((n_mem, MEM_W), F32)
        for hm in _head_masks(tm, MEM_W, MEM_HEADS):
            qh = _keep(hm, q)
            dyh = _keep(hm, dy)
            p = _mem_probs(qh, k)
            dp = _dot(dyh, v, NT)
            dv = dv + _dot(p.astype(BF16), dyh, TN)
            ds = (p * (dp - jnp.sum(dp * p, axis=-1, keepdims=True)) * QK_SCALE).astype(BF16)
            dq = dq + jnp.where(hm, _dot(ds, k), 0.0)
            dk = dk + _dot(ds, qh, TN)
        dq_ref[...] = dq.astype(BF16)
        dk_ref[...] += dk
        dv_ref[...] += dv

    const = lambda i: (0, 0)
    return pl.pallas_call(
        body, name=name, grid=(s // tm,),
        in_specs=[pl.BlockSpec((tm, MEM_W), lambda i: (i, q_blk)), pl.BlockSpec((n_mem, MEM_W), lambda i: (0, 2 * layer)),
                  pl.BlockSpec((n_mem, MEM_W), lambda i: (0, 2 * layer + 1)), pl.BlockSpec((tm, MEM_W), lambda i: (i, dy_blk))],
        out_specs=[pl.BlockSpec((tm, MEM_W), lambda i: (i, 0)), pl.BlockSpec((n_mem, MEM_W), const),
                   pl.BlockSpec((n_mem, MEM_W), const)],
        out_shape=[jax.ShapeDtypeStruct((s, MEM_W), BF16), jax.ShapeDtypeStruct((n_mem, MEM_W), F32),
                   jax.ShapeDtypeStruct((n_mem, MEM_W), F32)],
        compiler_params=_params("arbitrary"),
    )(proj, mem_kv, mem_kv, dy)


SB_KEYS = 512
SB_SUB = SB_KEYS // SB_BLK
SB_SHIFT = SB_SUB.bit_length() - 1


def _split(xf):
    hi = xf.astype(BF16)
    return hi, (xf - hi.astype(F32)).astype(BF16)


def _sb_consts():
    row = lax.bitwise_and(lax.broadcasted_iota(jnp.int32, (2 * SB_BLK, 2 * SB_BLK), 0), SB_BLK - 1)
    col = lax.broadcasted_iota(jnp.int32, (2 * SB_BLK, 2 * SB_BLK), 1)
    ones = col >= SB_BLK
    after2 = jnp.where(ones | (row > col), 1.0, 0.0).astype(BF16)
    from2 = jnp.where(ones | (row >= col), 1.0, 0.0).astype(BF16)
    r = lax.broadcasted_iota(jnp.int32, (SB_BLK, SB_BLK), 0)
    c = lax.broadcasted_iota(jnp.int32, (SB_BLK, SB_BLK), 1)
    return after2, from2, c - r, [c < HEAD_DIM, c >= HEAD_DIM]


def _suffix(xf, tri2):
    hi, lo = _split(xf)
    r = _dot(jnp.concatenate([hi, lo], axis=1), tri2)
    return r[:, :SB_BLK], r[:, SB_BLK:]


def _sb_weights(zc, after2, run, mask):
    e = jnp.exp(-jnp.abs(zc))
    l1 = jnp.log(1.0 + e)
    log_one_minus = jnp.minimum(-zc, 0.0) - l1
    log_beta = jnp.minimum(zc, 0.0) - l1
    if mask is not None:
        log_one_minus = jnp.where(mask, log_one_minus, 0.0)
    inner, total = _suffix(log_one_minus, after2)
    a = jnp.exp(log_beta + inner + run)
    if mask is not None:
        a = jnp.where(mask, a, 0.0)
    return e, a, run + total


def _sb_walk(i, block, state):
    own = lax.shift_right_logical(i, SB_SHIFT)
    state = block(own, state, True)
    return lax.fori_loop(0, own, lambda t, st: block(own - 1 - t, st, False), state)


def sb_fwd(proj, kv, name):
    s = proj.shape[0]
    assert s % SB_KEYS == 0

    def body(q_ref, k_ref, v_ref, o_ref):
        i = pl.program_id(1)
        after2, _, col_minus_row, heads = _sb_consts()
        q = q_ref[...].astype(F32) * QK_SCALE
        qhs = [jnp.where(hm, q, 0.0).astype(BF16) for hm in heads]
        first_row = lax.bitwise_and(i, SB_SUB - 1) * SB_BLK

        def block(j, state, masked):
            rows = pl.ds(pl.multiple_of(j * SB_KEYS, SB_KEYS), SB_KEYS)
            kb, vb = k_ref[rows, :], v_ref[rows, :]
            new = []
            for h in range(2):
                run, acc = state[h]
                z = _dot(qhs[h], kb, NT)
                parts = [None] * SB_SUB
                for c in reversed(range(SB_SUB)):
                    mask = (col_minus_row < first_row - c * SB_BLK) if masked else None
                    _, parts[c], run = _sb_weights(z[:, c * SB_BLK:(c + 1) * SB_BLK], after2, run, mask)
                hi, lo = _split(jnp.concatenate(parts, axis=1))
                new.append((run, acc + _dot(hi, vb) + _dot(lo, vb)))
            return tuple(new)

        zero = jnp.zeros((SB_BLK, LANES), F32)
        state = _sb_walk(i, block, ((zero, zero), (zero, zero)))
        o_ref[...] = jnp.where(heads[0], state[0][1], state[1][1])

    pairs = SB_W // LANES
    return pl.pallas_call(
        body, name=name, grid=(pairs, s // SB_BLK),
        in_specs=[pl.BlockSpec((SB_BLK, LANES), lambda p, i: (i, p)), pl.BlockSpec((s, LANES), lambda p, i: (0, p)),
                  pl.BlockSpec((s, LANES), lambda p, i: (0, pairs + p))],
        out_specs=pl.BlockSpec((SB_BLK, LANES), lambda p, i: (i, p)),
        out_shape=jax.ShapeDtypeStruct((s, SB_W), F32),
        compiler_params=_params("arbitrary", "arbitrary"),
    )(proj, kv, kv)


def sb_bwd(proj, kv, out, dy, name):
    s = proj.shape[0]

    def body(q_ref, k_ref, v_ref, o_ref, do_ref, dq_ref, dk_ref, dv_ref):
        i = pl.program_id(1)

        @pl.when(i == 0)
        def _():
            dk_ref[...] = jnp.zeros_like(dk_ref)
            dv_ref[...] = jnp.zeros_like(dv_ref)

        after2, from2, col_minus_row, heads = _sb_consts()
        q = q_ref[...].astype(F32) * QK_SCALE
        d_out = do_ref[...].astype(F32)
        prod = d_out * o_ref[...]
        qhs = [jnp.where(hm, q, 0.0).astype(BF16) for hm in heads]
        dohs = [jnp.where(hm, d_out, 0.0).astype(BF16) for hm in heads]
        totals = [jnp.broadcast_to(jnp.sum(jnp.where(hm, prod, 0.0), axis=1, keepdims=True), (SB_BLK, SB_BLK))
                  for hm in heads]
        q2 = jnp.concatenate(qhs, axis=0)
        do2 = jnp.concatenate(dohs, axis=0)
        first_row = lax.bitwise_and(i, SB_SUB - 1) * SB_BLK

        def block(j, state, masked):
            rows = pl.ds(pl.multiple_of(j * SB_KEYS, SB_KEYS), SB_KEYS)
            kb, vb = k_ref[rows, :], v_ref[rows, :]
            new, a_all, dz_all = [], [], []
            for h in range(2):
                run, seen, dq = state[h]
                z = _dot(qhs[h], kb, NT)
                da = _dot(dohs[h], vb, NT)
                a_parts, dz_parts = [None] * SB_SUB, [None] * SB_SUB
                for c in reversed(range(SB_SUB)):
                    cols = slice(c * SB_BLK, (c + 1) * SB_BLK)
                    mask = (col_minus_row < first_row - c * SB_BLK) if masked else None
                    zc = z[:, cols]
                    e, a, run = _sb_weights(zc, after2, run, mask)
                    dl = a * da[:, cols]
                    inner, total = _suffix(dl, from2)
                    d_lom = totals[h] - (inner + seen)
                    if masked:
                        d_lom = jnp.where(mask, d_lom, 0.0)
                    seen = seen + total
                    rinv = 1.0 / (1.0 + e)
                    small = e * rinv
                    pos = zc >= 0.0
                    dz = dl * jnp.where(pos, small, rinv) - d_lom * jnp.where(pos, rinv, small)
                    a_parts[c], dz_parts[c] = a.astype(BF16), dz.astype(BF16)
                a_all.append(jnp.concatenate(a_parts, axis=1))
                dz_all.append(jnp.concatenate(dz_parts, axis=1))
                new.append((run, seen, dq + _dot(dz_all[h], kb)))
            dv_ref[rows, :] += _dot(jnp.concatenate(a_all, axis=0), do2, TN)
            dk_ref[rows, :] += _dot(jnp.concatenate(dz_all, axis=0), q2, TN)
            return tuple(new)

        zero = jnp.zeros((SB_BLK, LANES), F32)
        state = _sb_walk(i, block, ((zero, zero, zero), (zero, zero, zero)))
        dq_ref[...] = (jnp.where(heads[0], state[0][2], state[1][2]) * QK_SCALE).astype(BF16)

    pairs = SB_W // LANES
    blk = lambda p, i: (i, p)
    col = lambda p, i: (0, p)
    return pl.pallas_call(
        body, name=name, grid=(pairs, s // SB_BLK),
        in_specs=[pl.BlockSpec((SB_BLK, LANES), blk), pl.BlockSpec((s, LANES), col),
                  pl.BlockSpec((s, LANES), lambda p, i: (0, pairs + p)), pl.BlockSpec((SB_BLK, LANES), blk),
                  pl.BlockSpec((SB_BLK, LANES), blk)],
        out_specs=[pl.BlockSpec((SB_BLK, LANES), blk), pl.BlockSpec((s, LANES), col), pl.BlockSpec((s, LANES), col)],
        out_shape=[jax.ShapeDtypeStruct((s, SB_W), BF16), jax.ShapeDtypeStruct((s, SB_W), F32),
                   jax.ShapeDtypeStruct((s, SB_W), F32)],
        compiler_params=_params("arbitrary", "arbitrary"),
    )(proj, kv, kv, out, dy)


def final_loss(x, g, target, name):
    s, d = x.shape
    tm = _tile(s, 256, 8)

    def body(x_ref, g_ref, t_ref, loss_ref, dx_ref, dg_ref):
        @pl.when(pl.program_id(0) == 0)
        def _():
            loss_ref[...] = jnp.zeros_like(loss_ref)
            dg_ref[...] = jnp.zeros_like(dg_ref)

        xf = x_ref[...]
        r = _rms(xf)
        xhat = xf * r
        gain = g_ref[...]
        diff = xhat * gain - t_ref[...]
        sq = jnp.sum(jnp.sum(diff * diff, axis=1, keepdims=True), axis=0, keepdims=True)
        loss_ref[...] += jnp.broadcast_to(sq, loss_ref.shape)
        dy = diff * (1.0 / d)
        dg_ref[...] += jnp.sum(dy * xhat, axis=0, keepdims=True)
        dxhat = dy * gain
        dx_ref[...] = r * (dxhat - xhat * jnp.mean(dxhat * xhat, axis=-1, keepdims=True))

    row = lambda i: (i, 0)
    const = lambda i: (0, 0)
    return pl.pallas_call(
        body, name=name, grid=(s // tm,),
        in_specs=[pl.BlockSpec((tm, d), row), pl.BlockSpec((1, d), const), pl.BlockSpec((tm, d), row)],
        out_specs=[pl.BlockSpec((8, LANES), const), pl.BlockSpec((tm, d), row), pl.BlockSpec((1, d), const)],
        out_shape=[jax.ShapeDtypeStruct((8, LANES), F32), jax.ShapeDtypeStruct((s, d), F32), jax.ShapeDtypeStruct((1, d), F32)],
        compiler_params=_params("arbitrary"),
    )(x, g, target)


def adamw(w, parts, m, v, name):
    rows, cols = w.shape
    k = parts.shape[0]
    tr = _tile(rows, 512, 16)
    c1, c2 = 1.0 - ADAM_B1 ** ADAM_STEP, 1.0 - ADAM_B2 ** ADAM_STEP

    def body(w_ref, p_ref, m_ref, v_ref, g_ref, d_ref, nm_ref, nv_ref):
        grad = p_ref[0].astype(F32)
        for s in range(1, k):
            grad = grad + p_ref[s].astype(F32)
        nm = ADAM_B1 * m_ref[...] + (1.0 - ADAM_B1) * grad
        nv = ADAM_B2 * v_ref[...] + (1.0 - ADAM_B2) * (grad * grad)
        g_ref[...] = grad
        d_ref[...] = -ADAM_LR * ((nm / c1) / (jnp.sqrt(nv / c2) + ADAM_EPS) + ADAM_WD * w_ref[...])
        nm_ref[...] = nm
        nv_ref[...] = nv

    spec = pl.BlockSpec((tr, cols), lambda i: (i, 0))
    shape = jax.ShapeDtypeStruct((rows, cols), F32)
    return pl.pallas_call(
        body, name=name, grid=(rows // tr,),
        in_specs=[spec, pl.BlockSpec((k, tr, cols), lambda i: (0, i, 0)), spec, spec],
        out_specs=[spec] * 4, out_shape=[shape] * 4,
        compiler_params=_params("arbitrary"),
    )(w, parts, m, v)


SHARDED = {"ffn1_w_gate": 2, "ffn1_w_up": 2, "ffn1_w_down": 1, "ffn2_w_gate": 2, "ffn2_w_up": 2, "ffn2_w_down": 1,
           "w_mem_kv": 1, "a_w_in": 2, "a_w_out": 1, "w_kv": 1, "b_w_in": 1, "b_w_out": 1}
SMALL = ["ffn1_norm", "mix_norm", "ffn2_norm", "mem_norm", "kv_norm", "final_norm", "a_v_norm", "a_w_spatial", "a_b_spatial"]
WEIGHTS = ["ffn1_norm", "ffn1_w_gate", "ffn1_w_up", "ffn1_w_down", "mix_norm", "ffn2_norm", "ffn2_w_gate", "ffn2_w_up",
           "ffn2_w_down", "mem_norm", "w_mem_kv", "a_w_in", "a_v_norm", "a_w_spatial", "a_b_spatial", "a_w_out", "kv_norm",
           "w_kv", "b_w_in", "b_w_out", "final_norm"]


def _gather_weights(shards):
    by_chip = exchange([shards[n].astype(BF16) for n in SHARDED], "quad", True, "gather_chips")
    by_core = exchange(by_chip, "pair", True, "gather_cores", split=True)
    return {n: jnp.concatenate([blocks[d % 2, d // 2] for d in range(N_DEV)], axis=axis)
            for (n, axis), blocks in zip(SHARDED.items(), by_core)}


def _scatter_grads(grads):
    by_core = []
    for n, axis in SHARDED.items():
        parts = jnp.stack(jnp.split(grads[n].astype(BF16), N_DEV, axis=axis))
        by_core.append(parts.reshape((4, 2) + parts.shape[1:]).swapaxes(0, 1))
    pair = exchange(by_core, "pair", False, "scatter_cores", split=True)
    chip_sums = [sum_leading(p.reshape(2, -1, p.shape[-1]), BF16, "scatter_cores_sum").reshape(p.shape[1:]) for p in pair]
    quad = exchange(chip_sums, "quad", False, "scatter_chips")
    return dict(zip(SHARDED, quad))


def _all_sum(parts, name):
    flat = jnp.concatenate([p.reshape(-1) for p in parts])
    pad = (-flat.size) % (16 * LANES)
    buf = jnp.pad(flat, (0, pad)).reshape(-1, LANES)
    total = sum_leading(exchange([buf], "all", True, name)[0], F32, name + "_sum").reshape(-1)
    out, off = [], 0
    for p in parts:
        out.append(total[off:off + p.size].reshape(p.shape))
        off += p.size
    return out


def _device_index():
    return 4 * lax.axis_index("x") + 2 * lax.axis_index("y") + lax.axis_index("c")


def kernel(x, mem, ffn1_norm, ffn1_w_gate, ffn1_w_up, ffn1_w_down, mix_norm, ffn2_norm, ffn2_w_gate, ffn2_w_up, ffn2_w_down, mem_norm, w_mem_kv, a_w_in, a_v_norm, a_w_spatial, a_b_spatial, a_w_out, kv_norm, w_kv, b_w_in, b_w_out, final_norm, loss_target, m_ffn1_norm, m_ffn1_w_gate, m_ffn1_w_up, m_ffn1_w_down, m_mix_norm, m_ffn2_norm, m_ffn2_w_gate, m_ffn2_w_up, m_ffn2_w_down, m_mem_norm, m_w_mem_kv, m_a_w_in, m_a_v_norm, m_a_w_spatial, m_a_b_spatial, m_a_w_out, m_kv_norm, m_w_kv, m_b_w_in, m_b_w_out, m_final_norm, v_ffn1_norm, v_ffn1_w_gate, v_ffn1_w_up, v_ffn1_w_down, v_mix_norm, v_ffn2_norm, v_ffn2_w_gate, v_ffn2_w_up, v_ffn2_w_down, v_mem_norm, v_w_mem_kv, v_a_w_in, v_a_v_norm, v_a_w_spatial, v_a_b_spatial, v_a_w_out, v_kv_norm, v_w_kv, v_b_w_in, v_b_w_out, v_final_norm):
    weights = dict(ffn1_norm=ffn1_norm, ffn1_w_gate=ffn1_w_gate, ffn1_w_up=ffn1_w_up, ffn1_w_down=ffn1_w_down, mix_norm=mix_norm, ffn2_norm=ffn2_norm, ffn2_w_gate=ffn2_w_gate, ffn2_w_up=ffn2_w_up, ffn2_w_down=ffn2_w_down, mem_norm=mem_norm, w_mem_kv=w_mem_kv, a_w_in=a_w_in, a_v_norm=a_v_norm, a_w_spatial=a_w_spatial, a_b_spatial=a_b_spatial, a_w_out=a_w_out, kv_norm=kv_norm, w_kv=w_kv, b_w_in=b_w_in, b_w_out=b_w_out, final_norm=final_norm)
    mom1 = dict(ffn1_norm=m_ffn1_norm, ffn1_w_gate=m_ffn1_w_gate, ffn1_w_up=m_ffn1_w_up, ffn1_w_down=m_ffn1_w_down, mix_norm=m_mix_norm, ffn2_norm=m_ffn2_norm, ffn2_w_gate=m_ffn2_w_gate, ffn2_w_up=m_ffn2_w_up, ffn2_w_down=m_ffn2_w_down, mem_norm=m_mem_norm, w_mem_kv=m_w_mem_kv, a_w_in=m_a_w_in, a_v_norm=m_a_v_norm, a_w_spatial=m_a_w_spatial, a_b_spatial=m_a_b_spatial, a_w_out=m_a_w_out, kv_norm=m_kv_norm, w_kv=m_w_kv, b_w_in=m_b_w_in, b_w_out=m_b_w_out, final_norm=m_final_norm)
    mom2 = dict(ffn1_norm=v_ffn1_norm, ffn1_w_gate=v_ffn1_w_gate, ffn1_w_up=v_ffn1_w_up, ffn1_w_down=v_ffn1_w_down, mix_norm=v_mix_norm, ffn2_norm=v_ffn2_norm, ffn2_w_gate=v_ffn2_w_gate, ffn2_w_up=v_ffn2_w_up, ffn2_w_down=v_ffn2_w_down, mem_norm=v_mem_norm, w_mem_kv=v_w_mem_kv, a_w_in=v_a_w_in, a_v_norm=v_a_v_norm, a_w_spatial=v_a_w_spatial, a_b_spatial=v_a_b_spatial, a_w_out=v_a_w_out, kv_norm=v_kv_norm, w_kv=v_w_kv, b_w_in=v_b_w_in, b_w_out=v_b_w_out, final_norm=v_final_norm)

    dev = _device_index()
    xs, mem_in, target = x[0], mem[0], loss_target[0]
    d_model = xs.shape[1]
    shards = {n: weights[n] for n in SHARDED}
    full = _gather_weights(shards)
    vn_width = a_v_norm.shape[1]
    a_v_full = _all_sum([lax.dynamic_update_slice(jnp.zeros((N_A, N_DEV * vn_width), F32), a_v_norm, (0, dev * vn_width))],
                        "gather_v_norm")[0]

    row = lambda v: v.reshape(1, -1)
    w_gu = {f: jnp.concatenate([full[f + "_w_gate"], full[f + "_w_up"]], axis=2) for f in ("ffn1", "ffn2")}
    w_mem_cat = full["w_mem_kv"].transpose(1, 0, 2).reshape(d_model, -1)
    bias = [jnp.repeat(a_b_spatial[i].T, GM_P, axis=1) for i in range(N_A)]

    mem_kv, mem_h = norm_mm(mem_in, row(mem_norm), w_mem_cat, BF16, "mem_kv", emit_h=True)

    def ffn_fwd(xin, f, l):
        gu = norm_mm(xin, row(weights[f + "_norm"][l]), w_gu[f][l], BF16, "ffn_gu")
        act = swiglu_fwd(gu, "ffn_act")
        return mm_res(act, full[f + "_w_down"][l], xin, 0.5, "ffn_down"), gu, act

    saved = []
    kv = x_kv = None
    cur = xs
    for l in range(DEPTH):
        st = {"x0": cur}
        if l == N_A:
            x_kv = cur
            kv = norm_mm(cur, row(kv_norm), full["w_kv"], BF16, "kv_proj")
        st["x1"], st["gu1"], st["act1"] = ffn_fwd(cur, "ffn1", l)
        if l < N_A:
            proj = norm_mm(st["x1"], row(mix_norm[l]), full["a_w_in"][l], F32, "a_proj")
            y_tok = gmlp_fwd(proj, row(a_v_full[l]), a_w_spatial[l], bias[l], "gmlp_fwd")
            y_mem = mem_fwd(proj, 2 * GM_W // MEM_W, mem_kv, l, "mem_fwd_a")
            w_out = full["a_w_out"][l]
        else:
            proj = norm_mm(st["x1"], row(mix_norm[l]), full["b_w_in"][l - N_A], BF16, "b_proj")
            st["sb_out"] = sb_fwd(proj, kv, "sb_fwd")
            y_tok = st["sb_out"].astype(BF16)
            y_mem = mem_fwd(proj, SB_W // MEM_W, mem_kv, l, "mem_fwd_b")
            w_out = full["b_w_out"][l - N_A]
        st["proj"] = proj
        st["y"] = jnp.concatenate([y_tok, y_mem], axis=1)
        st["x2"] = mm_res(st["y"], w_out, st["x1"], 1.0, "mix_out")
        cur, st["gu2"], st["act2"] = ffn_fwd(st["x2"], "ffn2", l)
        saved.append(st)

    loss_blk, dx, d_final = final_loss(cur, row(final_norm), target, "final_loss")
    loss = lax.psum(loss_blk[0, 0] * (0.5 / d_model), AXES)

    grads = {n: [None] * weights[n].shape[0] for n in WEIGHTS if weights[n].ndim >= 2 and n not in ("w_kv",)}
    grads["final_norm"] = d_final.reshape(-1)
    d_mem_kv = [None] * DEPTH
    d_kv = []

    def ffn_bwd(dx, xin, gu, act, f, l):
        d_act = mm_nt(dx, full[f + "_w_down"][l], 0.5, "ffn_dact")
        d_gu = swiglu_bwd(gu, d_act, "ffn_dgu")
        dx_new, d_gain, h = mm_nt_normbwd(d_gu, w_gu[f][l], xin, row(weights[f + "_norm"][l]), dx, "ffn_dx")
        d_wgu = mm_tn(h, d_gu, 1.0, "ffn_dwgu")
        half = d_wgu.shape[1] // 2
        grads[f + "_w_gate"][l], grads[f + "_w_up"][l] = d_wgu[:, :half], d_wgu[:, half:]
        grads[f + "_w_down"][l] = mm_tn(act, dx, 0.5, "ffn_dwdown", ta_target=1408, tb_target=1024)
        grads[f + "_norm"][l] = d_gain.reshape(-1)
        return dx_new

    for l in reversed(range(DEPTH)):
        st = saved[l]
        dx = ffn_bwd(dx, st["x2"], st["gu2"], st["act2"], "ffn2", l)
        proj = st["proj"]
        if l < N_A:
            w_in, w_out, key_in, key_out, idx = full["a_w_in"][l], full["a_w_out"][l], "a_w_in", "a_w_out", l
        else:
            w_in, w_out, key_in, key_out, idx = full["b_w_in"][l - N_A], full["b_w_out"][l - N_A], "b_w_in", "b_w_out", l - N_A
        dy = mm_nt(dx, w_out, 1.0, "mix_dy")
        grads[key_out][idx] = mm_tn(st["y"], dx, 1.0, "mix_dwout", tb_target=1024)
        if l < N_A:
            d_uv, d_ws, d_bs, d_vgain = gmlp_bwd(proj, dy, row(a_v_full[l]), a_w_spatial[l], bias[l], "gmlp_bwd")
            grads["a_w_spatial"][l], grads["a_b_spatial"][l], grads["a_v_norm"][l] = d_ws, d_bs[:, :, 0], d_vgain.reshape(-1)
            d_q, d_k, d_v = mem_bwd(proj, 2 * GM_W // MEM_W, mem_kv, l, dy, GM_W // MEM_W, "mem_bwd_a")
            d_proj = jnp.concatenate([d_uv, d_q], axis=1)
        else:
            d_qsb, d_ksb, d_vsb = sb_bwd(proj, kv, st["sb_out"], dy, "sb_bwd")
            d_kv.append(jnp.concatenate([d_ksb, d_vsb], axis=1))
            d_q, d_k, d_v = mem_bwd(proj, SB_W // MEM_W, mem_kv, l, dy, SB_W // MEM_W, "mem_bwd_b")
            d_proj = jnp.concatenate([d_qsb, d_q], axis=1)
        d_mem_kv[l] = jnp.concatenate([d_k, d_v], axis=1)
        dx, d_gain, h = mm_nt_normbwd(d_proj, w_in, st["x1"], row(mix_norm[l]), dx, "mix_dx")
        grads["mix_norm"][l] = d_gain.reshape(-1)
        grads[key_in][idx] = mm_tn(h, d_proj, 1.0, "mix_dwin")
        dx = ffn_bwd(dx, st["x0"], st["gu1"], st["act1"], "ffn1", l)
        if l == N_A:
            d_kv_b = sum_leading(jnp.stack(d_kv), BF16, "kv_dsum")
            dx, d_gain, h = mm_nt_normbwd(d_kv_b, full["w_kv"], x_kv, row(kv_norm), dx, "kv_dx")
            grads["kv_norm"] = d_gain.reshape(-1)
            grads["w_kv"] = mm_tn(h, d_kv_b, 1.0, "kv_dw")

    d_mem_all = jnp.concatenate(d_mem_kv, axis=1).astype(BF16)
    _, d_gain, _ = mm_nt_normbwd(d_mem_all, w_mem_cat, mem_in, row(mem_norm), None, "mem_dnorm")
    grads["mem_norm"] = d_gain.reshape(-1)
    d_wmem = mm_tn(mem_h, d_mem_all, 1.0, "mem_dw")
    grads["w_mem_kv"] = d_wmem.reshape(d_model, DEPTH, -1).transpose(1, 0, 2)
    grads = {n: (jnp.stack(g) if isinstance(g, list) else g) for n, g in grads.items()}

    parts = _scatter_grads({n: grads[n] for n in SHARDED})
    for n, g in zip(SMALL, _all_sum([grads[n] for n in SMALL], "sum_small")):
        parts[n] = g[None]
    parts["a_v_norm"] = lax.dynamic_slice(parts["a_v_norm"], (0, 0, dev * vn_width), (1,) + a_v_norm.shape)

    reduced, deltas, new_m, new_v = {}, {}, {}, {}
    for n in WEIGHTS:
        w = weights[n]
        view = (lambda a: a.reshape(-1, a.shape[-1]))
        res = adamw(view(w), parts[n].reshape(parts[n].shape[0], -1, w.shape[-1]), view(mom1[n]), view(mom2[n]), "adamw")
        reduced[n], deltas[n], new_m[n], new_v[n] = [r.reshape(w.shape) for r in res]

    return (loss, dx[None], *[reduced[n] for n in WEIGHTS], *[deltas[n] for n in WEIGHTS],
            *[new_m[n] for n in WEIGHTS], *[new_v[n] for n in WEIGHTS])
```

```python
import functools

import jax
import jax.numpy as jnp
from jax import lax
from jax.experimental import pallas as pl
from jax.experimental.pallas import tpu as pltpu

F32, BF16 = jnp.float32, jnp.bfloat16
MESH_ID = pl.DeviceIdType.MESH
AXES = ("x", "y", "c")
N_DEV = 8

EPS = 1e-6
DEPTH, N_A = 4, 2
GM_W, GM_GROUPS, GM_P = 768, 6, 128
MEM_W, MEM_HEADS, HEAD_DIM = 256, 4, 64
SB_W, SB_BLK = 768, 128
LANES = 128
QK_SCALE = HEAD_DIM ** -0.5
GELU_C, GELU_A = 0.7978845608028654, 0.044715

ADAM_LR, ADAM_B1, ADAM_B2, ADAM_EPS, ADAM_WD, ADAM_STEP = 0.001, 0.9, 0.999, 1e-08, 0.01, 10

VMEM_LIMIT = 56 * 1024 * 1024
PACK_COLS = 512

NT = (((1,), (1,)), ((), ()))
TN = (((0,), (0,)), ((), ()))


def _params(*sem):
    return pltpu.CompilerParams(dimension_semantics=sem, vmem_limit_bytes=VMEM_LIMIT)


def _tile(n, target, mult=LANES):
    best = None
    for t in range(mult, min(n, target) + 1, mult):
        if n % t == 0:
            best = t
    return best if best is not None else n


def _dot(a, b, dims=None):
    if dims is None:
        return jnp.dot(a, b, preferred_element_type=F32)
    return lax.dot_general(a, b, dims, preferred_element_type=F32)


def exchange(srcs, group, same_src, name, split=False):
    size = {"pair": 2, "quad": 4, "all": 8}[group]
    n = len(srcs)
    chunk_shapes = [tuple(s.shape) if same_src else tuple(s.shape[1:]) for s in srcs]
    pieces = [cs[0] if split else 1 for cs in chunk_shapes]
    n_dma = sum(pieces)

    def body(*refs):
        src_refs, out_refs = refs[:n], refs[n:2 * n]
        send_sems, recv_sems, local_sems = refs[2 * n:]
        x, y, c = lax.axis_index("x"), lax.axis_index("y"), lax.axis_index("c")
        if group == "pair":
            me, dev = c, lambda p: (x, y, p)
        elif group == "quad":
            me, dev = 2 * x + y, lambda p: (p // 2, p % 2, c)
        else:
            me, dev = 4 * x + 2 * y + c, lambda p: (p // 4, (p // 2) % 2, p % 2)

        def chunk(t, idx):
            return src_refs[t] if same_src else src_refs[t].at[idx]

        def copies(k, idx, slot, peer):
            out, w = [], k * n_dma
            for t in range(n):
                src, dst = chunk(t, idx), out_refs[t].at[slot]
                for s_ref, d_ref in ([(src.at[u], dst.at[u]) for u in range(pieces[t])] if split else [(src, dst)]):
                    out.append(pltpu.make_async_remote_copy(
                        src_ref=s_ref, dst_ref=d_ref, send_sem=send_sems.at[w], recv_sem=recv_sems.at[w],
                        device_id=dev(peer), device_id_type=MESH_ID))
                    w += 1
            return out

        local = [pltpu.make_async_copy(chunk(t, me), out_refs[t].at[me], local_sems.at[t]) for t in range(n)]
        for cp in local:
            cp.start()
        sends = []
        for k in range(1, size):
            peer = (me + k) % size
            sends += copies(k, peer, me, peer)
        for cp in sends:
            cp.start()
        for k in range(1, size):
            sender = (me + size - k) % size
            for cp in copies(k, me, sender, sender):
                cp.wait_recv()
        for cp in sends:
            cp.wait_send()
        for cp in local:
            cp.wait()

    hbm = pl.BlockSpec(memory_space=pltpu.HBM)
    return pl.pallas_call(
        body, name=name,
        out_shape=[jax.ShapeDtypeStruct((size,) + cs, s.dtype) for cs, s in zip(chunk_shapes, srcs)],
        in_specs=[hbm] * n, out_specs=[hbm] * n,
        scratch_shapes=[pltpu.SemaphoreType.DMA((size * n_dma,)), pltpu.SemaphoreType.DMA((size * n_dma,)),
                        pltpu.SemaphoreType.DMA((n,))],
    )(*srcs)


def sum_leading(parts, out_dtype, name):
    k, rows, cols = parts.shape
    tr = _tile(rows, 512, 16)

    def body(p_ref, o_ref):
        acc = p_ref[0].astype(F32)
        for s in range(1, k):
            acc = acc + p_ref[s].astype(F32)
        o_ref[...] = acc.astype(o_ref.dtype)

    return pl.pallas_call(
        body, name=name, grid=(rows // tr,),
        in_specs=[pl.BlockSpec((k, tr, cols), lambda i: (0, i, 0))],
        out_specs=pl.BlockSpec((tr, cols), lambda i: (i, 0)),
        out_shape=jax.ShapeDtypeStruct((rows, cols), out_dtype),
        compiler_params=_params("arbitrary"),
    )(parts)


def _rms(xf):
    return lax.rsqrt(jnp.mean(xf * xf, axis=-1, keepdims=True) + EPS)


def norm_mm(x, g, w, out_dtype, name, emit_h=False):
    m, d = x.shape
    n = w.shape[1]
    tm, tn = _tile(m, 1024, 8), _tile(n, 1408)

    def body(x_ref, g_ref, w_ref, o_ref, *rest):
        h_ref = rest[-1]

        @pl.when(pl.program_id(1) == 0)
        def _():
            xf = x_ref[...]
            hb = ((xf * _rms(xf)) * g_ref[...]).astype(BF16)
            h_ref[...] = hb
            if emit_h:
                rest[0][...] = hb

        o_ref[...] = _dot(h_ref[...], w_ref[...]).astype(o_ref.dtype)

    out_shape = [jax.ShapeDtypeStruct((m, n), out_dtype)]
    out_specs = [pl.BlockSpec((tm, tn), lambda i, j: (i, j))]
    if emit_h:
        out_shape.append(jax.ShapeDtypeStruct((m, d), BF16))
        out_specs.append(pl.BlockSpec((tm, d), lambda i, j: (i, 0)))
    res = pl.pallas_call(
        body, name=name, grid=(m // tm, n // tn),
        in_specs=[pl.BlockSpec((tm, d), lambda i, j: (i, 0)), pl.BlockSpec((1, d), lambda i, j: (0, 0)),
                  pl.BlockSpec((d, tn), lambda i, j: (0, j))],
        out_specs=out_specs, out_shape=out_shape,
        scratch_shapes=[pltpu.VMEM((tm, d), BF16)],
        compiler_params=_params("arbitrary", "arbitrary"),
    )(x, g, w)
    return res if emit_h else res[0]


def mm_res(a, w, res, alpha, name):
    m, k = a.shape
    n = w.shape[1]
    tm, tn = _tile(m, 1024, 8), _tile(n, 1024)

    def body(a_ref, w_ref, r_ref, o_ref):
        o_ref[...] = r_ref[...] + alpha * _dot(a_ref[...], w_ref[...])

    return pl.pallas_call(
        body, name=name, grid=(m // tm, n // tn),
        in_specs=[pl.BlockSpec((tm, k), lambda i, j: (i, 0)), pl.BlockSpec((k, tn), lambda i, j: (0, j)),
                  pl.BlockSpec((tm, tn), lambda i, j: (i, j))],
        out_specs=pl.BlockSpec((tm, tn), lambda i, j: (i, j)),
        out_shape=jax.ShapeDtypeStruct((m, n), F32),
        compiler_params=_params("arbitrary", "arbitrary"),
    )(a, w, res)


def mm_nt(x, w, alpha, name):
    m, d = x.shape
    n = w.shape[0]
    tm, tn = _tile(m, 1024, 8), _tile(n, 1408)

    def body(x_ref, w_ref, o_ref, xb_ref):
        @pl.when(pl.program_id(1) == 0)
        def _():
            xb_ref[...] = x_ref[...].astype(BF16)

        o_ref[...] = (alpha * _dot(xb_ref[...], w_ref[...], NT)).astype(o_ref.dtype)

    return pl.pallas_call(
        body, name=name, grid=(m // tm, n // tn),
        in_specs=[pl.BlockSpec((tm, d), lambda i, j: (i, 0)), pl.BlockSpec((tn, d), lambda i, j: (j, 0))],
        out_specs=pl.BlockSpec((tm, tn), lambda i, j: (i, j)),
        out_shape=jax.ShapeDtypeStruct((m, n), BF16),
        scratch_shapes=[pltpu.VMEM((tm, d), BF16)],
        compiler_params=_params("arbitrary", "arbitrary"),
    )(x, w)


def mm_tn(a, b, alpha, name, ta_target=1024, tb_target=512):
    s, ka = a.shape
    nb = b.shape[1]
    ta, tb, ts = _tile(ka, ta_target), _tile(nb, tb_target), _tile(s, 1024, 16)
    steps = s // ts

    def body(a_ref, b_ref, o_ref, acc_ref):
        t = pl.program_id(2)

        @pl.when(t == 0)
        def _():
            acc_ref[...] = jnp.zeros_like(acc_ref)

        acc_ref[...] += _dot(a_ref[...].astype(BF16), b_ref[...].astype(BF16), TN)

        @pl.when(t == steps - 1)
        def _():
            o_ref[...] = alpha * acc_ref[...]

    return pl.pallas_call(
        body, name=name, grid=(ka // ta, nb // tb, steps),
        in_specs=[pl.BlockSpec((ts, ta), lambda i, j, t: (t, i)), pl.BlockSpec((ts, tb), lambda i, j, t: (t, j))],
        out_specs=pl.BlockSpec((ta, tb), lambda i, j, t: (i, j)),
        out_shape=jax.ShapeDtypeStruct((ka, nb), F32),
        scratch_shapes=[pltpu.VMEM((ta, tb), F32)],
        compiler_params=_params("arbitrary", "arbitrary", "arbitrary"),
    )(a, b)


def mm_nt_normbwd(dy, w, x, g, res, name):
    m, n = dy.shape
    d = w.shape[0]
    tm, tk = _tile(m, 1024, 8), _tile(n, 1408)
    steps = n // tk
    has_res = res is not None

    def body(*refs):
        if has_res:
            dy_ref, w_ref, x_ref, g_ref, r_ref, dx_ref, dg_ref, h_ref, acc_ref = refs
        else:
            dy_ref, w_ref, x_ref, g_ref, dx_ref, dg_ref, h_ref, acc_ref = refs
        i, t = pl.program_id(0), pl.program_id(1)

        @pl.when(t == 0)
        def _():
            acc_ref[...] = jnp.zeros_like(acc_ref)

        @pl.when((t == 0) & (i == 0))
        def _():
            dg_ref[...] = jnp.zeros_like(dg_ref)

        acc_ref[...] += _dot(dy_ref[...], w_ref[...], NT)

        @pl.when(t == steps - 1)
        def _():
            xf = x_ref[...]
            r = _rms(xf)
            xhat = xf * r
            dh = acc_ref[...]
            gain = g_ref[...]
            dg_ref[...] += jnp.sum(dh * xhat, axis=0, keepdims=True)
            dxhat = dh * gain
            dx = r * (dxhat - xhat * jnp.mean(dxhat * xhat, axis=-1, keepdims=True))
            dx_ref[...] = (r_ref[...] + dx) if has_res else dx
            h_ref[...] = (xhat * gain).astype(BF16)

    row = lambda i, t: (i, 0)
    in_specs = [pl.BlockSpec((tm, tk), lambda i, t: (i, t)), pl.BlockSpec((d, tk), lambda i, t: (0, t)),
                pl.BlockSpec((tm, d), row), pl.BlockSpec((1, d), lambda i, t: (0, 0))]
    args = [dy, w, x, g]
    if has_res:
        in_specs.append(pl.BlockSpec((tm, d), row))
        args.append(res)
    return pl.pallas_call(
        body, name=name, grid=(m // tm, steps),
        in_specs=in_specs,
        out_specs=[pl.BlockSpec((tm, d), row), pl.BlockSpec((1, d), lambda i, t: (0, 0)), pl.BlockSpec((tm, d), row)],
        out_shape=[jax.ShapeDtypeStruct((m, d), F32), jax.ShapeDtypeStruct((1, d), F32), jax.ShapeDtypeStruct((m, d), BF16)],
        scratch_shapes=[pltpu.VMEM((tm, d), F32)],
        compiler_params=_params("arbitrary", "arbitrary"),
    )(*args)


def _sigmoid(z):
    return 1.0 / (1.0 + jnp.exp(-z))


def swiglu_fwd(gu, name):
    m, f2 = gu.shape
    f = f2 // 2
    tm = _tile(m, 256, 16)

    def body(gu_ref, o_ref):
        gate, up = gu_ref[:, :f].astype(F32), gu_ref[:, f:].astype(F32)
        o_ref[...] = (gate * _sigmoid(gate) * up).astype(BF16)

    return pl.pallas_call(
        body, name=name, grid=(m // tm,),
        in_specs=[pl.BlockSpec((tm, f2), lambda i: (i, 0))], out_specs=pl.BlockSpec((tm, f), lambda i: (i, 0)),
        out_shape=jax.ShapeDtypeStruct((m, f), BF16), compiler_params=_params("arbitrary"),
    )(gu)


def swiglu_bwd(gu, da, name):
    m, f2 = gu.shape
    f = f2 // 2
    tm = _tile(m, 256, 16)

    def body(gu_ref, da_ref, o_ref):
        gate, up = gu_ref[:, :f].astype(F32), gu_ref[:, f:].astype(F32)
        d = da_ref[...].astype(F32)
        sg = _sigmoid(gate)
        o_ref[:, :f] = (d * up * (sg * (1.0 + gate * (1.0 - sg)))).astype(BF16)
        o_ref[:, f:] = (d * (gate * sg)).astype(BF16)

    return pl.pallas_call(
        body, name=name, grid=(m // tm,),
        in_specs=[pl.BlockSpec((tm, f2), lambda i: (i, 0)), pl.BlockSpec((tm, f), lambda i: (i, 0))],
        out_specs=pl.BlockSpec((tm, f2), lambda i: (i, 0)),
        out_shape=jax.ShapeDtypeStruct((m, f2), BF16), compiler_params=_params("arbitrary"),
    )(gu, da)


def _gelu(x):
    return 0.5 * x * (1.0 + jnp.tanh(GELU_C * (x + GELU_A * x * x * x)))


def _gelu_grad(x):
    t = jnp.tanh(GELU_C * (x + GELU_A * x * x * x))
    return 0.5 * (1.0 + t) + 0.5 * x * (1.0 - t * t) * (GELU_C * (1.0 + 3.0 * GELU_A * x * x))


def _chunk_mask():
    row = lax.broadcasted_iota(jnp.int32, (GM_P, GM_P), 0)
    col = lax.broadcasted_iota(jnp.int32, (GM_P, GM_P), 1)
    return (col < GM_P // 2) | (row >= GM_P // 2)


def gmlp_fwd(proj, gain, w_s, bias, name):
    s, pw = proj.shape
    tm = _tile(s, 256, GM_P)

    def body(p_ref, gain_ref, w_ref, b_ref, o_ref):
        mask = _chunk_mask()
        u = _gelu(p_ref[:, :GM_W])
        v = _gelu(p_ref[:, GM_W:2 * GM_W])
        vn = ((v * _rms(v)) * gain_ref[...]).astype(BF16)
        for g in range(GM_GROUPS):
            wg = jnp.where(mask, w_ref[g], 0.0).astype(BF16)
            cols = slice(g * GM_P, (g + 1) * GM_P)
            for n in range(tm // GM_P):
                rows = slice(n * GM_P, (n + 1) * GM_P)
                mixed = _dot(wg, vn[rows, cols]) + b_ref[:, cols]
                o_ref[rows, cols] = (u[rows, cols] * mixed).astype(BF16)

    return pl.pallas_call(
        body, name=name, grid=(s // tm,),
        in_specs=[pl.BlockSpec((tm, pw), lambda i: (i, 0)), pl.BlockSpec((1, GM_W), lambda i: (0, 0)),
                  pl.BlockSpec((GM_GROUPS, GM_P, GM_P), lambda i: (0, 0, 0)), pl.BlockSpec((GM_P, GM_W), lambda i: (0, 0))],
        out_specs=pl.BlockSpec((tm, GM_W), lambda i: (i, 0)),
        out_shape=jax.ShapeDtypeStruct((s, GM_W), BF16), compiler_params=_params("arbitrary"),
    )(proj, gain, w_s, bias)


def gmlp_bwd(proj, dy, gain, w_s, bias, name):
    s, pw = proj.shape
    dw_total = dy.shape[1]
    tm = _tile(s, 256, GM_P)

    def body(p_ref, dy_ref, gain_ref, w_ref, b_ref, dp_ref, dw_ref, db_ref, dgain_ref, dvn_ref):
        @pl.when(pl.program_id(0) == 0)
        def _():
            dw_ref[...] = jnp.zeros_like(dw_ref)
            db_ref[...] = jnp.zeros_like(db_ref)
            dgain_ref[...] = jnp.zeros_like(dgain_ref)

        mask = _chunk_mask()
        pu = p_ref[:, :GM_W]
        pv = p_ref[:, GM_W:2 * GM_W]
        u = _gelu(pu)
        v = _gelu(pv)
        r = _rms(v)
        vhat = v * r
        gain = gain_ref[...]
        vn = (vhat * gain).astype(BF16)
        gu_grad = _gelu_grad(pu)
        for g in range(GM_GROUPS):
            wg = jnp.where(mask, w_ref[g], 0.0).astype(BF16)
            cols = slice(g * GM_P, (g + 1) * GM_P)
            dw_acc = jnp.zeros((GM_P, GM_P), F32)
            db_acc = jnp.zeros((GM_P, 1), F32)
            for n in range(tm // GM_P):
                rows = slice(n * GM_P, (n + 1) * GM_P)
                dyb = dy_ref[rows, cols].astype(F32)
                vnb = vn[rows, cols]
                mixed = _dot(wg, vnb) + b_ref[:, cols]
                dmixed = dyb * u[rows, cols]
                dmb = dmixed.astype(BF16)
                dp_ref[rows, cols] = (dyb * mixed * gu_grad[rows, cols]).astype(BF16)
                dw_acc = dw_acc + _dot(dmb, vnb, NT)
                db_acc = db_acc + jnp.sum(dmixed, axis=1, keepdims=True)
                dvn_ref[rows, cols] = _dot(wg, dmb, TN)
            dw_ref[g] += jnp.where(mask, dw_acc, 0.0)
            db_ref[g] += jnp.broadcast_to(db_acc, (GM_P, GM_P))
        dvn = dvn_ref[...]
        dgain_ref[...] += jnp.sum(dvn * vhat, axis=0, keepdims=True)
        dvhat = dvn * gain
        dv = r * (dvhat - vhat * jnp.mean(dvhat * vhat, axis=-1, keepdims=True))
        dp_ref[:, GM_W:] = (dv * _gelu_grad(pv)).astype(BF16)

    const3 = lambda i: (0, 0, 0)
    return pl.pallas_call(
        body, name=name, grid=(s // tm,),
        in_specs=[pl.BlockSpec((tm, pw), lambda i: (i, 0)), pl.BlockSpec((tm, dw_total), lambda i: (i, 0)),
                  pl.BlockSpec((1, GM_W), lambda i: (0, 0)), pl.BlockSpec((GM_GROUPS, GM_P, GM_P), const3),
                  pl.BlockSpec((GM_P, GM_W), lambda i: (0, 0))],
        out_specs=[pl.BlockSpec((tm, 2 * GM_W), lambda i: (i, 0)), pl.BlockSpec((GM_GROUPS, GM_P, GM_P), const3),
                   pl.BlockSpec((GM_GROUPS, GM_P, GM_P), const3), pl.BlockSpec((1, GM_W), lambda i: (0, 0))],
        out_shape=[jax.ShapeDtypeStruct((s, 2 * GM_W), BF16), jax.ShapeDtypeStruct((GM_GROUPS, GM_P, GM_P), F32),
                   jax.ShapeDtypeStruct((GM_GROUPS, GM_P, GM_P), F32), jax.ShapeDtypeStruct((1, GM_W), F32)],
        scratch_shapes=[pltpu.VMEM((tm, GM_W), F32)],
        compiler_params=_params("arbitrary"),
    )(proj, dy, gain, w_s, bias)


def _keep(mask, xb):
    return jnp.where(mask, xb.astype(F32), 0.0).astype(BF16)


def _head_masks(rows, width, heads):
    lane = lax.broadcasted_iota(jnp.int32, (rows, width), 1)
    return [(lane >= HEAD_DIM * h) & (lane < HEAD_DIM * (h + 1)) for h in range(heads)]


def _mem_probs(qh, k):
    sc = _dot(qh, k, NT) * QK_SCALE
    e = jnp.exp(sc - jnp.max(sc, axis=-1, keepdims=True))
    return e / jnp.sum(e, axis=-1, keepdims=True)


def mem_fwd(proj, q_blk, mem_kv, layer, name):
    s = proj.shape[0]
    n_mem = mem_kv.shape[0]
    tm = _tile(s, 512, 16)

    def body(q_ref, k_ref, v_ref, o_ref):
        q = q_ref[...].astype(BF16)
        k, v = k_ref[...], v_ref[...]
        out = jnp.zeros((tm, MEM_W), F32)
        for hm in _head_masks(tm, MEM_W, MEM_HEADS):
            p = _mem_probs(_keep(hm, q), k)
            out = out + jnp.where(hm, _dot(p.astype(BF16), v), 0.0)
        o_ref[...] = out.astype(BF16)

    return pl.pallas_call(
        body, name=name, grid=(s // tm,),
        in_specs=[pl.BlockSpec((tm, MEM_W), lambda i: (i, q_blk)), pl.BlockSpec((n_mem, MEM_W), lambda i: (0, 2 * layer)),
                  pl.BlockSpec((n_mem, MEM_W), lambda i: (0, 2 * layer + 1))],
        out_specs=pl.BlockSpec((tm, MEM_W), lambda i: (i, 0)),
        out_shape=jax.ShapeDtypeStruct((s, MEM_W), BF16), compiler_params=_params("arbitrary"),
    )(proj, mem_kv, mem_kv)


def mem_bwd(proj, q_blk, mem_kv, layer, dy, dy_blk, name):
    s = proj.shape[0]
    n_mem = mem_kv.shape[0]
    tm = _tile(s, 512, 16)

    def body(q_ref, k_ref, v_ref, dy_ref, dq_ref, dk_ref, dv_ref):
        @pl.when(pl.program_id(0) == 0)
        def _():
            dk_ref[...] = jnp.zeros_like(dk_ref)
            dv_ref[...] = jnp.zeros_like(dv_ref)

        q = q_ref[...].astype(BF16)
        k, v = k_ref[...], v_ref[...]
        dy = dy_ref[...]
        dq = jnp.zeros((tm, MEM_W), F32)
        dk = jnp.zeros((n_mem, MEM_W), F32)
        dv = jnp.zeros((n_mem, MEM_W), F32)
        for hm in _head_masks(tm, MEM_W, MEM_HEADS):
            qh = _keep(hm, q)
            dyh = _keep(hm, dy)
            p = _mem_probs(qh, k)
            dp = _dot(dyh, v, NT)
            dv = dv + _dot(p.astype(BF16), dyh, TN)
            ds = (p * (dp - jnp.sum(dp * p, axis=-1, keepdims=True)) * QK_SCALE).astype(BF16)
            dq = dq + jnp.where(hm, _dot(ds, k), 0.0)
            dk = dk + _dot(ds, qh, TN)
        dq_ref[...] = dq.astype(BF16)
        dk_ref[...] += dk
        dv_ref[...] += dv

    const = lambda i: (0, 0)
    return pl.pallas_call(
        body, name=name, grid=(s // tm,),
        in_specs=[pl.BlockSpec((tm, MEM_W), lambda i: (i, q_blk)), pl.BlockSpec((n_mem, MEM_W), lambda i: (0, 2 * layer)),
                  pl.BlockSpec((n_mem, MEM_W), lambda i: (0, 2 * layer + 1)), pl.BlockSpec((tm, MEM_W), lambda i: (i, dy_blk))],
        out_specs=[pl.BlockSpec((tm, MEM_W), lambda i: (i, 0)), pl.BlockSpec((n_mem, MEM_W), const),
                   pl.BlockSpec((n_mem, MEM_W), const)],
        out_shape=[jax.ShapeDtypeStruct((s, MEM_W), BF16), jax.ShapeDtypeStruct((n_mem, MEM_W), F32),
                   jax.ShapeDtypeStruct((n_mem, MEM_W), F32)],
        compiler_params=_params("arbitrary"),
    )(proj, mem_kv, mem_kv, dy)


SB_KEYS = 512
SB_SUB = SB_KEYS // SB_BLK
SB_SHIFT = SB_SUB.bit_length() - 1


def _split(xf):
    hi = xf.astype(BF16)
    return hi, (xf - hi.astype(F32)).astype(BF16)


def _sb_consts():
    row = lax.bitwise_and(lax.broadcasted_iota(jnp.int32, (2 * SB_BLK, 2 * SB_BLK), 0), SB_BLK - 1)
    col = lax.broadcasted_iota(jnp.int32, (2 * SB_BLK, 2 * SB_BLK), 1)
    ones = col >= SB_BLK
    after2 = jnp.where(ones | (row > col), 1.0, 0.0).astype(BF16)
    from2 = jnp.where(ones | (row >= col), 1.0, 0.0).astype(BF16)
    r = lax.broadcasted_iota(jnp.int32, (SB_BLK, SB_BLK), 0)
    c = lax.broadcasted_iota(jnp.int32, (SB_BLK, SB_BLK), 1)
    return after2, from2, c - r, [c < HEAD_DIM, c >= HEAD_DIM]


def _suffix(xf, tri2):
    hi, lo = _split(xf)
    r = _dot(jnp.concatenate([hi, lo], axis=1), tri2)
    return r[:, :SB_BLK], r[:, SB_BLK:]


def _sb_weights(zc, after2, run, mask):
    e = jnp.exp(-jnp.abs(zc))
    l1 = jnp.log(1.0 + e)
    log_one_minus = jnp.minimum(-zc, 0.0) - l1
    log_beta = jnp.minimum(zc, 0.0) - l1
    if mask is not None:
        log_one_minus = jnp.where(mask, log_one_minus, 0.0)
    inner, total = _suffix(log_one_minus, after2)
    a = jnp.exp(log_beta + inner + run)
    if mask is not None:
        a = jnp.where(mask, a, 0.0)
    return e, a, run + total


def _sb_walk(i, block, state):
    own = lax.shift_right_logical(i, SB_SHIFT)
    state = block(own, state, True)
    return lax.fori_loop(0, own, lambda t, st: block(own - 1 - t, st, False), state)


def sb_fwd(proj, kv, name):
    s = proj.shape[0]
    assert s % SB_KEYS == 0

    def body(q_ref, k_ref, v_ref, o_ref):
        i = pl.program_id(1)
        after2, _, col_minus_row, heads = _sb_consts()
        q = q_ref[...].astype(F32) * QK_SCALE
        qhs = [jnp.where(hm, q, 0.0).astype(BF16) for hm in heads]
        first_row = lax.bitwise_and(i, SB_SUB - 1) * SB_BLK

        def block(j, state, masked):
            rows = pl.ds(pl.multiple_of(j * SB_KEYS, SB_KEYS), SB_KEYS)
            kb, vb = k_ref[rows, :], v_ref[rows, :]
            new = []
            for h in range(2):
                run, acc = state[h]
                z = _dot(qhs[h], kb, NT)
                parts = [None] * SB_SUB
                for c in reversed(range(SB_SUB)):
                    mask = (col_minus_row < first_row - c * SB_BLK) if masked else None
                    _, parts[c], run = _sb_weights(z[:, c * SB_BLK:(c + 1) * SB_BLK], after2, run, mask)
                hi, lo = _split(jnp.concatenate(parts, axis=1))
                new.append((run, acc + _dot(hi, vb) + _dot(lo, vb)))
            return tuple(new)

        zero = jnp.zeros((SB_BLK, LANES), F32)
        state = _sb_walk(i, block, ((zero, zero), (zero, zero)))
        o_ref[...] = jnp.where(heads[0], state[0][1], state[1][1])

    pairs = SB_W // LANES
    return pl.pallas_call(
        body, name=name, grid=(pairs, s // SB_BLK),
        in_specs=[pl.BlockSpec((SB_BLK, LANES), lambda p, i: (i, p)), pl.BlockSpec((s, LANES), lambda p, i: (0, p)),
                  pl.BlockSpec((s, LANES), lambda p, i: (0, pairs + p))],
        out_specs=pl.BlockSpec((SB_BLK, LANES), lambda p, i: (i, p)),
        out_shape=jax.ShapeDtypeStruct((s, SB_W), F32),
        compiler_params=_params("arbitrary", "arbitrary"),
    )(proj, kv, kv)


def sb_bwd(proj, kv, out, dy, name):
    s = proj.shape[0]

    def body(q_ref, k_ref, v_ref, o_ref, do_ref, dq_ref, dk_ref, dv_ref):
        i = pl.program_id(1)

        @pl.when(i == 0)
        def _():
            dk_ref[...] = jnp.zeros_like(dk_ref)
            dv_ref[...] = jnp.zeros_like(dv_ref)

        after2, from2, col_minus_row, heads = _sb_consts()
        q = q_ref[...].astype(F32) * QK_SCALE
        d_out = do_ref[...].astype(F32)
        prod = d_out * o_ref[...]
        qhs = [jnp.where(hm, q, 0.0).astype(BF16) for hm in heads]
        dohs = [jnp.where(hm, d_out, 0.0).astype(BF16) for hm in heads]
        totals = [jnp.broadcast_to(jnp.sum(jnp.where(hm, prod, 0.0), axis=1, keepdims=True), (SB_BLK, SB_BLK))
                  for hm in heads]
        q2 = jnp.concatenate(qhs, axis=0)
        do2 = jnp.concatenate(dohs, axis=0)
        first_row = lax.bitwise_and(i, SB_SUB - 1) * SB_BLK

        def block(j, state, masked):
            rows = pl.ds(pl.multiple_of(j * SB_KEYS, SB_KEYS), SB_KEYS)
            kb, vb = k_ref[rows, :], v_ref[rows, :]
            new, a_all, dz_all = [], [], []
            for h in range(2):
                run, seen, dq = state[h]
                z = _dot(qhs[h], kb, NT)
                da = _dot(dohs[h], vb, NT)
                a_parts, dz_parts = [None] * SB_SUB, [None] * SB_SUB
                for c in reversed(range(SB_SUB)):
                    cols = slice(c * SB_BLK, (c + 1) * SB_BLK)
                    mask = (col_minus_row < first_row - c * SB_BLK) if masked else None
                    zc = z[:, cols]
                    e, a, run = _sb_weights(zc, after2, run, mask)
                    dl = a * da[:, cols]
                    inner, total = _suffix(dl, from2)
                    d_lom = totals[h] - (inner + seen)
                    if masked:
                        d_lom = jnp.where(mask, d_lom, 0.0)
                    seen = seen + total
                    rinv = 1.0 / (1.0 + e)
                    small = e * rinv
                    pos = zc >= 0.0
                    dz = dl * jnp.where(pos, small, rinv) - d_lom * jnp.where(pos, rinv, small)
                    a_parts[c], dz_parts[c] = a.astype(BF16), dz.astype(BF16)
                a_all.append(jnp.concatenate(a_parts, axis=1))
                dz_all.append(jnp.concatenate(dz_parts, axis=1))
                new.append((run, seen, dq + _dot(dz_all[h], kb)))
            dv_ref[rows, :] += _dot(jnp.concatenate(a_all, axis=0), do2, TN)
            dk_ref[rows, :] += _dot(jnp.concatenate(dz_all, axis=0), q2, TN)
            return tuple(new)

        zero = jnp.zeros((SB_BLK, LANES), F32)
        state = _sb_walk(i, block, ((zero, zero, zero), (zero, zero, zero)))
        dq_ref[...] = (jnp.where(heads[0], state[0][2], state[1][2]) * QK_SCALE).astype(BF16)

    pairs = SB_W // LANES
    blk = lambda p, i: (i, p)
    col = lambda p, i: (0, p)
    return pl.pallas_call(
        body, name=name, grid=(pairs, s // SB_BLK),
        in_specs=[pl.BlockSpec((SB_BLK, LANES), blk), pl.BlockSpec((s, LANES), col),
                  pl.BlockSpec((s, LANES), lambda p, i: (0, pairs + p)), pl.BlockSpec((SB_BLK, LANES), blk),
                  pl.BlockSpec((SB_BLK, LANES), blk)],
        out_specs=[pl.BlockSpec((SB_BLK, LANES), blk), pl.BlockSpec((s, LANES), col), pl.BlockSpec((s, LANES), col)],
        out_shape=[jax.ShapeDtypeStruct((s, SB_W), BF16), jax.ShapeDtypeStruct((s, SB_W), F32),
                   jax.ShapeDtypeStruct((s, SB_W), F32)],
        compiler_params=_params("arbitrary", "arbitrary"),
    )(proj, kv, kv, out, dy)


def final_loss(x, g, target, name):
    s, d = x.shape
    tm = _tile(s, 256, 8)

    def body(x_ref, g_ref, t_ref, loss_ref, dx_ref, dg_ref):
        @pl.when(pl.program_id(0) == 0)
        def _():
            loss_ref[...] = jnp.zeros_like(loss_ref)
            dg_ref[...] = jnp.zeros_like(dg_ref)

        xf = x_ref[...]
        r = _rms(xf)
        xhat = xf * r
        gain = g_ref[...]
        diff = xhat * gain - t_ref[...]
        sq = jnp.sum(jnp.sum(diff * diff, axis=1, keepdims=True), axis=0, keepdims=True)
        loss_ref[...] += jnp.broadcast_to(sq, loss_ref.shape)
        dy = diff * (1.0 / d)
        dg_ref[...] += jnp.sum(dy * xhat, axis=0, keepdims=True)
        dxhat = dy * gain
        dx_ref[...] = r * (dxhat - xhat * jnp.mean(dxhat * xhat, axis=-1, keepdims=True))

    row = lambda i: (i, 0)
    const = lambda i: (0, 0)
    return pl.pallas_call(
        body, name=name, grid=(s // tm,),
        in_specs=[pl.BlockSpec((tm, d), row), pl.BlockSpec((1, d), const), pl.BlockSpec((tm, d), row)],
        out_specs=[pl.BlockSpec((8, LANES), const), pl.BlockSpec((tm, d), row), pl.BlockSpec((1, d), const)],
        out_shape=[jax.ShapeDtypeStruct((8, LANES), F32), jax.ShapeDtypeStruct((s, d), F32), jax.ShapeDtypeStruct((1, d), F32)],
        compiler_params=_params("arbitrary"),
    )(x, g, target)


def adamw(w, parts, m, v, name):
    rows, cols = w.shape
    k = parts.shape[0]
    tr = _tile(rows, 512, 16)
    c1, c2 = 1.0 - ADAM_B1 ** ADAM_STEP, 1.0 - ADAM_B2 ** ADAM_STEP

    def body(w_ref, p_ref, m_ref, v_ref, g_ref, d_ref, nm_ref, nv_ref):
        grad = p_ref[0].astype(F32)
        for s in range(1, k):
            grad = grad + p_ref[s].astype(F32)
        nm = ADAM_B1 * m_ref[...] + (1.0 - ADAM_B1) * grad
        nv = ADAM_B2 * v_ref[...] + (1.0 - ADAM_B2) * (grad * grad)
        g_ref[...] = grad
        d_ref[...] = -ADAM_LR * ((nm / c1) / (jnp.sqrt(nv / c2) + ADAM_EPS) + ADAM_WD * w_ref[...])
        nm_ref[...] = nm
        nv_ref[...] = nv

    spec = pl.BlockSpec((tr, cols), lambda i: (i, 0))
    shape = jax.ShapeDtypeStruct((rows, cols), F32)
    return pl.pallas_call(
        body, name=name, grid=(rows // tr,),
        in_specs=[spec, pl.BlockSpec((k, tr, cols), lambda i: (0, i, 0)), spec, spec],
        out_specs=[spec] * 4, out_shape=[shape] * 4,
        compiler_params=_params("arbitrary"),
    )(w, parts, m, v)


SHARDED = {"ffn1_w_gate": 2, "ffn1_w_up": 2, "ffn1_w_down": 1, "ffn2_w_gate": 2, "ffn2_w_up": 2, "ffn2_w_down": 1,
           "w_mem_kv": 1, "a_w_in": 2, "a_w_out": 1, "w_kv": 1, "b_w_in": 1, "b_w_out": 1}
SMALL = ["ffn1_norm", "mix_norm", "ffn2_norm", "mem_norm", "kv_norm", "final_norm", "a_v_norm", "a_w_spatial", "a_b_spatial"]
WEIGHTS = ["ffn1_norm", "ffn1_w_gate", "ffn1_w_up", "ffn1_w_down", "mix_norm", "ffn2_norm", "ffn2_w_gate", "ffn2_w_up",
           "ffn2_w_down", "mem_norm", "w_mem_kv", "a_w_in", "a_v_norm", "a_w_spatial", "a_b_spatial", "a_w_out", "kv_norm",
           "w_kv", "b_w_in", "b_w_out", "final_norm"]


def _gather_weights(shards):
    by_dev = exchange([shards[n].astype(BF16) for n in SHARDED], "all", True, "gather_weights")
    return {n: jnp.concatenate([blocks[d] for d in range(N_DEV)], axis=axis)
            for (n, axis), blocks in zip(SHARDED.items(), by_dev)}


def _scatter_grads(grads):
    by_dev = [jnp.stack(jnp.split(grads[n].astype(BF16), N_DEV, axis=axis)) for n, axis in SHARDED.items()]
    return dict(zip(SHARDED, exchange(by_dev, "all", False, "scatter_grads")))


def _all_sum(parts, name):
    flat = jnp.concatenate([p.reshape(-1) for p in parts])
    pad = (-flat.size) % (16 * LANES)
    buf = jnp.pad(flat, (0, pad)).reshape(-1, LANES)
    total = sum_leading(exchange([buf], "all", True, name)[0], F32, name + "_sum").reshape(-1)
    out, off = [], 0
    for p in parts:
        out.append(total[off:off + p.size].reshape(p.shape))
        off += p.size
    return out


def _device_index():
    return 4 * lax.axis_index("x") + 2 * lax.axis_index("y") + lax.axis_index("c")


def kernel(x, mem, ffn1_norm, ffn1_w_gate, ffn1_w_up, ffn1_w_down, mix_norm, ffn2_norm, ffn2_w_gate, ffn2_w_up, ffn2_w_down, mem_norm, w_mem_kv, a_w_in, a_v_norm, a_w_spatial, a_b_spatial, a_w_out, kv_norm, w_kv, b_w_in, b_w_out, final_norm, loss_target, m_ffn1_norm, m_ffn1_w_gate, m_ffn1_w_up, m_ffn1_w_down, m_mix_norm, m_ffn2_norm, m_ffn2_w_gate, m_ffn2_w_up, m_ffn2_w_down, m_mem_norm, m_w_mem_kv, m_a_w_in, m_a_v_norm, m_a_w_spatial, m_a_b_spatial, m_a_w_out, m_kv_norm, m_w_kv, m_b_w_in, m_b_w_out, m_final_norm, v_ffn1_norm, v_ffn1_w_gate, v_ffn1_w_up, v_ffn1_w_down, v_mix_norm, v_ffn2_norm, v_ffn2_w_gate, v_ffn2_w_up, v_ffn2_w_down, v_mem_norm, v_w_mem_kv, v_a_w_in, v_a_v_norm, v_a_w_spatial, v_a_b_spatial, v_a_w_out, v_kv_norm, v_w_kv, v_b_w_in, v_b_w_out, v_final_norm):
    weights = dict(ffn1_norm=ffn1_norm, ffn1_w_gate=ffn1_w_gate, ffn1_w_up=ffn1_w_up, ffn1_w_down=ffn1_w_down, mix_norm=mix_norm, ffn2_norm=ffn2_norm, ffn2_w_gate=ffn2_w_gate, ffn2_w_up=ffn2_w_up, ffn2_w_down=ffn2_w_down, mem_norm=mem_norm, w_mem_kv=w_mem_kv, a_w_in=a_w_in, a_v_norm=a_v_norm, a_w_spatial=a_w_spatial, a_b_spatial=a_b_spatial, a_w_out=a_w_out, kv_norm=kv_norm, w_kv=w_kv, b_w_in=b_w_in, b_w_out=b_w_out, final_norm=final_norm)
    mom1 = dict(ffn1_norm=m_ffn1_norm, ffn1_w_gate=m_ffn1_w_gate, ffn1_w_up=m_ffn1_w_up, ffn1_w_down=m_ffn1_w_down, mix_norm=m_mix_norm, ffn2_norm=m_ffn2_norm, ffn2_w_gate=m_ffn2_w_gate, ffn2_w_up=m_ffn2_w_up, ffn2_w_down=m_ffn2_w_down, mem_norm=m_mem_norm, w_mem_kv=m_w_mem_kv, a_w_in=m_a_w_in, a_v_norm=m_a_v_norm, a_w_spatial=m_a_w_spatial, a_b_spatial=m_a_b_spatial, a_w_out=m_a_w_out, kv_norm=m_kv_norm, w_kv=m_w_kv, b_w_in=m_b_w_in, b_w_out=m_b_w_out, final_norm=m_final_norm)
    mom2 = dict(ffn1_norm=v_ffn1_norm, ffn1_w_gate=v_ffn1_w_gate, ffn1_w_up=v_ffn1_w_up, ffn1_w_down=v_ffn1_w_down, mix_norm=v_mix_norm, ffn2_norm=v_ffn2_norm, ffn2_w_gate=v_ffn2_w_gate, ffn2_w_up=v_ffn2_w_up, ffn2_w_down=v_ffn2_w_down, mem_norm=v_mem_norm, w_mem_kv=v_w_mem_kv, a_w_in=v_a_w_in, a_v_norm=v_a_v_norm, a_w_spatial=v_a_w_spatial, a_b_spatial=v_a_b_spatial, a_w_out=v_a_w_out, kv_norm=v_kv_norm, w_kv=v_w_kv, b_w_in=v_b_w_in, b_w_out=v_b_w_out, final_norm=v_final_norm)

    dev = _device_index()
    xs, mem_in, target = x[0], mem[0], loss_target[0]
    d_model = xs.shape[1]
    shards = {n: weights[n] for n in SHARDED}
    full = _gather_weights(shards)
    vn_width = a_v_norm.shape[1]
    a_v_full = _all_sum([lax.dynamic_update_slice(jnp.zeros((N_A, N_DEV * vn_width), F32), a_v_norm, (0, dev * vn_width))],
                        "gather_v_norm")[0]

    row = lambda v: v.reshape(1, -1)
    w_gu = {f: jnp.concatenate([full[f + "_w_gate"], full[f + "_w_up"]], axis=2) for f in ("ffn1", "ffn2")}
    w_mem_cat = full["w_mem_kv"].transpose(1, 0, 2).reshape(d_model, -1)
    bias = [jnp.repeat(a_b_spatial[i].T, GM_P, axis=1) for i in range(N_A)]

    mem_kv, mem_h = norm_mm(mem_in, row(mem_norm), w_mem_cat, BF16, "mem_kv", emit_h=True)

    def ffn_fwd(xin, f, l):
        gu = norm_mm(xin, row(weights[f + "_norm"][l]), w_gu[f][l], BF16, "ffn_gu")
        act = swiglu_fwd(gu, "ffn_act")
        return mm_res(act, full[f + "_w_down"][l], xin, 0.5, "ffn_down"), gu, act

    saved = []
    kv = x_kv = None
    cur = xs
    for l in range(DEPTH):
        st = {"x0": cur}
        if l == N_A:
            x_kv = cur
            kv = norm_mm(cur, row(kv_norm), full["w_kv"], BF16, "kv_proj")
        st["x1"], st["gu1"], st["act1"] = ffn_fwd(cur, "ffn1", l)
        if l < N_A:
            proj = norm_mm(st["x1"], row(mix_norm[l]), full["a_w_in"][l], F32, "a_proj")
            y_tok = gmlp_fwd(proj, row(a_v_full[l]), a_w_spatial[l], bias[l], "gmlp_fwd")
            y_mem = mem_fwd(proj, 2 * GM_W // MEM_W, mem_kv, l, "mem_fwd_a")
            w_out = full["a_w_out"][l]
        else:
            proj = norm_mm(st["x1"], row(mix_norm[l]), full["b_w_in"][l - N_A], BF16, "b_proj")
            st["sb_out"] = sb_fwd(proj, kv, "sb_fwd")
            y_tok = st["sb_out"].astype(BF16)
            y_mem = mem_fwd(proj, SB_W // MEM_W, mem_kv, l, "mem_fwd_b")
            w_out = full["b_w_out"][l - N_A]
        st["proj"] = proj
        st["y"] = jnp.concatenate([y_tok, y_mem], axis=1)
        st["x2"] = mm_res(st["y"], w_out, st["x1"], 1.0, "mix_out")
        cur, st["gu2"], st["act2"] = ffn_fwd(st["x2"], "ffn2", l)
        saved.append(st)

    loss_blk, dx, d_final = final_loss(cur, row(final_norm), target, "final_loss")
    loss = lax.psum(loss_blk[0, 0] * (0.5 / d_model), AXES)

    grads = {n: [None] * weights[n].shape[0] for n in WEIGHTS if weights[n].ndim >= 2 and n not in ("w_kv",)}
    grads["final_norm"] = d_final.reshape(-1)
    d_mem_kv = [None] * DEPTH
    d_kv = []

    def ffn_bwd(dx, xin, gu, act, f, l):
        d_act = mm_nt(dx, full[f + "_w_down"][l], 0.5, "ffn_dact")
        d_gu = swiglu_bwd(gu, d_act, "ffn_dgu")
        dx_new, d_gain, h = mm_nt_normbwd(d_gu, w_gu[f][l], xin, row(weights[f + "_norm"][l]), dx, "ffn_dx")
        d_wgu = mm_tn(h, d_gu, 1.0, "ffn_dwgu", tb_target=1408)
        half = d_wgu.shape[1] // 2
        grads[f + "_w_gate"][l], grads[f + "_w_up"][l] = d_wgu[:, :half], d_wgu[:, half:]
        grads[f + "_w_down"][l] = mm_tn(act, dx, 0.5, "ffn_dwdown", ta_target=1408, tb_target=1024)
        grads[f + "_norm"][l] = d_gain.reshape(-1)
        return dx_new

    for l in reversed(range(DEPTH)):
        st = saved[l]
        dx = ffn_bwd(dx, st["x2"], st["gu2"], st["act2"], "ffn2", l)
        proj = st["proj"]
        if l < N_A:
            w_in, w_out, key_in, key_out, idx = full["a_w_in"][l], full["a_w_out"][l], "a_w_in", "a_w_out", l
        else:
            w_in, w_out, key_in, key_out, idx = full["b_w_in"][l - N_A], full["b_w_out"][l - N_A], "b_w_in", "b_w_out", l - N_A
        dy = mm_nt(dx, w_out, 1.0, "mix_dy")
        grads[key_out][idx] = mm_tn(st["y"], dx, 1.0, "mix_dwout", tb_target=1024)
        if l < N_A:
            d_uv, d_ws, d_bs, d_vgain = gmlp_bwd(proj, dy, row(a_v_full[l]), a_w_spatial[l], bias[l], "gmlp_bwd")
            grads["a_w_spatial"][l], grads["a_b_spatial"][l], grads["a_v_norm"][l] = d_ws, d_bs[:, :, 0], d_vgain.reshape(-1)
            d_q, d_k, d_v = mem_bwd(proj, 2 * GM_W // MEM_W, mem_kv, l, dy, GM_W // MEM_W, "mem_bwd_a")
            d_proj = jnp.concatenate([d_uv, d_q], axis=1)
        else:
            d_qsb, d_ksb, d_vsb = sb_bwd(proj, kv, st["sb_out"], dy, "sb_bwd")
            d_kv.append(jnp.concatenate([d_ksb, d_vsb], axis=1))
            d_q, d_k, d_v = mem_bwd(proj, SB_W // MEM_W, mem_kv, l, dy, SB_W // MEM_W, "mem_bwd_b")
            d_proj = jnp.concatenate([d_qsb, d_q], axis=1)
        d_mem_kv[l] = jnp.concatenate([d_k, d_v], axis=1)
        dx, d_gain, h = mm_nt_normbwd(d_proj, w_in, st["x1"], row(mix_norm[l]), dx, "mix_dx")
        grads["mix_norm"][l] = d_gain.reshape(-1)
        grads[key_in][idx] = mm_tn(h, d_proj, 1.0, "mix_dwin")
        dx = ffn_bwd(dx, st["x0"], st["gu1"], st["act1"], "ffn1", l)
        if l == N_A:
            d_kv_b = sum_leading(jnp.stack(d_kv), BF16, "kv_dsum")
            dx, d_gain, h = mm_nt_normbwd(d_kv_b, full["w_kv"], x_kv, row(kv_norm), dx, "kv_dx")
            grads["kv_norm"] = d_gain.reshape(-1)
            grads["w_kv"] = mm_tn(h, d_kv_b, 1.0, "kv_dw")

    d_mem_all = jnp.concatenate(d_mem_kv, axis=1).astype(BF16)
    _, d_gain, _ = mm_nt_normbwd(d_mem_all, w_mem_cat, mem_in, row(mem_norm), None, "mem_dnorm")
    grads["mem_norm"] = d_gain.reshape(-1)
    d_wmem = mm_tn(mem_h, d_mem_all, 1.0, "mem_dw")
    grads["w_mem_kv"] = d_wmem.reshape(d_model, DEPTH, -1).transpose(1, 0, 2)
    grads = {n: (jnp.stack(g) if isinstance(g, list) else g) for n, g in grads.items()}

    parts = _scatter_grads({n: grads[n] for n in SHARDED})
    for n, g in zip(SMALL, _all_sum([grads[n] for n in SMALL], "sum_small")):
        parts[n] = g[None]
    parts["a_v_norm"] = lax.dynamic_slice(parts["a_v_norm"], (0, 0, dev * vn_width), (1,) + a_v_norm.shape)

    reduced, deltas, new_m, new_v = {}, {}, {}, {}
    for n in WEIGHTS:
        w = weights[n]
        view = (lambda a: a.reshape(-1, a.shape[-1]))
        res = adamw(view(w), parts[n].reshape(parts[n].shape[0], -1, w.shape[-1]), view(mom1[n]), view(mom2[n]), "adamw")
        reduced[n], deltas[n], new_m[n], new_v[n] = [r.reshape(w.shape) for r in res]

    return (loss, dx[None], *[reduced[n] for n in WEIGHTS], *[deltas[n] for n in WEIGHTS],
            *[new_m[n] for n in WEIGHTS], *[new_v[n] for n in WEIGHTS])
```

```python
import functools

import jax
import jax.numpy as jnp
from jax import lax
from jax.experimental import pallas as pl
from jax.experimental.pallas import tpu as pltpu

F32, BF16 = jnp.float32, jnp.bfloat16
MESH_ID = pl.DeviceIdType.MESH
AXES = ("x", "y", "c")
N_DEV = 8

EPS = 1e-6
DEPTH, N_A = 4, 2
GM_W, GM_GROUPS, GM_P = 768, 6, 128
MEM_W, MEM_HEADS, HEAD_DIM = 256, 4, 64
SB_W, SB_BLK = 768, 128
LANES = 128
QK_SCALE = HEAD_DIM ** -0.5
GELU_C, GELU_A = 0.7978845608028654, 0.044715

ADAM_LR, ADAM_B1, ADAM_B2, ADAM_EPS, ADAM_WD, ADAM_STEP = 0.001, 0.9, 0.999, 1e-08, 0.01, 10

VMEM_LIMIT = 56 * 1024 * 1024
PACK_COLS = 512

NT = (((1,), (1,)), ((), ()))
TN = (((0,), (0,)), ((), ()))


def _params(*sem):
    return pltpu.CompilerParams(dimension_semantics=sem, vmem_limit_bytes=VMEM_LIMIT)


def _tile(n, target, mult=LANES):
    best = None
    for t in range(mult, min(n, target) + 1, mult):
        if n % t == 0:
            best = t
    return best if best is not None else n


def _dot(a, b, dims=None):
    if dims is None:
        return jnp.dot(a, b, preferred_element_type=F32)
    return lax.dot_general(a, b, dims, preferred_element_type=F32)


def exchange(srcs, group, same_src, name, split=False):
    size = {"pair": 2, "quad": 4, "all": 8}[group]
    n = len(srcs)
    chunk_shapes = [tuple(s.shape) if same_src else tuple(s.shape[1:]) for s in srcs]
    pieces = [cs[0] if split else 1 for cs in chunk_shapes]
    n_dma = sum(pieces)

    def body(*refs):
        src_refs, out_refs = refs[:n], refs[n:2 * n]
        send_sems, recv_sems, local_sems = refs[2 * n:]
        x, y, c = lax.axis_index("x"), lax.axis_index("y"), lax.axis_index("c")
        if group == "pair":
            me, dev = c, lambda p: (x, y, p)
        elif group == "quad":
            me, dev = 2 * x + y, lambda p: (p // 2, p % 2, c)
        else:
            me, dev = 4 * x + 2 * y + c, lambda p: (p // 4, (p // 2) % 2, p % 2)

        def chunk(t, idx):
            return src_refs[t] if same_src else src_refs[t].at[idx]

        def copies(k, idx, slot, peer):
            out, w = [], k * n_dma
            for t in range(n):
                src, dst = chunk(t, idx), out_refs[t].at[slot]
                for s_ref, d_ref in ([(src.at[u], dst.at[u]) for u in range(pieces[t])] if split else [(src, dst)]):
                    out.append(pltpu.make_async_remote_copy(
                        src_ref=s_ref, dst_ref=d_ref, send_sem=send_sems.at[w], recv_sem=recv_sems.at[w],
                        device_id=dev(peer), device_id_type=MESH_ID))
                    w += 1
            return out

        local = [pltpu.make_async_copy(chunk(t, me), out_refs[t].at[me], local_sems.at[t]) for t in range(n)]
        for cp in local:
            cp.start()
        sends = []
        for k in range(1, size):
            peer = (me + k) % size
            sends += copies(k, peer, me, peer)
        for cp in sends:
            cp.start()
        for k in range(1, size):
            sender = (me + size - k) % size
            for cp in copies(k, me, sender, sender):
                cp.wait_recv()
        for cp in sends:
            cp.wait_send()
        for cp in local:
            cp.wait()

    hbm = pl.BlockSpec(memory_space=pltpu.HBM)
    return pl.pallas_call(
        body, name=name,
        out_shape=[jax.ShapeDtypeStruct((size,) + cs, s.dtype) for cs, s in zip(chunk_shapes, srcs)],
        in_specs=[hbm] * n, out_specs=[hbm] * n,
        scratch_shapes=[pltpu.SemaphoreType.DMA((size * n_dma,)), pltpu.SemaphoreType.DMA((size * n_dma,)),
                        pltpu.SemaphoreType.DMA((n,))],
    )(*srcs)


def sum_leading(parts, out_dtype, name):
    k, rows, cols = parts.shape
    tr = _tile(rows, 512, 16)

    def body(p_ref, o_ref):
        acc = p_ref[0].astype(F32)
        for s in range(1, k):
            acc = acc + p_ref[s].astype(F32)
        o_ref[...] = acc.astype(o_ref.dtype)

    return pl.pallas_call(
        body, name=name, grid=(rows // tr,),
        in_specs=[pl.BlockSpec((k, tr, cols), lambda i: (0, i, 0))],
        out_specs=pl.BlockSpec((tr, cols), lambda i: (i, 0)),
        out_shape=jax.ShapeDtypeStruct((rows, cols), out_dtype),
        compiler_params=_params("arbitrary"),
    )(parts)


def _rms(xf):
    return lax.rsqrt(jnp.mean(xf * xf, axis=-1, keepdims=True) + EPS)


def norm_mm(x, g, w, out_dtype, name, emit_h=False):
    m, d = x.shape
    n = w.shape[1]
    tm, tn = _tile(m, 1024, 8), _tile(n, 1408)

    def body(x_ref, g_ref, w_ref, o_ref, *rest):
        h_ref = rest[-1]

        @pl.when(pl.program_id(1) == 0)
        def _():
            xf = x_ref[...]
            hb = ((xf * _rms(xf)) * g_ref[...]).astype(BF16)
            h_ref[...] = hb
            if emit_h:
                rest[0][...] = hb

        o_ref[...] = _dot(h_ref[...], w_ref[...]).astype(o_ref.dtype)

    out_shape = [jax.ShapeDtypeStruct((m, n), out_dtype)]
    out_specs = [pl.BlockSpec((tm, tn), lambda i, j: (i, j))]
    if emit_h:
        out_shape.append(jax.ShapeDtypeStruct((m, d), BF16))
        out_specs.append(pl.BlockSpec((tm, d), lambda i, j: (i, 0)))
    res = pl.pallas_call(
        body, name=name, grid=(m // tm, n // tn),
        in_specs=[pl.BlockSpec((tm, d), lambda i, j: (i, 0)), pl.BlockSpec((1, d), lambda i, j: (0, 0)),
                  pl.BlockSpec((d, tn), lambda i, j: (0, j))],
        out_specs=out_specs, out_shape=out_shape,
        scratch_shapes=[pltpu.VMEM((tm, d), BF16)],
        compiler_params=_params("arbitrary", "arbitrary"),
    )(x, g, w)
    return res if emit_h else res[0]


def mm_res(a, w, res, alpha, name):
    m, k = a.shape
    n = w.shape[1]
    tm, tn = _tile(m, 1024, 8), _tile(n, 1024)

    def body(a_ref, w_ref, r_ref, o_ref):
        o_ref[...] = r_ref[...] + alpha * _dot(a_ref[...], w_ref[...])

    return pl.pallas_call(
        body, name=name, grid=(m // tm, n // tn),
        in_specs=[pl.BlockSpec((tm, k), lambda i, j: (i, 0)), pl.BlockSpec((k, tn), lambda i, j: (0, j)),
                  pl.BlockSpec((tm, tn), lambda i, j: (i, j))],
        out_specs=pl.BlockSpec((tm, tn), lambda i, j: (i, j)),
        out_shape=jax.ShapeDtypeStruct((m, n), F32),
        compiler_params=_params("arbitrary", "arbitrary"),
    )(a, w, res)


def mm_nt(x, w, alpha, name):
    m, d = x.shape
    n = w.shape[0]
    tm, tn = _tile(m, 1024, 8), _tile(n, 1408)

    def body(x_ref, w_ref, o_ref, xb_ref):
        @pl.when(pl.program_id(1) == 0)
        def _():
            xb_ref[...] = x_ref[...].astype(BF16)

        o_ref[...] = (alpha * _dot(xb_ref[...], w_ref[...], NT)).astype(o_ref.dtype)

    return pl.pallas_call(
        body, name=name, grid=(m // tm, n // tn),
        in_specs=[pl.BlockSpec((tm, d), lambda i, j: (i, 0)), pl.BlockSpec((tn, d), lambda i, j: (j, 0))],
        out_specs=pl.BlockSpec((tm, tn), lambda i, j: (i, j)),
        out_shape=jax.ShapeDtypeStruct((m, n), BF16),
        scratch_shapes=[pltpu.VMEM((tm, d), BF16)],
        compiler_params=_params("arbitrary", "arbitrary"),
    )(x, w)


def mm_tn(a, b, alpha, name, ta_target=1024, tb_target=512):
    s, ka = a.shape
    nb = b.shape[1]
    ta, tb, ts = _tile(ka, ta_target), _tile(nb, tb_target), _tile(s, 1024, 16)
    steps = s // ts

    def body(a_ref, b_ref, o_ref, acc_ref):
        t = pl.program_id(2)

        @pl.when(t == 0)
        def _():
            acc_ref[...] = jnp.zeros_like(acc_ref)

        acc_ref[...] += _dot(a_ref[...].astype(BF16), b_ref[...].astype(BF16), TN)

        @pl.when(t == steps - 1)
        def _():
            o_ref[...] = alpha * acc_ref[...]

    return pl.pallas_call(
        body, name=name, grid=(ka // ta, nb // tb, steps),
        in_specs=[pl.BlockSpec((ts, ta), lambda i, j, t: (t, i)), pl.BlockSpec((ts, tb), lambda i, j, t: (t, j))],
        out_specs=pl.BlockSpec((ta, tb), lambda i, j, t: (i, j)),
        out_shape=jax.ShapeDtypeStruct((ka, nb), F32),
        scratch_shapes=[pltpu.VMEM((ta, tb), F32)],
        compiler_params=_params("arbitrary", "arbitrary", "arbitrary"),
    )(a, b)


def mm_nt_normbwd(dy, w, x, g, res, name):
    m, n = dy.shape
    d = w.shape[0]
    tm, tk = _tile(m, 1024, 8), _tile(n, 1408)
    steps = n // tk
    has_res = res is not None

    def body(*refs):
        if has_res:
            dy_ref, w_ref, x_ref, g_ref, r_ref, dx_ref, dg_ref, h_ref, acc_ref = refs
        else:
            dy_ref, w_ref, x_ref, g_ref, dx_ref, dg_ref, h_ref, acc_ref = refs
        i, t = pl.program_id(0), pl.program_id(1)

        @pl.when(t == 0)
        def _():
            acc_ref[...] = jnp.zeros_like(acc_ref)

        @pl.when((t == 0) & (i == 0))
        def _():
            dg_ref[...] = jnp.zeros_like(dg_ref)

        acc_ref[...] += _dot(dy_ref[...], w_ref[...], NT)

        @pl.when(t == steps - 1)
        def _():
            xf = x_ref[...]
            r = _rms(xf)
            xhat = xf * r
            dh = acc_ref[...]
            gain = g_ref[...]
            dg_ref[...] += jnp.sum(dh * xhat, axis=0, keepdims=True)
            dxhat = dh * gain
            dx = r * (dxhat - xhat * jnp.mean(dxhat * xhat, axis=-1, keepdims=True))
            dx_ref[...] = (r_ref[...] + dx) if has_res else dx
            h_ref[...] = (xhat * gain).astype(BF16)

    row = lambda i, t: (i, 0)
    in_specs = [pl.BlockSpec((tm, tk), lambda i, t: (i, t)), pl.BlockSpec((d, tk), lambda i, t: (0, t)),
                pl.BlockSpec((tm, d), row), pl.BlockSpec((1, d), lambda i, t: (0, 0))]
    args = [dy, w, x, g]
    if has_res:
        in_specs.append(pl.BlockSpec((tm, d), row))
        args.append(res)
    return pl.pallas_call(
        body, name=name, grid=(m // tm, steps),
        in_specs=in_specs,
        out_specs=[pl.BlockSpec((tm, d), row), pl.BlockSpec((1, d), lambda i, t: (0, 0)), pl.BlockSpec((tm, d), row)],
        out_shape=[jax.ShapeDtypeStruct((m, d), F32), jax.ShapeDtypeStruct((1, d), F32), jax.ShapeDtypeStruct((m, d), BF16)],
        scratch_shapes=[pltpu.VMEM((tm, d), F32)],
        compiler_params=_params("arbitrary", "arbitrary"),
    )(*args)


def _sigmoid(z):
    return 1.0 / (1.0 + jnp.exp(-z))


def swiglu_fwd(gu, name):
    m, f2 = gu.shape
    f = f2 // 2
    tm = _tile(m, 256, 16)

    def body(gu_ref, o_ref):
        gate, up = gu_ref[:, :f].astype(F32), gu_ref[:, f:].astype(F32)
        o_ref[...] = (gate * _sigmoid(gate) * up).astype(BF16)

    return pl.pallas_call(
        body, name=name, grid=(m // tm,),
        in_specs=[pl.BlockSpec((tm, f2), lambda i: (i, 0))], out_specs=pl.BlockSpec((tm, f), lambda i: (i, 0)),
        out_shape=jax.ShapeDtypeStruct((m, f), BF16), compiler_params=_params("arbitrary"),
    )(gu)


def swiglu_bwd(gu, da, name):
    m, f2 = gu.shape
    f = f2 // 2
    tm = _tile(m, 256, 16)

    def body(gu_ref, da_ref, o_ref):
        gate, up = gu_ref[:, :f].astype(F32), gu_ref[:, f:].astype(F32)
        d = da_ref[...].astype(F32)
        sg = _sigmoid(gate)
        o_ref[:, :f] = (d * up * (sg * (1.0 + gate * (1.0 - sg)))).astype(BF16)
        o_ref[:, f:] = (d * (gate * sg)).astype(BF16)

    return pl.pallas_call(
        body, name=name, grid=(m // tm,),
        in_specs=[pl.BlockSpec((tm, f2), lambda i: (i, 0)), pl.BlockSpec((tm, f), lambda i: (i, 0))],
        out_specs=pl.BlockSpec((tm, f2), lambda i: (i, 0)),
        out_shape=jax.ShapeDtypeStruct((m, f2), BF16), compiler_params=_params("arbitrary"),
    )(gu, da)


def _gelu(x):
    return 0.5 * x * (1.0 + jnp.tanh(GELU_C * (x + GELU_A * x * x * x)))


def _gelu_grad(x):
    t = jnp.tanh(GELU_C * (x + GELU_A * x * x * x))
    return 0.5 * (1.0 + t) + 0.5 * x * (1.0 - t * t) * (GELU_C * (1.0 + 3.0 * GELU_A * x * x))


def _chunk_mask():
    row = lax.broadcasted_iota(jnp.int32, (GM_P, GM_P), 0)
    col = lax.broadcasted_iota(jnp.int32, (GM_P, GM_P), 1)
    return (col < GM_P // 2) | (row >= GM_P // 2)


def gmlp_fwd(proj, gain, w_s, bias, name):
    s, pw = proj.shape
    tm = _tile(s, 256, GM_P)

    def body(p_ref, gain_ref, w_ref, b_ref, o_ref):
        mask = _chunk_mask()
        u = _gelu(p_ref[:, :GM_W])
        v = _gelu(p_ref[:, GM_W:2 * GM_W])
        vn = ((v * _rms(v)) * gain_ref[...]).astype(BF16)
        for g in range(GM_GROUPS):
            wg = jnp.where(mask, w_ref[g], 0.0).astype(BF16)
            cols = slice(g * GM_P, (g + 1) * GM_P)
            for n in range(tm // GM_P):
                rows = slice(n * GM_P, (n + 1) * GM_P)
                mixed = _dot(wg, vn[rows, cols]) + b_ref[:, cols]
                o_ref[rows, cols] = (u[rows, cols] * mixed).astype(BF16)

    return pl.pallas_call(
        body, name=name, grid=(s // tm,),
        in_specs=[pl.BlockSpec((tm, pw), lambda i: (i, 0)), pl.BlockSpec((1, GM_W), lambda i: (0, 0)),
                  pl.BlockSpec((GM_GROUPS, GM_P, GM_P), lambda i: (0, 0, 0)), pl.BlockSpec((GM_P, GM_W), lambda i: (0, 0))],
        out_specs=pl.BlockSpec((tm, GM_W), lambda i: (i, 0)),
        out_shape=jax.ShapeDtypeStruct((s, GM_W), BF16), compiler_params=_params("arbitrary"),
    )(proj, gain, w_s, bias)


def gmlp_bwd(proj, dy, gain, w_s, bias, name):
    s, pw = proj.shape
    dw_total = dy.shape[1]
    tm = _tile(s, 256, GM_P)

    def body(p_ref, dy_ref, gain_ref, w_ref, b_ref, dp_ref, dw_ref, db_ref, dgain_ref, dvn_ref):
        @pl.when(pl.program_id(0) == 0)
        def _():
            dw_ref[...] = jnp.zeros_like(dw_ref)
            db_ref[...] = jnp.zeros_like(db_ref)
            dgain_ref[...] = jnp.zeros_like(dgain_ref)

        mask = _chunk_mask()
        pu = p_ref[:, :GM_W]
        pv = p_ref[:, GM_W:2 * GM_W]
        u = _gelu(pu)
        v = _gelu(pv)
        r = _rms(v)
        vhat = v * r
        gain = gain_ref[...]
        vn = (vhat * gain).astype(BF16)
        gu_grad = _gelu_grad(pu)
        for g in range(GM_GROUPS):
            wg = jnp.where(mask, w_ref[g], 0.0).astype(BF16)
            cols = slice(g * GM_P, (g + 1) * GM_P)
            dw_acc = jnp.zeros((GM_P, GM_P), F32)
            db_acc = jnp.zeros((GM_P, 1), F32)
            for n in range(tm // GM_P):
                rows = slice(n * GM_P, (n + 1) * GM_P)
                dyb = dy_ref[rows, cols].astype(F32)
                vnb = vn[rows, cols]
                mixed = _dot(wg, vnb) + b_ref[:, cols]
                dmixed = dyb * u[rows, cols]
                dmb = dmixed.astype(BF16)
                dp_ref[rows, cols] = (dyb * mixed * gu_grad[rows, cols]).astype(BF16)
                dw_acc = dw_acc + _dot(dmb, vnb, NT)
                db_acc = db_acc + jnp.sum(dmixed, axis=1, keepdims=True)
                dvn_ref[rows, cols] = _dot(wg, dmb, TN)
            dw_ref[g] += jnp.where(mask, dw_acc, 0.0)
            db_ref[g] += jnp.broadcast_to(db_acc, (GM_P, GM_P))
        dvn = dvn_ref[...]
        dgain_ref[...] += jnp.sum(dvn * vhat, axis=0, keepdims=True)
        dvhat = dvn * gain
        dv = r * (dvhat - vhat * jnp.mean(dvhat * vhat, axis=-1, keepdims=True))
        dp_ref[:, GM_W:] = (dv * _gelu_grad(pv)).astype(BF16)

    const3 = lambda i: (0, 0, 0)
    return pl.pallas_call(
        body, name=name, grid=(s // tm,),
        in_specs=[pl.BlockSpec((tm, pw), lambda i: (i, 0)), pl.BlockSpec((tm, dw_total), lambda i: (i, 0)),
                  pl.BlockSpec((1, GM_W), lambda i: (0, 0)), pl.BlockSpec((GM_GROUPS, GM_P, GM_P), const3),
                  pl.BlockSpec((GM_P, GM_W), lambda i: (0, 0))],
        out_specs=[pl.BlockSpec((tm, 2 * GM_W), lambda i: (i, 0)), pl.BlockSpec((GM_GROUPS, GM_P, GM_P), const3),
                   pl.BlockSpec((GM_GROUPS, GM_P, GM_P), const3), pl.BlockSpec((1, GM_W), lambda i: (0, 0))],
        out_shape=[jax.ShapeDtypeStruct((s, 2 * GM_W), BF16), jax.ShapeDtypeStruct((GM_GROUPS, GM_P, GM_P), F32),
                   jax.ShapeDtypeStruct((GM_GROUPS, GM_P, GM_P), F32), jax.ShapeDtypeStruct((1, GM_W), F32)],
        scratch_shapes=[pltpu.VMEM((tm, GM_W), F32)],
        compiler_params=_params("arbitrary"),
    )(proj, dy, gain, w_s, bias)


def _keep(mask, xb):
    return jnp.where(mask, xb.astype(F32), 0.0).astype(BF16)


def _head_masks(rows, width, heads):
    lane = lax.broadcasted_iota(jnp.int32, (rows, width), 1)
    return [(lane >= HEAD_DIM * h) & (lane < HEAD_DIM * (h + 1)) for h in range(heads)]


def _mem_probs(qh, k):
    sc = _dot(qh, k, NT) * QK_SCALE
    e = jnp.exp(sc - jnp.max(sc, axis=-1, keepdims=True))
    return e / jnp.sum(e, axis=-1, keepdims=True)


def mem_fwd(proj, q_blk, mem_kv, layer, name):
    s = proj.shape[0]
    n_mem = mem_kv.shape[0]
    tm = _tile(s, 512, 16)

    def body(q_ref, k_ref, v_ref, o_ref):
        q = q_ref[...].astype(BF16)
        k, v = k_ref[...], v_ref[...]
        out = jnp.zeros((tm, MEM_W), F32)
        for hm in _head_masks(tm, MEM_W, MEM_HEADS):
            p = _mem_probs(_keep(hm, q), k)
            out = out + jnp.where(hm, _dot(p.astype(BF16), v), 0.0)
        o_ref[...] = out.astype(BF16)

    return pl.pallas_call(
        body, name=name, grid=(s // tm,),
        in_specs=[pl.BlockSpec((tm, MEM_W), lambda i: (i, q_blk)), pl.BlockSpec((n_mem, MEM_W), lambda i: (0, 2 * layer)),
                  pl.BlockSpec((n_mem, MEM_W), lambda i: (0, 2 * layer + 1))],
        out_specs=pl.BlockSpec((tm, MEM_W), lambda i: (i, 0)),
        out_shape=jax.ShapeDtypeStruct((s, MEM_W), BF16), compiler_params=_params("arbitrary"),
    )(proj, mem_kv, mem_kv)


def mem_bwd(proj, q_blk, mem_kv, layer, dy, dy_blk, name):
    s = proj.shape[0]
    n_mem = mem_kv.shape[0]
    tm = _tile(s, 512, 16)

    def body(q_ref, k_ref, v_ref, dy_ref, dq_ref, dk_ref, dv_ref):
        @pl.when(pl.program_id(0) == 0)
        def _():
            dk_ref[...] = jnp.zeros_like(dk_ref)
            dv_ref[...] = jnp.zeros_like(dv_ref)

        q = q_ref[...].astype(BF16)
        k, v = k_ref[...], v_ref[...]
        dy = dy_ref[...]
        dq = jnp.zeros((tm, MEM_W), F32)
        dk = jnp.zeros((n_mem, MEM_W), F32)
        dv = jnp.zeros((n_mem, MEM_W), F32)
        for hm in _head_masks(tm, MEM_W, MEM_HEADS):
            qh = _keep(hm, q)
            dyh = _keep(hm, dy)
            p = _mem_probs(qh, k)
            dp = _dot(dyh, v, NT)
            dv = dv + _dot(p.astype(BF16), dyh, TN)
            ds = (p * (dp - jnp.sum(dp * p, axis=-1, keepdims=True)) * QK_SCALE).astype(BF16)
            dq = dq + jnp.where(hm, _dot(ds, k), 0.0)
            dk = dk + _dot(ds, qh, TN)
        dq_ref[...] = dq.astype(BF16)
        dk_ref[...] += dk
        dv_ref[...] += dv

    const = lambda i: (0, 0)
    return pl.pallas_call(
        body, name=name, grid=(s // tm,),
        in_specs=[pl.BlockSpec((tm, MEM_W), lambda i: (i, q_blk)), pl.BlockSpec((n_mem, MEM_W), lambda i: (0, 2 * layer)),
                  pl.BlockSpec((n_mem, MEM_W), lambda i: (0, 2 * layer + 1)), pl.BlockSpec((tm, MEM_W), lambda i: (i, dy_blk))],
        out_specs=[pl.BlockSpec((tm, MEM_W), lambda i: (i, 0)), pl.BlockSpec((n_mem, MEM_W), const),
                   pl.BlockSpec((n_mem, MEM_W), const)],
        out_shape=[jax.ShapeDtypeStruct((s, MEM_W), BF16), jax.ShapeDtypeStruct((n_mem, MEM_W), F32),
                   jax.ShapeDtypeStruct((n_mem, MEM_W), F32)],
        compiler_params=_params("arbitrary"),
    )(proj, mem_kv, mem_kv, dy)


SB_KEYS = 512
SB_SUB = SB_KEYS // SB_BLK
SB_QROWS = 256
SB_QB = SB_QROWS // SB_BLK
SB_CHAINS = 2 * SB_QB
LOG2E = 1.4426950408889634


def _split(xf):
    hi = xf.astype(BF16)
    return hi, (xf - hi.astype(F32)).astype(BF16)


def _sb_consts():
    row = lax.bitwise_and(lax.broadcasted_iota(jnp.int32, (2 * SB_BLK, 2 * SB_BLK), 0), SB_BLK - 1)
    col = lax.broadcasted_iota(jnp.int32, (2 * SB_BLK, 2 * SB_BLK), 1)
    ones = col >= SB_BLK
    after2 = jnp.where(ones | (row > col), 1.0, 0.0).astype(BF16)
    from2 = jnp.where(ones | (row >= col), 1.0, 0.0).astype(BF16)
    r = lax.broadcasted_iota(jnp.int32, (SB_BLK, SB_BLK), 0)
    c = lax.broadcasted_iota(jnp.int32, (SB_BLK, SB_BLK), 1)
    return after2, from2, c - r, [c < HEAD_DIM, c >= HEAD_DIM]


def _suffix(xf, tri2):
    hi, lo = _split(xf)
    return _dot(jnp.concatenate([hi, lo], axis=1), tri2)


def _sb_logs(z2, mask):
    l1 = jnp.log2(1.0 + jnp.exp2(-jnp.abs(z2)))
    log_one_minus = jnp.minimum(-z2, 0.0) - l1
    if mask is not None:
        log_one_minus = jnp.where(mask, log_one_minus, 0.0)
    return log_one_minus, jnp.minimum(z2, 0.0) - l1


def _sb_setup(i, q_ref, heads):
    q = q_ref[...].astype(F32) * QK_SCALE
    qs = [jnp.where(hm, q[r * SB_BLK:(r + 1) * SB_BLK], 0.0).astype(BF16) for r in range(SB_QB) for hm in heads]
    own = lax.shift_right_logical(i * SB_QB, SB_SUB.bit_length() - 1)
    first = [lax.bitwise_and(i * SB_QB + r, SB_SUB - 1) * SB_BLK for r in range(SB_QB) for _ in heads]
    return qs, own, first


def _sb_walk(own, block, state):
    state = block(own, state, True)
    return lax.fori_loop(0, own, lambda t, st: block(own - 1 - t, st, False), state)


def _sb_tiles():
    return [(c, n) for c in reversed(range(SB_SUB)) for n in range(SB_CHAINS)]


def _sb_heads_apart(stacked, heads, r):
    return jnp.where(heads[0], stacked[2 * r * SB_BLK:(2 * r + 1) * SB_BLK],
                     stacked[(2 * r + 1) * SB_BLK:(2 * r + 2) * SB_BLK])


def sb_fwd(proj, kv, name):
    s = proj.shape[0]
    assert s % SB_KEYS == 0 and SB_KEYS % SB_QROWS == 0

    def body(q_ref, k_ref, v_ref, o_ref):
        after2, _, col_minus_row, heads = _sb_consts()
        qs, own, first = _sb_setup(pl.program_id(1), q_ref, heads)
        q_all = jnp.concatenate(qs, axis=0)

        def block(j, state, masked):
            runs, acc = list(state[0]), state[1]
            rows = pl.ds(pl.multiple_of(j * SB_KEYS, SB_KEYS), SB_KEYS)
            kb, vb = k_ref[rows, :], v_ref[rows, :]
            z = _dot(q_all, kb, NT) * LOG2E
            pend = {}
            parts = [[None] * SB_SUB for _ in range(SB_CHAINS)]
            for c, n in _sb_tiles():
                mask = (col_minus_row < first[n] - c * SB_BLK) if masked else None
                lom, lb = _sb_logs(z[n * SB_BLK:(n + 1) * SB_BLK, c * SB_BLK:(c + 1) * SB_BLK], mask)
                pend[c, n] = (lb, _suffix(lom, after2), mask)
            for c, n in _sb_tiles():
                lb, r, mask = pend.pop((c, n))
                a = jnp.exp2(lb + r[:, :SB_BLK] + runs[n])
                if masked:
                    a = jnp.where(mask, a, 0.0)
                parts[n][c] = a.astype(BF16)
                runs[n] = runs[n] + r[:, SB_BLK:]
            a_all = jnp.concatenate([jnp.concatenate(p, axis=1) for p in parts], axis=0)
            return tuple(runs), acc + _dot(a_all, vb)

        zero = jnp.zeros((SB_BLK, LANES), F32)
        state = _sb_walk(own, block, ((zero,) * SB_CHAINS, jnp.zeros((SB_CHAINS * SB_BLK, LANES), F32)))
        for r in range(SB_QB):
            o_ref[r * SB_BLK:(r + 1) * SB_BLK, :] = _sb_heads_apart(state[1], heads, r)

    pairs = SB_W // LANES
    return pl.pallas_call(
        body, name=name, grid=(pairs, s // SB_QROWS),
        in_specs=[pl.BlockSpec((SB_QROWS, LANES), lambda p, i: (i, p)), pl.BlockSpec((s, LANES), lambda p, i: (0, p)),
                  pl.BlockSpec((s, LANES), lambda p, i: (0, pairs + p))],
        out_specs=pl.BlockSpec((SB_QROWS, LANES), lambda p, i: (i, p)),
        out_shape=jax.ShapeDtypeStruct((s, SB_W), F32),
        compiler_params=_params("arbitrary", "arbitrary"),
    )(proj, kv, kv)


def sb_bwd(proj, kv, out, dy, name):
    s = proj.shape[0]

    def body(q_ref, k_ref, v_ref, o_ref, do_ref, dq_ref, dk_ref, dv_ref):
        i = pl.program_id(1)

        @pl.when(i == 0)
        def _():
            dk_ref[...] = jnp.zeros_like(dk_ref)
            dv_ref[...] = jnp.zeros_like(dv_ref)

        after2, from2, col_minus_row, heads = _sb_consts()
        qs, own, first = _sb_setup(i, q_ref, heads)
        q_all = jnp.concatenate(qs, axis=0)
        d_out = do_ref[...].astype(F32)
        prod = d_out * o_ref[...]
        dos, totals = [], []
        for r in range(SB_QB):
            rr = slice(r * SB_BLK, (r + 1) * SB_BLK)
            for hm in heads:
                dos.append(jnp.where(hm, d_out[rr], 0.0).astype(BF16))
                totals.append(jnp.broadcast_to(jnp.sum(jnp.where(hm, prod[rr], 0.0), axis=1, keepdims=True),
                                               (SB_BLK, SB_BLK)))
        do_all = jnp.concatenate(dos, axis=0)

        def block(j, state, masked):
            runs, seens, dq = list(state[0]), list(state[1]), state[2]
            rows = pl.ds(pl.multiple_of(j * SB_KEYS, SB_KEYS), SB_KEYS)
            kb, vb = k_ref[rows, :], v_ref[rows, :]
            z = _dot(q_all, kb, NT) * LOG2E
            da = _dot(do_all, vb, NT)
            pend, pend2 = {}, {}
            a_parts = [[None] * SB_SUB for _ in range(SB_CHAINS)]
            dz_parts = [[None] * SB_SUB for _ in range(SB_CHAINS)]
            for c, n in _sb_tiles():
                mask = (col_minus_row < first[n] - c * SB_BLK) if masked else None
                lom, lb = _sb_logs(z[n * SB_BLK:(n + 1) * SB_BLK, c * SB_BLK:(c + 1) * SB_BLK], mask)
                pend[c, n] = (lom, lb, _suffix(lom, after2), mask)
            for c, n in _sb_tiles():
                lom, lb, r, mask = pend.pop((c, n))
                a = jnp.exp2(lb + r[:, :SB_BLK] + runs[n])
                if masked:
                    a = jnp.where(mask, a, 0.0)
                runs[n] = runs[n] + r[:, SB_BLK:]
                ab = a.astype(BF16)
                a_parts[n][c] = ab
                dl = ab.astype(F32) * da[n * SB_BLK:(n + 1) * SB_BLK, c * SB_BLK:(c + 1) * SB_BLK]
                pend2[c, n] = (lom, lb, dl, _suffix(dl, from2), mask)
            for c, n in _sb_tiles():
                lom, lb, dl, r2, mask = pend2.pop((c, n))
                d_lom = totals[n] - (r2[:, :SB_BLK] + seens[n])
                if masked:
                    d_lom = jnp.where(mask, d_lom, 0.0)
                seens[n] = seens[n] + r2[:, SB_BLK:]
                dz_parts[n][c] = (dl * jnp.exp2(lom) - d_lom * jnp.exp2(lb)).astype(BF16)
            a_all = jnp.concatenate([jnp.concatenate(p, axis=1) for p in a_parts], axis=0)
            dz_all = jnp.concatenate([jnp.concatenate(p, axis=1) for p in dz_parts], axis=0)
            dv_ref[rows, :] += _dot(a_all, do_all, TN)
            dk_ref[rows, :] += _dot(dz_all, q_all, TN)
            return tuple(runs), tuple(seens), dq + _dot(dz_all, kb)

        zero = jnp.zeros((SB_BLK, LANES), F32)
        state = _sb_walk(own, block, ((zero,) * SB_CHAINS, (zero,) * SB_CHAINS,
                                      jnp.zeros((SB_CHAINS * SB_BLK, LANES), F32)))
        for r in range(SB_QB):
            dq_ref[r * SB_BLK:(r + 1) * SB_BLK, :] = (_sb_heads_apart(state[2], heads, r) * QK_SCALE).astype(BF16)

    pairs = SB_W // LANES
    blk = lambda p, i: (i, p)
    col = lambda p, i: (0, p)
    return pl.pallas_call(
        body, name=name, grid=(pairs, s // SB_QROWS),
        in_specs=[pl.BlockSpec((SB_QROWS, LANES), blk), pl.BlockSpec((s, LANES), col),
                  pl.BlockSpec((s, LANES), lambda p, i: (0, pairs + p)), pl.BlockSpec((SB_QROWS, LANES), blk),
                  pl.BlockSpec((SB_QROWS, LANES), blk)],
        out_specs=[pl.BlockSpec((SB_QROWS, LANES), blk), pl.BlockSpec((s, LANES), col), pl.BlockSpec((s, LANES), col)],
        out_shape=[jax.ShapeDtypeStruct((s, SB_W), BF16), jax.ShapeDtypeStruct((s, SB_W), F32),
                   jax.ShapeDtypeStruct((s, SB_W), F32)],
        compiler_params=_params("arbitrary", "arbitrary"),
    )(proj, kv, kv, out, dy)


def final_loss(x, g, target, name):
    s, d = x.shape
    tm = _tile(s, 256, 8)

    def body(x_ref, g_ref, t_ref, loss_ref, dx_ref, dg_ref):
        @pl.when(pl.program_id(0) == 0)
        def _():
            loss_ref[...] = jnp.zeros_like(loss_ref)
            dg_ref[...] = jnp.zeros_like(dg_ref)

        xf = x_ref[...]
        r = _rms(xf)
        xhat = xf * r
        gain = g_ref[...]
        diff = xhat * gain - t_ref[...]
        sq = jnp.sum(jnp.sum(diff * diff, axis=1, keepdims=True), axis=0, keepdims=True)
        loss_ref[...] += jnp.broadcast_to(sq, loss_ref.shape)
        dy = diff * (1.0 / d)
        dg_ref[...] += jnp.sum(dy * xhat, axis=0, keepdims=True)
        dxhat = dy * gain
        dx_ref[...] = r * (dxhat - xhat * jnp.mean(dxhat * xhat, axis=-1, keepdims=True))

    row = lambda i: (i, 0)
    const = lambda i: (0, 0)
    return pl.pallas_call(
        body, name=name, grid=(s // tm,),
        in_specs=[pl.BlockSpec((tm, d), row), pl.BlockSpec((1, d), const), pl.BlockSpec((tm, d), row)],
        out_specs=[pl.BlockSpec((8, LANES), const), pl.BlockSpec((tm, d), row), pl.BlockSpec((1, d), const)],
        out_shape=[jax.ShapeDtypeStruct((8, LANES), F32), jax.ShapeDtypeStruct((s, d), F32), jax.ShapeDtypeStruct((1, d), F32)],
        compiler_params=_params("arbitrary"),
    )(x, g, target)


def adamw(w, parts, m, v, name):
    rows, cols = w.shape
    k = parts.shape[0]
    tr = _tile(rows, 512, 16)
    c1, c2 = 1.0 - ADAM_B1 ** ADAM_STEP, 1.0 - ADAM_B2 ** ADAM_STEP

    def body(w_ref, p_ref, m_ref, v_ref, g_ref, d_ref, nm_ref, nv_ref):
        grad = p_ref[0].astype(F32)
        for s in range(1, k):
            grad = grad + p_ref[s].astype(F32)
        nm = ADAM_B1 * m_ref[...] + (1.0 - ADAM_B1) * grad
        nv = ADAM_B2 * v_ref[...] + (1.0 - ADAM_B2) * (grad * grad)
        g_ref[...] = grad
        d_ref[...] = -ADAM_LR * ((nm / c1) / (jnp.sqrt(nv / c2) + ADAM_EPS) + ADAM_WD * w_ref[...])
        nm_ref[...] = nm
        nv_ref[...] = nv

    spec = pl.BlockSpec((tr, cols), lambda i: (i, 0))
    shape = jax.ShapeDtypeStruct((rows, cols), F32)
    return pl.pallas_call(
        body, name=name, grid=(rows // tr,),
        in_specs=[spec, pl.BlockSpec((k, tr, cols), lambda i: (0, i, 0)), spec, spec],
        out_specs=[spec] * 4, out_shape=[shape] * 4,
        compiler_params=_params("arbitrary"),
    )(w, parts, m, v)


SHARDED = {"ffn1_w_gate": 2, "ffn1_w_up": 2, "ffn1_w_down": 1, "ffn2_w_gate": 2, "ffn2_w_up": 2, "ffn2_w_down": 1,
           "w_mem_kv": 1, "a_w_in": 2, "a_w_out": 1, "w_kv": 1, "b_w_in": 1, "b_w_out": 1}
SMALL = ["ffn1_norm", "mix_norm", "ffn2_norm", "mem_norm", "kv_norm", "final_norm", "a_v_norm", "a_w_spatial", "a_b_spatial"]
WEIGHTS = ["ffn1_norm", "ffn1_w_gate", "ffn1_w_up", "ffn1_w_down", "mix_norm", "ffn2_norm", "ffn2_w_gate", "ffn2_w_up",
           "ffn2_w_down", "mem_norm", "w_mem_kv", "a_w_in", "a_v_norm", "a_w_spatial", "a_b_spatial", "a_w_out", "kv_norm",
           "w_kv", "b_w_in", "b_w_out", "final_norm"]


def _gather_weights(shards):
    by_dev = exchange([shards[n].astype(BF16) for n in SHARDED], "all", True, "gather_weights")
    return {n: jnp.concatenate([blocks[d] for d in range(N_DEV)], axis=axis)
            for (n, axis), blocks in zip(SHARDED.items(), by_dev)}


def _scatter_grads(grads):
    by_dev = [jnp.stack(jnp.split(grads[n].astype(BF16), N_DEV, axis=axis)) for n, axis in SHARDED.items()]
    return dict(zip(SHARDED, exchange(by_dev, "all", False, "scatter_grads")))


def _all_sum(parts, name):
    flat = jnp.concatenate([p.reshape(-1) for p in parts])
    pad = (-flat.size) % (16 * LANES)
    buf = jnp.pad(flat, (0, pad)).reshape(-1, LANES)
    total = sum_leading(exchange([buf], "all", True, name)[0], F32, name + "_sum").reshape(-1)
    out, off = [], 0
    for p in parts:
        out.append(total[off:off + p.size].reshape(p.shape))
        off += p.size
    return out


def _device_index():
    return 4 * lax.axis_index("x") + 2 * lax.axis_index("y") + lax.axis_index("c")


def kernel(x, mem, ffn1_norm, ffn1_w_gate, ffn1_w_up, ffn1_w_down, mix_norm, ffn2_norm, ffn2_w_gate, ffn2_w_up, ffn2_w_down, mem_norm, w_mem_kv, a_w_in, a_v_norm, a_w_spatial, a_b_spatial, a_w_out, kv_norm, w_kv, b_w_in, b_w_out, final_norm, loss_target, m_ffn1_norm, m_ffn1_w_gate, m_ffn1_w_up, m_ffn1_w_down, m_mix_norm, m_ffn2_norm, m_ffn2_w_gate, m_ffn2_w_up, m_ffn2_w_down, m_mem_norm, m_w_mem_kv, m_a_w_in, m_a_v_norm, m_a_w_spatial, m_a_b_spatial, m_a_w_out, m_kv_norm, m_w_kv, m_b_w_in, m_b_w_out, m_final_norm, v_ffn1_norm, v_ffn1_w_gate, v_ffn1_w_up, v_ffn1_w_down, v_mix_norm, v_ffn2_norm, v_ffn2_w_gate, v_ffn2_w_up, v_ffn2_w_down, v_mem_norm, v_w_mem_kv, v_a_w_in, v_a_v_norm, v_a_w_spatial, v_a_b_spatial, v_a_w_out, v_kv_norm, v_w_kv, v_b_w_in, v_b_w_out, v_final_norm):
    weights = dict(ffn1_norm=ffn1_norm, ffn1_w_gate=ffn1_w_gate, ffn1_w_up=ffn1_w_up, ffn1_w_down=ffn1_w_down, mix_norm=mix_norm, ffn2_norm=ffn2_norm, ffn2_w_gate=ffn2_w_gate, ffn2_w_up=ffn2_w_up, ffn2_w_down=ffn2_w_down, mem_norm=mem_norm, w_mem_kv=w_mem_kv, a_w_in=a_w_in, a_v_norm=a_v_norm, a_w_spatial=a_w_spatial, a_b_spatial=a_b_spatial, a_w_out=a_w_out, kv_norm=kv_norm, w_kv=w_kv, b_w_in=b_w_in, b_w_out=b_w_out, final_norm=final_norm)
    mom1 = dict(ffn1_norm=m_ffn1_norm, ffn1_w_gate=m_ffn1_w_gate, ffn1_w_up=m_ffn1_w_up, ffn1_w_down=m_ffn1_w_down, mix_norm=m_mix_norm, ffn2_norm=m_ffn2_norm, ffn2_w_gate=m_ffn2_w_gate, ffn2_w_up=m_ffn2_w_up, ffn2_w_down=m_ffn2_w_down, mem_norm=m_mem_norm, w_mem_kv=m_w_mem_kv, a_w_in=m_a_w_in, a_v_norm=m_a_v_norm, a_w_spatial=m_a_w_spatial, a_b_spatial=m_a_b_spatial, a_w_out=m_a_w_out, kv_norm=m_kv_norm, w_kv=m_w_kv, b_w_in=m_b_w_in, b_w_out=m_b_w_out, final_norm=m_final_norm)
    mom2 = dict(ffn1_norm=v_ffn1_norm, ffn1_w_gate=v_ffn1_w_gate, ffn1_w_up=v_ffn1_w_up, ffn1_w_down=v_ffn1_w_down, mix_norm=v_mix_norm, ffn2_norm=v_ffn2_norm, ffn2_w_gate=v_ffn2_w_gate, ffn2_w_up=v_ffn2_w_up, ffn2_w_down=v_ffn2_w_down, mem_norm=v_mem_norm, w_mem_kv=v_w_mem_kv, a_w_in=v_a_w_in, a_v_norm=v_a_v_norm, a_w_spatial=v_a_w_spatial, a_b_spatial=v_a_b_spatial, a_w_out=v_a_w_out, kv_norm=v_kv_norm, w_kv=v_w_kv, b_w_in=v_b_w_in, b_w_out=v_b_w_out, final_norm=v_final_norm)

    dev = _device_index()
    xs, mem_in, target = x[0], mem[0], loss_target[0]
    d_model = xs.shape[1]
    shards = {n: weights[n] for n in SHARDED}
    full = _gather_weights(shards)
    vn_width = a_v_norm.shape[1]
    a_v_full = _all_sum([lax.dynamic_update_slice(jnp.zeros((N_A, N_DEV * vn_width), F32), a_v_norm, (0, dev * vn_width))],
                        "gather_v_norm")[0]

    row = lambda v: v.reshape(1, -1)
    w_gu = {f: jnp.concatenate([full[f + "_w_gate"], full[f + "_w_up"]], axis=2) for f in ("ffn1", "ffn2")}
    w_mem_cat = full["w_mem_kv"].transpose(1, 0, 2).reshape(d_model, -1)
    bias = [jnp.repeat(a_b_spatial[i].T, GM_P, axis=1) for i in range(N_A)]

    mem_kv, mem_h = norm_mm(mem_in, row(mem_norm), w_mem_cat, BF16, "mem_kv", emit_h=True)

    def ffn_fwd(xin, f, l):
        gu = norm_mm(xin, row(weights[f + "_norm"][l]), w_gu[f][l], BF16, "ffn_gu")
        act = swiglu_fwd(gu, "ffn_act")
        return mm_res(act, full[f + "_w_down"][l], xin, 0.5, "ffn_down"), gu, act

    saved = []
    kv = x_kv = None
    cur = xs
    for l in range(DEPTH):
        st = {"x0": cur}
        if l == N_A:
            x_kv = cur
            kv = norm_mm(cur, row(kv_norm), full["w_kv"], BF16, "kv_proj")
        st["x1"], st["gu1"], st["act1"] = ffn_fwd(cur, "ffn1", l)
        if l < N_A:
            proj = norm_mm(st["x1"], row(mix_norm[l]), full["a_w_in"][l], F32, "a_proj")
            y_tok = gmlp_fwd(proj, row(a_v_full[l]), a_w_spatial[l], bias[l], "gmlp_fwd")
            y_mem = mem_fwd(proj, 2 * GM_W // MEM_W, mem_kv, l, "mem_fwd_a")
            w_out = full["a_w_out"][l]
        else:
            proj = norm_mm(st["x1"], row(mix_norm[l]), full["b_w_in"][l - N_A], BF16, "b_proj")
            st["sb_out"] = sb_fwd(proj, kv, "sb_fwd")
            y_tok = st["sb_out"].astype(BF16)
            y_mem = mem_fwd(proj, SB_W // MEM_W, mem_kv, l, "mem_fwd_b")
            w_out = full["b_w_out"][l - N_A]
        st["proj"] = proj
        st["y"] = jnp.concatenate([y_tok, y_mem], axis=1)
        st["x2"] = mm_res(st["y"], w_out, st["x1"], 1.0, "mix_out")
        cur, st["gu2"], st["act2"] = ffn_fwd(st["x2"], "ffn2", l)
        saved.append(st)

    loss_blk, dx, d_final = final_loss(cur, row(final_norm), target, "final_loss")
    loss = lax.psum(loss_blk[0, 0] * (0.5 / d_model), AXES)

    grads = {n: [None] * weights[n].shape[0] for n in WEIGHTS if weights[n].ndim >= 2 and n not in ("w_kv",)}
    grads["final_norm"] = d_final.reshape(-1)
    d_mem_kv = [None] * DEPTH
    d_kv = []

    def ffn_bwd(dx, xin, gu, act, f, l):
        d_act = mm_nt(dx, full[f + "_w_down"][l], 0.5, "ffn_dact")
        d_gu = swiglu_bwd(gu, d_act, "ffn_dgu")
        dx_new, d_gain, h = mm_nt_normbwd(d_gu, w_gu[f][l], xin, row(weights[f + "_norm"][l]), dx, "ffn_dx")
        d_wgu = mm_tn(h, d_gu, 1.0, "ffn_dwgu", tb_target=1408)
        half = d_wgu.shape[1] // 2
        grads[f + "_w_gate"][l], grads[f + "_w_up"][l] = d_wgu[:, :half], d_wgu[:, half:]
        grads[f + "_w_down"][l] = mm_tn(act, dx, 0.5, "ffn_dwdown", ta_target=1408, tb_target=1024)
        grads[f + "_norm"][l] = d_gain.reshape(-1)
        return dx_new

    for l in reversed(range(DEPTH)):
        st = saved[l]
        dx = ffn_bwd(dx, st["x2"], st["gu2"], st["act2"], "ffn2", l)
        proj = st["proj"]
        if l < N_A:
            w_in, w_out, key_in, key_out, idx = full["a_w_in"][l], full["a_w_out"][l], "a_w_in", "a_w_out", l
        else:
            w_in, w_out, key_in, key_out, idx = full["b_w_in"][l - N_A], full["b_w_out"][l - N_A], "b_w_in", "b_w_out", l - N_A
        dy = mm_nt(dx, w_out, 1.0, "mix_dy")
        grads[key_out][idx] = mm_tn(st["y"], dx, 1.0, "mix_dwout", tb_target=1024)
        if l < N_A:
            d_uv, d_ws, d_bs, d_vgain = gmlp_bwd(proj, dy, row(a_v_full[l]), a_w_spatial[l], bias[l], "gmlp_bwd")
            grads["a_w_spatial"][l], grads["a_b_spatial"][l], grads["a_v_norm"][l] = d_ws, d_bs[:, :, 0], d_vgain.reshape(-1)
            d_q, d_k, d_v = mem_bwd(proj, 2 * GM_W // MEM_W, mem_kv, l, dy, GM_W // MEM_W, "mem_bwd_a")
            d_proj = jnp.concatenate([d_uv, d_q], axis=1)
        else:
            d_qsb, d_ksb, d_vsb = sb_bwd(proj, kv, st["sb_out"], dy, "sb_bwd")
            d_kv.append(jnp.concatenate([d_ksb, d_vsb], axis=1))
            d_q, d_k, d_v = mem_bwd(proj, SB_W // MEM_W, mem_kv, l, dy, SB_W // MEM_W, "mem_bwd_b")
            d_proj = jnp.concatenate([d_qsb, d_q], axis=1)
        d_mem_kv[l] = jnp.concatenate([d_k, d_v], axis=1)
        dx, d_gain, h = mm_nt_normbwd(d_proj, w_in, st["x1"], row(mix_norm[l]), dx, "mix_dx")
        grads["mix_norm"][l] = d_gain.reshape(-1)
        grads[key_in][idx] = mm_tn(h, d_proj, 1.0, "mix_dwin")
        dx = ffn_bwd(dx, st["x0"], st["gu1"], st["act1"], "ffn1", l)
        if l == N_A:
            d_kv_b = sum_leading(jnp.stack(d_kv), BF16, "kv_dsum")
            dx, d_gain, h = mm_nt_normbwd(d_kv_b, full["w_kv"], x_kv, row(kv_norm), dx, "kv_dx")
            grads["kv_norm"] = d_gain.reshape(-1)
            grads["w_kv"] = mm_tn(h, d_kv_b, 1.0, "kv_dw")

    d_mem_all = jnp.concatenate(d_mem_kv, axis=1).astype(BF16)
    _, d_gain, _ = mm_nt_normbwd(d_mem_all, w_mem_cat, mem_in, row(mem_norm), None, "mem_dnorm")
    grads["mem_norm"] = d_gain.reshape(-1)
    d_wmem = mm_tn(mem_h, d_mem_all, 1.0, "mem_dw")
    grads["w_mem_kv"] = d_wmem.reshape(d_model, DEPTH, -1).transpose(1, 0, 2)
    grads = {n: (jnp.stack(g) if isinstance(g, list) else g) for n, g in grads.items()}

    parts = _scatter_grads({n: grads[n] for n in SHARDED})
    for n, g in zip(SMALL, _all_sum([grads[n] for n in SMALL], "sum_small")):
        parts[n] = g[None]
    parts["a_v_norm"] = lax.dynamic_slice(parts["a_v_norm"], (0, 0, dev * vn_width), (1,) + a_v_norm.shape)

    reduced, deltas, new_m, new_v = {}, {}, {}, {}
    for n in WEIGHTS:
        w = weights[n]
        view = (lambda a: a.reshape(-1, a.shape[-1]))
        res = adamw(view(w), parts[n].reshape(parts[n].shape[0], -1, w.shape[-1]), view(mom1[n]), view(mom2[n]), "adamw")
        reduced[n], deltas[n], new_m[n], new_v[n] = [r.reshape(w.shape) for r in res]

    return (loss, dx[None], *[reduced[n] for n in WEIGHTS], *[deltas[n] for n in WEIGHTS],
            *[new_m[n] for n in WEIGHTS], *[new_v[n] for n in WEIGHTS])
```

```python
import functools

import jax
import jax.numpy as jnp
from jax import lax
from jax.experimental import pallas as pl
from jax.experimental.pallas import tpu as pltpu

F32, BF16 = jnp.float32, jnp.bfloat16
MESH_ID = pl.DeviceIdType.MESH
AXES = ("x", "y", "c")
N_DEV = 8

EPS = 1e-6
DEPTH, N_A = 4, 2
GM_W, GM_GROUPS, GM_P = 768, 6, 128
MEM_W, MEM_HEADS, HEAD_DIM = 256, 4, 64
SB_W, SB_BLK = 768, 128
LANES = 128
QK_SCALE = HEAD_DIM ** -0.5
GELU_C, GELU_A = 0.7978845608028654, 0.044715

ADAM_LR, ADAM_B1, ADAM_B2, ADAM_EPS, ADAM_WD, ADAM_STEP = 0.001, 0.9, 0.999, 1e-08, 0.01, 10

VMEM_LIMIT = 56 * 1024 * 1024
PACK_COLS = 512

NT = (((1,), (1,)), ((), ()))
TN = (((0,), (0,)), ((), ()))


def _params(*sem):
    return pltpu.CompilerParams(dimension_semantics=sem, vmem_limit_bytes=VMEM_LIMIT)


def _tile(n, target, mult=LANES):
    best = None
    for t in range(mult, min(n, target) + 1, mult):
        if n % t == 0:
            best = t
    return best if best is not None else n


def _dot(a, b, dims=None):
    if dims is None:
        return jnp.dot(a, b, preferred_element_type=F32)
    return lax.dot_general(a, b, dims, preferred_element_type=F32)


def exchange(srcs, group, same_src, name, split=False):
    size = {"pair": 2, "quad": 4, "all": 8}[group]
    n = len(srcs)
    chunk_shapes = [tuple(s.shape) if same_src else tuple(s.shape[1:]) for s in srcs]
    pieces = [cs[0] if split else 1 for cs in chunk_shapes]
    n_dma = sum(pieces)

    def body(*refs):
        src_refs, out_refs = refs[:n], refs[n:2 * n]
        send_sems, recv_sems, local_sems = refs[2 * n:]
        x, y, c = lax.axis_index("x"), lax.axis_index("y"), lax.axis_index("c")
        if group == "pair":
            me, dev = c, lambda p: (x, y, p)
        elif group == "quad":
            me, dev = 2 * x + y, lambda p: (p // 2, p % 2, c)
        else:
            me, dev = 4 * x + 2 * y + c, lambda p: (p // 4, (p // 2) % 2, p % 2)

        def chunk(t, idx):
            return src_refs[t] if same_src else src_refs[t].at[idx]

        def copies(k, idx, slot, peer):
            out, w = [], k * n_dma
            for t in range(n):
                src, dst = chunk(t, idx), out_refs[t].at[slot]
                for s_ref, d_ref in ([(src.at[u], dst.at[u]) for u in range(pieces[t])] if split else [(src, dst)]):
                    out.append(pltpu.make_async_remote_copy(
                        src_ref=s_ref, dst_ref=d_ref, send_sem=send_sems.at[w], recv_sem=recv_sems.at[w],
                        device_id=dev(peer), device_id_type=MESH_ID))
                    w += 1
            return out

        local = [pltpu.make_async_copy(chunk(t, me), out_refs[t].at[me], local_sems.at[t]) for t in range(n)]
        for cp in local:
            cp.start()
        sends = []
        for k in range(1, size):
            peer = (me + k) % size
            sends += copies(k, peer, me, peer)
        for cp in sends:
            cp.start()
        for k in range(1, size):
            sender = (me + size - k) % size
            for cp in copies(k, me, sender, sender):
                cp.wait_recv()
        for cp in sends:
            cp.wait_send()
        for cp in local:
            cp.wait()

    hbm = pl.BlockSpec(memory_space=pltpu.HBM)
    return pl.pallas_call(
        body, name=name,
        out_shape=[jax.ShapeDtypeStruct((size,) + cs, s.dtype) for cs, s in zip(chunk_shapes, srcs)],
        in_specs=[hbm] * n, out_specs=[hbm] * n,
        scratch_shapes=[pltpu.SemaphoreType.DMA((size * n_dma,)), pltpu.SemaphoreType.DMA((size * n_dma,)),
                        pltpu.SemaphoreType.DMA((n,))],
    )(*srcs)


def sum_leading(parts, out_dtype, name):
    k, rows, cols = parts.shape
    tr = _tile(rows, 512, 16)

    def body(p_ref, o_ref):
        acc = p_ref[0].astype(F32)
        for s in range(1, k):
            acc = acc + p_ref[s].astype(F32)
        o_ref[...] = acc.astype(o_ref.dtype)

    return pl.pallas_call(
        body, name=name, grid=(rows // tr,),
        in_specs=[pl.BlockSpec((k, tr, cols), lambda i: (0, i, 0))],
        out_specs=pl.BlockSpec((tr, cols), lambda i: (i, 0)),
        out_shape=jax.ShapeDtypeStruct((rows, cols), out_dtype),
        compiler_params=_params("arbitrary"),
    )(parts)


def _rms(xf):
    return lax.rsqrt(jnp.mean(xf * xf, axis=-1, keepdims=True) + EPS)


def norm_mm(x, g, w, out_dtype, name, emit_h=False):
    m, d = x.shape
    n = w.shape[1]
    tm, tn = _tile(m, 1024, 8), _tile(n, 1408)

    def body(x_ref, g_ref, w_ref, o_ref, *rest):
        h_ref = rest[-1]

        @pl.when(pl.program_id(1) == 0)
        def _():
            xf = x_ref[...]
            hb = ((xf * _rms(xf)) * g_ref[...]).astype(BF16)
            h_ref[...] = hb
            if emit_h:
                rest[0][...] = hb

        o_ref[...] = _dot(h_ref[...], w_ref[...]).astype(o_ref.dtype)

    out_shape = [jax.ShapeDtypeStruct((m, n), out_dtype)]
    out_specs = [pl.BlockSpec((tm, tn), lambda i, j: (i, j))]
    if emit_h:
        out_shape.append(jax.ShapeDtypeStruct((m, d), BF16))
        out_specs.append(pl.BlockSpec((tm, d), lambda i, j: (i, 0)))
    res = pl.pallas_call(
        body, name=name, grid=(m // tm, n // tn),
        in_specs=[pl.BlockSpec((tm, d), lambda i, j: (i, 0)), pl.BlockSpec((1, d), lambda i, j: (0, 0)),
                  pl.BlockSpec((d, tn), lambda i, j: (0, j))],
        out_specs=out_specs, out_shape=out_shape,
        scratch_shapes=[pltpu.VMEM((tm, d), BF16)],
        compiler_params=_params("arbitrary", "arbitrary"),
    )(x, g, w)
    return res if emit_h else res[0]


def mm_res(a, w, res, alpha, name):
    m, k = a.shape
    n = w.shape[1]
    tm, tn = _tile(m, 1024, 8), _tile(n, 1024)

    def body(a_ref, w_ref, r_ref, o_ref):
        o_ref[...] = r_ref[...] + alpha * _dot(a_ref[...], w_ref[...])

    return pl.pallas_call(
        body, name=name, grid=(m // tm, n // tn),
        in_specs=[pl.BlockSpec((tm, k), lambda i, j: (i, 0)), pl.BlockSpec((k, tn), lambda i, j: (0, j)),
                  pl.BlockSpec((tm, tn), lambda i, j: (i, j))],
        out_specs=pl.BlockSpec((tm, tn), lambda i, j: (i, j)),
        out_shape=jax.ShapeDtypeStruct((m, n), F32),
        compiler_params=_params("arbitrary", "arbitrary"),
    )(a, w, res)


def mm_nt(x, w, alpha, name):
    m, d = x.shape
    n = w.shape[0]
    tm, tn = _tile(m, 1024, 8), _tile(n, 1408)

    def body(x_ref, w_ref, o_ref, xb_ref):
        @pl.when(pl.program_id(1) == 0)
        def _():
            xb_ref[...] = x_ref[...].astype(BF16)

        o_ref[...] = (alpha * _dot(xb_ref[...], w_ref[...], NT)).astype(o_ref.dtype)

    return pl.pallas_call(
        body, name=name, grid=(m // tm, n // tn),
        in_specs=[pl.BlockSpec((tm, d), lambda i, j: (i, 0)), pl.BlockSpec((tn, d), lambda i, j: (j, 0))],
        out_specs=pl.BlockSpec((tm, tn), lambda i, j: (i, j)),
        out_shape=jax.ShapeDtypeStruct((m, n), BF16),
        scratch_shapes=[pltpu.VMEM((tm, d), BF16)],
        compiler_params=_params("arbitrary", "arbitrary"),
    )(x, w)


def mm_tn(a, b, alpha, name, ta_target=1024, tb_target=512):
    s, ka = a.shape
    nb = b.shape[1]
    ta, tb, ts = _tile(ka, ta_target), _tile(nb, tb_target), _tile(s, 1024, 16)
    steps = s // ts

    def body(a_ref, b_ref, o_ref, acc_ref):
        t = pl.program_id(2)

        @pl.when(t == 0)
        def _():
            acc_ref[...] = jnp.zeros_like(acc_ref)

        acc_ref[...] += _dot(a_ref[...].astype(BF16), b_ref[...].astype(BF16), TN)

        @pl.when(t == steps - 1)
        def _():
            o_ref[...] = alpha * acc_ref[...]

    return pl.pallas_call(
        body, name=name, grid=(ka // ta, nb // tb, steps),
        in_specs=[pl.BlockSpec((ts, ta), lambda i, j, t: (t, i)), pl.BlockSpec((ts, tb), lambda i, j, t: (t, j))],
        out_specs=pl.BlockSpec((ta, tb), lambda i, j, t: (i, j)),
        out_shape=jax.ShapeDtypeStruct((ka, nb), F32),
        scratch_shapes=[pltpu.VMEM((ta, tb), F32)],
        compiler_params=_params("arbitrary", "arbitrary", "arbitrary"),
    )(a, b)


def mm_nt_normbwd(dy, w, x, g, res, name):
    m, n = dy.shape
    d = w.shape[0]
    tm, tk = _tile(m, 1024, 8), _tile(n, 1408)
    steps = n // tk
    has_res = res is not None

    def body(*refs):
        if has_res:
            dy_ref, w_ref, x_ref, g_ref, r_ref, dx_ref, dg_ref, h_ref, acc_ref = refs
        else:
            dy_ref, w_ref, x_ref, g_ref, dx_ref, dg_ref, h_ref, acc_ref = refs
        i, t = pl.program_id(0), pl.program_id(1)

        @pl.when(t == 0)
        def _():
            acc_ref[...] = jnp.zeros_like(acc_ref)

        @pl.when((t == 0) & (i == 0))
        def _():
            dg_ref[...] = jnp.zeros_like(dg_ref)

        acc_ref[...] += _dot(dy_ref[...], w_ref[...], NT)

        @pl.when(t == steps - 1)
        def _():
            xf = x_ref[...]
            r = _rms(xf)
            xhat = xf * r
            dh = acc_ref[...]
            gain = g_ref[...]
            dg_ref[...] += jnp.sum(dh * xhat, axis=0, keepdims=True)
            dxhat = dh * gain
            dx = r * (dxhat - xhat * jnp.mean(dxhat * xhat, axis=-1, keepdims=True))
            dx_ref[...] = (r_ref[...] + dx) if has_res else dx
            h_ref[...] = (xhat * gain).astype(BF16)

    row = lambda i, t: (i, 0)
    in_specs = [pl.BlockSpec((tm, tk), lambda i, t: (i, t)), pl.BlockSpec((d, tk), lambda i, t: (0, t)),
                pl.BlockSpec((tm, d), row), pl.BlockSpec((1, d), lambda i, t: (0, 0))]
    args = [dy, w, x, g]
    if has_res:
        in_specs.append(pl.BlockSpec((tm, d), row))
        args.append(res)
    return pl.pallas_call(
        body, name=name, grid=(m // tm, steps),
        in_specs=in_specs,
        out_specs=[pl.BlockSpec((tm, d), row), pl.BlockSpec((1, d), lambda i, t: (0, 0)), pl.BlockSpec((tm, d), row)],
        out_shape=[jax.ShapeDtypeStruct((m, d), F32), jax.ShapeDtypeStruct((1, d), F32), jax.ShapeDtypeStruct((m, d), BF16)],
        scratch_shapes=[pltpu.VMEM((tm, d), F32)],
        compiler_params=_params("arbitrary", "arbitrary"),
    )(*args)


def _sigmoid(z):
    return 1.0 / (1.0 + jnp.exp(-z))


def _swiglu(gate_b, up_b):
    gate = gate_b.astype(F32)
    return (gate * _sigmoid(gate) * up_b.astype(F32)).astype(BF16)


def swiglu_mm_res(gu, w, res, alpha, name):
    m, f2 = gu.shape
    f, n = w.shape
    tm, tc = _tile(m, 256, 16), _tile(f, 1408)

    def body(gu_ref, w_ref, r_ref, o_ref):
        acc = jnp.zeros((tm, n), F32)
        for c0 in range(0, f, tc):
            act = _swiglu(gu_ref[:, c0:c0 + tc], gu_ref[:, f + c0:f + c0 + tc])
            acc = acc + _dot(act, w_ref[c0:c0 + tc, :])
        o_ref[...] = r_ref[...] + alpha * acc

    return pl.pallas_call(
        body, name=name, grid=(m // tm,),
        in_specs=[pl.BlockSpec((tm, f2), lambda i: (i, 0)), pl.BlockSpec((f, n), lambda i: (0, 0)),
                  pl.BlockSpec((tm, n), lambda i: (i, 0))],
        out_specs=pl.BlockSpec((tm, n), lambda i: (i, 0)),
        out_shape=jax.ShapeDtypeStruct((m, n), F32), compiler_params=_params("arbitrary"),
    )(gu, w, res)


def swiglu_mm_tn(gu, b, alpha, name):
    s, f2 = gu.shape
    f, n = f2 // 2, b.shape[1]
    ta, ts = _tile(f, 1408), _tile(s, 512, 16)
    steps, half = s // ts, f // ta

    def body(g_ref, u_ref, b_ref, o_ref, acc_ref):
        t = pl.program_id(1)

        @pl.when(t == 0)
        def _():
            acc_ref[...] = jnp.zeros_like(acc_ref)

        acc_ref[...] += _dot(_swiglu(g_ref[...], u_ref[...]), b_ref[...].astype(BF16), TN)

        @pl.when(t == steps - 1)
        def _():
            o_ref[...] = alpha * acc_ref[...]

    return pl.pallas_call(
        body, name=name, grid=(half, steps),
        in_specs=[pl.BlockSpec((ts, ta), lambda i, t: (t, i)), pl.BlockSpec((ts, ta), lambda i, t: (t, half + i)),
                  pl.BlockSpec((ts, n), lambda i, t: (t, 0))],
        out_specs=pl.BlockSpec((ta, n), lambda i, t: (i, 0)),
        out_shape=jax.ShapeDtypeStruct((f, n), F32),
        scratch_shapes=[pltpu.VMEM((ta, n), F32)],
        compiler_params=_params("arbitrary", "arbitrary"),
    )(gu, gu, b)


def mm_nt_swiglu_bwd(x, w, gu, alpha, name):
    m, d = x.shape
    f = w.shape[0]
    tm, tc = _tile(m, 256, 16), _tile(f, 1408)

    def body(x_ref, w_ref, gu_ref, o_ref):
        xb = x_ref[...].astype(BF16)
        for c0 in range(0, f, tc):
            d_act = alpha * _dot(xb, w_ref[c0:c0 + tc, :], NT)
            gate, up = gu_ref[:, c0:c0 + tc].astype(F32), gu_ref[:, f + c0:f + c0 + tc].astype(F32)
            sg = _sigmoid(gate)
            o_ref[:, c0:c0 + tc] = (d_act * up * (sg * (1.0 + gate * (1.0 - sg)))).astype(BF16)
            o_ref[:, f + c0:f + c0 + tc] = (d_act * (gate * sg)).astype(BF16)

    return pl.pallas_call(
        body, name=name, grid=(m // tm,),
        in_specs=[pl.BlockSpec((tm, d), lambda i: (i, 0)), pl.BlockSpec((f, d), lambda i: (0, 0)),
                  pl.BlockSpec((tm, 2 * f), lambda i: (i, 0))],
        out_specs=pl.BlockSpec((tm, 2 * f), lambda i: (i, 0)),
        out_shape=jax.ShapeDtypeStruct((m, 2 * f), BF16), compiler_params=_params("arbitrary"),
    )(x, w, gu)


def _gelu(x):
    return 0.5 * x * (1.0 + jnp.tanh(GELU_C * (x + GELU_A * x * x * x)))


def _gelu_grad(x):
    t = jnp.tanh(GELU_C * (x + GELU_A * x * x * x))
    return 0.5 * (1.0 + t) + 0.5 * x * (1.0 - t * t) * (GELU_C * (1.0 + 3.0 * GELU_A * x * x))


def _chunk_mask():
    row = lax.broadcasted_iota(jnp.int32, (GM_P, GM_P), 0)
    col = lax.broadcasted_iota(jnp.int32, (GM_P, GM_P), 1)
    return (col < GM_P // 2) | (row >= GM_P // 2)


def gmlp_fwd(proj, gain, w_s, bias, name):
    s, pw = proj.shape
    tm = _tile(s, 256, GM_P)

    def body(p_ref, gain_ref, w_ref, b_ref, o_ref):
        mask = _chunk_mask()
        u = _gelu(p_ref[:, :GM_W])
        v = _gelu(p_ref[:, GM_W:2 * GM_W])
        vn = ((v * _rms(v)) * gain_ref[...]).astype(BF16)
        for g in range(GM_GROUPS):
            wg = jnp.where(mask, w_ref[g], 0.0).astype(BF16)
            cols = slice(g * GM_P, (g + 1) * GM_P)
            for n in range(tm // GM_P):
                rows = slice(n * GM_P, (n + 1) * GM_P)
                mixed = _dot(wg, vn[rows, cols]) + b_ref[:, cols]
                o_ref[rows, cols] = (u[rows, cols] * mixed).astype(BF16)

    return pl.pallas_call(
        body, name=name, grid=(s // tm,),
        in_specs=[pl.BlockSpec((tm, pw), lambda i: (i, 0)), pl.BlockSpec((1, GM_W), lambda i: (0, 0)),
                  pl.BlockSpec((GM_GROUPS, GM_P, GM_P), lambda i: (0, 0, 0)), pl.BlockSpec((GM_P, GM_W), lambda i: (0, 0))],
        out_specs=pl.BlockSpec((tm, GM_W), lambda i: (i, 0)),
        out_shape=jax.ShapeDtypeStruct((s, GM_W), BF16), compiler_params=_params("arbitrary"),
    )(proj, gain, w_s, bias)


def gmlp_bwd(proj, dy, gain, w_s, bias, name):
    s, pw = proj.shape
    dw_total = dy.shape[1]
    tm = _tile(s, 256, GM_P)

    def body(p_ref, dy_ref, gain_ref, w_ref, b_ref, dp_ref, dw_ref, db_ref, dgain_ref, dvn_ref):
        @pl.when(pl.program_id(0) == 0)
        def _():
            dw_ref[...] = jnp.zeros_like(dw_ref)
            db_ref[...] = jnp.zeros_like(db_ref)
            dgain_ref[...] = jnp.zeros_like(dgain_ref)

        mask = _chunk_mask()
        pu = p_ref[:, :GM_W]
        pv = p_ref[:, GM_W:2 * GM_W]
        u = _gelu(pu)
        v = _gelu(pv)
        r = _rms(v)
        vhat = v * r
        gain = gain_ref[...]
        vn = (vhat * gain).astype(BF16)
        gu_grad = _gelu_grad(pu)
        for g in range(GM_GROUPS):
            wg = jnp.where(mask, w_ref[g], 0.0).astype(BF16)
            cols = slice(g * GM_P, (g + 1) * GM_P)
            dw_acc = jnp.zeros((GM_P, GM_P), F32)
            db_acc = jnp.zeros((GM_P, 1), F32)
            for n in range(tm // GM_P):
                rows = slice(n * GM_P, (n + 1) * GM_P)
                dyb = dy_ref[rows, cols].astype(F32)
                vnb = vn[rows, cols]
                mixed = _dot(wg, vnb) + b_ref[:, cols]
                dmixed = dyb * u[rows, cols]
                dmb = dmixed.astype(BF16)
                dp_ref[rows, cols] = (dyb * mixed * gu_grad[rows, cols]).astype(BF16)
                dw_acc = dw_acc + _dot(dmb, vnb, NT)
                db_acc = db_acc + jnp.sum(dmixed, axis=1, keepdims=True)
                dvn_ref[rows, cols] = _dot(wg, dmb, TN)
            dw_ref[g] += jnp.where(mask, dw_acc, 0.0)
            db_ref[g] += jnp.broadcast_to(db_acc, (GM_P, GM_P))
        dvn = dvn_ref[...]
        dgain_ref[...] += jnp.sum(dvn * vhat, axis=0, keepdims=True)
        dvhat = dvn * gain
        dv = r * (dvhat - vhat * jnp.mean(dvhat * vhat, axis=-1, keepdims=True))
        dp_ref[:, GM_W:] = (dv * _gelu_grad(pv)).astype(BF16)

    const3 = lambda i: (0, 0, 0)
    return pl.pallas_call(
        body, name=name, grid=(s // tm,),
        in_specs=[pl.BlockSpec((tm, pw), lambda i: (i, 0)), pl.BlockSpec((tm, dw_total), lambda i: (i, 0)),
                  pl.BlockSpec((1, GM_W), lambda i: (0, 0)), pl.BlockSpec((GM_GROUPS, GM_P, GM_P), const3),
                  pl.BlockSpec((GM_P, GM_W), lambda i: (0, 0))],
        out_specs=[pl.BlockSpec((tm, 2 * GM_W), lambda i: (i, 0)), pl.BlockSpec((GM_GROUPS, GM_P, GM_P), const3),
                   pl.BlockSpec((GM_GROUPS, GM_P, GM_P), const3), pl.BlockSpec((1, GM_W), lambda i: (0, 0))],
        out_shape=[jax.ShapeDtypeStruct((s, 2 * GM_W), BF16), jax.ShapeDtypeStruct((GM_GROUPS, GM_P, GM_P), F32),
                   jax.ShapeDtypeStruct((GM_GROUPS, GM_P, GM_P), F32), jax.ShapeDtypeStruct((1, GM_W), F32)],
        scratch_shapes=[pltpu.VMEM((tm, GM_W), F32)],
        compiler_params=_params("arbitrary"),
    )(proj, dy, gain, w_s, bias)


def _keep(mask, xb):
    return jnp.where(mask, xb.astype(F32), 0.0).astype(BF16)


def _head_masks(rows, width, heads):
    lane = lax.broadcasted_iota(jnp.int32, (rows, width), 1)
    return [(lane >= HEAD_DIM * h) & (lane < HEAD_DIM * (h + 1)) for h in range(heads)]


def _mem_probs(qh, k):
    sc = _dot(qh, k, NT) * QK_SCALE
    e = jnp.exp(sc - jnp.max(sc, axis=-1, keepdims=True))
    return e / jnp.sum(e, axis=-1, keepdims=True)


def mem_fwd(proj, q_blk, mem_kv, layer, name):
    s = proj.shape[0]
    n_mem = mem_kv.shape[0]
    tm = _tile(s, 512, 16)

    def body(q_ref, k_ref, v_ref, o_ref):
        q = q_ref[...].astype(BF16)
        k, v = k_ref[...], v_ref[...]
        out = jnp.zeros((tm, MEM_W), F32)
        for hm in _head_masks(tm, MEM_W, MEM_HEADS):
            p = _mem_probs(_keep(hm, q), k)
            out = out + jnp.where(hm, _dot(p.astype(BF16), v), 0.0)
        o_ref[...] = out.astype(BF16)

    return pl.pallas_call(
        body, name=name, grid=(s // tm,),
        in_specs=[pl.BlockSpec((tm, MEM_W), lambda i: (i, q_blk)), pl.BlockSpec((n_mem, MEM_W), lambda i: (0, 2 * layer)),
                  pl.BlockSpec((n_mem, MEM_W), lambda i: (0, 2 * layer + 1))],
        out_specs=pl.BlockSpec((tm, MEM_W), lambda i: (i, 0)),
        out_shape=jax.ShapeDtypeStruct((s, MEM_W), BF16), compiler_params=_params("arbitrary"),
    )(proj, mem_kv, mem_kv)


def mem_bwd(proj, q_blk, mem_kv, layer, dy, dy_blk, name):
    s = proj.shape[0]
    n_mem = mem_kv.shape[0]
    tm = _tile(s, 512, 16)

    def body(q_ref, k_ref, v_ref, dy_ref, dq_ref, dk_ref, dv_ref):
        @pl.when(pl.program_id(0) == 0)
        def _():
            dk_ref[...] = jnp.zeros_like(dk_ref)
            dv_ref[...] = jnp.zeros_like(dv_ref)

        q = q_ref[...].astype(BF16)
        k, v = k_ref[...], v_ref[...]
        dy = dy_ref[...]
        dq = jnp.zeros((tm, MEM_W), F32)
        dk = jnp.zeros((n_mem, MEM_W), F32)
        dv = jnp.zeros((n_mem, MEM_W), F32)
        for hm in _head_masks(tm, MEM_W, MEM_HEADS):
            qh = _keep(hm, q)
            dyh = _keep(hm, dy)
            p = _mem_probs(qh, k)
            dp = _dot(dyh, v, NT)
            dv = dv + _dot(p.astype(BF16), dyh, TN)
            ds = (p * (dp - jnp.sum(dp * p, axis=-1, keepdims=True)) * QK_SCALE).astype(BF16)
            dq = dq + jnp.where(hm, _dot(ds, k), 0.0)
            dk = dk + _dot(ds, qh, TN)
        dq_ref[...] = dq.astype(BF16)
        dk_ref[...] += dk
        dv_ref[...] += dv

    const = lambda i: (0, 0)
    return pl.pallas_call(
        body, name=name, grid=(s // tm,),
        in_specs=[pl.BlockSpec((tm, MEM_W), lambda i: (i, q_blk)), pl.BlockSpec((n_mem, MEM_W), lambda i: (0, 2 * layer)),
                  pl.BlockSpec((n_mem, MEM_W), lambda i: (0, 2 * layer + 1)), pl.BlockSpec((tm, MEM_W), lambda i: (i, dy_blk))],
        out_specs=[pl.BlockSpec((tm, MEM_W), lambda i: (i, 0)), pl.BlockSpec((n_mem, MEM_W), const),
                   pl.BlockSpec((n_mem, MEM_W), const)],
        out_shape=[jax.ShapeDtypeStruct((s, MEM_W), BF16), jax.ShapeDtypeStruct((n_mem, MEM_W), F32),
                   jax.ShapeDtypeStruct((n_mem, MEM_W), F32)],
        compiler_params=_params("arbitrary"),
    )(proj, mem_kv, mem_kv, dy)


SB_KEYS = 512
SB_SUB = SB_KEYS // SB_BLK
SB_QROWS = 256
SB_QB = SB_QROWS // SB_BLK
SB_CHAINS = 2 * SB_QB
LOG2E = 1.4426950408889634


def _split(xf):
    hi = xf.astype(BF16)
    return hi, (xf - hi.astype(F32)).astype(BF16)


def _sb_consts():
    row = lax.bitwise_and(lax.broadcasted_iota(jnp.int32, (2 * SB_BLK, 2 * SB_BLK), 0), SB_BLK - 1)
    col = lax.broadcasted_iota(jnp.int32, (2 * SB_BLK, 2 * SB_BLK), 1)
    ones = col >= SB_BLK
    after2 = jnp.where(ones | (row > col), 1.0, 0.0).astype(BF16)
    from2 = jnp.where(ones | (row >= col), 1.0, 0.0).astype(BF16)
    r = lax.broadcasted_iota(jnp.int32, (SB_BLK, SB_BLK), 0)
    c = lax.broadcasted_iota(jnp.int32, (SB_BLK, SB_BLK), 1)
    return after2, from2, c - r, [c < HEAD_DIM, c >= HEAD_DIM]


def _suffix(xf, tri2):
    hi, lo = _split(xf)
    return _dot(jnp.concatenate([hi, lo], axis=1), tri2)


def _sb_logs(z2, mask):
    l1 = jnp.log2(1.0 + jnp.exp2(-jnp.abs(z2)))
    log_one_minus = jnp.minimum(-z2, 0.0) - l1
    if mask is not None:
        log_one_minus = jnp.where(mask, log_one_minus, 0.0)
    return log_one_minus, jnp.minimum(z2, 0.0) - l1


def _sb_queries(q_ref, heads):
    q = q_ref[...].astype(F32) * QK_SCALE
    return [jnp.where(hm, q[r * SB_BLK:(r + 1) * SB_BLK], 0.0).astype(BF16) for r in range(SB_QB) for hm in heads]


def _sb_walk(i, block, state):
    assert SB_SUB == 2 * SB_QB
    own = lax.shift_right_logical(i * SB_QB, SB_SUB.bit_length() - 1)
    firsts = [[(v * SB_QB + r) * SB_BLK for r in range(SB_QB) for _ in range(2)] for v in range(2)]
    state = lax.cond(lax.bitwise_and(i, 1) == 0, lambda st: block(own, st, firsts[0]),
                     lambda st: block(own, st, firsts[1]), state)
    return lax.fori_loop(0, own, lambda t, st: block(own - 1 - t, st, None), state)


def _sb_tiles(first):
    out = []
    for c in reversed(range(SB_SUB)):
        for n in range(SB_CHAINS):
            if first is None or c * SB_BLK < first[n]:
                out.append((c, n, "before"))
            elif c * SB_BLK == first[n]:
                out.append((c, n, "diagonal"))
    return out


def _sb_heads_apart(stacked, heads, r):
    return jnp.where(heads[0], stacked[2 * r * SB_BLK:(2 * r + 1) * SB_BLK],
                     stacked[(2 * r + 1) * SB_BLK:(2 * r + 2) * SB_BLK])


def sb_fwd(proj, kv, name):
    s = proj.shape[0]
    assert s % SB_KEYS == 0 and SB_KEYS % SB_QROWS == 0

    def body(q_ref, k_ref, v_ref, o_ref):
        after2, _, col_minus_row, heads = _sb_consts()
        q_all = jnp.concatenate(_sb_queries(q_ref, heads), axis=0)
        key_before_query = col_minus_row < 0

        def block(j, state, first):
            runs, acc = list(state[0]), state[1]
            rows = pl.ds(pl.multiple_of(j * SB_KEYS, SB_KEYS), SB_KEYS)
            kb, vb = k_ref[rows, :], v_ref[rows, :]
            z = _dot(q_all, kb, NT) * LOG2E
            pend = {}
            parts = [[jnp.zeros((SB_BLK, SB_BLK), BF16)] * SB_SUB for _ in range(SB_CHAINS)]
            for c, n, where in _sb_tiles(first):
                mask = key_before_query if where == "diagonal" else None
                lom, lb = _sb_logs(z[n * SB_BLK:(n + 1) * SB_BLK, c * SB_BLK:(c + 1) * SB_BLK], mask)
                pend[c, n] = (lb, _suffix(lom, after2), mask)
            for c, n, _ in _sb_tiles(first):
                lb, r, mask = pend.pop((c, n))
                a = jnp.exp2(lb + r[:, :SB_BLK] + runs[n])
                if mask is not None:
                    a = jnp.where(mask, a, 0.0)
                parts[n][c] = a.astype(BF16)
                runs[n] = runs[n] + r[:, SB_BLK:]
            a_all = jnp.concatenate([jnp.concatenate(p, axis=1) for p in parts], axis=0)
            return tuple(runs), acc + _dot(a_all, vb)

        zero = jnp.zeros((SB_BLK, LANES), F32)
        state = _sb_walk(pl.program_id(1), block, ((zero,) * SB_CHAINS, jnp.zeros((SB_CHAINS * SB_BLK, LANES), F32)))
        for r in range(SB_QB):
            o_ref[r * SB_BLK:(r + 1) * SB_BLK, :] = _sb_heads_apart(state[1], heads, r)

    pairs = SB_W // LANES
    return pl.pallas_call(
        body, name=name, grid=(pairs, s // SB_QROWS),
        in_specs=[pl.BlockSpec((SB_QROWS, LANES), lambda p, i: (i, p)), pl.BlockSpec((s, LANES), lambda p, i: (0, p)),
                  pl.BlockSpec((s, LANES), lambda p, i: (0, pairs + p))],
        out_specs=pl.BlockSpec((SB_QROWS, LANES), lambda p, i: (i, p)),
        out_shape=jax.ShapeDtypeStruct((s, SB_W), F32),
        compiler_params=_params("arbitrary", "arbitrary"),
    )(proj, kv, kv)


def sb_bwd(proj, kv, out, dy, name):
    s = proj.shape[0]

    def body(q_ref, k_ref, v_ref, o_ref, do_ref, dq_ref, dk_ref, dv_ref):
        i = pl.program_id(1)

        @pl.when(i == 0)
        def _():
            dk_ref[...] = jnp.zeros_like(dk_ref)
            dv_ref[...] = jnp.zeros_like(dv_ref)

        after2, from2, col_minus_row, heads = _sb_consts()
        q_all = jnp.concatenate(_sb_queries(q_ref, heads), axis=0)
        key_before_query = col_minus_row < 0
        d_out = do_ref[...].astype(F32)
        prod = d_out * o_ref[...]
        dos, totals = [], []
        for r in range(SB_QB):
            rr = slice(r * SB_BLK, (r + 1) * SB_BLK)
            for hm in heads:
                dos.append(jnp.where(hm, d_out[rr], 0.0).astype(BF16))
                totals.append(jnp.broadcast_to(jnp.sum(jnp.where(hm, prod[rr], 0.0), axis=1, keepdims=True),
                                               (SB_BLK, SB_BLK)))
        do_all = jnp.concatenate(dos, axis=0)

        def block(j, state, first):
            runs, seens, dq = list(state[0]), list(state[1]), state[2]
            rows = pl.ds(pl.multiple_of(j * SB_KEYS, SB_KEYS), SB_KEYS)
            kb, vb = k_ref[rows, :], v_ref[rows, :]
            z = _dot(q_all, kb, NT) * LOG2E
            da = _dot(do_all, vb, NT)
            pend, pend2 = {}, {}
            a_parts = [[jnp.zeros((SB_BLK, SB_BLK), BF16)] * SB_SUB for _ in range(SB_CHAINS)]
            dz_parts = [[jnp.zeros((SB_BLK, SB_BLK), BF16)] * SB_SUB for _ in range(SB_CHAINS)]
            for c, n, where in _sb_tiles(first):
                mask = key_before_query if where == "diagonal" else None
                lom, lb = _sb_logs(z[n * SB_BLK:(n + 1) * SB_BLK, c * SB_BLK:(c + 1) * SB_BLK], mask)
                pend[c, n] = (lom, lb, _suffix(lom, after2), mask)
            for c, n, _ in _sb_tiles(first):
                lom, lb, r, mask = pend.pop((c, n))
                a = jnp.exp2(lb + r[:, :SB_BLK] + runs[n])
                if mask is not None:
                    a = jnp.where(mask, a, 0.0)
                runs[n] = runs[n] + r[:, SB_BLK:]
                ab = a.astype(BF16)
                a_parts[n][c] = ab
                dl = ab.astype(F32) * da[n * SB_BLK:(n + 1) * SB_BLK, c * SB_BLK:(c + 1) * SB_BLK]
                pend2[c, n] = (lom, lb, dl, _suffix(dl, from2), mask)
            for c, n, _ in _sb_tiles(first):
                lom, lb, dl, r2, mask = pend2.pop((c, n))
                d_lom = totals[n] - (r2[:, :SB_BLK] + seens[n])
                if mask is not None:
                    d_lom = jnp.where(mask, d_lom, 0.0)
                seens[n] = seens[n] + r2[:, SB_BLK:]
                dz_parts[n][c] = (dl * jnp.exp2(lom) - d_lom * jnp.exp2(lb)).astype(BF16)
            a_all = jnp.concatenate([jnp.concatenate(p, axis=1) for p in a_parts], axis=0)
            dz_all = jnp.concatenate([jnp.concatenate(p, axis=1) for p in dz_parts], axis=0)
            dv_ref[rows, :] += _dot(a_all, do_all, TN)
            dk_ref[rows, :] += _dot(dz_all, q_all, TN)
            return tuple(runs), tuple(seens), dq + _dot(dz_all, kb)

        zero = jnp.zeros((SB_BLK, LANES), F32)
        state = _sb_walk(i, block, ((zero,) * SB_CHAINS, (zero,) * SB_CHAINS,
                                    jnp.zeros((SB_CHAINS * SB_BLK, LANES), F32)))
        for r in range(SB_QB):
            dq_ref[r * SB_BLK:(r + 1) * SB_BLK, :] = (_sb_heads_apart(state[2], heads, r) * QK_SCALE).astype(BF16)

    pairs = SB_W // LANES
    blk = lambda p, i: (i, p)
    col = lambda p, i: (0, p)
    return pl.pallas_call(
        body, name=name, grid=(pairs, s // SB_QROWS),
        in_specs=[pl.BlockSpec((SB_QROWS, LANES), blk), pl.BlockSpec((s, LANES), col),
                  pl.BlockSpec((s, LANES), lambda p, i: (0, pairs + p)), pl.BlockSpec((SB_QROWS, LANES), blk),
                  pl.BlockSpec((SB_QROWS, LANES), blk)],
        out_specs=[pl.BlockSpec((SB_QROWS, LANES), blk), pl.BlockSpec((s, LANES), col), pl.BlockSpec((s, LANES), col)],
        out_shape=[jax.ShapeDtypeStruct((s, SB_W), BF16), jax.ShapeDtypeStruct((s, SB_W), F32),
                   jax.ShapeDtypeStruct((s, SB_W), F32)],
        compiler_params=_params("arbitrary", "arbitrary"),
    )(proj, kv, kv, out, dy)


def final_loss(x, g, target, name):
    s, d = x.shape
    tm = _tile(s, 256, 8)

    def body(x_ref, g_ref, t_ref, loss_ref, dx_ref, dg_ref):
        @pl.when(pl.program_id(0) == 0)
        def _():
            loss_ref[...] = jnp.zeros_like(loss_ref)
            dg_ref[...] = jnp.zeros_like(dg_ref)

        xf = x_ref[...]
        r = _rms(xf)
        xhat = xf * r
        gain = g_ref[...]
        diff = xhat * gain - t_ref[...]
        sq = jnp.sum(jnp.sum(diff * diff, axis=1, keepdims=True), axis=0, keepdims=True)
        loss_ref[...] += jnp.broadcast_to(sq, loss_ref.shape)
        dy = diff * (1.0 / d)
        dg_ref[...] += jnp.sum(dy * xhat, axis=0, keepdims=True)
        dxhat = dy * gain
        dx_ref[...] = r * (dxhat - xhat * jnp.mean(dxhat * xhat, axis=-1, keepdims=True))

    row = lambda i: (i, 0)
    const = lambda i: (0, 0)
    return pl.pallas_call(
        body, name=name, grid=(s // tm,),
        in_specs=[pl.BlockSpec((tm, d), row), pl.BlockSpec((1, d), const), pl.BlockSpec((tm, d), row)],
        out_specs=[pl.BlockSpec((8, LANES), const), pl.BlockSpec((tm, d), row), pl.BlockSpec((1, d), const)],
        out_shape=[jax.ShapeDtypeStruct((8, LANES), F32), jax.ShapeDtypeStruct((s, d), F32), jax.ShapeDtypeStruct((1, d), F32)],
        compiler_params=_params("arbitrary"),
    )(x, g, target)


def adamw(w, parts, m, v, name):
    rows, cols = w.shape
    k = parts.shape[0]
    tr = _tile(rows, 512, 16)
    c1, c2 = 1.0 - ADAM_B1 ** ADAM_STEP, 1.0 - ADAM_B2 ** ADAM_STEP

    def body(w_ref, p_ref, m_ref, v_ref, g_ref, d_ref, nm_ref, nv_ref):
        grad = p_ref[0].astype(F32)
        for s in range(1, k):
            grad = grad + p_ref[s].astype(F32)
        nm = ADAM_B1 * m_ref[...] + (1.0 - ADAM_B1) * grad
        nv = ADAM_B2 * v_ref[...] + (1.0 - ADAM_B2) * (grad * grad)
        g_ref[...] = grad
        d_ref[...] = -ADAM_LR * ((nm / c1) / (jnp.sqrt(nv / c2) + ADAM_EPS) + ADAM_WD * w_ref[...])
        nm_ref[...] = nm
        nv_ref[...] = nv

    spec = pl.BlockSpec((tr, cols), lambda i: (i, 0))
    shape = jax.ShapeDtypeStruct((rows, cols), F32)
    return pl.pallas_call(
        body, name=name, grid=(rows // tr,),
        in_specs=[spec, pl.BlockSpec((k, tr, cols), lambda i: (0, i, 0)), spec, spec],
        out_specs=[spec] * 4, out_shape=[shape] * 4,
        compiler_params=_params("arbitrary"),
    )(w, parts, m, v)


SHARDED = {"ffn1_w_gate": 2, "ffn1_w_up": 2, "ffn1_w_down": 1, "ffn2_w_gate": 2, "ffn2_w_up": 2, "ffn2_w_down": 1,
           "w_mem_kv": 1, "a_w_in": 2, "a_w_out": 1, "w_kv": 1, "b_w_in": 1, "b_w_out": 1}
SMALL = ["ffn1_norm", "mix_norm", "ffn2_norm", "mem_norm", "kv_norm", "final_norm", "a_v_norm", "a_w_spatial", "a_b_spatial"]
WEIGHTS = ["ffn1_norm", "ffn1_w_gate", "ffn1_w_up", "ffn1_w_down", "mix_norm", "ffn2_norm", "ffn2_w_gate", "ffn2_w_up",
           "ffn2_w_down", "mem_norm", "w_mem_kv", "a_w_in", "a_v_norm", "a_w_spatial", "a_b_spatial", "a_w_out", "kv_norm",
           "w_kv", "b_w_in", "b_w_out", "final_norm"]


def _gather_weights(shards):
    by_dev = exchange([shards[n].astype(BF16) for n in SHARDED], "all", True, "gather_weights")
    return {n: jnp.concatenate([blocks[d] for d in range(N_DEV)], axis=axis)
            for (n, axis), blocks in zip(SHARDED.items(), by_dev)}


def _scatter_grads(grads):
    by_dev = [jnp.stack(jnp.split(grads[n].astype(BF16), N_DEV, axis=axis)) for n, axis in SHARDED.items()]
    return dict(zip(SHARDED, exchange(by_dev, "all", False, "scatter_grads")))


def _all_sum(parts, name):
    flat = jnp.concatenate([p.reshape(-1) for p in parts])
    pad = (-flat.size) % (16 * LANES)
    buf = jnp.pad(flat, (0, pad)).reshape(-1, LANES)
    total = sum_leading(exchange([buf], "all", True, name)[0], F32, name + "_sum").reshape(-1)
    out, off = [], 0
    for p in parts:
        out.append(total[off:off + p.size].reshape(p.shape))
        off += p.size
    return out


def _device_index():
    return 4 * lax.axis_index("x") + 2 * lax.axis_index("y") + lax.axis_index("c")


def kernel(x, mem, ffn1_norm, ffn1_w_gate, ffn1_w_up, ffn1_w_down, mix_norm, ffn2_norm, ffn2_w_gate, ffn2_w_up, ffn2_w_down, mem_norm, w_mem_kv, a_w_in, a_v_norm, a_w_spatial, a_b_spatial, a_w_out, kv_norm, w_kv, b_w_in, b_w_out, final_norm, loss_target, m_ffn1_norm, m_ffn1_w_gate, m_ffn1_w_up, m_ffn1_w_down, m_mix_norm, m_ffn2_norm, m_ffn2_w_gate, m_ffn2_w_up, m_ffn2_w_down, m_mem_norm, m_w_mem_kv, m_a_w_in, m_a_v_norm, m_a_w_spatial, m_a_b_spatial, m_a_w_out, m_kv_norm, m_w_kv, m_b_w_in, m_b_w_out, m_final_norm, v_ffn1_norm, v_ffn1_w_gate, v_ffn1_w_up, v_ffn1_w_down, v_mix_norm, v_ffn2_norm, v_ffn2_w_gate, v_ffn2_w_up, v_ffn2_w_down, v_mem_norm, v_w_mem_kv, v_a_w_in, v_a_v_norm, v_a_w_spatial, v_a_b_spatial, v_a_w_out, v_kv_norm, v_w_kv, v_b_w_in, v_b_w_out, v_final_norm):
    weights = dict(ffn1_norm=ffn1_norm, ffn1_w_gate=ffn1_w_gate, ffn1_w_up=ffn1_w_up, ffn1_w_down=ffn1_w_down, mix_norm=mix_norm, ffn2_norm=ffn2_norm, ffn2_w_gate=ffn2_w_gate, ffn2_w_up=ffn2_w_up, ffn2_w_down=ffn2_w_down, mem_norm=mem_norm, w_mem_kv=w_mem_kv, a_w_in=a_w_in, a_v_norm=a_v_norm, a_w_spatial=a_w_spatial, a_b_spatial=a_b_spatial, a_w_out=a_w_out, kv_norm=kv_norm, w_kv=w_kv, b_w_in=b_w_in, b_w_out=b_w_out, final_norm=final_norm)
    mom1 = dict(ffn1_norm=m_ffn1_norm, ffn1_w_gate=m_ffn1_w_gate, ffn1_w_up=m_ffn1_w_up, ffn1_w_down=m_ffn1_w_down, mix_norm=m_mix_norm, ffn2_norm=m_ffn2_norm, ffn2_w_gate=m_ffn2_w_gate, ffn2_w_up=m_ffn2_w_up, ffn2_w_down=m_ffn2_w_down, mem_norm=m_mem_norm, w_mem_kv=m_w_mem_kv, a_w_in=m_a_w_in, a_v_norm=m_a_v_norm, a_w_spatial=m_a_w_spatial, a_b_spatial=m_a_b_spatial, a_w_out=m_a_w_out, kv_norm=m_kv_norm, w_kv=m_w_kv, b_w_in=m_b_w_in, b_w_out=m_b_w_out, final_norm=m_final_norm)
    mom2 = dict(ffn1_norm=v_ffn1_norm, ffn1_w_gate=v_ffn1_w_gate, ffn1_w_up=v_ffn1_w_up, ffn1_w_down=v_ffn1_w_down, mix_norm=v_mix_norm, ffn2_norm=v_ffn2_norm, ffn2_w_gate=v_ffn2_w_gate, ffn2_w_up=v_ffn2_w_up, ffn2_w_down=v_ffn2_w_down, mem_norm=v_mem_norm, w_mem_kv=v_w_mem_kv, a_w_in=v_a_w_in, a_v_norm=v_a_v_norm, a_w_spatial=v_a_w_spatial, a_b_spatial=v_a_b_spatial, a_w_out=v_a_w_out, kv_norm=v_kv_norm, w_kv=v_w_kv, b_w_in=v_b_w_in, b_w_out=v_b_w_out, final_norm=v_final_norm)

    dev = _device_index()
    xs, mem_in, target = x[0], mem[0], loss_target[0]
    d_model = xs.shape[1]
    shards = {n: weights[n] for n in SHARDED}
    full = _gather_weights(shards)
    vn_width = a_v_norm.shape[1]
    a_v_full = _all_sum([lax.dynamic_update_slice(jnp.zeros((N_A, N_DEV * vn_width), F32), a_v_norm, (0, dev * vn_width))],
                        "gather_v_norm")[0]

    row = lambda v: v.reshape(1, -1)
    w_gu = {f: jnp.concatenate([full[f + "_w_gate"], full[f + "_w_up"]], axis=2) for f in ("ffn1", "ffn2")}
    w_mem_cat = full["w_mem_kv"].transpose(1, 0, 2).reshape(d_model, -1)
    bias = [jnp.repeat(a_b_spatial[i].T, GM_P, axis=1) for i in range(N_A)]

    mem_kv, mem_h = norm_mm(mem_in, row(mem_norm), w_mem_cat, BF16, "mem_kv", emit_h=True)

    def ffn_fwd(xin, f, l):
        gu = norm_mm(xin, row(weights[f + "_norm"][l]), w_gu[f][l], BF16, "ffn_gu")
        return swiglu_mm_res(gu, full[f + "_w_down"][l], xin, 0.5, "ffn_down"), gu

    saved = []
    kv = x_kv = None
    cur = xs
    for l in range(DEPTH):
        st = {"x0": cur}
        if l == N_A:
            x_kv = cur
            kv = norm_mm(cur, row(kv_norm), full["w_kv"], BF16, "kv_proj")
        st["x1"], st["gu1"] = ffn_fwd(cur, "ffn1", l)
        if l < N_A:
            proj = norm_mm(st["x1"], row(mix_norm[l]), full["a_w_in"][l], F32, "a_proj")
            y_tok = gmlp_fwd(proj, row(a_v_full[l]), a_w_spatial[l], bias[l], "gmlp_fwd")
            y_mem = mem_fwd(proj, 2 * GM_W // MEM_W, mem_kv, l, "mem_fwd_a")
            w_out = full["a_w_out"][l]
        else:
            proj = norm_mm(st["x1"], row(mix_norm[l]), full["b_w_in"][l - N_A], BF16, "b_proj")
            st["sb_out"] = sb_fwd(proj, kv, "sb_fwd")
            y_tok = st["sb_out"].astype(BF16)
            y_mem = mem_fwd(proj, SB_W // MEM_W, mem_kv, l, "mem_fwd_b")
            w_out = full["b_w_out"][l - N_A]
        st["proj"] = proj
        st["y"] = jnp.concatenate([y_tok, y_mem], axis=1)
        st["x2"] = mm_res(st["y"], w_out, st["x1"], 1.0, "mix_out")
        cur, st["gu2"] = ffn_fwd(st["x2"], "ffn2", l)
        saved.append(st)

    loss_blk, dx, d_final = final_loss(cur, row(final_norm), target, "final_loss")
    loss = lax.psum(loss_blk[0, 0] * (0.5 / d_model), AXES)

    grads = {n: [None] * weights[n].shape[0] for n in WEIGHTS if weights[n].ndim >= 2 and n not in ("w_kv",)}
    grads["final_norm"] = d_final.reshape(-1)
    d_mem_kv = [None] * DEPTH
    d_kv = []

    def ffn_bwd(dx, xin, gu, f, l):
        d_gu = mm_nt_swiglu_bwd(dx, full[f + "_w_down"][l], gu, 0.5, "ffn_dgu")
        dx_new, d_gain, h = mm_nt_normbwd(d_gu, w_gu[f][l], xin, row(weights[f + "_norm"][l]), dx, "ffn_dx")
        d_wgu = mm_tn(h, d_gu, 1.0, "ffn_dwgu", tb_target=1408)
        half = d_wgu.shape[1] // 2
        grads[f + "_w_gate"][l], grads[f + "_w_up"][l] = d_wgu[:, :half], d_wgu[:, half:]
        grads[f + "_w_down"][l] = swiglu_mm_tn(gu, dx, 0.5, "ffn_dwdown")
        grads[f + "_norm"][l] = d_gain.reshape(-1)
        return dx_new

    for l in reversed(range(DEPTH)):
        st = saved[l]
        dx = ffn_bwd(dx, st["x2"], st["gu2"], "ffn2", l)
        proj = st["proj"]
        if l < N_A:
            w_in, w_out, key_in, key_out, idx = full["a_w_in"][l], full["a_w_out"][l], "a_w_in", "a_w_out", l
        else:
            w_in, w_out, key_in, key_out, idx = full["b_w_in"][l - N_A], full["b_w_out"][l - N_A], "b_w_in", "b_w_out", l - N_A
        dy = mm_nt(dx, w_out, 1.0, "mix_dy")
        grads[key_out][idx] = mm_tn(st["y"], dx, 1.0, "mix_dwout", tb_target=1024)
        if l < N_A:
            d_uv, d_ws, d_bs, d_vgain = gmlp_bwd(proj, dy, row(a_v_full[l]), a_w_spatial[l], bias[l], "gmlp_bwd")
            grads["a_w_spatial"][l], grads["a_b_spatial"][l], grads["a_v_norm"][l] = d_ws, d_bs[:, :, 0], d_vgain.reshape(-1)
            d_q, d_k, d_v = mem_bwd(proj, 2 * GM_W // MEM_W, mem_kv, l, dy, GM_W // MEM_W, "mem_bwd_a")
            d_proj = jnp.concatenate([d_uv, d_q], axis=1)
        else:
            d_qsb, d_ksb, d_vsb = sb_bwd(proj, kv, st["sb_out"], dy, "sb_bwd")
            d_kv.append(jnp.concatenate([d_ksb, d_vsb], axis=1))
            d_q, d_k, d_v = mem_bwd(proj, SB_W // MEM_W, mem_kv, l, dy, SB_W // MEM_W, "mem_bwd_b")
            d_proj = jnp.concatenate([d_qsb, d_q], axis=1)
        d_mem_kv[l] = jnp.concatenate([d_k, d_v], axis=1)
        dx, d_gain, h = mm_nt_normbwd(d_proj, w_in, st["x1"], row(mix_norm[l]), dx, "mix_dx")
        grads["mix_norm"][l] = d_gain.reshape(-1)
        grads[key_in][idx] = mm_tn(h, d_proj, 1.0, "mix_dwin")
        dx = ffn_bwd(dx, st["x0"], st["gu1"], "ffn1", l)
        if l == N_A:
            d_kv_b = sum_leading(jnp.stack(d_kv), BF16, "kv_dsum")
            dx, d_gain, h = mm_nt_normbwd(d_kv_b, full["w_kv"], x_kv, row(kv_norm), dx, "kv_dx")
            grads["kv_norm"] = d_gain.reshape(-1)
            grads["w_kv"] = mm_tn(h, d_kv_b, 1.0, "kv_dw")

    d_mem_all = jnp.concatenate(d_mem_kv, axis=1).astype(BF16)
    _, d_gain, _ = mm_nt_normbwd(d_mem_all, w_mem_cat, mem_in, row(mem_norm), None, "mem_dnorm")
    grads["mem_norm"] = d_gain.reshape(-1)
    d_wmem = mm_tn(mem_h, d_mem_all, 1.0, "mem_dw")
    grads["w_mem_kv"] = d_wmem.reshape(d_model, DEPTH, -1).transpose(1, 0, 2)
    grads = {n: (jnp.stack(g) if isinstance(g, list) else g) for n, g in grads.items()}

    parts = _scatter_grads({n: grads[n] for n in SHARDED})
    for n, g in zip(SMALL, _all_sum([grads[n] for n in SMALL], "sum_small")):
        parts[n] = g[None]
    parts["a_v_norm"] = lax.dynamic_slice(parts["a_v_norm"], (0, 0, dev * vn_width), (1,) + a_v_norm.shape)

    reduced, deltas, new_m, new_v = {}, {}, {}, {}
    for n in WEIGHTS:
        w = weights[n]
        view = (lambda a: a.reshape(-1, a.shape[-1]))
        res = adamw(view(w), parts[n].reshape(parts[n].shape[0], -1, w.shape[-1]), view(mom1[n]), view(mom2[n]), "adamw")
        reduced[n], deltas[n], new_m[n], new_v[n] = [r.reshape(w.shape) for r in res]

    return (loss, dx[None], *[reduced[n] for n in WEIGHTS], *[deltas[n] for n in WEIGHTS],
            *[new_m[n] for n in WEIGHTS], *[new_v[n] for n in WEIGHTS])
```

```python
import functools

import jax
import jax.numpy as jnp
from jax import lax
from jax.experimental import pallas as pl
from jax.experimental.pallas import tpu as pltpu

F32, BF16 = jnp.float32, jnp.bfloat16
MESH_ID = pl.DeviceIdType.MESH
AXES = ("x", "y", "c")
N_DEV = 8

EPS = 1e-6
DEPTH, N_A = 4, 2
GM_W, GM_GROUPS, GM_P = 768, 6, 128
MEM_W, MEM_HEADS, HEAD_DIM = 256, 4, 64
SB_W, SB_BLK = 768, 128
LANES = 128
QK_SCALE = HEAD_DIM ** -0.5
GELU_C, GELU_A = 0.7978845608028654, 0.044715

ADAM_LR, ADAM_B1, ADAM_B2, ADAM_EPS, ADAM_WD, ADAM_STEP = 0.001, 0.9, 0.999, 1e-08, 0.01, 10

VMEM_LIMIT = 56 * 1024 * 1024
PACK_COLS = 512

NT = (((1,), (1,)), ((), ()))
TN = (((0,), (0,)), ((), ()))


def _params(*sem):
    return pltpu.CompilerParams(dimension_semantics=sem, vmem_limit_bytes=VMEM_LIMIT)


def _tile(n, target, mult=LANES):
    best = None
    for t in range(mult, min(n, target) + 1, mult):
        if n % t == 0:
            best = t
    return best if best is not None else n


def _dot(a, b, dims=None):
    if dims is None:
        return jnp.dot(a, b, preferred_element_type=F32)
    return lax.dot_general(a, b, dims, preferred_element_type=F32)


def exchange(srcs, group, same_src, name, split=False):
    size = {"pair": 2, "quad": 4, "all": 8}[group]
    n = len(srcs)
    chunk_shapes = [tuple(s.shape) if same_src else tuple(s.shape[1:]) for s in srcs]
    pieces = [cs[0] if split else 1 for cs in chunk_shapes]
    n_dma = sum(pieces)

    def body(*refs):
        src_refs, out_refs = refs[:n], refs[n:2 * n]
        send_sems, recv_sems, local_sems = refs[2 * n:]
        x, y, c = lax.axis_index("x"), lax.axis_index("y"), lax.axis_index("c")
        if group == "pair":
            me, dev = c, lambda p: (x, y, p)
        elif group == "quad":
            me, dev = 2 * x + y, lambda p: (p // 2, p % 2, c)
        else:
            me, dev = 4 * x + 2 * y + c, lambda p: (p // 4, (p // 2) % 2, p % 2)

        def chunk(t, idx):
            return src_refs[t] if same_src else src_refs[t].at[idx]

        def copies(k, idx, slot, peer):
            out, w = [], k * n_dma
            for t in range(n):
                src, dst = chunk(t, idx), out_refs[t].at[slot]
                for s_ref, d_ref in ([(src.at[u], dst.at[u]) for u in range(pieces[t])] if split else [(src, dst)]):
                    out.append(pltpu.make_async_remote_copy(
                        src_ref=s_ref, dst_ref=d_ref, send_sem=send_sems.at[w], recv_sem=recv_sems.at[w],
                        device_id=dev(peer), device_id_type=MESH_ID))
                    w += 1
            return out

        local = [pltpu.make_async_copy(chunk(t, me), out_refs[t].at[me], local_sems.at[t]) for t in range(n)]
        for cp in local:
            cp.start()
        sends = []
        for k in range(1, size):
            peer = (me + k) % size
            sends += copies(k, peer, me, peer)
        for cp in sends:
            cp.start()
        for k in range(1, size):
            sender = (me + size - k) % size
            for cp in copies(k, me, sender, sender):
                cp.wait_recv()
        for cp in sends:
            cp.wait_send()
        for cp in local:
            cp.wait()

    hbm = pl.BlockSpec(memory_space=pltpu.HBM)
    return pl.pallas_call(
        body, name=name,
        out_shape=[jax.ShapeDtypeStruct((size,) + cs, s.dtype) for cs, s in zip(chunk_shapes, srcs)],
        in_specs=[hbm] * n, out_specs=[hbm] * n,
        scratch_shapes=[pltpu.SemaphoreType.DMA((size * n_dma,)), pltpu.SemaphoreType.DMA((size * n_dma,)),
                        pltpu.SemaphoreType.DMA((n,))],
    )(*srcs)


HBM_SPEC = pl.BlockSpec(memory_space=pltpu.HBM)
SEM_SPEC = pl.BlockSpec(memory_space=pltpu.SEMAPHORE)
DATAFLOW = pltpu.SideEffectType.DATAFLOW_SIDE_EFFECTING


def _all_devices():
    me = 4 * lax.axis_index("x") + 2 * lax.axis_index("y") + lax.axis_index("c")
    return me, lambda p: (p // 4, (p // 2) % 2, p % 2)


def exchange_start(srcs, same_src, name):
    n = len(srcs)
    chunk_shapes = [tuple(s.shape) if same_src else tuple(s.shape[1:]) for s in srcs]
    srcs = [pltpu.with_memory_space_constraint(s, pltpu.HBM) for s in srcs]
    lands = [pltpu.with_memory_space_constraint(lax.empty((N_DEV,) + cs, s.dtype), pltpu.HBM)
             for cs, s in zip(chunk_shapes, srcs)]

    def body(*refs):
        src_refs, land_refs, send_sems, recv_sems, token = refs[:n], refs[n:2 * n], refs[2 * n], refs[2 * n + 1], refs[-1]
        me, dev = _all_devices()
        for k in range(1, N_DEV):
            peer = (me + k) % N_DEV
            for t in range(n):
                w = (k - 1) * n + t
                pltpu.make_async_remote_copy(
                    src_ref=src_refs[t] if same_src else src_refs[t].at[peer], dst_ref=land_refs[t].at[me],
                    send_sem=send_sems.at[w], recv_sem=recv_sems.at[w], device_id=dev(peer), device_id_type=MESH_ID).start()
        token[...] = jnp.zeros_like(token)

    n_copies = (N_DEV - 1) * n
    out = pl.pallas_call(
        body, name=name,
        out_shape=(pltpu.SemaphoreType.DMA((n_copies,)), pltpu.SemaphoreType.DMA((n_copies,)),
                   *[pltpu.HBM(a.shape, a.dtype) for a in srcs + lands], jax.ShapeDtypeStruct((8, LANES), F32)),
        in_specs=[HBM_SPEC] * (2 * n),
        out_specs=(SEM_SPEC, SEM_SPEC, *[HBM_SPEC] * (2 * n), pl.BlockSpec(memory_space=pltpu.VMEM)),
        input_output_aliases={t: 2 + t for t in range(2 * n)},
        compiler_params=pltpu.CompilerParams(has_side_effects=DATAFLOW),
    )(*srcs, *lands)
    return (out[0], out[1], list(out[2:2 + n]), list(out[2 + n:2 + 2 * n])), out[-1]


def exchange_wait(handle, after, same_src, name):
    send_sems, recv_sems, srcs, lands = handle
    n = len(srcs)

    def body(*refs):
        src_refs, land_refs, send_sems, recv_sems = refs[:n], refs[n:2 * n], refs[2 * n], refs[2 * n + 1]
        me, dev = _all_devices()
        for k in range(1, N_DEV):
            sender = (me + N_DEV - k) % N_DEV
            for t in range(n):
                w = (k - 1) * n + t
                copy = pltpu.make_async_remote_copy(
                    src_ref=src_refs[t] if same_src else src_refs[t].at[me], dst_ref=land_refs[t].at[sender],
                    send_sem=send_sems.at[w], recv_sem=recv_sems.at[w], device_id=dev(sender), device_id_type=MESH_ID)
                copy.wait_send()
                copy.wait_recv()

    out = pl.pallas_call(
        body, name=name,
        out_shape=[pltpu.HBM(a.shape, a.dtype) for a in srcs + lands],
        in_specs=[HBM_SPEC] * (2 * n) + [SEM_SPEC, SEM_SPEC, pl.BlockSpec(memory_space=pl.ANY)],
        out_specs=[HBM_SPEC] * (2 * n),
        input_output_aliases={t: t for t in range(2 * n)},
        compiler_params=pltpu.CompilerParams(has_side_effects=DATAFLOW),
    )(*srcs, *lands, send_sems, recv_sems, after)
    return list(out[n:])


def sum_leading(parts, out_dtype, name):
    k, rows, cols = parts.shape
    tr = _tile(rows, 512, 16)

    def body(p_ref, o_ref):
        acc = p_ref[0].astype(F32)
        for s in range(1, k):
            acc = acc + p_ref[s].astype(F32)
        o_ref[...] = acc.astype(o_ref.dtype)

    return pl.pallas_call(
        body, name=name, grid=(rows // tr,),
        in_specs=[pl.BlockSpec((k, tr, cols), lambda i: (0, i, 0))],
        out_specs=pl.BlockSpec((tr, cols), lambda i: (i, 0)),
        out_shape=jax.ShapeDtypeStruct((rows, cols), out_dtype),
        compiler_params=_params("arbitrary"),
    )(parts)


def _rms(xf):
    return lax.rsqrt(jnp.mean(xf * xf, axis=-1, keepdims=True) + EPS)


def norm_mm(x, g, w, out_dtype, name, emit_h=False):
    m, d = x.shape
    n = w.shape[1]
    tm, tn = _tile(m, 1024, 8), _tile(n, 1408)

    def body(x_ref, g_ref, w_ref, o_ref, *rest):
        h_ref = rest[-1]

        @pl.when(pl.program_id(1) == 0)
        def _():
            xf = x_ref[...]
            hb = ((xf * _rms(xf)) * g_ref[...]).astype(BF16)
            h_ref[...] = hb
            if emit_h:
                rest[0][...] = hb

        o_ref[...] = _dot(h_ref[...], w_ref[...]).astype(o_ref.dtype)

    out_shape = [jax.ShapeDtypeStruct((m, n), out_dtype)]
    out_specs = [pl.BlockSpec((tm, tn), lambda i, j: (i, j))]
    if emit_h:
        out_shape.append(jax.ShapeDtypeStruct((m, d), BF16))
        out_specs.append(pl.BlockSpec((tm, d), lambda i, j: (i, 0)))
    res = pl.pallas_call(
        body, name=name, grid=(m // tm, n // tn),
        in_specs=[pl.BlockSpec((tm, d), lambda i, j: (i, 0)), pl.BlockSpec((1, d), lambda i, j: (0, 0)),
                  pl.BlockSpec((d, tn), lambda i, j: (0, j))],
        out_specs=out_specs, out_shape=out_shape,
        scratch_shapes=[pltpu.VMEM((tm, d), BF16)],
        compiler_params=_params("arbitrary", "arbitrary"),
    )(x, g, w)
    return res if emit_h else res[0]


def mm_res(a, w, res, alpha, name):
    m, k = a.shape
    n = w.shape[1]
    tm, tn = _tile(m, 1024, 8), _tile(n, 1024)

    def body(a_ref, w_ref, r_ref, o_ref):
        o_ref[...] = r_ref[...] + alpha * _dot(a_ref[...], w_ref[...])

    return pl.pallas_call(
        body, name=name, grid=(m // tm, n // tn),
        in_specs=[pl.BlockSpec((tm, k), lambda i, j: (i, 0)), pl.BlockSpec((k, tn), lambda i, j: (0, j)),
                  pl.BlockSpec((tm, tn), lambda i, j: (i, j))],
        out_specs=pl.BlockSpec((tm, tn), lambda i, j: (i, j)),
        out_shape=jax.ShapeDtypeStruct((m, n), F32),
        compiler_params=_params("arbitrary", "arbitrary"),
    )(a, w, res)


def mm_nt(x, w, alpha, name):
    m, d = x.shape
    n = w.shape[0]
    tm, tn = _tile(m, 1024, 8), _tile(n, 1408)

    def body(x_ref, w_ref, o_ref, xb_ref):
        @pl.when(pl.program_id(1) == 0)
        def _():
            xb_ref[...] = x_ref[...].astype(BF16)

        o_ref[...] = (alpha * _dot(xb_ref[...], w_ref[...], NT)).astype(o_ref.dtype)

    return pl.pallas_call(
        body, name=name, grid=(m // tm, n // tn),
        in_specs=[pl.BlockSpec((tm, d), lambda i, j: (i, 0)), pl.BlockSpec((tn, d), lambda i, j: (j, 0))],
        out_specs=pl.BlockSpec((tm, tn), lambda i, j: (i, j)),
        out_shape=jax.ShapeDtypeStruct((m, n), BF16),
        scratch_shapes=[pltpu.VMEM((tm, d), BF16)],
        compiler_params=_params("arbitrary", "arbitrary"),
    )(x, w)


def mm_tn(a, b, alpha, name, ta_target=1024, tb_target=512):
    s, ka = a.shape
    nb = b.shape[1]
    ta, tb, ts = _tile(ka, ta_target), _tile(nb, tb_target), _tile(s, 1024, 16)
    steps = s // ts

    def body(a_ref, b_ref, o_ref, acc_ref):
        t = pl.program_id(2)

        @pl.when(t == 0)
        def _():
            acc_ref[...] = jnp.zeros_like(acc_ref)

        acc_ref[...] += _dot(a_ref[...].astype(BF16), b_ref[...].astype(BF16), TN)

        @pl.when(t == steps - 1)
        def _():
            o_ref[...] = alpha * acc_ref[...]

    return pl.pallas_call(
        body, name=name, grid=(ka // ta, nb // tb, steps),
        in_specs=[pl.BlockSpec((ts, ta), lambda i, j, t: (t, i)), pl.BlockSpec((ts, tb), lambda i, j, t: (t, j))],
        out_specs=pl.BlockSpec((ta, tb), lambda i, j, t: (i, j)),
        out_shape=jax.ShapeDtypeStruct((ka, nb), F32),
        scratch_shapes=[pltpu.VMEM((ta, tb), F32)],
        compiler_params=_params("arbitrary", "arbitrary", "arbitrary"),
    )(a, b)


def mm_nt_normbwd(dy, w, x, g, res, name):
    m, n = dy.shape
    d = w.shape[0]
    tm, tk = _tile(m, 1024, 8), _tile(n, 1408)
    steps = n // tk
    has_res = res is not None

    def body(*refs):
        if has_res:
            dy_ref, w_ref, x_ref, g_ref, r_ref, dx_ref, dg_ref, h_ref, acc_ref = refs
        else:
            dy_ref, w_ref, x_ref, g_ref, dx_ref, dg_ref, h_ref, acc_ref = refs
        i, t = pl.program_id(0), pl.program_id(1)

        @pl.when(t == 0)
        def _():
            acc_ref[...] = jnp.zeros_like(acc_ref)

        @pl.when((t == 0) & (i == 0))
        def _():
            dg_ref[...] = jnp.zeros_like(dg_ref)

        acc_ref[...] += _dot(dy_ref[...], w_ref[...], NT)

        @pl.when(t == steps - 1)
        def _():
            xf = x_ref[...]
            r = _rms(xf)
            xhat = xf * r
            dh = acc_ref[...]
            gain = g_ref[...]
            dg_ref[...] += jnp.sum(dh * xhat, axis=0, keepdims=True)
            dxhat = dh * gain
            dx = r * (dxhat - xhat * jnp.mean(dxhat * xhat, axis=-1, keepdims=True))
            dx_ref[...] = (r_ref[...] + dx) if has_res else dx
            h_ref[...] = (xhat * gain).astype(BF16)

    row = lambda i, t: (i, 0)
    in_specs = [pl.BlockSpec((tm, tk), lambda i, t: (i, t)), pl.BlockSpec((d, tk), lambda i, t: (0, t)),
                pl.BlockSpec((tm, d), row), pl.BlockSpec((1, d), lambda i, t: (0, 0))]
    args = [dy, w, x, g]
    if has_res:
        in_specs.append(pl.BlockSpec((tm, d), row))
        args.append(res)
    return pl.pallas_call(
        body, name=name, grid=(m // tm, steps),
        in_specs=in_specs,
        out_specs=[pl.BlockSpec((tm, d), row), pl.BlockSpec((1, d), lambda i, t: (0, 0)), pl.BlockSpec((tm, d), row)],
        out_shape=[jax.ShapeDtypeStruct((m, d), F32), jax.ShapeDtypeStruct((1, d), F32), jax.ShapeDtypeStruct((m, d), BF16)],
        scratch_shapes=[pltpu.VMEM((tm, d), F32)],
        compiler_params=_params("arbitrary", "arbitrary"),
    )(*args)


def _sigmoid(z):
    return 1.0 / (1.0 + jnp.exp(-z))


def _swiglu(gate_b, up_b):
    gate = gate_b.astype(F32)
    return (gate * _sigmoid(gate) * up_b.astype(F32)).astype(BF16)


def swiglu_mm_res(gu, w, res, alpha, name):
    m, f2 = gu.shape
    f, n = w.shape
    tm, tc = _tile(m, 256, 16), _tile(f, 1408)

    def body(gu_ref, w_ref, r_ref, o_ref):
        acc = jnp.zeros((tm, n), F32)
        for c0 in range(0, f, tc):
            act = _swiglu(gu_ref[:, c0:c0 + tc], gu_ref[:, f + c0:f + c0 + tc])
            acc = acc + _dot(act, w_ref[c0:c0 + tc, :])
        o_ref[...] = r_ref[...] + alpha * acc

    return pl.pallas_call(
        body, name=name, grid=(m // tm,),
        in_specs=[pl.BlockSpec((tm, f2), lambda i: (i, 0)), pl.BlockSpec((f, n), lambda i: (0, 0)),
                  pl.BlockSpec((tm, n), lambda i: (i, 0))],
        out_specs=pl.BlockSpec((tm, n), lambda i: (i, 0)),
        out_shape=jax.ShapeDtypeStruct((m, n), F32), compiler_params=_params("arbitrary"),
    )(gu, w, res)


def swiglu_mm_tn(gu, b, alpha, name):
    s, f2 = gu.shape
    f, n = f2 // 2, b.shape[1]
    ta, ts = _tile(f, 1408), _tile(s, 512, 16)
    steps, half = s // ts, f // ta

    def body(g_ref, u_ref, b_ref, o_ref, acc_ref):
        t = pl.program_id(1)

        @pl.when(t == 0)
        def _():
            acc_ref[...] = jnp.zeros_like(acc_ref)

        acc_ref[...] += _dot(_swiglu(g_ref[...], u_ref[...]), b_ref[...].astype(BF16), TN)

        @pl.when(t == steps - 1)
        def _():
            o_ref[...] = alpha * acc_ref[...]

    return pl.pallas_call(
        body, name=name, grid=(half, steps),
        in_specs=[pl.BlockSpec((ts, ta), lambda i, t: (t, i)), pl.BlockSpec((ts, ta), lambda i, t: (t, half + i)),
                  pl.BlockSpec((ts, n), lambda i, t: (t, 0))],
        out_specs=pl.BlockSpec((ta, n), lambda i, t: (i, 0)),
        out_shape=jax.ShapeDtypeStruct((f, n), F32),
        scratch_shapes=[pltpu.VMEM((ta, n), F32)],
        compiler_params=_params("arbitrary", "arbitrary"),
    )(gu, gu, b)


def mm_nt_swiglu_bwd(x, w, gu, alpha, name):
    m, d = x.shape
    f = w.shape[0]
    tm, tc = _tile(m, 256, 16), _tile(f, 1408)

    def body(x_ref, w_ref, gu_ref, o_ref):
        xb = x_ref[...].astype(BF16)
        for c0 in range(0, f, tc):
            d_act = alpha * _dot(xb, w_ref[c0:c0 + tc, :], NT)
            gate, up = gu_ref[:, c0:c0 + tc].astype(F32), gu_ref[:, f + c0:f + c0 + tc].astype(F32)
            sg = _sigmoid(gate)
            o_ref[:, c0:c0 + tc] = (d_act * up * (sg * (1.0 + gate * (1.0 - sg)))).astype(BF16)
            o_ref[:, f + c0:f + c0 + tc] = (d_act * (gate * sg)).astype(BF16)

    return pl.pallas_call(
        body, name=name, grid=(m // tm,),
        in_specs=[pl.BlockSpec((tm, d), lambda i: (i, 0)), pl.BlockSpec((f, d), lambda i: (0, 0)),
                  pl.BlockSpec((tm, 2 * f), lambda i: (i, 0))],
        out_specs=pl.BlockSpec((tm, 2 * f), lambda i: (i, 0)),
        out_shape=jax.ShapeDtypeStruct((m, 2 * f), BF16), compiler_params=_params("arbitrary"),
    )(x, w, gu)


def _gelu(x):
    return 0.5 * x * (1.0 + jnp.tanh(GELU_C * (x + GELU_A * x * x * x)))


def _gelu_grad(x):
    t = jnp.tanh(GELU_C * (x + GELU_A * x * x * x))
    return 0.5 * (1.0 + t) + 0.5 * x * (1.0 - t * t) * (GELU_C * (1.0 + 3.0 * GELU_A * x * x))


def _chunk_mask():
    row = lax.broadcasted_iota(jnp.int32, (GM_P, GM_P), 0)
    col = lax.broadcasted_iota(jnp.int32, (GM_P, GM_P), 1)
    return (col < GM_P // 2) | (row >= GM_P // 2)


def gmlp_fwd(proj, gain, w_s, bias, name):
    s, pw = proj.shape
    tm = _tile(s, 256, GM_P)

    def body(p_ref, gain_ref, w_ref, b_ref, o_ref):
        mask = _chunk_mask()
        u = _gelu(p_ref[:, :GM_W])
        v = _gelu(p_ref[:, GM_W:2 * GM_W])
        vn = ((v * _rms(v)) * gain_ref[...]).astype(BF16)
        for g in range(GM_GROUPS):
            wg = jnp.where(mask, w_ref[g], 0.0).astype(BF16)
            cols = slice(g * GM_P, (g + 1) * GM_P)
            for n in range(tm // GM_P):
                rows = slice(n * GM_P, (n + 1) * GM_P)
                mixed = _dot(wg, vn[rows, cols]) + b_ref[:, cols]
                o_ref[rows, cols] = (u[rows, cols] * mixed).astype(BF16)

    return pl.pallas_call(
        body, name=name, grid=(s // tm,),
        in_specs=[pl.BlockSpec((tm, pw), lambda i: (i, 0)), pl.BlockSpec((1, GM_W), lambda i: (0, 0)),
                  pl.BlockSpec((GM_GROUPS, GM_P, GM_P), lambda i: (0, 0, 0)), pl.BlockSpec((GM_P, GM_W), lambda i: (0, 0))],
        out_specs=pl.BlockSpec((tm, GM_W), lambda i: (i, 0)),
        out_shape=jax.ShapeDtypeStruct((s, GM_W), BF16), compiler_params=_params("arbitrary"),
    )(proj, gain, w_s, bias)


def gmlp_bwd(proj, dy, gain, w_s, bias, name):
    s, pw = proj.shape
    dw_total = dy.shape[1]
    tm = _tile(s, 256, GM_P)

    def body(p_ref, dy_ref, gain_ref, w_ref, b_ref, dp_ref, dw_ref, db_ref, dgain_ref, dvn_ref):
        @pl.when(pl.program_id(0) == 0)
        def _():
            dw_ref[...] = jnp.zeros_like(dw_ref)
            db_ref[...] = jnp.zeros_like(db_ref)
            dgain_ref[...] = jnp.zeros_like(dgain_ref)

        mask = _chunk_mask()
        pu = p_ref[:, :GM_W]
        pv = p_ref[:, GM_W:2 * GM_W]
        u = _gelu(pu)
        v = _gelu(pv)
        r = _rms(v)
        vhat = v * r
        gain = gain_ref[...]
        vn = (vhat * gain).astype(BF16)
        gu_grad = _gelu_grad(pu)
        for g in range(GM_GROUPS):
            wg = jnp.where(mask, w_ref[g], 0.0).astype(BF16)
            cols = slice(g * GM_P, (g + 1) * GM_P)
            dw_acc = jnp.zeros((GM_P, GM_P), F32)
            db_acc = jnp.zeros((GM_P, 1), F32)
            for n in range(tm // GM_P):
                rows = slice(n * GM_P, (n + 1) * GM_P)
                dyb = dy_ref[rows, cols].astype(F32)
                vnb = vn[rows, cols]
                mixed = _dot(wg, vnb) + b_ref[:, cols]
                dmixed = dyb * u[rows, cols]
                dmb = dmixed.astype(BF16)
                dp_ref[rows, cols] = (dyb * mixed * gu_grad[rows, cols]).astype(BF16)
                dw_acc = dw_acc + _dot(dmb, vnb, NT)
                db_acc = db_acc + jnp.sum(dmixed, axis=1, keepdims=True)
                dvn_ref[rows, cols] = _dot(wg, dmb, TN)
            dw_ref[g] += jnp.where(mask, dw_acc, 0.0)
            db_ref[g] += jnp.broadcast_to(db_acc, (GM_P, GM_P))
        dvn = dvn_ref[...]
        dgain_ref[...] += jnp.sum(dvn * vhat, axis=0, keepdims=True)
        dvhat = dvn * gain
        dv = r * (dvhat - vhat * jnp.mean(dvhat * vhat, axis=-1, keepdims=True))
        dp_ref[:, GM_W:] = (dv * _gelu_grad(pv)).astype(BF16)

    const3 = lambda i: (0, 0, 0)
    return pl.pallas_call(
        body, name=name, grid=(s // tm,),
        in_specs=[pl.BlockSpec((tm, pw), lambda i: (i, 0)), pl.BlockSpec((tm, dw_total), lambda i: (i, 0)),
                  pl.BlockSpec((1, GM_W), lambda i: (0, 0)), pl.BlockSpec((GM_GROUPS, GM_P, GM_P), const3),
                  pl.BlockSpec((GM_P, GM_W), lambda i: (0, 0))],
        out_specs=[pl.BlockSpec((tm, 2 * GM_W), lambda i: (i, 0)), pl.BlockSpec((GM_GROUPS, GM_P, GM_P), const3),
                   pl.BlockSpec((GM_GROUPS, GM_P, GM_P), const3), pl.BlockSpec((1, GM_W), lambda i: (0, 0))],
        out_shape=[jax.ShapeDtypeStruct((s, 2 * GM_W), BF16), jax.ShapeDtypeStruct((GM_GROUPS, GM_P, GM_P), F32),
                   jax.ShapeDtypeStruct((GM_GROUPS, GM_P, GM_P), F32), jax.ShapeDtypeStruct((1, GM_W), F32)],
        scratch_shapes=[pltpu.VMEM((tm, GM_W), F32)],
        compiler_params=_params("arbitrary"),
    )(proj, dy, gain, w_s, bias)


def _keep(mask, xb):
    return jnp.where(mask, xb.astype(F32), 0.0).astype(BF16)


def _head_masks(rows, width, heads):
    lane = lax.broadcasted_iota(jnp.int32, (rows, width), 1)
    return [(lane >= HEAD_DIM * h) & (lane < HEAD_DIM * (h + 1)) for h in range(heads)]


def _mem_probs(qh, k):
    sc = _dot(qh, k, NT) * QK_SCALE
    e = jnp.exp(sc - jnp.max(sc, axis=-1, keepdims=True))
    return e / jnp.sum(e, axis=-1, keepdims=True)


def mem_fwd(proj, q_blk, mem_kv, layer, name):
    s = proj.shape[0]
    n_mem = mem_kv.shape[0]
    tm = _tile(s, 512, 16)

    def body(q_ref, k_ref, v_ref, o_ref):
        q = q_ref[...].astype(BF16)
        k, v = k_ref[...], v_ref[...]
        out = jnp.zeros((tm, MEM_W), F32)
        for hm in _head_masks(tm, MEM_W, MEM_HEADS):
            p = _mem_probs(_keep(hm, q), k)
            out = out + jnp.where(hm, _dot(p.astype(BF16), v), 0.0)
        o_ref[...] = out.astype(BF16)

    return pl.pallas_call(
        body, name=name, grid=(s // tm,),
        in_specs=[pl.BlockSpec((tm, MEM_W), lambda i: (i, q_blk)), pl.BlockSpec((n_mem, MEM_W), lambda i: (0, 2 * layer)),
                  pl.BlockSpec((n_mem, MEM_W), lambda i: (0, 2 * layer + 1))],
        out_specs=pl.BlockSpec((tm, MEM_W), lambda i: (i, 0)),
        out_shape=jax.ShapeDtypeStruct((s, MEM_W), BF16), compiler_params=_params("arbitrary"),
    )(proj, mem_kv, mem_kv)


def mem_bwd(proj, q_blk, mem_kv, layer, dy, dy_blk, name):
    s = proj.shape[0]
    n_mem = mem_kv.shape[0]
    tm = _tile(s, 512, 16)

    def body(q_ref, k_ref, v_ref, dy_ref, dq_ref, dk_ref, dv_ref):
        @pl.when(pl.program_id(0) == 0)
        def _():
            dk_ref[...] = jnp.zeros_like(dk_ref)
            dv_ref[...] = jnp.zeros_like(dv_ref)

        q = q_ref[...].astype(BF16)
        k, v = k_ref[...], v_ref[...]
        dy = dy_ref[...]
        dq = jnp.zeros((tm, MEM_W), F32)
        dk = jnp.zeros((n_mem, MEM_W), F32)
        dv = jnp.zeros((n_mem, MEM_W), F32)
        for hm in _head_masks(tm, MEM_W, MEM_HEADS):
            qh = _keep(hm, q)
            dyh = _keep(hm, dy)
            p = _mem_probs(qh, k)
            dp = _dot(dyh, v, NT)
            dv = dv + _dot(p.astype(BF16), dyh, TN)
            ds = (p * (dp - jnp.sum(dp * p, axis=-1, keepdims=True)) * QK_SCALE).astype(BF16)
            dq = dq + jnp.where(hm, _dot(ds, k), 0.0)
            dk = dk + _dot(ds, qh, TN)
        dq_ref[...] = dq.astype(BF16)
        dk_ref[...] += dk
        dv_ref[...] += dv

    const = lambda i: (0, 0)
    return pl.pallas_call(
        body, name=name, grid=(s // tm,),
        in_specs=[pl.BlockSpec((tm, MEM_W), lambda i: (i, q_blk)), pl.BlockSpec((n_mem, MEM_W), lambda i: (0, 2 * layer)),
                  pl.BlockSpec((n_mem, MEM_W), lambda i: (0, 2 * layer + 1)), pl.BlockSpec((tm, MEM_W), lambda i: (i, dy_blk))],
        out_specs=[pl.BlockSpec((tm, MEM_W), lambda i: (i, 0)), pl.BlockSpec((n_mem, MEM_W), const),
                   pl.BlockSpec((n_mem, MEM_W), const)],
        out_shape=[jax.ShapeDtypeStruct((s, MEM_W), BF16), jax.ShapeDtypeStruct((n_mem, MEM_W), F32),
                   jax.ShapeDtypeStruct((n_mem, MEM_W), F32)],
        compiler_params=_params("arbitrary"),
    )(proj, mem_kv, mem_kv, dy)


SB_KEYS = 512
SB_SUB = SB_KEYS // SB_BLK
SB_QROWS = 256
SB_QB = SB_QROWS // SB_BLK
SB_CHAINS = 2 * SB_QB
LOG2E = 1.4426950408889634


def _split(xf):
    hi = xf.astype(BF16)
    return hi, (xf - hi.astype(F32)).astype(BF16)


def _sb_consts():
    row = lax.bitwise_and(lax.broadcasted_iota(jnp.int32, (2 * SB_BLK, 2 * SB_BLK), 0), SB_BLK - 1)
    col = lax.broadcasted_iota(jnp.int32, (2 * SB_BLK, 2 * SB_BLK), 1)
    ones = col >= SB_BLK
    after2 = jnp.where(ones | (row > col), 1.0, 0.0).astype(BF16)
    from2 = jnp.where(ones | (row >= col), 1.0, 0.0).astype(BF16)
    r = lax.broadcasted_iota(jnp.int32, (SB_BLK, SB_BLK), 0)
    c = lax.broadcasted_iota(jnp.int32, (SB_BLK, SB_BLK), 1)
    return after2, from2, c - r, [c < HEAD_DIM, c >= HEAD_DIM]


def _suffix(xf, tri2):
    hi, lo = _split(xf)
    return _dot(jnp.concatenate([hi, lo], axis=1), tri2)


def _sb_logs(z2, mask):
    l1 = jnp.log2(1.0 + jnp.exp2(-jnp.abs(z2)))
    log_one_minus = jnp.minimum(-z2, 0.0) - l1
    if mask is not None:
        log_one_minus = jnp.where(mask, log_one_minus, 0.0)
    return log_one_minus, jnp.minimum(z2, 0.0) - l1


def _sb_queries(q_ref, heads):
    q = q_ref[...].astype(F32) * QK_SCALE
    return [jnp.where(hm, q[r * SB_BLK:(r + 1) * SB_BLK], 0.0).astype(BF16) for r in range(SB_QB) for hm in heads]


def _sb_walk(i, block, state):
    assert SB_SUB == 2 * SB_QB
    own = lax.shift_right_logical(i * SB_QB, SB_SUB.bit_length() - 1)
    firsts = [[(v * SB_QB + r) * SB_BLK for r in range(SB_QB) for _ in range(2)] for v in range(2)]
    state = lax.cond(lax.bitwise_and(i, 1) == 0, lambda st: block(own, st, firsts[0]),
                     lambda st: block(own, st, firsts[1]), state)
    return lax.fori_loop(0, own, lambda t, st: block(own - 1 - t, st, None), state)


def _sb_tiles(first):
    out = []
    for c in reversed(range(SB_SUB)):
        for n in range(SB_CHAINS):
            if first is None or c * SB_BLK < first[n]:
                out.append((c, n, "before"))
            elif c * SB_BLK == first[n]:
                out.append((c, n, "diagonal"))
    return out


def _sb_heads_apart(stacked, heads, r):
    return jnp.where(heads[0], stacked[2 * r * SB_BLK:(2 * r + 1) * SB_BLK],
                     stacked[(2 * r + 1) * SB_BLK:(2 * r + 2) * SB_BLK])


def sb_fwd(proj, kv, name):
    s = proj.shape[0]
    assert s % SB_KEYS == 0 and SB_KEYS % SB_QROWS == 0

    def body(q_ref, k_ref, v_ref, o_ref):
        after2, _, col_minus_row, heads = _sb_consts()
        q_all = jnp.concatenate(_sb_queries(q_ref, heads), axis=0)
        key_before_query = col_minus_row < 0

        def block(j, state, first):
            runs, acc = list(state[0]), state[1]
            rows = pl.ds(pl.multiple_of(j * SB_KEYS, SB_KEYS), SB_KEYS)
            kb, vb = k_ref[rows, :], v_ref[rows, :]
            z = _dot(q_all, kb, NT) * LOG2E
            pend = {}
            parts = [[jnp.zeros((SB_BLK, SB_BLK), BF16)] * SB_SUB for _ in range(SB_CHAINS)]
            for c, n, where in _sb_tiles(first):
                mask = key_before_query if where == "diagonal" else None
                lom, lb = _sb_logs(z[n * SB_BLK:(n + 1) * SB_BLK, c * SB_BLK:(c + 1) * SB_BLK], mask)
                pend[c, n] = (lb, _suffix(lom, after2), mask)
            for c, n, _ in _sb_tiles(first):
                lb, r, mask = pend.pop((c, n))
                a = jnp.exp2(lb + r[:, :SB_BLK] + runs[n])
                if mask is not None:
                    a = jnp.where(mask, a, 0.0)
                parts[n][c] = a.astype(BF16)
                runs[n] = runs[n] + r[:, SB_BLK:]
            a_all = jnp.concatenate([jnp.concatenate(p, axis=1) for p in parts], axis=0)
            return tuple(runs), acc + _dot(a_all, vb)

        zero = jnp.zeros((SB_BLK, LANES), F32)
        state = _sb_walk(pl.program_id(1), block, ((zero,) * SB_CHAINS, jnp.zeros((SB_CHAINS * SB_BLK, LANES), F32)))
        for r in range(SB_QB):
            o_ref[r * SB_BLK:(r + 1) * SB_BLK, :] = _sb_heads_apart(state[1], heads, r)

    pairs = SB_W // LANES
    return pl.pallas_call(
        body, name=name, grid=(pairs, s // SB_QROWS),
        in_specs=[pl.BlockSpec((SB_QROWS, LANES), lambda p, i: (i, p)), pl.BlockSpec((s, LANES), lambda p, i: (0, p)),
                  pl.BlockSpec((s, LANES), lambda p, i: (0, pairs + p))],
        out_specs=pl.BlockSpec((SB_QROWS, LANES), lambda p, i: (i, p)),
        out_shape=jax.ShapeDtypeStruct((s, SB_W), F32),
        compiler_params=_params("arbitrary", "arbitrary"),
    )(proj, kv, kv)


def sb_bwd(proj, kv, out, dy, name):
    s = proj.shape[0]

    def body(q_ref, k_ref, v_ref, o_ref, do_ref, dq_ref, dk_ref, dv_ref):
        i = pl.program_id(1)

        @pl.when(i == 0)
        def _():
            dk_ref[...] = jnp.zeros_like(dk_ref)
            dv_ref[...] = jnp.zeros_like(dv_ref)

        after2, from2, col_minus_row, heads = _sb_consts()
        q_all = jnp.concatenate(_sb_queries(q_ref, heads), axis=0)
        key_before_query = col_minus_row < 0
        d_out = do_ref[...].astype(F32)
        prod = d_out * o_ref[...]
        dos, totals = [], []
        for r in range(SB_QB):
            rr = slice(r * SB_BLK, (r + 1) * SB_BLK)
            for hm in heads:
                dos.append(jnp.where(hm, d_out[rr], 0.0).astype(BF16))
                totals.append(jnp.broadcast_to(jnp.sum(jnp.where(hm, prod[rr], 0.0), axis=1, keepdims=True),
                                               (SB_BLK, SB_BLK)))
        do_all = jnp.concatenate(dos, axis=0)

        def block(j, state, first):
            runs, seens, dq = list(state[0]), list(state[1]), state[2]
            rows = pl.ds(pl.multiple_of(j * SB_KEYS, SB_KEYS), SB_KEYS)
            kb, vb = k_ref[rows, :], v_ref[rows, :]
            z = _dot(q_all, kb, NT) * LOG2E
            da = _dot(do_all, vb, NT)
            pend, pend2 = {}, {}
            a_parts = [[jnp.zeros((SB_BLK, SB_BLK), BF16)] * SB_SUB for _ in range(SB_CHAINS)]
            dz_parts = [[jnp.zeros((SB_BLK, SB_BLK), BF16)] * SB_SUB for _ in range(SB_CHAINS)]
            for c, n, where in _sb_tiles(first):
                mask = key_before_query if where == "diagonal" else None
                lom, lb = _sb_logs(z[n * SB_BLK:(n + 1) * SB_BLK, c * SB_BLK:(c + 1) * SB_BLK], mask)
                pend[c, n] = (lom, lb, _suffix(lom, after2), mask)
            for c, n, _ in _sb_tiles(first):
                lom, lb, r, mask = pend.pop((c, n))
                a = jnp.exp2(lb + r[:, :SB_BLK] + runs[n])
                if mask is not None:
                    a = jnp.where(mask, a, 0.0)
                runs[n] = runs[n] + r[:, SB_BLK:]
                ab = a.astype(BF16)
                a_parts[n][c] = ab
                dl = ab.astype(F32) * da[n * SB_BLK:(n + 1) * SB_BLK, c * SB_BLK:(c + 1) * SB_BLK]
                pend2[c, n] = (lom, lb, dl, _suffix(dl, from2), mask)
            for c, n, _ in _sb_tiles(first):
                lom, lb, dl, r2, mask = pend2.pop((c, n))
                d_lom = totals[n] - (r2[:, :SB_BLK] + seens[n])
                if mask is not None:
                    d_lom = jnp.where(mask, d_lom, 0.0)
                seens[n] = seens[n] + r2[:, SB_BLK:]
                dz_parts[n][c] = (dl * jnp.exp2(lom) - d_lom * jnp.exp2(lb)).astype(BF16)
            a_all = jnp.concatenate([jnp.concatenate(p, axis=1) for p in a_parts], axis=0)
            dz_all = jnp.concatenate([jnp.concatenate(p, axis=1) for p in dz_parts], axis=0)
            dv_ref[rows, :] += _dot(a_all, do_all, TN)
            dk_ref[rows, :] += _dot(dz_all, q_all, TN)
            return tuple(runs), tuple(seens), dq + _dot(dz_all, kb)

        zero = jnp.zeros((SB_BLK, LANES), F32)
        state = _sb_walk(i, block, ((zero,) * SB_CHAINS, (zero,) * SB_CHAINS,
                                    jnp.zeros((SB_CHAINS * SB_BLK, LANES), F32)))
        for r in range(SB_QB):
            dq_ref[r * SB_BLK:(r + 1) * SB_BLK, :] = (_sb_heads_apart(state[2], heads, r) * QK_SCALE).astype(BF16)

    pairs = SB_W // LANES
    blk = lambda p, i: (i, p)
    col = lambda p, i: (0, p)
    return pl.pallas_call(
        body, name=name, grid=(pairs, s // SB_QROWS),
        in_specs=[pl.BlockSpec((SB_QROWS, LANES), blk), pl.BlockSpec((s, LANES), col),
                  pl.BlockSpec((s, LANES), lambda p, i: (0, pairs + p)), pl.BlockSpec((SB_QROWS, LANES), blk),
                  pl.BlockSpec((SB_QROWS, LANES), blk)],
        out_specs=[pl.BlockSpec((SB_QROWS, LANES), blk), pl.BlockSpec((s, LANES), col), pl.BlockSpec((s, LANES), col)],
        out_shape=[jax.ShapeDtypeStruct((s, SB_W), BF16), jax.ShapeDtypeStruct((s, SB_W), F32),
                   jax.ShapeDtypeStruct((s, SB_W), F32)],
        compiler_params=_params("arbitrary", "arbitrary"),
    )(proj, kv, kv, out, dy)


def final_loss(x, g, target, name):
    s, d = x.shape
    tm = _tile(s, 256, 8)

    def body(x_ref, g_ref, t_ref, loss_ref, dx_ref, dg_ref):
        @pl.when(pl.program_id(0) == 0)
        def _():
            loss_ref[...] = jnp.zeros_like(loss_ref)
            dg_ref[...] = jnp.zeros_like(dg_ref)

        xf = x_ref[...]
        r = _rms(xf)
        xhat = xf * r
        gain = g_ref[...]
        diff = xhat * gain - t_ref[...]
        sq = jnp.sum(jnp.sum(diff * diff, axis=1, keepdims=True), axis=0, keepdims=True)
        loss_ref[...] += jnp.broadcast_to(sq, loss_ref.shape)
        dy = diff * (1.0 / d)
        dg_ref[...] += jnp.sum(dy * xhat, axis=0, keepdims=True)
        dxhat = dy * gain
        dx_ref[...] = r * (dxhat - xhat * jnp.mean(dxhat * xhat, axis=-1, keepdims=True))

    row = lambda i: (i, 0)
    const = lambda i: (0, 0)
    return pl.pallas_call(
        body, name=name, grid=(s // tm,),
        in_specs=[pl.BlockSpec((tm, d), row), pl.BlockSpec((1, d), const), pl.BlockSpec((tm, d), row)],
        out_specs=[pl.BlockSpec((8, LANES), const), pl.BlockSpec((tm, d), row), pl.BlockSpec((1, d), const)],
        out_shape=[jax.ShapeDtypeStruct((8, LANES), F32), jax.ShapeDtypeStruct((s, d), F32), jax.ShapeDtypeStruct((1, d), F32)],
        compiler_params=_params("arbitrary"),
    )(x, g, target)


def adamw(w, parts, m, v, name):
    rows, cols = w.shape
    k = parts.shape[0]
    tr = _tile(rows, 512, 16)
    c1, c2 = 1.0 - ADAM_B1 ** ADAM_STEP, 1.0 - ADAM_B2 ** ADAM_STEP

    def body(w_ref, p_ref, m_ref, v_ref, g_ref, d_ref, nm_ref, nv_ref):
        grad = p_ref[0].astype(F32)
        for s in range(1, k):
            grad = grad + p_ref[s].astype(F32)
        nm = ADAM_B1 * m_ref[...] + (1.0 - ADAM_B1) * grad
        nv = ADAM_B2 * v_ref[...] + (1.0 - ADAM_B2) * (grad * grad)
        g_ref[...] = grad
        d_ref[...] = -ADAM_LR * ((nm / c1) / (jnp.sqrt(nv / c2) + ADAM_EPS) + ADAM_WD * w_ref[...])
        nm_ref[...] = nm
        nv_ref[...] = nv

    spec = pl.BlockSpec((tr, cols), lambda i: (i, 0))
    shape = jax.ShapeDtypeStruct((rows, cols), F32)
    return pl.pallas_call(
        body, name=name, grid=(rows // tr,),
        in_specs=[spec, pl.BlockSpec((k, tr, cols), lambda i: (0, i, 0)), spec, spec],
        out_specs=[spec] * 4, out_shape=[shape] * 4,
        compiler_params=_params("arbitrary"),
    )(w, parts, m, v)


SHARDED = {"ffn1_w_gate": 2, "ffn1_w_up": 2, "ffn1_w_down": 1, "ffn2_w_gate": 2, "ffn2_w_up": 2, "ffn2_w_down": 1,
           "w_mem_kv": 1, "a_w_in": 2, "a_w_out": 1, "w_kv": 1, "b_w_in": 1, "b_w_out": 1}
SMALL = ["ffn1_norm", "mix_norm", "ffn2_norm", "mem_norm", "kv_norm", "final_norm", "a_v_norm", "a_w_spatial", "a_b_spatial"]
WEIGHTS = ["ffn1_norm", "ffn1_w_gate", "ffn1_w_up", "ffn1_w_down", "mix_norm", "ffn2_norm", "ffn2_w_gate", "ffn2_w_up",
           "ffn2_w_down", "mem_norm", "w_mem_kv", "a_w_in", "a_v_norm", "a_w_spatial", "a_b_spatial", "a_w_out", "kv_norm",
           "w_kv", "b_w_in", "b_w_out", "final_norm"]


FFN_W = ["ffn1_w_gate", "ffn1_w_up", "ffn1_w_down", "ffn2_w_gate", "ffn2_w_up", "ffn2_w_down"]
GROUPS = [
    {**{n: (0, 1) for n in FFN_W}, "w_mem_kv": (0, DEPTH), "a_w_in": (0, 1), "a_w_out": (0, 1)},
    {**{n: (1, 2) for n in FFN_W}, "a_w_in": (1, 2), "a_w_out": (1, 2)},
    {**{n: (2, DEPTH) for n in FFN_W}, "w_kv": None, "b_w_in": (0, 2), "b_w_out": (0, 2)},
]


def _take(a, rng):
    return a if rng is None else a[rng[0]:rng[1]]


def _group_blocks(shards, g):
    return [_take(shards[n], rng).astype(BF16) for n, rng in GROUPS[g].items()]


def _group_grads(grads, g):
    out = []
    for n, rng in GROUPS[g].items():
        whole = jnp.stack(grads[n][rng[0]:rng[1]]) if isinstance(grads[n], list) else _take(grads[n], rng)
        out.append(jnp.stack(jnp.split(whole.astype(BF16), N_DEV, axis=SHARDED[n])))
    return out


def _own_filled(lands, srcs, same_src, dev):
    return [lax.dynamic_update_index_in_dim(land, src if same_src else lax.dynamic_index_in_dim(src, dev, 0, False), dev, 0)
            for land, src in zip(lands, srcs)]


def _whole_weights(g, by_dev):
    return {n: jnp.concatenate([blocks[d] for d in range(N_DEV)], axis=SHARDED[n]) for n, blocks in zip(GROUPS[g], by_dev)}


def _all_sum(parts, name):
    flat = jnp.concatenate([p.reshape(-1) for p in parts])
    pad = (-flat.size) % (16 * LANES)
    buf = jnp.pad(flat, (0, pad)).reshape(-1, LANES)
    total = sum_leading(exchange([buf], "all", True, name)[0], F32, name + "_sum").reshape(-1)
    out, off = [], 0
    for p in parts:
        out.append(total[off:off + p.size].reshape(p.shape))
        off += p.size
    return out


def _device_index():
    return 4 * lax.axis_index("x") + 2 * lax.axis_index("y") + lax.axis_index("c")


def kernel(x, mem, ffn1_norm, ffn1_w_gate, ffn1_w_up, ffn1_w_down, mix_norm, ffn2_norm, ffn2_w_gate, ffn2_w_up, ffn2_w_down, mem_norm, w_mem_kv, a_w_in, a_v_norm, a_w_spatial, a_b_spatial, a_w_out, kv_norm, w_kv, b_w_in, b_w_out, final_norm, loss_target, m_ffn1_norm, m_ffn1_w_gate, m_ffn1_w_up, m_ffn1_w_down, m_mix_norm, m_ffn2_norm, m_ffn2_w_gate, m_ffn2_w_up, m_ffn2_w_down, m_mem_norm, m_w_mem_kv, m_a_w_in, m_a_v_norm, m_a_w_spatial, m_a_b_spatial, m_a_w_out, m_kv_norm, m_w_kv, m_b_w_in, m_b_w_out, m_final_norm, v_ffn1_norm, v_ffn1_w_gate, v_ffn1_w_up, v_ffn1_w_down, v_mix_norm, v_ffn2_norm, v_ffn2_w_gate, v_ffn2_w_up, v_ffn2_w_down, v_mem_norm, v_w_mem_kv, v_a_w_in, v_a_v_norm, v_a_w_spatial, v_a_b_spatial, v_a_w_out, v_kv_norm, v_w_kv, v_b_w_in, v_b_w_out, v_final_norm):
    weights = dict(ffn1_norm=ffn1_norm, ffn1_w_gate=ffn1_w_gate, ffn1_w_up=ffn1_w_up, ffn1_w_down=ffn1_w_down, mix_norm=mix_norm, ffn2_norm=ffn2_norm, ffn2_w_gate=ffn2_w_gate, ffn2_w_up=ffn2_w_up, ffn2_w_down=ffn2_w_down, mem_norm=mem_norm, w_mem_kv=w_mem_kv, a_w_in=a_w_in, a_v_norm=a_v_norm, a_w_spatial=a_w_spatial, a_b_spatial=a_b_spatial, a_w_out=a_w_out, kv_norm=kv_norm, w_kv=w_kv, b_w_in=b_w_in, b_w_out=b_w_out, final_norm=final_norm)
    mom1 = dict(ffn1_norm=m_ffn1_norm, ffn1_w_gate=m_ffn1_w_gate, ffn1_w_up=m_ffn1_w_up, ffn1_w_down=m_ffn1_w_down, mix_norm=m_mix_norm, ffn2_norm=m_ffn2_norm, ffn2_w_gate=m_ffn2_w_gate, ffn2_w_up=m_ffn2_w_up, ffn2_w_down=m_ffn2_w_down, mem_norm=m_mem_norm, w_mem_kv=m_w_mem_kv, a_w_in=m_a_w_in, a_v_norm=m_a_v_norm, a_w_spatial=m_a_w_spatial, a_b_spatial=m_a_b_spatial, a_w_out=m_a_w_out, kv_norm=m_kv_norm, w_kv=m_w_kv, b_w_in=m_b_w_in, b_w_out=m_b_w_out, final_norm=m_final_norm)
    mom2 = dict(ffn1_norm=v_ffn1_norm, ffn1_w_gate=v_ffn1_w_gate, ffn1_w_up=v_ffn1_w_up, ffn1_w_down=v_ffn1_w_down, mix_norm=v_mix_norm, ffn2_norm=v_ffn2_norm, ffn2_w_gate=v_ffn2_w_gate, ffn2_w_up=v_ffn2_w_up, ffn2_w_down=v_ffn2_w_down, mem_norm=v_mem_norm, w_mem_kv=v_w_mem_kv, a_w_in=v_a_w_in, a_v_norm=v_a_v_norm, a_w_spatial=v_a_w_spatial, a_b_spatial=v_a_b_spatial, a_w_out=v_a_w_out, kv_norm=v_kv_norm, w_kv=v_w_kv, b_w_in=v_b_w_in, b_w_out=v_b_w_out, final_norm=v_final_norm)

    dev = _device_index()
    xs, mem_in, target = x[0], mem[0], loss_target[0]
    d_model = xs.shape[1]
    shards = {n: weights[n] for n in SHARDED}
    blocks = [_group_blocks(shards, g) for g in range(len(GROUPS))]
    wholes = [_whole_weights(0, exchange(blocks[0], "all", True, "gather_first")), None, None]
    gather_1, started_1 = exchange_start(blocks[1], True, "gather_second_start")
    gather_2, started_2 = exchange_start(blocks[2], True, "gather_third_start")
    tie = started_1[0, 0] + started_2[0, 0]
    vn_width = a_v_norm.shape[1]
    a_v_full = _all_sum([lax.dynamic_update_slice(jnp.zeros((N_A, N_DEV * vn_width), F32), a_v_norm, (0, dev * vn_width))],
                        "gather_v_norm")[0]

    def whole(n, l=None):
        for g, grp in enumerate(GROUPS):
            if n in grp and (grp[n] is None or grp[n][0] <= l < grp[n][1]):
                return wholes[g][n] if grp[n] is None else wholes[g][n][l - grp[n][0]]

    gate_up = {}

    def whole_gu(f, l):
        if (f, l) not in gate_up:
            gate_up[f, l] = jnp.concatenate([whole(f + "_w_gate", l), whole(f + "_w_up", l)], axis=1)
        return gate_up[f, l]

    row = lambda v: v.reshape(1, -1)
    w_mem_cat = wholes[0]["w_mem_kv"].transpose(1, 0, 2).reshape(d_model, -1)
    bias = [jnp.repeat(a_b_spatial[i].T, GM_P, axis=1) for i in range(N_A)]

    mem_kv, mem_h = norm_mm(mem_in, row(mem_norm) + tie, w_mem_cat, BF16, "mem_kv", emit_h=True)

    def ffn_fwd(xin, f, l, tie=0.0):
        gu = norm_mm(xin, row(weights[f + "_norm"][l]) + tie, whole_gu(f, l), BF16, "ffn_gu")
        return swiglu_mm_res(gu, whole(f + "_w_down", l), xin, 0.5, "ffn_down"), gu

    saved = []
    kv = x_kv = None
    cur = xs
    for l in range(DEPTH):
        st = {"x0": cur}
        if l == 1:
            wholes[1] = _whole_weights(1, _own_filled(exchange_wait(gather_1, cur, True, "gather_second_wait"),
                                                      blocks[1], True, dev))
        if l == N_A:
            wholes[2] = _whole_weights(2, _own_filled(exchange_wait(gather_2, cur, True, "gather_third_wait"),
                                                      blocks[2], True, dev))
            x_kv = cur
            kv = norm_mm(cur, row(kv_norm), whole("w_kv"), BF16, "kv_proj")
        st["x1"], st["gu1"] = ffn_fwd(cur, "ffn1", l, tie if l == 0 else 0.0)
        if l < N_A:
            proj = norm_mm(st["x1"], row(mix_norm[l]), whole("a_w_in", l), F32, "a_proj")
            y_tok = gmlp_fwd(proj, row(a_v_full[l]), a_w_spatial[l], bias[l], "gmlp_fwd")
            y_mem = mem_fwd(proj, 2 * GM_W // MEM_W, mem_kv, l, "mem_fwd_a")
            w_out = whole("a_w_out", l)
        else:
            proj = norm_mm(st["x1"], row(mix_norm[l]), whole("b_w_in", l - N_A), BF16, "b_proj")
            st["sb_out"] = sb_fwd(proj, kv, "sb_fwd")
            y_tok = st["sb_out"].astype(BF16)
            y_mem = mem_fwd(proj, SB_W // MEM_W, mem_kv, l, "mem_fwd_b")
            w_out = whole("b_w_out", l - N_A)
        st["proj"] = proj
        st["y"] = jnp.concatenate([y_tok, y_mem], axis=1)
        st["x2"] = mm_res(st["y"], w_out, st["x1"], 1.0, "mix_out")
        cur, st["gu2"] = ffn_fwd(st["x2"], "ffn2", l)
        saved.append(st)

    loss_blk, dx, d_final = final_loss(cur, row(final_norm), target, "final_loss")
    loss = lax.psum(loss_blk[0, 0] * (0.5 / d_model), AXES)

    grads = {n: [None] * weights[n].shape[0] for n in WEIGHTS if weights[n].ndim >= 2 and n not in ("w_kv",)}
    grads["final_norm"] = d_final.reshape(-1)
    d_mem_kv = [None] * DEPTH
    d_kv = []

    def ffn_bwd(dx, xin, gu, f, l, tie=0.0):
        d_gu = mm_nt_swiglu_bwd(dx, whole(f + "_w_down", l), gu, 0.5, "ffn_dgu")
        dx_new, d_gain, h = mm_nt_normbwd(d_gu, whole_gu(f, l), xin, row(weights[f + "_norm"][l]) + tie, dx, "ffn_dx")
        d_wgu = mm_tn(h, d_gu, 1.0, "ffn_dwgu", tb_target=1408)
        half = d_wgu.shape[1] // 2
        grads[f + "_w_gate"][l], grads[f + "_w_up"][l] = d_wgu[:, :half], d_wgu[:, half:]
        grads[f + "_w_down"][l] = swiglu_mm_tn(gu, dx, 0.5, "ffn_dwdown")
        grads[f + "_norm"][l] = d_gain.reshape(-1)
        return dx_new

    scatter, tie = [None] * len(GROUPS), 0.0
    for l in reversed(range(DEPTH)):
        st = saved[l]
        dx = ffn_bwd(dx, st["x2"], st["gu2"], "ffn2", l, tie)
        proj = st["proj"]
        key_in, key_out, idx = ("a_w_in", "a_w_out", l) if l < N_A else ("b_w_in", "b_w_out", l - N_A)
        w_in, w_out = whole(key_in, idx), whole(key_out, idx)
        dy = mm_nt(dx, w_out, 1.0, "mix_dy")
        grads[key_out][idx] = mm_tn(st["y"], dx, 1.0, "mix_dwout", tb_target=1024)
        if l < N_A:
            d_uv, d_ws, d_bs, d_vgain = gmlp_bwd(proj, dy, row(a_v_full[l]), a_w_spatial[l], bias[l], "gmlp_bwd")
            grads["a_w_spatial"][l], grads["a_b_spatial"][l], grads["a_v_norm"][l] = d_ws, d_bs[:, :, 0], d_vgain.reshape(-1)
            d_q, d_k, d_v = mem_bwd(proj, 2 * GM_W // MEM_W, mem_kv, l, dy, GM_W // MEM_W, "mem_bwd_a")
            d_proj = jnp.concatenate([d_uv, d_q], axis=1)
        else:
            d_qsb, d_ksb, d_vsb = sb_bwd(proj, kv, st["sb_out"], dy, "sb_bwd")
            d_kv.append(jnp.concatenate([d_ksb, d_vsb], axis=1))
            d_q, d_k, d_v = mem_bwd(proj, SB_W // MEM_W, mem_kv, l, dy, SB_W // MEM_W, "mem_bwd_b")
            d_proj = jnp.concatenate([d_qsb, d_q], axis=1)
        d_mem_kv[l] = jnp.concatenate([d_k, d_v], axis=1)
        dx, d_gain, h = mm_nt_normbwd(d_proj, w_in, st["x1"], row(mix_norm[l]), dx, "mix_dx")
        grads["mix_norm"][l] = d_gain.reshape(-1)
        grads[key_in][idx] = mm_tn(h, d_proj, 1.0, "mix_dwin")
        dx = ffn_bwd(dx, st["x0"], st["gu1"], "ffn1", l)
        if l == N_A:
            d_kv_b = sum_leading(jnp.stack(d_kv), BF16, "kv_dsum")
            dx, d_gain, h = mm_nt_normbwd(d_kv_b, whole("w_kv"), x_kv, row(kv_norm), dx, "kv_dx")
            grads["kv_norm"] = d_gain.reshape(-1)
            grads["w_kv"] = mm_tn(h, d_kv_b, 1.0, "kv_dw")
        for g, name in ((2, "scatter_third_start"), (1, "scatter_second_start")):
            if l == GROUPS[g]["ffn1_w_gate"][0]:
                pieces = _group_grads(grads, g)
                handle, started = exchange_start(pieces, False, name)
                scatter[g], tie = (handle, pieces), started[0, 0]

    d_mem_all = jnp.concatenate(d_mem_kv, axis=1).astype(BF16)
    _, d_gain, _ = mm_nt_normbwd(d_mem_all, w_mem_cat, mem_in, row(mem_norm), None, "mem_dnorm")
    grads["mem_norm"] = d_gain.reshape(-1)
    d_wmem = mm_tn(mem_h, d_mem_all, 1.0, "mem_dw")
    grads["w_mem_kv"] = d_wmem.reshape(d_model, DEPTH, -1).transpose(1, 0, 2)

    by_dev = [exchange(_group_grads(grads, 0), "all", False, "scatter_first")]
    by_dev += [_own_filled(exchange_wait(scatter[g][0], dx, False, name), scatter[g][1], False, dev)
               for g, name in ((1, "scatter_second_wait"), (2, "scatter_third_wait"))]
    parts = {}
    for n in SHARDED:
        pieces = [by_dev[g][list(GROUPS[g]).index(n)] for g in range(len(GROUPS)) if n in GROUPS[g]]
        parts[n] = pieces[0] if len(pieces) == 1 else jnp.concatenate(pieces, axis=1)
    grads = {n: (jnp.stack(g) if isinstance(g, list) else g) for n, g in grads.items()}
    for n, g in zip(SMALL, _all_sum([grads[n] for n in SMALL], "sum_small")):
        parts[n] = g[None]
    parts["a_v_norm"] = lax.dynamic_slice(parts["a_v_norm"], (0, 0, dev * vn_width), (1,) + a_v_norm.shape)

    reduced, deltas, new_m, new_v = {}, {}, {}, {}
    for n in WEIGHTS:
        w = weights[n]
        view = (lambda a: a.reshape(-1, a.shape[-1]))
        res = adamw(view(w), parts[n].reshape(parts[n].shape[0], -1, w.shape[-1]), view(mom1[n]), view(mom2[n]), "adamw")
        reduced[n], deltas[n], new_m[n], new_v[n] = [r.reshape(w.shape) for r in res]

    return (loss, dx[None], *[reduced[n] for n in WEIGHTS], *[deltas[n] for n in WEIGHTS],
            *[new_m[n] for n in WEIGHTS], *[new_v[n] for n in WEIGHTS])
```

```python
import functools

import jax
import jax.numpy as jnp
from jax import lax
from jax.experimental import pallas as pl
from jax.experimental.pallas import tpu as pltpu

F32, BF16 = jnp.float32, jnp.bfloat16
MESH_ID = pl.DeviceIdType.MESH
AXES = ("x", "y", "c")
N_DEV = 8

EPS = 1e-6
DEPTH, N_A = 4, 2
GM_W, GM_GROUPS, GM_P = 768, 6, 128
MEM_W, MEM_HEADS, HEAD_DIM = 256, 4, 64
SB_W, SB_BLK = 768, 128
LANES = 128
QK_SCALE = HEAD_DIM ** -0.5
GELU_C, GELU_A = 0.7978845608028654, 0.044715

ADAM_LR, ADAM_B1, ADAM_B2, ADAM_EPS, ADAM_WD, ADAM_STEP = 0.001, 0.9, 0.999, 1e-08, 0.01, 10

VMEM_LIMIT = 56 * 1024 * 1024
PACK_COLS = 512

NT = (((1,), (1,)), ((), ()))
TN = (((0,), (0,)), ((), ()))


def _params(*sem):
    return pltpu.CompilerParams(dimension_semantics=sem, vmem_limit_bytes=VMEM_LIMIT)


def _tile(n, target, mult=LANES):
    best = None
    for t in range(mult, min(n, target) + 1, mult):
        if n % t == 0:
            best = t
    return best if best is not None else n


def _dot(a, b, dims=None):
    if dims is None:
        return jnp.dot(a, b, preferred_element_type=F32)
    return lax.dot_general(a, b, dims, preferred_element_type=F32)


def exchange(srcs, group, same_src, name, split=False):
    size = {"pair": 2, "quad": 4, "all": 8}[group]
    n = len(srcs)
    chunk_shapes = [tuple(s.shape) if same_src else tuple(s.shape[1:]) for s in srcs]
    pieces = [cs[0] if split else 1 for cs in chunk_shapes]
    n_dma = sum(pieces)

    def body(*refs):
        src_refs, out_refs = refs[:n], refs[n:2 * n]
        send_sems, recv_sems, local_sems = refs[2 * n:]
        x, y, c = lax.axis_index("x"), lax.axis_index("y"), lax.axis_index("c")
        if group == "pair":
            me, dev = c, lambda p: (x, y, p)
        elif group == "quad":
            me, dev = 2 * x + y, lambda p: (p // 2, p % 2, c)
        else:
            me, dev = 4 * x + 2 * y + c, lambda p: (p // 4, (p // 2) % 2, p % 2)

        def chunk(t, idx):
            return src_refs[t] if same_src else src_refs[t].at[idx]

        def copies(k, idx, slot, peer):
            out, w = [], k * n_dma
            for t in range(n):
                src, dst = chunk(t, idx), out_refs[t].at[slot]
                for s_ref, d_ref in ([(src.at[u], dst.at[u]) for u in range(pieces[t])] if split else [(src, dst)]):
                    out.append(pltpu.make_async_remote_copy(
                        src_ref=s_ref, dst_ref=d_ref, send_sem=send_sems.at[w], recv_sem=recv_sems.at[w],
                        device_id=dev(peer), device_id_type=MESH_ID))
                    w += 1
            return out

        local = [pltpu.make_async_copy(chunk(t, me), out_refs[t].at[me], local_sems.at[t]) for t in range(n)]
        for cp in local:
            cp.start()
        sends = []
        for k in range(1, size):
            peer = (me + k) % size
            sends += copies(k, peer, me, peer)
        for cp in sends:
            cp.start()
        for k in range(1, size):
            sender = (me + size - k) % size
            for cp in copies(k, me, sender, sender):
                cp.wait_recv()
        for cp in sends:
            cp.wait_send()
        for cp in local:
            cp.wait()

    hbm = pl.BlockSpec(memory_space=pltpu.HBM)
    return pl.pallas_call(
        body, name=name,
        out_shape=[jax.ShapeDtypeStruct((size,) + cs, s.dtype) for cs, s in zip(chunk_shapes, srcs)],
        in_specs=[hbm] * n, out_specs=[hbm] * n,
        scratch_shapes=[pltpu.SemaphoreType.DMA((size * n_dma,)), pltpu.SemaphoreType.DMA((size * n_dma,)),
                        pltpu.SemaphoreType.DMA((n,))],
    )(*srcs)


HBM_SPEC = pl.BlockSpec(memory_space=pltpu.HBM)
SEM_SPEC = pl.BlockSpec(memory_space=pltpu.SEMAPHORE)
DATAFLOW = pltpu.SideEffectType.DATAFLOW_SIDE_EFFECTING


def _all_devices():
    me = 4 * lax.axis_index("x") + 2 * lax.axis_index("y") + lax.axis_index("c")
    return me, lambda p: (p // 4, (p // 2) % 2, p % 2)


def exchange_start(srcs, same_src, name):
    n = len(srcs)
    chunk_shapes = [tuple(s.shape) if same_src else tuple(s.shape[1:]) for s in srcs]
    srcs = [pltpu.with_memory_space_constraint(s, pltpu.HBM) for s in srcs]
    lands = [pltpu.with_memory_space_constraint(lax.empty((N_DEV,) + cs, s.dtype), pltpu.HBM)
             for cs, s in zip(chunk_shapes, srcs)]

    def body(*refs):
        src_refs, land_refs, send_sems, recv_sems, token = refs[:n], refs[n:2 * n], refs[2 * n], refs[2 * n + 1], refs[-1]
        me, dev = _all_devices()
        for k in range(1, N_DEV):
            peer = (me + k) % N_DEV
            for t in range(n):
                w = (k - 1) * n + t
                pltpu.make_async_remote_copy(
                    src_ref=src_refs[t] if same_src else src_refs[t].at[peer], dst_ref=land_refs[t].at[me],
                    send_sem=send_sems.at[w], recv_sem=recv_sems.at[w], device_id=dev(peer), device_id_type=MESH_ID).start()
        token[...] = jnp.zeros_like(token)

    n_copies = (N_DEV - 1) * n
    out = pl.pallas_call(
        body, name=name,
        out_shape=(pltpu.SemaphoreType.DMA((n_copies,)), pltpu.SemaphoreType.DMA((n_copies,)),
                   *[pltpu.HBM(a.shape, a.dtype) for a in srcs + lands], jax.ShapeDtypeStruct((8, LANES), F32)),
        in_specs=[HBM_SPEC] * (2 * n),
        out_specs=(SEM_SPEC, SEM_SPEC, *[HBM_SPEC] * (2 * n), pl.BlockSpec(memory_space=pltpu.VMEM)),
        input_output_aliases={t: 2 + t for t in range(2 * n)},
        compiler_params=pltpu.CompilerParams(has_side_effects=DATAFLOW),
    )(*srcs, *lands)
    return (out[0], out[1], list(out[2:2 + n]), list(out[2 + n:2 + 2 * n])), out[-1]


def exchange_wait(handle, after, same_src, name):
    send_sems, recv_sems, srcs, lands = handle
    n = len(srcs)

    def body(*refs):
        src_refs, land_refs, send_sems, recv_sems = refs[:n], refs[n:2 * n], refs[2 * n], refs[2 * n + 1]
        me, dev = _all_devices()
        for k in range(1, N_DEV):
            sender = (me + N_DEV - k) % N_DEV
            for t in range(n):
                w = (k - 1) * n + t
                copy = pltpu.make_async_remote_copy(
                    src_ref=src_refs[t] if same_src else src_refs[t].at[me], dst_ref=land_refs[t].at[sender],
                    send_sem=send_sems.at[w], recv_sem=recv_sems.at[w], device_id=dev(sender), device_id_type=MESH_ID)
                copy.wait_send()
                copy.wait_recv()

    out = pl.pallas_call(
        body, name=name,
        out_shape=[pltpu.HBM(a.shape, a.dtype) for a in srcs + lands],
        in_specs=[HBM_SPEC] * (2 * n) + [SEM_SPEC, SEM_SPEC, pl.BlockSpec(memory_space=pl.ANY)],
        out_specs=[HBM_SPEC] * (2 * n),
        input_output_aliases={t: t for t in range(2 * n)},
        compiler_params=pltpu.CompilerParams(has_side_effects=DATAFLOW),
    )(*srcs, *lands, send_sems, recv_sems, after)
    return list(out[n:])


def sum_leading(parts, out_dtype, name):
    k, rows, cols = parts.shape
    tr = _tile(rows, 512, 16)

    def body(p_ref, o_ref):
        acc = p_ref[0].astype(F32)
        for s in range(1, k):
            acc = acc + p_ref[s].astype(F32)
        o_ref[...] = acc.astype(o_ref.dtype)

    return pl.pallas_call(
        body, name=name, grid=(rows // tr,),
        in_specs=[pl.BlockSpec((k, tr, cols), lambda i: (0, i, 0))],
        out_specs=pl.BlockSpec((tr, cols), lambda i: (i, 0)),
        out_shape=jax.ShapeDtypeStruct((rows, cols), out_dtype),
        compiler_params=_params("arbitrary"),
    )(parts)


def _rms(xf):
    return lax.rsqrt(jnp.mean(xf * xf, axis=-1, keepdims=True) + EPS)


def norm_mm(x, g, w, out_dtype, name, emit_h=False):
    m, d = x.shape
    n = w.shape[1]
    tm, tn = _tile(m, 1024, 8), _tile(n, 1408)

    def body(x_ref, g_ref, w_ref, o_ref, *rest):
        h_ref = rest[-1]

        @pl.when(pl.program_id(1) == 0)
        def _():
            xf = x_ref[...]
            hb = ((xf * _rms(xf)) * g_ref[...]).astype(BF16)
            h_ref[...] = hb
            if emit_h:
                rest[0][...] = hb

        o_ref[...] = _dot(h_ref[...], w_ref[...]).astype(o_ref.dtype)

    out_shape = [jax.ShapeDtypeStruct((m, n), out_dtype)]
    out_specs = [pl.BlockSpec((tm, tn), lambda i, j: (i, j))]
    if emit_h:
        out_shape.append(jax.ShapeDtypeStruct((m, d), BF16))
        out_specs.append(pl.BlockSpec((tm, d), lambda i, j: (i, 0)))
    res = pl.pallas_call(
        body, name=name, grid=(m // tm, n // tn),
        in_specs=[pl.BlockSpec((tm, d), lambda i, j: (i, 0)), pl.BlockSpec((1, d), lambda i, j: (0, 0)),
                  pl.BlockSpec((d, tn), lambda i, j: (0, j))],
        out_specs=out_specs, out_shape=out_shape,
        scratch_shapes=[pltpu.VMEM((tm, d), BF16)],
        compiler_params=_params("arbitrary", "arbitrary"),
    )(x, g, w)
    return res if emit_h else res[0]


def mm_res(a, w, res, alpha, name):
    m, k = a.shape
    n = w.shape[1]
    tm, tn = _tile(m, 1024, 8), _tile(n, 1024)

    def body(a_ref, w_ref, r_ref, o_ref):
        o_ref[...] = r_ref[...] + alpha * _dot(a_ref[...], w_ref[...])

    return pl.pallas_call(
        body, name=name, grid=(m // tm, n // tn),
        in_specs=[pl.BlockSpec((tm, k), lambda i, j: (i, 0)), pl.BlockSpec((k, tn), lambda i, j: (0, j)),
                  pl.BlockSpec((tm, tn), lambda i, j: (i, j))],
        out_specs=pl.BlockSpec((tm, tn), lambda i, j: (i, j)),
        out_shape=jax.ShapeDtypeStruct((m, n), F32),
        compiler_params=_params("arbitrary", "arbitrary"),
    )(a, w, res)


def mm_nt(x, w, alpha, name):
    m, d = x.shape
    n = w.shape[0]
    tm, tn = _tile(m, 1024, 8), _tile(n, 1408)

    def body(x_ref, w_ref, o_ref, xb_ref):
        @pl.when(pl.program_id(1) == 0)
        def _():
            xb_ref[...] = x_ref[...].astype(BF16)

        o_ref[...] = (alpha * _dot(xb_ref[...], w_ref[...], NT)).astype(o_ref.dtype)

    return pl.pallas_call(
        body, name=name, grid=(m // tm, n // tn),
        in_specs=[pl.BlockSpec((tm, d), lambda i, j: (i, 0)), pl.BlockSpec((tn, d), lambda i, j: (j, 0))],
        out_specs=pl.BlockSpec((tm, tn), lambda i, j: (i, j)),
        out_shape=jax.ShapeDtypeStruct((m, n), BF16),
        scratch_shapes=[pltpu.VMEM((tm, d), BF16)],
        compiler_params=_params("arbitrary", "arbitrary"),
    )(x, w)


def mm_tn(a, b, alpha, name, ta_target=1024, tb_target=512):
    s, ka = a.shape
    nb = b.shape[1]
    ta, tb, ts = _tile(ka, ta_target), _tile(nb, tb_target), _tile(s, 1024, 16)
    steps = s // ts

    def body(a_ref, b_ref, o_ref, acc_ref):
        t = pl.program_id(2)

        @pl.when(t == 0)
        def _():
            acc_ref[...] = jnp.zeros_like(acc_ref)

        acc_ref[...] += _dot(a_ref[...].astype(BF16), b_ref[...].astype(BF16), TN)

        @pl.when(t == steps - 1)
        def _():
            o_ref[...] = alpha * acc_ref[...]

    return pl.pallas_call(
        body, name=name, grid=(ka // ta, nb // tb, steps),
        in_specs=[pl.BlockSpec((ts, ta), lambda i, j, t: (t, i)), pl.BlockSpec((ts, tb), lambda i, j, t: (t, j))],
        out_specs=pl.BlockSpec((ta, tb), lambda i, j, t: (i, j)),
        out_shape=jax.ShapeDtypeStruct((ka, nb), F32),
        scratch_shapes=[pltpu.VMEM((ta, tb), F32)],
        compiler_params=_params("arbitrary", "arbitrary", "arbitrary"),
    )(a, b)


def mm_nt_normbwd(dy, w, x, g, res, name):
    m, n = dy.shape
    d = w.shape[0]
    tm, tk = _tile(m, 1024, 8), _tile(n, 1408)
    steps = n // tk
    has_res = res is not None

    def body(*refs):
        if has_res:
            dy_ref, w_ref, x_ref, g_ref, r_ref, dx_ref, dg_ref, h_ref, acc_ref = refs
        else:
            dy_ref, w_ref, x_ref, g_ref, dx_ref, dg_ref, h_ref, acc_ref = refs
        i, t = pl.program_id(0), pl.program_id(1)

        @pl.when(t == 0)
        def _():
            acc_ref[...] = jnp.zeros_like(acc_ref)

        @pl.when((t == 0) & (i == 0))
        def _():
            dg_ref[...] = jnp.zeros_like(dg_ref)

        acc_ref[...] += _dot(dy_ref[...], w_ref[...], NT)

        @pl.when(t == steps - 1)
        def _():
            xf = x_ref[...]
            r = _rms(xf)
            xhat = xf * r
            dh = acc_ref[...]
            gain = g_ref[...]
            dg_ref[...] += jnp.sum(dh * xhat, axis=0, keepdims=True)
            dxhat = dh * gain
            dx = r * (dxhat - xhat * jnp.mean(dxhat * xhat, axis=-1, keepdims=True))
            dx_ref[...] = (r_ref[...] + dx) if has_res else dx
            h_ref[...] = (xhat * gain).astype(BF16)

    row = lambda i, t: (i, 0)
    in_specs = [pl.BlockSpec((tm, tk), lambda i, t: (i, t)), pl.BlockSpec((d, tk), lambda i, t: (0, t)),
                pl.BlockSpec((tm, d), row), pl.BlockSpec((1, d), lambda i, t: (0, 0))]
    args = [dy, w, x, g]
    if has_res:
        in_specs.append(pl.BlockSpec((tm, d), row))
        args.append(res)
    return pl.pallas_call(
        body, name=name, grid=(m // tm, steps),
        in_specs=in_specs,
        out_specs=[pl.BlockSpec((tm, d), row), pl.BlockSpec((1, d), lambda i, t: (0, 0)), pl.BlockSpec((tm, d), row)],
        out_shape=[jax.ShapeDtypeStruct((m, d), F32), jax.ShapeDtypeStruct((1, d), F32), jax.ShapeDtypeStruct((m, d), BF16)],
        scratch_shapes=[pltpu.VMEM((tm, d), F32)],
        compiler_params=_params("arbitrary", "arbitrary"),
    )(*args)


def _sigmoid(z):
    return 1.0 / (1.0 + jnp.exp(-z))


def _swiglu(gate_b, up_b):
    gate = gate_b.astype(F32)
    return (gate * _sigmoid(gate) * up_b.astype(F32)).astype(BF16)


def swiglu_mm_res(gu, w, res, alpha, name):
    m, f2 = gu.shape
    f, n = w.shape
    tm, tc = _tile(m, 256, 16), _tile(f, 1408)

    def body(gu_ref, w_ref, r_ref, o_ref):
        acc = jnp.zeros((tm, n), F32)
        for c0 in range(0, f, tc):
            act = _swiglu(gu_ref[:, c0:c0 + tc], gu_ref[:, f + c0:f + c0 + tc])
            acc = acc + _dot(act, w_ref[c0:c0 + tc, :])
        o_ref[...] = r_ref[...] + alpha * acc

    return pl.pallas_call(
        body, name=name, grid=(m // tm,),
        in_specs=[pl.BlockSpec((tm, f2), lambda i: (i, 0)), pl.BlockSpec((f, n), lambda i: (0, 0)),
                  pl.BlockSpec((tm, n), lambda i: (i, 0))],
        out_specs=pl.BlockSpec((tm, n), lambda i: (i, 0)),
        out_shape=jax.ShapeDtypeStruct((m, n), F32), compiler_params=_params("arbitrary"),
    )(gu, w, res)


def swiglu_mm_tn(gu, b, alpha, name):
    s, f2 = gu.shape
    f, n = f2 // 2, b.shape[1]
    ta, ts = _tile(f, 1408), _tile(s, 512, 16)
    steps, half = s // ts, f // ta

    def body(g_ref, u_ref, b_ref, o_ref, acc_ref):
        t = pl.program_id(1)

        @pl.when(t == 0)
        def _():
            acc_ref[...] = jnp.zeros_like(acc_ref)

        acc_ref[...] += _dot(_swiglu(g_ref[...], u_ref[...]), b_ref[...].astype(BF16), TN)

        @pl.when(t == steps - 1)
        def _():
            o_ref[...] = alpha * acc_ref[...]

    return pl.pallas_call(
        body, name=name, grid=(half, steps),
        in_specs=[pl.BlockSpec((ts, ta), lambda i, t: (t, i)), pl.BlockSpec((ts, ta), lambda i, t: (t, half + i)),
                  pl.BlockSpec((ts, n), lambda i, t: (t, 0))],
        out_specs=pl.BlockSpec((ta, n), lambda i, t: (i, 0)),
        out_shape=jax.ShapeDtypeStruct((f, n), F32),
        scratch_shapes=[pltpu.VMEM((ta, n), F32)],
        compiler_params=_params("arbitrary", "arbitrary"),
    )(gu, gu, b)


def mm_nt_swiglu_bwd(x, w, gu, alpha, name):
    m, d = x.shape
    f = w.shape[0]
    tm, tc = _tile(m, 256, 16), _tile(f, 1408)

    def body(x_ref, w_ref, gu_ref, o_ref):
        xb = x_ref[...].astype(BF16)
        for c0 in range(0, f, tc):
            d_act = alpha * _dot(xb, w_ref[c0:c0 + tc, :], NT)
            gate, up = gu_ref[:, c0:c0 + tc].astype(F32), gu_ref[:, f + c0:f + c0 + tc].astype(F32)
            sg = _sigmoid(gate)
            o_ref[:, c0:c0 + tc] = (d_act * up * (sg * (1.0 + gate * (1.0 - sg)))).astype(BF16)
            o_ref[:, f + c0:f + c0 + tc] = (d_act * (gate * sg)).astype(BF16)

    return pl.pallas_call(
        body, name=name, grid=(m // tm,),
        in_specs=[pl.BlockSpec((tm, d), lambda i: (i, 0)), pl.BlockSpec((f, d), lambda i: (0, 0)),
                  pl.BlockSpec((tm, 2 * f), lambda i: (i, 0))],
        out_specs=pl.BlockSpec((tm, 2 * f), lambda i: (i, 0)),
        out_shape=jax.ShapeDtypeStruct((m, 2 * f), BF16), compiler_params=_params("arbitrary"),
    )(x, w, gu)


def _gelu(x):
    return 0.5 * x * (1.0 + jnp.tanh(GELU_C * (x + GELU_A * x * x * x)))


def _gelu_grad(x):
    t = jnp.tanh(GELU_C * (x + GELU_A * x * x * x))
    return 0.5 * (1.0 + t) + 0.5 * x * (1.0 - t * t) * (GELU_C * (1.0 + 3.0 * GELU_A * x * x))


def _chunk_mask():
    row = lax.broadcasted_iota(jnp.int32, (GM_P, GM_P), 0)
    col = lax.broadcasted_iota(jnp.int32, (GM_P, GM_P), 1)
    return (col < GM_P // 2) | (row >= GM_P // 2)


def gmlp_fwd(proj, gain, w_s, bias, name):
    s, pw = proj.shape
    tm = _tile(s, 256, GM_P)

    def body(p_ref, gain_ref, w_ref, b_ref, o_ref):
        mask = _chunk_mask()
        u = _gelu(p_ref[:, :GM_W])
        v = _gelu(p_ref[:, GM_W:2 * GM_W])
        vn = ((v * _rms(v)) * gain_ref[...]).astype(BF16)
        for g in range(GM_GROUPS):
            wg = jnp.where(mask, w_ref[g], 0.0).astype(BF16)
            cols = slice(g * GM_P, (g + 1) * GM_P)
            for n in range(tm // GM_P):
                rows = slice(n * GM_P, (n + 1) * GM_P)
                mixed = _dot(wg, vn[rows, cols]) + b_ref[:, cols]
                o_ref[rows, cols] = (u[rows, cols] * mixed).astype(BF16)

    return pl.pallas_call(
        body, name=name, grid=(s // tm,),
        in_specs=[pl.BlockSpec((tm, pw), lambda i: (i, 0)), pl.BlockSpec((1, GM_W), lambda i: (0, 0)),
                  pl.BlockSpec((GM_GROUPS, GM_P, GM_P), lambda i: (0, 0, 0)), pl.BlockSpec((GM_P, GM_W), lambda i: (0, 0))],
        out_specs=pl.BlockSpec((tm, GM_W), lambda i: (i, 0)),
        out_shape=jax.ShapeDtypeStruct((s, GM_W), BF16), compiler_params=_params("arbitrary"),
    )(proj, gain, w_s, bias)


def gmlp_bwd(proj, dy, gain, w_s, bias, name):
    s, pw = proj.shape
    dw_total = dy.shape[1]
    tm = _tile(s, 256, GM_P)

    def body(p_ref, dy_ref, gain_ref, w_ref, b_ref, dp_ref, dw_ref, db_ref, dgain_ref, dvn_ref):
        @pl.when(pl.program_id(0) == 0)
        def _():
            dw_ref[...] = jnp.zeros_like(dw_ref)
            db_ref[...] = jnp.zeros_like(db_ref)
            dgain_ref[...] = jnp.zeros_like(dgain_ref)

        mask = _chunk_mask()
        pu = p_ref[:, :GM_W]
        pv = p_ref[:, GM_W:2 * GM_W]
        u = _gelu(pu)
        v = _gelu(pv)
        r = _rms(v)
        vhat = v * r
        gain = gain_ref[...]
        vn = (vhat * gain).astype(BF16)
        gu_grad = _gelu_grad(pu)
        for g in range(GM_GROUPS):
            wg = jnp.where(mask, w_ref[g], 0.0).astype(BF16)
            cols = slice(g * GM_P, (g + 1) * GM_P)
            dw_acc = jnp.zeros((GM_P, GM_P), F32)
            db_acc = jnp.zeros((GM_P, 1), F32)
            for n in range(tm // GM_P):
                rows = slice(n * GM_P, (n + 1) * GM_P)
                dyb = dy_ref[rows, cols].astype(F32)
                vnb = vn[rows, cols]
                mixed = _dot(wg, vnb) + b_ref[:, cols]
                dmixed = dyb * u[rows, cols]
                dmb = dmixed.astype(BF16)
                dp_ref[rows, cols] = (dyb * mixed * gu_grad[rows, cols]).astype(BF16)
                dw_acc = dw_acc + _dot(dmb, vnb, NT)
                db_acc = db_acc + jnp.sum(dmixed, axis=1, keepdims=True)
                dvn_ref[rows, cols] = _dot(wg, dmb, TN)
            dw_ref[g] += jnp.where(mask, dw_acc, 0.0)
            db_ref[g] += jnp.broadcast_to(db_acc, (GM_P, GM_P))
        dvn = dvn_ref[...]
        dgain_ref[...] += jnp.sum(dvn * vhat, axis=0, keepdims=True)
        dvhat = dvn * gain
        dv = r * (dvhat - vhat * jnp.mean(dvhat * vhat, axis=-1, keepdims=True))
        dp_ref[:, GM_W:] = (dv * _gelu_grad(pv)).astype(BF16)

    const3 = lambda i: (0, 0, 0)
    return pl.pallas_call(
        body, name=name, grid=(s // tm,),
        in_specs=[pl.BlockSpec((tm, pw), lambda i: (i, 0)), pl.BlockSpec((tm, dw_total), lambda i: (i, 0)),
                  pl.BlockSpec((1, GM_W), lambda i: (0, 0)), pl.BlockSpec((GM_GROUPS, GM_P, GM_P), const3),
                  pl.BlockSpec((GM_P, GM_W), lambda i: (0, 0))],
        out_specs=[pl.BlockSpec((tm, 2 * GM_W), lambda i: (i, 0)), pl.BlockSpec((GM_GROUPS, GM_P, GM_P), const3),
                   pl.BlockSpec((GM_GROUPS, GM_P, GM_P), const3), pl.BlockSpec((1, GM_W), lambda i: (0, 0))],
        out_shape=[jax.ShapeDtypeStruct((s, 2 * GM_W), BF16), jax.ShapeDtypeStruct((GM_GROUPS, GM_P, GM_P), F32),
                   jax.ShapeDtypeStruct((GM_GROUPS, GM_P, GM_P), F32), jax.ShapeDtypeStruct((1, GM_W), F32)],
        scratch_shapes=[pltpu.VMEM((tm, GM_W), F32)],
        compiler_params=_params("arbitrary"),
    )(proj, dy, gain, w_s, bias)


def _keep(mask, xb):
    return jnp.where(mask, xb.astype(F32), 0.0).astype(BF16)


def _head_masks(rows, width, heads):
    lane = lax.broadcasted_iota(jnp.int32, (rows, width), 1)
    return [(lane >= HEAD_DIM * h) & (lane < HEAD_DIM * (h + 1)) for h in range(heads)]


def _mem_probs(qh, k):
    sc = _dot(qh, k, NT) * QK_SCALE
    e = jnp.exp(sc - jnp.max(sc, axis=-1, keepdims=True))
    return e / jnp.sum(e, axis=-1, keepdims=True)


def mem_fwd(proj, q_blk, mem_kv, layer, name):
    s = proj.shape[0]
    n_mem = mem_kv.shape[0]
    tm = _tile(s, 512, 16)

    def body(q_ref, k_ref, v_ref, o_ref):
        q = q_ref[...].astype(BF16)
        k, v = k_ref[...], v_ref[...]
        out = jnp.zeros((tm, MEM_W), F32)
        for hm in _head_masks(tm, MEM_W, MEM_HEADS):
            p = _mem_probs(_keep(hm, q), k)
            out = out + jnp.where(hm, _dot(p.astype(BF16), v), 0.0)
        o_ref[...] = out.astype(BF16)

    return pl.pallas_call(
        body, name=name, grid=(s // tm,),
        in_specs=[pl.BlockSpec((tm, MEM_W), lambda i: (i, q_blk)), pl.BlockSpec((n_mem, MEM_W), lambda i: (0, 2 * layer)),
                  pl.BlockSpec((n_mem, MEM_W), lambda i: (0, 2 * layer + 1))],
        out_specs=pl.BlockSpec((tm, MEM_W), lambda i: (i, 0)),
        out_shape=jax.ShapeDtypeStruct((s, MEM_W), BF16), compiler_params=_params("arbitrary"),
    )(proj, mem_kv, mem_kv)


def mem_bwd(proj, q_blk, mem_kv, layer, dy, dy_blk, name):
    s = proj.shape[0]
    n_mem = mem_kv.shape[0]
    tm = _tile(s, 512, 16)

    def body(q_ref, k_ref, v_ref, dy_ref, dq_ref, dk_ref, dv_ref):
        @pl.when(pl.program_id(0) == 0)
        def _():
            dk_ref[...] = jnp.zeros_like(dk_ref)
            dv_ref[...] = jnp.zeros_like(dv_ref)

        q = q_ref[...].astype(BF16)
        k, v = k_ref[...], v_ref[...]
        dy = dy_ref[...]
        dq = jnp.zeros((tm, MEM_W), F32)
        dk = jnp.zeros((n_mem, MEM_W), F32)
        dv = jnp.zeros((n_mem, MEM_W), F32)
        for hm in _head_masks(tm, MEM_W, MEM_HEADS):
            qh = _keep(hm, q)
            dyh = _keep(hm, dy)
            p = _mem_probs(qh, k)
            dp = _dot(dyh, v, NT)
            dv = dv + _dot(p.astype(BF16), dyh, TN)
            ds = (p * (dp - jnp.sum(dp * p, axis=-1, keepdims=True)) * QK_SCALE).astype(BF16)
            dq = dq + jnp.where(hm, _dot(ds, k), 0.0)
            dk = dk + _dot(ds, qh, TN)
        dq_ref[...] = dq.astype(BF16)
        dk_ref[...] += dk
        dv_ref[...] += dv

    const = lambda i: (0, 0)
    return pl.pallas_call(
        body, name=name, grid=(s // tm,),
        in_specs=[pl.BlockSpec((tm, MEM_W), lambda i: (i, q_blk)), pl.BlockSpec((n_mem, MEM_W), lambda i: (0, 2 * layer)),
                  pl.BlockSpec((n_mem, MEM_W), lambda i: (0, 2 * layer + 1)), pl.BlockSpec((tm, MEM_W), lambda i: (i, dy_blk))],
        out_specs=[pl.BlockSpec((tm, MEM_W), lambda i: (i, 0)), pl.BlockSpec((n_mem, MEM_W), const),
                   pl.BlockSpec((n_mem, MEM_W), const)],
        out_shape=[jax.ShapeDtypeStruct((s, MEM_W), BF16), jax.ShapeDtypeStruct((n_mem, MEM_W), F32),
                   jax.ShapeDtypeStruct((n_mem, MEM_W), F32)],
        compiler_params=_params("arbitrary"),
    )(proj, mem_kv, mem_kv, dy)


SB_KEYS = 512
SB_SUB = SB_KEYS // SB_BLK
SB_QROWS = 256
SB_QB = SB_QROWS // SB_BLK
SB_CHAINS = 2 * SB_QB
LOG2E = 1.4426950408889634


def _split(xf):
    hi = xf.astype(BF16)
    return hi, (xf - hi.astype(F32)).astype(BF16)


def _sb_consts():
    row = lax.bitwise_and(lax.broadcasted_iota(jnp.int32, (2 * SB_BLK, 2 * SB_BLK), 0), SB_BLK - 1)
    col = lax.broadcasted_iota(jnp.int32, (2 * SB_BLK, 2 * SB_BLK), 1)
    ones = col >= SB_BLK
    after2 = jnp.where(ones | (row > col), 1.0, 0.0).astype(BF16)
    from2 = jnp.where(ones | (row >= col), 1.0, 0.0).astype(BF16)
    r = lax.broadcasted_iota(jnp.int32, (SB_BLK, SB_BLK), 0)
    c = lax.broadcasted_iota(jnp.int32, (SB_BLK, SB_BLK), 1)
    return after2, from2, c - r, [c < HEAD_DIM, c >= HEAD_DIM]


def _suffix(xf, tri2):
    hi, lo = _split(xf)
    return _dot(jnp.concatenate([hi, lo], axis=1), tri2)


def _sb_logs(z2, mask):
    l1 = jnp.log2(1.0 + jnp.exp2(-jnp.abs(z2)))
    log_one_minus = jnp.minimum(-z2, 0.0) - l1
    if mask is not None:
        log_one_minus = jnp.where(mask, log_one_minus, 0.0)
    return log_one_minus, jnp.minimum(z2, 0.0) - l1


def _sb_queries(q_ref, heads):
    q = q_ref[...].astype(F32) * QK_SCALE
    return [jnp.where(hm, q[r * SB_BLK:(r + 1) * SB_BLK], 0.0).astype(BF16) for r in range(SB_QB) for hm in heads]


def _sb_walk(i, block, state):
    assert SB_SUB == 2 * SB_QB
    own = lax.shift_right_logical(i * SB_QB, SB_SUB.bit_length() - 1)
    firsts = [[(v * SB_QB + r) * SB_BLK for r in range(SB_QB) for _ in range(2)] for v in range(2)]
    state = lax.cond(lax.bitwise_and(i, 1) == 0, lambda st: block(own, st, firsts[0]),
                     lambda st: block(own, st, firsts[1]), state)
    return lax.fori_loop(0, own, lambda t, st: block(own - 1 - t, st, None), state)


def _sb_tiles(first):
    out = []
    for c in reversed(range(SB_SUB)):
        for n in range(SB_CHAINS):
            if first is None or c * SB_BLK < first[n]:
                out.append((c, n, "before"))
            elif c * SB_BLK == first[n]:
                out.append((c, n, "diagonal"))
    return out


def _sb_heads_apart(stacked, heads, r):
    return jnp.where(heads[0], stacked[2 * r * SB_BLK:(2 * r + 1) * SB_BLK],
                     stacked[(2 * r + 1) * SB_BLK:(2 * r + 2) * SB_BLK])


def sb_fwd(proj, kv, name):
    s = proj.shape[0]
    assert s % SB_KEYS == 0 and SB_KEYS % SB_QROWS == 0

    def body(q_ref, k_ref, v_ref, o_ref):
        after2, _, col_minus_row, heads = _sb_consts()
        q_all = jnp.concatenate(_sb_queries(q_ref, heads), axis=0)
        key_before_query = col_minus_row < 0

        def block(j, state, first):
            runs, acc = list(state[0]), state[1]
            rows = pl.ds(pl.multiple_of(j * SB_KEYS, SB_KEYS), SB_KEYS)
            kb, vb = k_ref[rows, :], v_ref[rows, :]
            z = _dot(q_all, kb, NT) * LOG2E
            pend = {}
            parts = [[jnp.zeros((SB_BLK, SB_BLK), BF16)] * SB_SUB for _ in range(SB_CHAINS)]
            for c, n, where in _sb_tiles(first):
                mask = key_before_query if where == "diagonal" else None
                lom, lb = _sb_logs(z[n * SB_BLK:(n + 1) * SB_BLK, c * SB_BLK:(c + 1) * SB_BLK], mask)
                pend[c, n] = (lb, _suffix(lom, after2), mask)
            for c, n, _ in _sb_tiles(first):
                lb, r, mask = pend.pop((c, n))
                a = jnp.exp2(lb + r[:, :SB_BLK] + runs[n])
                if mask is not None:
                    a = jnp.where(mask, a, 0.0)
                parts[n][c] = a.astype(BF16)
                runs[n] = runs[n] + r[:, SB_BLK:]
            a_all = jnp.concatenate([jnp.concatenate(p, axis=1) for p in parts], axis=0)
            return tuple(runs), acc + _dot(a_all, vb)

        zero = jnp.zeros((SB_BLK, LANES), F32)
        state = _sb_walk(pl.program_id(1), block, ((zero,) * SB_CHAINS, jnp.zeros((SB_CHAINS * SB_BLK, LANES), F32)))
        for r in range(SB_QB):
            o_ref[r * SB_BLK:(r + 1) * SB_BLK, :] = _sb_heads_apart(state[1], heads, r)

    pairs = SB_W // LANES
    return pl.pallas_call(
        body, name=name, grid=(pairs, s // SB_QROWS),
        in_specs=[pl.BlockSpec((SB_QROWS, LANES), lambda p, i: (i, p)), pl.BlockSpec((s, LANES), lambda p, i: (0, p)),
                  pl.BlockSpec((s, LANES), lambda p, i: (0, pairs + p))],
        out_specs=pl.BlockSpec((SB_QROWS, LANES), lambda p, i: (i, p)),
        out_shape=jax.ShapeDtypeStruct((s, SB_W), F32),
        compiler_params=_params("arbitrary", "arbitrary"),
    )(proj, kv, kv)


def sb_bwd(proj, kv, out, dy, name):
    s = proj.shape[0]

    def body(q_ref, k_ref, v_ref, o_ref, do_ref, dq_ref, dk_ref, dv_ref):
        i = pl.program_id(1)

        @pl.when(i == 0)
        def _():
            dk_ref[...] = jnp.zeros_like(dk_ref)
            dv_ref[...] = jnp.zeros_like(dv_ref)

        after2, from2, col_minus_row, heads = _sb_consts()
        q_all = jnp.concatenate(_sb_queries(q_ref, heads), axis=0)
        key_before_query = col_minus_row < 0
        d_out = do_ref[...].astype(F32)
        prod = d_out * o_ref[...]
        dos, totals = [], []
        for r in range(SB_QB):
            rr = slice(r * SB_BLK, (r + 1) * SB_BLK)
            for hm in heads:
                dos.append(jnp.where(hm, d_out[rr], 0.0).astype(BF16))
                totals.append(jnp.broadcast_to(jnp.sum(jnp.where(hm, prod[rr], 0.0), axis=1, keepdims=True),
                                               (SB_BLK, SB_BLK)))
        do_all = jnp.concatenate(dos, axis=0)

        def block(j, state, first):
            runs, seens, dq = list(state[0]), list(state[1]), state[2]
            rows = pl.ds(pl.multiple_of(j * SB_KEYS, SB_KEYS), SB_KEYS)
            kb, vb = k_ref[rows, :], v_ref[rows, :]
            z = _dot(q_all, kb, NT) * LOG2E
            da = _dot(do_all, vb, NT)
            pend, pend2 = {}, {}
            a_parts = [[jnp.zeros((SB_BLK, SB_BLK), BF16)] * SB_SUB for _ in range(SB_CHAINS)]
            dz_parts = [[jnp.zeros((SB_BLK, SB_BLK), BF16)] * SB_SUB for _ in range(SB_CHAINS)]
            for c, n, where in _sb_tiles(first):
                mask = key_before_query if where == "diagonal" else None
                lom, lb = _sb_logs(z[n * SB_BLK:(n + 1) * SB_BLK, c * SB_BLK:(c + 1) * SB_BLK], mask)
                pend[c, n] = (lom, lb, _suffix(lom, after2), mask)
            for c, n, _ in _sb_tiles(first):
                lom, lb, r, mask = pend.pop((c, n))
                a = jnp.exp2(lb + r[:, :SB_BLK] + runs[n])
                if mask is not None:
                    a = jnp.where(mask, a, 0.0)
                runs[n] = runs[n] + r[:, SB_BLK:]
                ab = a.astype(BF16)
                a_parts[n][c] = ab
                dl = ab.astype(F32) * da[n * SB_BLK:(n + 1) * SB_BLK, c * SB_BLK:(c + 1) * SB_BLK]
                pend2[c, n] = (lom, lb, dl, _suffix(dl, from2), mask)
            for c, n, _ in _sb_tiles(first):
                lom, lb, dl, r2, mask = pend2.pop((c, n))
                d_lom = totals[n] - (r2[:, :SB_BLK] + seens[n])
                if mask is not None:
                    d_lom = jnp.where(mask, d_lom, 0.0)
                seens[n] = seens[n] + r2[:, SB_BLK:]
                dz_parts[n][c] = (dl * jnp.exp2(lom) - d_lom * jnp.exp2(lb)).astype(BF16)
            a_all = jnp.concatenate([jnp.concatenate(p, axis=1) for p in a_parts], axis=0)
            dz_all = jnp.concatenate([jnp.concatenate(p, axis=1) for p in dz_parts], axis=0)
            dv_ref[rows, :] += _dot(a_all, do_all, TN)
            dk_ref[rows, :] += _dot(dz_all, q_all, TN)
            return tuple(runs), tuple(seens), dq + _dot(dz_all, kb)

        zero = jnp.zeros((SB_BLK, LANES), F32)
        state = _sb_walk(i, block, ((zero,) * SB_CHAINS, (zero,) * SB_CHAINS,
                                    jnp.zeros((SB_CHAINS * SB_BLK, LANES), F32)))
        for r in range(SB_QB):
            dq_ref[r * SB_BLK:(r + 1) * SB_BLK, :] = (_sb_heads_apart(state[2], heads, r) * QK_SCALE).astype(BF16)

    pairs = SB_W // LANES
    blk = lambda p, i: (i, p)
    col = lambda p, i: (0, p)
    return pl.pallas_call(
        body, name=name, grid=(pairs, s // SB_QROWS),
        in_specs=[pl.BlockSpec((SB_QROWS, LANES), blk), pl.BlockSpec((s, LANES), col),
                  pl.BlockSpec((s, LANES), lambda p, i: (0, pairs + p)), pl.BlockSpec((SB_QROWS, LANES), blk),
                  pl.BlockSpec((SB_QROWS, LANES), blk)],
        out_specs=[pl.BlockSpec((SB_QROWS, LANES), blk), pl.BlockSpec((s, LANES), col), pl.BlockSpec((s, LANES), col)],
        out_shape=[jax.ShapeDtypeStruct((s, SB_W), BF16), jax.ShapeDtypeStruct((s, SB_W), F32),
                   jax.ShapeDtypeStruct((s, SB_W), F32)],
        compiler_params=_params("arbitrary", "arbitrary"),
    )(proj, kv, kv, out, dy)


def final_loss(x, g, target, name):
    s, d = x.shape
    tm = _tile(s, 256, 8)

    def body(x_ref, g_ref, t_ref, loss_ref, dx_ref, dg_ref):
        @pl.when(pl.program_id(0) == 0)
        def _():
            loss_ref[...] = jnp.zeros_like(loss_ref)
            dg_ref[...] = jnp.zeros_like(dg_ref)

        xf = x_ref[...]
        r = _rms(xf)
        xhat = xf * r
        gain = g_ref[...]
        diff = xhat * gain - t_ref[...]
        sq = jnp.sum(jnp.sum(diff * diff, axis=1, keepdims=True), axis=0, keepdims=True)
        loss_ref[...] += jnp.broadcast_to(sq, loss_ref.shape)
        dy = diff * (1.0 / d)
        dg_ref[...] += jnp.sum(dy * xhat, axis=0, keepdims=True)
        dxhat = dy * gain
        dx_ref[...] = r * (dxhat - xhat * jnp.mean(dxhat * xhat, axis=-1, keepdims=True))

    row = lambda i: (i, 0)
    const = lambda i: (0, 0)
    return pl.pallas_call(
        body, name=name, grid=(s // tm,),
        in_specs=[pl.BlockSpec((tm, d), row), pl.BlockSpec((1, d), const), pl.BlockSpec((tm, d), row)],
        out_specs=[pl.BlockSpec((8, LANES), const), pl.BlockSpec((tm, d), row), pl.BlockSpec((1, d), const)],
        out_shape=[jax.ShapeDtypeStruct((8, LANES), F32), jax.ShapeDtypeStruct((s, d), F32), jax.ShapeDtypeStruct((1, d), F32)],
        compiler_params=_params("arbitrary"),
    )(x, g, target)


def adamw(w, parts, m, v, name):
    rows, cols = w.shape
    k = parts.shape[0]
    tr = _tile(rows, 512, 16)
    c1, c2 = 1.0 - ADAM_B1 ** ADAM_STEP, 1.0 - ADAM_B2 ** ADAM_STEP

    def body(w_ref, p_ref, m_ref, v_ref, g_ref, d_ref, nm_ref, nv_ref):
        grad = p_ref[0].astype(F32)
        for s in range(1, k):
            grad = grad + p_ref[s].astype(F32)
        nm = ADAM_B1 * m_ref[...] + (1.0 - ADAM_B1) * grad
        nv = ADAM_B2 * v_ref[...] + (1.0 - ADAM_B2) * (grad * grad)
        g_ref[...] = grad
        d_ref[...] = -ADAM_LR * ((nm / c1) / (jnp.sqrt(nv / c2) + ADAM_EPS) + ADAM_WD * w_ref[...])
        nm_ref[...] = nm
        nv_ref[...] = nv

    spec = pl.BlockSpec((tr, cols), lambda i: (i, 0))
    shape = jax.ShapeDtypeStruct((rows, cols), F32)
    return pl.pallas_call(
        body, name=name, grid=(rows // tr,),
        in_specs=[spec, pl.BlockSpec((k, tr, cols), lambda i: (0, i, 0)), spec, spec],
        out_specs=[spec] * 4, out_shape=[shape] * 4,
        compiler_params=_params("arbitrary"),
    )(w, parts, m, v)


SHARDED = {"ffn1_w_gate": 2, "ffn1_w_up": 2, "ffn1_w_down": 1, "ffn2_w_gate": 2, "ffn2_w_up": 2, "ffn2_w_down": 1,
           "w_mem_kv": 1, "a_w_in": 2, "a_w_out": 1, "w_kv": 1, "b_w_in": 1, "b_w_out": 1}
SMALL = ["ffn1_norm", "mix_norm", "ffn2_norm", "mem_norm", "kv_norm", "final_norm", "a_v_norm", "a_w_spatial", "a_b_spatial"]
WEIGHTS = ["ffn1_norm", "ffn1_w_gate", "ffn1_w_up", "ffn1_w_down", "mix_norm", "ffn2_norm", "ffn2_w_gate", "ffn2_w_up",
           "ffn2_w_down", "mem_norm", "w_mem_kv", "a_w_in", "a_v_norm", "a_w_spatial", "a_b_spatial", "a_w_out", "kv_norm",
           "w_kv", "b_w_in", "b_w_out", "final_norm"]


FFN1_W = ["ffn1_w_gate", "ffn1_w_up", "ffn1_w_down"]
FFN2_W = ["ffn2_w_gate", "ffn2_w_up", "ffn2_w_down"]
GROUPS = [
    {**{n: (0, 1) for n in FFN1_W}, "w_mem_kv": (0, DEPTH)},
    {"a_w_in": (0, 1), "a_w_out": (0, 1), **{n: (0, 1) for n in FFN2_W}},
    {**{n: (1, 2) for n in FFN1_W + FFN2_W}, "a_w_in": (1, 2), "a_w_out": (1, 2)},
    {**{n: (2, DEPTH) for n in FFN1_W + FFN2_W}, "w_kv": None, "b_w_in": (0, 2), "b_w_out": (0, 2)},
]


def _take(a, rng):
    return a if rng is None else a[rng[0]:rng[1]]


def _group_blocks(shards, g):
    return [_take(shards[n], rng).astype(BF16) for n, rng in GROUPS[g].items()]


def _group_grads(grads, g):
    out = []
    for n, rng in GROUPS[g].items():
        whole = jnp.stack(grads[n][rng[0]:rng[1]]) if isinstance(grads[n], list) else _take(grads[n], rng)
        out.append(jnp.stack(jnp.split(whole.astype(BF16), N_DEV, axis=SHARDED[n])))
    return out


def _with_own(lands, srcs, same_src, dev):
    out = []
    for land, src in zip(lands, srcs):
        own = src if same_src else lax.dynamic_index_in_dim(src, dev, 0, False)
        slot = lax.broadcasted_iota(jnp.int32, (N_DEV,) + (1,) * own.ndim, 0) == dev
        out.append(jnp.where(slot, own[None], land))
    return out


def _grads_ready(grads, g):
    for n, rng in GROUPS[g].items():
        if n not in grads or (isinstance(grads[n], list) and any(p is None for p in grads[n][rng[0]:rng[1]])):
            return False
    return True


def _whole_weights(g, by_dev):
    return {n: jnp.concatenate([blocks[d] for d in range(N_DEV)], axis=SHARDED[n]) for n, blocks in zip(GROUPS[g], by_dev)}


def _all_sum(parts, name):
    flat = jnp.concatenate([p.reshape(-1) for p in parts])
    pad = (-flat.size) % (16 * LANES)
    buf = jnp.pad(flat, (0, pad)).reshape(-1, LANES)
    total = sum_leading(exchange([buf], "all", True, name)[0], F32, name + "_sum").reshape(-1)
    out, off = [], 0
    for p in parts:
        out.append(total[off:off + p.size].reshape(p.shape))
        off += p.size
    return out


def _device_index():
    return 4 * lax.axis_index("x") + 2 * lax.axis_index("y") + lax.axis_index("c")


def kernel(x, mem, ffn1_norm, ffn1_w_gate, ffn1_w_up, ffn1_w_down, mix_norm, ffn2_norm, ffn2_w_gate, ffn2_w_up, ffn2_w_down, mem_norm, w_mem_kv, a_w_in, a_v_norm, a_w_spatial, a_b_spatial, a_w_out, kv_norm, w_kv, b_w_in, b_w_out, final_norm, loss_target, m_ffn1_norm, m_ffn1_w_gate, m_ffn1_w_up, m_ffn1_w_down, m_mix_norm, m_ffn2_norm, m_ffn2_w_gate, m_ffn2_w_up, m_ffn2_w_down, m_mem_norm, m_w_mem_kv, m_a_w_in, m_a_v_norm, m_a_w_spatial, m_a_b_spatial, m_a_w_out, m_kv_norm, m_w_kv, m_b_w_in, m_b_w_out, m_final_norm, v_ffn1_norm, v_ffn1_w_gate, v_ffn1_w_up, v_ffn1_w_down, v_mix_norm, v_ffn2_norm, v_ffn2_w_gate, v_ffn2_w_up, v_ffn2_w_down, v_mem_norm, v_w_mem_kv, v_a_w_in, v_a_v_norm, v_a_w_spatial, v_a_b_spatial, v_a_w_out, v_kv_norm, v_w_kv, v_b_w_in, v_b_w_out, v_final_norm):
    weights = dict(ffn1_norm=ffn1_norm, ffn1_w_gate=ffn1_w_gate, ffn1_w_up=ffn1_w_up, ffn1_w_down=ffn1_w_down, mix_norm=mix_norm, ffn2_norm=ffn2_norm, ffn2_w_gate=ffn2_w_gate, ffn2_w_up=ffn2_w_up, ffn2_w_down=ffn2_w_down, mem_norm=mem_norm, w_mem_kv=w_mem_kv, a_w_in=a_w_in, a_v_norm=a_v_norm, a_w_spatial=a_w_spatial, a_b_spatial=a_b_spatial, a_w_out=a_w_out, kv_norm=kv_norm, w_kv=w_kv, b_w_in=b_w_in, b_w_out=b_w_out, final_norm=final_norm)
    mom1 = dict(ffn1_norm=m_ffn1_norm, ffn1_w_gate=m_ffn1_w_gate, ffn1_w_up=m_ffn1_w_up, ffn1_w_down=m_ffn1_w_down, mix_norm=m_mix_norm, ffn2_norm=m_ffn2_norm, ffn2_w_gate=m_ffn2_w_gate, ffn2_w_up=m_ffn2_w_up, ffn2_w_down=m_ffn2_w_down, mem_norm=m_mem_norm, w_mem_kv=m_w_mem_kv, a_w_in=m_a_w_in, a_v_norm=m_a_v_norm, a_w_spatial=m_a_w_spatial, a_b_spatial=m_a_b_spatial, a_w_out=m_a_w_out, kv_norm=m_kv_norm, w_kv=m_w_kv, b_w_in=m_b_w_in, b_w_out=m_b_w_out, final_norm=m_final_norm)
    mom2 = dict(ffn1_norm=v_ffn1_norm, ffn1_w_gate=v_ffn1_w_gate, ffn1_w_up=v_ffn1_w_up, ffn1_w_down=v_ffn1_w_down, mix_norm=v_mix_norm, ffn2_norm=v_ffn2_norm, ffn2_w_gate=v_ffn2_w_gate, ffn2_w_up=v_ffn2_w_up, ffn2_w_down=v_ffn2_w_down, mem_norm=v_mem_norm, w_mem_kv=v_w_mem_kv, a_w_in=v_a_w_in, a_v_norm=v_a_v_norm, a_w_spatial=v_a_w_spatial, a_b_spatial=v_a_b_spatial, a_w_out=v_a_w_out, kv_norm=v_kv_norm, w_kv=v_w_kv, b_w_in=v_b_w_in, b_w_out=v_b_w_out, final_norm=v_final_norm)

    dev = _device_index()
    xs, mem_in, target = x[0], mem[0], loss_target[0]
    d_model = xs.shape[1]
    shards = {n: weights[n] for n in SHARDED}
    blocks = [_group_blocks(shards, g) for g in range(len(GROUPS))]
    arrived = exchange(blocks[0], "all", True, "gather_0")
    wholes = [_whole_weights(0, arrived)] + [None] * (len(GROUPS) - 1)
    gathers, order, tie = [None] * len(GROUPS), arrived, 0.0
    for g in range(1, len(GROUPS)):
        _, ordered = lax.optimization_barrier((order, blocks[g]))
        gathers[g], order = exchange_start(ordered, True, f"gather_{g}_start")
        tie = tie + order[0, 0]
    vn_width = a_v_norm.shape[1]
    a_v_full = _all_sum([lax.dynamic_update_slice(jnp.zeros((N_A, N_DEV * vn_width), F32), a_v_norm, (0, dev * vn_width))],
                        "gather_v_norm")[0]
    latest = [xs]

    def whole(n, l=None):
        for g, grp in enumerate(GROUPS):
            if n in grp and (grp[n] is None or grp[n][0] <= l < grp[n][1]):
                if wholes[g] is None:
                    lands = exchange_wait(gathers[g], latest[0], True, f"gather_{g}_wait")
                    wholes[g] = _whole_weights(g, _with_own(lands, blocks[g], True, dev))
                return wholes[g][n] if grp[n] is None else wholes[g][n][l - grp[n][0]]

    gate_up = {}

    def whole_gu(f, l):
        if (f, l) not in gate_up:
            gate_up[f, l] = jnp.concatenate([whole(f + "_w_gate", l), whole(f + "_w_up", l)], axis=1)
        return gate_up[f, l]

    row = lambda v: v.reshape(1, -1)
    w_mem_cat = wholes[0]["w_mem_kv"].transpose(1, 0, 2).reshape(d_model, -1)
    bias = [jnp.repeat(a_b_spatial[i].T, GM_P, axis=1) for i in range(N_A)]

    mem_kv, mem_h = norm_mm(mem_in, row(mem_norm) + tie, w_mem_cat, BF16, "mem_kv", emit_h=True)

    def ffn_fwd(xin, f, l, tie=0.0):
        gu = norm_mm(xin, row(weights[f + "_norm"][l]) + tie, whole_gu(f, l), BF16, "ffn_gu")
        return swiglu_mm_res(gu, whole(f + "_w_down", l), xin, 0.5, "ffn_down"), gu

    saved = []
    kv = x_kv = None
    cur = xs
    for l in range(DEPTH):
        st = {"x0": cur}
        if l == N_A:
            x_kv = cur
            kv = norm_mm(cur, row(kv_norm), whole("w_kv"), BF16, "kv_proj")
        st["x1"], st["gu1"] = ffn_fwd(cur, "ffn1", l, tie if l == 0 else 0.0)
        latest[0] = st["x1"]
        if l < N_A:
            proj = norm_mm(st["x1"], row(mix_norm[l]), whole("a_w_in", l), F32, "a_proj")
            y_tok = gmlp_fwd(proj, row(a_v_full[l]), a_w_spatial[l], bias[l], "gmlp_fwd")
            y_mem = mem_fwd(proj, 2 * GM_W // MEM_W, mem_kv, l, "mem_fwd_a")
            w_out = whole("a_w_out", l)
        else:
            proj = norm_mm(st["x1"], row(mix_norm[l]), whole("b_w_in", l - N_A), BF16, "b_proj")
            st["sb_out"] = sb_fwd(proj, kv, "sb_fwd")
            y_tok = st["sb_out"].astype(BF16)
            y_mem = mem_fwd(proj, SB_W // MEM_W, mem_kv, l, "mem_fwd_b")
            w_out = whole("b_w_out", l - N_A)
        st["proj"] = proj
        st["y"] = jnp.concatenate([y_tok, y_mem], axis=1)
        st["x2"] = mm_res(st["y"], w_out, st["x1"], 1.0, "mix_out")
        latest[0] = st["x2"]
        cur, st["gu2"] = ffn_fwd(st["x2"], "ffn2", l)
        latest[0] = cur
        saved.append(st)

    loss_blk, dx, d_final = final_loss(cur, row(final_norm), target, "final_loss")
    loss = lax.psum(loss_blk[0, 0] * (0.5 / d_model), AXES)

    grads = {n: [None] * weights[n].shape[0] for n in WEIGHTS if weights[n].ndim >= 2 and n not in ("w_kv",)}
    grads["final_norm"] = d_final.reshape(-1)
    d_mem_kv = [None] * DEPTH
    d_kv = []

    def ffn_bwd(dx, xin, gu, f, l, tie=0.0):
        d_gu = mm_nt_swiglu_bwd(dx, whole(f + "_w_down", l), gu, 0.5, "ffn_dgu")
        dx_new, d_gain, h = mm_nt_normbwd(d_gu, whole_gu(f, l), xin, row(weights[f + "_norm"][l]) + tie, dx, "ffn_dx")
        d_wgu = mm_tn(h, d_gu, 1.0, "ffn_dwgu", tb_target=1408)
        half = d_wgu.shape[1] // 2
        grads[f + "_w_gate"][l], grads[f + "_w_up"][l] = d_wgu[:, :half], d_wgu[:, half:]
        grads[f + "_w_down"][l] = swiglu_mm_tn(gu, dx, 0.5, "ffn_dwdown")
        grads[f + "_norm"][l] = d_gain.reshape(-1)
        return dx_new

    scatter = [None] * len(GROUPS)

    def start_scatters(tie):
        for g in reversed(range(1, len(GROUPS))):
            if scatter[g] is None and _grads_ready(grads, g):
                pieces = _group_grads(grads, g)
                handle, started = exchange_start(pieces, False, f"scatter_{g}_start")
                scatter[g], tie = (handle, pieces), started[0, 0]
        return tie

    tie = 0.0
    for l in reversed(range(DEPTH)):
        st = saved[l]
        dx = ffn_bwd(dx, st["x2"], st["gu2"], "ffn2", l, tie)
        proj = st["proj"]
        key_in, key_out, idx = ("a_w_in", "a_w_out", l) if l < N_A else ("b_w_in", "b_w_out", l - N_A)
        w_in, w_out = whole(key_in, idx), whole(key_out, idx)
        dy = mm_nt(dx, w_out, 1.0, "mix_dy")
        grads[key_out][idx] = mm_tn(st["y"], dx, 1.0, "mix_dwout", tb_target=1024)
        if l < N_A:
            d_uv, d_ws, d_bs, d_vgain = gmlp_bwd(proj, dy, row(a_v_full[l]), a_w_spatial[l], bias[l], "gmlp_bwd")
            grads["a_w_spatial"][l], grads["a_b_spatial"][l], grads["a_v_norm"][l] = d_ws, d_bs[:, :, 0], d_vgain.reshape(-1)
            d_q, d_k, d_v = mem_bwd(proj, 2 * GM_W // MEM_W, mem_kv, l, dy, GM_W // MEM_W, "mem_bwd_a")
            d_proj = jnp.concatenate([d_uv, d_q], axis=1)
        else:
            d_qsb, d_ksb, d_vsb = sb_bwd(proj, kv, st["sb_out"], dy, "sb_bwd")
            d_kv.append(jnp.concatenate([d_ksb, d_vsb], axis=1))
            d_q, d_k, d_v = mem_bwd(proj, SB_W // MEM_W, mem_kv, l, dy, SB_W // MEM_W, "mem_bwd_b")
            d_proj = jnp.concatenate([d_qsb, d_q], axis=1)
        d_mem_kv[l] = jnp.concatenate([d_k, d_v], axis=1)
        dx, d_gain, h = mm_nt_normbwd(d_proj, w_in, st["x1"], row(mix_norm[l]), dx, "mix_dx")
        grads["mix_norm"][l] = d_gain.reshape(-1)
        grads[key_in][idx] = mm_tn(h, d_proj, 1.0, "mix_dwin")
        tie = start_scatters(tie)
        dx = ffn_bwd(dx, st["x0"], st["gu1"], "ffn1", l, tie)
        if l == N_A:
            d_kv_b = sum_leading(jnp.stack(d_kv), BF16, "kv_dsum")
            dx, d_gain, h = mm_nt_normbwd(d_kv_b, whole("w_kv"), x_kv, row(kv_norm), dx, "kv_dx")
            grads["kv_norm"] = d_gain.reshape(-1)
            grads["w_kv"] = mm_tn(h, d_kv_b, 1.0, "kv_dw")
        tie = start_scatters(tie)

    d_mem_all = jnp.concatenate(d_mem_kv, axis=1).astype(BF16)
    _, d_gain, _ = mm_nt_normbwd(d_mem_all, w_mem_cat, mem_in, row(mem_norm), None, "mem_dnorm")
    grads["mem_norm"] = d_gain.reshape(-1)
    d_wmem = mm_tn(mem_h, d_mem_all, 1.0, "mem_dw")
    grads["w_mem_kv"] = d_wmem.reshape(d_model, DEPTH, -1).transpose(1, 0, 2)

    by_dev = [exchange(_group_grads(grads, 0), "all", False, "scatter_0")]
    by_dev += [_with_own(exchange_wait(scatter[g][0], dx, False, f"scatter_{g}_wait"), scatter[g][1], False, dev)
               for g in range(1, len(GROUPS))]
    parts = {}
    for n in SHARDED:
        pieces = [by_dev[g][list(GROUPS[g]).index(n)] for g in range(len(GROUPS)) if n in GROUPS[g]]
        parts[n] = pieces[0] if len(pieces) == 1 else jnp.concatenate(pieces, axis=1)
    grads = {n: (jnp.stack(g) if isinstance(g, list) else g) for n, g in grads.items()}
    for n, g in zip(SMALL, _all_sum([grads[n] for n in SMALL], "sum_small")):
        parts[n] = g[None]
    parts["a_v_norm"] = lax.dynamic_slice(parts["a_v_norm"], (0, 0, dev * vn_width), (1,) + a_v_norm.shape)

    reduced, deltas, new_m, new_v = {}, {}, {}, {}
    for n in WEIGHTS:
        w = weights[n]
        view = (lambda a: a.reshape(-1, a.shape[-1]))
        res = adamw(view(w), parts[n].reshape(parts[n].shape[0], -1, w.shape[-1]), view(mom1[n]), view(mom2[n]), "adamw")
        reduced[n], deltas[n], new_m[n], new_v[n] = [r.reshape(w.shape) for r in res]

    return (loss, dx[None], *[reduced[n] for n in WEIGHTS], *[deltas[n] for n in WEIGHTS],
            *[new_m[n] for n in WEIGHTS], *[new_v[n] for n in WEIGHTS])
```

```python
import functools

import jax
import jax.numpy as jnp
from jax import lax
from jax.experimental import pallas as pl
from jax.experimental.pallas import tpu as pltpu

F32, BF16 = jnp.float32, jnp.bfloat16
MESH_ID = pl.DeviceIdType.MESH
AXES = ("x", "y", "c")
N_DEV = 8

EPS = 1e-6
DEPTH, N_A = 4, 2
GM_W, GM_GROUPS, GM_P = 768, 6, 128
MEM_W, MEM_HEADS, HEAD_DIM = 256, 4, 64
SB_W, SB_BLK = 768, 128
LANES = 128
QK_SCALE = HEAD_DIM ** -0.5
GELU_C, GELU_A = 0.7978845608028654, 0.044715

ADAM_LR, ADAM_B1, ADAM_B2, ADAM_EPS, ADAM_WD, ADAM_STEP = 0.001, 0.9, 0.999, 1e-08, 0.01, 10

VMEM_LIMIT = 56 * 1024 * 1024
PACK_COLS = 512

NT = (((1,), (1,)), ((), ()))
TN = (((0,), (0,)), ((), ()))


def _params(*sem):
    return pltpu.CompilerParams(dimension_semantics=sem, vmem_limit_bytes=VMEM_LIMIT)


def _tile(n, target, mult=LANES):
    best = None
    for t in range(mult, min(n, target) + 1, mult):
        if n % t == 0:
            best = t
    return best if best is not None else n


def _dot(a, b, dims=None):
    if dims is None:
        return jnp.dot(a, b, preferred_element_type=F32)
    return lax.dot_general(a, b, dims, preferred_element_type=F32)


def exchange(srcs, group, same_src, name, split=False):
    size = {"pair": 2, "quad": 4, "all": 8}[group]
    n = len(srcs)
    chunk_shapes = [tuple(s.shape) if same_src else tuple(s.shape[1:]) for s in srcs]
    pieces = [cs[0] if split else 1 for cs in chunk_shapes]
    n_dma = sum(pieces)

    def body(*refs):
        src_refs, out_refs = refs[:n], refs[n:2 * n]
        send_sems, recv_sems, local_sems = refs[2 * n:]
        x, y, c = lax.axis_index("x"), lax.axis_index("y"), lax.axis_index("c")
        if group == "pair":
            me, dev = c, lambda p: (x, y, p)
        elif group == "quad":
            me, dev = 2 * x + y, lambda p: (p // 2, p % 2, c)
        else:
            me, dev = 4 * x + 2 * y + c, lambda p: (p // 4, (p // 2) % 2, p % 2)

        def chunk(t, idx):
            return src_refs[t] if same_src else src_refs[t].at[idx]

        def copies(k, idx, slot, peer):
            out, w = [], k * n_dma
            for t in range(n):
                src, dst = chunk(t, idx), out_refs[t].at[slot]
                for s_ref, d_ref in ([(src.at[u], dst.at[u]) for u in range(pieces[t])] if split else [(src, dst)]):
                    out.append(pltpu.make_async_remote_copy(
                        src_ref=s_ref, dst_ref=d_ref, send_sem=send_sems.at[w], recv_sem=recv_sems.at[w],
                        device_id=dev(peer), device_id_type=MESH_ID))
                    w += 1
            return out

        local = [pltpu.make_async_copy(chunk(t, me), out_refs[t].at[me], local_sems.at[t]) for t in range(n)]
        for cp in local:
            cp.start()
        sends = []
        for k in range(1, size):
            peer = (me + k) % size
            sends += copies(k, peer, me, peer)
        for cp in sends:
            cp.start()
        for k in range(1, size):
            sender = (me + size - k) % size
            for cp in copies(k, me, sender, sender):
                cp.wait_recv()
        for cp in sends:
            cp.wait_send()
        for cp in local:
            cp.wait()

    hbm = pl.BlockSpec(memory_space=pltpu.HBM)
    return pl.pallas_call(
        body, name=name,
        out_shape=[jax.ShapeDtypeStruct((size,) + cs, s.dtype) for cs, s in zip(chunk_shapes, srcs)],
        in_specs=[hbm] * n, out_specs=[hbm] * n,
        scratch_shapes=[pltpu.SemaphoreType.DMA((size * n_dma,)), pltpu.SemaphoreType.DMA((size * n_dma,)),
                        pltpu.SemaphoreType.DMA((n,))],
    )(*srcs)


HBM_SPEC = pl.BlockSpec(memory_space=pltpu.HBM)
SEM_SPEC = pl.BlockSpec(memory_space=pltpu.SEMAPHORE)
DATAFLOW = pltpu.SideEffectType.DATAFLOW_SIDE_EFFECTING


def _all_devices():
    me = 4 * lax.axis_index("x") + 2 * lax.axis_index("y") + lax.axis_index("c")
    return me, lambda p: (p // 4, (p // 2) % 2, p % 2)


def exchange_start(srcs, same_src, name):
    n = len(srcs)
    chunk_shapes = [tuple(s.shape) if same_src else tuple(s.shape[1:]) for s in srcs]
    srcs = [pltpu.with_memory_space_constraint(s, pltpu.HBM) for s in srcs]
    lands = [pltpu.with_memory_space_constraint(lax.empty((N_DEV,) + cs, s.dtype), pltpu.HBM)
             for cs, s in zip(chunk_shapes, srcs)]

    def body(*refs):
        src_refs, land_refs, send_sems, recv_sems, token = refs[:n], refs[n:2 * n], refs[2 * n], refs[2 * n + 1], refs[-1]
        me, dev = _all_devices()
        for k in range(1, N_DEV):
            peer = (me + k) % N_DEV
            for t in range(n):
                w = (k - 1) * n + t
                pltpu.make_async_remote_copy(
                    src_ref=src_refs[t] if same_src else src_refs[t].at[peer], dst_ref=land_refs[t].at[me],
                    send_sem=send_sems.at[w], recv_sem=recv_sems.at[w], device_id=dev(peer), device_id_type=MESH_ID).start()
        token[...] = jnp.zeros_like(token)

    n_copies = (N_DEV - 1) * n
    out = pl.pallas_call(
        body, name=name,
        out_shape=(pltpu.SemaphoreType.DMA((n_copies,)), pltpu.SemaphoreType.DMA((n_copies,)),
                   *[pltpu.HBM(a.shape, a.dtype) for a in srcs + lands], jax.ShapeDtypeStruct((8, LANES), F32)),
        in_specs=[HBM_SPEC] * (2 * n),
        out_specs=(SEM_SPEC, SEM_SPEC, *[HBM_SPEC] * (2 * n), pl.BlockSpec(memory_space=pltpu.VMEM)),
        input_output_aliases={t: 2 + t for t in range(2 * n)},
        compiler_params=pltpu.CompilerParams(has_side_effects=DATAFLOW),
    )(*srcs, *lands)
    return (out[0], out[1], list(out[2:2 + n]), list(out[2 + n:2 + 2 * n])), out[-1]


def exchange_wait(handle, after, same_src, name):
    send_sems, recv_sems, srcs, lands = handle
    n = len(srcs)

    def body(*refs):
        src_refs, land_refs, send_sems, recv_sems = refs[:n], refs[n:2 * n], refs[2 * n], refs[2 * n + 1]
        me, dev = _all_devices()
        for k in range(1, N_DEV):
            sender = (me + N_DEV - k) % N_DEV
            for t in range(n):
                w = (k - 1) * n + t
                copy = pltpu.make_async_remote_copy(
                    src_ref=src_refs[t] if same_src else src_refs[t].at[me], dst_ref=land_refs[t].at[sender],
                    send_sem=send_sems.at[w], recv_sem=recv_sems.at[w], device_id=dev(sender), device_id_type=MESH_ID)
                copy.wait_send()
                copy.wait_recv()

    out = pl.pallas_call(
        body, name=name,
        out_shape=[pltpu.HBM(a.shape, a.dtype) for a in srcs + lands],
        in_specs=[HBM_SPEC] * (2 * n) + [SEM_SPEC, SEM_SPEC, pl.BlockSpec(memory_space=pl.ANY)],
        out_specs=[HBM_SPEC] * (2 * n),
        input_output_aliases={t: t for t in range(2 * n)},
        compiler_params=pltpu.CompilerParams(has_side_effects=DATAFLOW),
    )(*srcs, *lands, send_sems, recv_sems, after)
    return list(out[n:])


def sum_leading(parts, out_dtype, name):
    k, rows, cols = parts.shape
    tr = _tile(rows, 512, 16)

    def body(p_ref, o_ref):
        acc = p_ref[0].astype(F32)
        for s in range(1, k):
            acc = acc + p_ref[s].astype(F32)
        o_ref[...] = acc.astype(o_ref.dtype)

    return pl.pallas_call(
        body, name=name, grid=(rows // tr,),
        in_specs=[pl.BlockSpec((k, tr, cols), lambda i: (0, i, 0))],
        out_specs=pl.BlockSpec((tr, cols), lambda i: (i, 0)),
        out_shape=jax.ShapeDtypeStruct((rows, cols), out_dtype),
        compiler_params=_params("arbitrary"),
    )(parts)


def _rms(xf):
    return lax.rsqrt(jnp.mean(xf * xf, axis=-1, keepdims=True) + EPS)


def norm_mm(x, g, w, out_dtype, name, emit_h=False):
    m, d = x.shape
    n = w.shape[1]
    tm, tn = _tile(m, 1024, 8), _tile(n, 1408)

    def body(x_ref, g_ref, w_ref, o_ref, *rest):
        h_ref = rest[-1]

        @pl.when(pl.program_id(1) == 0)
        def _():
            xf = x_ref[...]
            hb = ((xf * _rms(xf)) * g_ref[...]).astype(BF16)
            h_ref[...] = hb
            if emit_h:
                rest[0][...] = hb

        o_ref[...] = _dot(h_ref[...], w_ref[...]).astype(o_ref.dtype)

    out_shape = [jax.ShapeDtypeStruct((m, n), out_dtype)]
    out_specs = [pl.BlockSpec((tm, tn), lambda i, j: (i, j))]
    if emit_h:
        out_shape.append(jax.ShapeDtypeStruct((m, d), BF16))
        out_specs.append(pl.BlockSpec((tm, d), lambda i, j: (i, 0)))
    res = pl.pallas_call(
        body, name=name, grid=(m // tm, n // tn),
        in_specs=[pl.BlockSpec((tm, d), lambda i, j: (i, 0)), pl.BlockSpec((1, d), lambda i, j: (0, 0)),
                  pl.BlockSpec((d, tn), lambda i, j: (0, j))],
        out_specs=out_specs, out_shape=out_shape,
        scratch_shapes=[pltpu.VMEM((tm, d), BF16)],
        compiler_params=_params("arbitrary", "arbitrary"),
    )(x, g, w)
    return res if emit_h else res[0]


def mm_res(a, w, res, alpha, name):
    m, k = a.shape
    n = w.shape[1]
    tm, tn = _tile(m, 1024, 8), _tile(n, 1024)

    def body(a_ref, w_ref, r_ref, o_ref):
        o_ref[...] = r_ref[...] + alpha * _dot(a_ref[...], w_ref[...])

    return pl.pallas_call(
        body, name=name, grid=(m // tm, n // tn),
        in_specs=[pl.BlockSpec((tm, k), lambda i, j: (i, 0)), pl.BlockSpec((k, tn), lambda i, j: (0, j)),
                  pl.BlockSpec((tm, tn), lambda i, j: (i, j))],
        out_specs=pl.BlockSpec((tm, tn), lambda i, j: (i, j)),
        out_shape=jax.ShapeDtypeStruct((m, n), F32),
        compiler_params=_params("arbitrary", "arbitrary"),
    )(a, w, res)


def mm_nt(x, w, alpha, name):
    m, d = x.shape
    n = w.shape[0]
    tm, tn = _tile(m, 1024, 8), _tile(n, 1408)

    def body(x_ref, w_ref, o_ref, xb_ref):
        @pl.when(pl.program_id(1) == 0)
        def _():
            xb_ref[...] = x_ref[...].astype(BF16)

        o_ref[...] = (alpha * _dot(xb_ref[...], w_ref[...], NT)).astype(o_ref.dtype)

    return pl.pallas_call(
        body, name=name, grid=(m // tm, n // tn),
        in_specs=[pl.BlockSpec((tm, d), lambda i, j: (i, 0)), pl.BlockSpec((tn, d), lambda i, j: (j, 0))],
        out_specs=pl.BlockSpec((tm, tn), lambda i, j: (i, j)),
        out_shape=jax.ShapeDtypeStruct((m, n), BF16),
        scratch_shapes=[pltpu.VMEM((tm, d), BF16)],
        compiler_params=_params("arbitrary", "arbitrary"),
    )(x, w)


def mm_tn(a, b, alpha, name, ta_target=1024, tb_target=512):
    s, ka = a.shape
    nb = b.shape[1]
    ta, tb, ts = _tile(ka, ta_target), _tile(nb, tb_target), _tile(s, 1024, 16)
    steps = s // ts

    def body(a_ref, b_ref, o_ref, acc_ref):
        t = pl.program_id(2)

        @pl.when(t == 0)
        def _():
            acc_ref[...] = jnp.zeros_like(acc_ref)

        acc_ref[...] += _dot(a_ref[...].astype(BF16), b_ref[...].astype(BF16), TN)

        @pl.when(t == steps - 1)
        def _():
            o_ref[...] = alpha * acc_ref[...]

    return pl.pallas_call(
        body, name=name, grid=(ka // ta, nb // tb, steps),
        in_specs=[pl.BlockSpec((ts, ta), lambda i, j, t: (t, i)), pl.BlockSpec((ts, tb), lambda i, j, t: (t, j))],
        out_specs=pl.BlockSpec((ta, tb), lambda i, j, t: (i, j)),
        out_shape=jax.ShapeDtypeStruct((ka, nb), F32),
        scratch_shapes=[pltpu.VMEM((ta, tb), F32)],
        compiler_params=_params("arbitrary", "arbitrary", "arbitrary"),
    )(a, b)


def mm_nt_normbwd(dy, w, x, g, res, name):
    m, n = dy.shape
    d = w.shape[0]
    tm, tk = _tile(m, 1024, 8), _tile(n, 1408)
    steps = n // tk
    has_res = res is not None

    def body(*refs):
        if has_res:
            dy_ref, w_ref, x_ref, g_ref, r_ref, dx_ref, dg_ref, h_ref, acc_ref = refs
        else:
            dy_ref, w_ref, x_ref, g_ref, dx_ref, dg_ref, h_ref, acc_ref = refs
        i, t = pl.program_id(0), pl.program_id(1)

        @pl.when(t == 0)
        def _():
            acc_ref[...] = jnp.zeros_like(acc_ref)

        @pl.when((t == 0) & (i == 0))
        def _():
            dg_ref[...] = jnp.zeros_like(dg_ref)

        acc_ref[...] += _dot(dy_ref[...], w_ref[...], NT)

        @pl.when(t == steps - 1)
        def _():
            xf = x_ref[...]
            r = _rms(xf)
            xhat = xf * r
            dh = acc_ref[...]
            gain = g_ref[...]
            dg_ref[...] += jnp.sum(dh * xhat, axis=0, keepdims=True)
            dxhat = dh * gain
            dx = r * (dxhat - xhat * jnp.mean(dxhat * xhat, axis=-1, keepdims=True))
            dx_ref[...] = (r_ref[...] + dx) if has_res else dx
            h_ref[...] = (xhat * gain).astype(BF16)

    row = lambda i, t: (i, 0)
    in_specs = [pl.BlockSpec((tm, tk), lambda i, t: (i, t)), pl.BlockSpec((d, tk), lambda i, t: (0, t)),
                pl.BlockSpec((tm, d), row), pl.BlockSpec((1, d), lambda i, t: (0, 0))]
    args = [dy, w, x, g]
    if has_res:
        in_specs.append(pl.BlockSpec((tm, d), row))
        args.append(res)
    return pl.pallas_call(
        body, name=name, grid=(m // tm, steps),
        in_specs=in_specs,
        out_specs=[pl.BlockSpec((tm, d), row), pl.BlockSpec((1, d), lambda i, t: (0, 0)), pl.BlockSpec((tm, d), row)],
        out_shape=[jax.ShapeDtypeStruct((m, d), F32), jax.ShapeDtypeStruct((1, d), F32), jax.ShapeDtypeStruct((m, d), BF16)],
        scratch_shapes=[pltpu.VMEM((tm, d), F32)],
        compiler_params=_params("arbitrary", "arbitrary"),
    )(*args)


def _sigmoid(z):
    return 1.0 / (1.0 + jnp.exp(-z))


def _swiglu(gate_b, up_b):
    gate = gate_b.astype(F32)
    return (gate * _sigmoid(gate) * up_b.astype(F32)).astype(BF16)


def swiglu_mm_res(gu, w, res, alpha, name):
    m, f2 = gu.shape
    f, n = w.shape
    tm, tc = _tile(m, 256, 16), _tile(f, 1408)

    def body(gu_ref, w_ref, r_ref, o_ref):
        acc = jnp.zeros((tm, n), F32)
        for c0 in range(0, f, tc):
            act = _swiglu(gu_ref[:, c0:c0 + tc], gu_ref[:, f + c0:f + c0 + tc])
            acc = acc + _dot(act, w_ref[c0:c0 + tc, :])
        o_ref[...] = r_ref[...] + alpha * acc

    return pl.pallas_call(
        body, name=name, grid=(m // tm,),
        in_specs=[pl.BlockSpec((tm, f2), lambda i: (i, 0)), pl.BlockSpec((f, n), lambda i: (0, 0)),
                  pl.BlockSpec((tm, n), lambda i: (i, 0))],
        out_specs=pl.BlockSpec((tm, n), lambda i: (i, 0)),
        out_shape=jax.ShapeDtypeStruct((m, n), F32), compiler_params=_params("arbitrary"),
    )(gu, w, res)


def swiglu_mm_tn(gu, b, alpha, name):
    s, f2 = gu.shape
    f, n = f2 // 2, b.shape[1]
    ta, ts = _tile(f, 1408), _tile(s, 512, 16)
    steps, half = s // ts, f // ta

    def body(g_ref, u_ref, b_ref, o_ref, acc_ref):
        t = pl.program_id(1)

        @pl.when(t == 0)
        def _():
            acc_ref[...] = jnp.zeros_like(acc_ref)

        acc_ref[...] += _dot(_swiglu(g_ref[...], u_ref[...]), b_ref[...].astype(BF16), TN)

        @pl.when(t == steps - 1)
        def _():
            o_ref[...] = alpha * acc_ref[...]

    return pl.pallas_call(
        body, name=name, grid=(half, steps),
        in_specs=[pl.BlockSpec((ts, ta), lambda i, t: (t, i)), pl.BlockSpec((ts, ta), lambda i, t: (t, half + i)),
                  pl.BlockSpec((ts, n), lambda i, t: (t, 0))],
        out_specs=pl.BlockSpec((ta, n), lambda i, t: (i, 0)),
        out_shape=jax.ShapeDtypeStruct((f, n), F32),
        scratch_shapes=[pltpu.VMEM((ta, n), F32)],
        compiler_params=_params("arbitrary", "arbitrary"),
    )(gu, gu, b)


def mm_nt_swiglu_bwd(x, w, gu, alpha, name):
    m, d = x.shape
    f = w.shape[0]
    tm, tc = _tile(m, 256, 16), _tile(f, 1408)

    def body(x_ref, w_ref, gu_ref, o_ref):
        xb = x_ref[...].astype(BF16)
        for c0 in range(0, f, tc):
            d_act = alpha * _dot(xb, w_ref[c0:c0 + tc, :], NT)
            gate, up = gu_ref[:, c0:c0 + tc].astype(F32), gu_ref[:, f + c0:f + c0 + tc].astype(F32)
            sg = _sigmoid(gate)
            o_ref[:, c0:c0 + tc] = (d_act * up * (sg * (1.0 + gate * (1.0 - sg)))).astype(BF16)
            o_ref[:, f + c0:f + c0 + tc] = (d_act * (gate * sg)).astype(BF16)

    return pl.pallas_call(
        body, name=name, grid=(m // tm,),
        in_specs=[pl.BlockSpec((tm, d), lambda i: (i, 0)), pl.BlockSpec((f, d), lambda i: (0, 0)),
                  pl.BlockSpec((tm, 2 * f), lambda i: (i, 0))],
        out_specs=pl.BlockSpec((tm, 2 * f), lambda i: (i, 0)),
        out_shape=jax.ShapeDtypeStruct((m, 2 * f), BF16), compiler_params=_params("arbitrary"),
    )(x, w, gu)


def _gelu(x):
    return 0.5 * x * (1.0 + jnp.tanh(GELU_C * (x + GELU_A * x * x * x)))


def _gelu_grad(x):
    t = jnp.tanh(GELU_C * (x + GELU_A * x * x * x))
    return 0.5 * (1.0 + t) + 0.5 * x * (1.0 - t * t) * (GELU_C * (1.0 + 3.0 * GELU_A * x * x))


def _chunk_mask():
    row = lax.broadcasted_iota(jnp.int32, (GM_P, GM_P), 0)
    col = lax.broadcasted_iota(jnp.int32, (GM_P, GM_P), 1)
    return (col < GM_P // 2) | (row >= GM_P // 2)


def gmlp_fwd(proj, gain, w_s, bias, name):
    s, pw = proj.shape
    tm = _tile(s, 256, GM_P)

    def body(p_ref, gain_ref, w_ref, b_ref, o_ref):
        mask = _chunk_mask()
        u = _gelu(p_ref[:, :GM_W])
        v = _gelu(p_ref[:, GM_W:2 * GM_W])
        vn = ((v * _rms(v)) * gain_ref[...]).astype(BF16)
        for g in range(GM_GROUPS):
            wg = jnp.where(mask, w_ref[g], 0.0).astype(BF16)
            cols = slice(g * GM_P, (g + 1) * GM_P)
            for n in range(tm // GM_P):
                rows = slice(n * GM_P, (n + 1) * GM_P)
                mixed = _dot(wg, vn[rows, cols]) + b_ref[:, cols]
                o_ref[rows, cols] = (u[rows, cols] * mixed).astype(BF16)

    return pl.pallas_call(
        body, name=name, grid=(s // tm,),
        in_specs=[pl.BlockSpec((tm, pw), lambda i: (i, 0)), pl.BlockSpec((1, GM_W), lambda i: (0, 0)),
                  pl.BlockSpec((GM_GROUPS, GM_P, GM_P), lambda i: (0, 0, 0)), pl.BlockSpec((GM_P, GM_W), lambda i: (0, 0))],
        out_specs=pl.BlockSpec((tm, GM_W), lambda i: (i, 0)),
        out_shape=jax.ShapeDtypeStruct((s, GM_W), BF16), compiler_params=_params("arbitrary"),
    )(proj, gain, w_s, bias)


def gmlp_bwd(proj, dy, gain, w_s, bias, name):
    s, pw = proj.shape
    dw_total = dy.shape[1]
    tm = _tile(s, 256, GM_P)

    def body(p_ref, dy_ref, gain_ref, w_ref, b_ref, dp_ref, dw_ref, db_ref, dgain_ref, dvn_ref):
        @pl.when(pl.program_id(0) == 0)
        def _():
            dw_ref[...] = jnp.zeros_like(dw_ref)
            db_ref[...] = jnp.zeros_like(db_ref)
            dgain_ref[...] = jnp.zeros_like(dgain_ref)

        mask = _chunk_mask()
        pu = p_ref[:, :GM_W]
        pv = p_ref[:, GM_W:2 * GM_W]
        u = _gelu(pu)
        v = _gelu(pv)
        r = _rms(v)
        vhat = v * r
        gain = gain_ref[...]
        vn = (vhat * gain).astype(BF16)
        gu_grad = _gelu_grad(pu)
        for g in range(GM_GROUPS):
            wg = jnp.where(mask, w_ref[g], 0.0).astype(BF16)
            cols = slice(g * GM_P, (g + 1) * GM_P)
            dw_acc = jnp.zeros((GM_P, GM_P), F32)
            db_acc = jnp.zeros((GM_P, 1), F32)
            for n in range(tm // GM_P):
                rows = slice(n * GM_P, (n + 1) * GM_P)
                dyb = dy_ref[rows, cols].astype(F32)
                vnb = vn[rows, cols]
                mixed = _dot(wg, vnb) + b_ref[:, cols]
                dmixed = dyb * u[rows, cols]
                dmb = dmixed.astype(BF16)
                dp_ref[rows, cols] = (dyb * mixed * gu_grad[rows, cols]).astype(BF16)
                dw_acc = dw_acc + _dot(dmb, vnb, NT)
                db_acc = db_acc + jnp.sum(dmixed, axis=1, keepdims=True)
                dvn_ref[rows, cols] = _dot(wg, dmb, TN)
            dw_ref[g] += jnp.where(mask, dw_acc, 0.0)
            db_ref[g] += jnp.broadcast_to(db_acc, (GM_P, GM_P))
        dvn = dvn_ref[...]
        dgain_ref[...] += jnp.sum(dvn * vhat, axis=0, keepdims=True)
        dvhat = dvn * gain
        dv = r * (dvhat - vhat * jnp.mean(dvhat * vhat, axis=-1, keepdims=True))
        dp_ref[:, GM_W:] = (dv * _gelu_grad(pv)).astype(BF16)

    const3 = lambda i: (0, 0, 0)
    return pl.pallas_call(
        body, name=name, grid=(s // tm,),
        in_specs=[pl.BlockSpec((tm, pw), lambda i: (i, 0)), pl.BlockSpec((tm, dw_total), lambda i: (i, 0)),
                  pl.BlockSpec((1, GM_W), lambda i: (0, 0)), pl.BlockSpec((GM_GROUPS, GM_P, GM_P), const3),
                  pl.BlockSpec((GM_P, GM_W), lambda i: (0, 0))],
        out_specs=[pl.BlockSpec((tm, 2 * GM_W), lambda i: (i, 0)), pl.BlockSpec((GM_GROUPS, GM_P, GM_P), const3),
                   pl.BlockSpec((GM_GROUPS, GM_P, GM_P), const3), pl.BlockSpec((1, GM_W), lambda i: (0, 0))],
        out_shape=[jax.ShapeDtypeStruct((s, 2 * GM_W), BF16), jax.ShapeDtypeStruct((GM_GROUPS, GM_P, GM_P), F32),
                   jax.ShapeDtypeStruct((GM_GROUPS, GM_P, GM_P), F32), jax.ShapeDtypeStruct((1, GM_W), F32)],
        scratch_shapes=[pltpu.VMEM((tm, GM_W), F32)],
        compiler_params=_params("arbitrary"),
    )(proj, dy, gain, w_s, bias)


def _keep(mask, xb):
    return jnp.where(mask, xb.astype(F32), 0.0).astype(BF16)


def _head_masks(rows, width, heads):
    lane = lax.broadcasted_iota(jnp.int32, (rows, width), 1)
    return [(lane >= HEAD_DIM * h) & (lane < HEAD_DIM * (h + 1)) for h in range(heads)]


def _mem_probs(qh, k):
    sc = _dot(qh, k, NT) * QK_SCALE
    e = jnp.exp(sc - jnp.max(sc, axis=-1, keepdims=True))
    return e / jnp.sum(e, axis=-1, keepdims=True)


def mem_fwd(proj, q_blk, mem_kv, layer, name):
    s = proj.shape[0]
    n_mem = mem_kv.shape[0]
    tm = _tile(s, 512, 16)

    def body(q_ref, k_ref, v_ref, o_ref):
        q = q_ref[...].astype(BF16)
        k, v = k_ref[...], v_ref[...]
        out = jnp.zeros((tm, MEM_W), F32)
        for hm in _head_masks(tm, MEM_W, MEM_HEADS):
            p = _mem_probs(_keep(hm, q), k)
            out = out + jnp.where(hm, _dot(p.astype(BF16), v), 0.0)
        o_ref[...] = out.astype(BF16)

    return pl.pallas_call(
        body, name=name, grid=(s // tm,),
        in_specs=[pl.BlockSpec((tm, MEM_W), lambda i: (i, q_blk)), pl.BlockSpec((n_mem, MEM_W), lambda i: (0, 2 * layer)),
                  pl.BlockSpec((n_mem, MEM_W), lambda i: (0, 2 * layer + 1))],
        out_specs=pl.BlockSpec((tm, MEM_W), lambda i: (i, 0)),
        out_shape=jax.ShapeDtypeStruct((s, MEM_W), BF16), compiler_params=_params("arbitrary"),
    )(proj, mem_kv, mem_kv)


def mem_bwd(proj, q_blk, mem_kv, layer, dy, dy_blk, name):
    s = proj.shape[0]
    n_mem = mem_kv.shape[0]
    tm = _tile(s, 512, 16)

    def body(q_ref, k_ref, v_ref, dy_ref, dq_ref, dk_ref, dv_ref):
        @pl.when(pl.program_id(0) == 0)
        def _():
            dk_ref[...] = jnp.zeros_like(dk_ref)
            dv_ref[...] = jnp.zeros_like(dv_ref)

        q = q_ref[...].astype(BF16)
        k, v = k_ref[...], v_ref[...]
        dy = dy_ref[...]
        dq = jnp.zeros((tm, MEM_W), F32)
        dk = jnp.zeros((n_mem, MEM_W), F32)
        dv = jnp.zeros((n_mem, MEM_W), F32)
        for hm in _head_masks(tm, MEM_W, MEM_HEADS):
            qh = _keep(hm, q)
            dyh = _keep(hm, dy)
            p = _mem_probs(qh, k)
            dp = _dot(dyh, v, NT)
            dv = dv + _dot(p.astype(BF16), dyh, TN)
            ds = (p * (dp - jnp.sum(dp * p, axis=-1, keepdims=True)) * QK_SCALE).astype(BF16)
            dq = dq + jnp.where(hm, _dot(ds, k), 0.0)
            dk = dk + _dot(ds, qh, TN)
        dq_ref[...] = dq.astype(BF16)
        dk_ref[...] += dk
        dv_ref[...] += dv

    const = lambda i: (0, 0)
    return pl.pallas_call(
        body, name=name, grid=(s // tm,),
        in_specs=[pl.BlockSpec((tm, MEM_W), lambda i: (i, q_blk)), pl.BlockSpec((n_mem, MEM_W), lambda i: (0, 2 * layer)),
                  pl.BlockSpec((n_mem, MEM_W), lambda i: (0, 2 * layer + 1)), pl.BlockSpec((tm, MEM_W), lambda i: (i, dy_blk))],
        out_specs=[pl.BlockSpec((tm, MEM_W), lambda i: (i, 0)), pl.BlockSpec((n_mem, MEM_W), const),
                   pl.BlockSpec((n_mem, MEM_W), const)],
        out_shape=[jax.ShapeDtypeStruct((s, MEM_W), BF16), jax.ShapeDtypeStruct((n_mem, MEM_W), F32),
                   jax.ShapeDtypeStruct((n_mem, MEM_W), F32)],
        compiler_params=_params("arbitrary"),
    )(proj, mem_kv, mem_kv, dy)


SB_KEYS = 512
SB_SUB = SB_KEYS // SB_BLK
SB_QROWS = 256
SB_QB = SB_QROWS // SB_BLK
SB_CHAINS = 2 * SB_QB


def _split(xf):
    hi = xf.astype(BF16)
    return hi, (xf - hi.astype(F32)).astype(BF16)


def _sb_consts():
    row = lax.bitwise_and(lax.broadcasted_iota(jnp.int32, (2 * SB_BLK, 2 * SB_BLK), 0), SB_BLK - 1)
    col = lax.broadcasted_iota(jnp.int32, (2 * SB_BLK, 2 * SB_BLK), 1)
    ones = col >= SB_BLK
    after2 = jnp.where(ones | (row > col), -1.0, 0.0).astype(BF16)
    from2 = jnp.where(ones | (row >= col), 1.0, 0.0).astype(BF16)
    r = lax.broadcasted_iota(jnp.int32, (SB_BLK, SB_BLK), 0)
    c = lax.broadcasted_iota(jnp.int32, (SB_BLK, SB_BLK), 1)
    return after2, from2, c - r, [c < HEAD_DIM, c >= HEAD_DIM]


def _suffix(xf, tri2):
    hi, lo = _split(xf)
    return _dot(jnp.concatenate([hi, lo], axis=1), tri2)


def _sb_logs(z, mask):
    softplus = jnp.maximum(z, 0.0) + jnp.log(1.0 + jnp.exp(-jnp.abs(z)))
    log_beta = z - softplus
    if mask is not None:
        softplus = jnp.where(mask, softplus, 0.0)
    return softplus, log_beta


def _sb_queries(q_ref, heads):
    q = q_ref[...].astype(F32) * QK_SCALE
    return [jnp.where(hm, q[r * SB_BLK:(r + 1) * SB_BLK], 0.0).astype(BF16) for r in range(SB_QB) for hm in heads]


def _sb_walk(i, block, state):
    assert SB_SUB == 2 * SB_QB
    own = lax.shift_right_logical(i * SB_QB, SB_SUB.bit_length() - 1)
    firsts = [[(v * SB_QB + r) * SB_BLK for r in range(SB_QB) for _ in range(2)] for v in range(2)]
    state = lax.cond(lax.bitwise_and(i, 1) == 0, lambda st: block(own, st, firsts[0]),
                     lambda st: block(own, st, firsts[1]), state)
    return lax.fori_loop(0, own, lambda t, st: block(own - 1 - t, st, None), state)


def _sb_tiles(first):
    out = []
    for c in reversed(range(SB_SUB)):
        for n in range(SB_CHAINS):
            if first is None or c * SB_BLK < first[n]:
                out.append((c, n, "before"))
            elif c * SB_BLK == first[n]:
                out.append((c, n, "diagonal"))
    return out


def _sb_heads_apart(stacked, heads, r):
    return jnp.where(heads[0], stacked[2 * r * SB_BLK:(2 * r + 1) * SB_BLK],
                     stacked[(2 * r + 1) * SB_BLK:(2 * r + 2) * SB_BLK])


def sb_fwd(proj, kv, name):
    s = proj.shape[0]
    assert s % SB_KEYS == 0 and SB_KEYS % SB_QROWS == 0

    def body(q_ref, k_ref, v_ref, o_ref):
        after2, _, col_minus_row, heads = _sb_consts()
        q_all = jnp.concatenate(_sb_queries(q_ref, heads), axis=0)
        key_before_query = col_minus_row < 0

        def block(j, state, first):
            runs, acc = list(state[0]), state[1]
            rows = pl.ds(pl.multiple_of(j * SB_KEYS, SB_KEYS), SB_KEYS)
            kb, vb = k_ref[rows, :], v_ref[rows, :]
            z = _dot(q_all, kb, NT)
            pend = {}
            parts = [[jnp.zeros((SB_BLK, SB_BLK), BF16)] * SB_SUB for _ in range(SB_CHAINS)]
            for c, n, where in _sb_tiles(first):
                mask = key_before_query if where == "diagonal" else None
                softplus, lb = _sb_logs(z[n * SB_BLK:(n + 1) * SB_BLK, c * SB_BLK:(c + 1) * SB_BLK], mask)
                pend[c, n] = (lb, _suffix(softplus, after2), mask)
            for c, n, _ in _sb_tiles(first):
                lb, r, mask = pend.pop((c, n))
                a = jnp.exp(lb + r[:, :SB_BLK] + runs[n])
                if mask is not None:
                    a = jnp.where(mask, a, 0.0)
                parts[n][c] = a.astype(BF16)
                runs[n] = runs[n] + r[:, SB_BLK:]
            a_all = jnp.concatenate([jnp.concatenate(p, axis=1) for p in parts], axis=0)
            return tuple(runs), acc + _dot(a_all, vb)

        zero = jnp.zeros((SB_BLK, LANES), F32)
        state = _sb_walk(pl.program_id(1), block, ((zero,) * SB_CHAINS, jnp.zeros((SB_CHAINS * SB_BLK, LANES), F32)))
        for r in range(SB_QB):
            o_ref[r * SB_BLK:(r + 1) * SB_BLK, :] = _sb_heads_apart(state[1], heads, r)

    pairs = SB_W // LANES
    return pl.pallas_call(
        body, name=name, grid=(pairs, s // SB_QROWS),
        in_specs=[pl.BlockSpec((SB_QROWS, LANES), lambda p, i: (i, p)), pl.BlockSpec((s, LANES), lambda p, i: (0, p)),
                  pl.BlockSpec((s, LANES), lambda p, i: (0, pairs + p))],
        out_specs=pl.BlockSpec((SB_QROWS, LANES), lambda p, i: (i, p)),
        out_shape=jax.ShapeDtypeStruct((s, SB_W), F32),
        compiler_params=_params("arbitrary", "arbitrary"),
    )(proj, kv, kv)


def sb_bwd(proj, kv, out, dy, name):
    s = proj.shape[0]

    def body(q_ref, k_ref, v_ref, o_ref, do_ref, dq_ref, dk_ref, dv_ref):
        i = pl.program_id(1)

        @pl.when(i == 0)
        def _():
            dk_ref[...] = jnp.zeros_like(dk_ref)
            dv_ref[...] = jnp.zeros_like(dv_ref)

        after2, from2, col_minus_row, heads = _sb_consts()
        q_all = jnp.concatenate(_sb_queries(q_ref, heads), axis=0)
        key_before_query = col_minus_row < 0
        d_out = do_ref[...].astype(F32)
        prod = d_out * o_ref[...]
        dos, totals = [], []
        for r in range(SB_QB):
            rr = slice(r * SB_BLK, (r + 1) * SB_BLK)
            for hm in heads:
                dos.append(jnp.where(hm, d_out[rr], 0.0).astype(BF16))
                totals.append(jnp.broadcast_to(jnp.sum(jnp.where(hm, prod[rr], 0.0), axis=1, keepdims=True),
                                               (SB_BLK, SB_BLK)))
        do_all = jnp.concatenate(dos, axis=0)

        def block(j, state, first):
            runs, seens, dq = list(state[0]), list(state[1]), state[2]
            rows = pl.ds(pl.multiple_of(j * SB_KEYS, SB_KEYS), SB_KEYS)
            kb, vb = k_ref[rows, :], v_ref[rows, :]
            z = _dot(q_all, kb, NT)
            da = _dot(do_all, vb, NT)
            pend, pend2 = {}, {}
            a_parts = [[jnp.zeros((SB_BLK, SB_BLK), BF16)] * SB_SUB for _ in range(SB_CHAINS)]
            dz_parts = [[jnp.zeros((SB_BLK, SB_BLK), BF16)] * SB_SUB for _ in range(SB_CHAINS)]
            for c, n, where in _sb_tiles(first):
                mask = key_before_query if where == "diagonal" else None
                softplus, lb = _sb_logs(z[n * SB_BLK:(n + 1) * SB_BLK, c * SB_BLK:(c + 1) * SB_BLK], mask)
                pend[c, n] = (softplus, lb, _suffix(softplus, after2), mask)
            for c, n, _ in _sb_tiles(first):
                softplus, lb, r, mask = pend.pop((c, n))
                a = jnp.exp(lb + r[:, :SB_BLK] + runs[n])
                if mask is not None:
                    a = jnp.where(mask, a, 0.0)
                runs[n] = runs[n] + r[:, SB_BLK:]
                ab = a.astype(BF16)
                a_parts[n][c] = ab
                dl = ab.astype(F32) * da[n * SB_BLK:(n + 1) * SB_BLK, c * SB_BLK:(c + 1) * SB_BLK]
                pend2[c, n] = (softplus, lb, dl, _suffix(dl, from2), mask)
            for c, n, _ in _sb_tiles(first):
                softplus, lb, dl, r2, mask = pend2.pop((c, n))
                d_lom = totals[n] - (r2[:, :SB_BLK] + seens[n])
                if mask is not None:
                    d_lom = jnp.where(mask, d_lom, 0.0)
                seens[n] = seens[n] + r2[:, SB_BLK:]
                dz_parts[n][c] = (dl * jnp.exp(-softplus) - d_lom * jnp.exp(lb)).astype(BF16)
            a_all = jnp.concatenate([jnp.concatenate(p, axis=1) for p in a_parts], axis=0)
            dz_all = jnp.concatenate([jnp.concatenate(p, axis=1) for p in dz_parts], axis=0)
            dv_ref[rows, :] += _dot(a_all, do_all, TN)
            dk_ref[rows, :] += _dot(dz_all, q_all, TN)
            return tuple(runs), tuple(seens), dq + _dot(dz_all, kb)

        zero = jnp.zeros((SB_BLK, LANES), F32)
        state = _sb_walk(i, block, ((zero,) * SB_CHAINS, (zero,) * SB_CHAINS,
                                    jnp.zeros((SB_CHAINS * SB_BLK, LANES), F32)))
        for r in range(SB_QB):
            dq_ref[r * SB_BLK:(r + 1) * SB_BLK, :] = (_sb_heads_apart(state[2], heads, r) * QK_SCALE).astype(BF16)

    pairs = SB_W // LANES
    blk = lambda p, i: (i, p)
    col = lambda p, i: (0, p)
    return pl.pallas_call(
        body, name=name, grid=(pairs, s // SB_QROWS),
        in_specs=[pl.BlockSpec((SB_QROWS, LANES), blk), pl.BlockSpec((s, LANES), col),
                  pl.BlockSpec((s, LANES), lambda p, i: (0, pairs + p)), pl.BlockSpec((SB_QROWS, LANES), blk),
                  pl.BlockSpec((SB_QROWS, LANES), blk)],
        out_specs=[pl.BlockSpec((SB_QROWS, LANES), blk), pl.BlockSpec((s, LANES), col), pl.BlockSpec((s, LANES), col)],
        out_shape=[jax.ShapeDtypeStruct((s, SB_W), BF16), jax.ShapeDtypeStruct((s, SB_W), F32),
                   jax.ShapeDtypeStruct((s, SB_W), F32)],
        compiler_params=_params("arbitrary", "arbitrary"),
    )(proj, kv, kv, out, dy)


def final_loss(x, g, target, name):
    s, d = x.shape
    tm = _tile(s, 256, 8)

    def body(x_ref, g_ref, t_ref, loss_ref, dx_ref, dg_ref):
        @pl.when(pl.program_id(0) == 0)
        def _():
            loss_ref[...] = jnp.zeros_like(loss_ref)
            dg_ref[...] = jnp.zeros_like(dg_ref)

        xf = x_ref[...]
        r = _rms(xf)
        xhat = xf * r
        gain = g_ref[...]
        diff = xhat * gain - t_ref[...]
        sq = jnp.sum(jnp.sum(diff * diff, axis=1, keepdims=True), axis=0, keepdims=True)
        loss_ref[...] += jnp.broadcast_to(sq, loss_ref.shape)
        dy = diff * (1.0 / d)
        dg_ref[...] += jnp.sum(dy * xhat, axis=0, keepdims=True)
        dxhat = dy * gain
        dx_ref[...] = r * (dxhat - xhat * jnp.mean(dxhat * xhat, axis=-1, keepdims=True))

    row = lambda i: (i, 0)
    const = lambda i: (0, 0)
    return pl.pallas_call(
        body, name=name, grid=(s // tm,),
        in_specs=[pl.BlockSpec((tm, d), row), pl.BlockSpec((1, d), const), pl.BlockSpec((tm, d), row)],
        out_specs=[pl.BlockSpec((8, LANES), const), pl.BlockSpec((tm, d), row), pl.BlockSpec((1, d), const)],
        out_shape=[jax.ShapeDtypeStruct((8, LANES), F32), jax.ShapeDtypeStruct((s, d), F32), jax.ShapeDtypeStruct((1, d), F32)],
        compiler_params=_params("arbitrary"),
    )(x, g, target)


def adamw(w, parts, m, v, name):
    rows, cols = w.shape
    k = parts.shape[0]
    tr = _tile(rows, 512, 16)
    c1, c2 = 1.0 - ADAM_B1 ** ADAM_STEP, 1.0 - ADAM_B2 ** ADAM_STEP

    def body(w_ref, p_ref, m_ref, v_ref, g_ref, d_ref, nm_ref, nv_ref):
        grad = p_ref[0].astype(F32)
        for s in range(1, k):
            grad = grad + p_ref[s].astype(F32)
        nm = ADAM_B1 * m_ref[...] + (1.0 - ADAM_B1) * grad
        nv = ADAM_B2 * v_ref[...] + (1.0 - ADAM_B2) * (grad * grad)
        g_ref[...] = grad
        d_ref[...] = -ADAM_LR * ((nm / c1) / (jnp.sqrt(nv / c2) + ADAM_EPS) + ADAM_WD * w_ref[...])
        nm_ref[...] = nm
        nv_ref[...] = nv

    spec = pl.BlockSpec((tr, cols), lambda i: (i, 0))
    shape = jax.ShapeDtypeStruct((rows, cols), F32)
    return pl.pallas_call(
        body, name=name, grid=(rows // tr,),
        in_specs=[spec, pl.BlockSpec((k, tr, cols), lambda i: (0, i, 0)), spec, spec],
        out_specs=[spec] * 4, out_shape=[shape] * 4,
        compiler_params=_params("arbitrary"),
    )(w, parts, m, v)


SHARDED = {"ffn1_w_gate": 2, "ffn1_w_up": 2, "ffn1_w_down": 1, "ffn2_w_gate": 2, "ffn2_w_up": 2, "ffn2_w_down": 1,
           "w_mem_kv": 1, "a_w_in": 2, "a_w_out": 1, "w_kv": 1, "b_w_in": 1, "b_w_out": 1}
SMALL = ["ffn1_norm", "mix_norm", "ffn2_norm", "mem_norm", "kv_norm", "final_norm", "a_v_norm", "a_w_spatial", "a_b_spatial"]
WEIGHTS = ["ffn1_norm", "ffn1_w_gate", "ffn1_w_up", "ffn1_w_down", "mix_norm", "ffn2_norm", "ffn2_w_gate", "ffn2_w_up",
           "ffn2_w_down", "mem_norm", "w_mem_kv", "a_w_in", "a_v_norm", "a_w_spatial", "a_b_spatial", "a_w_out", "kv_norm",
           "w_kv", "b_w_in", "b_w_out", "final_norm"]


FFN1_W = ["ffn1_w_gate", "ffn1_w_up", "ffn1_w_down"]
FFN2_W = ["ffn2_w_gate", "ffn2_w_up", "ffn2_w_down"]
GROUPS = [
    {**{n: (0, 1) for n in FFN1_W}, "w_mem_kv": (0, DEPTH)},
    {"a_w_in": (0, 1), "a_w_out": (0, 1), **{n: (0, 1) for n in FFN2_W}},
    {**{n: (1, 2) for n in FFN1_W + FFN2_W}, "a_w_in": (1, 2), "a_w_out": (1, 2)},
    {**{n: (2, DEPTH) for n in FFN1_W + FFN2_W}, "w_kv": None, "b_w_in": (0, 2), "b_w_out": (0, 2)},
]


def _take(a, rng):
    return a if rng is None else a[rng[0]:rng[1]]


def _group_blocks(shards, g):
    return [_take(shards[n], rng).astype(BF16) for n, rng in GROUPS[g].items()]


def _group_grads(grads, g):
    out = []
    for n, rng in GROUPS[g].items():
        whole = jnp.stack(grads[n][rng[0]:rng[1]]) if isinstance(grads[n], list) else _take(grads[n], rng)
        out.append(jnp.stack(jnp.split(whole.astype(BF16), N_DEV, axis=SHARDED[n])))
    return out


def _with_own(lands, srcs, same_src, dev):
    out = []
    for land, src in zip(lands, srcs):
        own = src if same_src else lax.dynamic_index_in_dim(src, dev, 0, False)
        slot = lax.broadcasted_iota(jnp.int32, (N_DEV,) + (1,) * own.ndim, 0) == dev
        out.append(jnp.where(slot, own[None], land))
    return out


def _grads_ready(grads, g):
    for n, rng in GROUPS[g].items():
        if n not in grads or (isinstance(grads[n], list) and any(p is None for p in grads[n][rng[0]:rng[1]])):
            return False
    return True


def _whole_weights(g, by_dev):
    landed = dict(zip(GROUPS[g], by_dev))
    out = {}
    for n, blocks in landed.items():
        if n.endswith("_w_up"):
            continue
        pieces = [blocks[d] for d in range(N_DEV)]
        if n.endswith("_w_gate"):
            pieces += [landed[n.replace("_w_gate", "_w_up")][d] for d in range(N_DEV)]
        out[n] = jnp.concatenate(pieces, axis=SHARDED[n])
    return out


def _all_sum(parts, name):
    flat = jnp.concatenate([p.reshape(-1) for p in parts])
    pad = (-flat.size) % (16 * LANES)
    buf = jnp.pad(flat, (0, pad)).reshape(-1, LANES)
    total = sum_leading(exchange([buf], "all", True, name)[0], F32, name + "_sum").reshape(-1)
    out, off = [], 0
    for p in parts:
        out.append(total[off:off + p.size].reshape(p.shape))
        off += p.size
    return out


def _device_index():
    return 4 * lax.axis_index("x") + 2 * lax.axis_index("y") + lax.axis_index("c")


def kernel(x, mem, ffn1_norm, ffn1_w_gate, ffn1_w_up, ffn1_w_down, mix_norm, ffn2_norm, ffn2_w_gate, ffn2_w_up, ffn2_w_down, mem_norm, w_mem_kv, a_w_in, a_v_norm, a_w_spatial, a_b_spatial, a_w_out, kv_norm, w_kv, b_w_in, b_w_out, final_norm, loss_target, m_ffn1_norm, m_ffn1_w_gate, m_ffn1_w_up, m_ffn1_w_down, m_mix_norm, m_ffn2_norm, m_ffn2_w_gate, m_ffn2_w_up, m_ffn2_w_down, m_mem_norm, m_w_mem_kv, m_a_w_in, m_a_v_norm, m_a_w_spatial, m_a_b_spatial, m_a_w_out, m_kv_norm, m_w_kv, m_b_w_in, m_b_w_out, m_final_norm, v_ffn1_norm, v_ffn1_w_gate, v_ffn1_w_up, v_ffn1_w_down, v_mix_norm, v_ffn2_norm, v_ffn2_w_gate, v_ffn2_w_up, v_ffn2_w_down, v_mem_norm, v_w_mem_kv, v_a_w_in, v_a_v_norm, v_a_w_spatial, v_a_b_spatial, v_a_w_out, v_kv_norm, v_w_kv, v_b_w_in, v_b_w_out, v_final_norm):
    weights = dict(ffn1_norm=ffn1_norm, ffn1_w_gate=ffn1_w_gate, ffn1_w_up=ffn1_w_up, ffn1_w_down=ffn1_w_down, mix_norm=mix_norm, ffn2_norm=ffn2_norm, ffn2_w_gate=ffn2_w_gate, ffn2_w_up=ffn2_w_up, ffn2_w_down=ffn2_w_down, mem_norm=mem_norm, w_mem_kv=w_mem_kv, a_w_in=a_w_in, a_v_norm=a_v_norm, a_w_spatial=a_w_spatial, a_b_spatial=a_b_spatial, a_w_out=a_w_out, kv_norm=kv_norm, w_kv=w_kv, b_w_in=b_w_in, b_w_out=b_w_out, final_norm=final_norm)
    mom1 = dict(ffn1_norm=m_ffn1_norm, ffn1_w_gate=m_ffn1_w_gate, ffn1_w_up=m_ffn1_w_up, ffn1_w_down=m_ffn1_w_down, mix_norm=m_mix_norm, ffn2_norm=m_ffn2_norm, ffn2_w_gate=m_ffn2_w_gate, ffn2_w_up=m_ffn2_w_up, ffn2_w_down=m_ffn2_w_down, mem_norm=m_mem_norm, w_mem_kv=m_w_mem_kv, a_w_in=m_a_w_in, a_v_norm=m_a_v_norm, a_w_spatial=m_a_w_spatial, a_b_spatial=m_a_b_spatial, a_w_out=m_a_w_out, kv_norm=m_kv_norm, w_kv=m_w_kv, b_w_in=m_b_w_in, b_w_out=m_b_w_out, final_norm=m_final_norm)
    mom2 = dict(ffn1_norm=v_ffn1_norm, ffn1_w_gate=v_ffn1_w_gate, ffn1_w_up=v_ffn1_w_up, ffn1_w_down=v_ffn1_w_down, mix_norm=v_mix_norm, ffn2_norm=v_ffn2_norm, ffn2_w_gate=v_ffn2_w_gate, ffn2_w_up=v_ffn2_w_up, ffn2_w_down=v_ffn2_w_down, mem_norm=v_mem_norm, w_mem_kv=v_w_mem_kv, a_w_in=v_a_w_in, a_v_norm=v_a_v_norm, a_w_spatial=v_a_w_spatial, a_b_spatial=v_a_b_spatial, a_w_out=v_a_w_out, kv_norm=v_kv_norm, w_kv=v_w_kv, b_w_in=v_b_w_in, b_w_out=v_b_w_out, final_norm=v_final_norm)

    dev = _device_index()
    xs, mem_in, target = x[0], mem[0], loss_target[0]
    d_model = xs.shape[1]
    shards = {n: weights[n] for n in SHARDED}
    blocks = [_group_blocks(shards, g) for g in range(len(GROUPS))]
    arrived = exchange(blocks[0], "all", True, "gather_0")
    wholes = [_whole_weights(0, arrived)] + [None] * (len(GROUPS) - 1)
    gathers, order, tie = [None] * len(GROUPS), arrived, 0.0
    for g in range(1, len(GROUPS)):
        _, ordered = lax.optimization_barrier((order, blocks[g]))
        gathers[g], order = exchange_start(ordered, True, f"gather_{g}_start")
        tie = tie + order[0, 0]
    vn_width = a_v_norm.shape[1]
    a_v_full = _all_sum([lax.dynamic_update_slice(jnp.zeros((N_A, N_DEV * vn_width), F32), a_v_norm, (0, dev * vn_width))],
                        "gather_v_norm")[0]
    latest = [xs]

    def whole(n, l=None):
        for g, grp in enumerate(GROUPS):
            if n in grp and (grp[n] is None or grp[n][0] <= l < grp[n][1]):
                if wholes[g] is None:
                    lands = exchange_wait(gathers[g], latest[0], True, f"gather_{g}_wait")
                    wholes[g] = _whole_weights(g, _with_own(lands, blocks[g], True, dev))
                return wholes[g][n] if grp[n] is None else wholes[g][n][l - grp[n][0]]

    def whole_gu(f, l):
        return whole(f + "_w_gate", l)

    row = lambda v: v.reshape(1, -1)
    w_mem_cat = wholes[0]["w_mem_kv"].transpose(1, 0, 2).reshape(d_model, -1)
    bias = [jnp.repeat(a_b_spatial[i].T, GM_P, axis=1) for i in range(N_A)]

    mem_kv, mem_h = norm_mm(mem_in, row(mem_norm) + tie, w_mem_cat, BF16, "mem_kv", emit_h=True)

    def ffn_fwd(xin, f, l, tie=0.0):
        gu = norm_mm(xin, row(weights[f + "_norm"][l]) + tie, whole_gu(f, l), BF16, "ffn_gu")
        return swiglu_mm_res(gu, whole(f + "_w_down", l), xin, 0.5, "ffn_down"), gu

    saved = []
    kv = x_kv = None
    cur = xs
    for l in range(DEPTH):
        st = {"x0": cur}
        if l == N_A:
            x_kv = cur
            kv = norm_mm(cur, row(kv_norm), whole("w_kv"), BF16, "kv_proj")
        st["x1"], st["gu1"] = ffn_fwd(cur, "ffn1", l, tie if l == 0 else 0.0)
        latest[0] = st["x1"]
        if l < N_A:
            proj = norm_mm(st["x1"], row(mix_norm[l]), whole("a_w_in", l), F32, "a_proj")
            y_tok = gmlp_fwd(proj, row(a_v_full[l]), a_w_spatial[l], bias[l], "gmlp_fwd")
            y_mem = mem_fwd(proj, 2 * GM_W // MEM_W, mem_kv, l, "mem_fwd_a")
            w_out = whole("a_w_out", l)
        else:
            proj = norm_mm(st["x1"], row(mix_norm[l]), whole("b_w_in", l - N_A), BF16, "b_proj")
            st["sb_out"] = sb_fwd(proj, kv, "sb_fwd")
            y_tok = st["sb_out"].astype(BF16)
            y_mem = mem_fwd(proj, SB_W // MEM_W, mem_kv, l, "mem_fwd_b")
            w_out = whole("b_w_out", l - N_A)
        st["proj"] = proj
        st["y"] = jnp.concatenate([y_tok, y_mem], axis=1)
        st["x2"] = mm_res(st["y"], w_out, st["x1"], 1.0, "mix_out")
        latest[0] = st["x2"]
        cur, st["gu2"] = ffn_fwd(st["x2"], "ffn2", l)
        latest[0] = cur
        saved.append(st)

    loss_blk, dx, d_final = final_loss(cur, row(final_norm), target, "final_loss")
    loss = lax.psum(loss_blk[0, 0] * (0.5 / d_model), AXES)

    grads = {n: [None] * weights[n].shape[0] for n in WEIGHTS if weights[n].ndim >= 2 and n not in ("w_kv",)}
    grads["final_norm"] = d_final.reshape(-1)
    d_mem_kv = [None] * DEPTH
    d_kv = []

    def ffn_bwd(dx, xin, gu, f, l, tie=0.0):
        d_gu = mm_nt_swiglu_bwd(dx, whole(f + "_w_down", l), gu, 0.5, "ffn_dgu")
        dx_new, d_gain, h = mm_nt_normbwd(d_gu, whole_gu(f, l), xin, row(weights[f + "_norm"][l]) + tie, dx, "ffn_dx")
        d_wgu = mm_tn(h, d_gu, 1.0, "ffn_dwgu", tb_target=1408)
        half = d_wgu.shape[1] // 2
        grads[f + "_w_gate"][l], grads[f + "_w_up"][l] = d_wgu[:, :half], d_wgu[:, half:]
        grads[f + "_w_down"][l] = swiglu_mm_tn(gu, dx, 0.5, "ffn_dwdown")
        grads[f + "_norm"][l] = d_gain.reshape(-1)
        return dx_new

    scatter = [None] * len(GROUPS)

    def start_scatters(tie):
        for g in reversed(range(1, len(GROUPS))):
            if scatter[g] is None and _grads_ready(grads, g):
                pieces = _group_grads(grads, g)
                handle, started = exchange_start(pieces, False, f"scatter_{g}_start")
                scatter[g], tie = (handle, pieces), started[0, 0]
        return tie

    tie = 0.0
    for l in reversed(range(DEPTH)):
        st = saved[l]
        dx = ffn_bwd(dx, st["x2"], st["gu2"], "ffn2", l, tie)
        proj = st["proj"]
        key_in, key_out, idx = ("a_w_in", "a_w_out", l) if l < N_A else ("b_w_in", "b_w_out", l - N_A)
        w_in, w_out = whole(key_in, idx), whole(key_out, idx)
        dy = mm_nt(dx, w_out, 1.0, "mix_dy")
        grads[key_out][idx] = mm_tn(st["y"], dx, 1.0, "mix_dwout", tb_target=1024)
        if l < N_A:
            d_uv, d_ws, d_bs, d_vgain = gmlp_bwd(proj, dy, row(a_v_full[l]), a_w_spatial[l], bias[l], "gmlp_bwd")
            grads["a_w_spatial"][l], grads["a_b_spatial"][l], grads["a_v_norm"][l] = d_ws, d_bs[:, :, 0], d_vgain.reshape(-1)
            d_q, d_k, d_v = mem_bwd(proj, 2 * GM_W // MEM_W, mem_kv, l, dy, GM_W // MEM_W, "mem_bwd_a")
            d_proj = jnp.concatenate([d_uv, d_q], axis=1)
        else:
            d_qsb, d_ksb, d_vsb = sb_bwd(proj, kv, st["sb_out"], dy, "sb_bwd")
            d_kv.append(jnp.concatenate([d_ksb, d_vsb], axis=1))
            d_q, d_k, d_v = mem_bwd(proj, SB_W // MEM_W, mem_kv, l, dy, SB_W // MEM_W, "mem_bwd_b")
            d_proj = jnp.concatenate([d_qsb, d_q], axis=1)
        d_mem_kv[l] = jnp.concatenate([d_k, d_v], axis=1)
        dx, d_gain, h = mm_nt_normbwd(d_proj, w_in, st["x1"], row(mix_norm[l]), dx, "mix_dx")
        grads["mix_norm"][l] = d_gain.reshape(-1)
        grads[key_in][idx] = mm_tn(h, d_proj, 1.0, "mix_dwin")
        tie = start_scatters(tie)
        dx = ffn_bwd(dx, st["x0"], st["gu1"], "ffn1", l, tie)
        if l == N_A:
            d_kv_b = sum_leading(jnp.stack(d_kv), BF16, "kv_dsum")
            dx, d_gain, h = mm_nt_normbwd(d_kv_b, whole("w_kv"), x_kv, row(kv_norm), dx, "kv_dx")
            grads["kv_norm"] = d_gain.reshape(-1)
            grads["w_kv"] = mm_tn(h, d_kv_b, 1.0, "kv_dw")
        tie = start_scatters(tie)

    d_mem_all = jnp.concatenate(d_mem_kv, axis=1).astype(BF16)
    _, d_gain, _ = mm_nt_normbwd(d_mem_all, w_mem_cat, mem_in, row(mem_norm), None, "mem_dnorm")
    grads["mem_norm"] = d_gain.reshape(-1)
    d_wmem = mm_tn(mem_h, d_mem_all, 1.0, "mem_dw")
    grads["w_mem_kv"] = d_wmem.reshape(d_model, DEPTH, -1).transpose(1, 0, 2)

    by_dev = [exchange(_group_grads(grads, 0), "all", False, "scatter_0")]
    by_dev += [_with_own(exchange_wait(scatter[g][0], dx, False, f"scatter_{g}_wait"), scatter[g][1], False, dev)
               for g in range(1, len(GROUPS))]
    parts = {}
    for n in SHARDED:
        pieces = [by_dev[g][list(GROUPS[g]).index(n)] for g in range(len(GROUPS)) if n in GROUPS[g]]
        parts[n] = pieces[0] if len(pieces) == 1 else jnp.concatenate(pieces, axis=1)
    grads = {n: (jnp.stack(g) if isinstance(g, list) else g) for n, g in grads.items()}
    for n, g in zip(SMALL, _all_sum([grads[n] for n in SMALL], "sum_small")):
        parts[n] = g[None]
    parts["a_v_norm"] = lax.dynamic_slice(parts["a_v_norm"], (0, 0, dev * vn_width), (1,) + a_v_norm.shape)

    reduced, deltas, new_m, new_v = {}, {}, {}, {}
    for n in WEIGHTS:
        w = weights[n]
        view = (lambda a: a.reshape(-1, a.shape[-1]))
        res = adamw(view(w), parts[n].reshape(parts[n].shape[0], -1, w.shape[-1]), view(mom1[n]), view(mom2[n]), "adamw")
        reduced[n], deltas[n], new_m[n], new_v[n] = [r.reshape(w.shape) for r in res]

    return (loss, dx[None], *[reduced[n] for n in WEIGHTS], *[deltas[n] for n in WEIGHTS],
            *[new_m[n] for n in WEIGHTS], *[new_v[n] for n in WEIGHTS])
```

```python
import functools

import jax
import jax.numpy as jnp
from jax import lax
from jax.experimental import pallas as pl
from jax.experimental.pallas import tpu as pltpu

F32, BF16 = jnp.float32, jnp.bfloat16
MESH_ID = pl.DeviceIdType.MESH
AXES = ("x", "y", "c")
N_DEV = 8

EPS = 1e-6
DEPTH, N_A = 4, 2
GM_W, GM_GROUPS, GM_P = 768, 6, 128
MEM_W, MEM_HEADS, HEAD_DIM = 256, 4, 64
SB_W, SB_BLK = 768, 128
LANES = 128
QK_SCALE = HEAD_DIM ** -0.5
GELU_C, GELU_A = 0.7978845608028654, 0.044715

ADAM_LR, ADAM_B1, ADAM_B2, ADAM_EPS, ADAM_WD, ADAM_STEP = 0.001, 0.9, 0.999, 1e-08, 0.01, 10

VMEM_LIMIT = 56 * 1024 * 1024
PACK_COLS = 512

NT = (((1,), (1,)), ((), ()))
TN = (((0,), (0,)), ((), ()))


def _params(*sem):
    return pltpu.CompilerParams(dimension_semantics=sem, vmem_limit_bytes=VMEM_LIMIT)


def _tile(n, target, mult=LANES):
    best = None
    for t in range(mult, min(n, target) + 1, mult):
        if n % t == 0:
            best = t
    return best if best is not None else n


def _dot(a, b, dims=None):
    if dims is None:
        return jnp.dot(a, b, preferred_element_type=F32)
    return lax.dot_general(a, b, dims, preferred_element_type=F32)


def exchange(srcs, group, same_src, name, split=False):
    size = {"pair": 2, "quad": 4, "all": 8}[group]
    n = len(srcs)
    chunk_shapes = [tuple(s.shape) if same_src else tuple(s.shape[1:]) for s in srcs]
    pieces = [cs[0] if split else 1 for cs in chunk_shapes]
    n_dma = sum(pieces)

    def body(*refs):
        src_refs, out_refs = refs[:n], refs[n:2 * n]
        send_sems, recv_sems, local_sems = refs[2 * n:]
        x, y, c = lax.axis_index("x"), lax.axis_index("y"), lax.axis_index("c")
        if group == "pair":
            me, dev = c, lambda p: (x, y, p)
        elif group == "quad":
            me, dev = 2 * x + y, lambda p: (p // 2, p % 2, c)
        else:
            me, dev = 4 * x + 2 * y + c, lambda p: (p // 4, (p // 2) % 2, p % 2)

        def chunk(t, idx):
            return src_refs[t] if same_src else src_refs[t].at[idx]

        def copies(k, idx, slot, peer):
            out, w = [], k * n_dma
            for t in range(n):
                src, dst = chunk(t, idx), out_refs[t].at[slot]
                for s_ref, d_ref in ([(src.at[u], dst.at[u]) for u in range(pieces[t])] if split else [(src, dst)]):
                    out.append(pltpu.make_async_remote_copy(
                        src_ref=s_ref, dst_ref=d_ref, send_sem=send_sems.at[w], recv_sem=recv_sems.at[w],
                        device_id=dev(peer), device_id_type=MESH_ID))
                    w += 1
            return out

        local = [pltpu.make_async_copy(chunk(t, me), out_refs[t].at[me], local_sems.at[t]) for t in range(n)]
        for cp in local:
            cp.start()
        sends = []
        for k in range(1, size):
            peer = (me + k) % size
            sends += copies(k, peer, me, peer)
        for cp in sends:
            cp.start()
        for k in range(1, size):
            sender = (me + size - k) % size
            for cp in copies(k, me, sender, sender):
                cp.wait_recv()
        for cp in sends:
            cp.wait_send()
        for cp in local:
            cp.wait()

    hbm = pl.BlockSpec(memory_space=pltpu.HBM)
    return pl.pallas_call(
        body, name=name,
        out_shape=[jax.ShapeDtypeStruct((size,) + cs, s.dtype) for cs, s in zip(chunk_shapes, srcs)],
        in_specs=[hbm] * n, out_specs=[hbm] * n,
        scratch_shapes=[pltpu.SemaphoreType.DMA((size * n_dma,)), pltpu.SemaphoreType.DMA((size * n_dma,)),
                        pltpu.SemaphoreType.DMA((n,))],
    )(*srcs)


def sum_leading(parts, out_dtype, name):
    k, rows, cols = parts.shape
    tr = _tile(rows, 512, 16)

    def body(p_ref, o_ref):
        acc = p_ref[0].astype(F32)
        for s in range(1, k):
            acc = acc + p_ref[s].astype(F32)
        o_ref[...] = acc.astype(o_ref.dtype)

    return pl.pallas_call(
        body, name=name, grid=(rows // tr,),
        in_specs=[pl.BlockSpec((k, tr, cols), lambda i: (0, i, 0))],
        out_specs=pl.BlockSpec((tr, cols), lambda i: (i, 0)),
        out_shape=jax.ShapeDtypeStruct((rows, cols), out_dtype),
        compiler_params=_params("arbitrary"),
    )(parts)


def _rms(xf):
    return lax.rsqrt(jnp.mean(xf * xf, axis=-1, keepdims=True) + EPS)


def norm_mm(x, g, w, out_dtype, name, emit_h=False):
    m, d = x.shape
    n = w.shape[1]
    tm, tn = _tile(m, 1024, 8), _tile(n, 1408)

    def body(x_ref, g_ref, w_ref, o_ref, *rest):
        h_ref = rest[-1]

        @pl.when(pl.program_id(1) == 0)
        def _():
            xf = x_ref[...]
            hb = ((xf * _rms(xf)) * g_ref[...]).astype(BF16)
            h_ref[...] = hb
            if emit_h:
                rest[0][...] = hb

        o_ref[...] = _dot(h_ref[...], w_ref[...]).astype(o_ref.dtype)

    out_shape = [jax.ShapeDtypeStruct((m, n), out_dtype)]
    out_specs = [pl.BlockSpec((tm, tn), lambda i, j: (i, j))]
    if emit_h:
        out_shape.append(jax.ShapeDtypeStruct((m, d), BF16))
        out_specs.append(pl.BlockSpec((tm, d), lambda i, j: (i, 0)))
    res = pl.pallas_call(
        body, name=name, grid=(m // tm, n // tn),
        in_specs=[pl.BlockSpec((tm, d), lambda i, j: (i, 0)), pl.BlockSpec((1, d), lambda i, j: (0, 0)),
                  pl.BlockSpec((d, tn), lambda i, j: (0, j))],
        out_specs=out_specs, out_shape=out_shape,
        scratch_shapes=[pltpu.VMEM((tm, d), BF16)],
        compiler_params=_params("arbitrary", "arbitrary"),
    )(x, g, w)
    return res if emit_h else res[0]


def mm_res(a, w, res, alpha, name):
    m, k = a.shape
    n = w.shape[1]
    tm, tn = _tile(m, 1024, 8), _tile(n, 1024)

    def body(a_ref, w_ref, r_ref, o_ref):
        o_ref[...] = r_ref[...] + alpha * _dot(a_ref[...], w_ref[...])

    return pl.pallas_call(
        body, name=name, grid=(m // tm, n // tn),
        in_specs=[pl.BlockSpec((tm, k), lambda i, j: (i, 0)), pl.BlockSpec((k, tn), lambda i, j: (0, j)),
                  pl.BlockSpec((tm, tn), lambda i, j: (i, j))],
        out_specs=pl.BlockSpec((tm, tn), lambda i, j: (i, j)),
        out_shape=jax.ShapeDtypeStruct((m, n), F32),
        compiler_params=_params("arbitrary", "arbitrary"),
    )(a, w, res)


def mm_nt(x, w, alpha, name):
    m, d = x.shape
    n = w.shape[0]
    tm, tn = _tile(m, 1024, 8), _tile(n, 1408)

    def body(x_ref, w_ref, o_ref, xb_ref):
        @pl.when(pl.program_id(1) == 0)
        def _():
            xb_ref[...] = x_ref[...].astype(BF16)

        o_ref[...] = (alpha * _dot(xb_ref[...], w_ref[...], NT)).astype(o_ref.dtype)

    return pl.pallas_call(
        body, name=name, grid=(m // tm, n // tn),
        in_specs=[pl.BlockSpec((tm, d), lambda i, j: (i, 0)), pl.BlockSpec((tn, d), lambda i, j: (j, 0))],
        out_specs=pl.BlockSpec((tm, tn), lambda i, j: (i, j)),
        out_shape=jax.ShapeDtypeStruct((m, n), BF16),
        scratch_shapes=[pltpu.VMEM((tm, d), BF16)],
        compiler_params=_params("arbitrary", "arbitrary"),
    )(x, w)


def mm_tn(a, b, alpha, name, ta_target=1024, tb_target=512):
    s, ka = a.shape
    nb = b.shape[1]
    ta, tb, ts = _tile(ka, ta_target), _tile(nb, tb_target), _tile(s, 1024, 16)
    steps = s // ts

    def body(a_ref, b_ref, o_ref, acc_ref):
        t = pl.program_id(2)

        @pl.when(t == 0)
        def _():
            acc_ref[...] = jnp.zeros_like(acc_ref)

        acc_ref[...] += _dot(a_ref[...].astype(BF16), b_ref[...].astype(BF16), TN)

        @pl.when(t == steps - 1)
        def _():
            o_ref[...] = alpha * acc_ref[...]

    return pl.pallas_call(
        body, name=name, grid=(ka // ta, nb // tb, steps),
        in_specs=[pl.BlockSpec((ts, ta), lambda i, j, t: (t, i)), pl.BlockSpec((ts, tb), lambda i, j, t: (t, j))],
        out_specs=pl.BlockSpec((ta, tb), lambda i, j, t: (i, j)),
        out_shape=jax.ShapeDtypeStruct((ka, nb), F32),
        scratch_shapes=[pltpu.VMEM((ta, tb), F32)],
        compiler_params=_params("arbitrary", "arbitrary", "arbitrary"),
    )(a, b)


def mm_nt_normbwd(dy, w, x, g, res, name):
    m, n = dy.shape
    d = w.shape[0]
    tm, tk = _tile(m, 1024, 8), _tile(n, 1408)
    steps = n // tk
    has_res = res is not None

    def body(*refs):
        if has_res:
            dy_ref, w_ref, x_ref, g_ref, r_ref, dx_ref, dg_ref, h_ref, acc_ref = refs
        else:
            dy_ref, w_ref, x_ref, g_ref, dx_ref, dg_ref, h_ref, acc_ref = refs
        i, t = pl.program_id(0), pl.program_id(1)

        @pl.when(t == 0)
        def _():
            acc_ref[...] = jnp.zeros_like(acc_ref)

        @pl.when((t == 0) & (i == 0))
        def _():
            dg_ref[...] = jnp.zeros_like(dg_ref)

        acc_ref[...] += _dot(dy_ref[...], w_ref[...], NT)

        @pl.when(t == steps - 1)
        def _():
            xf = x_ref[...]
            r = _rms(xf)
            xhat = xf * r
            dh = acc_ref[...]
            gain = g_ref[...]
            dg_ref[...] += jnp.sum(dh * xhat, axis=0, keepdims=True)
            dxhat = dh * gain
            dx = r * (dxhat - xhat * jnp.mean(dxhat * xhat, axis=-1, keepdims=True))
            dx_ref[...] = (r_ref[...] + dx) if has_res else dx
            h_ref[...] = (xhat * gain).astype(BF16)

    row = lambda i, t: (i, 0)
    in_specs = [pl.BlockSpec((tm, tk), lambda i, t: (i, t)), pl.BlockSpec((d, tk), lambda i, t: (0, t)),
                pl.BlockSpec((tm, d), row), pl.BlockSpec((1, d), lambda i, t: (0, 0))]
    args = [dy, w, x, g]
    if has_res:
        in_specs.append(pl.BlockSpec((tm, d), row))
        args.append(res)
    return pl.pallas_call(
        body, name=name, grid=(m // tm, steps),
        in_specs=in_specs,
        out_specs=[pl.BlockSpec((tm, d), row), pl.BlockSpec((1, d), lambda i, t: (0, 0)), pl.BlockSpec((tm, d), row)],
        out_shape=[jax.ShapeDtypeStruct((m, d), F32), jax.ShapeDtypeStruct((1, d), F32), jax.ShapeDtypeStruct((m, d), BF16)],
        scratch_shapes=[pltpu.VMEM((tm, d), F32)],
        compiler_params=_params("arbitrary", "arbitrary"),
    )(*args)


def _sigmoid(z):
    return 1.0 / (1.0 + jnp.exp(-z))


def _swiglu(gate_b, up_b):
    gate = gate_b.astype(F32)
    return (gate * _sigmoid(gate) * up_b.astype(F32)).astype(BF16)


def swiglu_mm_res(gu, w, res, alpha, name):
    m, f2 = gu.shape
    f, n = w.shape
    tm, tc = _tile(m, 256, 16), _tile(f, 256)

    def body(gu_ref, w_ref, r_ref, o_ref):
        acc = jnp.zeros((tm, n), F32)
        for c0 in range(0, f, tc):
            act = _swiglu(gu_ref[:, c0:c0 + tc], gu_ref[:, f + c0:f + c0 + tc])
            acc = acc + _dot(act, w_ref[c0:c0 + tc, :])
        o_ref[...] = r_ref[...] + alpha * acc

    return pl.pallas_call(
        body, name=name, grid=(m // tm,),
        in_specs=[pl.BlockSpec((tm, f2), lambda i: (i, 0)), pl.BlockSpec((f, n), lambda i: (0, 0)),
                  pl.BlockSpec((tm, n), lambda i: (i, 0))],
        out_specs=pl.BlockSpec((tm, n), lambda i: (i, 0)),
        out_shape=jax.ShapeDtypeStruct((m, n), F32), compiler_params=_params("arbitrary"),
    )(gu, w, res)


def swiglu_mm_tn(gu, b, alpha, name):
    s, f2 = gu.shape
    f, n = f2 // 2, b.shape[1]
    ta, ts = _tile(f, 1408), _tile(s, 512, 16)
    steps, half = s // ts, f // ta

    def body(g_ref, u_ref, b_ref, o_ref, acc_ref):
        t = pl.program_id(1)

        @pl.when(t == 0)
        def _():
            acc_ref[...] = jnp.zeros_like(acc_ref)

        bb = b_ref[...].astype(BF16)
        for c0 in range(0, ta, LANES):
            acc_ref[c0:c0 + LANES, :] += _dot(_swiglu(g_ref[:, c0:c0 + LANES], u_ref[:, c0:c0 + LANES]), bb, TN)

        @pl.when(t == steps - 1)
        def _():
            o_ref[...] = alpha * acc_ref[...]

    return pl.pallas_call(
        body, name=name, grid=(half, steps),
        in_specs=[pl.BlockSpec((ts, ta), lambda i, t: (t, i)), pl.BlockSpec((ts, ta), lambda i, t: (t, half + i)),
                  pl.BlockSpec((ts, n), lambda i, t: (t, 0))],
        out_specs=pl.BlockSpec((ta, n), lambda i, t: (i, 0)),
        out_shape=jax.ShapeDtypeStruct((f, n), F32),
        scratch_shapes=[pltpu.VMEM((ta, n), F32)],
        compiler_params=_params("arbitrary", "arbitrary"),
    )(gu, gu, b)


def mm_nt_swiglu_bwd(x, w, gu, alpha, name):
    m, d = x.shape
    f = w.shape[0]
    tm, tc = _tile(m, 256, 16), _tile(f, 256)

    def body(x_ref, w_ref, gu_ref, o_ref):
        xb = x_ref[...].astype(BF16)
        for c0 in range(0, f, tc):
            d_act = alpha * _dot(xb, w_ref[c0:c0 + tc, :], NT)
            gate, up = gu_ref[:, c0:c0 + tc].astype(F32), gu_ref[:, f + c0:f + c0 + tc].astype(F32)
            sg = _sigmoid(gate)
            o_ref[:, c0:c0 + tc] = (d_act * up * (sg * (1.0 + gate * (1.0 - sg)))).astype(BF16)
            o_ref[:, f + c0:f + c0 + tc] = (d_act * (gate * sg)).astype(BF16)

    return pl.pallas_call(
        body, name=name, grid=(m // tm,),
        in_specs=[pl.BlockSpec((tm, d), lambda i: (i, 0)), pl.BlockSpec((f, d), lambda i: (0, 0)),
                  pl.BlockSpec((tm, 2 * f), lambda i: (i, 0))],
        out_specs=pl.BlockSpec((tm, 2 * f), lambda i: (i, 0)),
        out_shape=jax.ShapeDtypeStruct((m, 2 * f), BF16), compiler_params=_params("arbitrary"),
    )(x, w, gu)


def _gelu(x):
    return 0.5 * x * (1.0 + jnp.tanh(GELU_C * (x + GELU_A * x * x * x)))


def _gelu_grad(x):
    t = jnp.tanh(GELU_C * (x + GELU_A * x * x * x))
    return 0.5 * (1.0 + t) + 0.5 * x * (1.0 - t * t) * (GELU_C * (1.0 + 3.0 * GELU_A * x * x))


def _chunk_mask():
    row = lax.broadcasted_iota(jnp.int32, (GM_P, GM_P), 0)
    col = lax.broadcasted_iota(jnp.int32, (GM_P, GM_P), 1)
    return (col < GM_P // 2) | (row >= GM_P // 2)


def gmlp_fwd(proj, gain, w_s, bias, name):
    s, pw = proj.shape
    tm = _tile(s, 256, GM_P)

    def body(p_ref, gain_ref, w_ref, b_ref, o_ref):
        mask = _chunk_mask()
        u = _gelu(p_ref[:, :GM_W])
        v = _gelu(p_ref[:, GM_W:2 * GM_W])
        vn = ((v * _rms(v)) * gain_ref[...]).astype(BF16)
        for g in range(GM_GROUPS):
            wg = jnp.where(mask, w_ref[g], 0.0).astype(BF16)
            cols = slice(g * GM_P, (g + 1) * GM_P)
            for n in range(tm // GM_P):
                rows = slice(n * GM_P, (n + 1) * GM_P)
                mixed = _dot(wg, vn[rows, cols]) + b_ref[:, cols]
                o_ref[rows, cols] = (u[rows, cols] * mixed).astype(BF16)

    return pl.pallas_call(
        body, name=name, grid=(s // tm,),
        in_specs=[pl.BlockSpec((tm, pw), lambda i: (i, 0)), pl.BlockSpec((1, GM_W), lambda i: (0, 0)),
                  pl.BlockSpec((GM_GROUPS, GM_P, GM_P), lambda i: (0, 0, 0)), pl.BlockSpec((GM_P, GM_W), lambda i: (0, 0))],
        out_specs=pl.BlockSpec((tm, GM_W), lambda i: (i, 0)),
        out_shape=jax.ShapeDtypeStruct((s, GM_W), BF16), compiler_params=_params("arbitrary"),
    )(proj, gain, w_s, bias)


def gmlp_bwd(proj, dy, gain, w_s, bias, name):
    s, pw = proj.shape
    dw_total = dy.shape[1]
    tm = _tile(s, 256, GM_P)

    def body(p_ref, dy_ref, gain_ref, w_ref, b_ref, dp_ref, dw_ref, db_ref, dgain_ref, dvn_ref):
        @pl.when(pl.program_id(0) == 0)
        def _():
            dw_ref[...] = jnp.zeros_like(dw_ref)
            db_ref[...] = jnp.zeros_like(db_ref)
            dgain_ref[...] = jnp.zeros_like(dgain_ref)

        mask = _chunk_mask()
        pu = p_ref[:, :GM_W]
        pv = p_ref[:, GM_W:2 * GM_W]
        u = _gelu(pu)
        v = _gelu(pv)
        r = _rms(v)
        vhat = v * r
        gain = gain_ref[...]
        vn = (vhat * gain).astype(BF16)
        gu_grad = _gelu_grad(pu)
        for g in range(GM_GROUPS):
            wg = jnp.where(mask, w_ref[g], 0.0).astype(BF16)
            cols = slice(g * GM_P, (g + 1) * GM_P)
            dw_acc = jnp.zeros((GM_P, GM_P), F32)
            db_acc = jnp.zeros((GM_P, 1), F32)
            for n in range(tm // GM_P):
                rows = slice(n * GM_P, (n + 1) * GM_P)
                dyb = dy_ref[rows, cols].astype(F32)
                vnb = vn[rows, cols]
                mixed = _dot(wg, vnb) + b_ref[:, cols]
                dmixed = dyb * u[rows, cols]
                dmb = dmixed.astype(BF16)
                dp_ref[rows, cols] = (dyb * mixed * gu_grad[rows, cols]).astype(BF16)
                dw_acc = dw_acc + _dot(dmb, vnb, NT)
                db_acc = db_acc + jnp.sum(dmixed, axis=1, keepdims=True)
                dvn_ref[rows, cols] = _dot(wg, dmb, TN)
            dw_ref[g] += jnp.where(mask, dw_acc, 0.0)
            db_ref[g] += jnp.broadcast_to(db_acc, (GM_P, GM_P))
        dvn = dvn_ref[...]
        dgain_ref[...] += jnp.sum(dvn * vhat, axis=0, keepdims=True)
        dvhat = dvn * gain
        dv = r * (dvhat - vhat * jnp.mean(dvhat * vhat, axis=-1, keepdims=True))
        dp_ref[:, GM_W:] = (dv * _gelu_grad(pv)).astype(BF16)

    const3 = lambda i: (0, 0, 0)
    return pl.pallas_call(
        body, name=name, grid=(s // tm,),
        in_specs=[pl.BlockSpec((tm, pw), lambda i: (i, 0)), pl.BlockSpec((tm, dw_total), lambda i: (i, 0)),
                  pl.BlockSpec((1, GM_W), lambda i: (0, 0)), pl.BlockSpec((GM_GROUPS, GM_P, GM_P), const3),
                  pl.BlockSpec((GM_P, GM_W), lambda i: (0, 0))],
        out_specs=[pl.BlockSpec((tm, 2 * GM_W), lambda i: (i, 0)), pl.BlockSpec((GM_GROUPS, GM_P, GM_P), const3),
                   pl.BlockSpec((GM_GROUPS, GM_P, GM_P), const3), pl.BlockSpec((1, GM_W), lambda i: (0, 0))],
        out_shape=[jax.ShapeDtypeStruct((s, 2 * GM_W), BF16), jax.ShapeDtypeStruct((GM_GROUPS, GM_P, GM_P), F32),
                   jax.ShapeDtypeStruct((GM_GROUPS, GM_P, GM_P), F32), jax.ShapeDtypeStruct((1, GM_W), F32)],
        scratch_shapes=[pltpu.VMEM((tm, GM_W), F32)],
        compiler_params=_params("arbitrary"),
    )(proj, dy, gain, w_s, bias)


def _keep(mask, xb):
    return jnp.where(mask, xb.astype(F32), 0.0).astype(BF16)


def _head_masks(rows, width, heads):
    lane = lax.broadcasted_iota(jnp.int32, (rows, width), 1)
    return [(lane >= HEAD_DIM * h) & (lane < HEAD_DIM * (h + 1)) for h in range(heads)]


def _mem_probs(qh, k):
    sc = _dot(qh, k, NT) * QK_SCALE
    e = jnp.exp(sc - jnp.max(sc, axis=-1, keepdims=True))
    return e / jnp.sum(e, axis=-1, keepdims=True)


def mem_fwd(proj, q_blk, mem_kv, layer, name):
    s = proj.shape[0]
    n_mem = mem_kv.shape[0]
    tm = _tile(s, 512, 16)

    def body(q_ref, k_ref, v_ref, o_ref):
        q = q_ref[...].astype(BF16)
        k, v = k_ref[...], v_ref[...]
        out = jnp.zeros((tm, MEM_W), F32)
        for hm in _head_masks(tm, MEM_W, MEM_HEADS):
            p = _mem_probs(_keep(hm, q), k)
            out = out + jnp.where(hm, _dot(p.astype(BF16), v), 0.0)
        o_ref[...] = out.astype(BF16)

    return pl.pallas_call(
        body, name=name, grid=(s // tm,),
        in_specs=[pl.BlockSpec((tm, MEM_W), lambda i: (i, q_blk)), pl.BlockSpec((n_mem, MEM_W), lambda i: (0, 2 * layer)),
                  pl.BlockSpec((n_mem, MEM_W), lambda i: (0, 2 * layer + 1))],
        out_specs=pl.BlockSpec((tm, MEM_W), lambda i: (i, 0)),
        out_shape=jax.ShapeDtypeStruct((s, MEM_W), BF16), compiler_params=_params("arbitrary"),
    )(proj, mem_kv, mem_kv)


def mem_bwd(proj, q_blk, mem_kv, layer, dy, dy_blk, name):
    s = proj.shape[0]
    n_mem = mem_kv.shape[0]
    tm = _tile(s, 512, 16)

    def body(q_ref, k_ref, v_ref, dy_ref, dq_ref, dk_ref, dv_ref):
        @pl.when(pl.program_id(0) == 0)
        def _():
            dk_ref[...] = jnp.zeros_like(dk_ref)
            dv_ref[...] = jnp.zeros_like(dv_ref)

        q = q_ref[...].astype(BF16)
        k, v = k_ref[...], v_ref[...]
        dy = dy_ref[...]
        dq = jnp.zeros((tm, MEM_W), F32)
        dk = jnp.zeros((n_mem, MEM_W), F32)
        dv = jnp.zeros((n_mem, MEM_W), F32)
        for hm in _head_masks(tm, MEM_W, MEM_HEADS):
            qh = _keep(hm, q)
            dyh = _keep(hm, dy)
            p = _mem_probs(qh, k)
            dp = _dot(dyh, v, NT)
            dv = dv + _dot(p.astype(BF16), dyh, TN)
            ds = (p * (dp - jnp.sum(dp * p, axis=-1, keepdims=True)) * QK_SCALE).astype(BF16)
            dq = dq + jnp.where(hm, _dot(ds, k), 0.0)
            dk = dk + _dot(ds, qh, TN)
        dq_ref[...] = dq.astype(BF16)
        dk_ref[...] += dk
        dv_ref[...] += dv

    const = lambda i: (0, 0)
    return pl.pallas_call(
        body, name=name, grid=(s // tm,),
        in_specs=[pl.BlockSpec((tm, MEM_W), lambda i: (i, q_blk)), pl.BlockSpec((n_mem, MEM_W), lambda i: (0, 2 * layer)),
                  pl.BlockSpec((n_mem, MEM_W), lambda i: (0, 2 * layer + 1)), pl.BlockSpec((tm, MEM_W), lambda i: (i, dy_blk))],
        out_specs=[pl.BlockSpec((tm, MEM_W), lambda i: (i, 0)), pl.BlockSpec((n_mem, MEM_W), const),
                   pl.BlockSpec((n_mem, MEM_W), const)],
        out_shape=[jax.ShapeDtypeStruct((s, MEM_W), BF16), jax.ShapeDtypeStruct((n_mem, MEM_W), F32),
                   jax.ShapeDtypeStruct((n_mem, MEM_W), F32)],
        compiler_params=_params("arbitrary"),
    )(proj, mem_kv, mem_kv, dy)


SB_KEYS = 512
SB_SUB = SB_KEYS // SB_BLK
SB_QROWS = 512
SB_QB = SB_QROWS // SB_BLK
SB_CHAINS = 2 * SB_QB
SB_DEAD = -110.0


def _split(xf):
    hi = xf.astype(BF16)
    return hi, (xf - hi.astype(F32)).astype(BF16)


def _sb_consts():
    row = lax.bitwise_and(lax.broadcasted_iota(jnp.int32, (2 * SB_BLK, 2 * SB_BLK), 0), SB_BLK - 1)
    col = lax.broadcasted_iota(jnp.int32, (2 * SB_BLK, 2 * SB_BLK), 1)
    ones = col >= SB_BLK
    after2 = jnp.where(ones | (row > col), -1.0, 0.0).astype(BF16)
    from2 = jnp.where(ones | (row >= col), 1.0, 0.0).astype(BF16)
    r = lax.broadcasted_iota(jnp.int32, (SB_BLK, SB_BLK), 0)
    c = lax.broadcasted_iota(jnp.int32, (SB_BLK, SB_BLK), 1)
    return after2, from2, c - r, [c < HEAD_DIM, c >= HEAD_DIM]


def _suffix(xf, tri2):
    hi, lo = _split(xf)
    return _dot(jnp.concatenate([hi, lo], axis=1), tri2)


def _sb_logs(z, mask):
    softplus = jnp.maximum(z, 0.0) + jnp.log(1.0 + jnp.exp(-jnp.abs(z)))
    log_beta = z - softplus
    if mask is not None:
        softplus = jnp.where(mask, softplus, 0.0)
    return softplus, log_beta


def _sb_queries(q_ref, heads):
    q = q_ref[...].astype(F32) * QK_SCALE
    return [jnp.where(hm, q[r * SB_BLK:(r + 1) * SB_BLK], 0.0).astype(BF16) for r in range(SB_QB) for hm in heads]


def _sb_walk(i, block, state):
    places = SB_SUB // SB_QB
    assert places in (1, 2)
    own = lax.shift_right_logical(i * SB_QB, SB_SUB.bit_length() - 1)
    firsts = [[(v * SB_QB + r) * SB_BLK for r in range(SB_QB) for _ in range(2)] for v in range(places)]
    if places == 1:
        state = block(own, state, firsts[0])
    else:
        state = lax.cond(lax.bitwise_and(i, 1) == 0, lambda st: block(own, st, firsts[0]),
                         lambda st: block(own, st, firsts[1]), state)

    def live(carry):
        j, st = carry
        most = st[0][0]
        for run in st[0][1:]:
            most = jnp.maximum(most, run)
        return (j >= 0) & (jnp.max(most) > SB_DEAD)

    return lax.while_loop(live, lambda carry: (carry[0] - 1, block(carry[0], carry[1], None)), (own - 1, state))[1]


def _sb_tiles(first):
    out = []
    for c in reversed(range(SB_SUB)):
        for n in range(SB_CHAINS):
            if first is None or c * SB_BLK < first[n]:
                out.append((c, n, "before"))
            elif c * SB_BLK == first[n]:
                out.append((c, n, "diagonal"))
    return out


def _sb_heads_apart(stacked, heads, r):
    return jnp.where(heads[0], stacked[2 * r * SB_BLK:(2 * r + 1) * SB_BLK],
                     stacked[(2 * r + 1) * SB_BLK:(2 * r + 2) * SB_BLK])


def sb_fwd(proj, kv, name):
    s = proj.shape[0]
    assert s % SB_KEYS == 0 and SB_KEYS % SB_QROWS == 0

    def body(q_ref, k_ref, v_ref, o_ref):
        after2, _, col_minus_row, heads = _sb_consts()
        q_all = jnp.concatenate(_sb_queries(q_ref, heads), axis=0)
        key_before_query = col_minus_row < 0

        def block(j, state, first):
            runs, acc = list(state[0]), state[1]
            rows = pl.ds(pl.multiple_of(j * SB_KEYS, SB_KEYS), SB_KEYS)
            kb, vb = k_ref[rows, :], v_ref[rows, :]
            z = _dot(q_all, kb, NT)
            pend = {}
            parts = [[jnp.zeros((SB_BLK, SB_BLK), BF16)] * SB_SUB for _ in range(SB_CHAINS)]
            for c, n, where in _sb_tiles(first):
                mask = key_before_query if where == "diagonal" else None
                softplus, lb = _sb_logs(z[n * SB_BLK:(n + 1) * SB_BLK, c * SB_BLK:(c + 1) * SB_BLK], mask)
                pend[c, n] = (lb, _suffix(softplus, after2), mask)
            for c, n, _ in _sb_tiles(first):
                lb, r, mask = pend.pop((c, n))
                a = jnp.exp(lb + r[:, :SB_BLK] + runs[n])
                if mask is not None:
                    a = jnp.where(mask, a, 0.0)
                parts[n][c] = a.astype(BF16)
                runs[n] = runs[n] + r[:, SB_BLK:]
            a_all = jnp.concatenate([jnp.concatenate(p, axis=1) for p in parts], axis=0)
            return tuple(runs), acc + _dot(a_all, vb)

        zero = jnp.zeros((SB_BLK, LANES), F32)
        state = _sb_walk(pl.program_id(1), block, ((zero,) * SB_CHAINS, jnp.zeros((SB_CHAINS * SB_BLK, LANES), F32)))
        for r in range(SB_QB):
            o_ref[r * SB_BLK:(r + 1) * SB_BLK, :] = _sb_heads_apart(state[1], heads, r)

    pairs = SB_W // LANES
    return pl.pallas_call(
        body, name=name, grid=(pairs, s // SB_QROWS),
        in_specs=[pl.BlockSpec((SB_QROWS, LANES), lambda p, i: (i, p)), pl.BlockSpec((s, LANES), lambda p, i: (0, p)),
                  pl.BlockSpec((s, LANES), lambda p, i: (0, pairs + p))],
        out_specs=pl.BlockSpec((SB_QROWS, LANES), lambda p, i: (i, p)),
        out_shape=jax.ShapeDtypeStruct((s, SB_W), F32),
        compiler_params=_params("arbitrary", "arbitrary"),
    )(proj, kv, kv)


def sb_bwd(proj, kv, out, dy, name):
    s = proj.shape[0]

    def body(q_ref, k_ref, v_ref, o_ref, do_ref, dq_ref, dk_ref, dv_ref):
        i = pl.program_id(1)

        @pl.when(i == 0)
        def _():
            dk_ref[...] = jnp.zeros_like(dk_ref)
            dv_ref[...] = jnp.zeros_like(dv_ref)

        after2, from2, col_minus_row, heads = _sb_consts()
        q_all = jnp.concatenate(_sb_queries(q_ref, heads), axis=0)
        key_before_query = col_minus_row < 0
        d_out = do_ref[...].astype(F32)
        prod = d_out * o_ref[...]
        dos, totals = [], []
        for r in range(SB_QB):
            rr = slice(r * SB_BLK, (r + 1) * SB_BLK)
            for hm in heads:
                dos.append(jnp.where(hm, d_out[rr], 0.0).astype(BF16))
                totals.append(jnp.broadcast_to(jnp.sum(jnp.where(hm, prod[rr], 0.0), axis=1, keepdims=True),
                                               (SB_BLK, SB_BLK)))
        do_all = jnp.concatenate(dos, axis=0)

        def block(j, state, first):
            runs, seens, dq = list(state[0]), list(state[1]), state[2]
            rows = pl.ds(pl.multiple_of(j * SB_KEYS, SB_KEYS), SB_KEYS)
            kb, vb = k_ref[rows, :], v_ref[rows, :]
            z = _dot(q_all, kb, NT)
            da = _dot(do_all, vb, NT)
            pend, pend2 = {}, {}
            a_parts = [[jnp.zeros((SB_BLK, SB_BLK), BF16)] * SB_SUB for _ in range(SB_CHAINS)]
            dz_parts = [[jnp.zeros((SB_BLK, SB_BLK), BF16)] * SB_SUB for _ in range(SB_CHAINS)]
            for c, n, where in _sb_tiles(first):
                mask = key_before_query if where == "diagonal" else None
                softplus, lb = _sb_logs(z[n * SB_BLK:(n + 1) * SB_BLK, c * SB_BLK:(c + 1) * SB_BLK], mask)
                pend[c, n] = (softplus, lb, _suffix(softplus, after2), mask)
            for c, n, _ in _sb_tiles(first):
                softplus, lb, r, mask = pend.pop((c, n))
                a = jnp.exp(lb + r[:, :SB_BLK] + runs[n])
                if mask is not None:
                    a = jnp.where(mask, a, 0.0)
                runs[n] = runs[n] + r[:, SB_BLK:]
                ab = a.astype(BF16)
                a_parts[n][c] = ab
                dl = ab.astype(F32) * da[n * SB_BLK:(n + 1) * SB_BLK, c * SB_BLK:(c + 1) * SB_BLK]
                pend2[c, n] = (softplus, lb, dl, _suffix(dl, from2), mask)
            for c, n, _ in _sb_tiles(first):
                softplus, lb, dl, r2, mask = pend2.pop((c, n))
                d_lom = totals[n] - (r2[:, :SB_BLK] + seens[n])
                if mask is not None:
                    d_lom = jnp.where(mask, d_lom, 0.0)
                seens[n] = seens[n] + r2[:, SB_BLK:]
                dz_parts[n][c] = (dl * jnp.exp(-softplus) - d_lom * jnp.exp(lb)).astype(BF16)
            a_all = jnp.concatenate([jnp.concatenate(p, axis=1) for p in a_parts], axis=0)
            dz_all = jnp.concatenate([jnp.concatenate(p, axis=1) for p in dz_parts], axis=0)
            dv_ref[rows, :] += _dot(a_all, do_all, TN)
            dk_ref[rows, :] += _dot(dz_all, q_all, TN)
            return tuple(runs), tuple(seens), dq + _dot(dz_all, kb)

        zero = jnp.zeros((SB_BLK, LANES), F32)
        state = _sb_walk(i, block, ((zero,) * SB_CHAINS, (zero,) * SB_CHAINS,
                                    jnp.zeros((SB_CHAINS * SB_BLK, LANES), F32)))
        for r in range(SB_QB):
            dq_ref[r * SB_BLK:(r + 1) * SB_BLK, :] = (_sb_heads_apart(state[2], heads, r) * QK_SCALE).astype(BF16)

    pairs = SB_W // LANES
    blk = lambda p, i: (i, p)
    col = lambda p, i: (0, p)
    return pl.pallas_call(
        body, name=name, grid=(pairs, s // SB_QROWS),
        in_specs=[pl.BlockSpec((SB_QROWS, LANES), blk), pl.BlockSpec((s, LANES), col),
                  pl.BlockSpec((s, LANES), lambda p, i: (0, pairs + p)), pl.BlockSpec((SB_QROWS, LANES), blk),
                  pl.BlockSpec((SB_QROWS, LANES), blk)],
        out_specs=[pl.BlockSpec((SB_QROWS, LANES), blk), pl.BlockSpec((s, LANES), col), pl.BlockSpec((s, LANES), col)],
        out_shape=[jax.ShapeDtypeStruct((s, SB_W), BF16), jax.ShapeDtypeStruct((s, SB_W), F32),
                   jax.ShapeDtypeStruct((s, SB_W), F32)],
        compiler_params=_params("arbitrary", "arbitrary"),
    )(proj, kv, kv, out, dy)


def final_loss(x, g, target, name):
    s, d = x.shape
    tm = _tile(s, 256, 8)

    def body(x_ref, g_ref, t_ref, loss_ref, dx_ref, dg_ref):
        @pl.when(pl.program_id(0) == 0)
        def _():
            loss_ref[...] = jnp.zeros_like(loss_ref)
            dg_ref[...] = jnp.zeros_like(dg_ref)

        xf = x_ref[...]
        r = _rms(xf)
        xhat = xf * r
        gain = g_ref[...]
        diff = xhat * gain - t_ref[...]
        sq = jnp.sum(jnp.sum(diff * diff, axis=1, keepdims=True), axis=0, keepdims=True)
        loss_ref[...] += jnp.broadcast_to(sq, loss_ref.shape)
        dy = diff * (1.0 / d)
        dg_ref[...] += jnp.sum(dy * xhat, axis=0, keepdims=True)
        dxhat = dy * gain
        dx_ref[...] = r * (dxhat - xhat * jnp.mean(dxhat * xhat, axis=-1, keepdims=True))

    row = lambda i: (i, 0)
    const = lambda i: (0, 0)
    return pl.pallas_call(
        body, name=name, grid=(s // tm,),
        in_specs=[pl.BlockSpec((tm, d), row), pl.BlockSpec((1, d), const), pl.BlockSpec((tm, d), row)],
        out_specs=[pl.BlockSpec((8, LANES), const), pl.BlockSpec((tm, d), row), pl.BlockSpec((1, d), const)],
        out_shape=[jax.ShapeDtypeStruct((8, LANES), F32), jax.ShapeDtypeStruct((s, d), F32), jax.ShapeDtypeStruct((1, d), F32)],
        compiler_params=_params("arbitrary"),
    )(x, g, target)


def adamw(w, parts, m, v, name):
    rows, cols = w.shape
    k = parts.shape[0]
    tr = _tile(rows, 512, 16)
    c1, c2 = 1.0 - ADAM_B1 ** ADAM_STEP, 1.0 - ADAM_B2 ** ADAM_STEP

    def body(w_ref, p_ref, m_ref, v_ref, g_ref, d_ref, nm_ref, nv_ref):
        grad = p_ref[0].astype(F32)
        for s in range(1, k):
            grad = grad + p_ref[s].astype(F32)
        nm = ADAM_B1 * m_ref[...] + (1.0 - ADAM_B1) * grad
        nv = ADAM_B2 * v_ref[...] + (1.0 - ADAM_B2) * (grad * grad)
        g_ref[...] = grad
        d_ref[...] = -ADAM_LR * ((nm / c1) / (jnp.sqrt(nv / c2) + ADAM_EPS) + ADAM_WD * w_ref[...])
        nm_ref[...] = nm
        nv_ref[...] = nv

    spec = pl.BlockSpec((tr, cols), lambda i: (i, 0))
    shape = jax.ShapeDtypeStruct((rows, cols), F32)
    return pl.pallas_call(
        body, name=name, grid=(rows // tr,),
        in_specs=[spec, pl.BlockSpec((k, tr, cols), lambda i: (0, i, 0)), spec, spec],
        out_specs=[spec] * 4, out_shape=[shape] * 4,
        compiler_params=_params("arbitrary"),
    )(w, parts, m, v)


SHARDED = {"ffn1_w_gate": 2, "ffn1_w_up": 2, "ffn1_w_down": 1, "ffn2_w_gate": 2, "ffn2_w_up": 2, "ffn2_w_down": 1,
           "w_mem_kv": 1, "a_w_in": 2, "a_w_out": 1, "w_kv": 1, "b_w_in": 1, "b_w_out": 1}
SMALL = ["ffn1_norm", "mix_norm", "ffn2_norm", "mem_norm", "kv_norm", "final_norm", "a_v_norm", "a_w_spatial", "a_b_spatial"]
WEIGHTS = ["ffn1_norm", "ffn1_w_gate", "ffn1_w_up", "ffn1_w_down", "mix_norm", "ffn2_norm", "ffn2_w_gate", "ffn2_w_up",
           "ffn2_w_down", "mem_norm", "w_mem_kv", "a_w_in", "a_v_norm", "a_w_spatial", "a_b_spatial", "a_w_out", "kv_norm",
           "w_kv", "b_w_in", "b_w_out", "final_norm"]


def _gather_weights(shards):
    by_dev = exchange([shards[n].astype(BF16) for n in SHARDED], "all", True, "gather_weights")
    landed = dict(zip(SHARDED, by_dev))
    out = {}
    for n, blocks in landed.items():
        if n.endswith("_w_up"):
            continue
        pieces = [blocks[d] for d in range(N_DEV)]
        if n.endswith("_w_gate"):
            pieces += [landed[n.replace("_w_gate", "_w_up")][d] for d in range(N_DEV)]
        out[n] = jnp.concatenate(pieces, axis=SHARDED[n])
    return out


def _scatter_grads(grads):
    by_dev = [jnp.stack(jnp.split(grads[n].astype(BF16), N_DEV, axis=axis)) for n, axis in SHARDED.items()]
    return dict(zip(SHARDED, exchange(by_dev, "all", False, "scatter_grads")))


def _all_sum(parts, name):
    flat = jnp.concatenate([p.reshape(-1) for p in parts])
    pad = (-flat.size) % (16 * LANES)
    buf = jnp.pad(flat, (0, pad)).reshape(-1, LANES)
    total = sum_leading(exchange([buf], "all", True, name)[0], F32, name + "_sum").reshape(-1)
    out, off = [], 0
    for p in parts:
        out.append(total[off:off + p.size].reshape(p.shape))
        off += p.size
    return out


def _device_index():
    return 4 * lax.axis_index("x") + 2 * lax.axis_index("y") + lax.axis_index("c")


def kernel(x, mem, ffn1_norm, ffn1_w_gate, ffn1_w_up, ffn1_w_down, mix_norm, ffn2_norm, ffn2_w_gate, ffn2_w_up, ffn2_w_down, mem_norm, w_mem_kv, a_w_in, a_v_norm, a_w_spatial, a_b_spatial, a_w_out, kv_norm, w_kv, b_w_in, b_w_out, final_norm, loss_target, m_ffn1_norm, m_ffn1_w_gate, m_ffn1_w_up, m_ffn1_w_down, m_mix_norm, m_ffn2_norm, m_ffn2_w_gate, m_ffn2_w_up, m_ffn2_w_down, m_mem_norm, m_w_mem_kv, m_a_w_in, m_a_v_norm, m_a_w_spatial, m_a_b_spatial, m_a_w_out, m_kv_norm, m_w_kv, m_b_w_in, m_b_w_out, m_final_norm, v_ffn1_norm, v_ffn1_w_gate, v_ffn1_w_up, v_ffn1_w_down, v_mix_norm, v_ffn2_norm, v_ffn2_w_gate, v_ffn2_w_up, v_ffn2_w_down, v_mem_norm, v_w_mem_kv, v_a_w_in, v_a_v_norm, v_a_w_spatial, v_a_b_spatial, v_a_w_out, v_kv_norm, v_w_kv, v_b_w_in, v_b_w_out, v_final_norm):
    weights = dict(ffn1_norm=ffn1_norm, ffn1_w_gate=ffn1_w_gate, ffn1_w_up=ffn1_w_up, ffn1_w_down=ffn1_w_down, mix_norm=mix_norm, ffn2_norm=ffn2_norm, ffn2_w_gate=ffn2_w_gate, ffn2_w_up=ffn2_w_up, ffn2_w_down=ffn2_w_down, mem_norm=mem_norm, w_mem_kv=w_mem_kv, a_w_in=a_w_in, a_v_norm=a_v_norm, a_w_spatial=a_w_spatial, a_b_spatial=a_b_spatial, a_w_out=a_w_out, kv_norm=kv_norm, w_kv=w_kv, b_w_in=b_w_in, b_w_out=b_w_out, final_norm=final_norm)
    mom1 = dict(ffn1_norm=m_ffn1_norm, ffn1_w_gate=m_ffn1_w_gate, ffn1_w_up=m_ffn1_w_up, ffn1_w_down=m_ffn1_w_down, mix_norm=m_mix_norm, ffn2_norm=m_ffn2_norm, ffn2_w_gate=m_ffn2_w_gate, ffn2_w_up=m_ffn2_w_up, ffn2_w_down=m_ffn2_w_down, mem_norm=m_mem_norm, w_mem_kv=m_w_mem_kv, a_w_in=m_a_w_in, a_v_norm=m_a_v_norm, a_w_spatial=m_a_w_spatial, a_b_spatial=m_a_b_spatial, a_w_out=m_a_w_out, kv_norm=m_kv_norm, w_kv=m_w_kv, b_w_in=m_b_w_in, b_w_out=m_b_w_out, final_norm=m_final_norm)
    mom2 = dict(ffn1_norm=v_ffn1_norm, ffn1_w_gate=v_ffn1_w_gate, ffn1_w_up=v_ffn1_w_up, ffn1_w_down=v_ffn1_w_down, mix_norm=v_mix_norm, ffn2_norm=v_ffn2_norm, ffn2_w_gate=v_ffn2_w_gate, ffn2_w_up=v_ffn2_w_up, ffn2_w_down=v_ffn2_w_down, mem_norm=v_mem_norm, w_mem_kv=v_w_mem_kv, a_w_in=v_a_w_in, a_v_norm=v_a_v_norm, a_w_spatial=v_a_w_spatial, a_b_spatial=v_a_b_spatial, a_w_out=v_a_w_out, kv_norm=v_kv_norm, w_kv=v_w_kv, b_w_in=v_b_w_in, b_w_out=v_b_w_out, final_norm=v_final_norm)

    dev = _device_index()
    xs, mem_in, target = x[0], mem[0], loss_target[0]
    d_model = xs.shape[1]
    shards = {n: weights[n] for n in SHARDED}
    wholes = _gather_weights(shards)
    vn_width = a_v_norm.shape[1]
    a_v_full = _all_sum([lax.dynamic_update_slice(jnp.zeros((N_A, N_DEV * vn_width), F32), a_v_norm, (0, dev * vn_width))],
                        "gather_v_norm")[0]

    def whole(n, l=None):
        return wholes[n] if l is None else wholes[n][l]

    def whole_gu(f, l):
        return whole(f + "_w_gate", l)

    row = lambda v: v.reshape(1, -1)
    w_mem_cat = wholes["w_mem_kv"].transpose(1, 0, 2).reshape(d_model, -1)
    bias = [jnp.repeat(a_b_spatial[i].T, GM_P, axis=1) for i in range(N_A)]

    mem_kv, mem_h = norm_mm(mem_in, row(mem_norm), w_mem_cat, BF16, "mem_kv", emit_h=True)

    def ffn_fwd(xin, f, l):
        gu = norm_mm(xin, row(weights[f + "_norm"][l]), whole_gu(f, l), BF16, "ffn_gu")
        return swiglu_mm_res(gu, whole(f + "_w_down", l), xin, 0.5, "ffn_down"), gu

    saved = []
    kv = x_kv = None
    cur = xs
    for l in range(DEPTH):
        st = {"x0": cur}
        if l == N_A:
            x_kv = cur
            kv = norm_mm(cur, row(kv_norm), whole("w_kv"), BF16, "kv_proj")
        st["x1"], st["gu1"] = ffn_fwd(cur, "ffn1", l)
        if l < N_A:
            proj = norm_mm(st["x1"], row(mix_norm[l]), whole("a_w_in", l), F32, "a_proj")
            y_tok = gmlp_fwd(proj, row(a_v_full[l]), a_w_spatial[l], bias[l], "gmlp_fwd")
            y_mem = mem_fwd(proj, 2 * GM_W // MEM_W, mem_kv, l, "mem_fwd_a")
            w_out = whole("a_w_out", l)
        else:
            proj = norm_mm(st["x1"], row(mix_norm[l]), whole("b_w_in", l - N_A), BF16, "b_proj")
            st["sb_out"] = sb_fwd(proj, kv, "sb_fwd")
            y_tok = st["sb_out"].astype(BF16)
            y_mem = mem_fwd(proj, SB_W // MEM_W, mem_kv, l, "mem_fwd_b")
            w_out = whole("b_w_out", l - N_A)
        st["proj"] = proj
        st["y"] = jnp.concatenate([y_tok, y_mem], axis=1)
        st["x2"] = mm_res(st["y"], w_out, st["x1"], 1.0, "mix_out")
        cur, st["gu2"] = ffn_fwd(st["x2"], "ffn2", l)
        saved.append(st)

    loss_blk, dx, d_final = final_loss(cur, row(final_norm), target, "final_loss")
    loss = lax.psum(loss_blk[0, 0] * (0.5 / d_model), AXES)

    grads = {n: [None] * weights[n].shape[0] for n in WEIGHTS if weights[n].ndim >= 2 and n not in ("w_kv",)}
    grads["final_norm"] = d_final.reshape(-1)
    d_mem_kv = [None] * DEPTH
    d_kv = []

    def ffn_bwd(dx, xin, gu, f, l):
        d_gu = mm_nt_swiglu_bwd(dx, whole(f + "_w_down", l), gu, 0.5, "ffn_dgu")
        dx_new, d_gain, h = mm_nt_normbwd(d_gu, whole_gu(f, l), xin, row(weights[f + "_norm"][l]), dx, "ffn_dx")
        d_wgu = mm_tn(h, d_gu, 1.0, "ffn_dwgu", tb_target=1408)
        half = d_wgu.shape[1] // 2
        grads[f + "_w_gate"][l], grads[f + "_w_up"][l] = d_wgu[:, :half], d_wgu[:, half:]
        grads[f + "_w_down"][l] = swiglu_mm_tn(gu, dx, 0.5, "ffn_dwdown")
        grads[f + "_norm"][l] = d_gain.reshape(-1)
        return dx_new

    for l in reversed(range(DEPTH)):
        st = saved[l]
        dx = ffn_bwd(dx, st["x2"], st["gu2"], "ffn2", l)
        proj = st["proj"]
        key_in, key_out, idx = ("a_w_in", "a_w_out", l) if l < N_A else ("b_w_in", "b_w_out", l - N_A)
        w_in, w_out = whole(key_in, idx), whole(key_out, idx)
        dy = mm_nt(dx, w_out, 1.0, "mix_dy")
        grads[key_out][idx] = mm_tn(st["y"], dx, 1.0, "mix_dwout", tb_target=1024)
        if l < N_A:
            d_uv, d_ws, d_bs, d_vgain = gmlp_bwd(proj, dy, row(a_v_full[l]), a_w_spatial[l], bias[l], "gmlp_bwd")
            grads["a_w_spatial"][l], grads["a_b_spatial"][l], grads["a_v_norm"][l] = d_ws, d_bs[:, :, 0], d_vgain.reshape(-1)
            d_q, d_k, d_v = mem_bwd(proj, 2 * GM_W // MEM_W, mem_kv, l, dy, GM_W // MEM_W, "mem_bwd_a")
            d_proj = jnp.concatenate([d_uv, d_q], axis=1)
        else:
            d_qsb, d_ksb, d_vsb = sb_bwd(proj, kv, st["sb_out"], dy, "sb_bwd")
            d_kv.append(jnp.concatenate([d_ksb, d_vsb], axis=1))
            d_q, d_k, d_v = mem_bwd(proj, SB_W // MEM_W, mem_kv, l, dy, SB_W // MEM_W, "mem_bwd_b")
            d_proj = jnp.concatenate([d_qsb, d_q], axis=1)
        d_mem_kv[l] = jnp.concatenate([d_k, d_v], axis=1)
        dx, d_gain, h = mm_nt_normbwd(d_proj, w_in, st["x1"], row(mix_norm[l]), dx, "mix_dx")
        grads["mix_norm"][l] = d_gain.reshape(-1)
        grads[key_in][idx] = mm_tn(h, d_proj, 1.0, "mix_dwin")
        dx = ffn_bwd(dx, st["x0"], st["gu1"], "ffn1", l)
        if l == N_A:
            d_kv_b = sum_leading(jnp.stack(d_kv), BF16, "kv_dsum")
            dx, d_gain, h = mm_nt_normbwd(d_kv_b, whole("w_kv"), x_kv, row(kv_norm), dx, "kv_dx")
            grads["kv_norm"] = d_gain.reshape(-1)
            grads["w_kv"] = mm_tn(h, d_kv_b, 1.0, "kv_dw")

    d_mem_all = jnp.concatenate(d_mem_kv, axis=1).astype(BF16)
    _, d_gain, _ = mm_nt_normbwd(d_mem_all, w_mem_cat, mem_in, row(mem_norm), None, "mem_dnorm")
    grads["mem_norm"] = d_gain.reshape(-1)
    d_wmem = mm_tn(mem_h, d_mem_all, 1.0, "mem_dw")
    grads["w_mem_kv"] = d_wmem.reshape(d_model, DEPTH, -1).transpose(1, 0, 2)

    grads = {n: (jnp.stack(g) if isinstance(g, list) else g) for n, g in grads.items()}
    parts = _scatter_grads({n: grads[n] for n in SHARDED})
    for n, g in zip(SMALL, _all_sum([grads[n] for n in SMALL], "sum_small")):
        parts[n] = g[None]
    parts["a_v_norm"] = lax.dynamic_slice(parts["a_v_norm"], (0, 0, dev * vn_width), (1,) + a_v_norm.shape)

    reduced, deltas, new_m, new_v = {}, {}, {}, {}
    for n in WEIGHTS:
        w = weights[n]
        view = (lambda a: a.reshape(-1, a.shape[-1]))
        res = adamw(view(w), parts[n].reshape(parts[n].shape[0], -1, w.shape[-1]), view(mom1[n]), view(mom2[n]), "adamw")
        reduced[n], deltas[n], new_m[n], new_v[n] = [r.reshape(w.shape) for r in res]

    return (loss, dx[None], *[reduced[n] for n in WEIGHTS], *[deltas[n] for n in WEIGHTS],
            *[new_m[n] for n in WEIGHTS], *[new_v[n] for n in WEIGHTS])
```

```python
import functools

import jax
import jax.numpy as jnp
from jax import lax
from jax.experimental import pallas as pl
from jax.experimental.pallas import tpu as pltpu

F32, BF16 = jnp.float32, jnp.bfloat16
MESH_ID = pl.DeviceIdType.MESH
AXES = ("x", "y", "c")
N_DEV = 8

EPS = 1e-6
DEPTH, N_A = 4, 2
GM_W, GM_GROUPS, GM_P = 768, 6, 128
MEM_W, MEM_HEADS, HEAD_DIM = 256, 4, 64
SB_W, SB_BLK = 768, 128
LANES = 128
QK_SCALE = HEAD_DIM ** -0.5
GELU_C, GELU_A = 0.7978845608028654, 0.044715

ADAM_LR, ADAM_B1, ADAM_B2, ADAM_EPS, ADAM_WD, ADAM_STEP = 0.001, 0.9, 0.999, 1e-08, 0.01, 10

VMEM_LIMIT = 56 * 1024 * 1024
PACK_COLS = 512

NT = (((1,), (1,)), ((), ()))
TN = (((0,), (0,)), ((), ()))


def _params(*sem):
    return pltpu.CompilerParams(dimension_semantics=sem, vmem_limit_bytes=VMEM_LIMIT)


def _tile(n, target, mult=LANES):
    best = None
    for t in range(mult, min(n, target) + 1, mult):
        if n % t == 0:
            best = t
    return best if best is not None else n


def _dot(a, b, dims=None):
    if dims is None:
        return jnp.dot(a, b, preferred_element_type=F32)
    return lax.dot_general(a, b, dims, preferred_element_type=F32)


def exchange(srcs, group, same_src, name, split=False):
    size = {"pair": 2, "quad": 4, "all": 8}[group]
    n = len(srcs)
    chunk_shapes = [tuple(s.shape) if same_src else tuple(s.shape[1:]) for s in srcs]
    pieces = [cs[0] if split else 1 for cs in chunk_shapes]
    n_dma = sum(pieces)

    def body(*refs):
        src_refs, out_refs = refs[:n], refs[n:2 * n]
        send_sems, recv_sems, local_sems = refs[2 * n:]
        x, y, c = lax.axis_index("x"), lax.axis_index("y"), lax.axis_index("c")
        if group == "pair":
            me, dev = c, lambda p: (x, y, p)
        elif group == "quad":
            me, dev = 2 * x + y, lambda p: (p // 2, p % 2, c)
        else:
            me, dev = 4 * x + 2 * y + c, lambda p: (p // 4, (p // 2) % 2, p % 2)

        def chunk(t, idx):
            return src_refs[t] if same_src else src_refs[t].at[idx]

        def copies(k, idx, slot, peer):
            out, w = [], k * n_dma
            for t in range(n):
                src, dst = chunk(t, idx), out_refs[t].at[slot]
                for s_ref, d_ref in ([(src.at[u], dst.at[u]) for u in range(pieces[t])] if split else [(src, dst)]):
                    out.append(pltpu.make_async_remote_copy(
                        src_ref=s_ref, dst_ref=d_ref, send_sem=send_sems.at[w], recv_sem=recv_sems.at[w],
                        device_id=dev(peer), device_id_type=MESH_ID))
                    w += 1
            return out

        local = [pltpu.make_async_copy(chunk(t, me), out_refs[t].at[me], local_sems.at[t]) for t in range(n)]
        for cp in local:
            cp.start()
        sends = []
        for k in range(1, size):
            peer = (me + k) % size
            sends += copies(k, peer, me, peer)
        for cp in sends:
            cp.start()
        for k in range(1, size):
            sender = (me + size - k) % size
            for cp in copies(k, me, sender, sender):
                cp.wait_recv()
        for cp in sends:
            cp.wait_send()
        for cp in local:
            cp.wait()

    hbm = pl.BlockSpec(memory_space=pltpu.HBM)
    return pl.pallas_call(
        body, name=name,
        out_shape=[jax.ShapeDtypeStruct((size,) + cs, s.dtype) for cs, s in zip(chunk_shapes, srcs)],
        in_specs=[hbm] * n, out_specs=[hbm] * n,
        scratch_shapes=[pltpu.SemaphoreType.DMA((size * n_dma,)), pltpu.SemaphoreType.DMA((size * n_dma,)),
                        pltpu.SemaphoreType.DMA((n,))],
    )(*srcs)


class Side:
    def __init__(self, srcs, same_src):
        self.srcs, self.same_src, self.n = list(srcs), same_src, len(srcs)
        self.chunk_shapes = [tuple(s.shape) if same_src else tuple(s.shape[1:]) for s in srcs]

    def out_shapes(self):
        return [jax.ShapeDtypeStruct((N_DEV,) + cs, s.dtype) for cs, s in zip(self.chunk_shapes, self.srcs)]

    def scratch(self):
        return [pltpu.SemaphoreType.DMA((N_DEV * self.n,)), pltpu.SemaphoreType.DMA((N_DEV * self.n,))]

    def _copies(self, src_refs, land_refs, send_sems, recv_sems, outgoing):
        me = 4 * lax.axis_index("x") + 2 * lax.axis_index("y") + lax.axis_index("c")
        out = []
        for k in range(1, N_DEV):
            peer = (me + k) % N_DEV if outgoing else (me + N_DEV - k) % N_DEV
            for t in range(self.n):
                src = src_refs[t] if self.same_src else src_refs[t].at[peer if outgoing else me]
                out.append(pltpu.make_async_remote_copy(
                    src_ref=src, dst_ref=land_refs[t].at[me if outgoing else peer],
                    send_sem=send_sems.at[k * self.n + t], recv_sem=recv_sems.at[k * self.n + t],
                    device_id=(peer // 4, (peer // 2) % 2, peer % 2), device_id_type=MESH_ID))
        return out

    def _own(self, src_refs, land_refs, send_sems):
        me = 4 * lax.axis_index("x") + 2 * lax.axis_index("y") + lax.axis_index("c")
        return [pltpu.make_async_copy(src_refs[t] if self.same_src else src_refs[t].at[me], land_refs[t].at[me],
                                      send_sems.at[t]) for t in range(self.n)]

    def start(self, src_refs, land_refs, send_sems, recv_sems):
        for cp in self._own(src_refs, land_refs, send_sems) + self._copies(src_refs, land_refs, send_sems, recv_sems, True):
            cp.start()

    def wait(self, src_refs, land_refs, send_sems, recv_sems):
        for cp in self._copies(src_refs, land_refs, send_sems, recv_sems, False):
            cp.wait_recv()
        for cp in self._copies(src_refs, land_refs, send_sems, recv_sems, True):
            cp.wait_send()
        for cp in self._own(src_refs, land_refs, send_sems):
            cp.wait()


def _call(body, side, name, grid, in_specs, out_specs, out_shape, scratch_shapes, dims, args):
    if side is None:
        res = pl.pallas_call(body, name=name, grid=grid, in_specs=in_specs, out_specs=out_specs, out_shape=out_shape,
                             scratch_shapes=scratch_shapes, compiler_params=_params(*dims))(*args)
        return list(res), []
    n_in, n_out, n_scr, ns = len(in_specs), len(out_specs), len(scratch_shapes), side.n

    def wrapped(*refs):
        ins, srcs = refs[:n_in], refs[n_in:n_in + ns]
        outs, lands = refs[n_in + ns:n_in + ns + n_out], refs[n_in + ns + n_out:n_in + 2 * ns + n_out]
        scratch, (send_sems, recv_sems) = refs[n_in + 2 * ns + n_out:n_in + 2 * ns + n_out + n_scr], refs[-2:]
        first, last = None, None
        for axis, steps in enumerate(grid):
            i = pl.program_id(axis)
            first = (i == 0) if first is None else first & (i == 0)
            last = (i == steps - 1) if last is None else last & (i == steps - 1)

        @pl.when(first)
        def _():
            side.start(srcs, lands, send_sems, recv_sems)

        body(*ins, *outs, *scratch)

        @pl.when(last)
        def _():
            side.wait(srcs, lands, send_sems, recv_sems)

    hbm = pl.BlockSpec(memory_space=pltpu.HBM)
    res = pl.pallas_call(
        wrapped, name=name, grid=grid, in_specs=list(in_specs) + [hbm] * ns, out_specs=list(out_specs) + [hbm] * ns,
        out_shape=list(out_shape) + side.out_shapes(), scratch_shapes=list(scratch_shapes) + side.scratch(),
        compiler_params=_params(*dims))(*args, *side.srcs)
    return list(res[:n_out]), list(res[n_out:])


def sum_leading(parts, out_dtype, name):
    k, rows, cols = parts.shape
    tr = _tile(rows, 512, 16)

    def body(p_ref, o_ref):
        acc = p_ref[0].astype(F32)
        for s in range(1, k):
            acc = acc + p_ref[s].astype(F32)
        o_ref[...] = acc.astype(o_ref.dtype)

    return pl.pallas_call(
        body, name=name, grid=(rows // tr,),
        in_specs=[pl.BlockSpec((k, tr, cols), lambda i: (0, i, 0))],
        out_specs=pl.BlockSpec((tr, cols), lambda i: (i, 0)),
        out_shape=jax.ShapeDtypeStruct((rows, cols), out_dtype),
        compiler_params=_params("arbitrary"),
    )(parts)


def _rms(xf):
    return lax.rsqrt(jnp.mean(xf * xf, axis=-1, keepdims=True) + EPS)


def norm_mm(x, g, w, out_dtype, name, emit_h=False, side=None):
    m, d = x.shape
    n = w.shape[1]
    tm, tn = _tile(m, 1024, 8), _tile(n, 1408)

    def body(x_ref, g_ref, w_ref, o_ref, *rest):
        h_ref = rest[-1]

        @pl.when(pl.program_id(1) == 0)
        def _():
            xf = x_ref[...]
            hb = ((xf * _rms(xf)) * g_ref[...]).astype(BF16)
            h_ref[...] = hb
            if emit_h:
                rest[0][...] = hb

        o_ref[...] = _dot(h_ref[...], w_ref[...]).astype(o_ref.dtype)

    out_shape = [jax.ShapeDtypeStruct((m, n), out_dtype)]
    out_specs = [pl.BlockSpec((tm, tn), lambda i, j: (i, j))]
    if emit_h:
        out_shape.append(jax.ShapeDtypeStruct((m, d), BF16))
        out_specs.append(pl.BlockSpec((tm, d), lambda i, j: (i, 0)))
    res, landed = _call(
        body, side, name, (m // tm, n // tn),
        [pl.BlockSpec((tm, d), lambda i, j: (i, 0)), pl.BlockSpec((1, d), lambda i, j: (0, 0)),
         pl.BlockSpec((d, tn), lambda i, j: (0, j))],
        out_specs, out_shape, [pltpu.VMEM((tm, d), BF16)], ("arbitrary", "arbitrary"), (x, g, w))
    out = res if emit_h else res[0]
    return out if side is None else (out, landed)


def mm_res(a, w, res, alpha, name):
    m, k = a.shape
    n = w.shape[1]
    tm, tn = _tile(m, 1024, 8), _tile(n, 1024)

    def body(a_ref, w_ref, r_ref, o_ref):
        o_ref[...] = r_ref[...] + alpha * _dot(a_ref[...], w_ref[...])

    return pl.pallas_call(
        body, name=name, grid=(m // tm, n // tn),
        in_specs=[pl.BlockSpec((tm, k), lambda i, j: (i, 0)), pl.BlockSpec((k, tn), lambda i, j: (0, j)),
                  pl.BlockSpec((tm, tn), lambda i, j: (i, j))],
        out_specs=pl.BlockSpec((tm, tn), lambda i, j: (i, j)),
        out_shape=jax.ShapeDtypeStruct((m, n), F32),
        compiler_params=_params("arbitrary", "arbitrary"),
    )(a, w, res)


def mm_nt(x, w, alpha, name):
    m, d = x.shape
    n = w.shape[0]
    tm, tn = _tile(m, 1024, 8), _tile(n, 1408)

    def body(x_ref, w_ref, o_ref, xb_ref):
        @pl.when(pl.program_id(1) == 0)
        def _():
            xb_ref[...] = x_ref[...].astype(BF16)

        o_ref[...] = (alpha * _dot(xb_ref[...], w_ref[...], NT)).astype(o_ref.dtype)

    return pl.pallas_call(
        body, name=name, grid=(m // tm, n // tn),
        in_specs=[pl.BlockSpec((tm, d), lambda i, j: (i, 0)), pl.BlockSpec((tn, d), lambda i, j: (j, 0))],
        out_specs=pl.BlockSpec((tm, tn), lambda i, j: (i, j)),
        out_shape=jax.ShapeDtypeStruct((m, n), BF16),
        scratch_shapes=[pltpu.VMEM((tm, d), BF16)],
        compiler_params=_params("arbitrary", "arbitrary"),
    )(x, w)


def mm_tn(a, b, alpha, name, ta_target=1024, tb_target=512, side=None):
    s, ka = a.shape
    nb = b.shape[1]
    ta, tb, ts = _tile(ka, ta_target), _tile(nb, tb_target), _tile(s, 1024, 16)
    steps = s // ts

    def body(a_ref, b_ref, o_ref, acc_ref):
        t = pl.program_id(2)

        @pl.when(t == 0)
        def _():
            acc_ref[...] = jnp.zeros_like(acc_ref)

        acc_ref[...] += _dot(a_ref[...].astype(BF16), b_ref[...].astype(BF16), TN)

        @pl.when(t == steps - 1)
        def _():
            o_ref[...] = alpha * acc_ref[...]

    res, landed = _call(
        body, side, name, (ka // ta, nb // tb, steps),
        [pl.BlockSpec((ts, ta), lambda i, j, t: (t, i)), pl.BlockSpec((ts, tb), lambda i, j, t: (t, j))],
        [pl.BlockSpec((ta, tb), lambda i, j, t: (i, j))], [jax.ShapeDtypeStruct((ka, nb), F32)],
        [pltpu.VMEM((ta, tb), F32)], ("arbitrary", "arbitrary", "arbitrary"), (a, b))
    return res[0] if side is None else (res[0], landed)


def mm_nt_normbwd(dy, w, x, g, res, name, side=None):
    m, n = dy.shape
    d = w.shape[0]
    tm, tk = _tile(m, 1024, 8), _tile(n, 1408)
    steps = n // tk
    has_res = res is not None

    def body(*refs):
        if has_res:
            dy_ref, w_ref, x_ref, g_ref, r_ref, dx_ref, dg_ref, h_ref, acc_ref = refs
        else:
            dy_ref, w_ref, x_ref, g_ref, dx_ref, dg_ref, h_ref, acc_ref = refs
        i, t = pl.program_id(0), pl.program_id(1)

        @pl.when(t == 0)
        def _():
            acc_ref[...] = jnp.zeros_like(acc_ref)

        @pl.when((t == 0) & (i == 0))
        def _():
            dg_ref[...] = jnp.zeros_like(dg_ref)

        acc_ref[...] += _dot(dy_ref[...], w_ref[...], NT)

        @pl.when(t == steps - 1)
        def _():
            xf = x_ref[...]
            r = _rms(xf)
            xhat = xf * r
            dh = acc_ref[...]
            gain = g_ref[...]
            dg_ref[...] += jnp.sum(dh * xhat, axis=0, keepdims=True)
            dxhat = dh * gain
            dx = r * (dxhat - xhat * jnp.mean(dxhat * xhat, axis=-1, keepdims=True))
            dx_ref[...] = (r_ref[...] + dx) if has_res else dx
            h_ref[...] = (xhat * gain).astype(BF16)

    row = lambda i, t: (i, 0)
    in_specs = [pl.BlockSpec((tm, tk), lambda i, t: (i, t)), pl.BlockSpec((d, tk), lambda i, t: (0, t)),
                pl.BlockSpec((tm, d), row), pl.BlockSpec((1, d), lambda i, t: (0, 0))]
    args = [dy, w, x, g]
    if has_res:
        in_specs.append(pl.BlockSpec((tm, d), row))
        args.append(res)
    res, landed = _call(
        body, side, name, (m // tm, steps), in_specs,
        [pl.BlockSpec((tm, d), row), pl.BlockSpec((1, d), lambda i, t: (0, 0)), pl.BlockSpec((tm, d), row)],
        [jax.ShapeDtypeStruct((m, d), F32), jax.ShapeDtypeStruct((1, d), F32), jax.ShapeDtypeStruct((m, d), BF16)],
        [pltpu.VMEM((tm, d), F32)], ("arbitrary", "arbitrary"), args)
    return res if side is None else (res, landed)


def _sigmoid(z):
    return 1.0 / (1.0 + jnp.exp(-z))


def _swiglu(gate_b, up_b):
    gate = gate_b.astype(F32)
    return (gate * _sigmoid(gate) * up_b.astype(F32)).astype(BF16)


def swiglu_mm_res(gu, w, res, alpha, name, side=None):
    m, f2 = gu.shape
    f, n = w.shape
    tm, tc = _tile(m, 256, 16), _tile(f, 256)

    def body(gu_ref, w_ref, r_ref, o_ref):
        acc = jnp.zeros((tm, n), F32)
        for c0 in range(0, f, tc):
            act = _swiglu(gu_ref[:, c0:c0 + tc], gu_ref[:, f + c0:f + c0 + tc])
            acc = acc + _dot(act, w_ref[c0:c0 + tc, :])
        o_ref[...] = r_ref[...] + alpha * acc

    out, landed = _call(
        body, side, name, (m // tm,),
        [pl.BlockSpec((tm, f2), lambda i: (i, 0)), pl.BlockSpec((f, n), lambda i: (0, 0)),
         pl.BlockSpec((tm, n), lambda i: (i, 0))],
        [pl.BlockSpec((tm, n), lambda i: (i, 0))], [jax.ShapeDtypeStruct((m, n), F32)], [], ("arbitrary",), (gu, w, res))
    return out[0] if side is None else (out[0], landed)


def swiglu_mm_tn(gu, b, alpha, name, side=None):
    s, f2 = gu.shape
    f, n = f2 // 2, b.shape[1]
    ta, ts = _tile(f, 1408), _tile(s, 512, 16)
    steps, half = s // ts, f // ta

    def body(g_ref, u_ref, b_ref, o_ref, acc_ref):
        t = pl.program_id(1)

        @pl.when(t == 0)
        def _():
            acc_ref[...] = jnp.zeros_like(acc_ref)

        bb = b_ref[...].astype(BF16)
        for c0 in range(0, ta, LANES):
            acc_ref[c0:c0 + LANES, :] += _dot(_swiglu(g_ref[:, c0:c0 + LANES], u_ref[:, c0:c0 + LANES]), bb, TN)

        @pl.when(t == steps - 1)
        def _():
            o_ref[...] = alpha * acc_ref[...]

    res, landed = _call(
        body, side, name, (half, steps),
        [pl.BlockSpec((ts, ta), lambda i, t: (t, i)), pl.BlockSpec((ts, ta), lambda i, t: (t, half + i)),
         pl.BlockSpec((ts, n), lambda i, t: (t, 0))],
        [pl.BlockSpec((ta, n), lambda i, t: (i, 0))], [jax.ShapeDtypeStruct((f, n), F32)],
        [pltpu.VMEM((ta, n), F32)], ("arbitrary", "arbitrary"), (gu, gu, b))
    return res[0] if side is None else (res[0], landed)


def mm_nt_swiglu_bwd(x, w, gu, alpha, name, side=None):
    m, d = x.shape
    f = w.shape[0]
    tm, tc = _tile(m, 256, 16), _tile(f, 256)

    def body(x_ref, w_ref, gu_ref, o_ref):
        xb = x_ref[...].astype(BF16)
        for c0 in range(0, f, tc):
            d_act = alpha * _dot(xb, w_ref[c0:c0 + tc, :], NT)
            gate, up = gu_ref[:, c0:c0 + tc].astype(F32), gu_ref[:, f + c0:f + c0 + tc].astype(F32)
            sg = _sigmoid(gate)
            o_ref[:, c0:c0 + tc] = (d_act * up * (sg * (1.0 + gate * (1.0 - sg)))).astype(BF16)
            o_ref[:, f + c0:f + c0 + tc] = (d_act * (gate * sg)).astype(BF16)

    res, landed = _call(
        body, side, name, (m // tm,),
        [pl.BlockSpec((tm, d), lambda i: (i, 0)), pl.BlockSpec((f, d), lambda i: (0, 0)),
         pl.BlockSpec((tm, 2 * f), lambda i: (i, 0))],
        [pl.BlockSpec((tm, 2 * f), lambda i: (i, 0))], [jax.ShapeDtypeStruct((m, 2 * f), BF16)], [], ("arbitrary",),
        (x, w, gu))
    return res[0] if side is None else (res[0], landed)


def _gelu(x):
    return 0.5 * x * (1.0 + jnp.tanh(GELU_C * (x + GELU_A * x * x * x)))


def _gelu_grad(x):
    t = jnp.tanh(GELU_C * (x + GELU_A * x * x * x))
    return 0.5 * (1.0 + t) + 0.5 * x * (1.0 - t * t) * (GELU_C * (1.0 + 3.0 * GELU_A * x * x))


def _chunk_mask():
    row = lax.broadcasted_iota(jnp.int32, (GM_P, GM_P), 0)
    col = lax.broadcasted_iota(jnp.int32, (GM_P, GM_P), 1)
    return (col < GM_P // 2) | (row >= GM_P // 2)


def gmlp_fwd(proj, gain, w_s, bias, name):
    s, pw = proj.shape
    tm = _tile(s, 256, GM_P)

    def body(p_ref, gain_ref, w_ref, b_ref, o_ref):
        mask = _chunk_mask()
        u = _gelu(p_ref[:, :GM_W])
        v = _gelu(p_ref[:, GM_W:2 * GM_W])
        vn = ((v * _rms(v)) * gain_ref[...]).astype(BF16)
        for g in range(GM_GROUPS):
            wg = jnp.where(mask, w_ref[g], 0.0).astype(BF16)
            cols = slice(g * GM_P, (g + 1) * GM_P)
            for n in range(tm // GM_P):
                rows = slice(n * GM_P, (n + 1) * GM_P)
                mixed = _dot(wg, vn[rows, cols]) + b_ref[:, cols]
                o_ref[rows, cols] = (u[rows, cols] * mixed).astype(BF16)

    return pl.pallas_call(
        body, name=name, grid=(s // tm,),
        in_specs=[pl.BlockSpec((tm, pw), lambda i: (i, 0)), pl.BlockSpec((1, GM_W), lambda i: (0, 0)),
                  pl.BlockSpec((GM_GROUPS, GM_P, GM_P), lambda i: (0, 0, 0)), pl.BlockSpec((GM_P, GM_W), lambda i: (0, 0))],
        out_specs=pl.BlockSpec((tm, GM_W), lambda i: (i, 0)),
        out_shape=jax.ShapeDtypeStruct((s, GM_W), BF16), compiler_params=_params("arbitrary"),
    )(proj, gain, w_s, bias)


def gmlp_bwd(proj, dy, gain, w_s, bias, name):
    s, pw = proj.shape
    dw_total = dy.shape[1]
    tm = _tile(s, 256, GM_P)

    def body(p_ref, dy_ref, gain_ref, w_ref, b_ref, dp_ref, dw_ref, db_ref, dgain_ref, dvn_ref):
        @pl.when(pl.program_id(0) == 0)
        def _():
            dw_ref[...] = jnp.zeros_like(dw_ref)
            db_ref[...] = jnp.zeros_like(db_ref)
            dgain_ref[...] = jnp.zeros_like(dgain_ref)

        mask = _chunk_mask()
        pu = p_ref[:, :GM_W]
        pv = p_ref[:, GM_W:2 * GM_W]
        u = _gelu(pu)
        v = _gelu(pv)
        r = _rms(v)
        vhat = v * r
        gain = gain_ref[...]
        vn = (vhat * gain).astype(BF16)
        gu_grad = _gelu_grad(pu)
        for g in range(GM_GROUPS):
            wg = jnp.where(mask, w_ref[g], 0.0).astype(BF16)
            cols = slice(g * GM_P, (g + 1) * GM_P)
            dw_acc = jnp.zeros((GM_P, GM_P), F32)
            db_acc = jnp.zeros((GM_P, 1), F32)
            for n in range(tm // GM_P):
                rows = slice(n * GM_P, (n + 1) * GM_P)
                dyb = dy_ref[rows, cols].astype(F32)
                vnb = vn[rows, cols]
                mixed = _dot(wg, vnb) + b_ref[:, cols]
                dmixed = dyb * u[rows, cols]
                dmb = dmixed.astype(BF16)
                dp_ref[rows, cols] = (dyb * mixed * gu_grad[rows, cols]).astype(BF16)
                dw_acc = dw_acc + _dot(dmb, vnb, NT)
                db_acc = db_acc + jnp.sum(dmixed, axis=1, keepdims=True)
                dvn_ref[rows, cols] = _dot(wg, dmb, TN)
            dw_ref[g] += jnp.where(mask, dw_acc, 0.0)
            db_ref[g] += jnp.broadcast_to(db_acc, (GM_P, GM_P))
        dvn = dvn_ref[...]
        dgain_ref[...] += jnp.sum(dvn * vhat, axis=0, keepdims=True)
        dvhat = dvn * gain
        dv = r * (dvhat - vhat * jnp.mean(dvhat * vhat, axis=-1, keepdims=True))
        dp_ref[:, GM_W:] = (dv * _gelu_grad(pv)).astype(BF16)

    const3 = lambda i: (0, 0, 0)
    return pl.pallas_call(
        body, name=name, grid=(s // tm,),
        in_specs=[pl.BlockSpec((tm, pw), lambda i: (i, 0)), pl.BlockSpec((tm, dw_total), lambda i: (i, 0)),
                  pl.BlockSpec((1, GM_W), lambda i: (0, 0)), pl.BlockSpec((GM_GROUPS, GM_P, GM_P), const3),
                  pl.BlockSpec((GM_P, GM_W), lambda i: (0, 0))],
        out_specs=[pl.BlockSpec((tm, 2 * GM_W), lambda i: (i, 0)), pl.BlockSpec((GM_GROUPS, GM_P, GM_P), const3),
                   pl.BlockSpec((GM_GROUPS, GM_P, GM_P), const3), pl.BlockSpec((1, GM_W), lambda i: (0, 0))],
        out_shape=[jax.ShapeDtypeStruct((s, 2 * GM_W), BF16), jax.ShapeDtypeStruct((GM_GROUPS, GM_P, GM_P), F32),
                   jax.ShapeDtypeStruct((GM_GROUPS, GM_P, GM_P), F32), jax.ShapeDtypeStruct((1, GM_W), F32)],
        scratch_shapes=[pltpu.VMEM((tm, GM_W), F32)],
        compiler_params=_params("arbitrary"),
    )(proj, dy, gain, w_s, bias)


def _keep(mask, xb):
    return jnp.where(mask, xb.astype(F32), 0.0).astype(BF16)


def _head_masks(rows, width, heads):
    lane = lax.broadcasted_iota(jnp.int32, (rows, width), 1)
    return [(lane >= HEAD_DIM * h) & (lane < HEAD_DIM * (h + 1)) for h in range(heads)]


def _mem_probs(qh, k):
    sc = _dot(qh, k, NT) * QK_SCALE
    e = jnp.exp(sc - jnp.max(sc, axis=-1, keepdims=True))
    return e / jnp.sum(e, axis=-1, keepdims=True)


def mem_fwd(proj, q_blk, mem_kv, layer, name):
    s = proj.shape[0]
    n_mem = mem_kv.shape[0]
    tm = _tile(s, 512, 16)

    def body(q_ref, k_ref, v_ref, o_ref):
        q = q_ref[...].astype(BF16)
        k, v = k_ref[...], v_ref[...]
        out = jnp.zeros((tm, MEM_W), F32)
        for hm in _head_masks(tm, MEM_W, MEM_HEADS):
            p = _mem_probs(_keep(hm, q), k)
            out = out + jnp.where(hm, _dot(p.astype(BF16), v), 0.0)
        o_ref[...] = out.astype(BF16)

    return pl.pallas_call(
        body, name=name, grid=(s // tm,),
        in_specs=[pl.BlockSpec((tm, MEM_W), lambda i: (i, q_blk)), pl.BlockSpec((n_mem, MEM_W), lambda i: (0, 2 * layer)),
                  pl.BlockSpec((n_mem, MEM_W), lambda i: (0, 2 * layer + 1))],
        out_specs=pl.BlockSpec((tm, MEM_W), lambda i: (i, 0)),
        out_shape=jax.ShapeDtypeStruct((s, MEM_W), BF16), compiler_params=_params("arbitrary"),
    )(proj, mem_kv, mem_kv)


def mem_bwd(proj, q_blk, mem_kv, layer, dy, dy_blk, name):
    s = proj.shape[0]
    n_mem = mem_kv.shape[0]
    tm = _tile(s, 512, 16)

    def body(q_ref, k_ref, v_ref, dy_ref, dq_ref, dk_ref, dv_ref):
        @pl.when(pl.program_id(0) == 0)
        def _():
            dk_ref[...] = jnp.zeros_like(dk_ref)
            dv_ref[...] = jnp.zeros_like(dv_ref)

        q = q_ref[...].astype(BF16)
        k, v = k_ref[...], v_ref[...]
        dy = dy_ref[...]
        dq = jnp.zeros((tm, MEM_W), F32)
        dk = jnp.zeros((n_mem, MEM_W), F32)
        dv = jnp.zeros((n_mem, MEM_W), F32)
        for hm in _head_masks(tm, MEM_W, MEM_HEADS):
            qh = _keep(hm, q)
            dyh = _keep(hm, dy)
            p = _mem_probs(qh, k)
            dp = _dot(dyh, v, NT)
            dv = dv + _dot(p.astype(BF16), dyh, TN)
            ds = (p * (dp - jnp.sum(dp * p, axis=-1, keepdims=True)) * QK_SCALE).astype(BF16)
            dq = dq + jnp.where(hm, _dot(ds, k), 0.0)
            dk = dk + _dot(ds, qh, TN)
        dq_ref[...] = dq.astype(BF16)
        dk_ref[...] += dk
        dv_ref[...] += dv

    const = lambda i: (0, 0)
    return pl.pallas_call(
        body, name=name, grid=(s // tm,),
        in_specs=[pl.BlockSpec((tm, MEM_W), lambda i: (i, q_blk)), pl.BlockSpec((n_mem, MEM_W), lambda i: (0, 2 * layer)),
                  pl.BlockSpec((n_mem, MEM_W), lambda i: (0, 2 * layer + 1)), pl.BlockSpec((tm, MEM_W), lambda i: (i, dy_blk))],
        out_specs=[pl.BlockSpec((tm, MEM_W), lambda i: (i, 0)), pl.BlockSpec((n_mem, MEM_W), const),
                   pl.BlockSpec((n_mem, MEM_W), const)],
        out_shape=[jax.ShapeDtypeStruct((s, MEM_W), BF16), jax.ShapeDtypeStruct((n_mem, MEM_W), F32),
                   jax.ShapeDtypeStruct((n_mem, MEM_W), F32)],
        compiler_params=_params("arbitrary"),
    )(proj, mem_kv, mem_kv, dy)


SB_KEYS = 512
SB_SUB = SB_KEYS // SB_BLK
SB_QROWS = 512
SB_QB = SB_QROWS // SB_BLK
SB_CHAINS = 2 * SB_QB
SB_DEAD = -110.0


def _split(xf):
    hi = xf.astype(BF16)
    return hi, (xf - hi.astype(F32)).astype(BF16)


def _sb_consts():
    row = lax.bitwise_and(lax.broadcasted_iota(jnp.int32, (2 * SB_BLK, 2 * SB_BLK), 0), SB_BLK - 1)
    col = lax.broadcasted_iota(jnp.int32, (2 * SB_BLK, 2 * SB_BLK), 1)
    ones = col >= SB_BLK
    after2 = jnp.where(ones | (row > col), -1.0, 0.0).astype(BF16)
    from2 = jnp.where(ones | (row >= col), 1.0, 0.0).astype(BF16)
    r = lax.broadcasted_iota(jnp.int32, (SB_BLK, SB_BLK), 0)
    c = lax.broadcasted_iota(jnp.int32, (SB_BLK, SB_BLK), 1)
    return after2, from2, c - r, [c < HEAD_DIM, c >= HEAD_DIM]


def _suffix(xf, tri2):
    hi, lo = _split(xf)
    return _dot(jnp.concatenate([hi, lo], axis=1), tri2)


def _sb_logs(z, mask):
    softplus = jnp.maximum(z, 0.0) + jnp.log(1.0 + jnp.exp(-jnp.abs(z)))
    log_beta = z - softplus
    if mask is not None:
        softplus = jnp.where(mask, softplus, 0.0)
    return softplus, log_beta


def _sb_queries(q_ref, heads):
    q = q_ref[...].astype(F32) * QK_SCALE
    return [jnp.where(hm, q[r * SB_BLK:(r + 1) * SB_BLK], 0.0).astype(BF16) for r in range(SB_QB) for hm in heads]


def _sb_walk(i, block, state):
    places = SB_SUB // SB_QB
    assert places in (1, 2)
    own = lax.shift_right_logical(i * SB_QB, SB_SUB.bit_length() - 1)
    firsts = [[(v * SB_QB + r) * SB_BLK for r in range(SB_QB) for _ in range(2)] for v in range(places)]
    if places == 1:
        state = block(own, state, firsts[0])
    else:
        state = lax.cond(lax.bitwise_and(i, 1) == 0, lambda st: block(own, st, firsts[0]),
                         lambda st: block(own, st, firsts[1]), state)

    def live(carry):
        j, st = carry
        most = st[0][0]
        for run in st[0][1:]:
            most = jnp.maximum(most, run)
        return (j >= 0) & (jnp.max(most) > SB_DEAD)

    return lax.while_loop(live, lambda carry: (carry[0] - 1, block(carry[0], carry[1], None)), (own - 1, state))[1]


def _sb_tiles(first):
    out = []
    for c in reversed(range(SB_SUB)):
        for n in range(SB_CHAINS):
            if first is None or c * SB_BLK < first[n]:
                out.append((c, n, "before"))
            elif c * SB_BLK == first[n]:
                out.append((c, n, "diagonal"))
    return out


def _sb_heads_apart(stacked, heads, r):
    return jnp.where(heads[0], stacked[2 * r * SB_BLK:(2 * r + 1) * SB_BLK],
                     stacked[(2 * r + 1) * SB_BLK:(2 * r + 2) * SB_BLK])


def sb_fwd(proj, kv, name):
    s = proj.shape[0]
    assert s % SB_KEYS == 0 and SB_KEYS % SB_QROWS == 0

    def body(q_ref, k_ref, v_ref, o_ref):
        after2, _, col_minus_row, heads = _sb_consts()
        q_all = jnp.concatenate(_sb_queries(q_ref, heads), axis=0)
        key_before_query = col_minus_row < 0

        def block(j, state, first):
            runs, acc = list(state[0]), state[1]
            rows = pl.ds(pl.multiple_of(j * SB_KEYS, SB_KEYS), SB_KEYS)
            kb, vb = k_ref[rows, :], v_ref[rows, :]
            z = _dot(q_all, kb, NT)
            pend = {}
            parts = [[jnp.zeros((SB_BLK, SB_BLK), BF16)] * SB_SUB for _ in range(SB_CHAINS)]
            for c, n, where in _sb_tiles(first):
                mask = key_before_query if where == "diagonal" else None
                softplus, lb = _sb_logs(z[n * SB_BLK:(n + 1) * SB_BLK, c * SB_BLK:(c + 1) * SB_BLK], mask)
                pend[c, n] = (lb, _suffix(softplus, after2), mask)
            for c, n, _ in _sb_tiles(first):
                lb, r, mask = pend.pop((c, n))
                a = jnp.exp(lb + r[:, :SB_BLK] + runs[n])
                if mask is not None:
                    a = jnp.where(mask, a, 0.0)
                parts[n][c] = a.astype(BF16)
                runs[n] = runs[n] + r[:, SB_BLK:]
            a_all = jnp.concatenate([jnp.concatenate(p, axis=1) for p in parts], axis=0)
            return tuple(runs), acc + _dot(a_all, vb)

        zero = jnp.zeros((SB_BLK, LANES), F32)
        state = _sb_walk(pl.program_id(1), block, ((zero,) * SB_CHAINS, jnp.zeros((SB_CHAINS * SB_BLK, LANES), F32)))
        for r in range(SB_QB):
            o_ref[r * SB_BLK:(r + 1) * SB_BLK, :] = _sb_heads_apart(state[1], heads, r)

    pairs = SB_W // LANES
    return pl.pallas_call(
        body, name=name, grid=(pairs, s // SB_QROWS),
        in_specs=[pl.BlockSpec((SB_QROWS, LANES), lambda p, i: (i, p)), pl.BlockSpec((s, LANES), lambda p, i: (0, p)),
                  pl.BlockSpec((s, LANES), lambda p, i: (0, pairs + p))],
        out_specs=pl.BlockSpec((SB_QROWS, LANES), lambda p, i: (i, p)),
        out_shape=jax.ShapeDtypeStruct((s, SB_W), F32),
        compiler_params=_params("arbitrary", "arbitrary"),
    )(proj, kv, kv)


def sb_bwd(proj, kv, out, dy, name):
    s = proj.shape[0]

    def body(q_ref, k_ref, v_ref, o_ref, do_ref, dq_ref, dk_ref, dv_ref):
        i = pl.program_id(1)

        @pl.when(i == 0)
        def _():
            dk_ref[...] = jnp.zeros_like(dk_ref)
            dv_ref[...] = jnp.zeros_like(dv_ref)

        after2, from2, col_minus_row, heads = _sb_consts()
        q_all = jnp.concatenate(_sb_queries(q_ref, heads), axis=0)
        key_before_query = col_minus_row < 0
        d_out = do_ref[...].astype(F32)
        prod = d_out * o_ref[...]
        dos, totals = [], []
        for r in range(SB_QB):
            rr = slice(r * SB_BLK, (r + 1) * SB_BLK)
            for hm in heads:
                dos.append(jnp.where(hm, d_out[rr], 0.0).astype(BF16))
                totals.append(jnp.broadcast_to(jnp.sum(jnp.where(hm, prod[rr], 0.0), axis=1, keepdims=True),
                                               (SB_BLK, SB_BLK)))
        do_all = jnp.concatenate(dos, axis=0)

        def block(j, state, first):
            runs, seens, dq = list(state[0]), list(state[1]), state[2]
            rows = pl.ds(pl.multiple_of(j * SB_KEYS, SB_KEYS), SB_KEYS)
            kb, vb = k_ref[rows, :], v_ref[rows, :]
            z = _dot(q_all, kb, NT)
            da = _dot(do_all, vb, NT)
            pend, pend2 = {}, {}
            a_parts = [[jnp.zeros((SB_BLK, SB_BLK), BF16)] * SB_SUB for _ in range(SB_CHAINS)]
            dz_parts = [[jnp.zeros((SB_BLK, SB_BLK), BF16)] * SB_SUB for _ in range(SB_CHAINS)]
            for c, n, where in _sb_tiles(first):
                mask = key_before_query if where == "diagonal" else None
                softplus, lb = _sb_logs(z[n * SB_BLK:(n + 1) * SB_BLK, c * SB_BLK:(c + 1) * SB_BLK], mask)
                pend[c, n] = (softplus, lb, _suffix(softplus, after2), mask)
            for c, n, _ in _sb_tiles(first):
                softplus, lb, r, mask = pend.pop((c, n))
                a = jnp.exp(lb + r[:, :SB_BLK] + runs[n])
                if mask is not None:
                    a = jnp.where(mask, a, 0.0)
                runs[n] = runs[n] + r[:, SB_BLK:]
                ab = a.astype(BF16)
                a_parts[n][c] = ab
                dl = ab.astype(F32) * da[n * SB_BLK:(n + 1) * SB_BLK, c * SB_BLK:(c + 1) * SB_BLK]
                pend2[c, n] = (softplus, lb, dl, _suffix(dl, from2), mask)
            for c, n, _ in _sb_tiles(first):
                softplus, lb, dl, r2, mask = pend2.pop((c, n))
                d_lom = totals[n] - (r2[:, :SB_BLK] + seens[n])
                if mask is not None:
                    d_lom = jnp.where(mask, d_lom, 0.0)
                seens[n] = seens[n] + r2[:, SB_BLK:]
                dz_parts[n][c] = (dl * jnp.exp(-softplus) - d_lom * jnp.exp(lb)).astype(BF16)
            a_all = jnp.concatenate([jnp.concatenate(p, axis=1) for p in a_parts], axis=0)
            dz_all = jnp.concatenate([jnp.concatenate(p, axis=1) for p in dz_parts], axis=0)
            dv_ref[rows, :] += _dot(a_all, do_all, TN)
            dk_ref[rows, :] += _dot(dz_all, q_all, TN)
            return tuple(runs), tuple(seens), dq + _dot(dz_all, kb)

        zero = jnp.zeros((SB_BLK, LANES), F32)
        state = _sb_walk(i, block, ((zero,) * SB_CHAINS, (zero,) * SB_CHAINS,
                                    jnp.zeros((SB_CHAINS * SB_BLK, LANES), F32)))
        for r in range(SB_QB):
            dq_ref[r * SB_BLK:(r + 1) * SB_BLK, :] = (_sb_heads_apart(state[2], heads, r) * QK_SCALE).astype(BF16)

    pairs = SB_W // LANES
    blk = lambda p, i: (i, p)
    col = lambda p, i: (0, p)
    return pl.pallas_call(
        body, name=name, grid=(pairs, s // SB_QROWS),
        in_specs=[pl.BlockSpec((SB_QROWS, LANES), blk), pl.BlockSpec((s, LANES), col),
                  pl.BlockSpec((s, LANES), lambda p, i: (0, pairs + p)), pl.BlockSpec((SB_QROWS, LANES), blk),
                  pl.BlockSpec((SB_QROWS, LANES), blk)],
        out_specs=[pl.BlockSpec((SB_QROWS, LANES), blk), pl.BlockSpec((s, LANES), col), pl.BlockSpec((s, LANES), col)],
        out_shape=[jax.ShapeDtypeStruct((s, SB_W), BF16), jax.ShapeDtypeStruct((s, SB_W), F32),
                   jax.ShapeDtypeStruct((s, SB_W), F32)],
        compiler_params=_params("arbitrary", "arbitrary"),
    )(proj, kv, kv, out, dy)


def final_loss(x, g, target, name):
    s, d = x.shape
    tm = _tile(s, 256, 8)

    def body(x_ref, g_ref, t_ref, loss_ref, dx_ref, dg_ref):
        @pl.when(pl.program_id(0) == 0)
        def _():
            loss_ref[...] = jnp.zeros_like(loss_ref)
            dg_ref[...] = jnp.zeros_like(dg_ref)

        xf = x_ref[...]
        r = _rms(xf)
        xhat = xf * r
        gain = g_ref[...]
        diff = xhat * gain - t_ref[...]
        sq = jnp.sum(jnp.sum(diff * diff, axis=1, keepdims=True), axis=0, keepdims=True)
        loss_ref[...] += jnp.broadcast_to(sq, loss_ref.shape)
        dy = diff * (1.0 / d)
        dg_ref[...] += jnp.sum(dy * xhat, axis=0, keepdims=True)
        dxhat = dy * gain
        dx_ref[...] = r * (dxhat - xhat * jnp.mean(dxhat * xhat, axis=-1, keepdims=True))

    row = lambda i: (i, 0)
    const = lambda i: (0, 0)
    return pl.pallas_call(
        body, name=name, grid=(s // tm,),
        in_specs=[pl.BlockSpec((tm, d), row), pl.BlockSpec((1, d), const), pl.BlockSpec((tm, d), row)],
        out_specs=[pl.BlockSpec((8, LANES), const), pl.BlockSpec((tm, d), row), pl.BlockSpec((1, d), const)],
        out_shape=[jax.ShapeDtypeStruct((8, LANES), F32), jax.ShapeDtypeStruct((s, d), F32), jax.ShapeDtypeStruct((1, d), F32)],
        compiler_params=_params("arbitrary"),
    )(x, g, target)


def adamw(w, parts, m, v, name):
    rows, cols = w.shape
    k = parts.shape[0]
    tr = _tile(rows, 512, 16)
    c1, c2 = 1.0 - ADAM_B1 ** ADAM_STEP, 1.0 - ADAM_B2 ** ADAM_STEP

    def body(w_ref, p_ref, m_ref, v_ref, g_ref, d_ref, nm_ref, nv_ref):
        grad = p_ref[0].astype(F32)
        for s in range(1, k):
            grad = grad + p_ref[s].astype(F32)
        nm = ADAM_B1 * m_ref[...] + (1.0 - ADAM_B1) * grad
        nv = ADAM_B2 * v_ref[...] + (1.0 - ADAM_B2) * (grad * grad)
        g_ref[...] = grad
        d_ref[...] = -ADAM_LR * ((nm / c1) / (jnp.sqrt(nv / c2) + ADAM_EPS) + ADAM_WD * w_ref[...])
        nm_ref[...] = nm
        nv_ref[...] = nv

    spec = pl.BlockSpec((tr, cols), lambda i: (i, 0))
    shape = jax.ShapeDtypeStruct((rows, cols), F32)
    return pl.pallas_call(
        body, name=name, grid=(rows // tr,),
        in_specs=[spec, pl.BlockSpec((k, tr, cols), lambda i: (0, i, 0)), spec, spec],
        out_specs=[spec] * 4, out_shape=[shape] * 4,
        compiler_params=_params("arbitrary"),
    )(w, parts, m, v)


SHARDED = {"ffn1_w_gate": 2, "ffn1_w_up": 2, "ffn1_w_down": 1, "ffn2_w_gate": 2, "ffn2_w_up": 2, "ffn2_w_down": 1,
           "w_mem_kv": 1, "a_w_in": 2, "a_w_out": 1, "w_kv": 1, "b_w_in": 1, "b_w_out": 1}
SMALL = ["ffn1_norm", "mix_norm", "ffn2_norm", "mem_norm", "kv_norm", "final_norm", "a_v_norm", "a_w_spatial", "a_b_spatial"]
WEIGHTS = ["ffn1_norm", "ffn1_w_gate", "ffn1_w_up", "ffn1_w_down", "mix_norm", "ffn2_norm", "ffn2_w_gate", "ffn2_w_up",
           "ffn2_w_down", "mem_norm", "w_mem_kv", "a_w_in", "a_v_norm", "a_w_spatial", "a_b_spatial", "a_w_out", "kv_norm",
           "w_kv", "b_w_in", "b_w_out", "final_norm"]


def _all_sum(parts, name):
    flat = jnp.concatenate([p.reshape(-1) for p in parts])
    pad = (-flat.size) % (16 * LANES)
    buf = jnp.pad(flat, (0, pad)).reshape(-1, LANES)
    total = sum_leading(exchange([buf], "all", True, name)[0], F32, name + "_sum").reshape(-1)
    out, off = [], 0
    for p in parts:
        out.append(total[off:off + p.size].reshape(p.shape))
        off += p.size
    return out


def _device_index():
    return 4 * lax.axis_index("x") + 2 * lax.axis_index("y") + lax.axis_index("c")


def kernel(x, mem, ffn1_norm, ffn1_w_gate, ffn1_w_up, ffn1_w_down, mix_norm, ffn2_norm, ffn2_w_gate, ffn2_w_up, ffn2_w_down, mem_norm, w_mem_kv, a_w_in, a_v_norm, a_w_spatial, a_b_spatial, a_w_out, kv_norm, w_kv, b_w_in, b_w_out, final_norm, loss_target, m_ffn1_norm, m_ffn1_w_gate, m_ffn1_w_up, m_ffn1_w_down, m_mix_norm, m_ffn2_norm, m_ffn2_w_gate, m_ffn2_w_up, m_ffn2_w_down, m_mem_norm, m_w_mem_kv, m_a_w_in, m_a_v_norm, m_a_w_spatial, m_a_b_spatial, m_a_w_out, m_kv_norm, m_w_kv, m_b_w_in, m_b_w_out, m_final_norm, v_ffn1_norm, v_ffn1_w_gate, v_ffn1_w_up, v_ffn1_w_down, v_mix_norm, v_ffn2_norm, v_ffn2_w_gate, v_ffn2_w_up, v_ffn2_w_down, v_mem_norm, v_w_mem_kv, v_a_w_in, v_a_v_norm, v_a_w_spatial, v_a_b_spatial, v_a_w_out, v_kv_norm, v_w_kv, v_b_w_in, v_b_w_out, v_final_norm):
    weights = dict(ffn1_norm=ffn1_norm, ffn1_w_gate=ffn1_w_gate, ffn1_w_up=ffn1_w_up, ffn1_w_down=ffn1_w_down, mix_norm=mix_norm, ffn2_norm=ffn2_norm, ffn2_w_gate=ffn2_w_gate, ffn2_w_up=ffn2_w_up, ffn2_w_down=ffn2_w_down, mem_norm=mem_norm, w_mem_kv=w_mem_kv, a_w_in=a_w_in, a_v_norm=a_v_norm, a_w_spatial=a_w_spatial, a_b_spatial=a_b_spatial, a_w_out=a_w_out, kv_norm=kv_norm, w_kv=w_kv, b_w_in=b_w_in, b_w_out=b_w_out, final_norm=final_norm)
    mom1 = dict(ffn1_norm=m_ffn1_norm, ffn1_w_gate=m_ffn1_w_gate, ffn1_w_up=m_ffn1_w_up, ffn1_w_down=m_ffn1_w_down, mix_norm=m_mix_norm, ffn2_norm=m_ffn2_norm, ffn2_w_gate=m_ffn2_w_gate, ffn2_w_up=m_ffn2_w_up, ffn2_w_down=m_ffn2_w_down, mem_norm=m_mem_norm, w_mem_kv=m_w_mem_kv, a_w_in=m_a_w_in, a_v_norm=m_a_v_norm, a_w_spatial=m_a_w_spatial, a_b_spatial=m_a_b_spatial, a_w_out=m_a_w_out, kv_norm=m_kv_norm, w_kv=m_w_kv, b_w_in=m_b_w_in, b_w_out=m_b_w_out, final_norm=m_final_norm)
    mom2 = dict(ffn1_norm=v_ffn1_norm, ffn1_w_gate=v_ffn1_w_gate, ffn1_w_up=v_ffn1_w_up, ffn1_w_down=v_ffn1_w_down, mix_norm=v_mix_norm, ffn2_norm=v_ffn2_norm, ffn2_w_gate=v_ffn2_w_gate, ffn2_w_up=v_ffn2_w_up, ffn2_w_down=v_ffn2_w_down, mem_norm=v_mem_norm, w_mem_kv=v_w_mem_kv, a_w_in=v_a_w_in, a_v_norm=v_a_v_norm, a_w_spatial=v_a_w_spatial, a_b_spatial=v_a_b_spatial, a_w_out=v_a_w_out, kv_norm=v_kv_norm, w_kv=v_w_kv, b_w_in=v_b_w_in, b_w_out=v_b_w_out, final_norm=v_final_norm)

    dev = _device_index()
    xs, mem_in, target = x[0], mem[0], loss_target[0]
    d_model = xs.shape[1]
    shards = {n: weights[n] for n in SHARDED}

    def riders(l, f, part):
        w_in, w_out, idx = ("a_w_in", "a_w_out", l) if l < N_A else ("b_w_in", "b_w_out", l - N_A)
        if part == "gu":
            return [(f + "_w_gate", l), (f + "_w_up", l)]
        more = [(w_in, idx)] if f == "ffn1" else [(w_out, idx)] + ([("w_kv", None)] if l == N_A else [])
        return [(f + "_w_down", l)] + more

    def cut_axis(key):
        return SHARDED[key[0]] - (0 if key[1] is None else 1)

    def block(key):
        return (shards[key[0]] if key[1] is None else shards[key[0]][key[1]]).astype(BF16)

    first = [k for f in ("ffn1", "ffn2") for part in ("gu", "down") for k in riders(0, f, part)] + [("w_mem_kv", None)]
    landed = dict(zip(first, exchange([block(k) for k in first], "all", True, "gather_first")))
    assembled = {}

    def whole(n, l=None):
        if (n, l) not in assembled:
            pieces = [landed[n, l][d] for d in range(N_DEV)]
            if n.endswith("_w_gate"):
                pieces += [landed[n.replace("_w_gate", "_w_up"), l][d] for d in range(N_DEV)]
            assembled[n, l] = jnp.concatenate(pieces, axis=cut_axis((n, l)))
        return assembled[n, l]

    def whole_gu(f, l):
        return whole(f + "_w_gate", l)

    def ride(keys, same_src, pieces):
        return Side([pieces(k) for k in keys], same_src) if keys else None

    vn_width = a_v_norm.shape[1]
    a_v_full = _all_sum([lax.dynamic_update_slice(jnp.zeros((N_A, N_DEV * vn_width), F32), a_v_norm, (0, dev * vn_width))],
                        "gather_v_norm")[0]
    row = lambda v: v.reshape(1, -1)
    w_mem_cat = whole("w_mem_kv").transpose(1, 0, 2).reshape(d_model, -1)
    bias = [jnp.repeat(a_b_spatial[i].T, GM_P, axis=1) for i in range(N_A)]

    mem_kv, mem_h = norm_mm(mem_in, row(mem_norm), w_mem_cat, BF16, "mem_kv", emit_h=True)

    def ffn_fwd(xin, f, l):
        keys_gu, keys_down = (riders(l + 1, f, "gu"), riders(l + 1, f, "down")) if l + 1 < DEPTH else ([], [])
        gu = norm_mm(xin, row(weights[f + "_norm"][l]), whole_gu(f, l), BF16, "ffn_gu", side=ride(keys_gu, True, block))
        if keys_gu:
            gu, arrived = gu
            landed.update(zip(keys_gu, arrived))
        out = swiglu_mm_res(gu, whole(f + "_w_down", l), xin, 0.5, "ffn_down", side=ride(keys_down, True, block))
        if keys_down:
            out, arrived = out
            landed.update(zip(keys_down, arrived))
        return out, gu

    saved = []
    kv = x_kv = None
    cur = xs
    for l in range(DEPTH):
        st = {"x0": cur}
        if l == N_A:
            x_kv = cur
            kv = norm_mm(cur, row(kv_norm), whole("w_kv"), BF16, "kv_proj")
        st["x1"], st["gu1"] = ffn_fwd(cur, "ffn1", l)
        if l < N_A:
            proj = norm_mm(st["x1"], row(mix_norm[l]), whole("a_w_in", l), F32, "a_proj")
            y_tok = gmlp_fwd(proj, row(a_v_full[l]), a_w_spatial[l], bias[l], "gmlp_fwd")
            y_mem = mem_fwd(proj, 2 * GM_W // MEM_W, mem_kv, l, "mem_fwd_a")
            w_out = whole("a_w_out", l)
        else:
            proj = norm_mm(st["x1"], row(mix_norm[l]), whole("b_w_in", l - N_A), BF16, "b_proj")
            st["sb_out"] = sb_fwd(proj, kv, "sb_fwd")
            y_tok = st["sb_out"].astype(BF16)
            y_mem = mem_fwd(proj, SB_W // MEM_W, mem_kv, l, "mem_fwd_b")
            w_out = whole("b_w_out", l - N_A)
        st["proj"] = proj
        st["y"] = jnp.concatenate([y_tok, y_mem], axis=1)
        st["x2"] = mm_res(st["y"], w_out, st["x1"], 1.0, "mix_out")
        cur, st["gu2"] = ffn_fwd(st["x2"], "ffn2", l)
        saved.append(st)

    loss_blk, dx, d_final = final_loss(cur, row(final_norm), target, "final_loss")
    loss = lax.psum(loss_blk[0, 0] * (0.5 / d_model), AXES)

    grads = {n: [None] * weights[n].shape[0] for n in WEIGHTS if weights[n].ndim >= 2 and n not in ("w_kv",)}
    grads["final_norm"] = d_final.reshape(-1)
    d_mem_kv = [None] * DEPTH
    d_kv = []

    summed = {}

    def pieces(key):
        g = grads[key[0]] if key[1] is None else grads[key[0]][key[1]]
        return jnp.stack(jnp.split(g.astype(BF16), N_DEV, axis=cut_axis(key)))

    def ffn_bwd(dx, xin, gu, f, l):
        carried = [riders(l + 1, g, part) if f == "ffn2" and l + 1 < DEPTH else [] for g in ("ffn2", "ffn1")
                   for part in ("gu", "down")]
        sides = [ride(keys, False, pieces) for keys in carried]

        def own(result, k):
            if sides[k] is None:
                return result
            summed.update(zip(carried[k], result[1]))
            return result[0]

        d_gu = own(mm_nt_swiglu_bwd(dx, whole(f + "_w_down", l), gu, 0.5, "ffn_dgu", side=sides[0]), 0)
        dx_new, d_gain, h = own(mm_nt_normbwd(d_gu, whole_gu(f, l), xin, row(weights[f + "_norm"][l]), dx, "ffn_dx",
                                              side=sides[1]), 1)
        d_wgu = own(mm_tn(h, d_gu, 1.0, "ffn_dwgu", tb_target=1408, side=sides[2]), 2)
        d_wdown = own(swiglu_mm_tn(gu, dx, 0.5, "ffn_dwdown", side=sides[3]), 3)
        half = d_wgu.shape[1] // 2
        grads[f + "_w_gate"][l], grads[f + "_w_up"][l] = d_wgu[:, :half], d_wgu[:, half:]
        grads[f + "_w_down"][l] = d_wdown
        grads[f + "_norm"][l] = d_gain.reshape(-1)
        return dx_new

    for l in reversed(range(DEPTH)):
        st = saved[l]
        dx = ffn_bwd(dx, st["x2"], st["gu2"], "ffn2", l)
        proj = st["proj"]
        key_in, key_out, idx = ("a_w_in", "a_w_out", l) if l < N_A else ("b_w_in", "b_w_out", l - N_A)
        w_in, w_out = whole(key_in, idx), whole(key_out, idx)
        dy = mm_nt(dx, w_out, 1.0, "mix_dy")
        grads[key_out][idx] = mm_tn(st["y"], dx, 1.0, "mix_dwout", tb_target=1024)
        if l < N_A:
            d_uv, d_ws, d_bs, d_vgain = gmlp_bwd(proj, dy, row(a_v_full[l]), a_w_spatial[l], bias[l], "gmlp_bwd")
            grads["a_w_spatial"][l], grads["a_b_spatial"][l], grads["a_v_norm"][l] = d_ws, d_bs[:, :, 0], d_vgain.reshape(-1)
            d_q, d_k, d_v = mem_bwd(proj, 2 * GM_W // MEM_W, mem_kv, l, dy, GM_W // MEM_W, "mem_bwd_a")
            d_proj = jnp.concatenate([d_uv, d_q], axis=1)
        else:
            d_qsb, d_ksb, d_vsb = sb_bwd(proj, kv, st["sb_out"], dy, "sb_bwd")
            d_kv.append(jnp.concatenate([d_ksb, d_vsb], axis=1))
            d_q, d_k, d_v = mem_bwd(proj, SB_W // MEM_W, mem_kv, l, dy, SB_W // MEM_W, "mem_bwd_b")
            d_proj = jnp.concatenate([d_qsb, d_q], axis=1)
        d_mem_kv[l] = jnp.concatenate([d_k, d_v], axis=1)
        dx, d_gain, h = mm_nt_normbwd(d_proj, w_in, st["x1"], row(mix_norm[l]), dx, "mix_dx")
        grads["mix_norm"][l] = d_gain.reshape(-1)
        grads[key_in][idx] = mm_tn(h, d_proj, 1.0, "mix_dwin")
        dx = ffn_bwd(dx, st["x0"], st["gu1"], "ffn1", l)
        if l == N_A:
            d_kv_b = sum_leading(jnp.stack(d_kv), BF16, "kv_dsum")
            dx, d_gain, h = mm_nt_normbwd(d_kv_b, whole("w_kv"), x_kv, row(kv_norm), dx, "kv_dx")
            grads["kv_norm"] = d_gain.reshape(-1)
            grads["w_kv"] = mm_tn(h, d_kv_b, 1.0, "kv_dw")

    d_mem_all = jnp.concatenate(d_mem_kv, axis=1).astype(BF16)
    _, d_gain, _ = mm_nt_normbwd(d_mem_all, w_mem_cat, mem_in, row(mem_norm), None, "mem_dnorm")
    grads["mem_norm"] = d_gain.reshape(-1)
    d_wmem = mm_tn(mem_h, d_mem_all, 1.0, "mem_dw")
    grads["w_mem_kv"] = d_wmem.reshape(d_model, DEPTH, -1).transpose(1, 0, 2)

    summed.update(zip(first, exchange([pieces(k) for k in first], "all", False, "scatter_last")))
    parts = {n: summed[n, None] if (n, None) in summed else
             jnp.stack([summed[n, i] for i in range(weights[n].shape[0])], axis=1) for n in SHARDED}
    grads = {n: (jnp.stack(g) if isinstance(g, list) else g) for n, g in grads.items()}
    for n, g in zip(SMALL, _all_sum([grads[n] for n in SMALL], "sum_small")):
        parts[n] = g[None]
    parts["a_v_norm"] = lax.dynamic_slice(parts["a_v_norm"], (0, 0, dev * vn_width), (1,) + a_v_norm.shape)

    reduced, deltas, new_m, new_v = {}, {}, {}, {}
    for n in WEIGHTS:
        w = weights[n]
        view = (lambda a: a.reshape(-1, a.shape[-1]))
        res = adamw(view(w), parts[n].reshape(parts[n].shape[0], -1, w.shape[-1]), view(mom1[n]), view(mom2[n]), "adamw")
        reduced[n], deltas[n], new_m[n], new_v[n] = [r.reshape(w.shape) for r in res]

    return (loss, dx[None], *[reduced[n] for n in WEIGHTS], *[deltas[n] for n in WEIGHTS],
            *[new_m[n] for n in WEIGHTS], *[new_v[n] for n in WEIGHTS])
```

```python
import functools

import jax
import jax.numpy as jnp
from jax import lax
from jax.experimental import pallas as pl
from jax.experimental.pallas import tpu as pltpu

F32, BF16 = jnp.float32, jnp.bfloat16
MESH_ID = pl.DeviceIdType.MESH
AXES = ("x", "y", "c")
N_DEV = 8

EPS = 1e-6
DEPTH, N_A = 4, 2
GM_W, GM_GROUPS, GM_P = 768, 6, 128
MEM_W, MEM_HEADS, HEAD_DIM = 256, 4, 64
SB_W, SB_BLK = 768, 128
LANES = 128
QK_SCALE = HEAD_DIM ** -0.5
GELU_C, GELU_A = 0.7978845608028654, 0.044715

ADAM_LR, ADAM_B1, ADAM_B2, ADAM_EPS, ADAM_WD, ADAM_STEP = 0.001, 0.9, 0.999, 1e-08, 0.01, 10

VMEM_LIMIT = 56 * 1024 * 1024
PACK_COLS = 512

NT = (((1,), (1,)), ((), ()))
TN = (((0,), (0,)), ((), ()))


def _params(*sem):
    return pltpu.CompilerParams(dimension_semantics=sem, vmem_limit_bytes=VMEM_LIMIT)


def _tile(n, target, mult=LANES):
    best = None
    for t in range(mult, min(n, target) + 1, mult):
        if n % t == 0:
            best = t
    return best if best is not None else n


def _dot(a, b, dims=None):
    if dims is None:
        return jnp.dot(a, b, preferred_element_type=F32)
    return lax.dot_general(a, b, dims, preferred_element_type=F32)


def exchange(srcs, group, same_src, name, split=False):
    size = {"pair": 2, "quad": 4, "all": 8}[group]
    n = len(srcs)
    chunk_shapes = [tuple(s.shape) if same_src else tuple(s.shape[1:]) for s in srcs]
    pieces = [cs[0] if split else 1 for cs in chunk_shapes]
    n_dma = sum(pieces)

    def body(*refs):
        src_refs, out_refs = refs[:n], refs[n:2 * n]
        send_sems, recv_sems, local_sems = refs[2 * n:]
        x, y, c = lax.axis_index("x"), lax.axis_index("y"), lax.axis_index("c")
        if group == "pair":
            me, dev = c, lambda p: (x, y, p)
        elif group == "quad":
            me, dev = 2 * x + y, lambda p: (p // 2, p % 2, c)
        else:
            me, dev = 4 * x + 2 * y + c, lambda p: (p // 4, (p // 2) % 2, p % 2)

        def chunk(t, idx):
            return src_refs[t] if same_src else src_refs[t].at[idx]

        def copies(k, idx, slot, peer):
            out, w = [], k * n_dma
            for t in range(n):
                src, dst = chunk(t, idx), out_refs[t].at[slot]
                for s_ref, d_ref in ([(src.at[u], dst.at[u]) for u in range(pieces[t])] if split else [(src, dst)]):
                    out.append(pltpu.make_async_remote_copy(
                        src_ref=s_ref, dst_ref=d_ref, send_sem=send_sems.at[w], recv_sem=recv_sems.at[w],
                        device_id=dev(peer), device_id_type=MESH_ID))
                    w += 1
            return out

        local = [pltpu.make_async_copy(chunk(t, me), out_refs[t].at[me], local_sems.at[t]) for t in range(n)]
        for cp in local:
            cp.start()
        sends = []
        for k in range(1, size):
            peer = (me + k) % size
            sends += copies(k, peer, me, peer)
        for cp in sends:
            cp.start()
        for k in range(1, size):
            sender = (me + size - k) % size
            for cp in copies(k, me, sender, sender):
                cp.wait_recv()
        for cp in sends:
            cp.wait_send()
        for cp in local:
            cp.wait()

    hbm = pl.BlockSpec(memory_space=pltpu.HBM)
    return pl.pallas_call(
        body, name=name,
        out_shape=[jax.ShapeDtypeStruct((size,) + cs, s.dtype) for cs, s in zip(chunk_shapes, srcs)],
        in_specs=[hbm] * n, out_specs=[hbm] * n,
        scratch_shapes=[pltpu.SemaphoreType.DMA((size * n_dma,)), pltpu.SemaphoreType.DMA((size * n_dma,)),
                        pltpu.SemaphoreType.DMA((n,))],
    )(*srcs)


class Side:
    def __init__(self, srcs, same_src):
        self.srcs, self.same_src, self.n = list(srcs), same_src, len(srcs)
        self.chunk_shapes = [tuple(s.shape) if same_src else tuple(s.shape[1:]) for s in srcs]

    def out_shapes(self):
        return [jax.ShapeDtypeStruct((N_DEV,) + cs, s.dtype) for cs, s in zip(self.chunk_shapes, self.srcs)]

    def scratch(self):
        return [pltpu.SemaphoreType.DMA((N_DEV * self.n,)), pltpu.SemaphoreType.DMA((N_DEV * self.n,))]

    def _copies(self, src_refs, land_refs, send_sems, recv_sems, outgoing):
        me = 4 * lax.axis_index("x") + 2 * lax.axis_index("y") + lax.axis_index("c")
        out = []
        for k in range(1, N_DEV):
            peer = (me + k) % N_DEV if outgoing else (me + N_DEV - k) % N_DEV
            for t in range(self.n):
                src = src_refs[t] if self.same_src else src_refs[t].at[peer if outgoing else me]
                out.append(pltpu.make_async_remote_copy(
                    src_ref=src, dst_ref=land_refs[t].at[me if outgoing else peer],
                    send_sem=send_sems.at[k * self.n + t], recv_sem=recv_sems.at[k * self.n + t],
                    device_id=(peer // 4, (peer // 2) % 2, peer % 2), device_id_type=MESH_ID))
        return out

    def _own(self, src_refs, land_refs, send_sems):
        me = 4 * lax.axis_index("x") + 2 * lax.axis_index("y") + lax.axis_index("c")
        return [pltpu.make_async_copy(src_refs[t] if self.same_src else src_refs[t].at[me], land_refs[t].at[me],
                                      send_sems.at[t]) for t in range(self.n)]

    def start(self, src_refs, land_refs, send_sems, recv_sems):
        for cp in self._own(src_refs, land_refs, send_sems) + self._copies(src_refs, land_refs, send_sems, recv_sems, True):
            cp.start()

    def wait(self, src_refs, land_refs, send_sems, recv_sems):
        for cp in self._copies(src_refs, land_refs, send_sems, recv_sems, False):
            cp.wait_recv()
        for cp in self._copies(src_refs, land_refs, send_sems, recv_sems, True):
            cp.wait_send()
        for cp in self._own(src_refs, land_refs, send_sems):
            cp.wait()


def _call(body, side, name, grid, in_specs, out_specs, out_shape, scratch_shapes, dims, args):
    if side is None:
        res = pl.pallas_call(body, name=name, grid=grid, in_specs=in_specs, out_specs=out_specs, out_shape=out_shape,
                             scratch_shapes=scratch_shapes, compiler_params=_params(*dims))(*args)
        return list(res), []
    n_in, n_out, n_scr, ns = len(in_specs), len(out_specs), len(scratch_shapes), side.n

    def wrapped(*refs):
        ins, srcs = refs[:n_in], refs[n_in:n_in + ns]
        outs, lands = refs[n_in + ns:n_in + ns + n_out], refs[n_in + ns + n_out:n_in + 2 * ns + n_out]
        scratch, (send_sems, recv_sems) = refs[n_in + 2 * ns + n_out:n_in + 2 * ns + n_out + n_scr], refs[-2:]
        first, last = None, None
        for axis, steps in enumerate(grid):
            i = pl.program_id(axis)
            first = (i == 0) if first is None else first & (i == 0)
            last = (i == steps - 1) if last is None else last & (i == steps - 1)

        @pl.when(first)
        def _():
            side.start(srcs, lands, send_sems, recv_sems)

        body(*ins, *outs, *scratch)

        @pl.when(last)
        def _():
            side.wait(srcs, lands, send_sems, recv_sems)

    hbm = pl.BlockSpec(memory_space=pltpu.HBM)
    res = pl.pallas_call(
        wrapped, name=name, grid=grid, in_specs=list(in_specs) + [hbm] * ns, out_specs=list(out_specs) + [hbm] * ns,
        out_shape=list(out_shape) + side.out_shapes(), scratch_shapes=list(scratch_shapes) + side.scratch(),
        compiler_params=_params(*dims))(*args, *side.srcs)
    return list(res[:n_out]), list(res[n_out:])


def sum_leading(parts, out_dtype, name):
    k, rows, cols = parts.shape
    tr = _tile(rows, 512, 16)

    def body(p_ref, o_ref):
        acc = p_ref[0].astype(F32)
        for s in range(1, k):
            acc = acc + p_ref[s].astype(F32)
        o_ref[...] = acc.astype(o_ref.dtype)

    return pl.pallas_call(
        body, name=name, grid=(rows // tr,),
        in_specs=[pl.BlockSpec((k, tr, cols), lambda i: (0, i, 0))],
        out_specs=pl.BlockSpec((tr, cols), lambda i: (i, 0)),
        out_shape=jax.ShapeDtypeStruct((rows, cols), out_dtype),
        compiler_params=_params("arbitrary"),
    )(parts)


def _rms(xf):
    return lax.rsqrt(jnp.mean(xf * xf, axis=-1, keepdims=True) + EPS)


def norm_mm(x, g, w, out_dtype, name, emit_h=False, side=None):
    m, d = x.shape
    n = w.shape[1]
    tm, tn = _tile(m, 1024, 8), _tile(n, 1408)

    def body(x_ref, g_ref, w_ref, o_ref, *rest):
        h_ref = rest[-1]

        @pl.when(pl.program_id(1) == 0)
        def _():
            xf = x_ref[...]
            hb = ((xf * _rms(xf)) * g_ref[...]).astype(BF16)
            h_ref[...] = hb
            if emit_h:
                rest[0][...] = hb

        o_ref[...] = _dot(h_ref[...], w_ref[...]).astype(o_ref.dtype)

    out_shape = [jax.ShapeDtypeStruct((m, n), out_dtype)]
    out_specs = [pl.BlockSpec((tm, tn), lambda i, j: (i, j))]
    if emit_h:
        out_shape.append(jax.ShapeDtypeStruct((m, d), BF16))
        out_specs.append(pl.BlockSpec((tm, d), lambda i, j: (i, 0)))
    res, landed = _call(
        body, side, name, (m // tm, n // tn),
        [pl.BlockSpec((tm, d), lambda i, j: (i, 0)), pl.BlockSpec((1, d), lambda i, j: (0, 0)),
         pl.BlockSpec((d, tn), lambda i, j: (0, j))],
        out_specs, out_shape, [pltpu.VMEM((tm, d), BF16)], ("arbitrary", "arbitrary"), (x, g, w))
    out = res if emit_h else res[0]
    return out if side is None else (out, landed)


def mm_res(a, w, res, alpha, name, side=None):
    m, k = a.shape
    n = w.shape[1]
    tm, tn = _tile(m, 1024, 8), _tile(n, 1024)

    def body(a_ref, w_ref, r_ref, o_ref):
        o_ref[...] = r_ref[...] + alpha * _dot(a_ref[...], w_ref[...])

    out, landed = _call(
        body, side, name, (m // tm, n // tn),
        [pl.BlockSpec((tm, k), lambda i, j: (i, 0)), pl.BlockSpec((k, tn), lambda i, j: (0, j)),
         pl.BlockSpec((tm, tn), lambda i, j: (i, j))],
        [pl.BlockSpec((tm, tn), lambda i, j: (i, j))], [jax.ShapeDtypeStruct((m, n), F32)], [],
        ("arbitrary", "arbitrary"), (a, w, res))
    return out[0] if side is None else (out[0], landed)


def mm_nt(x, w, alpha, name):
    m, d = x.shape
    n = w.shape[0]
    tm, tn = _tile(m, 1024, 8), _tile(n, 1408)

    def body(x_ref, w_ref, o_ref, xb_ref):
        @pl.when(pl.program_id(1) == 0)
        def _():
            xb_ref[...] = x_ref[...].astype(BF16)

        o_ref[...] = (alpha * _dot(xb_ref[...], w_ref[...], NT)).astype(o_ref.dtype)

    return pl.pallas_call(
        body, name=name, grid=(m // tm, n // tn),
        in_specs=[pl.BlockSpec((tm, d), lambda i, j: (i, 0)), pl.BlockSpec((tn, d), lambda i, j: (j, 0))],
        out_specs=pl.BlockSpec((tm, tn), lambda i, j: (i, j)),
        out_shape=jax.ShapeDtypeStruct((m, n), BF16),
        scratch_shapes=[pltpu.VMEM((tm, d), BF16)],
        compiler_params=_params("arbitrary", "arbitrary"),
    )(x, w)


def mm_tn(a, b, alpha, name, ta_target=1024, tb_target=512, side=None):
    s, ka = a.shape
    nb = b.shape[1]
    ta, tb, ts = _tile(ka, ta_target), _tile(nb, tb_target), _tile(s, 1024, 16)
    steps = s // ts

    def body(a_ref, b_ref, o_ref, acc_ref):
        t = pl.program_id(2)

        @pl.when(t == 0)
        def _():
            acc_ref[...] = jnp.zeros_like(acc_ref)

        acc_ref[...] += _dot(a_ref[...].astype(BF16), b_ref[...].astype(BF16), TN)

        @pl.when(t == steps - 1)
        def _():
            o_ref[...] = alpha * acc_ref[...]

    res, landed = _call(
        body, side, name, (ka // ta, nb // tb, steps),
        [pl.BlockSpec((ts, ta), lambda i, j, t: (t, i)), pl.BlockSpec((ts, tb), lambda i, j, t: (t, j))],
        [pl.BlockSpec((ta, tb), lambda i, j, t: (i, j))], [jax.ShapeDtypeStruct((ka, nb), F32)],
        [pltpu.VMEM((ta, tb), F32)], ("arbitrary", "arbitrary", "arbitrary"), (a, b))
    return res[0] if side is None else (res[0], landed)


def mm_nt_normbwd(dy, w, x, g, res, name, side=None):
    m, n = dy.shape
    d = w.shape[0]
    tm, tk = _tile(m, 1024, 8), _tile(n, 1408)
    steps = n // tk
    has_res = res is not None

    def body(*refs):
        if has_res:
            dy_ref, w_ref, x_ref, g_ref, r_ref, dx_ref, dg_ref, h_ref, acc_ref = refs
        else:
            dy_ref, w_ref, x_ref, g_ref, dx_ref, dg_ref, h_ref, acc_ref = refs
        i, t = pl.program_id(0), pl.program_id(1)

        @pl.when(t == 0)
        def _():
            acc_ref[...] = jnp.zeros_like(acc_ref)

        @pl.when((t == 0) & (i == 0))
        def _():
            dg_ref[...] = jnp.zeros_like(dg_ref)

        acc_ref[...] += _dot(dy_ref[...], w_ref[...], NT)

        @pl.when(t == steps - 1)
        def _():
            xf = x_ref[...]
            r = _rms(xf)
            xhat = xf * r
            dh = acc_ref[...]
            gain = g_ref[...]
            dg_ref[...] += jnp.sum(dh * xhat, axis=0, keepdims=True)
            dxhat = dh * gain
            dx = r * (dxhat - xhat * jnp.mean(dxhat * xhat, axis=-1, keepdims=True))
            dx_ref[...] = (r_ref[...] + dx) if has_res else dx
            h_ref[...] = (xhat * gain).astype(BF16)

    row = lambda i, t: (i, 0)
    in_specs = [pl.BlockSpec((tm, tk), lambda i, t: (i, t)), pl.BlockSpec((d, tk), lambda i, t: (0, t)),
                pl.BlockSpec((tm, d), row), pl.BlockSpec((1, d), lambda i, t: (0, 0))]
    args = [dy, w, x, g]
    if has_res:
        in_specs.append(pl.BlockSpec((tm, d), row))
        args.append(res)
    res, landed = _call(
        body, side, name, (m // tm, steps), in_specs,
        [pl.BlockSpec((tm, d), row), pl.BlockSpec((1, d), lambda i, t: (0, 0)), pl.BlockSpec((tm, d), row)],
        [jax.ShapeDtypeStruct((m, d), F32), jax.ShapeDtypeStruct((1, d), F32), jax.ShapeDtypeStruct((m, d), BF16)],
        [pltpu.VMEM((tm, d), F32)], ("arbitrary", "arbitrary"), args)
    return res if side is None else (res, landed)


def _sigmoid(z):
    return 1.0 / (1.0 + jnp.exp(-z))


def _swiglu(gate_b, up_b):
    gate = gate_b.astype(F32)
    return (gate * _sigmoid(gate) * up_b.astype(F32)).astype(BF16)


def swiglu_mm_res(gu, w, res, alpha, name, side=None):
    m, f2 = gu.shape
    f, n = w.shape
    tm, tc = _tile(m, 256, 16), _tile(f, 256)

    def body(gu_ref, w_ref, r_ref, o_ref):
        acc = jnp.zeros((tm, n), F32)
        for c0 in range(0, f, tc):
            act = _swiglu(gu_ref[:, c0:c0 + tc], gu_ref[:, f + c0:f + c0 + tc])
            acc = acc + _dot(act, w_ref[c0:c0 + tc, :])
        o_ref[...] = r_ref[...] + alpha * acc

    out, landed = _call(
        body, side, name, (m // tm,),
        [pl.BlockSpec((tm, f2), lambda i: (i, 0)), pl.BlockSpec((f, n), lambda i: (0, 0)),
         pl.BlockSpec((tm, n), lambda i: (i, 0))],
        [pl.BlockSpec((tm, n), lambda i: (i, 0))], [jax.ShapeDtypeStruct((m, n), F32)], [], ("arbitrary",), (gu, w, res))
    return out[0] if side is None else (out[0], landed)


def swiglu_mm_tn(gu, b, alpha, name, side=None):
    s, f2 = gu.shape
    f, n = f2 // 2, b.shape[1]
    ta, ts = _tile(f, 1408), _tile(s, 512, 16)
    steps, half = s // ts, f // ta

    def body(g_ref, u_ref, b_ref, o_ref, acc_ref):
        t = pl.program_id(1)

        @pl.when(t == 0)
        def _():
            acc_ref[...] = jnp.zeros_like(acc_ref)

        bb = b_ref[...].astype(BF16)
        for c0 in range(0, ta, LANES):
            acc_ref[c0:c0 + LANES, :] += _dot(_swiglu(g_ref[:, c0:c0 + LANES], u_ref[:, c0:c0 + LANES]), bb, TN)

        @pl.when(t == steps - 1)
        def _():
            o_ref[...] = alpha * acc_ref[...]

    res, landed = _call(
        body, side, name, (half, steps),
        [pl.BlockSpec((ts, ta), lambda i, t: (t, i)), pl.BlockSpec((ts, ta), lambda i, t: (t, half + i)),
         pl.BlockSpec((ts, n), lambda i, t: (t, 0))],
        [pl.BlockSpec((ta, n), lambda i, t: (i, 0))], [jax.ShapeDtypeStruct((f, n), F32)],
        [pltpu.VMEM((ta, n), F32)], ("arbitrary", "arbitrary"), (gu, gu, b))
    return res[0] if side is None else (res[0], landed)


def mm_nt_swiglu_bwd(x, w, gu, alpha, name, side=None):
    m, d = x.shape
    f = w.shape[0]
    tm, tc = _tile(m, 256, 16), _tile(f, 256)

    def body(x_ref, w_ref, gu_ref, o_ref):
        xb = x_ref[...].astype(BF16)
        for c0 in range(0, f, tc):
            d_act = alpha * _dot(xb, w_ref[c0:c0 + tc, :], NT)
            gate, up = gu_ref[:, c0:c0 + tc].astype(F32), gu_ref[:, f + c0:f + c0 + tc].astype(F32)
            sg = _sigmoid(gate)
            o_ref[:, c0:c0 + tc] = (d_act * up * (sg * (1.0 + gate * (1.0 - sg)))).astype(BF16)
            o_ref[:, f + c0:f + c0 + tc] = (d_act * (gate * sg)).astype(BF16)

    res, landed = _call(
        body, side, name, (m // tm,),
        [pl.BlockSpec((tm, d), lambda i: (i, 0)), pl.BlockSpec((f, d), lambda i: (0, 0)),
         pl.BlockSpec((tm, 2 * f), lambda i: (i, 0))],
        [pl.BlockSpec((tm, 2 * f), lambda i: (i, 0))], [jax.ShapeDtypeStruct((m, 2 * f), BF16)], [], ("arbitrary",),
        (x, w, gu))
    return res[0] if side is None else (res[0], landed)


def _gelu(x):
    return 0.5 * x * (1.0 + jnp.tanh(GELU_C * (x + GELU_A * x * x * x)))


def _gelu_grad(x):
    t = jnp.tanh(GELU_C * (x + GELU_A * x * x * x))
    return 0.5 * (1.0 + t) + 0.5 * x * (1.0 - t * t) * (GELU_C * (1.0 + 3.0 * GELU_A * x * x))


def _chunk_mask():
    row = lax.broadcasted_iota(jnp.int32, (GM_P, GM_P), 0)
    col = lax.broadcasted_iota(jnp.int32, (GM_P, GM_P), 1)
    return (col < GM_P // 2) | (row >= GM_P // 2)


def gmlp_fwd(proj, gain, w_s, bias, name):
    s, pw = proj.shape
    tm = _tile(s, 256, GM_P)

    def body(p_ref, gain_ref, w_ref, b_ref, o_ref):
        mask = _chunk_mask()
        u = _gelu(p_ref[:, :GM_W])
        v = _gelu(p_ref[:, GM_W:2 * GM_W])
        vn = ((v * _rms(v)) * gain_ref[...]).astype(BF16)
        for g in range(GM_GROUPS):
            wg = jnp.where(mask, w_ref[g], 0.0).astype(BF16)
            cols = slice(g * GM_P, (g + 1) * GM_P)
            for n in range(tm // GM_P):
                rows = slice(n * GM_P, (n + 1) * GM_P)
                mixed = _dot(wg, vn[rows, cols]) + b_ref[:, cols]
                o_ref[rows, cols] = (u[rows, cols] * mixed).astype(BF16)

    return pl.pallas_call(
        body, name=name, grid=(s // tm,),
        in_specs=[pl.BlockSpec((tm, pw), lambda i: (i, 0)), pl.BlockSpec((1, GM_W), lambda i: (0, 0)),
                  pl.BlockSpec((GM_GROUPS, GM_P, GM_P), lambda i: (0, 0, 0)), pl.BlockSpec((GM_P, GM_W), lambda i: (0, 0))],
        out_specs=pl.BlockSpec((tm, GM_W), lambda i: (i, 0)),
        out_shape=jax.ShapeDtypeStruct((s, GM_W), BF16), compiler_params=_params("arbitrary"),
    )(proj, gain, w_s, bias)


def gmlp_bwd(proj, dy, gain, w_s, bias, name):
    s, pw = proj.shape
    dw_total = dy.shape[1]
    tm = _tile(s, 256, GM_P)

    def body(p_ref, dy_ref, gain_ref, w_ref, b_ref, dp_ref, dw_ref, db_ref, dgain_ref, dvn_ref):
        @pl.when(pl.program_id(0) == 0)
        def _():
            dw_ref[...] = jnp.zeros_like(dw_ref)
            db_ref[...] = jnp.zeros_like(db_ref)
            dgain_ref[...] = jnp.zeros_like(dgain_ref)

        mask = _chunk_mask()
        pu = p_ref[:, :GM_W]
        pv = p_ref[:, GM_W:2 * GM_W]
        u = _gelu(pu)
        v = _gelu(pv)
        r = _rms(v)
        vhat = v * r
        gain = gain_ref[...]
        vn = (vhat * gain).astype(BF16)
        gu_grad = _gelu_grad(pu)
        for g in range(GM_GROUPS):
            wg = jnp.where(mask, w_ref[g], 0.0).astype(BF16)
            cols = slice(g * GM_P, (g + 1) * GM_P)
            dw_acc = jnp.zeros((GM_P, GM_P), F32)
            db_acc = jnp.zeros((GM_P, 1), F32)
            for n in range(tm // GM_P):
                rows = slice(n * GM_P, (n + 1) * GM_P)
                dyb = dy_ref[rows, cols].astype(F32)
                vnb = vn[rows, cols]
                mixed = _dot(wg, vnb) + b_ref[:, cols]
                dmixed = dyb * u[rows, cols]
                dmb = dmixed.astype(BF16)
                dp_ref[rows, cols] = (dyb * mixed * gu_grad[rows, cols]).astype(BF16)
                dw_acc = dw_acc + _dot(dmb, vnb, NT)
                db_acc = db_acc + jnp.sum(dmixed, axis=1, keepdims=True)
                dvn_ref[rows, cols] = _dot(wg, dmb, TN)
            dw_ref[g] += jnp.where(mask, dw_acc, 0.0)
            db_ref[g] += jnp.broadcast_to(db_acc, (GM_P, GM_P))
        dvn = dvn_ref[...]
        dgain_ref[...] += jnp.sum(dvn * vhat, axis=0, keepdims=True)
        dvhat = dvn * gain
        dv = r * (dvhat - vhat * jnp.mean(dvhat * vhat, axis=-1, keepdims=True))
        dp_ref[:, GM_W:] = (dv * _gelu_grad(pv)).astype(BF16)

    const3 = lambda i: (0, 0, 0)
    return pl.pallas_call(
        body, name=name, grid=(s // tm,),
        in_specs=[pl.BlockSpec((tm, pw), lambda i: (i, 0)), pl.BlockSpec((tm, dw_total), lambda i: (i, 0)),
                  pl.BlockSpec((1, GM_W), lambda i: (0, 0)), pl.BlockSpec((GM_GROUPS, GM_P, GM_P), const3),
                  pl.BlockSpec((GM_P, GM_W), lambda i: (0, 0))],
        out_specs=[pl.BlockSpec((tm, 2 * GM_W), lambda i: (i, 0)), pl.BlockSpec((GM_GROUPS, GM_P, GM_P), const3),
                   pl.BlockSpec((GM_GROUPS, GM_P, GM_P), const3), pl.BlockSpec((1, GM_W), lambda i: (0, 0))],
        out_shape=[jax.ShapeDtypeStruct((s, 2 * GM_W), BF16), jax.ShapeDtypeStruct((GM_GROUPS, GM_P, GM_P), F32),
                   jax.ShapeDtypeStruct((GM_GROUPS, GM_P, GM_P), F32), jax.ShapeDtypeStruct((1, GM_W), F32)],
        scratch_shapes=[pltpu.VMEM((tm, GM_W), F32)],
        compiler_params=_params("arbitrary"),
    )(proj, dy, gain, w_s, bias)


def _keep(mask, xb):
    return jnp.where(mask, xb.astype(F32), 0.0).astype(BF16)


def _head_masks(rows, width, heads):
    lane = lax.broadcasted_iota(jnp.int32, (rows, width), 1)
    return [(lane >= HEAD_DIM * h) & (lane < HEAD_DIM * (h + 1)) for h in range(heads)]


def _mem_probs(qh, k):
    sc = _dot(qh, k, NT) * QK_SCALE
    e = jnp.exp(sc - jnp.max(sc, axis=-1, keepdims=True))
    return e / jnp.sum(e, axis=-1, keepdims=True)


def mem_fwd(proj, q_blk, mem_kv, layer, name):
    s = proj.shape[0]
    n_mem = mem_kv.shape[0]
    tm = _tile(s, 512, 16)

    def body(q_ref, k_ref, v_ref, o_ref):
        q = q_ref[...].astype(BF16)
        k, v = k_ref[...], v_ref[...]
        out = jnp.zeros((tm, MEM_W), F32)
        for hm in _head_masks(tm, MEM_W, MEM_HEADS):
            p = _mem_probs(_keep(hm, q), k)
            out = out + jnp.where(hm, _dot(p.astype(BF16), v), 0.0)
        o_ref[...] = out.astype(BF16)

    return pl.pallas_call(
        body, name=name, grid=(s // tm,),
        in_specs=[pl.BlockSpec((tm, MEM_W), lambda i: (i, q_blk)), pl.BlockSpec((n_mem, MEM_W), lambda i: (0, 2 * layer)),
                  pl.BlockSpec((n_mem, MEM_W), lambda i: (0, 2 * layer + 1))],
        out_specs=pl.BlockSpec((tm, MEM_W), lambda i: (i, 0)),
        out_shape=jax.ShapeDtypeStruct((s, MEM_W), BF16), compiler_params=_params("arbitrary"),
    )(proj, mem_kv, mem_kv)


def mem_bwd(proj, q_blk, mem_kv, layer, dy, dy_blk, name):
    s = proj.shape[0]
    n_mem = mem_kv.shape[0]
    tm = _tile(s, 512, 16)

    def body(q_ref, k_ref, v_ref, dy_ref, dq_ref, dk_ref, dv_ref):
        @pl.when(pl.program_id(0) == 0)
        def _():
            dk_ref[...] = jnp.zeros_like(dk_ref)
            dv_ref[...] = jnp.zeros_like(dv_ref)

        q = q_ref[...].astype(BF16)
        k, v = k_ref[...], v_ref[...]
        dy = dy_ref[...]
        dq = jnp.zeros((tm, MEM_W), F32)
        dk = jnp.zeros((n_mem, MEM_W), F32)
        dv = jnp.zeros((n_mem, MEM_W), F32)
        for hm in _head_masks(tm, MEM_W, MEM_HEADS):
            qh = _keep(hm, q)
            dyh = _keep(hm, dy)
            p = _mem_probs(qh, k)
            dp = _dot(dyh, v, NT)
            dv = dv + _dot(p.astype(BF16), dyh, TN)
            ds = (p * (dp - jnp.sum(dp * p, axis=-1, keepdims=True)) * QK_SCALE).astype(BF16)
            dq = dq + jnp.where(hm, _dot(ds, k), 0.0)
            dk = dk + _dot(ds, qh, TN)
        dq_ref[...] = dq.astype(BF16)
        dk_ref[...] += dk
        dv_ref[...] += dv

    const = lambda i: (0, 0)
    return pl.pallas_call(
        body, name=name, grid=(s // tm,),
        in_specs=[pl.BlockSpec((tm, MEM_W), lambda i: (i, q_blk)), pl.BlockSpec((n_mem, MEM_W), lambda i: (0, 2 * layer)),
                  pl.BlockSpec((n_mem, MEM_W), lambda i: (0, 2 * layer + 1)), pl.BlockSpec((tm, MEM_W), lambda i: (i, dy_blk))],
        out_specs=[pl.BlockSpec((tm, MEM_W), lambda i: (i, 0)), pl.BlockSpec((n_mem, MEM_W), const),
                   pl.BlockSpec((n_mem, MEM_W), const)],
        out_shape=[jax.ShapeDtypeStruct((s, MEM_W), BF16), jax.ShapeDtypeStruct((n_mem, MEM_W), F32),
                   jax.ShapeDtypeStruct((n_mem, MEM_W), F32)],
        compiler_params=_params("arbitrary"),
    )(proj, mem_kv, mem_kv, dy)


SB_KEYS = 512
SB_SUB = SB_KEYS // SB_BLK
SB_QROWS = 512
SB_QB = SB_QROWS // SB_BLK
SB_CHAINS = 2 * SB_QB
SB_DEAD = -110.0


def _split(xf):
    hi = xf.astype(BF16)
    return hi, (xf - hi.astype(F32)).astype(BF16)


def _sb_consts():
    row = lax.bitwise_and(lax.broadcasted_iota(jnp.int32, (2 * SB_BLK, 2 * SB_BLK), 0), SB_BLK - 1)
    col = lax.broadcasted_iota(jnp.int32, (2 * SB_BLK, 2 * SB_BLK), 1)
    ones = col >= SB_BLK
    after2 = jnp.where(ones | (row > col), -1.0, 0.0).astype(BF16)
    from2 = jnp.where(ones | (row >= col), 1.0, 0.0).astype(BF16)
    r = lax.broadcasted_iota(jnp.int32, (SB_BLK, SB_BLK), 0)
    c = lax.broadcasted_iota(jnp.int32, (SB_BLK, SB_BLK), 1)
    return after2, from2, c - r, [c < HEAD_DIM, c >= HEAD_DIM]


def _suffix(xf, tri2):
    hi, lo = _split(xf)
    return _dot(jnp.concatenate([hi, lo], axis=1), tri2)


def _sb_logs(z, mask):
    softplus = jnp.maximum(z, 0.0) + jnp.log(1.0 + jnp.exp(-jnp.abs(z)))
    log_beta = z - softplus
    if mask is not None:
        softplus = jnp.where(mask, softplus, 0.0)
    return softplus, log_beta


def _sb_queries(q_ref, heads):
    q = q_ref[...].astype(F32) * QK_SCALE
    return [jnp.where(hm, q[r * SB_BLK:(r + 1) * SB_BLK], 0.0).astype(BF16) for r in range(SB_QB) for hm in heads]


def _sb_walk(i, block, state):
    places = SB_SUB // SB_QB
    assert places in (1, 2)
    own = lax.shift_right_logical(i * SB_QB, SB_SUB.bit_length() - 1)
    firsts = [[(v * SB_QB + r) * SB_BLK for r in range(SB_QB) for _ in range(2)] for v in range(places)]
    if places == 1:
        state = block(own, state, firsts[0])
    else:
        state = lax.cond(lax.bitwise_and(i, 1) == 0, lambda st: block(own, st, firsts[0]),
                         lambda st: block(own, st, firsts[1]), state)

    def live(carry):
        j, st = carry
        most = st[0][0]
        for run in st[0][1:]:
            most = jnp.maximum(most, run)
        return (j >= 0) & (jnp.max(most) > SB_DEAD)

    return lax.while_loop(live, lambda carry: (carry[0] - 1, block(carry[0], carry[1], None)), (own - 1, state))[1]


def _sb_tiles(first):
    out = []
    for c in reversed(range(SB_SUB)):
        for n in range(SB_CHAINS):
            if first is None or c * SB_BLK < first[n]:
                out.append((c, n, "before"))
            elif c * SB_BLK == first[n]:
                out.append((c, n, "diagonal"))
    return out


def _sb_heads_apart(stacked, heads, r):
    return jnp.where(heads[0], stacked[2 * r * SB_BLK:(2 * r + 1) * SB_BLK],
                     stacked[(2 * r + 1) * SB_BLK:(2 * r + 2) * SB_BLK])


def sb_fwd(proj, kv, name):
    s = proj.shape[0]
    assert s % SB_KEYS == 0 and SB_KEYS % SB_QROWS == 0

    def body(q_ref, k_ref, v_ref, o_ref):
        after2, _, col_minus_row, heads = _sb_consts()
        q_all = jnp.concatenate(_sb_queries(q_ref, heads), axis=0)
        key_before_query = col_minus_row < 0

        def block(j, state, first):
            runs, acc = list(state[0]), state[1]
            rows = pl.ds(pl.multiple_of(j * SB_KEYS, SB_KEYS), SB_KEYS)
            kb, vb = k_ref[rows, :], v_ref[rows, :]
            z = _dot(q_all, kb, NT)
            pend = {}
            parts = [[jnp.zeros((SB_BLK, SB_BLK), BF16)] * SB_SUB for _ in range(SB_CHAINS)]
            for c, n, where in _sb_tiles(first):
                mask = key_before_query if where == "diagonal" else None
                softplus, lb = _sb_logs(z[n * SB_BLK:(n + 1) * SB_BLK, c * SB_BLK:(c + 1) * SB_BLK], mask)
                pend[c, n] = (lb, _suffix(softplus, after2), mask)
            for c, n, _ in _sb_tiles(first):
                lb, r, mask = pend.pop((c, n))
                a = jnp.exp(lb + r[:, :SB_BLK] + runs[n])
                if mask is not None:
                    a = jnp.where(mask, a, 0.0)
                parts[n][c] = a.astype(BF16)
                runs[n] = runs[n] + r[:, SB_BLK:]
            a_all = jnp.concatenate([jnp.concatenate(p, axis=1) for p in parts], axis=0)
            return tuple(runs), acc + _dot(a_all, vb)

        zero = jnp.zeros((SB_BLK, LANES), F32)
        state = _sb_walk(pl.program_id(1), block, ((zero,) * SB_CHAINS, jnp.zeros((SB_CHAINS * SB_BLK, LANES), F32)))
        for r in range(SB_QB):
            o_ref[r * SB_BLK:(r + 1) * SB_BLK, :] = _sb_heads_apart(state[1], heads, r)

    pairs = SB_W // LANES
    return pl.pallas_call(
        body, name=name, grid=(pairs, s // SB_QROWS),
        in_specs=[pl.BlockSpec((SB_QROWS, LANES), lambda p, i: (i, p)), pl.BlockSpec((s, LANES), lambda p, i: (0, p)),
                  pl.BlockSpec((s, LANES), lambda p, i: (0, pairs + p))],
        out_specs=pl.BlockSpec((SB_QROWS, LANES), lambda p, i: (i, p)),
        out_shape=jax.ShapeDtypeStruct((s, SB_W), F32),
        compiler_params=_params("arbitrary", "arbitrary"),
    )(proj, kv, kv)


def sb_bwd(proj, kv, out, dy, name):
    s = proj.shape[0]

    def body(q_ref, k_ref, v_ref, o_ref, do_ref, dq_ref, dk_ref, dv_ref):
        i = pl.program_id(1)

        @pl.when(i == 0)
        def _():
            dk_ref[...] = jnp.zeros_like(dk_ref)
            dv_ref[...] = jnp.zeros_like(dv_ref)

        after2, from2, col_minus_row, heads = _sb_consts()
        q_all = jnp.concatenate(_sb_queries(q_ref, heads), axis=0)
        key_before_query = col_minus_row < 0
        d_out = do_ref[...].astype(F32)
        prod = d_out * o_ref[...]
        dos, totals = [], []
        for r in range(SB_QB):
            rr = slice(r * SB_BLK, (r + 1) * SB_BLK)
            for hm in heads:
                dos.append(jnp.where(hm, d_out[rr], 0.0).astype(BF16))
                totals.append(jnp.broadcast_to(jnp.sum(jnp.where(hm, prod[rr], 0.0), axis=1, keepdims=True),
                                               (SB_BLK, SB_BLK)))
        do_all = jnp.concatenate(dos, axis=0)

        def block(j, state, first):
            runs, seens, dq = list(state[0]), list(state[1]), state[2]
            rows = pl.ds(pl.multiple_of(j * SB_KEYS, SB_KEYS), SB_KEYS)
            kb, vb = k_ref[rows, :], v_ref[rows, :]
            z = _dot(q_all, kb, NT)
            da = _dot(do_all, vb, NT)
            pend, pend2 = {}, {}
            a_parts = [[jnp.zeros((SB_BLK, SB_BLK), BF16)] * SB_SUB for _ in range(SB_CHAINS)]
            dz_parts = [[jnp.zeros((SB_BLK, SB_BLK), BF16)] * SB_SUB for _ in range(SB_CHAINS)]
            for c, n, where in _sb_tiles(first):
                mask = key_before_query if where == "diagonal" else None
                softplus, lb = _sb_logs(z[n * SB_BLK:(n + 1) * SB_BLK, c * SB_BLK:(c + 1) * SB_BLK], mask)
                pend[c, n] = (softplus, lb, _suffix(softplus, after2), mask)
            for c, n, _ in _sb_tiles(first):
                softplus, lb, r, mask = pend.pop((c, n))
                a = jnp.exp(lb + r[:, :SB_BLK] + runs[n])
                if mask is not None:
                    a = jnp.where(mask, a, 0.0)
                runs[n] = runs[n] + r[:, SB_BLK:]
                ab = a.astype(BF16)
                a_parts[n][c] = ab
                dl = ab.astype(F32) * da[n * SB_BLK:(n + 1) * SB_BLK, c * SB_BLK:(c + 1) * SB_BLK]
                pend2[c, n] = (softplus, lb, dl, _suffix(dl, from2), mask)
            for c, n, _ in _sb_tiles(first):
                softplus, lb, dl, r2, mask = pend2.pop((c, n))
                d_lom = totals[n] - (r2[:, :SB_BLK] + seens[n])
                if mask is not None:
                    d_lom = jnp.where(mask, d_lom, 0.0)
                seens[n] = seens[n] + r2[:, SB_BLK:]
                dz_parts[n][c] = (dl * jnp.exp(-softplus) - d_lom * jnp.exp(lb)).astype(BF16)
            a_all = jnp.concatenate([jnp.concatenate(p, axis=1) for p in a_parts], axis=0)
            dz_all = jnp.concatenate([jnp.concatenate(p, axis=1) for p in dz_parts], axis=0)
            dv_ref[rows, :] += _dot(a_all, do_all, TN)
            dk_ref[rows, :] += _dot(dz_all, q_all, TN)
            return tuple(runs), tuple(seens), dq + _dot(dz_all, kb)

        zero = jnp.zeros((SB_BLK, LANES), F32)
        state = _sb_walk(i, block, ((zero,) * SB_CHAINS, (zero,) * SB_CHAINS,
                                    jnp.zeros((SB_CHAINS * SB_BLK, LANES), F32)))
        for r in range(SB_QB):
            dq_ref[r * SB_BLK:(r + 1) * SB_BLK, :] = (_sb_heads_apart(state[2], heads, r) * QK_SCALE).astype(BF16)

    pairs = SB_W // LANES
    blk = lambda p, i: (i, p)
    col = lambda p, i: (0, p)
    return pl.pallas_call(
        body, name=name, grid=(pairs, s // SB_QROWS),
        in_specs=[pl.BlockSpec((SB_QROWS, LANES), blk), pl.BlockSpec((s, LANES), col),
                  pl.BlockSpec((s, LANES), lambda p, i: (0, pairs + p)), pl.BlockSpec((SB_QROWS, LANES), blk),
                  pl.BlockSpec((SB_QROWS, LANES), blk)],
        out_specs=[pl.BlockSpec((SB_QROWS, LANES), blk), pl.BlockSpec((s, LANES), col), pl.BlockSpec((s, LANES), col)],
        out_shape=[jax.ShapeDtypeStruct((s, SB_W), BF16), jax.ShapeDtypeStruct((s, SB_W), F32),
                   jax.ShapeDtypeStruct((s, SB_W), F32)],
        compiler_params=_params("arbitrary", "arbitrary"),
    )(proj, kv, kv, out, dy)


def final_loss(x, g, target, name):
    s, d = x.shape
    tm = _tile(s, 256, 8)

    def body(x_ref, g_ref, t_ref, loss_ref, dx_ref, dg_ref):
        @pl.when(pl.program_id(0) == 0)
        def _():
            loss_ref[...] = jnp.zeros_like(loss_ref)
            dg_ref[...] = jnp.zeros_like(dg_ref)

        xf = x_ref[...]
        r = _rms(xf)
        xhat = xf * r
        gain = g_ref[...]
        diff = xhat * gain - t_ref[...]
        sq = jnp.sum(jnp.sum(diff * diff, axis=1, keepdims=True), axis=0, keepdims=True)
        loss_ref[...] += jnp.broadcast_to(sq, loss_ref.shape)
        dy = diff * (1.0 / d)
        dg_ref[...] += jnp.sum(dy * xhat, axis=0, keepdims=True)
        dxhat = dy * gain
        dx_ref[...] = r * (dxhat - xhat * jnp.mean(dxhat * xhat, axis=-1, keepdims=True))

    row = lambda i: (i, 0)
    const = lambda i: (0, 0)
    return pl.pallas_call(
        body, name=name, grid=(s // tm,),
        in_specs=[pl.BlockSpec((tm, d), row), pl.BlockSpec((1, d), const), pl.BlockSpec((tm, d), row)],
        out_specs=[pl.BlockSpec((8, LANES), const), pl.BlockSpec((tm, d), row), pl.BlockSpec((1, d), const)],
        out_shape=[jax.ShapeDtypeStruct((8, LANES), F32), jax.ShapeDtypeStruct((s, d), F32), jax.ShapeDtypeStruct((1, d), F32)],
        compiler_params=_params("arbitrary"),
    )(x, g, target)


def adamw(w, parts, m, v, name):
    rows, cols = w.shape
    k = parts.shape[0]
    tr = _tile(rows, 512, 16)
    c1, c2 = 1.0 - ADAM_B1 ** ADAM_STEP, 1.0 - ADAM_B2 ** ADAM_STEP

    def body(w_ref, p_ref, m_ref, v_ref, g_ref, d_ref, nm_ref, nv_ref):
        grad = p_ref[0].astype(F32)
        for s in range(1, k):
            grad = grad + p_ref[s].astype(F32)
        nm = ADAM_B1 * m_ref[...] + (1.0 - ADAM_B1) * grad
        nv = ADAM_B2 * v_ref[...] + (1.0 - ADAM_B2) * (grad * grad)
        g_ref[...] = grad
        d_ref[...] = -ADAM_LR * ((nm / c1) / (jnp.sqrt(nv / c2) + ADAM_EPS) + ADAM_WD * w_ref[...])
        nm_ref[...] = nm
        nv_ref[...] = nv

    spec = pl.BlockSpec((tr, cols), lambda i: (i, 0))
    shape = jax.ShapeDtypeStruct((rows, cols), F32)
    return pl.pallas_call(
        body, name=name, grid=(rows // tr,),
        in_specs=[spec, pl.BlockSpec((k, tr, cols), lambda i: (0, i, 0)), spec, spec],
        out_specs=[spec] * 4, out_shape=[shape] * 4,
        compiler_params=_params("arbitrary"),
    )(w, parts, m, v)


SHARDED = {"ffn1_w_gate": 2, "ffn1_w_up": 2, "ffn1_w_down": 1, "ffn2_w_gate": 2, "ffn2_w_up": 2, "ffn2_w_down": 1,
           "w_mem_kv": 1, "a_w_in": 2, "a_w_out": 1, "w_kv": 1, "b_w_in": 1, "b_w_out": 1}
SMALL = ["ffn1_norm", "mix_norm", "ffn2_norm", "mem_norm", "kv_norm", "final_norm", "a_v_norm", "a_w_spatial", "a_b_spatial"]
WEIGHTS = ["ffn1_norm", "ffn1_w_gate", "ffn1_w_up", "ffn1_w_down", "mix_norm", "ffn2_norm", "ffn2_w_gate", "ffn2_w_up",
           "ffn2_w_down", "mem_norm", "w_mem_kv", "a_w_in", "a_v_norm", "a_w_spatial", "a_b_spatial", "a_w_out", "kv_norm",
           "w_kv", "b_w_in", "b_w_out", "final_norm"]


def _all_sum(parts, name):
    flat = jnp.concatenate([p.reshape(-1) for p in parts])
    pad = (-flat.size) % (16 * LANES)
    buf = jnp.pad(flat, (0, pad)).reshape(-1, LANES)
    total = sum_leading(exchange([buf], "all", True, name)[0], F32, name + "_sum").reshape(-1)
    out, off = [], 0
    for p in parts:
        out.append(total[off:off + p.size].reshape(p.shape))
        off += p.size
    return out


def _device_index():
    return 4 * lax.axis_index("x") + 2 * lax.axis_index("y") + lax.axis_index("c")


def kernel(x, mem, ffn1_norm, ffn1_w_gate, ffn1_w_up, ffn1_w_down, mix_norm, ffn2_norm, ffn2_w_gate, ffn2_w_up, ffn2_w_down, mem_norm, w_mem_kv, a_w_in, a_v_norm, a_w_spatial, a_b_spatial, a_w_out, kv_norm, w_kv, b_w_in, b_w_out, final_norm, loss_target, m_ffn1_norm, m_ffn1_w_gate, m_ffn1_w_up, m_ffn1_w_down, m_mix_norm, m_ffn2_norm, m_ffn2_w_gate, m_ffn2_w_up, m_ffn2_w_down, m_mem_norm, m_w_mem_kv, m_a_w_in, m_a_v_norm, m_a_w_spatial, m_a_b_spatial, m_a_w_out, m_kv_norm, m_w_kv, m_b_w_in, m_b_w_out, m_final_norm, v_ffn1_norm, v_ffn1_w_gate, v_ffn1_w_up, v_ffn1_w_down, v_mix_norm, v_ffn2_norm, v_ffn2_w_gate, v_ffn2_w_up, v_ffn2_w_down, v_mem_norm, v_w_mem_kv, v_a_w_in, v_a_v_norm, v_a_w_spatial, v_a_b_spatial, v_a_w_out, v_kv_norm, v_w_kv, v_b_w_in, v_b_w_out, v_final_norm):
    weights = dict(ffn1_norm=ffn1_norm, ffn1_w_gate=ffn1_w_gate, ffn1_w_up=ffn1_w_up, ffn1_w_down=ffn1_w_down, mix_norm=mix_norm, ffn2_norm=ffn2_norm, ffn2_w_gate=ffn2_w_gate, ffn2_w_up=ffn2_w_up, ffn2_w_down=ffn2_w_down, mem_norm=mem_norm, w_mem_kv=w_mem_kv, a_w_in=a_w_in, a_v_norm=a_v_norm, a_w_spatial=a_w_spatial, a_b_spatial=a_b_spatial, a_w_out=a_w_out, kv_norm=kv_norm, w_kv=w_kv, b_w_in=b_w_in, b_w_out=b_w_out, final_norm=final_norm)
    mom1 = dict(ffn1_norm=m_ffn1_norm, ffn1_w_gate=m_ffn1_w_gate, ffn1_w_up=m_ffn1_w_up, ffn1_w_down=m_ffn1_w_down, mix_norm=m_mix_norm, ffn2_norm=m_ffn2_norm, ffn2_w_gate=m_ffn2_w_gate, ffn2_w_up=m_ffn2_w_up, ffn2_w_down=m_ffn2_w_down, mem_norm=m_mem_norm, w_mem_kv=m_w_mem_kv, a_w_in=m_a_w_in, a_v_norm=m_a_v_norm, a_w_spatial=m_a_w_spatial, a_b_spatial=m_a_b_spatial, a_w_out=m_a_w_out, kv_norm=m_kv_norm, w_kv=m_w_kv, b_w_in=m_b_w_in, b_w_out=m_b_w_out, final_norm=m_final_norm)
    mom2 = dict(ffn1_norm=v_ffn1_norm, ffn1_w_gate=v_ffn1_w_gate, ffn1_w_up=v_ffn1_w_up, ffn1_w_down=v_ffn1_w_down, mix_norm=v_mix_norm, ffn2_norm=v_ffn2_norm, ffn2_w_gate=v_ffn2_w_gate, ffn2_w_up=v_ffn2_w_up, ffn2_w_down=v_ffn2_w_down, mem_norm=v_mem_norm, w_mem_kv=v_w_mem_kv, a_w_in=v_a_w_in, a_v_norm=v_a_v_norm, a_w_spatial=v_a_w_spatial, a_b_spatial=v_a_b_spatial, a_w_out=v_a_w_out, kv_norm=v_kv_norm, w_kv=v_w_kv, b_w_in=v_b_w_in, b_w_out=v_b_w_out, final_norm=v_final_norm)

    dev = _device_index()
    xs, mem_in, target = x[0], mem[0], loss_target[0]
    d_model = xs.shape[1]
    shards = {n: weights[n] for n in SHARDED}

    def mix_keys(l):
        w_in, w_out, idx = ("a_w_in", "a_w_out", l) if l < N_A else ("b_w_in", "b_w_out", l - N_A)
        return (w_in, idx), (w_out, idx)

    def ffn_keys(ffn):
        return ([], []) if ffn is None else ([(ffn[0] + "_w_gate", ffn[1]), (ffn[0] + "_w_up", ffn[1])],
                                             [(ffn[0] + "_w_down", ffn[1])])

    def ffn_after(f, l):
        return ("ffn2", l) if f == "ffn1" else (("ffn1", l + 1) if l + 1 < DEPTH else None)

    def cut_axis(key):
        return SHARDED[key[0]] - (0 if key[1] is None else 1)

    def block(key):
        return (shards[key[0]] if key[1] is None else shards[key[0]][key[1]]).astype(BF16)

    def carrying(call, keys, same_src, source, store):
        if not keys:
            return call(None)
        result, arrived = call(Side([source(k) for k in keys], same_src))
        store.update(zip(keys, arrived))
        return result

    first_gu, first_down = ffn_keys(("ffn1", 0))
    first = first_gu + first_down + [mix_keys(0)[0], ("w_mem_kv", None)]
    landed = dict(zip(first, exchange([block(k) for k in first], "all", True, "gather_first")))
    assembled = {}

    def whole(n, l=None):
        if (n, l) not in assembled:
            pieces = [landed[n, l][d] for d in range(N_DEV)]
            if n.endswith("_w_gate"):
                pieces += [landed[n.replace("_w_gate", "_w_up"), l][d] for d in range(N_DEV)]
            assembled[n, l] = jnp.concatenate(pieces, axis=cut_axis((n, l)))
        return assembled[n, l]

    def whole_gu(f, l):
        return whole(f + "_w_gate", l)

    vn_width = a_v_norm.shape[1]
    a_v_full = _all_sum([lax.dynamic_update_slice(jnp.zeros((N_A, N_DEV * vn_width), F32), a_v_norm, (0, dev * vn_width))],
                        "gather_v_norm")[0]
    row = lambda v: v.reshape(1, -1)
    w_mem_cat = whole("w_mem_kv").transpose(1, 0, 2).reshape(d_model, -1)
    bias = [jnp.repeat(a_b_spatial[i].T, GM_P, axis=1) for i in range(N_A)]

    mem_kv, mem_h = norm_mm(mem_in, row(mem_norm), w_mem_cat, BF16, "mem_kv", emit_h=True)

    def ffn_fwd(xin, f, l):
        keys_gu, keys_down = ffn_keys(ffn_after(f, l))
        if ffn_after(f, l) == ("ffn1", N_A):
            keys_down = keys_down + [("w_kv", None)]
        gu = carrying(lambda side: norm_mm(xin, row(weights[f + "_norm"][l]), whole_gu(f, l), BF16, "ffn_gu", side=side),
                      keys_gu, True, block, landed)
        out = carrying(lambda side: swiglu_mm_res(gu, whole(f + "_w_down", l), xin, 0.5, "ffn_down", side=side),
                       keys_down, True, block, landed)
        return out, gu

    saved = []
    kv = x_kv = None
    cur = xs
    for l in range(DEPTH):
        st = {"x0": cur}
        if l == N_A:
            x_kv = cur
            kv = norm_mm(cur, row(kv_norm), whole("w_kv"), BF16, "kv_proj")
        st["x1"], st["gu1"] = ffn_fwd(cur, "ffn1", l)
        key_in, key_out = mix_keys(l)
        proj = carrying(lambda side: norm_mm(st["x1"], row(mix_norm[l]), whole(*key_in), F32 if l < N_A else BF16,
                                             "a_proj" if l < N_A else "b_proj", side=side), [key_out], True, block, landed)
        if l < N_A:
            y_tok = gmlp_fwd(proj, row(a_v_full[l]), a_w_spatial[l], bias[l], "gmlp_fwd")
            y_mem = mem_fwd(proj, 2 * GM_W // MEM_W, mem_kv, l, "mem_fwd_a")
        else:
            st["sb_out"] = sb_fwd(proj, kv, "sb_fwd")
            y_tok = st["sb_out"].astype(BF16)
            y_mem = mem_fwd(proj, SB_W // MEM_W, mem_kv, l, "mem_fwd_b")
        st["proj"] = proj
        st["y"] = jnp.concatenate([y_tok, y_mem], axis=1)
        st["x2"] = carrying(lambda side: mm_res(st["y"], whole(*key_out), st["x1"], 1.0, "mix_out", side=side),
                            [mix_keys(l + 1)[0]] if l + 1 < DEPTH else [], True, block, landed)
        cur, st["gu2"] = ffn_fwd(st["x2"], "ffn2", l)
        saved.append(st)

    loss_blk, dx, d_final = final_loss(cur, row(final_norm), target, "final_loss")
    loss = lax.psum(loss_blk[0, 0] * (0.5 / d_model), AXES)

    grads = {n: [None] * weights[n].shape[0] for n in WEIGHTS if weights[n].ndim >= 2 and n not in ("w_kv",)}
    grads["final_norm"] = d_final.reshape(-1)
    d_mem_kv = [None] * DEPTH
    d_kv = []

    summed = {}

    def pieces(key):
        g = (grads[key[0]] if key[1] is None else grads[key[0]][key[1]]).astype(BF16)
        axis = cut_axis(key)
        cut = g.reshape(g.shape[:axis] + (N_DEV, g.shape[axis] // N_DEV) + g.shape[axis + 1:])
        return jnp.moveaxis(cut, axis, 0)

    def ffn_bwd(dx, xin, gu, f, l):
        keys_gu, keys_down = ffn_keys(ffn_after(f, l))
        if f == "ffn2" and l + 1 < DEPTH:
            keys_down = keys_down + list(mix_keys(l + 1)) + ([("w_kv", None)] if l + 1 == N_A else [])
        d_gu = carrying(lambda side: mm_nt_swiglu_bwd(dx, whole(f + "_w_down", l), gu, 0.5, "ffn_dgu", side=side),
                        keys_gu, False, pieces, summed)
        dx_new, d_gain, h = carrying(
            lambda side: mm_nt_normbwd(d_gu, whole_gu(f, l), xin, row(weights[f + "_norm"][l]), dx, "ffn_dx", side=side),
            keys_down, False, pieces, summed)
        d_wgu = mm_tn(h, d_gu, 1.0, "ffn_dwgu", tb_target=1408)
        d_wdown = swiglu_mm_tn(gu, dx, 0.5, "ffn_dwdown")
        half = d_wgu.shape[1] // 2
        grads[f + "_w_gate"][l], grads[f + "_w_up"][l] = d_wgu[:, :half], d_wgu[:, half:]
        grads[f + "_w_down"][l] = d_wdown
        grads[f + "_norm"][l] = d_gain.reshape(-1)
        return dx_new

    for l in reversed(range(DEPTH)):
        st = saved[l]
        dx = ffn_bwd(dx, st["x2"], st["gu2"], "ffn2", l)
        proj = st["proj"]
        (key_in, idx), (key_out, _) = mix_keys(l)
        w_in, w_out = whole(key_in, idx), whole(key_out, idx)
        dy = mm_nt(dx, w_out, 1.0, "mix_dy")
        grads[key_out][idx] = mm_tn(st["y"], dx, 1.0, "mix_dwout", tb_target=1024)
        if l < N_A:
            d_uv, d_ws, d_bs, d_vgain = gmlp_bwd(proj, dy, row(a_v_full[l]), a_w_spatial[l], bias[l], "gmlp_bwd")
            grads["a_w_spatial"][l], grads["a_b_spatial"][l], grads["a_v_norm"][l] = d_ws, d_bs[:, :, 0], d_vgain.reshape(-1)
            d_q, d_k, d_v = mem_bwd(proj, 2 * GM_W // MEM_W, mem_kv, l, dy, GM_W // MEM_W, "mem_bwd_a")
            d_proj = jnp.concatenate([d_uv, d_q], axis=1)
        else:
            d_qsb, d_ksb, d_vsb = sb_bwd(proj, kv, st["sb_out"], dy, "sb_bwd")
            d_kv.append(jnp.concatenate([d_ksb, d_vsb], axis=1))
            d_q, d_k, d_v = mem_bwd(proj, SB_W // MEM_W, mem_kv, l, dy, SB_W // MEM_W, "mem_bwd_b")
            d_proj = jnp.concatenate([d_qsb, d_q], axis=1)
        d_mem_kv[l] = jnp.concatenate([d_k, d_v], axis=1)
        dx, d_gain, h = mm_nt_normbwd(d_proj, w_in, st["x1"], row(mix_norm[l]), dx, "mix_dx")
        grads["mix_norm"][l] = d_gain.reshape(-1)
        grads[key_in][idx] = mm_tn(h, d_proj, 1.0, "mix_dwin")
        dx = ffn_bwd(dx, st["x0"], st["gu1"], "ffn1", l)
        if l == N_A:
            d_kv_b = sum_leading(jnp.stack(d_kv), BF16, "kv_dsum")
            dx, d_gain, h = mm_nt_normbwd(d_kv_b, whole("w_kv"), x_kv, row(kv_norm), dx, "kv_dx")
            grads["kv_norm"] = d_gain.reshape(-1)
            grads["w_kv"] = mm_tn(h, d_kv_b, 1.0, "kv_dw")

    d_mem_all = jnp.concatenate(d_mem_kv, axis=1).astype(BF16)
    _, d_gain, _ = mm_nt_normbwd(d_mem_all, w_mem_cat, mem_in, row(mem_norm), None, "mem_dnorm")
    grads["mem_norm"] = d_gain.reshape(-1)
    d_wmem = mm_tn(mem_h, d_mem_all, 1.0, "mem_dw")
    grads["w_mem_kv"] = d_wmem.reshape(d_model, DEPTH, -1).transpose(1, 0, 2)

    last = first + [mix_keys(0)[1]]
    summed.update(zip(last, exchange([pieces(k) for k in last], "all", False, "scatter_last")))
    parts = {n: summed[n, None] if (n, None) in summed else
             jnp.stack([summed[n, i] for i in range(weights[n].shape[0])], axis=1) for n in SHARDED}
    grads = {n: (jnp.stack(g) if isinstance(g, list) else g) for n, g in grads.items()}
    for n, g in zip(SMALL, _all_sum([grads[n] for n in SMALL], "sum_small")):
        parts[n] = g[None]
    parts["a_v_norm"] = lax.dynamic_slice(parts["a_v_norm"], (0, 0, dev * vn_width), (1,) + a_v_norm.shape)

    reduced, deltas, new_m, new_v = {}, {}, {}, {}
    for n in WEIGHTS:
        w = weights[n]
        view = (lambda a: a.reshape(-1, a.shape[-1]))
        res = adamw(view(w), parts[n].reshape(parts[n].shape[0], -1, w.shape[-1]), view(mom1[n]), view(mom2[n]), "adamw")
        reduced[n], deltas[n], new_m[n], new_v[n] = [r.reshape(w.shape) for r in res]

    return (loss, dx[None], *[reduced[n] for n in WEIGHTS], *[deltas[n] for n in WEIGHTS],
            *[new_m[n] for n in WEIGHTS], *[new_v[n] for n in WEIGHTS])
```

```python
import functools

import jax
import jax.numpy as jnp
from jax import lax
from jax.experimental import pallas as pl
from jax.experimental.pallas import tpu as pltpu

F32, BF16 = jnp.float32, jnp.bfloat16
MESH_ID = pl.DeviceIdType.MESH
AXES = ("x", "y", "c")
N_DEV = 8

EPS = 1e-6
DEPTH, N_A = 4, 2
GM_W, GM_GROUPS, GM_P = 768, 6, 128
MEM_W, MEM_HEADS, HEAD_DIM = 256, 4, 64
SB_W, SB_BLK = 768, 128
LANES = 128
QK_SCALE = HEAD_DIM ** -0.5
GELU_C, GELU_A = 0.7978845608028654, 0.044715

ADAM_LR, ADAM_B1, ADAM_B2, ADAM_EPS, ADAM_WD, ADAM_STEP = 0.001, 0.9, 0.999, 1e-08, 0.01, 10

VMEM_LIMIT = 56 * 1024 * 1024
PACK_COLS = 512

NT = (((1,), (1,)), ((), ()))
TN = (((0,), (0,)), ((), ()))


def _params(*sem):
    return pltpu.CompilerParams(dimension_semantics=sem, vmem_limit_bytes=VMEM_LIMIT)


def _tile(n, target, mult=LANES):
    best = None
    for t in range(mult, min(n, target) + 1, mult):
        if n % t == 0:
            best = t
    return best if best is not None else n


def _dot(a, b, dims=None):
    if dims is None:
        return jnp.dot(a, b, preferred_element_type=F32)
    return lax.dot_general(a, b, dims, preferred_element_type=F32)


def exchange(srcs, group, same_src, name, split=False):
    size = {"pair": 2, "quad": 4, "all": 8}[group]
    n = len(srcs)
    chunk_shapes = [tuple(s.shape) if same_src else tuple(s.shape[1:]) for s in srcs]
    pieces = [cs[0] if split else 1 for cs in chunk_shapes]
    n_dma = sum(pieces)

    def body(*refs):
        src_refs, out_refs = refs[:n], refs[n:2 * n]
        send_sems, recv_sems, local_sems = refs[2 * n:]
        x, y, c = lax.axis_index("x"), lax.axis_index("y"), lax.axis_index("c")
        if group == "pair":
            me, dev = c, lambda p: (x, y, p)
        elif group == "quad":
            me, dev = 2 * x + y, lambda p: (p // 2, p % 2, c)
        else:
            me, dev = 4 * x + 2 * y + c, lambda p: (p // 4, (p // 2) % 2, p % 2)

        def chunk(t, idx):
            return src_refs[t] if same_src else src_refs[t].at[idx]

        def copies(k, idx, slot, peer):
            out, w = [], k * n_dma
            for t in range(n):
                src, dst = chunk(t, idx), out_refs[t].at[slot]
                for s_ref, d_ref in ([(src.at[u], dst.at[u]) for u in range(pieces[t])] if split else [(src, dst)]):
                    out.append(pltpu.make_async_remote_copy(
                        src_ref=s_ref, dst_ref=d_ref, send_sem=send_sems.at[w], recv_sem=recv_sems.at[w],
                        device_id=dev(peer), device_id_type=MESH_ID))
                    w += 1
            return out

        local = [pltpu.make_async_copy(chunk(t, me), out_refs[t].at[me], local_sems.at[t]) for t in range(n)]
        for cp in local:
            cp.start()
        sends = []
        for k in range(1, size):
            peer = (me + k) % size
            sends += copies(k, peer, me, peer)
        for cp in sends:
            cp.start()
        for k in range(1, size):
            sender = (me + size - k) % size
            for cp in copies(k, me, sender, sender):
                cp.wait_recv()
        for cp in sends:
            cp.wait_send()
        for cp in local:
            cp.wait()

    hbm = pl.BlockSpec(memory_space=pltpu.HBM)
    return pl.pallas_call(
        body, name=name,
        out_shape=[jax.ShapeDtypeStruct((size,) + cs, s.dtype) for cs, s in zip(chunk_shapes, srcs)],
        in_specs=[hbm] * n, out_specs=[hbm] * n,
        scratch_shapes=[pltpu.SemaphoreType.DMA((size * n_dma,)), pltpu.SemaphoreType.DMA((size * n_dma,)),
                        pltpu.SemaphoreType.DMA((n,))],
    )(*srcs)


class Side:
    def __init__(self, srcs, same_src):
        self.srcs, self.same_src, self.n = list(srcs), same_src, len(srcs)
        self.chunk_shapes = [tuple(s.shape) if same_src else tuple(s.shape[1:]) for s in srcs]

    def out_shapes(self):
        return [jax.ShapeDtypeStruct((N_DEV,) + cs, s.dtype) for cs, s in zip(self.chunk_shapes, self.srcs)]

    def scratch(self):
        return [pltpu.SemaphoreType.DMA((N_DEV * self.n,)), pltpu.SemaphoreType.DMA((N_DEV * self.n,))]

    def _copies(self, src_refs, land_refs, send_sems, recv_sems, outgoing):
        me = 4 * lax.axis_index("x") + 2 * lax.axis_index("y") + lax.axis_index("c")
        out = []
        for k in range(1, N_DEV):
            peer = (me + k) % N_DEV if outgoing else (me + N_DEV - k) % N_DEV
            for t in range(self.n):
                src = src_refs[t] if self.same_src else src_refs[t].at[peer if outgoing else me]
                out.append(pltpu.make_async_remote_copy(
                    src_ref=src, dst_ref=land_refs[t].at[me if outgoing else peer],
                    send_sem=send_sems.at[k * self.n + t], recv_sem=recv_sems.at[k * self.n + t],
                    device_id=(peer // 4, (peer // 2) % 2, peer % 2), device_id_type=MESH_ID))
        return out

    def _own(self, src_refs, land_refs, send_sems):
        me = 4 * lax.axis_index("x") + 2 * lax.axis_index("y") + lax.axis_index("c")
        return [pltpu.make_async_copy(src_refs[t] if self.same_src else src_refs[t].at[me], land_refs[t].at[me],
                                      send_sems.at[t]) for t in range(self.n)]

    def start(self, src_refs, land_refs, send_sems, recv_sems):
        for cp in self._own(src_refs, land_refs, send_sems) + self._copies(src_refs, land_refs, send_sems, recv_sems, True):
            cp.start()

    def wait(self, src_refs, land_refs, send_sems, recv_sems):
        for cp in self._copies(src_refs, land_refs, send_sems, recv_sems, False):
            cp.wait_recv()
        for cp in self._copies(src_refs, land_refs, send_sems, recv_sems, True):
            cp.wait_send()
        for cp in self._own(src_refs, land_refs, send_sems):
            cp.wait()


def _call(body, side, name, grid, in_specs, out_specs, out_shape, scratch_shapes, dims, args):
    if side is None:
        res = pl.pallas_call(body, name=name, grid=grid, in_specs=in_specs, out_specs=out_specs, out_shape=out_shape,
                             scratch_shapes=scratch_shapes, compiler_params=_params(*dims))(*args)
        return list(res), []
    n_in, n_out, n_scr, ns = len(in_specs), len(out_specs), len(scratch_shapes), side.n

    def wrapped(*refs):
        ins, srcs = refs[:n_in], refs[n_in:n_in + ns]
        outs, lands = refs[n_in + ns:n_in + ns + n_out], refs[n_in + ns + n_out:n_in + 2 * ns + n_out]
        scratch, (send_sems, recv_sems) = refs[n_in + 2 * ns + n_out:n_in + 2 * ns + n_out + n_scr], refs[-2:]
        first, last = None, None
        for axis, steps in enumerate(grid):
            i = pl.program_id(axis)
            first = (i == 0) if first is None else first & (i == 0)
            last = (i == steps - 1) if last is None else last & (i == steps - 1)

        @pl.when(first)
        def _():
            side.start(srcs, lands, send_sems, recv_sems)

        body(*ins, *outs, *scratch)

        @pl.when(last)
        def _():
            side.wait(srcs, lands, send_sems, recv_sems)

    hbm = pl.BlockSpec(memory_space=pltpu.HBM)
    res = pl.pallas_call(
        wrapped, name=name, grid=grid, in_specs=list(in_specs) + [hbm] * ns, out_specs=list(out_specs) + [hbm] * ns,
        out_shape=list(out_shape) + side.out_shapes(), scratch_shapes=list(scratch_shapes) + side.scratch(),
        compiler_params=_params(*dims))(*args, *side.srcs)
    return list(res[:n_out]), list(res[n_out:])


def sum_leading(parts, out_dtype, name):
    k, rows, cols = parts.shape
    tr = _tile(rows, 512, 16)

    def body(p_ref, o_ref):
        acc = p_ref[0].astype(F32)
        for s in range(1, k):
            acc = acc + p_ref[s].astype(F32)
        o_ref[...] = acc.astype(o_ref.dtype)

    return pl.pallas_call(
        body, name=name, grid=(rows // tr,),
        in_specs=[pl.BlockSpec((k, tr, cols), lambda i: (0, i, 0))],
        out_specs=pl.BlockSpec((tr, cols), lambda i: (i, 0)),
        out_shape=jax.ShapeDtypeStruct((rows, cols), out_dtype),
        compiler_params=_params("arbitrary"),
    )(parts)


def _rms(xf):
    return lax.rsqrt(jnp.mean(xf * xf, axis=-1, keepdims=True) + EPS)


def norm_mm(x, g, w, out_dtype, name, emit_h=False, side=None):
    m, d = x.shape
    n = w.shape[1]
    tm, tn = _tile(m, 1024, 8), _tile(n, 1408)

    def body(x_ref, g_ref, w_ref, o_ref, *rest):
        h_ref = rest[-1]

        @pl.when(pl.program_id(1) == 0)
        def _():
            xf = x_ref[...]
            hb = ((xf * _rms(xf)) * g_ref[...]).astype(BF16)
            h_ref[...] = hb
            if emit_h:
                rest[0][...] = hb

        o_ref[...] = _dot(h_ref[...], w_ref[...]).astype(o_ref.dtype)

    out_shape = [jax.ShapeDtypeStruct((m, n), out_dtype)]
    out_specs = [pl.BlockSpec((tm, tn), lambda i, j: (i, j))]
    if emit_h:
        out_shape.append(jax.ShapeDtypeStruct((m, d), BF16))
        out_specs.append(pl.BlockSpec((tm, d), lambda i, j: (i, 0)))
    res, landed = _call(
        body, side, name, (m // tm, n // tn),
        [pl.BlockSpec((tm, d), lambda i, j: (i, 0)), pl.BlockSpec((1, d), lambda i, j: (0, 0)),
         pl.BlockSpec((d, tn), lambda i, j: (0, j))],
        out_specs, out_shape, [pltpu.VMEM((tm, d), BF16)], ("arbitrary", "arbitrary"), (x, g, w))
    out = res if emit_h else res[0]
    return out if side is None else (out, landed)


def mm_res(a, w, res, alpha, name, side=None):
    m, k = a.shape
    n = w.shape[1]
    tm, tn = _tile(m, 1024, 8), _tile(n, 1024)

    def body(a_ref, w_ref, r_ref, o_ref):
        o_ref[...] = r_ref[...] + alpha * _dot(a_ref[...], w_ref[...])

    out, landed = _call(
        body, side, name, (m // tm, n // tn),
        [pl.BlockSpec((tm, k), lambda i, j: (i, 0)), pl.BlockSpec((k, tn), lambda i, j: (0, j)),
         pl.BlockSpec((tm, tn), lambda i, j: (i, j))],
        [pl.BlockSpec((tm, tn), lambda i, j: (i, j))], [jax.ShapeDtypeStruct((m, n), F32)], [],
        ("arbitrary", "arbitrary"), (a, w, res))
    return out[0] if side is None else (out[0], landed)


def mm_nt(x, w, alpha, name):
    m, d = x.shape
    n = w.shape[0]
    tm, tn = _tile(m, 1024, 8), _tile(n, 1408)

    def body(x_ref, w_ref, o_ref, xb_ref):
        @pl.when(pl.program_id(1) == 0)
        def _():
            xb_ref[...] = x_ref[...].astype(BF16)

        o_ref[...] = (alpha * _dot(xb_ref[...], w_ref[...], NT)).astype(o_ref.dtype)

    return pl.pallas_call(
        body, name=name, grid=(m // tm, n // tn),
        in_specs=[pl.BlockSpec((tm, d), lambda i, j: (i, 0)), pl.BlockSpec((tn, d), lambda i, j: (j, 0))],
        out_specs=pl.BlockSpec((tm, tn), lambda i, j: (i, j)),
        out_shape=jax.ShapeDtypeStruct((m, n), BF16),
        scratch_shapes=[pltpu.VMEM((tm, d), BF16)],
        compiler_params=_params("arbitrary", "arbitrary"),
    )(x, w)


def mm_tn(a, b, alpha, name, ta_target=1024, tb_target=512, side=None):
    s, ka = a.shape
    nb = b.shape[1]
    ta, tb, ts = _tile(ka, ta_target), _tile(nb, tb_target), _tile(s, 1024, 16)
    steps = s // ts

    def body(a_ref, b_ref, o_ref, acc_ref):
        t = pl.program_id(2)

        @pl.when(t == 0)
        def _():
            acc_ref[...] = jnp.zeros_like(acc_ref)

        acc_ref[...] += _dot(a_ref[...].astype(BF16), b_ref[...].astype(BF16), TN)

        @pl.when(t == steps - 1)
        def _():
            o_ref[...] = alpha * acc_ref[...]

    res, landed = _call(
        body, side, name, (ka // ta, nb // tb, steps),
        [pl.BlockSpec((ts, ta), lambda i, j, t: (t, i)), pl.BlockSpec((ts, tb), lambda i, j, t: (t, j))],
        [pl.BlockSpec((ta, tb), lambda i, j, t: (i, j))], [jax.ShapeDtypeStruct((ka, nb), F32)],
        [pltpu.VMEM((ta, tb), F32)], ("arbitrary", "arbitrary", "arbitrary"), (a, b))
    return res[0] if side is None else (res[0], landed)


def mm_nt_normbwd(dy, w, x, g, res, name, side=None):
    m, n = dy.shape
    d = w.shape[0]
    tm, tk = _tile(m, 1024, 8), _tile(n, 1408)
    steps = n // tk
    has_res = res is not None

    def body(*refs):
        if has_res:
            dy_ref, w_ref, x_ref, g_ref, r_ref, dx_ref, dg_ref, h_ref, acc_ref = refs
        else:
            dy_ref, w_ref, x_ref, g_ref, dx_ref, dg_ref, h_ref, acc_ref = refs
        i, t = pl.program_id(0), pl.program_id(1)

        @pl.when(t == 0)
        def _():
            acc_ref[...] = jnp.zeros_like(acc_ref)

        @pl.when((t == 0) & (i == 0))
        def _():
            dg_ref[...] = jnp.zeros_like(dg_ref)

        acc_ref[...] += _dot(dy_ref[...], w_ref[...], NT)

        @pl.when(t == steps - 1)
        def _():
            xf = x_ref[...]
            r = _rms(xf)
            xhat = xf * r
            dh = acc_ref[...]
            gain = g_ref[...]
            dg_ref[...] += jnp.sum(dh * xhat, axis=0, keepdims=True)
            dxhat = dh * gain
            dx = r * (dxhat - xhat * jnp.mean(dxhat * xhat, axis=-1, keepdims=True))
            dx_ref[...] = (r_ref[...] + dx) if has_res else dx
            h_ref[...] = (xhat * gain).astype(BF16)

    row = lambda i, t: (i, 0)
    in_specs = [pl.BlockSpec((tm, tk), lambda i, t: (i, t)), pl.BlockSpec((d, tk), lambda i, t: (0, t)),
                pl.BlockSpec((tm, d), row), pl.BlockSpec((1, d), lambda i, t: (0, 0))]
    args = [dy, w, x, g]
    if has_res:
        in_specs.append(pl.BlockSpec((tm, d), row))
        args.append(res)
    res, landed = _call(
        body, side, name, (m // tm, steps), in_specs,
        [pl.BlockSpec((tm, d), row), pl.BlockSpec((1, d), lambda i, t: (0, 0)), pl.BlockSpec((tm, d), row)],
        [jax.ShapeDtypeStruct((m, d), F32), jax.ShapeDtypeStruct((1, d), F32), jax.ShapeDtypeStruct((m, d), BF16)],
        [pltpu.VMEM((tm, d), F32)], ("arbitrary", "arbitrary"), args)
    return res if side is None else (res, landed)


def _sigmoid(z):
    return 1.0 / (1.0 + jnp.exp(-z))


def _swiglu(gate_b, up_b):
    gate = gate_b.astype(F32)
    return (gate * _sigmoid(gate) * up_b.astype(F32)).astype(BF16)


def swiglu_mm_res(gu, w, res, alpha, name, side=None):
    m, f2 = gu.shape
    f, n = w.shape
    tm, tc = _tile(m, 256, 16), _tile(f, 256)

    def body(gu_ref, w_ref, r_ref, o_ref):
        acc = jnp.zeros((tm, n), F32)
        for c0 in range(0, f, tc):
            act = _swiglu(gu_ref[:, c0:c0 + tc], gu_ref[:, f + c0:f + c0 + tc])
            acc = acc + _dot(act, w_ref[c0:c0 + tc, :])
        o_ref[...] = r_ref[...] + alpha * acc

    out, landed = _call(
        body, side, name, (m // tm,),
        [pl.BlockSpec((tm, f2), lambda i: (i, 0)), pl.BlockSpec((f, n), lambda i: (0, 0)),
         pl.BlockSpec((tm, n), lambda i: (i, 0))],
        [pl.BlockSpec((tm, n), lambda i: (i, 0))], [jax.ShapeDtypeStruct((m, n), F32)], [], ("arbitrary",), (gu, w, res))
    return out[0] if side is None else (out[0], landed)


def swiglu_mm_tn(gu, b, alpha, name, side=None):
    s, f2 = gu.shape
    f, n = f2 // 2, b.shape[1]
    ta, ts = _tile(f, 1408), _tile(s, 512, 16)
    steps, half = s // ts, f // ta

    def body(g_ref, u_ref, b_ref, o_ref, acc_ref):
        t = pl.program_id(1)

        @pl.when(t == 0)
        def _():
            acc_ref[...] = jnp.zeros_like(acc_ref)

        bb = b_ref[...].astype(BF16)
        for c0 in range(0, ta, LANES):
            acc_ref[c0:c0 + LANES, :] += _dot(_swiglu(g_ref[:, c0:c0 + LANES], u_ref[:, c0:c0 + LANES]), bb, TN)

        @pl.when(t == steps - 1)
        def _():
            o_ref[...] = alpha * acc_ref[...]

    res, landed = _call(
        body, side, name, (half, steps),
        [pl.BlockSpec((ts, ta), lambda i, t: (t, i)), pl.BlockSpec((ts, ta), lambda i, t: (t, half + i)),
         pl.BlockSpec((ts, n), lambda i, t: (t, 0))],
        [pl.BlockSpec((ta, n), lambda i, t: (i, 0))], [jax.ShapeDtypeStruct((f, n), F32)],
        [pltpu.VMEM((ta, n), F32)], ("arbitrary", "arbitrary"), (gu, gu, b))
    return res[0] if side is None else (res[0], landed)


def mm_nt_swiglu_bwd(x, w, gu, alpha, name, side=None):
    m, d = x.shape
    f = w.shape[0]
    tm, tc = _tile(m, 256, 16), _tile(f, 256)

    def body(x_ref, w_ref, gu_ref, o_ref):
        xb = x_ref[...].astype(BF16)
        for c0 in range(0, f, tc):
            d_act = alpha * _dot(xb, w_ref[c0:c0 + tc, :], NT)
            gate, up = gu_ref[:, c0:c0 + tc].astype(F32), gu_ref[:, f + c0:f + c0 + tc].astype(F32)
            sg = _sigmoid(gate)
            o_ref[:, c0:c0 + tc] = (d_act * up * (sg * (1.0 + gate * (1.0 - sg)))).astype(BF16)
            o_ref[:, f + c0:f + c0 + tc] = (d_act * (gate * sg)).astype(BF16)

    res, landed = _call(
        body, side, name, (m // tm,),
        [pl.BlockSpec((tm, d), lambda i: (i, 0)), pl.BlockSpec((f, d), lambda i: (0, 0)),
         pl.BlockSpec((tm, 2 * f), lambda i: (i, 0))],
        [pl.BlockSpec((tm, 2 * f), lambda i: (i, 0))], [jax.ShapeDtypeStruct((m, 2 * f), BF16)], [], ("arbitrary",),
        (x, w, gu))
    return res[0] if side is None else (res[0], landed)


def _gelu(x):
    return 0.5 * x * (1.0 + jnp.tanh(GELU_C * (x + GELU_A * x * x * x)))


def _gelu_grad(x):
    t = jnp.tanh(GELU_C * (x + GELU_A * x * x * x))
    return 0.5 * (1.0 + t) + 0.5 * x * (1.0 - t * t) * (GELU_C * (1.0 + 3.0 * GELU_A * x * x))


def _chunk_mask():
    row = lax.broadcasted_iota(jnp.int32, (GM_P, GM_P), 0)
    col = lax.broadcasted_iota(jnp.int32, (GM_P, GM_P), 1)
    return (col < GM_P // 2) | (row >= GM_P // 2)


def gmlp_fwd(proj, gain, w_s, bias, name):
    s, pw = proj.shape
    tm = _tile(s, 256, GM_P)

    def body(p_ref, gain_ref, w_ref, b_ref, o_ref):
        mask = _chunk_mask()
        u = _gelu(p_ref[:, :GM_W])
        v = _gelu(p_ref[:, GM_W:2 * GM_W])
        vn = ((v * _rms(v)) * gain_ref[...]).astype(BF16)
        for g in range(GM_GROUPS):
            wg = jnp.where(mask, w_ref[g], 0.0).astype(BF16)
            cols = slice(g * GM_P, (g + 1) * GM_P)
            for n in range(tm // GM_P):
                rows = slice(n * GM_P, (n + 1) * GM_P)
                mixed = _dot(wg, vn[rows, cols]) + b_ref[:, cols]
                o_ref[rows, cols] = (u[rows, cols] * mixed).astype(BF16)

    return pl.pallas_call(
        body, name=name, grid=(s // tm,),
        in_specs=[pl.BlockSpec((tm, pw), lambda i: (i, 0)), pl.BlockSpec((1, GM_W), lambda i: (0, 0)),
                  pl.BlockSpec((GM_GROUPS, GM_P, GM_P), lambda i: (0, 0, 0)), pl.BlockSpec((GM_P, GM_W), lambda i: (0, 0))],
        out_specs=pl.BlockSpec((tm, GM_W), lambda i: (i, 0)),
        out_shape=jax.ShapeDtypeStruct((s, GM_W), BF16), compiler_params=_params("arbitrary"),
    )(proj, gain, w_s, bias)


def gmlp_bwd(proj, dy, gain, w_s, bias, name):
    s, pw = proj.shape
    dw_total = dy.shape[1]
    tm = _tile(s, 256, GM_P)

    def body(p_ref, dy_ref, gain_ref, w_ref, b_ref, dp_ref, dw_ref, db_ref, dgain_ref, dvn_ref):
        @pl.when(pl.program_id(0) == 0)
        def _():
            dw_ref[...] = jnp.zeros_like(dw_ref)
            db_ref[...] = jnp.zeros_like(db_ref)
            dgain_ref[...] = jnp.zeros_like(dgain_ref)

        mask = _chunk_mask()
        pu = p_ref[:, :GM_W]
        pv = p_ref[:, GM_W:2 * GM_W]
        u = _gelu(pu)
        v = _gelu(pv)
        r = _rms(v)
        vhat = v * r
        gain = gain_ref[...]
        vn = (vhat * gain).astype(BF16)
        gu_grad = _gelu_grad(pu)
        for g in range(GM_GROUPS):
            wg = jnp.where(mask, w_ref[g], 0.0).astype(BF16)
            cols = slice(g * GM_P, (g + 1) * GM_P)
            dw_acc = jnp.zeros((GM_P, GM_P), F32)
            db_acc = jnp.zeros((GM_P, 1), F32)
            for n in range(tm // GM_P):
                rows = slice(n * GM_P, (n + 1) * GM_P)
                dyb = dy_ref[rows, cols].astype(F32)
                vnb = vn[rows, cols]
                mixed = _dot(wg, vnb) + b_ref[:, cols]
                dmixed = dyb * u[rows, cols]
                dmb = dmixed.astype(BF16)
                dp_ref[rows, cols] = (dyb * mixed * gu_grad[rows, cols]).astype(BF16)
                dw_acc = dw_acc + _dot(dmb, vnb, NT)
                db_acc = db_acc + jnp.sum(dmixed, axis=1, keepdims=True)
                dvn_ref[rows, cols] = _dot(wg, dmb, TN)
            dw_ref[g] += jnp.where(mask, dw_acc, 0.0)
            db_ref[g] += jnp.broadcast_to(db_acc, (GM_P, GM_P))
        dvn = dvn_ref[...]
        dgain_ref[...] += jnp.sum(dvn * vhat, axis=0, keepdims=True)
        dvhat = dvn * gain
        dv = r * (dvhat - vhat * jnp.mean(dvhat * vhat, axis=-1, keepdims=True))
        dp_ref[:, GM_W:] = (dv * _gelu_grad(pv)).astype(BF16)

    const3 = lambda i: (0, 0, 0)
    return pl.pallas_call(
        body, name=name, grid=(s // tm,),
        in_specs=[pl.BlockSpec((tm, pw), lambda i: (i, 0)), pl.BlockSpec((tm, dw_total), lambda i: (i, 0)),
                  pl.BlockSpec((1, GM_W), lambda i: (0, 0)), pl.BlockSpec((GM_GROUPS, GM_P, GM_P), const3),
                  pl.BlockSpec((GM_P, GM_W), lambda i: (0, 0))],
        out_specs=[pl.BlockSpec((tm, 2 * GM_W), lambda i: (i, 0)), pl.BlockSpec((GM_GROUPS, GM_P, GM_P), const3),
                   pl.BlockSpec((GM_GROUPS, GM_P, GM_P), const3), pl.BlockSpec((1, GM_W), lambda i: (0, 0))],
        out_shape=[jax.ShapeDtypeStruct((s, 2 * GM_W), BF16), jax.ShapeDtypeStruct((GM_GROUPS, GM_P, GM_P), F32),
                   jax.ShapeDtypeStruct((GM_GROUPS, GM_P, GM_P), F32), jax.ShapeDtypeStruct((1, GM_W), F32)],
        scratch_shapes=[pltpu.VMEM((tm, GM_W), F32)],
        compiler_params=_params("arbitrary"),
    )(proj, dy, gain, w_s, bias)


def _keep(mask, xb):
    return jnp.where(mask, xb.astype(F32), 0.0).astype(BF16)


def _head_masks(rows, width, heads):
    lane = lax.broadcasted_iota(jnp.int32, (rows, width), 1)
    return [(lane >= HEAD_DIM * h) & (lane < HEAD_DIM * (h + 1)) for h in range(heads)]


def _mem_probs(qh, k):
    sc = _dot(qh, k, NT) * QK_SCALE
    e = jnp.exp(sc - jnp.max(sc, axis=-1, keepdims=True))
    return e / jnp.sum(e, axis=-1, keepdims=True)


def mem_fwd(proj, q_blk, mem_kv, layer, name):
    s = proj.shape[0]
    n_mem = mem_kv.shape[0]
    tm = _tile(s, 512, 16)

    def body(q_ref, k_ref, v_ref, o_ref):
        q = q_ref[...].astype(BF16)
        k, v = k_ref[...], v_ref[...]
        out = jnp.zeros((tm, MEM_W), F32)
        for hm in _head_masks(tm, MEM_W, MEM_HEADS):
            p = _mem_probs(_keep(hm, q), k)
            out = out + jnp.where(hm, _dot(p.astype(BF16), v), 0.0)
        o_ref[...] = out.astype(BF16)

    return pl.pallas_call(
        body, name=name, grid=(s // tm,),
        in_specs=[pl.BlockSpec((tm, MEM_W), lambda i: (i, q_blk)), pl.BlockSpec((n_mem, MEM_W), lambda i: (0, 2 * layer)),
                  pl.BlockSpec((n_mem, MEM_W), lambda i: (0, 2 * layer + 1))],
        out_specs=pl.BlockSpec((tm, MEM_W), lambda i: (i, 0)),
        out_shape=jax.ShapeDtypeStruct((s, MEM_W), BF16), compiler_params=_params("arbitrary"),
    )(proj, mem_kv, mem_kv)


def mem_bwd(proj, q_blk, mem_kv, layer, dy, dy_blk, name):
    s = proj.shape[0]
    n_mem = mem_kv.shape[0]
    tm = _tile(s, 512, 16)

    def body(q_ref, k_ref, v_ref, dy_ref, dq_ref, dk_ref, dv_ref):
        @pl.when(pl.program_id(0) == 0)
        def _():
            dk_ref[...] = jnp.zeros_like(dk_ref)
            dv_ref[...] = jnp.zeros_like(dv_ref)

        q = q_ref[...].astype(BF16)
        k, v = k_ref[...], v_ref[...]
        dy = dy_ref[...]
        dq = jnp.zeros((tm, MEM_W), F32)
        dk = jnp.zeros((n_mem, MEM_W), F32)
        dv = jnp.zeros((n_mem, MEM_W), F32)
        for hm in _head_masks(tm, MEM_W, MEM_HEADS):
            qh = _keep(hm, q)
            dyh = _keep(hm, dy)
            p = _mem_probs(qh, k)
            dp = _dot(dyh, v, NT)
            dv = dv + _dot(p.astype(BF16), dyh, TN)
            ds = (p * (dp - jnp.sum(dp * p, axis=-1, keepdims=True)) * QK_SCALE).astype(BF16)
            dq = dq + jnp.where(hm, _dot(ds, k), 0.0)
            dk = dk + _dot(ds, qh, TN)
        dq_ref[...] = dq.astype(BF16)
        dk_ref[...] += dk
        dv_ref[...] += dv

    const = lambda i: (0, 0)
    return pl.pallas_call(
        body, name=name, grid=(s // tm,),
        in_specs=[pl.BlockSpec((tm, MEM_W), lambda i: (i, q_blk)), pl.BlockSpec((n_mem, MEM_W), lambda i: (0, 2 * layer)),
                  pl.BlockSpec((n_mem, MEM_W), lambda i: (0, 2 * layer + 1)), pl.BlockSpec((tm, MEM_W), lambda i: (i, dy_blk))],
        out_specs=[pl.BlockSpec((tm, MEM_W), lambda i: (i, 0)), pl.BlockSpec((n_mem, MEM_W), const),
                   pl.BlockSpec((n_mem, MEM_W), const)],
        out_shape=[jax.ShapeDtypeStruct((s, MEM_W), BF16), jax.ShapeDtypeStruct((n_mem, MEM_W), F32),
                   jax.ShapeDtypeStruct((n_mem, MEM_W), F32)],
        compiler_params=_params("arbitrary"),
    )(proj, mem_kv, mem_kv, dy)


SB_KEYS = 512
SB_SUB = SB_KEYS // SB_BLK
SB_QROWS = 512
SB_QB = SB_QROWS // SB_BLK
SB_CHAINS = 2 * SB_QB
SB_DEAD = -110.0


def _split(xf):
    hi = xf.astype(BF16)
    return hi, (xf - hi.astype(F32)).astype(BF16)


def _sb_consts():
    row = lax.bitwise_and(lax.broadcasted_iota(jnp.int32, (2 * SB_BLK, 2 * SB_BLK), 0), SB_BLK - 1)
    col = lax.broadcasted_iota(jnp.int32, (2 * SB_BLK, 2 * SB_BLK), 1)
    ones = col >= SB_BLK
    after2 = jnp.where(ones | (row > col), -1.0, 0.0).astype(BF16)
    from2 = jnp.where(ones | (row >= col), 1.0, 0.0).astype(BF16)
    r = lax.broadcasted_iota(jnp.int32, (SB_BLK, SB_BLK), 0)
    c = lax.broadcasted_iota(jnp.int32, (SB_BLK, SB_BLK), 1)
    return after2, from2, c - r, [c < HEAD_DIM, c >= HEAD_DIM]


def _suffix(xf, tri2):
    hi, lo = _split(xf)
    return _dot(jnp.concatenate([hi, lo], axis=1), tri2)


def _sb_logs(z, mask):
    softplus = jnp.maximum(z, 0.0) + jnp.log(1.0 + jnp.exp(-jnp.abs(z)))
    log_beta = z - softplus
    if mask is not None:
        softplus = jnp.where(mask, softplus, 0.0)
    return softplus, log_beta


def _sb_queries(q_ref, heads):
    q = q_ref[...].astype(F32) * QK_SCALE
    return [jnp.where(hm, q[r * SB_BLK:(r + 1) * SB_BLK], 0.0).astype(BF16) for r in range(SB_QB) for hm in heads]


def _sb_walk(i, block, states):
    assert SB_SUB == SB_QB and SB_KEYS == SB_QROWS
    states = [block(r, i, states[r], r * SB_BLK) for r in range(SB_QB)]

    def live(state):
        return jnp.max(jnp.maximum(state[0][0], state[0][1])) > SB_DEAD

    def any_live(carry):
        alive = live(carry[1][0])
        for state in carry[1][1:]:
            alive = alive | live(state)
        return (carry[0] >= 0) & alive

    def step(carry):
        j, sts = carry
        return j - 1, tuple(lax.cond(live(sts[r]), lambda st, r=r: block(r, j, st, None), lambda st: st, sts[r])
                            for r in range(SB_QB))

    return lax.while_loop(any_live, step, (i - 1, tuple(states)))[1]


def _sb_tiles(first):
    out = []
    for c in reversed(range(SB_SUB)):
        for h in range(2):
            if first is None or c * SB_BLK < first:
                out.append((c, h, "before"))
            elif c * SB_BLK == first:
                out.append((c, h, "diagonal"))
    return out


def _sb_heads_apart(stacked, heads, r):
    return jnp.where(heads[0], stacked[2 * r * SB_BLK:(2 * r + 1) * SB_BLK],
                     stacked[(2 * r + 1) * SB_BLK:(2 * r + 2) * SB_BLK])


def sb_fwd(proj, kv, name):
    s = proj.shape[0]
    assert s % SB_KEYS == 0 and SB_KEYS % SB_QROWS == 0

    def body(q_ref, k_ref, v_ref, o_ref):
        after2, _, col_minus_row, heads = _sb_consts()
        qs = _sb_queries(q_ref, heads)
        q_pairs = [jnp.concatenate(qs[2 * r:2 * r + 2], axis=0) for r in range(SB_QB)]
        key_before_query = col_minus_row < 0

        def block(r, j, state, first):
            runs, acc = list(state[0]), state[1]
            rows = pl.ds(pl.multiple_of(j * SB_KEYS, SB_KEYS), SB_KEYS)
            kb, vb = k_ref[rows, :], v_ref[rows, :]
            z = _dot(q_pairs[r], kb, NT)
            pend = {}
            parts = [[jnp.zeros((SB_BLK, SB_BLK), BF16)] * SB_SUB for _ in range(2)]
            for c, h, where in _sb_tiles(first):
                mask = key_before_query if where == "diagonal" else None
                softplus, lb = _sb_logs(z[h * SB_BLK:(h + 1) * SB_BLK, c * SB_BLK:(c + 1) * SB_BLK], mask)
                pend[c, h] = (lb, _suffix(softplus, after2), mask)
            for c, h, _ in _sb_tiles(first):
                lb, suffix, mask = pend.pop((c, h))
                a = jnp.exp(lb + suffix[:, :SB_BLK] + runs[h])
                if mask is not None:
                    a = jnp.where(mask, a, 0.0)
                parts[h][c] = a.astype(BF16)
                runs[h] = runs[h] + suffix[:, SB_BLK:]
            a_all = jnp.concatenate([jnp.concatenate(p, axis=1) for p in parts], axis=0)
            return tuple(runs), acc + _dot(a_all, vb)

        zero = jnp.zeros((SB_BLK, LANES), F32)
        states = _sb_walk(pl.program_id(1), block, [((zero, zero), jnp.zeros((2 * SB_BLK, LANES), F32))] * SB_QB)
        for r in range(SB_QB):
            o_ref[r * SB_BLK:(r + 1) * SB_BLK, :] = _sb_heads_apart(states[r][1], heads, 0)

    pairs = SB_W // LANES
    return pl.pallas_call(
        body, name=name, grid=(pairs, s // SB_QROWS),
        in_specs=[pl.BlockSpec((SB_QROWS, LANES), lambda p, i: (i, p)), pl.BlockSpec((s, LANES), lambda p, i: (0, p)),
                  pl.BlockSpec((s, LANES), lambda p, i: (0, pairs + p))],
        out_specs=pl.BlockSpec((SB_QROWS, LANES), lambda p, i: (i, p)),
        out_shape=jax.ShapeDtypeStruct((s, SB_W), F32),
        compiler_params=_params("arbitrary", "arbitrary"),
    )(proj, kv, kv)


def sb_bwd(proj, kv, out, dy, name):
    s = proj.shape[0]

    def body(q_ref, k_ref, v_ref, o_ref, do_ref, dq_ref, dk_ref, dv_ref):
        i = pl.program_id(1)

        @pl.when(i == 0)
        def _():
            dk_ref[...] = jnp.zeros_like(dk_ref)
            dv_ref[...] = jnp.zeros_like(dv_ref)

        after2, from2, col_minus_row, heads = _sb_consts()
        qs = _sb_queries(q_ref, heads)
        q_pairs = [jnp.concatenate(qs[2 * r:2 * r + 2], axis=0) for r in range(SB_QB)]
        key_before_query = col_minus_row < 0
        d_out = do_ref[...].astype(F32)
        prod = d_out * o_ref[...]
        do_pairs, totals = [], []
        for r in range(SB_QB):
            rr = slice(r * SB_BLK, (r + 1) * SB_BLK)
            do_pairs.append(jnp.concatenate([jnp.where(hm, d_out[rr], 0.0).astype(BF16) for hm in heads], axis=0))
            totals.append([jnp.broadcast_to(jnp.sum(jnp.where(hm, prod[rr], 0.0), axis=1, keepdims=True),
                                            (SB_BLK, SB_BLK)) for hm in heads])

        def block(r, j, state, first):
            runs, seens, dq = list(state[0]), list(state[1]), state[2]
            rows = pl.ds(pl.multiple_of(j * SB_KEYS, SB_KEYS), SB_KEYS)
            kb, vb = k_ref[rows, :], v_ref[rows, :]
            z = _dot(q_pairs[r], kb, NT)
            da = _dot(do_pairs[r], vb, NT)
            pend, pend2 = {}, {}
            a_parts = [[jnp.zeros((SB_BLK, SB_BLK), BF16)] * SB_SUB for _ in range(2)]
            dz_parts = [[jnp.zeros((SB_BLK, SB_BLK), BF16)] * SB_SUB for _ in range(2)]
            for c, h, where in _sb_tiles(first):
                mask = key_before_query if where == "diagonal" else None
                softplus, lb = _sb_logs(z[h * SB_BLK:(h + 1) * SB_BLK, c * SB_BLK:(c + 1) * SB_BLK], mask)
                pend[c, h] = (softplus, lb, _suffix(softplus, after2), mask)
            for c, h, _ in _sb_tiles(first):
                softplus, lb, suffix, mask = pend.pop((c, h))
                a = jnp.exp(lb + suffix[:, :SB_BLK] + runs[h])
                if mask is not None:
                    a = jnp.where(mask, a, 0.0)
                runs[h] = runs[h] + suffix[:, SB_BLK:]
                ab = a.astype(BF16)
                a_parts[h][c] = ab
                dl = ab.astype(F32) * da[h * SB_BLK:(h + 1) * SB_BLK, c * SB_BLK:(c + 1) * SB_BLK]
                pend2[c, h] = (softplus, lb, dl, _suffix(dl, from2), mask)
            for c, h, _ in _sb_tiles(first):
                softplus, lb, dl, suffix, mask = pend2.pop((c, h))
                d_lom = totals[r][h] - (suffix[:, :SB_BLK] + seens[h])
                if mask is not None:
                    d_lom = jnp.where(mask, d_lom, 0.0)
                seens[h] = seens[h] + suffix[:, SB_BLK:]
                dz_parts[h][c] = (dl * jnp.exp(-softplus) - d_lom * jnp.exp(lb)).astype(BF16)
            a_all = jnp.concatenate([jnp.concatenate(p, axis=1) for p in a_parts], axis=0)
            dz_all = jnp.concatenate([jnp.concatenate(p, axis=1) for p in dz_parts], axis=0)
            dv_ref[rows, :] += _dot(a_all, do_pairs[r], TN)
            dk_ref[rows, :] += _dot(dz_all, q_pairs[r], TN)
            return tuple(runs), tuple(seens), dq + _dot(dz_all, kb)

        zero = jnp.zeros((SB_BLK, LANES), F32)
        states = _sb_walk(i, block, [((zero, zero), (zero, zero), jnp.zeros((2 * SB_BLK, LANES), F32))] * SB_QB)
        for r in range(SB_QB):
            dq_ref[r * SB_BLK:(r + 1) * SB_BLK, :] = (_sb_heads_apart(states[r][2], heads, 0) * QK_SCALE).astype(BF16)

    pairs = SB_W // LANES
    blk = lambda p, i: (i, p)
    col = lambda p, i: (0, p)
    return pl.pallas_call(
        body, name=name, grid=(pairs, s // SB_QROWS),
        in_specs=[pl.BlockSpec((SB_QROWS, LANES), blk), pl.BlockSpec((s, LANES), col),
                  pl.BlockSpec((s, LANES), lambda p, i: (0, pairs + p)), pl.BlockSpec((SB_QROWS, LANES), blk),
                  pl.BlockSpec((SB_QROWS, LANES), blk)],
        out_specs=[pl.BlockSpec((SB_QROWS, LANES), blk), pl.BlockSpec((s, LANES), col), pl.BlockSpec((s, LANES), col)],
        out_shape=[jax.ShapeDtypeStruct((s, SB_W), BF16), jax.ShapeDtypeStruct((s, SB_W), F32),
                   jax.ShapeDtypeStruct((s, SB_W), F32)],
        compiler_params=_params("arbitrary", "arbitrary"),
    )(proj, kv, kv, out, dy)


def final_loss(x, g, target, name):
    s, d = x.shape
    tm = _tile(s, 256, 8)

    def body(x_ref, g_ref, t_ref, loss_ref, dx_ref, dg_ref):
        @pl.when(pl.program_id(0) == 0)
        def _():
            loss_ref[...] = jnp.zeros_like(loss_ref)
            dg_ref[...] = jnp.zeros_like(dg_ref)

        xf = x_ref[...]
        r = _rms(xf)
        xhat = xf * r
        gain = g_ref[...]
        diff = xhat * gain - t_ref[...]
        sq = jnp.sum(jnp.sum(diff * diff, axis=1, keepdims=True), axis=0, keepdims=True)
        loss_ref[...] += jnp.broadcast_to(sq, loss_ref.shape)
        dy = diff * (1.0 / d)
        dg_ref[...] += jnp.sum(dy * xhat, axis=0, keepdims=True)
        dxhat = dy * gain
        dx_ref[...] = r * (dxhat - xhat * jnp.mean(dxhat * xhat, axis=-1, keepdims=True))

    row = lambda i: (i, 0)
    const = lambda i: (0, 0)
    return pl.pallas_call(
        body, name=name, grid=(s // tm,),
        in_specs=[pl.BlockSpec((tm, d), row), pl.BlockSpec((1, d), const), pl.BlockSpec((tm, d), row)],
        out_specs=[pl.BlockSpec((8, LANES), const), pl.BlockSpec((tm, d), row), pl.BlockSpec((1, d), const)],
        out_shape=[jax.ShapeDtypeStruct((8, LANES), F32), jax.ShapeDtypeStruct((s, d), F32), jax.ShapeDtypeStruct((1, d), F32)],
        compiler_params=_params("arbitrary"),
    )(x, g, target)


def adamw(w, parts, m, v, name):
    rows, cols = w.shape
    k = parts.shape[0]
    tr = _tile(rows, 512, 16)
    c1, c2 = 1.0 - ADAM_B1 ** ADAM_STEP, 1.0 - ADAM_B2 ** ADAM_STEP

    def body(w_ref, p_ref, m_ref, v_ref, g_ref, d_ref, nm_ref, nv_ref):
        grad = p_ref[0].astype(F32)
        for s in range(1, k):
            grad = grad + p_ref[s].astype(F32)
        nm = ADAM_B1 * m_ref[...] + (1.0 - ADAM_B1) * grad
        nv = ADAM_B2 * v_ref[...] + (1.0 - ADAM_B2) * (grad * grad)
        g_ref[...] = grad
        d_ref[...] = -ADAM_LR * ((nm / c1) / (jnp.sqrt(nv / c2) + ADAM_EPS) + ADAM_WD * w_ref[...])
        nm_ref[...] = nm
        nv_ref[...] = nv

    spec = pl.BlockSpec((tr, cols), lambda i: (i, 0))
    shape = jax.ShapeDtypeStruct((rows, cols), F32)
    return pl.pallas_call(
        body, name=name, grid=(rows // tr,),
        in_specs=[spec, pl.BlockSpec((k, tr, cols), lambda i: (0, i, 0)), spec, spec],
        out_specs=[spec] * 4, out_shape=[shape] * 4,
        compiler_params=_params("arbitrary"),
    )(w, parts, m, v)


SHARDED = {"ffn1_w_gate": 2, "ffn1_w_up": 2, "ffn1_w_down": 1, "ffn2_w_gate": 2, "ffn2_w_up": 2, "ffn2_w_down": 1,
           "w_mem_kv": 1, "a_w_in": 2, "a_w_out": 1, "w_kv": 1, "b_w_in": 1, "b_w_out": 1}
SMALL = ["ffn1_norm", "mix_norm", "ffn2_norm", "mem_norm", "kv_norm", "final_norm", "a_v_norm", "a_w_spatial", "a_b_spatial"]
WEIGHTS = ["ffn1_norm", "ffn1_w_gate", "ffn1_w_up", "ffn1_w_down", "mix_norm", "ffn2_norm", "ffn2_w_gate", "ffn2_w_up",
           "ffn2_w_down", "mem_norm", "w_mem_kv", "a_w_in", "a_v_norm", "a_w_spatial", "a_b_spatial", "a_w_out", "kv_norm",
           "w_kv", "b_w_in", "b_w_out", "final_norm"]


def _all_sum(parts, name):
    flat = jnp.concatenate([p.reshape(-1) for p in parts])
    pad = (-flat.size) % (16 * LANES)
    buf = jnp.pad(flat, (0, pad)).reshape(-1, LANES)
    total = sum_leading(exchange([buf], "all", True, name)[0], F32, name + "_sum").reshape(-1)
    out, off = [], 0
    for p in parts:
        out.append(total[off:off + p.size].reshape(p.shape))
        off += p.size
    return out


def _device_index():
    return 4 * lax.axis_index("x") + 2 * lax.axis_index("y") + lax.axis_index("c")


def kernel(x, mem, ffn1_norm, ffn1_w_gate, ffn1_w_up, ffn1_w_down, mix_norm, ffn2_norm, ffn2_w_gate, ffn2_w_up, ffn2_w_down, mem_norm, w_mem_kv, a_w_in, a_v_norm, a_w_spatial, a_b_spatial, a_w_out, kv_norm, w_kv, b_w_in, b_w_out, final_norm, loss_target, m_ffn1_norm, m_ffn1_w_gate, m_ffn1_w_up, m_ffn1_w_down, m_mix_norm, m_ffn2_norm, m_ffn2_w_gate, m_ffn2_w_up, m_ffn2_w_down, m_mem_norm, m_w_mem_kv, m_a_w_in, m_a_v_norm, m_a_w_spatial, m_a_b_spatial, m_a_w_out, m_kv_norm, m_w_kv, m_b_w_in, m_b_w_out, m_final_norm, v_ffn1_norm, v_ffn1_w_gate, v_ffn1_w_up, v_ffn1_w_down, v_mix_norm, v_ffn2_norm, v_ffn2_w_gate, v_ffn2_w_up, v_ffn2_w_down, v_mem_norm, v_w_mem_kv, v_a_w_in, v_a_v_norm, v_a_w_spatial, v_a_b_spatial, v_a_w_out, v_kv_norm, v_w_kv, v_b_w_in, v_b_w_out, v_final_norm):
    weights = dict(ffn1_norm=ffn1_norm, ffn1_w_gate=ffn1_w_gate, ffn1_w_up=ffn1_w_up, ffn1_w_down=ffn1_w_down, mix_norm=mix_norm, ffn2_norm=ffn2_norm, ffn2_w_gate=ffn2_w_gate, ffn2_w_up=ffn2_w_up, ffn2_w_down=ffn2_w_down, mem_norm=mem_norm, w_mem_kv=w_mem_kv, a_w_in=a_w_in, a_v_norm=a_v_norm, a_w_spatial=a_w_spatial, a_b_spatial=a_b_spatial, a_w_out=a_w_out, kv_norm=kv_norm, w_kv=w_kv, b_w_in=b_w_in, b_w_out=b_w_out, final_norm=final_norm)
    mom1 = dict(ffn1_norm=m_ffn1_norm, ffn1_w_gate=m_ffn1_w_gate, ffn1_w_up=m_ffn1_w_up, ffn1_w_down=m_ffn1_w_down, mix_norm=m_mix_norm, ffn2_norm=m_ffn2_norm, ffn2_w_gate=m_ffn2_w_gate, ffn2_w_up=m_ffn2_w_up, ffn2_w_down=m_ffn2_w_down, mem_norm=m_mem_norm, w_mem_kv=m_w_mem_kv, a_w_in=m_a_w_in, a_v_norm=m_a_v_norm, a_w_spatial=m_a_w_spatial, a_b_spatial=m_a_b_spatial, a_w_out=m_a_w_out, kv_norm=m_kv_norm, w_kv=m_w_kv, b_w_in=m_b_w_in, b_w_out=m_b_w_out, final_norm=m_final_norm)
    mom2 = dict(ffn1_norm=v_ffn1_norm, ffn1_w_gate=v_ffn1_w_gate, ffn1_w_up=v_ffn1_w_up, ffn1_w_down=v_ffn1_w_down, mix_norm=v_mix_norm, ffn2_norm=v_ffn2_norm, ffn2_w_gate=v_ffn2_w_gate, ffn2_w_up=v_ffn2_w_up, ffn2_w_down=v_ffn2_w_down, mem_norm=v_mem_norm, w_mem_kv=v_w_mem_kv, a_w_in=v_a_w_in, a_v_norm=v_a_v_norm, a_w_spatial=v_a_w_spatial, a_b_spatial=v_a_b_spatial, a_w_out=v_a_w_out, kv_norm=v_kv_norm, w_kv=v_w_kv, b_w_in=v_b_w_in, b_w_out=v_b_w_out, final_norm=v_final_norm)

    dev = _device_index()
    xs, mem_in, target = x[0], mem[0], loss_target[0]
    d_model = xs.shape[1]
    shards = {n: weights[n] for n in SHARDED}

    def mix_keys(l):
        w_in, w_out, idx = ("a_w_in", "a_w_out", l) if l < N_A else ("b_w_in", "b_w_out", l - N_A)
        return (w_in, idx), (w_out, idx)

    def ffn_keys(ffn):
        return ([], []) if ffn is None else ([(ffn[0] + "_w_gate", ffn[1]), (ffn[0] + "_w_up", ffn[1])],
                                             [(ffn[0] + "_w_down", ffn[1])])

    def ffn_after(f, l):
        return ("ffn2", l) if f == "ffn1" else (("ffn1", l + 1) if l + 1 < DEPTH else None)

    def cut_axis(key):
        return SHARDED[key[0]] - (0 if key[1] is None else 1)

    def block(key):
        return (shards[key[0]] if key[1] is None else shards[key[0]][key[1]]).astype(BF16)

    def carrying(call, keys, same_src, source, store):
        if not keys:
            return call(None)
        result, arrived = call(Side([source(k) for k in keys], same_src))
        store.update(zip(keys, arrived))
        return result

    first_gu, first_down = ffn_keys(("ffn1", 0))
    first = first_gu + first_down + [mix_keys(0)[0], ("w_mem_kv", None)]
    landed = dict(zip(first, exchange([block(k) for k in first], "all", True, "gather_first")))
    assembled = {}

    def whole(n, l=None):
        if (n, l) not in assembled:
            pieces = [landed[n, l][d] for d in range(N_DEV)]
            if n.endswith("_w_gate"):
                pieces += [landed[n.replace("_w_gate", "_w_up"), l][d] for d in range(N_DEV)]
            assembled[n, l] = jnp.concatenate(pieces, axis=cut_axis((n, l)))
        return assembled[n, l]

    def whole_gu(f, l):
        return whole(f + "_w_gate", l)

    vn_width = a_v_norm.shape[1]
    a_v_full = _all_sum([lax.dynamic_update_slice(jnp.zeros((N_A, N_DEV * vn_width), F32), a_v_norm, (0, dev * vn_width))],
                        "gather_v_norm")[0]
    row = lambda v: v.reshape(1, -1)
    w_mem_cat = whole("w_mem_kv").transpose(1, 0, 2).reshape(d_model, -1)
    bias = [jnp.repeat(a_b_spatial[i].T, GM_P, axis=1) for i in range(N_A)]

    mem_kv, mem_h = norm_mm(mem_in, row(mem_norm), w_mem_cat, BF16, "mem_kv", emit_h=True)

    def ffn_fwd(xin, f, l):
        keys_gu, keys_down = ffn_keys(ffn_after(f, l))
        if ffn_after(f, l) == ("ffn1", N_A):
            keys_down = keys_down + [("w_kv", None)]
        gu = carrying(lambda side: norm_mm(xin, row(weights[f + "_norm"][l]), whole_gu(f, l), BF16, "ffn_gu", side=side),
                      keys_gu, True, block, landed)
        out = carrying(lambda side: swiglu_mm_res(gu, whole(f + "_w_down", l), xin, 0.5, "ffn_down", side=side),
                       keys_down, True, block, landed)
        return out, gu

    saved = []
    kv = x_kv = None
    cur = xs
    for l in range(DEPTH):
        st = {"x0": cur}
        if l == N_A:
            x_kv = cur
            kv = norm_mm(cur, row(kv_norm), whole("w_kv"), BF16, "kv_proj")
        st["x1"], st["gu1"] = ffn_fwd(cur, "ffn1", l)
        key_in, key_out = mix_keys(l)
        proj = carrying(lambda side: norm_mm(st["x1"], row(mix_norm[l]), whole(*key_in), F32 if l < N_A else BF16,
                                             "a_proj" if l < N_A else "b_proj", side=side), [key_out], True, block, landed)
        if l < N_A:
            y_tok = gmlp_fwd(proj, row(a_v_full[l]), a_w_spatial[l], bias[l], "gmlp_fwd")
            y_mem = mem_fwd(proj, 2 * GM_W // MEM_W, mem_kv, l, "mem_fwd_a")
        else:
            st["sb_out"] = sb_fwd(proj, kv, "sb_fwd")
            y_tok = st["sb_out"].astype(BF16)
            y_mem = mem_fwd(proj, SB_W // MEM_W, mem_kv, l, "mem_fwd_b")
        st["proj"] = proj
        st["y"] = jnp.concatenate([y_tok, y_mem], axis=1)
        st["x2"] = carrying(lambda side: mm_res(st["y"], whole(*key_out), st["x1"], 1.0, "mix_out", side=side),
                            [mix_keys(l + 1)[0]] if l + 1 < DEPTH else [], True, block, landed)
        cur, st["gu2"] = ffn_fwd(st["x2"], "ffn2", l)
        saved.append(st)

    loss_blk, dx, d_final = final_loss(cur, row(final_norm), target, "final_loss")
    loss = lax.psum(loss_blk[0, 0] * (0.5 / d_model), AXES)

    grads = {n: [None] * weights[n].shape[0] for n in WEIGHTS if weights[n].ndim >= 2 and n not in ("w_kv",)}
    grads["final_norm"] = d_final.reshape(-1)
    d_mem_kv = [None] * DEPTH
    d_kv = []

    summed = {}

    def pieces(key):
        g = (grads[key[0]] if key[1] is None else grads[key[0]][key[1]]).astype(BF16)
        axis = cut_axis(key)
        cut = g.reshape(g.shape[:axis] + (N_DEV, g.shape[axis] // N_DEV) + g.shape[axis + 1:])
        return jnp.moveaxis(cut, axis, 0)

    def ffn_bwd(dx, xin, gu, f, l):
        keys_gu, keys_down = ffn_keys(ffn_after(f, l))
        if f == "ffn2" and l + 1 < DEPTH:
            keys_down = keys_down + list(mix_keys(l + 1)) + ([("w_kv", None)] if l + 1 == N_A else [])
        d_gu = carrying(lambda side: mm_nt_swiglu_bwd(dx, whole(f + "_w_down", l), gu, 0.5, "ffn_dgu", side=side),
                        keys_gu, False, pieces, summed)
        dx_new, d_gain, h = carrying(
            lambda side: mm_nt_normbwd(d_gu, whole_gu(f, l), xin, row(weights[f + "_norm"][l]), dx, "ffn_dx", side=side),
            keys_down, False, pieces, summed)
        d_wgu = mm_tn(h, d_gu, 1.0, "ffn_dwgu", tb_target=1408)
        d_wdown = swiglu_mm_tn(gu, dx, 0.5, "ffn_dwdown")
        half = d_wgu.shape[1] // 2
        grads[f + "_w_gate"][l], grads[f + "_w_up"][l] = d_wgu[:, :half], d_wgu[:, half:]
        grads[f + "_w_down"][l] = d_wdown
        grads[f + "_norm"][l] = d_gain.reshape(-1)
        return dx_new

    for l in reversed(range(DEPTH)):
        st = saved[l]
        dx = ffn_bwd(dx, st["x2"], st["gu2"], "ffn2", l)
        proj = st["proj"]
        (key_in, idx), (key_out, _) = mix_keys(l)
        w_in, w_out = whole(key_in, idx), whole(key_out, idx)
        dy = mm_nt(dx, w_out, 1.0, "mix_dy")
        grads[key_out][idx] = mm_tn(st["y"], dx, 1.0, "mix_dwout", tb_target=1024)
        if l < N_A:
            d_uv, d_ws, d_bs, d_vgain = gmlp_bwd(proj, dy, row(a_v_full[l]), a_w_spatial[l], bias[l], "gmlp_bwd")
            grads["a_w_spatial"][l], grads["a_b_spatial"][l], grads["a_v_norm"][l] = d_ws, d_bs[:, :, 0], d_vgain.reshape(-1)
            d_q, d_k, d_v = mem_bwd(proj, 2 * GM_W // MEM_W, mem_kv, l, dy, GM_W // MEM_W, "mem_bwd_a")
            d_proj = jnp.concatenate([d_uv, d_q], axis=1)
        else:
            d_qsb, d_ksb, d_vsb = sb_bwd(proj, kv, st["sb_out"], dy, "sb_bwd")
            d_kv.append(jnp.concatenate([d_ksb, d_vsb], axis=1))
            d_q, d_k, d_v = mem_bwd(proj, SB_W // MEM_W, mem_kv, l, dy, SB_W // MEM_W, "mem_bwd_b")
            d_proj = jnp.concatenate([d_qsb, d_q], axis=1)
        d_mem_kv[l] = jnp.concatenate([d_k, d_v], axis=1)
        dx, d_gain, h = mm_nt_normbwd(d_proj, w_in, st["x1"], row(mix_norm[l]), dx, "mix_dx")
        grads["mix_norm"][l] = d_gain.reshape(-1)
        grads[key_in][idx] = mm_tn(h, d_proj, 1.0, "mix_dwin")
        dx = ffn_bwd(dx, st["x0"], st["gu1"], "ffn1", l)
        if l == N_A:
            d_kv_b = sum_leading(jnp.stack(d_kv), BF16, "kv_dsum")
            dx, d_gain, h = mm_nt_normbwd(d_kv_b, whole("w_kv"), x_kv, row(kv_norm), dx, "kv_dx")
            grads["kv_norm"] = d_gain.reshape(-1)
            grads["w_kv"] = mm_tn(h, d_kv_b, 1.0, "kv_dw")

    d_mem_all = jnp.concatenate(d_mem_kv, axis=1).astype(BF16)
    _, d_gain, _ = mm_nt_normbwd(d_mem_all, w_mem_cat, mem_in, row(mem_norm), None, "mem_dnorm")
    grads["mem_norm"] = d_gain.reshape(-1)
    d_wmem = mm_tn(mem_h, d_mem_all, 1.0, "mem_dw")
    grads["w_mem_kv"] = d_wmem.reshape(d_model, DEPTH, -1).transpose(1, 0, 2)

    last = first + [mix_keys(0)[1]]
    summed.update(zip(last, exchange([pieces(k) for k in last], "all", False, "scatter_last")))
    parts = {n: summed[n, None] if (n, None) in summed else
             jnp.stack([summed[n, i] for i in range(weights[n].shape[0])], axis=1) for n in SHARDED}
    grads = {n: (jnp.stack(g) if isinstance(g, list) else g) for n, g in grads.items()}
    for n, g in zip(SMALL, _all_sum([grads[n] for n in SMALL], "sum_small")):
        parts[n] = g[None]
    parts["a_v_norm"] = lax.dynamic_slice(parts["a_v_norm"], (0, 0, dev * vn_width), (1,) + a_v_norm.shape)

    reduced, deltas, new_m, new_v = {}, {}, {}, {}
    for n in WEIGHTS:
        w = weights[n]
        view = (lambda a: a.reshape(-1, a.shape[-1]))
        res = adamw(view(w), parts[n].reshape(parts[n].shape[0], -1, w.shape[-1]), view(mom1[n]), view(mom2[n]), "adamw")
        reduced[n], deltas[n], new_m[n], new_v[n] = [r.reshape(w.shape) for r in res]

    return (loss, dx[None], *[reduced[n] for n in WEIGHTS], *[deltas[n] for n in WEIGHTS],
            *[new_m[n] for n in WEIGHTS], *[new_v[n] for n in WEIGHTS])
```

```python
import functools

import jax
import jax.numpy as jnp
from jax import lax
from jax.experimental import pallas as pl
from jax.experimental.pallas import tpu as pltpu

F32, BF16 = jnp.float32, jnp.bfloat16
MESH_ID = pl.DeviceIdType.MESH
AXES = ("x", "y", "c")
N_DEV = 8

EPS = 1e-6
DEPTH, N_A = 4, 2
GM_W, GM_GROUPS, GM_P = 768, 6, 128
MEM_W, MEM_HEADS, HEAD_DIM = 256, 4, 64
SB_W, SB_BLK = 768, 128
LANES = 128
QK_SCALE = HEAD_DIM ** -0.5
GELU_C, GELU_A = 0.7978845608028654, 0.044715

ADAM_LR, ADAM_B1, ADAM_B2, ADAM_EPS, ADAM_WD, ADAM_STEP = 0.001, 0.9, 0.999, 1e-08, 0.01, 10

VMEM_LIMIT = 56 * 1024 * 1024
PACK_COLS = 512

NT = (((1,), (1,)), ((), ()))
TN = (((0,), (0,)), ((), ()))


def _params(*sem):
    return pltpu.CompilerParams(dimension_semantics=sem, vmem_limit_bytes=VMEM_LIMIT)


def _tile(n, target, mult=LANES):
    best = None
    for t in range(mult, min(n, target) + 1, mult):
        if n % t == 0:
            best = t
    return best if best is not None else n


def _dot(a, b, dims=None):
    if dims is None:
        return jnp.dot(a, b, preferred_element_type=F32)
    return lax.dot_general(a, b, dims, preferred_element_type=F32)


def exchange(srcs, group, same_src, name, split=False):
    size = {"pair": 2, "quad": 4, "all": 8}[group]
    n = len(srcs)
    chunk_shapes = [tuple(s.shape) if same_src else tuple(s.shape[1:]) for s in srcs]
    pieces = [cs[0] if split else 1 for cs in chunk_shapes]
    n_dma = sum(pieces)

    def body(*refs):
        src_refs, out_refs = refs[:n], refs[n:2 * n]
        send_sems, recv_sems, local_sems = refs[2 * n:]
        x, y, c = lax.axis_index("x"), lax.axis_index("y"), lax.axis_index("c")
        if group == "pair":
            me, dev = c, lambda p: (x, y, p)
        elif group == "quad":
            me, dev = 2 * x + y, lambda p: (p // 2, p % 2, c)
        else:
            me, dev = 4 * x + 2 * y + c, lambda p: (p // 4, (p // 2) % 2, p % 2)

        def chunk(t, idx):
            return src_refs[t] if same_src else src_refs[t].at[idx]

        def copies(k, idx, slot, peer):
            out, w = [], k * n_dma
            for t in range(n):
                src, dst = chunk(t, idx), out_refs[t].at[slot]
                for s_ref, d_ref in ([(src.at[u], dst.at[u]) for u in range(pieces[t])] if split else [(src, dst)]):
                    out.append(pltpu.make_async_remote_copy(
                        src_ref=s_ref, dst_ref=d_ref, send_sem=send_sems.at[w], recv_sem=recv_sems.at[w],
                        device_id=dev(peer), device_id_type=MESH_ID))
                    w += 1
            return out

        local = [pltpu.make_async_copy(chunk(t, me), out_refs[t].at[me], local_sems.at[t]) for t in range(n)]
        for cp in local:
            cp.start()
        sends = []
        for k in range(1, size):
            peer = (me + k) % size
            sends += copies(k, peer, me, peer)
        for cp in sends:
            cp.start()
        for k in range(1, size):
            sender = (me + size - k) % size
            for cp in copies(k, me, sender, sender):
                cp.wait_recv()
        for cp in sends:
            cp.wait_send()
        for cp in local:
            cp.wait()

    hbm = pl.BlockSpec(memory_space=pltpu.HBM)
    return pl.pallas_call(
        body, name=name,
        out_shape=[jax.ShapeDtypeStruct((size,) + cs, s.dtype) for cs, s in zip(chunk_shapes, srcs)],
        in_specs=[hbm] * n, out_specs=[hbm] * n,
        scratch_shapes=[pltpu.SemaphoreType.DMA((size * n_dma,)), pltpu.SemaphoreType.DMA((size * n_dma,)),
                        pltpu.SemaphoreType.DMA((n,))],
    )(*srcs)


class Side:
    def __init__(self, srcs, same_src):
        self.srcs, self.same_src, self.n = list(srcs), same_src, len(srcs)
        self.chunk_shapes = [tuple(s.shape) if same_src else tuple(s.shape[1:]) for s in srcs]

    def out_shapes(self):
        return [jax.ShapeDtypeStruct((N_DEV,) + cs, s.dtype) for cs, s in zip(self.chunk_shapes, self.srcs)]

    def scratch(self):
        return [pltpu.SemaphoreType.DMA((N_DEV * self.n,)), pltpu.SemaphoreType.DMA((N_DEV * self.n,))]

    def _copies(self, src_refs, land_refs, send_sems, recv_sems, outgoing):
        me = 4 * lax.axis_index("x") + 2 * lax.axis_index("y") + lax.axis_index("c")
        out = []
        for k in range(1, N_DEV):
            peer = (me + k) % N_DEV if outgoing else (me + N_DEV - k) % N_DEV
            for t in range(self.n):
                src = src_refs[t] if self.same_src else src_refs[t].at[peer if outgoing else me]
                out.append(pltpu.make_async_remote_copy(
                    src_ref=src, dst_ref=land_refs[t].at[me if outgoing else peer],
                    send_sem=send_sems.at[k * self.n + t], recv_sem=recv_sems.at[k * self.n + t],
                    device_id=(peer // 4, (peer // 2) % 2, peer % 2), device_id_type=MESH_ID))
        return out

    def _own(self, src_refs, land_refs, send_sems):
        me = 4 * lax.axis_index("x") + 2 * lax.axis_index("y") + lax.axis_index("c")
        return [pltpu.make_async_copy(src_refs[t] if self.same_src else src_refs[t].at[me], land_refs[t].at[me],
                                      send_sems.at[t]) for t in range(self.n)]

    def start(self, src_refs, land_refs, send_sems, recv_sems):
        for cp in self._own(src_refs, land_refs, send_sems) + self._copies(src_refs, land_refs, send_sems, recv_sems, True):
            cp.start()

    def wait(self, src_refs, land_refs, send_sems, recv_sems):
        for cp in self._copies(src_refs, land_refs, send_sems, recv_sems, False):
            cp.wait_recv()
        for cp in self._copies(src_refs, land_refs, send_sems, recv_sems, True):
            cp.wait_send()
        for cp in self._own(src_refs, land_refs, send_sems):
            cp.wait()


def _call(body, side, name, grid, in_specs, out_specs, out_shape, scratch_shapes, dims, args):
    if side is None:
        res = pl.pallas_call(body, name=name, grid=grid, in_specs=in_specs, out_specs=out_specs, out_shape=out_shape,
                             scratch_shapes=scratch_shapes, compiler_params=_params(*dims))(*args)
        return list(res), []
    n_in, n_out, n_scr, ns = len(in_specs), len(out_specs), len(scratch_shapes), side.n

    def wrapped(*refs):
        ins, srcs = refs[:n_in], refs[n_in:n_in + ns]
        outs, lands = refs[n_in + ns:n_in + ns + n_out], refs[n_in + ns + n_out:n_in + 2 * ns + n_out]
        scratch, (send_sems, recv_sems) = refs[n_in + 2 * ns + n_out:n_in + 2 * ns + n_out + n_scr], refs[-2:]
        first, last = None, None
        for axis, steps in enumerate(grid):
            i = pl.program_id(axis)
            first = (i == 0) if first is None else first & (i == 0)
            last = (i == steps - 1) if last is None else last & (i == steps - 1)

        @pl.when(first)
        def _():
            side.start(srcs, lands, send_sems, recv_sems)

        body(*ins, *outs, *scratch)

        @pl.when(last)
        def _():
            side.wait(srcs, lands, send_sems, recv_sems)

    hbm = pl.BlockSpec(memory_space=pltpu.HBM)
    res = pl.pallas_call(
        wrapped, name=name, grid=grid, in_specs=list(in_specs) + [hbm] * ns, out_specs=list(out_specs) + [hbm] * ns,
        out_shape=list(out_shape) + side.out_shapes(), scratch_shapes=list(scratch_shapes) + side.scratch(),
        compiler_params=_params(*dims))(*args, *side.srcs)
    return list(res[:n_out]), list(res[n_out:])


def sum_leading(parts, out_dtype, name):
    k, rows, cols = parts.shape
    tr = _tile(rows, 512, 16)

    def body(p_ref, o_ref):
        acc = p_ref[0].astype(F32)
        for s in range(1, k):
            acc = acc + p_ref[s].astype(F32)
        o_ref[...] = acc.astype(o_ref.dtype)

    return pl.pallas_call(
        body, name=name, grid=(rows // tr,),
        in_specs=[pl.BlockSpec((k, tr, cols), lambda i: (0, i, 0))],
        out_specs=pl.BlockSpec((tr, cols), lambda i: (i, 0)),
        out_shape=jax.ShapeDtypeStruct((rows, cols), out_dtype),
        compiler_params=_params("arbitrary"),
    )(parts)


def _rms(xf):
    return lax.rsqrt(jnp.mean(xf * xf, axis=-1, keepdims=True) + EPS)


def norm_mm(x, g, w, out_dtype, name, emit_h=False, side=None):
    m, d = x.shape
    n = w.shape[1]
    tm, tn = _tile(m, 1024, 8), _tile(n, 1408)

    def body(x_ref, g_ref, w_ref, o_ref, *rest):
        h_ref = rest[-1]

        @pl.when(pl.program_id(1) == 0)
        def _():
            xf = x_ref[...]
            hb = ((xf * _rms(xf)) * g_ref[...]).astype(BF16)
            h_ref[...] = hb
            if emit_h:
                rest[0][...] = hb

        o_ref[...] = _dot(h_ref[...], w_ref[...]).astype(o_ref.dtype)

    out_shape = [jax.ShapeDtypeStruct((m, n), out_dtype)]
    out_specs = [pl.BlockSpec((tm, tn), lambda i, j: (i, j))]
    if emit_h:
        out_shape.append(jax.ShapeDtypeStruct((m, d), BF16))
        out_specs.append(pl.BlockSpec((tm, d), lambda i, j: (i, 0)))
    res, landed = _call(
        body, side, name, (m // tm, n // tn),
        [pl.BlockSpec((tm, d), lambda i, j: (i, 0)), pl.BlockSpec((1, d), lambda i, j: (0, 0)),
         pl.BlockSpec((d, tn), lambda i, j: (0, j))],
        out_specs, out_shape, [pltpu.VMEM((tm, d), BF16)], ("arbitrary", "arbitrary"), (x, g, w))
    out = res if emit_h else res[0]
    return out if side is None else (out, landed)


def mm_res(a, w, res, alpha, name, side=None):
    m, k = a.shape
    n = w.shape[1]
    tm, tn = _tile(m, 1024, 8), _tile(n, 1024)

    def body(a_ref, w_ref, r_ref, o_ref):
        o_ref[...] = r_ref[...] + alpha * _dot(a_ref[...], w_ref[...])

    out, landed = _call(
        body, side, name, (m // tm, n // tn),
        [pl.BlockSpec((tm, k), lambda i, j: (i, 0)), pl.BlockSpec((k, tn), lambda i, j: (0, j)),
         pl.BlockSpec((tm, tn), lambda i, j: (i, j))],
        [pl.BlockSpec((tm, tn), lambda i, j: (i, j))], [jax.ShapeDtypeStruct((m, n), F32)], [],
        ("arbitrary", "arbitrary"), (a, w, res))
    return out[0] if side is None else (out[0], landed)


def mm_nt(x, w, alpha, name):
    m, d = x.shape
    n = w.shape[0]
    tm, tn = _tile(m, 1024, 8), _tile(n, 1408)

    def body(x_ref, w_ref, o_ref, xb_ref):
        @pl.when(pl.program_id(1) == 0)
        def _():
            xb_ref[...] = x_ref[...].astype(BF16)

        o_ref[...] = (alpha * _dot(xb_ref[...], w_ref[...], NT)).astype(o_ref.dtype)

    return pl.pallas_call(
        body, name=name, grid=(m // tm, n // tn),
        in_specs=[pl.BlockSpec((tm, d), lambda i, j: (i, 0)), pl.BlockSpec((tn, d), lambda i, j: (j, 0))],
        out_specs=pl.BlockSpec((tm, tn), lambda i, j: (i, j)),
        out_shape=jax.ShapeDtypeStruct((m, n), BF16),
        scratch_shapes=[pltpu.VMEM((tm, d), BF16)],
        compiler_params=_params("arbitrary", "arbitrary"),
    )(x, w)


def mm_tn(a, b, alpha, name, ta_target=1024, tb_target=512, side=None):
    s, ka = a.shape
    nb = b.shape[1]
    ta, tb, ts = _tile(ka, ta_target), _tile(nb, tb_target), _tile(s, 1024, 16)
    steps = s // ts

    def body(a_ref, b_ref, o_ref, acc_ref):
        t = pl.program_id(2)

        @pl.when(t == 0)
        def _():
            acc_ref[...] = jnp.zeros_like(acc_ref)

        acc_ref[...] += _dot(a_ref[...].astype(BF16), b_ref[...].astype(BF16), TN)

        @pl.when(t == steps - 1)
        def _():
            o_ref[...] = (alpha * acc_ref[...]).astype(o_ref.dtype)

    res, landed = _call(
        body, side, name, (ka // ta, nb // tb, steps),
        [pl.BlockSpec((ts, ta), lambda i, j, t: (t, i)), pl.BlockSpec((ts, tb), lambda i, j, t: (t, j))],
        [pl.BlockSpec((ta, tb), lambda i, j, t: (i, j))], [jax.ShapeDtypeStruct((ka, nb), BF16)],
        [pltpu.VMEM((ta, tb), F32)], ("arbitrary", "arbitrary", "arbitrary"), (a, b))
    return res[0] if side is None else (res[0], landed)


def mm_nt_normbwd(dy, w, x, g, res, name, side=None):
    m, n = dy.shape
    d = w.shape[0]
    tm, tk = _tile(m, 1024, 8), _tile(n, 1408)
    steps = n // tk
    has_res = res is not None

    def body(*refs):
        if has_res:
            dy_ref, w_ref, x_ref, g_ref, r_ref, dx_ref, dg_ref, h_ref, acc_ref = refs
        else:
            dy_ref, w_ref, x_ref, g_ref, dx_ref, dg_ref, h_ref, acc_ref = refs
        i, t = pl.program_id(0), pl.program_id(1)

        @pl.when(t == 0)
        def _():
            acc_ref[...] = jnp.zeros_like(acc_ref)

        @pl.when((t == 0) & (i == 0))
        def _():
            dg_ref[...] = jnp.zeros_like(dg_ref)

        acc_ref[...] += _dot(dy_ref[...], w_ref[...], NT)

        @pl.when(t == steps - 1)
        def _():
            xf = x_ref[...]
            r = _rms(xf)
            xhat = xf * r
            dh = acc_ref[...]
            gain = g_ref[...]
            dg_ref[...] += jnp.sum(dh * xhat, axis=0, keepdims=True)
            dxhat = dh * gain
            dx = r * (dxhat - xhat * jnp.mean(dxhat * xhat, axis=-1, keepdims=True))
            dx_ref[...] = (r_ref[...] + dx) if has_res else dx
            h_ref[...] = (xhat * gain).astype(BF16)

    row = lambda i, t: (i, 0)
    in_specs = [pl.BlockSpec((tm, tk), lambda i, t: (i, t)), pl.BlockSpec((d, tk), lambda i, t: (0, t)),
                pl.BlockSpec((tm, d), row), pl.BlockSpec((1, d), lambda i, t: (0, 0))]
    args = [dy, w, x, g]
    if has_res:
        in_specs.append(pl.BlockSpec((tm, d), row))
        args.append(res)
    res, landed = _call(
        body, side, name, (m // tm, steps), in_specs,
        [pl.BlockSpec((tm, d), row), pl.BlockSpec((1, d), lambda i, t: (0, 0)), pl.BlockSpec((tm, d), row)],
        [jax.ShapeDtypeStruct((m, d), F32), jax.ShapeDtypeStruct((1, d), F32), jax.ShapeDtypeStruct((m, d), BF16)],
        [pltpu.VMEM((tm, d), F32)], ("arbitrary", "arbitrary"), args)
    return res if side is None else (res, landed)


def _sigmoid(z):
    return 1.0 / (1.0 + jnp.exp(-z))


def _swiglu(gate_b, up_b):
    gate = gate_b.astype(F32)
    return (gate * _sigmoid(gate) * up_b.astype(F32)).astype(BF16)


def swiglu_mm_res(gu, w, res, alpha, name, side=None):
    m, f2 = gu.shape
    f, n = w.shape
    tm, tc = _tile(m, 256, 16), _tile(f, 256)

    def body(gu_ref, w_ref, r_ref, o_ref):
        acc = jnp.zeros((tm, n), F32)
        for c0 in range(0, f, tc):
            act = _swiglu(gu_ref[:, c0:c0 + tc], gu_ref[:, f + c0:f + c0 + tc])
            acc = acc + _dot(act, w_ref[c0:c0 + tc, :])
        o_ref[...] = r_ref[...] + alpha * acc

    out, landed = _call(
        body, side, name, (m // tm,),
        [pl.BlockSpec((tm, f2), lambda i: (i, 0)), pl.BlockSpec((f, n), lambda i: (0, 0)),
         pl.BlockSpec((tm, n), lambda i: (i, 0))],
        [pl.BlockSpec((tm, n), lambda i: (i, 0))], [jax.ShapeDtypeStruct((m, n), F32)], [], ("arbitrary",), (gu, w, res))
    return out[0] if side is None else (out[0], landed)


def swiglu_mm_tn(gu, b, alpha, name, side=None):
    s, f2 = gu.shape
    f, n = f2 // 2, b.shape[1]
    ta, ts = _tile(f, 1408), _tile(s, 512, 16)
    steps, half = s // ts, f // ta

    def body(g_ref, u_ref, b_ref, o_ref, acc_ref):
        t = pl.program_id(1)

        @pl.when(t == 0)
        def _():
            acc_ref[...] = jnp.zeros_like(acc_ref)

        bb = b_ref[...].astype(BF16)
        for c0 in range(0, ta, LANES):
            acc_ref[c0:c0 + LANES, :] += _dot(_swiglu(g_ref[:, c0:c0 + LANES], u_ref[:, c0:c0 + LANES]), bb, TN)

        @pl.when(t == steps - 1)
        def _():
            o_ref[...] = (alpha * acc_ref[...]).astype(o_ref.dtype)

    res, landed = _call(
        body, side, name, (half, steps),
        [pl.BlockSpec((ts, ta), lambda i, t: (t, i)), pl.BlockSpec((ts, ta), lambda i, t: (t, half + i)),
         pl.BlockSpec((ts, n), lambda i, t: (t, 0))],
        [pl.BlockSpec((ta, n), lambda i, t: (i, 0))], [jax.ShapeDtypeStruct((f, n), BF16)],
        [pltpu.VMEM((ta, n), F32)], ("arbitrary", "arbitrary"), (gu, gu, b))
    return res[0] if side is None else (res[0], landed)


def mm_nt_swiglu_bwd(x, w, gu, alpha, name, side=None):
    m, d = x.shape
    f = w.shape[0]
    tm, tc = _tile(m, 256, 16), _tile(f, 256)

    def body(x_ref, w_ref, gu_ref, o_ref):
        xb = x_ref[...].astype(BF16)
        for c0 in range(0, f, tc):
            d_act = alpha * _dot(xb, w_ref[c0:c0 + tc, :], NT)
            gate, up = gu_ref[:, c0:c0 + tc].astype(F32), gu_ref[:, f + c0:f + c0 + tc].astype(F32)
            sg = _sigmoid(gate)
            o_ref[:, c0:c0 + tc] = (d_act * up * (sg * (1.0 + gate * (1.0 - sg)))).astype(BF16)
            o_ref[:, f + c0:f + c0 + tc] = (d_act * (gate * sg)).astype(BF16)

    res, landed = _call(
        body, side, name, (m // tm,),
        [pl.BlockSpec((tm, d), lambda i: (i, 0)), pl.BlockSpec((f, d), lambda i: (0, 0)),
         pl.BlockSpec((tm, 2 * f), lambda i: (i, 0))],
        [pl.BlockSpec((tm, 2 * f), lambda i: (i, 0))], [jax.ShapeDtypeStruct((m, 2 * f), BF16)], [], ("arbitrary",),
        (x, w, gu))
    return res[0] if side is None else (res[0], landed)


def _gelu(x):
    return 0.5 * x * (1.0 + jnp.tanh(GELU_C * (x + GELU_A * x * x * x)))


def _gelu_grad(x):
    t = jnp.tanh(GELU_C * (x + GELU_A * x * x * x))
    return 0.5 * (1.0 + t) + 0.5 * x * (1.0 - t * t) * (GELU_C * (1.0 + 3.0 * GELU_A * x * x))


def _chunk_mask():
    row = lax.broadcasted_iota(jnp.int32, (GM_P, GM_P), 0)
    col = lax.broadcasted_iota(jnp.int32, (GM_P, GM_P), 1)
    return (col < GM_P // 2) | (row >= GM_P // 2)


def gmlp_fwd(proj, gain, w_s, bias, name):
    s, pw = proj.shape
    tm = _tile(s, 256, GM_P)

    def body(p_ref, gain_ref, w_ref, b_ref, o_ref):
        mask = _chunk_mask()
        u = _gelu(p_ref[:, :GM_W])
        v = _gelu(p_ref[:, GM_W:2 * GM_W])
        vn = ((v * _rms(v)) * gain_ref[...]).astype(BF16)
        for g in range(GM_GROUPS):
            wg = jnp.where(mask, w_ref[g], 0.0).astype(BF16)
            cols = slice(g * GM_P, (g + 1) * GM_P)
            for n in range(tm // GM_P):
                rows = slice(n * GM_P, (n + 1) * GM_P)
                mixed = _dot(wg, vn[rows, cols]) + b_ref[:, cols]
                o_ref[rows, cols] = (u[rows, cols] * mixed).astype(BF16)

    return pl.pallas_call(
        body, name=name, grid=(s // tm,),
        in_specs=[pl.BlockSpec((tm, pw), lambda i: (i, 0)), pl.BlockSpec((1, GM_W), lambda i: (0, 0)),
                  pl.BlockSpec((GM_GROUPS, GM_P, GM_P), lambda i: (0, 0, 0)), pl.BlockSpec((GM_P, GM_W), lambda i: (0, 0))],
        out_specs=pl.BlockSpec((tm, GM_W), lambda i: (i, 0)),
        out_shape=jax.ShapeDtypeStruct((s, GM_W), BF16), compiler_params=_params("arbitrary"),
    )(proj, gain, w_s, bias)


def gmlp_bwd(proj, dy, gain, w_s, bias, name):
    s, pw = proj.shape
    dw_total = dy.shape[1]
    tm = _tile(s, 256, GM_P)

    def body(p_ref, dy_ref, gain_ref, w_ref, b_ref, dp_ref, dw_ref, db_ref, dgain_ref, dvn_ref):
        @pl.when(pl.program_id(0) == 0)
        def _():
            dw_ref[...] = jnp.zeros_like(dw_ref)
            db_ref[...] = jnp.zeros_like(db_ref)
            dgain_ref[...] = jnp.zeros_like(dgain_ref)

        mask = _chunk_mask()
        pu = p_ref[:, :GM_W]
        pv = p_ref[:, GM_W:2 * GM_W]
        u = _gelu(pu)
        v = _gelu(pv)
        r = _rms(v)
        vhat = v * r
        gain = gain_ref[...]
        vn = (vhat * gain).astype(BF16)
        gu_grad = _gelu_grad(pu)
        for g in range(GM_GROUPS):
            wg = jnp.where(mask, w_ref[g], 0.0).astype(BF16)
            cols = slice(g * GM_P, (g + 1) * GM_P)
            dw_acc = jnp.zeros((GM_P, GM_P), F32)
            db_acc = jnp.zeros((GM_P, 1), F32)
            for n in range(tm // GM_P):
                rows = slice(n * GM_P, (n + 1) * GM_P)
                dyb = dy_ref[rows, cols].astype(F32)
                vnb = vn[rows, cols]
                mixed = _dot(wg, vnb) + b_ref[:, cols]
                dmixed = dyb * u[rows, cols]
                dmb = dmixed.astype(BF16)
                dp_ref[rows, cols] = (dyb * mixed * gu_grad[rows, cols]).astype(BF16)
                dw_acc = dw_acc + _dot(dmb, vnb, NT)
                db_acc = db_acc + jnp.sum(dmixed, axis=1, keepdims=True)
                dvn_ref[rows, cols] = _dot(wg, dmb, TN)
            dw_ref[g] += jnp.where(mask, dw_acc, 0.0)
            db_ref[g] += jnp.broadcast_to(db_acc, (GM_P, GM_P))
        dvn = dvn_ref[...]
        dgain_ref[...] += jnp.sum(dvn * vhat, axis=0, keepdims=True)
        dvhat = dvn * gain
        dv = r * (dvhat - vhat * jnp.mean(dvhat * vhat, axis=-1, keepdims=True))
        dp_ref[:, GM_W:] = (dv * _gelu_grad(pv)).astype(BF16)

    const3 = lambda i: (0, 0, 0)
    return pl.pallas_call(
        body, name=name, grid=(s // tm,),
        in_specs=[pl.BlockSpec((tm, pw), lambda i: (i, 0)), pl.BlockSpec((tm, dw_total), lambda i: (i, 0)),
                  pl.BlockSpec((1, GM_W), lambda i: (0, 0)), pl.BlockSpec((GM_GROUPS, GM_P, GM_P), const3),
                  pl.BlockSpec((GM_P, GM_W), lambda i: (0, 0))],
        out_specs=[pl.BlockSpec((tm, 2 * GM_W), lambda i: (i, 0)), pl.BlockSpec((GM_GROUPS, GM_P, GM_P), const3),
                   pl.BlockSpec((GM_GROUPS, GM_P, GM_P), const3), pl.BlockSpec((1, GM_W), lambda i: (0, 0))],
        out_shape=[jax.ShapeDtypeStruct((s, 2 * GM_W), BF16), jax.ShapeDtypeStruct((GM_GROUPS, GM_P, GM_P), F32),
                   jax.ShapeDtypeStruct((GM_GROUPS, GM_P, GM_P), F32), jax.ShapeDtypeStruct((1, GM_W), F32)],
        scratch_shapes=[pltpu.VMEM((tm, GM_W), F32)],
        compiler_params=_params("arbitrary"),
    )(proj, dy, gain, w_s, bias)


def _keep(mask, xb):
    return jnp.where(mask, xb.astype(F32), 0.0).astype(BF16)


def _head_masks(rows, width, heads):
    lane = lax.broadcasted_iota(jnp.int32, (rows, width), 1)
    return [(lane >= HEAD_DIM * h) & (lane < HEAD_DIM * (h + 1)) for h in range(heads)]


def _mem_probs(qh, k):
    sc = _dot(qh, k, NT) * QK_SCALE
    e = jnp.exp(sc - jnp.max(sc, axis=-1, keepdims=True))
    return e / jnp.sum(e, axis=-1, keepdims=True)


def mem_fwd(proj, q_blk, mem_kv, layer, name):
    s = proj.shape[0]
    n_mem = mem_kv.shape[0]
    tm = _tile(s, 512, 16)

    def body(q_ref, k_ref, v_ref, o_ref):
        q = q_ref[...].astype(BF16)
        k, v = k_ref[...], v_ref[...]
        out = jnp.zeros((tm, MEM_W), F32)
        for hm in _head_masks(tm, MEM_W, MEM_HEADS):
            p = _mem_probs(_keep(hm, q), k)
            out = out + jnp.where(hm, _dot(p.astype(BF16), v), 0.0)
        o_ref[...] = out.astype(BF16)

    return pl.pallas_call(
        body, name=name, grid=(s // tm,),
        in_specs=[pl.BlockSpec((tm, MEM_W), lambda i: (i, q_blk)), pl.BlockSpec((n_mem, MEM_W), lambda i: (0, 2 * layer)),
                  pl.BlockSpec((n_mem, MEM_W), lambda i: (0, 2 * layer + 1))],
        out_specs=pl.BlockSpec((tm, MEM_W), lambda i: (i, 0)),
        out_shape=jax.ShapeDtypeStruct((s, MEM_W), BF16), compiler_params=_params("arbitrary"),
    )(proj, mem_kv, mem_kv)


def mem_bwd(proj, q_blk, mem_kv, layer, dy, dy_blk, name):
    s = proj.shape[0]
    n_mem = mem_kv.shape[0]
    tm = _tile(s, 512, 16)

    def body(q_ref, k_ref, v_ref, dy_ref, dq_ref, dk_ref, dv_ref):
        @pl.when(pl.program_id(0) == 0)
        def _():
            dk_ref[...] = jnp.zeros_like(dk_ref)
            dv_ref[...] = jnp.zeros_like(dv_ref)

        q = q_ref[...].astype(BF16)
        k, v = k_ref[...], v_ref[...]
        dy = dy_ref[...]
        dq = jnp.zeros((tm, MEM_W), F32)
        dk = jnp.zeros((n_mem, MEM_W), F32)
        dv = jnp.zeros((n_mem, MEM_W), F32)
        for hm in _head_masks(tm, MEM_W, MEM_HEADS):
            qh = _keep(hm, q)
            dyh = _keep(hm, dy)
            p = _mem_probs(qh, k)
            dp = _dot(dyh, v, NT)
            dv = dv + _dot(p.astype(BF16), dyh, TN)
            ds = (p * (dp - jnp.sum(dp * p, axis=-1, keepdims=True)) * QK_SCALE).astype(BF16)
            dq = dq + jnp.where(hm, _dot(ds, k), 0.0)
            dk = dk + _dot(ds, qh, TN)
        dq_ref[...] = dq.astype(BF16)
        dk_ref[...] += dk
        dv_ref[...] += dv

    const = lambda i: (0, 0)
    return pl.pallas_call(
        body, name=name, grid=(s // tm,),
        in_specs=[pl.BlockSpec((tm, MEM_W), lambda i: (i, q_blk)), pl.BlockSpec((n_mem, MEM_W), lambda i: (0, 2 * layer)),
                  pl.BlockSpec((n_mem, MEM_W), lambda i: (0, 2 * layer + 1)), pl.BlockSpec((tm, MEM_W), lambda i: (i, dy_blk))],
        out_specs=[pl.BlockSpec((tm, MEM_W), lambda i: (i, 0)), pl.BlockSpec((n_mem, MEM_W), const),
                   pl.BlockSpec((n_mem, MEM_W), const)],
        out_shape=[jax.ShapeDtypeStruct((s, MEM_W), BF16), jax.ShapeDtypeStruct((n_mem, MEM_W), F32),
                   jax.ShapeDtypeStruct((n_mem, MEM_W), F32)],
        compiler_params=_params("arbitrary"),
    )(proj, mem_kv, mem_kv, dy)


SB_KEYS = 256
SB_SUB = SB_KEYS // SB_BLK
SB_QROWS = 256
SB_QB = SB_QROWS // SB_BLK
SB_CHAINS = 2 * SB_QB
SB_DEAD = -110.0


def _split(xf):
    hi = xf.astype(BF16)
    return hi, (xf - hi.astype(F32)).astype(BF16)


def _sb_consts():
    row = lax.bitwise_and(lax.broadcasted_iota(jnp.int32, (2 * SB_BLK, 2 * SB_BLK), 0), SB_BLK - 1)
    col = lax.broadcasted_iota(jnp.int32, (2 * SB_BLK, 2 * SB_BLK), 1)
    ones = col >= SB_BLK
    after2 = jnp.where(ones | (row > col), -1.0, 0.0).astype(BF16)
    from2 = jnp.where(ones | (row >= col), 1.0, 0.0).astype(BF16)
    r = lax.broadcasted_iota(jnp.int32, (SB_BLK, SB_BLK), 0)
    c = lax.broadcasted_iota(jnp.int32, (SB_BLK, SB_BLK), 1)
    return after2, from2, c - r, [c < HEAD_DIM, c >= HEAD_DIM]


def _suffix(xf, tri2):
    hi, lo = _split(xf)
    return _dot(jnp.concatenate([hi, lo], axis=1), tri2)


def _sb_logs(z, mask):
    softplus = jnp.maximum(z, 0.0) + jnp.log(1.0 + jnp.exp(-jnp.abs(z)))
    log_beta = z - softplus
    if mask is not None:
        softplus = jnp.where(mask, softplus, 0.0)
    return softplus, log_beta


def _sb_queries(q_ref, heads):
    q = q_ref[...].astype(F32) * QK_SCALE
    return [jnp.where(hm, q[r * SB_BLK:(r + 1) * SB_BLK], 0.0).astype(BF16) for r in range(SB_QB) for hm in heads]


def _sb_walk(i, block, state):
    places = SB_SUB // SB_QB
    assert places in (1, 2)
    own = lax.shift_right_logical(i * SB_QB, SB_SUB.bit_length() - 1)
    firsts = [[(v * SB_QB + r) * SB_BLK for r in range(SB_QB) for _ in range(2)] for v in range(places)]
    if places == 1:
        state = block(own, state, firsts[0])
    else:
        state = lax.cond(lax.bitwise_and(i, 1) == 0, lambda st: block(own, st, firsts[0]),
                         lambda st: block(own, st, firsts[1]), state)

    def live(carry):
        j, st = carry
        most = st[0][0]
        for run in st[0][1:]:
            most = jnp.maximum(most, run)
        return (j >= 0) & (jnp.max(most) > SB_DEAD)

    return lax.while_loop(live, lambda carry: (carry[0] - 1, block(carry[0], carry[1], None)), (own - 1, state))[1]


def _sb_tiles(first):
    out = []
    for c in reversed(range(SB_SUB)):
        for n in range(SB_CHAINS):
            if first is None or c * SB_BLK < first[n]:
                out.append((c, n, "before"))
            elif c * SB_BLK == first[n]:
                out.append((c, n, "diagonal"))
    return out


def _sb_heads_apart(stacked, heads, r):
    return jnp.where(heads[0], stacked[2 * r * SB_BLK:(2 * r + 1) * SB_BLK],
                     stacked[(2 * r + 1) * SB_BLK:(2 * r + 2) * SB_BLK])


def sb_fwd(proj, kv, name):
    s = proj.shape[0]
    assert s % SB_KEYS == 0 and SB_KEYS % SB_QROWS == 0

    def body(q_ref, k_ref, v_ref, o_ref):
        after2, _, col_minus_row, heads = _sb_consts()
        q_all = jnp.concatenate(_sb_queries(q_ref, heads), axis=0)
        key_before_query = col_minus_row < 0

        def block(j, state, first):
            runs, acc = list(state[0]), state[1]
            rows = pl.ds(pl.multiple_of(j * SB_KEYS, SB_KEYS), SB_KEYS)
            kb, vb = k_ref[rows, :], v_ref[rows, :]
            z = _dot(q_all, kb, NT)
            pend = {}
            parts = [[jnp.zeros((SB_BLK, SB_BLK), BF16)] * SB_SUB for _ in range(SB_CHAINS)]
            for c, n, where in _sb_tiles(first):
                mask = key_before_query if where == "diagonal" else None
                softplus, lb = _sb_logs(z[n * SB_BLK:(n + 1) * SB_BLK, c * SB_BLK:(c + 1) * SB_BLK], mask)
                pend[c, n] = (lb, _suffix(softplus, after2), mask)
            for c, n, _ in _sb_tiles(first):
                lb, r, mask = pend.pop((c, n))
                a = jnp.exp(lb + r[:, :SB_BLK] + runs[n])
                if mask is not None:
                    a = jnp.where(mask, a, 0.0)
                parts[n][c] = a.astype(BF16)
                runs[n] = runs[n] + r[:, SB_BLK:]
            a_all = jnp.concatenate([jnp.concatenate(p, axis=1) for p in parts], axis=0)
            return tuple(runs), acc + _dot(a_all, vb)

        zero = jnp.zeros((SB_BLK, LANES), F32)
        state = _sb_walk(pl.program_id(1), block, ((zero,) * SB_CHAINS, jnp.zeros((SB_CHAINS * SB_BLK, LANES), F32)))
        for r in range(SB_QB):
            o_ref[r * SB_BLK:(r + 1) * SB_BLK, :] = _sb_heads_apart(state[1], heads, r)

    pairs = SB_W // LANES
    return pl.pallas_call(
        body, name=name, grid=(pairs, s // SB_QROWS),
        in_specs=[pl.BlockSpec((SB_QROWS, LANES), lambda p, i: (i, p)), pl.BlockSpec((s, LANES), lambda p, i: (0, p)),
                  pl.BlockSpec((s, LANES), lambda p, i: (0, pairs + p))],
        out_specs=pl.BlockSpec((SB_QROWS, LANES), lambda p, i: (i, p)),
        out_shape=jax.ShapeDtypeStruct((s, SB_W), F32),
        compiler_params=_params("arbitrary", "arbitrary"),
    )(proj, kv, kv)


def sb_bwd(proj, kv, out, dy, name):
    s = proj.shape[0]

    def body(q_ref, k_ref, v_ref, o_ref, do_ref, dq_ref, dk_ref, dv_ref):
        i = pl.program_id(1)

        @pl.when(i == 0)
        def _():
            dk_ref[...] = jnp.zeros_like(dk_ref)
            dv_ref[...] = jnp.zeros_like(dv_ref)

        after2, from2, col_minus_row, heads = _sb_consts()
        q_all = jnp.concatenate(_sb_queries(q_ref, heads), axis=0)
        key_before_query = col_minus_row < 0
        d_out = do_ref[...].astype(F32)
        prod = d_out * o_ref[...]
        dos, totals = [], []
        for r in range(SB_QB):
            rr = slice(r * SB_BLK, (r + 1) * SB_BLK)
            for hm in heads:
                dos.append(jnp.where(hm, d_out[rr], 0.0).astype(BF16))
                totals.append(jnp.broadcast_to(jnp.sum(jnp.where(hm, prod[rr], 0.0), axis=1, keepdims=True),
                                               (SB_BLK, SB_BLK)))
        do_all = jnp.concatenate(dos, axis=0)

        def block(j, state, first):
            runs, seens, dq = list(state[0]), list(state[1]), state[2]
            rows = pl.ds(pl.multiple_of(j * SB_KEYS, SB_KEYS), SB_KEYS)
            kb, vb = k_ref[rows, :], v_ref[rows, :]
            z = _dot(q_all, kb, NT)
            da = _dot(do_all, vb, NT)
            pend, pend2 = {}, {}
            a_parts = [[jnp.zeros((SB_BLK, SB_BLK), BF16)] * SB_SUB for _ in range(SB_CHAINS)]
            dz_parts = [[jnp.zeros((SB_BLK, SB_BLK), BF16)] * SB_SUB for _ in range(SB_CHAINS)]
            for c, n, where in _sb_tiles(first):
                mask = key_before_query if where == "diagonal" else None
                softplus, lb = _sb_logs(z[n * SB_BLK:(n + 1) * SB_BLK, c * SB_BLK:(c + 1) * SB_BLK], mask)
                pend[c, n] = (softplus, lb, _suffix(softplus, after2), mask)
            for c, n, _ in _sb_tiles(first):
                softplus, lb, r, mask = pend.pop((c, n))
                a = jnp.exp(lb + r[:, :SB_BLK] + runs[n])
                if mask is not None:
                    a = jnp.where(mask, a, 0.0)
                runs[n] = runs[n] + r[:, SB_BLK:]
                ab = a.astype(BF16)
                a_parts[n][c] = ab
                dl = ab.astype(F32) * da[n * SB_BLK:(n + 1) * SB_BLK, c * SB_BLK:(c + 1) * SB_BLK]
                pend2[c, n] = (softplus, lb, dl, _suffix(dl, from2), mask)
            for c, n, _ in _sb_tiles(first):
                softplus, lb, dl, r2, mask = pend2.pop((c, n))
                d_lom = totals[n] - (r2[:, :SB_BLK] + seens[n])
                if mask is not None:
                    d_lom = jnp.where(mask, d_lom, 0.0)
                seens[n] = seens[n] + r2[:, SB_BLK:]
                dz_parts[n][c] = (dl * jnp.exp(-softplus) - d_lom * jnp.exp(lb)).astype(BF16)
            a_all = jnp.concatenate([jnp.concatenate(p, axis=1) for p in a_parts], axis=0)
            dz_all = jnp.concatenate([jnp.concatenate(p, axis=1) for p in dz_parts], axis=0)
            dv_ref[rows, :] += _dot(a_all, do_all, TN)
            dk_ref[rows, :] += _dot(dz_all, q_all, TN)
            return tuple(runs), tuple(seens), dq + _dot(dz_all, kb)

        zero = jnp.zeros((SB_BLK, LANES), F32)
        state = _sb_walk(i, block, ((zero,) * SB_CHAINS, (zero,) * SB_CHAINS,
                                    jnp.zeros((SB_CHAINS * SB_BLK, LANES), F32)))
        for r in range(SB_QB):
            dq_ref[r * SB_BLK:(r + 1) * SB_BLK, :] = (_sb_heads_apart(state[2], heads, r) * QK_SCALE).astype(BF16)

    pairs = SB_W // LANES
    blk = lambda p, i: (i, p)
    col = lambda p, i: (0, p)
    return pl.pallas_call(
        body, name=name, grid=(pairs, s // SB_QROWS),
        in_specs=[pl.BlockSpec((SB_QROWS, LANES), blk), pl.BlockSpec((s, LANES), col),
                  pl.BlockSpec((s, LANES), lambda p, i: (0, pairs + p)), pl.BlockSpec((SB_QROWS, LANES), blk),
                  pl.BlockSpec((SB_QROWS, LANES), blk)],
        out_specs=[pl.BlockSpec((SB_QROWS, LANES), blk), pl.BlockSpec((s, LANES), col), pl.BlockSpec((s, LANES), col)],
        out_shape=[jax.ShapeDtypeStruct((s, SB_W), BF16), jax.ShapeDtypeStruct((s, SB_W), F32),
                   jax.ShapeDtypeStruct((s, SB_W), F32)],
        compiler_params=_params("arbitrary", "arbitrary"),
    )(proj, kv, kv, out, dy)


def final_loss(x, g, target, name):
    s, d = x.shape
    tm = _tile(s, 256, 8)

    def body(x_ref, g_ref, t_ref, loss_ref, dx_ref, dg_ref):
        @pl.when(pl.program_id(0) == 0)
        def _():
            loss_ref[...] = jnp.zeros_like(loss_ref)
            dg_ref[...] = jnp.zeros_like(dg_ref)

        xf = x_ref[...]
        r = _rms(xf)
        xhat = xf * r
        gain = g_ref[...]
        diff = xhat * gain - t_ref[...]
        sq = jnp.sum(jnp.sum(diff * diff, axis=1, keepdims=True), axis=0, keepdims=True)
        loss_ref[...] += jnp.broadcast_to(sq, loss_ref.shape)
        dy = diff * (1.0 / d)
        dg_ref[...] += jnp.sum(dy * xhat, axis=0, keepdims=True)
        dxhat = dy * gain
        dx_ref[...] = r * (dxhat - xhat * jnp.mean(dxhat * xhat, axis=-1, keepdims=True))

    row = lambda i: (i, 0)
    const = lambda i: (0, 0)
    return pl.pallas_call(
        body, name=name, grid=(s // tm,),
        in_specs=[pl.BlockSpec((tm, d), row), pl.BlockSpec((1, d), const), pl.BlockSpec((tm, d), row)],
        out_specs=[pl.BlockSpec((8, LANES), const), pl.BlockSpec((tm, d), row), pl.BlockSpec((1, d), const)],
        out_shape=[jax.ShapeDtypeStruct((8, LANES), F32), jax.ShapeDtypeStruct((s, d), F32), jax.ShapeDtypeStruct((1, d), F32)],
        compiler_params=_params("arbitrary"),
    )(x, g, target)


def adamw(w, parts, m, v, name):
    rows, cols = w.shape
    k = parts.shape[0]
    tr = _tile(rows, 512, 16)
    c1, c2 = 1.0 - ADAM_B1 ** ADAM_STEP, 1.0 - ADAM_B2 ** ADAM_STEP

    def body(w_ref, p_ref, m_ref, v_ref, g_ref, d_ref, nm_ref, nv_ref):
        grad = p_ref[0].astype(F32)
        for s in range(1, k):
            grad = grad + p_ref[s].astype(F32)
        nm = ADAM_B1 * m_ref[...] + (1.0 - ADAM_B1) * grad
        nv = ADAM_B2 * v_ref[...] + (1.0 - ADAM_B2) * (grad * grad)
        g_ref[...] = grad
        d_ref[...] = -ADAM_LR * ((nm / c1) / (jnp.sqrt(nv / c2) + ADAM_EPS) + ADAM_WD * w_ref[...])
        nm_ref[...] = nm
        nv_ref[...] = nv

    spec = pl.BlockSpec((tr, cols), lambda i: (i, 0))
    shape = jax.ShapeDtypeStruct((rows, cols), F32)
    return pl.pallas_call(
        body, name=name, grid=(rows // tr,),
        in_specs=[spec, pl.BlockSpec((k, tr, cols), lambda i: (0, i, 0)), spec, spec],
        out_specs=[spec] * 4, out_shape=[shape] * 4,
        compiler_params=_params("arbitrary"),
    )(w, parts, m, v)


SHARDED = {"ffn1_w_gate": 2, "ffn1_w_up": 2, "ffn1_w_down": 1, "ffn2_w_gate": 2, "ffn2_w_up": 2, "ffn2_w_down": 1,
           "w_mem_kv": 1, "a_w_in": 2, "a_w_out": 1, "w_kv": 1, "b_w_in": 1, "b_w_out": 1}
SMALL = ["ffn1_norm", "mix_norm", "ffn2_norm", "mem_norm", "kv_norm", "final_norm", "a_v_norm", "a_w_spatial", "a_b_spatial"]
WEIGHTS = ["ffn1_norm", "ffn1_w_gate", "ffn1_w_up", "ffn1_w_down", "mix_norm", "ffn2_norm", "ffn2_w_gate", "ffn2_w_up",
           "ffn2_w_down", "mem_norm", "w_mem_kv", "a_w_in", "a_v_norm", "a_w_spatial", "a_b_spatial", "a_w_out", "kv_norm",
           "w_kv", "b_w_in", "b_w_out", "final_norm"]


def _all_sum(parts, name):
    flat = jnp.concatenate([p.reshape(-1) for p in parts])
    pad = (-flat.size) % (16 * LANES)
    buf = jnp.pad(flat, (0, pad)).reshape(-1, LANES)
    total = sum_leading(exchange([buf], "all", True, name)[0], F32, name + "_sum").reshape(-1)
    out, off = [], 0
    for p in parts:
        out.append(total[off:off + p.size].reshape(p.shape))
        off += p.size
    return out


def _device_index():
    return 4 * lax.axis_index("x") + 2 * lax.axis_index("y") + lax.axis_index("c")


def kernel(x, mem, ffn1_norm, ffn1_w_gate, ffn1_w_up, ffn1_w_down, mix_norm, ffn2_norm, ffn2_w_gate, ffn2_w_up, ffn2_w_down, mem_norm, w_mem_kv, a_w_in, a_v_norm, a_w_spatial, a_b_spatial, a_w_out, kv_norm, w_kv, b_w_in, b_w_out, final_norm, loss_target, m_ffn1_norm, m_ffn1_w_gate, m_ffn1_w_up, m_ffn1_w_down, m_mix_norm, m_ffn2_norm, m_ffn2_w_gate, m_ffn2_w_up, m_ffn2_w_down, m_mem_norm, m_w_mem_kv, m_a_w_in, m_a_v_norm, m_a_w_spatial, m_a_b_spatial, m_a_w_out, m_kv_norm, m_w_kv, m_b_w_in, m_b_w_out, m_final_norm, v_ffn1_norm, v_ffn1_w_gate, v_ffn1_w_up, v_ffn1_w_down, v_mix_norm, v_ffn2_norm, v_ffn2_w_gate, v_ffn2_w_up, v_ffn2_w_down, v_mem_norm, v_w_mem_kv, v_a_w_in, v_a_v_norm, v_a_w_spatial, v_a_b_spatial, v_a_w_out, v_kv_norm, v_w_kv, v_b_w_in, v_b_w_out, v_final_norm):
    weights = dict(ffn1_norm=ffn1_norm, ffn1_w_gate=ffn1_w_gate, ffn1_w_up=ffn1_w_up, ffn1_w_down=ffn1_w_down, mix_norm=mix_norm, ffn2_norm=ffn2_norm, ffn2_w_gate=ffn2_w_gate, ffn2_w_up=ffn2_w_up, ffn2_w_down=ffn2_w_down, mem_norm=mem_norm, w_mem_kv=w_mem_kv, a_w_in=a_w_in, a_v_norm=a_v_norm, a_w_spatial=a_w_spatial, a_b_spatial=a_b_spatial, a_w_out=a_w_out, kv_norm=kv_norm, w_kv=w_kv, b_w_in=b_w_in, b_w_out=b_w_out, final_norm=final_norm)
    mom1 = dict(ffn1_norm=m_ffn1_norm, ffn1_w_gate=m_ffn1_w_gate, ffn1_w_up=m_ffn1_w_up, ffn1_w_down=m_ffn1_w_down, mix_norm=m_mix_norm, ffn2_norm=m_ffn2_norm, ffn2_w_gate=m_ffn2_w_gate, ffn2_w_up=m_ffn2_w_up, ffn2_w_down=m_ffn2_w_down, mem_norm=m_mem_norm, w_mem_kv=m_w_mem_kv, a_w_in=m_a_w_in, a_v_norm=m_a_v_norm, a_w_spatial=m_a_w_spatial, a_b_spatial=m_a_b_spatial, a_w_out=m_a_w_out, kv_norm=m_kv_norm, w_kv=m_w_kv, b_w_in=m_b_w_in, b_w_out=m_b_w_out, final_norm=m_final_norm)
    mom2 = dict(ffn1_norm=v_ffn1_norm, ffn1_w_gate=v_ffn1_w_gate, ffn1_w_up=v_ffn1_w_up, ffn1_w_down=v_ffn1_w_down, mix_norm=v_mix_norm, ffn2_norm=v_ffn2_norm, ffn2_w_gate=v_ffn2_w_gate, ffn2_w_up=v_ffn2_w_up, ffn2_w_down=v_ffn2_w_down, mem_norm=v_mem_norm, w_mem_kv=v_w_mem_kv, a_w_in=v_a_w_in, a_v_norm=v_a_v_norm, a_w_spatial=v_a_w_spatial, a_b_spatial=v_a_b_spatial, a_w_out=v_a_w_out, kv_norm=v_kv_norm, w_kv=v_w_kv, b_w_in=v_b_w_in, b_w_out=v_b_w_out, final_norm=v_final_norm)

    dev = _device_index()
    xs, mem_in, target = x[0], mem[0], loss_target[0]
    d_model = xs.shape[1]
    shards = {n: weights[n] for n in SHARDED}

    def mix_keys(l):
        w_in, w_out, idx = ("a_w_in", "a_w_out", l) if l < N_A else ("b_w_in", "b_w_out", l - N_A)
        return (w_in, idx), (w_out, idx)

    def ffn_keys(ffn):
        return ([], []) if ffn is None else ([(ffn[0] + "_w_gate", ffn[1]), (ffn[0] + "_w_up", ffn[1])],
                                             [(ffn[0] + "_w_down", ffn[1])])

    def ffn_after(f, l):
        return ("ffn2", l) if f == "ffn1" else (("ffn1", l + 1) if l + 1 < DEPTH else None)

    def cut_axis(key):
        return SHARDED[key[0]] - (0 if key[1] is None else 1)

    def block(key):
        return (shards[key[0]] if key[1] is None else shards[key[0]][key[1]]).astype(BF16)

    def carrying(call, keys, same_src, source, store):
        if not keys:
            return call(None)
        result, arrived = call(Side([source(k) for k in keys], same_src))
        store.update(zip(keys, arrived))
        return result

    first_gu, first_down = ffn_keys(("ffn1", 0))
    first = first_gu + first_down + [mix_keys(0)[0], ("w_mem_kv", None)]
    landed = dict(zip(first, exchange([block(k) for k in first], "all", True, "gather_first")))
    assembled = {}

    def whole(n, l=None):
        if (n, l) not in assembled:
            pieces = [landed[n, l][d] for d in range(N_DEV)]
            if n.endswith("_w_gate"):
                pieces += [landed[n.replace("_w_gate", "_w_up"), l][d] for d in range(N_DEV)]
            assembled[n, l] = jnp.concatenate(pieces, axis=cut_axis((n, l)))
        return assembled[n, l]

    def whole_gu(f, l):
        return whole(f + "_w_gate", l)

    vn_width = a_v_norm.shape[1]
    a_v_full = _all_sum([lax.dynamic_update_slice(jnp.zeros((N_A, N_DEV * vn_width), F32), a_v_norm, (0, dev * vn_width))],
                        "gather_v_norm")[0]
    row = lambda v: v.reshape(1, -1)
    w_mem_cat = whole("w_mem_kv").transpose(1, 0, 2).reshape(d_model, -1)
    bias = [jnp.repeat(a_b_spatial[i].T, GM_P, axis=1) for i in range(N_A)]

    mem_kv, mem_h = norm_mm(mem_in, row(mem_norm), w_mem_cat, BF16, "mem_kv", emit_h=True)

    def ffn_fwd(xin, f, l):
        keys_gu, keys_down = ffn_keys(ffn_after(f, l))
        if ffn_after(f, l) == ("ffn1", N_A):
            keys_down = keys_down + [("w_kv", None)]
        gu = carrying(lambda side: norm_mm(xin, row(weights[f + "_norm"][l]), whole_gu(f, l), BF16, "ffn_gu", side=side),
                      keys_gu, True, block, landed)
        out = carrying(lambda side: swiglu_mm_res(gu, whole(f + "_w_down", l), xin, 0.5, "ffn_down", side=side),
                       keys_down, True, block, landed)
        return out, gu

    saved = []
    kv = x_kv = None
    cur = xs
    for l in range(DEPTH):
        st = {"x0": cur}
        if l == N_A:
            x_kv = cur
            kv = norm_mm(cur, row(kv_norm), whole("w_kv"), BF16, "kv_proj")
        st["x1"], st["gu1"] = ffn_fwd(cur, "ffn1", l)
        key_in, key_out = mix_keys(l)
        proj = carrying(lambda side: norm_mm(st["x1"], row(mix_norm[l]), whole(*key_in), F32 if l < N_A else BF16,
                                             "a_proj" if l < N_A else "b_proj", side=side), [key_out], True, block, landed)
        if l < N_A:
            y_tok = gmlp_fwd(proj, row(a_v_full[l]), a_w_spatial[l], bias[l], "gmlp_fwd")
            y_mem = mem_fwd(proj, 2 * GM_W // MEM_W, mem_kv, l, "mem_fwd_a")
        else:
            st["sb_out"] = sb_fwd(proj, kv, "sb_fwd")
            y_tok = st["sb_out"].astype(BF16)
            y_mem = mem_fwd(proj, SB_W // MEM_W, mem_kv, l, "mem_fwd_b")
        st["proj"] = proj
        st["y"] = jnp.concatenate([y_tok, y_mem], axis=1)
        st["x2"] = carrying(lambda side: mm_res(st["y"], whole(*key_out), st["x1"], 1.0, "mix_out", side=side),
                            [mix_keys(l + 1)[0]] if l + 1 < DEPTH else [], True, block, landed)
        cur, st["gu2"] = ffn_fwd(st["x2"], "ffn2", l)
        saved.append(st)

    loss_blk, dx, d_final = final_loss(cur, row(final_norm), target, "final_loss")
    loss = lax.psum(loss_blk[0, 0] * (0.5 / d_model), AXES)

    grads = {n: [None] * weights[n].shape[0] for n in WEIGHTS if weights[n].ndim >= 2 and n not in ("w_kv",)}
    grads["final_norm"] = d_final.reshape(-1)
    d_mem_kv = [None] * DEPTH
    d_kv = []

    summed = {}

    def pieces(key):
        g = (grads[key[0]] if key[1] is None else grads[key[0]][key[1]]).astype(BF16)
        axis = cut_axis(key)
        cut = g.reshape(g.shape[:axis] + (N_DEV, g.shape[axis] // N_DEV) + g.shape[axis + 1:])
        return jnp.moveaxis(cut, axis, 0)

    def ffn_bwd(dx, xin, gu, f, l):
        keys_gu, keys_down = ffn_keys(ffn_after(f, l))
        if f == "ffn2" and l + 1 < DEPTH:
            keys_down = keys_down + list(mix_keys(l + 1)) + ([("w_kv", None)] if l + 1 == N_A else [])
        d_gu = carrying(lambda side: mm_nt_swiglu_bwd(dx, whole(f + "_w_down", l), gu, 0.5, "ffn_dgu", side=side),
                        keys_gu, False, pieces, summed)
        dx_new, d_gain, h = carrying(
            lambda side: mm_nt_normbwd(d_gu, whole_gu(f, l), xin, row(weights[f + "_norm"][l]), dx, "ffn_dx", side=side),
            keys_down, False, pieces, summed)
        d_wgu = mm_tn(h, d_gu, 1.0, "ffn_dwgu", tb_target=1408)
        d_wdown = swiglu_mm_tn(gu, dx, 0.5, "ffn_dwdown")
        half = d_wgu.shape[1] // 2
        grads[f + "_w_gate"][l], grads[f + "_w_up"][l] = d_wgu[:, :half], d_wgu[:, half:]
        grads[f + "_w_down"][l] = d_wdown
        grads[f + "_norm"][l] = d_gain.reshape(-1)
        return dx_new

    for l in reversed(range(DEPTH)):
        st = saved[l]
        dx = ffn_bwd(dx, st["x2"], st["gu2"], "ffn2", l)
        proj = st["proj"]
        (key_in, idx), (key_out, _) = mix_keys(l)
        w_in, w_out = whole(key_in, idx), whole(key_out, idx)
        dy = mm_nt(dx, w_out, 1.0, "mix_dy")
        grads[key_out][idx] = mm_tn(st["y"], dx, 1.0, "mix_dwout", tb_target=1024)
        if l < N_A:
            d_uv, d_ws, d_bs, d_vgain = gmlp_bwd(proj, dy, row(a_v_full[l]), a_w_spatial[l], bias[l], "gmlp_bwd")
            grads["a_w_spatial"][l], grads["a_b_spatial"][l], grads["a_v_norm"][l] = d_ws, d_bs[:, :, 0], d_vgain.reshape(-1)
            d_q, d_k, d_v = mem_bwd(proj, 2 * GM_W // MEM_W, mem_kv, l, dy, GM_W // MEM_W, "mem_bwd_a")
            d_proj = jnp.concatenate([d_uv, d_q], axis=1)
        else:
            d_qsb, d_ksb, d_vsb = sb_bwd(proj, kv, st["sb_out"], dy, "sb_bwd")
            d_kv.append(jnp.concatenate([d_ksb, d_vsb], axis=1))
            d_q, d_k, d_v = mem_bwd(proj, SB_W // MEM_W, mem_kv, l, dy, SB_W // MEM_W, "mem_bwd_b")
            d_proj = jnp.concatenate([d_qsb, d_q], axis=1)
        d_mem_kv[l] = jnp.concatenate([d_k, d_v], axis=1)
        dx, d_gain, h = mm_nt_normbwd(d_proj, w_in, st["x1"], row(mix_norm[l]), dx, "mix_dx")
        grads["mix_norm"][l] = d_gain.reshape(-1)
        grads[key_in][idx] = mm_tn(h, d_proj, 1.0, "mix_dwin")
        dx = ffn_bwd(dx, st["x0"], st["gu1"], "ffn1", l)
        if l == N_A:
            d_kv_b = sum_leading(jnp.stack(d_kv), BF16, "kv_dsum")
            dx, d_gain, h = mm_nt_normbwd(d_kv_b, whole("w_kv"), x_kv, row(kv_norm), dx, "kv_dx")
            grads["kv_norm"] = d_gain.reshape(-1)
            grads["w_kv"] = mm_tn(h, d_kv_b, 1.0, "kv_dw")

    d_mem_all = jnp.concatenate(d_mem_kv, axis=1).astype(BF16)
    _, d_gain, _ = mm_nt_normbwd(d_mem_all, w_mem_cat, mem_in, row(mem_norm), None, "mem_dnorm")
    grads["mem_norm"] = d_gain.reshape(-1)
    d_wmem = mm_tn(mem_h, d_mem_all, 1.0, "mem_dw")
    grads["w_mem_kv"] = d_wmem.reshape(d_model, DEPTH, -1).transpose(1, 0, 2)

    last = first + [mix_keys(0)[1]]
    summed.update(zip(last, exchange([pieces(k) for k in last], "all", False, "scatter_last")))
    parts = {n: summed[n, None] if (n, None) in summed else
             jnp.stack([summed[n, i] for i in range(weights[n].shape[0])], axis=1) for n in SHARDED}
    grads = {n: (jnp.stack(g) if isinstance(g, list) else g) for n, g in grads.items()}
    for n, g in zip(SMALL, _all_sum([grads[n] for n in SMALL], "sum_small")):
        parts[n] = g[None]
    parts["a_v_norm"] = lax.dynamic_slice(parts["a_v_norm"], (0, 0, dev * vn_width), (1,) + a_v_norm.shape)

    reduced, deltas, new_m, new_v = {}, {}, {}, {}
    for n in WEIGHTS:
        w = weights[n]
        view = (lambda a: a.reshape(-1, a.shape[-1]))
        res = adamw(view(w), parts[n].reshape(parts[n].shape[0], -1, w.shape[-1]), view(mom1[n]), view(mom2[n]), "adamw")
        reduced[n], deltas[n], new_m[n], new_v[n] = [r.reshape(w.shape) for r in res]

    return (loss, dx[None], *[reduced[n] for n in WEIGHTS], *[deltas[n] for n in WEIGHTS],
            *[new_m[n] for n in WEIGHTS], *[new_v[n] for n in WEIGHTS])
```

```python
import functools

import jax
import jax.numpy as jnp
from jax import lax
from jax.experimental import pallas as pl
from jax.experimental.pallas import tpu as pltpu

F32, BF16 = jnp.float32, jnp.bfloat16
MESH_ID = pl.DeviceIdType.MESH
AXES = ("x", "y", "c")
N_DEV = 8

EPS = 1e-6
DEPTH, N_A = 4, 2
GM_W, GM_GROUPS, GM_P = 768, 6, 128
MEM_W, MEM_HEADS, HEAD_DIM = 256, 4, 64
SB_W, SB_BLK = 768, 128
LANES = 128
QK_SCALE = HEAD_DIM ** -0.5
GELU_C, GELU_A = 0.7978845608028654, 0.044715

ADAM_LR, ADAM_B1, ADAM_B2, ADAM_EPS, ADAM_WD, ADAM_STEP = 0.001, 0.9, 0.999, 1e-08, 0.01, 10

VMEM_LIMIT = 56 * 1024 * 1024
PACK_COLS = 512

NT = (((1,), (1,)), ((), ()))
TN = (((0,), (0,)), ((), ()))


def _params(*sem):
    return pltpu.CompilerParams(dimension_semantics=sem, vmem_limit_bytes=VMEM_LIMIT)


def _tile(n, target, mult=LANES):
    best = None
    for t in range(mult, min(n, target) + 1, mult):
        if n % t == 0:
            best = t
    return best if best is not None else n


def _dot(a, b, dims=None):
    if dims is None:
        return jnp.dot(a, b, preferred_element_type=F32)
    return lax.dot_general(a, b, dims, preferred_element_type=F32)


def exchange(srcs, group, same_src, name, split=False):
    size = {"pair": 2, "quad": 4, "all": 8}[group]
    n = len(srcs)
    chunk_shapes = [tuple(s.shape) if same_src else tuple(s.shape[1:]) for s in srcs]
    pieces = [cs[0] if split else 1 for cs in chunk_shapes]
    n_dma = sum(pieces)

    def body(*refs):
        src_refs, out_refs = refs[:n], refs[n:2 * n]
        send_sems, recv_sems, local_sems = refs[2 * n:]
        x, y, c = lax.axis_index("x"), lax.axis_index("y"), lax.axis_index("c")
        if group == "pair":
            me, dev = c, lambda p: (x, y, p)
        elif group == "quad":
            me, dev = 2 * x + y, lambda p: (p // 2, p % 2, c)
        else:
            me, dev = 4 * x + 2 * y + c, lambda p: (p // 4, (p // 2) % 2, p % 2)

        def chunk(t, idx):
            return src_refs[t] if same_src else src_refs[t].at[idx]

        def copies(k, idx, slot, peer):
            out, w = [], k * n_dma
            for t in range(n):
                src, dst = chunk(t, idx), out_refs[t].at[slot]
                for s_ref, d_ref in ([(src.at[u], dst.at[u]) for u in range(pieces[t])] if split else [(src, dst)]):
                    out.append(pltpu.make_async_remote_copy(
                        src_ref=s_ref, dst_ref=d_ref, send_sem=send_sems.at[w], recv_sem=recv_sems.at[w],
                        device_id=dev(peer), device_id_type=MESH_ID))
                    w += 1
            return out

        local = [pltpu.make_async_copy(chunk(t, me), out_refs[t].at[me], local_sems.at[t]) for t in range(n)]
        for cp in local:
            cp.start()
        sends = []
        for k in range(1, size):
            peer = (me + k) % size
            sends += copies(k, peer, me, peer)
        for cp in sends:
            cp.start()
        for k in range(1, size):
            sender = (me + size - k) % size
            for cp in copies(k, me, sender, sender):
                cp.wait_recv()
        for cp in sends:
            cp.wait_send()
        for cp in local:
            cp.wait()

    hbm = pl.BlockSpec(memory_space=pltpu.HBM)
    return pl.pallas_call(
        body, name=name,
        out_shape=[jax.ShapeDtypeStruct((size,) + cs, s.dtype) for cs, s in zip(chunk_shapes, srcs)],
        in_specs=[hbm] * n, out_specs=[hbm] * n,
        scratch_shapes=[pltpu.SemaphoreType.DMA((size * n_dma,)), pltpu.SemaphoreType.DMA((size * n_dma,)),
                        pltpu.SemaphoreType.DMA((n,))],
    )(*srcs)


class Side:
    def __init__(self, srcs, same_src):
        self.srcs, self.same_src, self.n = list(srcs), same_src, len(srcs)
        self.chunk_shapes = [tuple(s.shape) if same_src else tuple(s.shape[1:]) for s in srcs]

    def out_shapes(self):
        return [jax.ShapeDtypeStruct((N_DEV,) + cs, s.dtype) for cs, s in zip(self.chunk_shapes, self.srcs)]

    def scratch(self):
        return [pltpu.SemaphoreType.DMA((N_DEV * self.n,)), pltpu.SemaphoreType.DMA((N_DEV * self.n,))]

    def _copies(self, src_refs, land_refs, send_sems, recv_sems, outgoing):
        me = 4 * lax.axis_index("x") + 2 * lax.axis_index("y") + lax.axis_index("c")
        out = []
        for k in range(1, N_DEV):
            peer = (me + k) % N_DEV if outgoing else (me + N_DEV - k) % N_DEV
            for t in range(self.n):
                src = src_refs[t] if self.same_src else src_refs[t].at[peer if outgoing else me]
                out.append(pltpu.make_async_remote_copy(
                    src_ref=src, dst_ref=land_refs[t].at[me if outgoing else peer],
                    send_sem=send_sems.at[k * self.n + t], recv_sem=recv_sems.at[k * self.n + t],
                    device_id=(peer // 4, (peer // 2) % 2, peer % 2), device_id_type=MESH_ID))
        return out

    def _own(self, src_refs, land_refs, send_sems):
        me = 4 * lax.axis_index("x") + 2 * lax.axis_index("y") + lax.axis_index("c")
        return [pltpu.make_async_copy(src_refs[t] if self.same_src else src_refs[t].at[me], land_refs[t].at[me],
                                      send_sems.at[t]) for t in range(self.n)]

    def start(self, src_refs, land_refs, send_sems, recv_sems):
        for cp in self._own(src_refs, land_refs, send_sems) + self._copies(src_refs, land_refs, send_sems, recv_sems, True):
            cp.start()

    def wait(self, src_refs, land_refs, send_sems, recv_sems):
        for cp in self._copies(src_refs, land_refs, send_sems, recv_sems, False):
            cp.wait_recv()
        for cp in self._copies(src_refs, land_refs, send_sems, recv_sems, True):
            cp.wait_send()
        for cp in self._own(src_refs, land_refs, send_sems):
            cp.wait()


def _call(body, side, name, grid, in_specs, out_specs, out_shape, scratch_shapes, dims, args):
    if side is None:
        res = pl.pallas_call(body, name=name, grid=grid, in_specs=in_specs, out_specs=out_specs, out_shape=out_shape,
                             scratch_shapes=scratch_shapes, compiler_params=_params(*dims))(*args)
        return list(res), []
    n_in, n_out, n_scr, ns = len(in_specs), len(out_specs), len(scratch_shapes), side.n

    def wrapped(*refs):
        ins, srcs = refs[:n_in], refs[n_in:n_in + ns]
        outs, lands = refs[n_in + ns:n_in + ns + n_out], refs[n_in + ns + n_out:n_in + 2 * ns + n_out]
        scratch, (send_sems, recv_sems) = refs[n_in + 2 * ns + n_out:n_in + 2 * ns + n_out + n_scr], refs[-2:]
        first, last = None, None
        for axis, steps in enumerate(grid):
            i = pl.program_id(axis)
            first = (i == 0) if first is None else first & (i == 0)
            last = (i == steps - 1) if last is None else last & (i == steps - 1)

        @pl.when(first)
        def _():
            side.start(srcs, lands, send_sems, recv_sems)

        body(*ins, *outs, *scratch)

        @pl.when(last)
        def _():
            side.wait(srcs, lands, send_sems, recv_sems)

    hbm = pl.BlockSpec(memory_space=pltpu.HBM)
    res = pl.pallas_call(
        wrapped, name=name, grid=grid, in_specs=list(in_specs) + [hbm] * ns, out_specs=list(out_specs) + [hbm] * ns,
        out_shape=list(out_shape) + side.out_shapes(), scratch_shapes=list(scratch_shapes) + side.scratch(),
        compiler_params=_params(*dims))(*args, *side.srcs)
    return list(res[:n_out]), list(res[n_out:])


def sum_leading(parts, out_dtype, name):
    k, rows, cols = parts.shape
    tr = _tile(rows, 512, 16)

    def body(p_ref, o_ref):
        acc = p_ref[0].astype(F32)
        for s in range(1, k):
            acc = acc + p_ref[s].astype(F32)
        o_ref[...] = acc.astype(o_ref.dtype)

    return pl.pallas_call(
        body, name=name, grid=(rows // tr,),
        in_specs=[pl.BlockSpec((k, tr, cols), lambda i: (0, i, 0))],
        out_specs=pl.BlockSpec((tr, cols), lambda i: (i, 0)),
        out_shape=jax.ShapeDtypeStruct((rows, cols), out_dtype),
        compiler_params=_params("arbitrary"),
    )(parts)


def _rms(xf):
    return lax.rsqrt(jnp.mean(xf * xf, axis=-1, keepdims=True) + EPS)


def norm_mm(x, g, w, out_dtype, name, emit_h=False, side=None):
    m, d = x.shape
    n = w.shape[1]
    tm, tn = _tile(m, 1024, 8), _tile(n, 1408)

    def body(x_ref, g_ref, w_ref, o_ref, *rest):
        h_ref = rest[-1]

        @pl.when(pl.program_id(1) == 0)
        def _():
            xf = x_ref[...]
            hb = ((xf * _rms(xf)) * g_ref[...]).astype(BF16)
            h_ref[...] = hb
            if emit_h:
                rest[0][...] = hb

        o_ref[...] = _dot(h_ref[...], w_ref[...]).astype(o_ref.dtype)

    out_shape = [jax.ShapeDtypeStruct((m, n), out_dtype)]
    out_specs = [pl.BlockSpec((tm, tn), lambda i, j: (i, j))]
    if emit_h:
        out_shape.append(jax.ShapeDtypeStruct((m, d), BF16))
        out_specs.append(pl.BlockSpec((tm, d), lambda i, j: (i, 0)))
    res, landed = _call(
        body, side, name, (m // tm, n // tn),
        [pl.BlockSpec((tm, d), lambda i, j: (i, 0)), pl.BlockSpec((1, d), lambda i, j: (0, 0)),
         pl.BlockSpec((d, tn), lambda i, j: (0, j))],
        out_specs, out_shape, [pltpu.VMEM((tm, d), BF16)], ("arbitrary", "arbitrary"), (x, g, w))
    out = res if emit_h else res[0]
    return out if side is None else (out, landed)


def mm_res(a, w, res, alpha, name, side=None):
    m, k = a.shape
    n = w.shape[1]
    tm, tn = _tile(m, 1024, 8), _tile(n, 1024)

    def body(a_ref, w_ref, r_ref, o_ref):
        o_ref[...] = r_ref[...] + alpha * _dot(a_ref[...], w_ref[...])

    out, landed = _call(
        body, side, name, (m // tm, n // tn),
        [pl.BlockSpec((tm, k), lambda i, j: (i, 0)), pl.BlockSpec((k, tn), lambda i, j: (0, j)),
         pl.BlockSpec((tm, tn), lambda i, j: (i, j))],
        [pl.BlockSpec((tm, tn), lambda i, j: (i, j))], [jax.ShapeDtypeStruct((m, n), F32)], [],
        ("arbitrary", "arbitrary"), (a, w, res))
    return out[0] if side is None else (out[0], landed)


def mm_nt(x, w, alpha, name):
    m, d = x.shape
    n = w.shape[0]
    tm, tn = _tile(m, 1024, 8), _tile(n, 1408)

    def body(x_ref, w_ref, o_ref, xb_ref):
        @pl.when(pl.program_id(1) == 0)
        def _():
            xb_ref[...] = x_ref[...].astype(BF16)

        o_ref[...] = (alpha * _dot(xb_ref[...], w_ref[...], NT)).astype(o_ref.dtype)

    return pl.pallas_call(
        body, name=name, grid=(m // tm, n // tn),
        in_specs=[pl.BlockSpec((tm, d), lambda i, j: (i, 0)), pl.BlockSpec((tn, d), lambda i, j: (j, 0))],
        out_specs=pl.BlockSpec((tm, tn), lambda i, j: (i, j)),
        out_shape=jax.ShapeDtypeStruct((m, n), BF16),
        scratch_shapes=[pltpu.VMEM((tm, d), BF16)],
        compiler_params=_params("arbitrary", "arbitrary"),
    )(x, w)


def mm_tn(a, b, alpha, name, ta_target=1024, tb_target=512, side=None):
    s, ka = a.shape
    nb = b.shape[1]
    ta, tb, ts = _tile(ka, ta_target), _tile(nb, tb_target), _tile(s, 1024, 16)
    steps = s // ts

    def body(a_ref, b_ref, o_ref, acc_ref):
        t = pl.program_id(2)

        @pl.when(t == 0)
        def _():
            acc_ref[...] = jnp.zeros_like(acc_ref)

        acc_ref[...] += _dot(a_ref[...].astype(BF16), b_ref[...].astype(BF16), TN)

        @pl.when(t == steps - 1)
        def _():
            o_ref[...] = (alpha * acc_ref[...]).astype(o_ref.dtype)

    res, landed = _call(
        body, side, name, (ka // ta, nb // tb, steps),
        [pl.BlockSpec((ts, ta), lambda i, j, t: (t, i)), pl.BlockSpec((ts, tb), lambda i, j, t: (t, j))],
        [pl.BlockSpec((ta, tb), lambda i, j, t: (i, j))], [jax.ShapeDtypeStruct((ka, nb), BF16)],
        [pltpu.VMEM((ta, tb), F32)], ("arbitrary", "arbitrary", "arbitrary"), (a, b))
    return res[0] if side is None else (res[0], landed)


def mm_nt_normbwd(dy, w, x, g, res, name, side=None):
    m, n = dy.shape
    d = w.shape[0]
    tm, tk = _tile(m, 1024, 8), _tile(n, 1408)
    steps = n // tk
    has_res = res is not None

    def body(*refs):
        if has_res:
            dy_ref, w_ref, x_ref, g_ref, r_ref, dx_ref, dg_ref, h_ref, acc_ref = refs
        else:
            dy_ref, w_ref, x_ref, g_ref, dx_ref, dg_ref, h_ref, acc_ref = refs
        i, t = pl.program_id(0), pl.program_id(1)

        @pl.when(t == 0)
        def _():
            acc_ref[...] = jnp.zeros_like(acc_ref)

        @pl.when((t == 0) & (i == 0))
        def _():
            dg_ref[...] = jnp.zeros_like(dg_ref)

        acc_ref[...] += _dot(dy_ref[...], w_ref[...], NT)

        @pl.when(t == steps - 1)
        def _():
            xf = x_ref[...]
            r = _rms(xf)
            xhat = xf * r
            dh = acc_ref[...]
            gain = g_ref[...]
            dg_ref[...] += jnp.sum(dh * xhat, axis=0, keepdims=True)
            dxhat = dh * gain
            dx = r * (dxhat - xhat * jnp.mean(dxhat * xhat, axis=-1, keepdims=True))
            dx_ref[...] = (r_ref[...] + dx) if has_res else dx
            h_ref[...] = (xhat * gain).astype(BF16)

    row = lambda i, t: (i, 0)
    in_specs = [pl.BlockSpec((tm, tk), lambda i, t: (i, t)), pl.BlockSpec((d, tk), lambda i, t: (0, t)),
                pl.BlockSpec((tm, d), row), pl.BlockSpec((1, d), lambda i, t: (0, 0))]
    args = [dy, w, x, g]
    if has_res:
        in_specs.append(pl.BlockSpec((tm, d), row))
        args.append(res)
    res, landed = _call(
        body, side, name, (m // tm, steps), in_specs,
        [pl.BlockSpec((tm, d), row), pl.BlockSpec((1, d), lambda i, t: (0, 0)), pl.BlockSpec((tm, d), row)],
        [jax.ShapeDtypeStruct((m, d), F32), jax.ShapeDtypeStruct((1, d), F32), jax.ShapeDtypeStruct((m, d), BF16)],
        [pltpu.VMEM((tm, d), F32)], ("arbitrary", "arbitrary"), args)
    return res if side is None else (res, landed)


def _sigmoid(z):
    return 1.0 / (1.0 + jnp.exp(-z))


def _swiglu(gate_b, up_b):
    gate = gate_b.astype(F32)
    return (gate * _sigmoid(gate) * up_b.astype(F32)).astype(BF16)


def swiglu_mm_res(gu, w, res, alpha, name, side=None):
    m, f2 = gu.shape
    f, n = w.shape
    tm, tc = _tile(m, 256, 16), _tile(f, 256)

    def body(gu_ref, w_ref, r_ref, o_ref):
        acc = jnp.zeros((tm, n), F32)
        for c0 in range(0, f, tc):
            act = _swiglu(gu_ref[:, c0:c0 + tc], gu_ref[:, f + c0:f + c0 + tc])
            acc = acc + _dot(act, w_ref[c0:c0 + tc, :])
        o_ref[...] = r_ref[...] + alpha * acc

    out, landed = _call(
        body, side, name, (m // tm,),
        [pl.BlockSpec((tm, f2), lambda i: (i, 0)), pl.BlockSpec((f, n), lambda i: (0, 0)),
         pl.BlockSpec((tm, n), lambda i: (i, 0))],
        [pl.BlockSpec((tm, n), lambda i: (i, 0))], [jax.ShapeDtypeStruct((m, n), F32)], [], ("arbitrary",), (gu, w, res))
    return out[0] if side is None else (out[0], landed)


def swiglu_mm_tn(gu, b, alpha, name, side=None):
    s, f2 = gu.shape
    f, n = f2 // 2, b.shape[1]
    ta, ts = _tile(f, 1408), _tile(s, 512, 16)
    steps, half = s // ts, f // ta

    def body(g_ref, u_ref, b_ref, o_ref, acc_ref):
        t = pl.program_id(1)

        @pl.when(t == 0)
        def _():
            acc_ref[...] = jnp.zeros_like(acc_ref)

        bb = b_ref[...].astype(BF16)
        for c0 in range(0, ta, LANES):
            acc_ref[c0:c0 + LANES, :] += _dot(_swiglu(g_ref[:, c0:c0 + LANES], u_ref[:, c0:c0 + LANES]), bb, TN)

        @pl.when(t == steps - 1)
        def _():
            o_ref[...] = (alpha * acc_ref[...]).astype(o_ref.dtype)

    res, landed = _call(
        body, side, name, (half, steps),
        [pl.BlockSpec((ts, ta), lambda i, t: (t, i)), pl.BlockSpec((ts, ta), lambda i, t: (t, half + i)),
         pl.BlockSpec((ts, n), lambda i, t: (t, 0))],
        [pl.BlockSpec((ta, n), lambda i, t: (i, 0))], [jax.ShapeDtypeStruct((f, n), BF16)],
        [pltpu.VMEM((ta, n), F32)], ("arbitrary", "arbitrary"), (gu, gu, b))
    return res[0] if side is None else (res[0], landed)


def mm_nt_swiglu_bwd(x, w, gu, alpha, name, side=None):
    m, d = x.shape
    f = w.shape[0]
    tm, tc = _tile(m, 256, 16), _tile(f, 256)

    def body(x_ref, w_ref, gu_ref, o_ref):
        xb = x_ref[...].astype(BF16)
        for c0 in range(0, f, tc):
            d_act = alpha * _dot(xb, w_ref[c0:c0 + tc, :], NT)
            gate, up = gu_ref[:, c0:c0 + tc].astype(F32), gu_ref[:, f + c0:f + c0 + tc].astype(F32)
            sg = _sigmoid(gate)
            o_ref[:, c0:c0 + tc] = (d_act * up * (sg * (1.0 + gate * (1.0 - sg)))).astype(BF16)
            o_ref[:, f + c0:f + c0 + tc] = (d_act * (gate * sg)).astype(BF16)

    res, landed = _call(
        body, side, name, (m // tm,),
        [pl.BlockSpec((tm, d), lambda i: (i, 0)), pl.BlockSpec((f, d), lambda i: (0, 0)),
         pl.BlockSpec((tm, 2 * f), lambda i: (i, 0))],
        [pl.BlockSpec((tm, 2 * f), lambda i: (i, 0))], [jax.ShapeDtypeStruct((m, 2 * f), BF16)], [], ("arbitrary",),
        (x, w, gu))
    return res[0] if side is None else (res[0], landed)


def _gelu(x):
    return 0.5 * x * (1.0 + jnp.tanh(GELU_C * (x + GELU_A * x * x * x)))


def _gelu_grad(x):
    t = jnp.tanh(GELU_C * (x + GELU_A * x * x * x))
    return 0.5 * (1.0 + t) + 0.5 * x * (1.0 - t * t) * (GELU_C * (1.0 + 3.0 * GELU_A * x * x))


def _chunk_mask():
    row = lax.broadcasted_iota(jnp.int32, (GM_P, GM_P), 0)
    col = lax.broadcasted_iota(jnp.int32, (GM_P, GM_P), 1)
    return (col < GM_P // 2) | (row >= GM_P // 2)


def gmlp_fwd(proj, gain, w_s, bias, name):
    s, pw = proj.shape
    tm = _tile(s, 256, GM_P)

    def body(p_ref, gain_ref, w_ref, b_ref, o_ref):
        mask = _chunk_mask()
        u = _gelu(p_ref[:, :GM_W])
        v = _gelu(p_ref[:, GM_W:2 * GM_W])
        vn = ((v * _rms(v)) * gain_ref[...]).astype(BF16)
        for g in range(GM_GROUPS):
            wg = jnp.where(mask, w_ref[g], 0.0).astype(BF16)
            cols = slice(g * GM_P, (g + 1) * GM_P)
            for n in range(tm // GM_P):
                rows = slice(n * GM_P, (n + 1) * GM_P)
                mixed = _dot(wg, vn[rows, cols]) + b_ref[:, cols]
                o_ref[rows, cols] = (u[rows, cols] * mixed).astype(BF16)

    return pl.pallas_call(
        body, name=name, grid=(s // tm,),
        in_specs=[pl.BlockSpec((tm, pw), lambda i: (i, 0)), pl.BlockSpec((1, GM_W), lambda i: (0, 0)),
                  pl.BlockSpec((GM_GROUPS, GM_P, GM_P), lambda i: (0, 0, 0)), pl.BlockSpec((GM_P, GM_W), lambda i: (0, 0))],
        out_specs=pl.BlockSpec((tm, GM_W), lambda i: (i, 0)),
        out_shape=jax.ShapeDtypeStruct((s, GM_W + MEM_W), BF16), compiler_params=_params("arbitrary"),
    )(proj, gain, w_s, bias)


def gmlp_bwd(proj, dy, gain, w_s, bias, name):
    s, pw = proj.shape
    dw_total = dy.shape[1]
    tm = _tile(s, 256, GM_P)

    def body(p_ref, dy_ref, gain_ref, w_ref, b_ref, dp_ref, dw_ref, db_ref, dgain_ref, dvn_ref):
        @pl.when(pl.program_id(0) == 0)
        def _():
            dw_ref[...] = jnp.zeros_like(dw_ref)
            db_ref[...] = jnp.zeros_like(db_ref)
            dgain_ref[...] = jnp.zeros_like(dgain_ref)

        mask = _chunk_mask()
        pu = p_ref[:, :GM_W]
        pv = p_ref[:, GM_W:2 * GM_W]
        u = _gelu(pu)
        v = _gelu(pv)
        r = _rms(v)
        vhat = v * r
        gain = gain_ref[...]
        vn = (vhat * gain).astype(BF16)
        gu_grad = _gelu_grad(pu)
        for g in range(GM_GROUPS):
            wg = jnp.where(mask, w_ref[g], 0.0).astype(BF16)
            cols = slice(g * GM_P, (g + 1) * GM_P)
            dw_acc = jnp.zeros((GM_P, GM_P), F32)
            db_acc = jnp.zeros((GM_P, 1), F32)
            for n in range(tm // GM_P):
                rows = slice(n * GM_P, (n + 1) * GM_P)
                dyb = dy_ref[rows, cols].astype(F32)
                vnb = vn[rows, cols]
                mixed = _dot(wg, vnb) + b_ref[:, cols]
                dmixed = dyb * u[rows, cols]
                dmb = dmixed.astype(BF16)
                dp_ref[rows, cols] = (dyb * mixed * gu_grad[rows, cols]).astype(BF16)
                dw_acc = dw_acc + _dot(dmb, vnb, NT)
                db_acc = db_acc + jnp.sum(dmixed, axis=1, keepdims=True)
                dvn_ref[rows, cols] = _dot(wg, dmb, TN)
            dw_ref[g] += jnp.where(mask, dw_acc, 0.0)
            db_ref[g] += jnp.broadcast_to(db_acc, (GM_P, GM_P))
        dvn = dvn_ref[...]
        dgain_ref[...] += jnp.sum(dvn * vhat, axis=0, keepdims=True)
        dvhat = dvn * gain
        dv = r * (dvhat - vhat * jnp.mean(dvhat * vhat, axis=-1, keepdims=True))
        dp_ref[:, GM_W:] = (dv * _gelu_grad(pv)).astype(BF16)

    const3 = lambda i: (0, 0, 0)
    return pl.pallas_call(
        body, name=name, grid=(s // tm,),
        in_specs=[pl.BlockSpec((tm, pw), lambda i: (i, 0)), pl.BlockSpec((tm, dw_total), lambda i: (i, 0)),
                  pl.BlockSpec((1, GM_W), lambda i: (0, 0)), pl.BlockSpec((GM_GROUPS, GM_P, GM_P), const3),
                  pl.BlockSpec((GM_P, GM_W), lambda i: (0, 0))],
        out_specs=[pl.BlockSpec((tm, 2 * GM_W), lambda i: (i, 0)), pl.BlockSpec((GM_GROUPS, GM_P, GM_P), const3),
                   pl.BlockSpec((GM_GROUPS, GM_P, GM_P), const3), pl.BlockSpec((1, GM_W), lambda i: (0, 0))],
        out_shape=[jax.ShapeDtypeStruct((s, pw), BF16), jax.ShapeDtypeStruct((GM_GROUPS, GM_P, GM_P), F32),
                   jax.ShapeDtypeStruct((GM_GROUPS, GM_P, GM_P), F32), jax.ShapeDtypeStruct((1, GM_W), F32)],
        scratch_shapes=[pltpu.VMEM((tm, GM_W), F32)],
        compiler_params=_params("arbitrary"),
    )(proj, dy, gain, w_s, bias)


def _keep(mask, xb):
    return jnp.where(mask, xb.astype(F32), 0.0).astype(BF16)


def _head_masks(rows, width, heads):
    lane = lax.broadcasted_iota(jnp.int32, (rows, width), 1)
    return [(lane >= HEAD_DIM * h) & (lane < HEAD_DIM * (h + 1)) for h in range(heads)]


def _mem_probs(qh, k):
    sc = _dot(qh, k, NT) * QK_SCALE
    e = jnp.exp(sc - jnp.max(sc, axis=-1, keepdims=True))
    return e / jnp.sum(e, axis=-1, keepdims=True)


def mem_fwd(proj, q_blk, mem_kv, layer, into, into_blk, name):
    s = proj.shape[0]
    n_mem = mem_kv.shape[0]
    tm = _tile(s, 512, 16)

    def body(q_ref, k_ref, v_ref, into_ref, o_ref):
        q = q_ref[...].astype(BF16)
        k, v = k_ref[...], v_ref[...]
        out = jnp.zeros((tm, MEM_W), F32)
        for hm in _head_masks(tm, MEM_W, MEM_HEADS):
            p = _mem_probs(_keep(hm, q), k)
            out = out + jnp.where(hm, _dot(p.astype(BF16), v), 0.0)
        o_ref[...] = out.astype(BF16)

    return pl.pallas_call(
        body, name=name, grid=(s // tm,),
        in_specs=[pl.BlockSpec((tm, MEM_W), lambda i: (i, q_blk)), pl.BlockSpec((n_mem, MEM_W), lambda i: (0, 2 * layer)),
                  pl.BlockSpec((n_mem, MEM_W), lambda i: (0, 2 * layer + 1)), pl.BlockSpec(memory_space=pl.ANY)],
        out_specs=pl.BlockSpec((tm, MEM_W), lambda i: (i, into_blk)),
        out_shape=jax.ShapeDtypeStruct(into.shape, BF16), input_output_aliases={3: 0},
        compiler_params=_params("arbitrary"),
    )(proj, mem_kv, mem_kv, into)


def mem_bwd(proj, q_blk, mem_kv, layer, dy, dy_blk, into, name):
    s = proj.shape[0]
    n_mem = mem_kv.shape[0]
    tm = _tile(s, 512, 16)

    def body(q_ref, k_ref, v_ref, dy_ref, into_ref, dq_ref, dk_ref, dv_ref):
        @pl.when(pl.program_id(0) == 0)
        def _():
            dk_ref[...] = jnp.zeros_like(dk_ref)
            dv_ref[...] = jnp.zeros_like(dv_ref)

        q = q_ref[...].astype(BF16)
        k, v = k_ref[...], v_ref[...]
        dy = dy_ref[...]
        dq = jnp.zeros((tm, MEM_W), F32)
        dk = jnp.zeros((n_mem, MEM_W), F32)
        dv = jnp.zeros((n_mem, MEM_W), F32)
        for hm in _head_masks(tm, MEM_W, MEM_HEADS):
            qh = _keep(hm, q)
            dyh = _keep(hm, dy)
            p = _mem_probs(qh, k)
            dp = _dot(dyh, v, NT)
            dv = dv + _dot(p.astype(BF16), dyh, TN)
            ds = (p * (dp - jnp.sum(dp * p, axis=-1, keepdims=True)) * QK_SCALE).astype(BF16)
            dq = dq + jnp.where(hm, _dot(ds, k), 0.0)
            dk = dk + _dot(ds, qh, TN)
        dq_ref[...] = dq.astype(BF16)
        dk_ref[...] += dk
        dv_ref[...] += dv

    const = lambda i: (0, 0)
    return pl.pallas_call(
        body, name=name, grid=(s // tm,),
        in_specs=[pl.BlockSpec((tm, MEM_W), lambda i: (i, q_blk)), pl.BlockSpec((n_mem, MEM_W), lambda i: (0, 2 * layer)),
                  pl.BlockSpec((n_mem, MEM_W), lambda i: (0, 2 * layer + 1)), pl.BlockSpec((tm, MEM_W), lambda i: (i, dy_blk)),
                  pl.BlockSpec(memory_space=pl.ANY)],
        out_specs=[pl.BlockSpec((tm, MEM_W), lambda i: (i, q_blk)), pl.BlockSpec((n_mem, MEM_W), const),
                   pl.BlockSpec((n_mem, MEM_W), const)],
        out_shape=[jax.ShapeDtypeStruct(into.shape, BF16), jax.ShapeDtypeStruct((n_mem, MEM_W), F32),
                   jax.ShapeDtypeStruct((n_mem, MEM_W), F32)],
        input_output_aliases={4: 0}, compiler_params=_params("arbitrary"),
    )(proj, mem_kv, mem_kv, dy, into)


SB_KEYS = 256
SB_SUB = SB_KEYS // SB_BLK
SB_QROWS = 256
SB_QB = SB_QROWS // SB_BLK
SB_CHAINS = 2 * SB_QB
SB_DEAD = -110.0


def _split(xf):
    hi = xf.astype(BF16)
    return hi, (xf - hi.astype(F32)).astype(BF16)


def _sb_consts():
    row = lax.bitwise_and(lax.broadcasted_iota(jnp.int32, (2 * SB_BLK, 2 * SB_BLK), 0), SB_BLK - 1)
    col = lax.broadcasted_iota(jnp.int32, (2 * SB_BLK, 2 * SB_BLK), 1)
    ones = col >= SB_BLK
    after2 = jnp.where(ones | (row > col), -1.0, 0.0).astype(BF16)
    from2 = jnp.where(ones | (row >= col), 1.0, 0.0).astype(BF16)
    r = lax.broadcasted_iota(jnp.int32, (SB_BLK, SB_BLK), 0)
    c = lax.broadcasted_iota(jnp.int32, (SB_BLK, SB_BLK), 1)
    return after2, from2, c - r, [c < HEAD_DIM, c >= HEAD_DIM]


def _suffix(xf, tri2):
    hi, lo = _split(xf)
    return _dot(jnp.concatenate([hi, lo], axis=1), tri2)


def _sb_logs(z, mask):
    softplus = jnp.maximum(z, 0.0) + jnp.log(1.0 + jnp.exp(-jnp.abs(z)))
    log_beta = z - softplus
    if mask is not None:
        softplus = jnp.where(mask, softplus, 0.0)
    return softplus, log_beta


def _sb_queries(q_ref, heads):
    q = q_ref[...].astype(F32) * QK_SCALE
    return [jnp.where(hm, q[r * SB_BLK:(r + 1) * SB_BLK], 0.0).astype(BF16) for r in range(SB_QB) for hm in heads]


def _sb_walk(i, block, state):
    places = SB_SUB // SB_QB
    assert places in (1, 2)
    own = lax.shift_right_logical(i * SB_QB, SB_SUB.bit_length() - 1)
    firsts = [[(v * SB_QB + r) * SB_BLK for r in range(SB_QB) for _ in range(2)] for v in range(places)]
    if places == 1:
        state = block(own, state, firsts[0])
    else:
        state = lax.cond(lax.bitwise_and(i, 1) == 0, lambda st: block(own, st, firsts[0]),
                         lambda st: block(own, st, firsts[1]), state)

    def live(carry):
        j, st = carry
        most = st[0][0]
        for run in st[0][1:]:
            most = jnp.maximum(most, run)
        return (j >= 0) & (jnp.max(most) > SB_DEAD)

    return lax.while_loop(live, lambda carry: (carry[0] - 1, block(carry[0], carry[1], None)), (own - 1, state))[1]


def _sb_tiles(first):
    out = []
    for c in reversed(range(SB_SUB)):
        for n in range(SB_CHAINS):
            if first is None or c * SB_BLK < first[n]:
                out.append((c, n, "before"))
            elif c * SB_BLK == first[n]:
                out.append((c, n, "diagonal"))
    return out


def _sb_heads_apart(stacked, heads, r):
    return jnp.where(heads[0], stacked[2 * r * SB_BLK:(2 * r + 1) * SB_BLK],
                     stacked[(2 * r + 1) * SB_BLK:(2 * r + 2) * SB_BLK])


def sb_fwd(proj, kv, name):
    s = proj.shape[0]
    assert s % SB_KEYS == 0 and SB_KEYS % SB_QROWS == 0

    def body(q_ref, k_ref, v_ref, o_ref, y_ref):
        after2, _, col_minus_row, heads = _sb_consts()
        q_all = jnp.concatenate(_sb_queries(q_ref, heads), axis=0)
        key_before_query = col_minus_row < 0

        def block(j, state, first):
            runs, acc = list(state[0]), state[1]
            rows = pl.ds(pl.multiple_of(j * SB_KEYS, SB_KEYS), SB_KEYS)
            kb, vb = k_ref[rows, :], v_ref[rows, :]
            z = _dot(q_all, kb, NT)
            pend = {}
            parts = [[jnp.zeros((SB_BLK, SB_BLK), BF16)] * SB_SUB for _ in range(SB_CHAINS)]
            for c, n, where in _sb_tiles(first):
                mask = key_before_query if where == "diagonal" else None
                softplus, lb = _sb_logs(z[n * SB_BLK:(n + 1) * SB_BLK, c * SB_BLK:(c + 1) * SB_BLK], mask)
                pend[c, n] = (lb, _suffix(softplus, after2), mask)
            for c, n, _ in _sb_tiles(first):
                lb, r, mask = pend.pop((c, n))
                a = jnp.exp(lb + r[:, :SB_BLK] + runs[n])
                if mask is not None:
                    a = jnp.where(mask, a, 0.0)
                parts[n][c] = a.astype(BF16)
                runs[n] = runs[n] + r[:, SB_BLK:]
            a_all = jnp.concatenate([jnp.concatenate(p, axis=1) for p in parts], axis=0)
            return tuple(runs), acc + _dot(a_all, vb)

        zero = jnp.zeros((SB_BLK, LANES), F32)
        state = _sb_walk(pl.program_id(1), block, ((zero,) * SB_CHAINS, jnp.zeros((SB_CHAINS * SB_BLK, LANES), F32)))
        for r in range(SB_QB):
            out = _sb_heads_apart(state[1], heads, r)
            o_ref[r * SB_BLK:(r + 1) * SB_BLK, :] = out
            y_ref[r * SB_BLK:(r + 1) * SB_BLK, :] = out.astype(BF16)

    pairs = SB_W // LANES
    block_spec = pl.BlockSpec((SB_QROWS, LANES), lambda p, i: (i, p))
    return pl.pallas_call(
        body, name=name, grid=(pairs, s // SB_QROWS),
        in_specs=[block_spec, pl.BlockSpec((s, LANES), lambda p, i: (0, p)),
                  pl.BlockSpec((s, LANES), lambda p, i: (0, pairs + p))],
        out_specs=[block_spec, block_spec],
        out_shape=[jax.ShapeDtypeStruct((s, SB_W), F32), jax.ShapeDtypeStruct((s, SB_W + MEM_W), BF16)],
        compiler_params=_params("arbitrary", "arbitrary"),
    )(proj, kv, kv)


def sb_bwd(proj, kv, out, dy, name):
    s = proj.shape[0]

    def body(q_ref, k_ref, v_ref, o_ref, do_ref, dq_ref, dk_ref, dv_ref):
        i = pl.program_id(1)

        @pl.when(i == 0)
        def _():
            dk_ref[...] = jnp.zeros_like(dk_ref)
            dv_ref[...] = jnp.zeros_like(dv_ref)

        after2, from2, col_minus_row, heads = _sb_consts()
        q_all = jnp.concatenate(_sb_queries(q_ref, heads), axis=0)
        key_before_query = col_minus_row < 0
        d_out = do_ref[...].astype(F32)
        prod = d_out * o_ref[...]
        dos, totals = [], []
        for r in range(SB_QB):
            rr = slice(r * SB_BLK, (r + 1) * SB_BLK)
            for hm in heads:
                dos.append(jnp.where(hm, d_out[rr], 0.0).astype(BF16))
                totals.append(jnp.broadcast_to(jnp.sum(jnp.where(hm, prod[rr], 0.0), axis=1, keepdims=True),
                                               (SB_BLK, SB_BLK)))
        do_all = jnp.concatenate(dos, axis=0)

        def block(j, state, first):
            runs, seens, dq = list(state[0]), list(state[1]), state[2]
            rows = pl.ds(pl.multiple_of(j * SB_KEYS, SB_KEYS), SB_KEYS)
            kb, vb = k_ref[rows, :], v_ref[rows, :]
            z = _dot(q_all, kb, NT)
            da = _dot(do_all, vb, NT)
            pend, pend2 = {}, {}
            a_parts = [[jnp.zeros((SB_BLK, SB_BLK), BF16)] * SB_SUB for _ in range(SB_CHAINS)]
            dz_parts = [[jnp.zeros((SB_BLK, SB_BLK), BF16)] * SB_SUB for _ in range(SB_CHAINS)]
            for c, n, where in _sb_tiles(first):
                mask = key_before_query if where == "diagonal" else None
                softplus, lb = _sb_logs(z[n * SB_BLK:(n + 1) * SB_BLK, c * SB_BLK:(c + 1) * SB_BLK], mask)
                pend[c, n] = (softplus, lb, _suffix(softplus, after2), mask)
            for c, n, _ in _sb_tiles(first):
                softplus, lb, r, mask = pend.pop((c, n))
                a = jnp.exp(lb + r[:, :SB_BLK] + runs[n])
                if mask is not None:
                    a = jnp.where(mask, a, 0.0)
                runs[n] = runs[n] + r[:, SB_BLK:]
                ab = a.astype(BF16)
                a_parts[n][c] = ab
                dl = ab.astype(F32) * da[n * SB_BLK:(n + 1) * SB_BLK, c * SB_BLK:(c + 1) * SB_BLK]
                pend2[c, n] = (softplus, lb, dl, _suffix(dl, from2), mask)
            for c, n, _ in _sb_tiles(first):
                softplus, lb, dl, r2, mask = pend2.pop((c, n))
                d_lom = totals[n] - (r2[:, :SB_BLK] + seens[n])
                if mask is not None:
                    d_lom = jnp.where(mask, d_lom, 0.0)
                seens[n] = seens[n] + r2[:, SB_BLK:]
                dz_parts[n][c] = (dl * jnp.exp(-softplus) - d_lom * jnp.exp(lb)).astype(BF16)
            a_all = jnp.concatenate([jnp.concatenate(p, axis=1) for p in a_parts], axis=0)
            dz_all = jnp.concatenate([jnp.concatenate(p, axis=1) for p in dz_parts], axis=0)
            dv_ref[rows, :] += _dot(a_all, do_all, TN)
            dk_ref[rows, :] += _dot(dz_all, q_all, TN)
            return tuple(runs), tuple(seens), dq + _dot(dz_all, kb)

        zero = jnp.zeros((SB_BLK, LANES), F32)
        state = _sb_walk(i, block, ((zero,) * SB_CHAINS, (zero,) * SB_CHAINS,
                                    jnp.zeros((SB_CHAINS * SB_BLK, LANES), F32)))
        for r in range(SB_QB):
            dq_ref[r * SB_BLK:(r + 1) * SB_BLK, :] = (_sb_heads_apart(state[2], heads, r) * QK_SCALE).astype(BF16)

    pairs = SB_W // LANES
    blk = lambda p, i: (i, p)
    col = lambda p, i: (0, p)
    return pl.pallas_call(
        body, name=name, grid=(pairs, s // SB_QROWS),
        in_specs=[pl.BlockSpec((SB_QROWS, LANES), blk), pl.BlockSpec((s, LANES), col),
                  pl.BlockSpec((s, LANES), lambda p, i: (0, pairs + p)), pl.BlockSpec((SB_QROWS, LANES), blk),
                  pl.BlockSpec((SB_QROWS, LANES), blk)],
        out_specs=[pl.BlockSpec((SB_QROWS, LANES), blk), pl.BlockSpec((s, LANES), col), pl.BlockSpec((s, LANES), col)],
        out_shape=[jax.ShapeDtypeStruct((s, SB_W + MEM_W), BF16), jax.ShapeDtypeStruct((s, SB_W), F32),
                   jax.ShapeDtypeStruct((s, SB_W), F32)],
        compiler_params=_params("arbitrary", "arbitrary"),
    )(proj, kv, kv, out, dy)


def final_loss(x, g, target, name):
    s, d = x.shape
    tm = _tile(s, 256, 8)

    def body(x_ref, g_ref, t_ref, loss_ref, dx_ref, dg_ref):
        @pl.when(pl.program_id(0) == 0)
        def _():
            loss_ref[...] = jnp.zeros_like(loss_ref)
            dg_ref[...] = jnp.zeros_like(dg_ref)

        xf = x_ref[...]
        r = _rms(xf)
        xhat = xf * r
        gain = g_ref[...]
        diff = xhat * gain - t_ref[...]
        sq = jnp.sum(jnp.sum(diff * diff, axis=1, keepdims=True), axis=0, keepdims=True)
        loss_ref[...] += jnp.broadcast_to(sq, loss_ref.shape)
        dy = diff * (1.0 / d)
        dg_ref[...] += jnp.sum(dy * xhat, axis=0, keepdims=True)
        dxhat = dy * gain
        dx_ref[...] = r * (dxhat - xhat * jnp.mean(dxhat * xhat, axis=-1, keepdims=True))

    row = lambda i: (i, 0)
    const = lambda i: (0, 0)
    return pl.pallas_call(
        body, name=name, grid=(s // tm,),
        in_specs=[pl.BlockSpec((tm, d), row), pl.BlockSpec((1, d), const), pl.BlockSpec((tm, d), row)],
        out_specs=[pl.BlockSpec((8, LANES), const), pl.BlockSpec((tm, d), row), pl.BlockSpec((1, d), const)],
        out_shape=[jax.ShapeDtypeStruct((8, LANES), F32), jax.ShapeDtypeStruct((s, d), F32), jax.ShapeDtypeStruct((1, d), F32)],
        compiler_params=_params("arbitrary"),
    )(x, g, target)


def adamw(w, parts, m, v, name):
    rows, cols = w.shape
    k = parts.shape[0]
    tr = _tile(rows, 512, 16)
    c1, c2 = 1.0 - ADAM_B1 ** ADAM_STEP, 1.0 - ADAM_B2 ** ADAM_STEP

    def body(w_ref, p_ref, m_ref, v_ref, g_ref, d_ref, nm_ref, nv_ref):
        grad = p_ref[0].astype(F32)
        for s in range(1, k):
            grad = grad + p_ref[s].astype(F32)
        nm = ADAM_B1 * m_ref[...] + (1.0 - ADAM_B1) * grad
        nv = ADAM_B2 * v_ref[...] + (1.0 - ADAM_B2) * (grad * grad)
        g_ref[...] = grad
        d_ref[...] = -ADAM_LR * ((nm / c1) / (jnp.sqrt(nv / c2) + ADAM_EPS) + ADAM_WD * w_ref[...])
        nm_ref[...] = nm
        nv_ref[...] = nv

    spec = pl.BlockSpec((tr, cols), lambda i: (i, 0))
    shape = jax.ShapeDtypeStruct((rows, cols), F32)
    return pl.pallas_call(
        body, name=name, grid=(rows // tr,),
        in_specs=[spec, pl.BlockSpec((k, tr, cols), lambda i: (0, i, 0)), spec, spec],
        out_specs=[spec] * 4, out_shape=[shape] * 4,
        compiler_params=_params("arbitrary"),
    )(w, parts, m, v)


SHARDED = {"ffn1_w_gate": 2, "ffn1_w_up": 2, "ffn1_w_down": 1, "ffn2_w_gate": 2, "ffn2_w_up": 2, "ffn2_w_down": 1,
           "w_mem_kv": 1, "a_w_in": 2, "a_w_out": 1, "w_kv": 1, "b_w_in": 1, "b_w_out": 1}
SMALL = ["ffn1_norm", "mix_norm", "ffn2_norm", "mem_norm", "kv_norm", "final_norm", "a_v_norm", "a_w_spatial", "a_b_spatial"]
WEIGHTS = ["ffn1_norm", "ffn1_w_gate", "ffn1_w_up", "ffn1_w_down", "mix_norm", "ffn2_norm", "ffn2_w_gate", "ffn2_w_up",
           "ffn2_w_down", "mem_norm", "w_mem_kv", "a_w_in", "a_v_norm", "a_w_spatial", "a_b_spatial", "a_w_out", "kv_norm",
           "w_kv", "b_w_in", "b_w_out", "final_norm"]


def _all_sum(parts, name):
    flat = jnp.concatenate([p.reshape(-1) for p in parts])
    pad = (-flat.size) % (16 * LANES)
    buf = jnp.pad(flat, (0, pad)).reshape(-1, LANES)
    total = sum_leading(exchange([buf], "all", True, name)[0], F32, name + "_sum").reshape(-1)
    out, off = [], 0
    for p in parts:
        out.append(total[off:off + p.size].reshape(p.shape))
        off += p.size
    return out


def _device_index():
    return 4 * lax.axis_index("x") + 2 * lax.axis_index("y") + lax.axis_index("c")


def kernel(x, mem, ffn1_norm, ffn1_w_gate, ffn1_w_up, ffn1_w_down, mix_norm, ffn2_norm, ffn2_w_gate, ffn2_w_up, ffn2_w_down, mem_norm, w_mem_kv, a_w_in, a_v_norm, a_w_spatial, a_b_spatial, a_w_out, kv_norm, w_kv, b_w_in, b_w_out, final_norm, loss_target, m_ffn1_norm, m_ffn1_w_gate, m_ffn1_w_up, m_ffn1_w_down, m_mix_norm, m_ffn2_norm, m_ffn2_w_gate, m_ffn2_w_up, m_ffn2_w_down, m_mem_norm, m_w_mem_kv, m_a_w_in, m_a_v_norm, m_a_w_spatial, m_a_b_spatial, m_a_w_out, m_kv_norm, m_w_kv, m_b_w_in, m_b_w_out, m_final_norm, v_ffn1_norm, v_ffn1_w_gate, v_ffn1_w_up, v_ffn1_w_down, v_mix_norm, v_ffn2_norm, v_ffn2_w_gate, v_ffn2_w_up, v_ffn2_w_down, v_mem_norm, v_w_mem_kv, v_a_w_in, v_a_v_norm, v_a_w_spatial, v_a_b_spatial, v_a_w_out, v_kv_norm, v_w_kv, v_b_w_in, v_b_w_out, v_final_norm):
    weights = dict(ffn1_norm=ffn1_norm, ffn1_w_gate=ffn1_w_gate, ffn1_w_up=ffn1_w_up, ffn1_w_down=ffn1_w_down, mix_norm=mix_norm, ffn2_norm=ffn2_norm, ffn2_w_gate=ffn2_w_gate, ffn2_w_up=ffn2_w_up, ffn2_w_down=ffn2_w_down, mem_norm=mem_norm, w_mem_kv=w_mem_kv, a_w_in=a_w_in, a_v_norm=a_v_norm, a_w_spatial=a_w_spatial, a_b_spatial=a_b_spatial, a_w_out=a_w_out, kv_norm=kv_norm, w_kv=w_kv, b_w_in=b_w_in, b_w_out=b_w_out, final_norm=final_norm)
    mom1 = dict(ffn1_norm=m_ffn1_norm, ffn1_w_gate=m_ffn1_w_gate, ffn1_w_up=m_ffn1_w_up, ffn1_w_down=m_ffn1_w_down, mix_norm=m_mix_norm, ffn2_norm=m_ffn2_norm, ffn2_w_gate=m_ffn2_w_gate, ffn2_w_up=m_ffn2_w_up, ffn2_w_down=m_ffn2_w_down, mem_norm=m_mem_norm, w_mem_kv=m_w_mem_kv, a_w_in=m_a_w_in, a_v_norm=m_a_v_norm, a_w_spatial=m_a_w_spatial, a_b_spatial=m_a_b_spatial, a_w_out=m_a_w_out, kv_norm=m_kv_norm, w_kv=m_w_kv, b_w_in=m_b_w_in, b_w_out=m_b_w_out, final_norm=m_final_norm)
    mom2 = dict(ffn1_norm=v_ffn1_norm, ffn1_w_gate=v_ffn1_w_gate, ffn1_w_up=v_ffn1_w_up, ffn1_w_down=v_ffn1_w_down, mix_norm=v_mix_norm, ffn2_norm=v_ffn2_norm, ffn2_w_gate=v_ffn2_w_gate, ffn2_w_up=v_ffn2_w_up, ffn2_w_down=v_ffn2_w_down, mem_norm=v_mem_norm, w_mem_kv=v_w_mem_kv, a_w_in=v_a_w_in, a_v_norm=v_a_v_norm, a_w_spatial=v_a_w_spatial, a_b_spatial=v_a_b_spatial, a_w_out=v_a_w_out, kv_norm=v_kv_norm, w_kv=v_w_kv, b_w_in=v_b_w_in, b_w_out=v_b_w_out, final_norm=v_final_norm)

    dev = _device_index()
    xs, mem_in, target = x[0], mem[0], loss_target[0]
    d_model = xs.shape[1]
    shards = {n: weights[n] for n in SHARDED}

    def mix_keys(l):
        w_in, w_out, idx = ("a_w_in", "a_w_out", l) if l < N_A else ("b_w_in", "b_w_out", l - N_A)
        return (w_in, idx), (w_out, idx)

    def ffn_keys(ffn):
        return ([], []) if ffn is None else ([(ffn[0] + "_w_gate", ffn[1]), (ffn[0] + "_w_up", ffn[1])],
                                             [(ffn[0] + "_w_down", ffn[1])])

    def ffn_after(f, l):
        return ("ffn2", l) if f == "ffn1" else (("ffn1", l + 1) if l + 1 < DEPTH else None)

    def cut_axis(key):
        return SHARDED[key[0]] - (0 if key[1] is None else 1)

    def block(key):
        return (shards[key[0]] if key[1] is None else shards[key[0]][key[1]]).astype(BF16)

    def carrying(call, keys, same_src, source, store):
        if not keys:
            return call(None)
        result, arrived = call(Side([source(k) for k in keys], same_src))
        store.update(zip(keys, arrived))
        return result

    first_gu, first_down = ffn_keys(("ffn1", 0))
    first = first_gu + first_down + [mix_keys(0)[0], ("w_mem_kv", None)]
    landed = dict(zip(first, exchange([block(k) for k in first], "all", True, "gather_first")))
    assembled = {}

    def whole(n, l=None):
        if (n, l) not in assembled:
            pieces = [landed[n, l][d] for d in range(N_DEV)]
            if n.endswith("_w_gate"):
                pieces += [landed[n.replace("_w_gate", "_w_up"), l][d] for d in range(N_DEV)]
            assembled[n, l] = jnp.concatenate(pieces, axis=cut_axis((n, l)))
        return assembled[n, l]

    def whole_gu(f, l):
        return whole(f + "_w_gate", l)

    vn_width = a_v_norm.shape[1]
    a_v_full = _all_sum([lax.dynamic_update_slice(jnp.zeros((N_A, N_DEV * vn_width), F32), a_v_norm, (0, dev * vn_width))],
                        "gather_v_norm")[0]
    row = lambda v: v.reshape(1, -1)
    w_mem_cat = whole("w_mem_kv").transpose(1, 0, 2).reshape(d_model, -1)
    bias = [jnp.repeat(a_b_spatial[i].T, GM_P, axis=1) for i in range(N_A)]

    mem_kv, mem_h = norm_mm(mem_in, row(mem_norm), w_mem_cat, BF16, "mem_kv", emit_h=True)

    def ffn_fwd(xin, f, l):
        keys_gu, keys_down = ffn_keys(ffn_after(f, l))
        if ffn_after(f, l) == ("ffn1", N_A):
            keys_down = keys_down + [("w_kv", None)]
        gu = carrying(lambda side: norm_mm(xin, row(weights[f + "_norm"][l]), whole_gu(f, l), BF16, "ffn_gu", side=side),
                      keys_gu, True, block, landed)
        out = carrying(lambda side: swiglu_mm_res(gu, whole(f + "_w_down", l), xin, 0.5, "ffn_down", side=side),
                       keys_down, True, block, landed)
        return out, gu

    saved = []
    kv = x_kv = None
    cur = xs
    for l in range(DEPTH):
        st = {"x0": cur}
        if l == N_A:
            x_kv = cur
            kv = norm_mm(cur, row(kv_norm), whole("w_kv"), BF16, "kv_proj")
        st["x1"], st["gu1"] = ffn_fwd(cur, "ffn1", l)
        key_in, key_out = mix_keys(l)
        proj = carrying(lambda side: norm_mm(st["x1"], row(mix_norm[l]), whole(*key_in), F32 if l < N_A else BF16,
                                             "a_proj" if l < N_A else "b_proj", side=side), [key_out], True, block, landed)
        if l < N_A:
            y_tok = gmlp_fwd(proj, row(a_v_full[l]), a_w_spatial[l], bias[l], "gmlp_fwd")
            st["y"] = mem_fwd(proj, 2 * GM_W // MEM_W, mem_kv, l, y_tok, GM_W // MEM_W, "mem_fwd_a")
        else:
            st["sb_out"], y_tok = sb_fwd(proj, kv, "sb_fwd")
            st["y"] = mem_fwd(proj, SB_W // MEM_W, mem_kv, l, y_tok, SB_W // MEM_W, "mem_fwd_b")
        st["proj"] = proj
        st["x2"] = carrying(lambda side: mm_res(st["y"], whole(*key_out), st["x1"], 1.0, "mix_out", side=side),
                            [mix_keys(l + 1)[0]] if l + 1 < DEPTH else [], True, block, landed)
        cur, st["gu2"] = ffn_fwd(st["x2"], "ffn2", l)
        saved.append(st)

    loss_blk, dx, d_final = final_loss(cur, row(final_norm), target, "final_loss")
    loss = lax.psum(loss_blk[0, 0] * (0.5 / d_model), AXES)

    grads = {n: [None] * weights[n].shape[0] for n in WEIGHTS if weights[n].ndim >= 2 and n not in ("w_kv",)}
    grads["final_norm"] = d_final.reshape(-1)
    d_mem_kv = [None] * DEPTH
    d_kv = []

    summed = {}

    def pieces(key):
        g = (grads[key[0]] if key[1] is None else grads[key[0]][key[1]]).astype(BF16)
        axis = cut_axis(key)
        cut = g.reshape(g.shape[:axis] + (N_DEV, g.shape[axis] // N_DEV) + g.shape[axis + 1:])
        return jnp.moveaxis(cut, axis, 0)

    def ffn_bwd(dx, xin, gu, f, l):
        keys_gu, keys_down = ffn_keys(ffn_after(f, l))
        keys_mix = []
        if f == "ffn2" and l + 1 < DEPTH:
            keys_mix = list(mix_keys(l + 1)) + ([("w_kv", None)] if l + 1 == N_A else [])
        d_gu = carrying(lambda side: mm_nt_swiglu_bwd(dx, whole(f + "_w_down", l), gu, 0.5, "ffn_dgu", side=side),
                        keys_down, False, pieces, summed)
        dx_new, d_gain, h = carrying(
            lambda side: mm_nt_normbwd(d_gu, whole_gu(f, l), xin, row(weights[f + "_norm"][l]), dx, "ffn_dx", side=side),
            keys_gu, False, pieces, summed)
        d_wgu = carrying(lambda side: mm_tn(h, d_gu, 1.0, "ffn_dwgu", tb_target=1408, side=side),
                         keys_mix, False, pieces, summed)
        d_wdown = swiglu_mm_tn(gu, dx, 0.5, "ffn_dwdown")
        half = d_wgu.shape[1] // 2
        grads[f + "_w_gate"][l], grads[f + "_w_up"][l] = d_wgu[:, :half], d_wgu[:, half:]
        grads[f + "_w_down"][l] = d_wdown
        grads[f + "_norm"][l] = d_gain.reshape(-1)
        return dx_new

    for l in reversed(range(DEPTH)):
        st = saved[l]
        dx = ffn_bwd(dx, st["x2"], st["gu2"], "ffn2", l)
        proj = st["proj"]
        (key_in, idx), (key_out, _) = mix_keys(l)
        w_in, w_out = whole(key_in, idx), whole(key_out, idx)
        dy = mm_nt(dx, w_out, 1.0, "mix_dy")
        grads[key_out][idx] = mm_tn(st["y"], dx, 1.0, "mix_dwout", tb_target=1024)
        if l < N_A:
            d_uv, d_ws, d_bs, d_vgain = gmlp_bwd(proj, dy, row(a_v_full[l]), a_w_spatial[l], bias[l], "gmlp_bwd")
            grads["a_w_spatial"][l], grads["a_b_spatial"][l], grads["a_v_norm"][l] = d_ws, d_bs[:, :, 0], d_vgain.reshape(-1)
            d_proj, d_k, d_v = mem_bwd(proj, 2 * GM_W // MEM_W, mem_kv, l, dy, GM_W // MEM_W, d_uv, "mem_bwd_a")
        else:
            d_qsb, d_ksb, d_vsb = sb_bwd(proj, kv, st["sb_out"], dy, "sb_bwd")
            d_kv.append(jnp.concatenate([d_ksb, d_vsb], axis=1))
            d_proj, d_k, d_v = mem_bwd(proj, SB_W // MEM_W, mem_kv, l, dy, SB_W // MEM_W, d_qsb, "mem_bwd_b")
        d_mem_kv[l] = jnp.concatenate([d_k, d_v], axis=1)
        dx, d_gain, h = mm_nt_normbwd(d_proj, w_in, st["x1"], row(mix_norm[l]), dx, "mix_dx")
        grads["mix_norm"][l] = d_gain.reshape(-1)
        grads[key_in][idx] = mm_tn(h, d_proj, 1.0, "mix_dwin")
        dx = ffn_bwd(dx, st["x0"], st["gu1"], "ffn1", l)
        if l == N_A:
            d_kv_b = sum_leading(jnp.stack(d_kv), BF16, "kv_dsum")
            dx, d_gain, h = mm_nt_normbwd(d_kv_b, whole("w_kv"), x_kv, row(kv_norm), dx, "kv_dx")
            grads["kv_norm"] = d_gain.reshape(-1)
            grads["w_kv"] = mm_tn(h, d_kv_b, 1.0, "kv_dw")

    d_mem_all = jnp.concatenate(d_mem_kv, axis=1).astype(BF16)
    _, d_gain, _ = mm_nt_normbwd(d_mem_all, w_mem_cat, mem_in, row(mem_norm), None, "mem_dnorm")
    grads["mem_norm"] = d_gain.reshape(-1)
    d_wmem = mm_tn(mem_h, d_mem_all, 1.0, "mem_dw")
    grads["w_mem_kv"] = d_wmem.reshape(d_model, DEPTH, -1).transpose(1, 0, 2)

    last = first + [mix_keys(0)[1]]
    summed.update(zip(last, exchange([pieces(k) for k in last], "all", False, "scatter_last")))
    parts = {n: summed[n, None] if (n, None) in summed else
             jnp.stack([summed[n, i] for i in range(weights[n].shape[0])], axis=1) for n in SHARDED}
    grads = {n: (jnp.stack(g) if isinstance(g, list) else g) for n, g in grads.items()}
    for n, g in zip(SMALL, _all_sum([grads[n] for n in SMALL], "sum_small")):
        parts[n] = g[None]
    parts["a_v_norm"] = lax.dynamic_slice(parts["a_v_norm"], (0, 0, dev * vn_width), (1,) + a_v_norm.shape)

    reduced, deltas, new_m, new_v = {}, {}, {}, {}
    for n in WEIGHTS:
        w = weights[n]
        view = (lambda a: a.reshape(-1, a.shape[-1]))
        res = adamw(view(w), parts[n].reshape(parts[n].shape[0], -1, w.shape[-1]), view(mom1[n]), view(mom2[n]), "adamw")
        reduced[n], deltas[n], new_m[n], new_v[n] = [r.reshape(w.shape) for r in res]

    return (loss, dx[None], *[reduced[n] for n in WEIGHTS], *[deltas[n] for n in WEIGHTS],
            *[new_m[n] for n in WEIGHTS], *[new_v[n] for n in WEIGHTS])
```

```python
import functools

import jax
import jax.numpy as jnp
from jax import lax
from jax.experimental import pallas as pl
from jax.experimental.pallas import tpu as pltpu

F32, BF16 = jnp.float32, jnp.bfloat16
MESH_ID = pl.DeviceIdType.MESH
AXES = ("x", "y", "c")
N_DEV = 8

EPS = 1e-6
DEPTH, N_A = 4, 2
GM_W, GM_GROUPS, GM_P = 768, 6, 128
MEM_W, MEM_HEADS, HEAD_DIM = 256, 4, 64
SB_W, SB_BLK = 768, 128
LANES = 128
QK_SCALE = HEAD_DIM ** -0.5
GELU_C, GELU_A = 0.7978845608028654, 0.044715

ADAM_LR, ADAM_B1, ADAM_B2, ADAM_EPS, ADAM_WD, ADAM_STEP = 0.001, 0.9, 0.999, 1e-08, 0.01, 10

VMEM_LIMIT = 56 * 1024 * 1024
PACK_COLS = 512

NT = (((1,), (1,)), ((), ()))
TN = (((0,), (0,)), ((), ()))


def _params(*sem):
    return pltpu.CompilerParams(dimension_semantics=sem, vmem_limit_bytes=VMEM_LIMIT)


def _tile(n, target, mult=LANES):
    best = None
    for t in range(mult, min(n, target) + 1, mult):
        if n % t == 0:
            best = t
    return best if best is not None else n


def _dot(a, b, dims=None):
    if dims is None:
        return jnp.dot(a, b, preferred_element_type=F32)
    return lax.dot_general(a, b, dims, preferred_element_type=F32)


def exchange(srcs, group, same_src, name, split=False):
    size = {"pair": 2, "quad": 4, "all": 8}[group]
    n = len(srcs)
    chunk_shapes = [tuple(s.shape) if same_src else tuple(s.shape[1:]) for s in srcs]
    pieces = [cs[0] if split else 1 for cs in chunk_shapes]
    n_dma = sum(pieces)

    def body(*refs):
        src_refs, out_refs = refs[:n], refs[n:2 * n]
        send_sems, recv_sems, local_sems = refs[2 * n:]
        x, y, c = lax.axis_index("x"), lax.axis_index("y"), lax.axis_index("c")
        if group == "pair":
            me, dev = c, lambda p: (x, y, p)
        elif group == "quad":
            me, dev = 2 * x + y, lambda p: (p // 2, p % 2, c)
        else:
            me, dev = 4 * x + 2 * y + c, lambda p: (p // 4, (p // 2) % 2, p % 2)

        def chunk(t, idx):
            return src_refs[t] if same_src else src_refs[t].at[idx]

        def copies(k, idx, slot, peer):
            out, w = [], k * n_dma
            for t in range(n):
                src, dst = chunk(t, idx), out_refs[t].at[slot]
                for s_ref, d_ref in ([(src.at[u], dst.at[u]) for u in range(pieces[t])] if split else [(src, dst)]):
                    out.append(pltpu.make_async_remote_copy(
                        src_ref=s_ref, dst_ref=d_ref, send_sem=send_sems.at[w], recv_sem=recv_sems.at[w],
                        device_id=dev(peer), device_id_type=MESH_ID))
                    w += 1
            return out

        local = [pltpu.make_async_copy(chunk(t, me), out_refs[t].at[me], local_sems.at[t]) for t in range(n)]
        for cp in local:
            cp.start()
        sends = []
        for k in range(1, size):
            peer = (me + k) % size
            sends += copies(k, peer, me, peer)
        for cp in sends:
            cp.start()
        for k in range(1, size):
            sender = (me + size - k) % size
            for cp in copies(k, me, sender, sender):
                cp.wait_recv()
        for cp in sends:
            cp.wait_send()
        for cp in local:
            cp.wait()

    hbm = pl.BlockSpec(memory_space=pltpu.HBM)
    return pl.pallas_call(
        body, name=name,
        out_shape=[jax.ShapeDtypeStruct((size,) + cs, s.dtype) for cs, s in zip(chunk_shapes, srcs)],
        in_specs=[hbm] * n, out_specs=[hbm] * n,
        scratch_shapes=[pltpu.SemaphoreType.DMA((size * n_dma,)), pltpu.SemaphoreType.DMA((size * n_dma,)),
                        pltpu.SemaphoreType.DMA((n,))],
    )(*srcs)


class Side:
    def __init__(self, srcs, same_src):
        self.srcs, self.same_src, self.n = list(srcs), same_src, len(srcs)
        self.chunk_shapes = [tuple(s.shape) if same_src else tuple(s.shape[1:]) for s in srcs]

    def out_shapes(self):
        return [jax.ShapeDtypeStruct((N_DEV,) + cs, s.dtype) for cs, s in zip(self.chunk_shapes, self.srcs)]

    def scratch(self):
        return [pltpu.SemaphoreType.DMA((N_DEV * self.n,)), pltpu.SemaphoreType.DMA((N_DEV * self.n,))]

    def _copies(self, src_refs, land_refs, send_sems, recv_sems, outgoing):
        me = 4 * lax.axis_index("x") + 2 * lax.axis_index("y") + lax.axis_index("c")
        out = []
        for k in range(1, N_DEV):
            peer = (me + k) % N_DEV if outgoing else (me + N_DEV - k) % N_DEV
            for t in range(self.n):
                src = src_refs[t] if self.same_src else src_refs[t].at[peer if outgoing else me]
                out.append(pltpu.make_async_remote_copy(
                    src_ref=src, dst_ref=land_refs[t].at[me if outgoing else peer],
                    send_sem=send_sems.at[k * self.n + t], recv_sem=recv_sems.at[k * self.n + t],
                    device_id=(peer // 4, (peer // 2) % 2, peer % 2), device_id_type=MESH_ID))
        return out

    def _own(self, src_refs, land_refs, send_sems):
        me = 4 * lax.axis_index("x") + 2 * lax.axis_index("y") + lax.axis_index("c")
        return [pltpu.make_async_copy(src_refs[t] if self.same_src else src_refs[t].at[me], land_refs[t].at[me],
                                      send_sems.at[t]) for t in range(self.n)]

    def start(self, src_refs, land_refs, send_sems, recv_sems):
        for cp in self._own(src_refs, land_refs, send_sems) + self._copies(src_refs, land_refs, send_sems, recv_sems, True):
            cp.start()

    def wait(self, src_refs, land_refs, send_sems, recv_sems):
        for cp in self._copies(src_refs, land_refs, send_sems, recv_sems, False):
            cp.wait_recv()
        for cp in self._copies(src_refs, land_refs, send_sems, recv_sems, True):
            cp.wait_send()
        for cp in self._own(src_refs, land_refs, send_sems):
            cp.wait()


def _call(body, side, name, grid, in_specs, out_specs, out_shape, scratch_shapes, dims, args):
    if side is None:
        res = pl.pallas_call(body, name=name, grid=grid, in_specs=in_specs, out_specs=out_specs, out_shape=out_shape,
                             scratch_shapes=scratch_shapes, compiler_params=_params(*dims))(*args)
        return list(res), []
    n_in, n_out, n_scr, ns = len(in_specs), len(out_specs), len(scratch_shapes), side.n

    def wrapped(*refs):
        ins, srcs = refs[:n_in], refs[n_in:n_in + ns]
        outs, lands = refs[n_in + ns:n_in + ns + n_out], refs[n_in + ns + n_out:n_in + 2 * ns + n_out]
        scratch, (send_sems, recv_sems) = refs[n_in + 2 * ns + n_out:n_in + 2 * ns + n_out + n_scr], refs[-2:]
        first, last = None, None
        for axis, steps in enumerate(grid):
            i = pl.program_id(axis)
            first = (i == 0) if first is None else first & (i == 0)
            last = (i == steps - 1) if last is None else last & (i == steps - 1)

        @pl.when(first)
        def _():
            side.start(srcs, lands, send_sems, recv_sems)

        body(*ins, *outs, *scratch)

        @pl.when(last)
        def _():
            side.wait(srcs, lands, send_sems, recv_sems)

    hbm = pl.BlockSpec(memory_space=pltpu.HBM)
    res = pl.pallas_call(
        wrapped, name=name, grid=grid, in_specs=list(in_specs) + [hbm] * ns, out_specs=list(out_specs) + [hbm] * ns,
        out_shape=list(out_shape) + side.out_shapes(), scratch_shapes=list(scratch_shapes) + side.scratch(),
        compiler_params=_params(*dims))(*args, *side.srcs)
    return list(res[:n_out]), list(res[n_out:])


def sum_leading(parts, out_dtype, name):
    k, rows, cols = parts.shape
    tr = _tile(rows, 512, 16)

    def body(p_ref, o_ref):
        acc = p_ref[0].astype(F32)
        for s in range(1, k):
            acc = acc + p_ref[s].astype(F32)
        o_ref[...] = acc.astype(o_ref.dtype)

    return pl.pallas_call(
        body, name=name, grid=(rows // tr,),
        in_specs=[pl.BlockSpec((k, tr, cols), lambda i: (0, i, 0))],
        out_specs=pl.BlockSpec((tr, cols), lambda i: (i, 0)),
        out_shape=jax.ShapeDtypeStruct((rows, cols), out_dtype),
        compiler_params=_params("arbitrary"),
    )(parts)


def _rms(xf):
    return lax.rsqrt(jnp.mean(xf * xf, axis=-1, keepdims=True) + EPS)


def norm_mm(x, g, w, out_dtype, name, emit_h=False, side=None):
    m, d = x.shape
    n = w.shape[1]
    tm, tn = _tile(m, 1024, 8), _tile(n, 1408)

    def body(x_ref, g_ref, w_ref, o_ref, *rest):
        h_ref = rest[-1]

        @pl.when(pl.program_id(1) == 0)
        def _():
            xf = x_ref[...]
            hb = ((xf * _rms(xf)) * g_ref[...]).astype(BF16)
            h_ref[...] = hb
            if emit_h:
                rest[0][...] = hb

        o_ref[...] = _dot(h_ref[...], w_ref[...]).astype(o_ref.dtype)

    out_shape = [jax.ShapeDtypeStruct((m, n), out_dtype)]
    out_specs = [pl.BlockSpec((tm, tn), lambda i, j: (i, j))]
    if emit_h:
        out_shape.append(jax.ShapeDtypeStruct((m, d), BF16))
        out_specs.append(pl.BlockSpec((tm, d), lambda i, j: (i, 0)))
    res, landed = _call(
        body, side, name, (m // tm, n // tn),
        [pl.BlockSpec((tm, d), lambda i, j: (i, 0)), pl.BlockSpec((1, d), lambda i, j: (0, 0)),
         pl.BlockSpec((d, tn), lambda i, j: (0, j))],
        out_specs, out_shape, [pltpu.VMEM((tm, d), BF16)], ("arbitrary", "arbitrary"), (x, g, w))
    out = res if emit_h else res[0]
    return out if side is None else (out, landed)


def mm_res(a, w, res, alpha, name, side=None):
    m, k = a.shape
    n = w.shape[1]
    tm, tn = _tile(m, 1024, 8), _tile(n, 1024)

    def body(a_ref, w_ref, r_ref, o_ref):
        o_ref[...] = r_ref[...] + alpha * _dot(a_ref[...], w_ref[...])

    out, landed = _call(
        body, side, name, (m // tm, n // tn),
        [pl.BlockSpec((tm, k), lambda i, j: (i, 0)), pl.BlockSpec((k, tn), lambda i, j: (0, j)),
         pl.BlockSpec((tm, tn), lambda i, j: (i, j))],
        [pl.BlockSpec((tm, tn), lambda i, j: (i, j))], [jax.ShapeDtypeStruct((m, n), F32)], [],
        ("arbitrary", "arbitrary"), (a, w, res))
    return out[0] if side is None else (out[0], landed)


def mm_nt(x, w, alpha, name):
    m, d = x.shape
    n = w.shape[0]
    tm, tn = _tile(m, 1024, 8), _tile(n, 1408)

    def body(x_ref, w_ref, o_ref, xb_ref):
        @pl.when(pl.program_id(1) == 0)
        def _():
            xb_ref[...] = x_ref[...].astype(BF16)

        o_ref[...] = (alpha * _dot(xb_ref[...], w_ref[...], NT)).astype(o_ref.dtype)

    return pl.pallas_call(
        body, name=name, grid=(m // tm, n // tn),
        in_specs=[pl.BlockSpec((tm, d), lambda i, j: (i, 0)), pl.BlockSpec((tn, d), lambda i, j: (j, 0))],
        out_specs=pl.BlockSpec((tm, tn), lambda i, j: (i, j)),
        out_shape=jax.ShapeDtypeStruct((m, n), BF16),
        scratch_shapes=[pltpu.VMEM((tm, d), BF16)],
        compiler_params=_params("arbitrary", "arbitrary"),
    )(x, w)


def mm_tn(a, b, alpha, name, ta_target=1024, tb_target=512, side=None):
    s, ka = a.shape
    nb = b.shape[1]
    ta, tb, ts = _tile(ka, ta_target), _tile(nb, tb_target), _tile(s, 1024, 16)
    steps = s // ts

    def body(a_ref, b_ref, o_ref, acc_ref):
        t = pl.program_id(2)

        @pl.when(t == 0)
        def _():
            acc_ref[...] = jnp.zeros_like(acc_ref)

        acc_ref[...] += _dot(a_ref[...].astype(BF16), b_ref[...].astype(BF16), TN)

        @pl.when(t == steps - 1)
        def _():
            o_ref[...] = (alpha * acc_ref[...]).astype(o_ref.dtype)

    res, landed = _call(
        body, side, name, (ka // ta, nb // tb, steps),
        [pl.BlockSpec((ts, ta), lambda i, j, t: (t, i)), pl.BlockSpec((ts, tb), lambda i, j, t: (t, j))],
        [pl.BlockSpec((ta, tb), lambda i, j, t: (i, j))], [jax.ShapeDtypeStruct((ka, nb), BF16)],
        [pltpu.VMEM((ta, tb), F32)], ("arbitrary", "arbitrary", "arbitrary"), (a, b))
    return res[0] if side is None else (res[0], landed)


def mm_nt_normbwd(dy, w, x, g, res, name, side=None):
    m, n = dy.shape
    d = w.shape[0]
    tm, tk = _tile(m, 1024, 8), _tile(n, 1408)
    steps = n // tk
    has_res = res is not None

    def body(*refs):
        if has_res:
            dy_ref, w_ref, x_ref, g_ref, r_ref, dx_ref, dg_ref, h_ref, acc_ref = refs
        else:
            dy_ref, w_ref, x_ref, g_ref, dx_ref, dg_ref, h_ref, acc_ref = refs
        i, t = pl.program_id(0), pl.program_id(1)

        @pl.when(t == 0)
        def _():
            acc_ref[...] = jnp.zeros_like(acc_ref)

        @pl.when((t == 0) & (i == 0))
        def _():
            dg_ref[...] = jnp.zeros_like(dg_ref)

        acc_ref[...] += _dot(dy_ref[...], w_ref[...], NT)

        @pl.when(t == steps - 1)
        def _():
            xf = x_ref[...]
            r = _rms(xf)
            xhat = xf * r
            dh = acc_ref[...]
            gain = g_ref[...]
            dg_ref[...] += jnp.sum(dh * xhat, axis=0, keepdims=True)
            dxhat = dh * gain
            dx = r * (dxhat - xhat * jnp.mean(dxhat * xhat, axis=-1, keepdims=True))
            dx_ref[...] = (r_ref[...] + dx) if has_res else dx
            h_ref[...] = (xhat * gain).astype(BF16)

    row = lambda i, t: (i, 0)
    in_specs = [pl.BlockSpec((tm, tk), lambda i, t: (i, t)), pl.BlockSpec((d, tk), lambda i, t: (0, t)),
                pl.BlockSpec((tm, d), row), pl.BlockSpec((1, d), lambda i, t: (0, 0))]
    args = [dy, w, x, g]
    if has_res:
        in_specs.append(pl.BlockSpec((tm, d), row))
        args.append(res)
    res, landed = _call(
        body, side, name, (m // tm, steps), in_specs,
        [pl.BlockSpec((tm, d), row), pl.BlockSpec((1, d), lambda i, t: (0, 0)), pl.BlockSpec((tm, d), row)],
        [jax.ShapeDtypeStruct((m, d), F32), jax.ShapeDtypeStruct((1, d), F32), jax.ShapeDtypeStruct((m, d), BF16)],
        [pltpu.VMEM((tm, d), F32)], ("arbitrary", "arbitrary"), args)
    return res if side is None else (res, landed)


def _sigmoid(z):
    return 1.0 / (1.0 + jnp.exp(-z))


def _swiglu(gate_b, up_b):
    gate = gate_b.astype(F32)
    return (gate * _sigmoid(gate) * up_b.astype(F32)).astype(BF16)


def swiglu_mm_res(gu, w, res, alpha, name, side=None):
    m, f2 = gu.shape
    f, n = w.shape
    tm, tc = _tile(m, 256, 16), _tile(f, 256)

    def body(gu_ref, w_ref, r_ref, o_ref):
        acc = jnp.zeros((tm, n), F32)
        for c0 in range(0, f, tc):
            act = _swiglu(gu_ref[:, c0:c0 + tc], gu_ref[:, f + c0:f + c0 + tc])
            acc = acc + _dot(act, w_ref[c0:c0 + tc, :])
        o_ref[...] = r_ref[...] + alpha * acc

    out, landed = _call(
        body, side, name, (m // tm,),
        [pl.BlockSpec((tm, f2), lambda i: (i, 0)), pl.BlockSpec((f, n), lambda i: (0, 0)),
         pl.BlockSpec((tm, n), lambda i: (i, 0))],
        [pl.BlockSpec((tm, n), lambda i: (i, 0))], [jax.ShapeDtypeStruct((m, n), F32)], [], ("arbitrary",), (gu, w, res))
    return out[0] if side is None else (out[0], landed)


def swiglu_mm_tn(gu, b, alpha, name, side=None):
    s, f2 = gu.shape
    f, n = f2 // 2, b.shape[1]
    ta, ts = _tile(f, 1408), _tile(s, 512, 16)
    steps, half = s // ts, f // ta

    def body(g_ref, u_ref, b_ref, o_ref, acc_ref):
        t = pl.program_id(1)

        @pl.when(t == 0)
        def _():
            acc_ref[...] = jnp.zeros_like(acc_ref)

        bb = b_ref[...].astype(BF16)
        for c0 in range(0, ta, LANES):
            acc_ref[c0:c0 + LANES, :] += _dot(_swiglu(g_ref[:, c0:c0 + LANES], u_ref[:, c0:c0 + LANES]), bb, TN)

        @pl.when(t == steps - 1)
        def _():
            o_ref[...] = (alpha * acc_ref[...]).astype(o_ref.dtype)

    res, landed = _call(
        body, side, name, (half, steps),
        [pl.BlockSpec((ts, ta), lambda i, t: (t, i)), pl.BlockSpec((ts, ta), lambda i, t: (t, half + i)),
         pl.BlockSpec((ts, n), lambda i, t: (t, 0))],
        [pl.BlockSpec((ta, n), lambda i, t: (i, 0))], [jax.ShapeDtypeStruct((f, n), BF16)],
        [pltpu.VMEM((ta, n), F32)], ("arbitrary", "arbitrary"), (gu, gu, b))
    return res[0] if side is None else (res[0], landed)


def mm_nt_swiglu_bwd(x, w, gu, alpha, name, side=None):
    m, d = x.shape
    f = w.shape[0]
    tm, tc = _tile(m, 256, 16), _tile(f, 256)

    def body(x_ref, w_ref, gu_ref, o_ref):
        xb = x_ref[...].astype(BF16)
        for c0 in range(0, f, tc):
            d_act = alpha * _dot(xb, w_ref[c0:c0 + tc, :], NT)
            gate, up = gu_ref[:, c0:c0 + tc].astype(F32), gu_ref[:, f + c0:f + c0 + tc].astype(F32)
            sg = _sigmoid(gate)
            o_ref[:, c0:c0 + tc] = (d_act * up * (sg * (1.0 + gate * (1.0 - sg)))).astype(BF16)
            o_ref[:, f + c0:f + c0 + tc] = (d_act * (gate * sg)).astype(BF16)

    res, landed = _call(
        body, side, name, (m // tm,),
        [pl.BlockSpec((tm, d), lambda i: (i, 0)), pl.BlockSpec((f, d), lambda i: (0, 0)),
         pl.BlockSpec((tm, 2 * f), lambda i: (i, 0))],
        [pl.BlockSpec((tm, 2 * f), lambda i: (i, 0))], [jax.ShapeDtypeStruct((m, 2 * f), BF16)], [], ("arbitrary",),
        (x, w, gu))
    return res[0] if side is None else (res[0], landed)


def _gelu(x):
    return 0.5 * x * (1.0 + jnp.tanh(GELU_C * (x + GELU_A * x * x * x)))


def _gelu_grad(x):
    t = jnp.tanh(GELU_C * (x + GELU_A * x * x * x))
    return 0.5 * (1.0 + t) + 0.5 * x * (1.0 - t * t) * (GELU_C * (1.0 + 3.0 * GELU_A * x * x))


def _chunk_mask():
    row = lax.broadcasted_iota(jnp.int32, (GM_P, GM_P), 0)
    col = lax.broadcasted_iota(jnp.int32, (GM_P, GM_P), 1)
    return (col < GM_P // 2) | (row >= GM_P // 2)


def gmlp_fwd(proj, gain, w_s, bias, name):
    s, pw = proj.shape
    tm = _tile(s, 256, GM_P)

    def body(p_ref, gain_ref, w_ref, b_ref, o_ref):
        mask = _chunk_mask()
        u = _gelu(p_ref[:, :GM_W])
        v = _gelu(p_ref[:, GM_W:2 * GM_W])
        vn = ((v * _rms(v)) * gain_ref[...]).astype(BF16)
        for g in range(GM_GROUPS):
            wg = jnp.where(mask, w_ref[g], 0.0).astype(BF16)
            cols = slice(g * GM_P, (g + 1) * GM_P)
            for n in range(tm // GM_P):
                rows = slice(n * GM_P, (n + 1) * GM_P)
                mixed = _dot(wg, vn[rows, cols]) + b_ref[:, cols]
                o_ref[rows, cols] = (u[rows, cols] * mixed).astype(BF16)

    return pl.pallas_call(
        body, name=name, grid=(s // tm,),
        in_specs=[pl.BlockSpec((tm, pw), lambda i: (i, 0)), pl.BlockSpec((1, GM_W), lambda i: (0, 0)),
                  pl.BlockSpec((GM_GROUPS, GM_P, GM_P), lambda i: (0, 0, 0)), pl.BlockSpec((GM_P, GM_W), lambda i: (0, 0))],
        out_specs=pl.BlockSpec((tm, GM_W), lambda i: (i, 0)),
        out_shape=jax.ShapeDtypeStruct((s, GM_W + MEM_W), BF16), compiler_params=_params("arbitrary"),
    )(proj, gain, w_s, bias)


def gmlp_bwd(proj, dy, gain, w_s, bias, name):
    s, pw = proj.shape
    dw_total = dy.shape[1]
    tm = _tile(s, 256, GM_P)

    def body(p_ref, dy_ref, gain_ref, w_ref, b_ref, dp_ref, dw_ref, db_ref, dgain_ref, dvn_ref):
        @pl.when(pl.program_id(0) == 0)
        def _():
            dw_ref[...] = jnp.zeros_like(dw_ref)
            db_ref[...] = jnp.zeros_like(db_ref)
            dgain_ref[...] = jnp.zeros_like(dgain_ref)

        mask = _chunk_mask()
        pu = p_ref[:, :GM_W]
        pv = p_ref[:, GM_W:2 * GM_W]
        u = _gelu(pu)
        v = _gelu(pv)
        r = _rms(v)
        vhat = v * r
        gain = gain_ref[...]
        vn = (vhat * gain).astype(BF16)
        gu_grad = _gelu_grad(pu)
        for g in range(GM_GROUPS):
            wg = jnp.where(mask, w_ref[g], 0.0).astype(BF16)
            cols = slice(g * GM_P, (g + 1) * GM_P)
            dw_acc = jnp.zeros((GM_P, GM_P), F32)
            db_acc = jnp.zeros((GM_P, 1), F32)
            for n in range(tm // GM_P):
                rows = slice(n * GM_P, (n + 1) * GM_P)
                dyb = dy_ref[rows, cols].astype(F32)
                vnb = vn[rows, cols]
                mixed = _dot(wg, vnb) + b_ref[:, cols]
                dmixed = dyb * u[rows, cols]
                dmb = dmixed.astype(BF16)
                dp_ref[rows, cols] = (dyb * mixed * gu_grad[rows, cols]).astype(BF16)
                dw_acc = dw_acc + _dot(dmb, vnb, NT)
                db_acc = db_acc + jnp.sum(dmixed, axis=1, keepdims=True)
                dvn_ref[rows, cols] = _dot(wg, dmb, TN)
            dw_ref[g] += jnp.where(mask, dw_acc, 0.0)
            db_ref[g] += jnp.broadcast_to(db_acc, (GM_P, GM_P))
        dvn = dvn_ref[...]
        dgain_ref[...] += jnp.sum(dvn * vhat, axis=0, keepdims=True)
        dvhat = dvn * gain
        dv = r * (dvhat - vhat * jnp.mean(dvhat * vhat, axis=-1, keepdims=True))
        dp_ref[:, GM_W:] = (dv * _gelu_grad(pv)).astype(BF16)

    const3 = lambda i: (0, 0, 0)
    return pl.pallas_call(
        body, name=name, grid=(s // tm,),
        in_specs=[pl.BlockSpec((tm, pw), lambda i: (i, 0)), pl.BlockSpec((tm, dw_total), lambda i: (i, 0)),
                  pl.BlockSpec((1, GM_W), lambda i: (0, 0)), pl.BlockSpec((GM_GROUPS, GM_P, GM_P), const3),
                  pl.BlockSpec((GM_P, GM_W), lambda i: (0, 0))],
        out_specs=[pl.BlockSpec((tm, 2 * GM_W), lambda i: (i, 0)), pl.BlockSpec((GM_GROUPS, GM_P, GM_P), const3),
                   pl.BlockSpec((GM_GROUPS, GM_P, GM_P), const3), pl.BlockSpec((1, GM_W), lambda i: (0, 0))],
        out_shape=[jax.ShapeDtypeStruct((s, pw), BF16), jax.ShapeDtypeStruct((GM_GROUPS, GM_P, GM_P), F32),
                   jax.ShapeDtypeStruct((GM_GROUPS, GM_P, GM_P), F32), jax.ShapeDtypeStruct((1, GM_W), F32)],
        scratch_shapes=[pltpu.VMEM((tm, GM_W), F32)],
        compiler_params=_params("arbitrary"),
    )(proj, dy, gain, w_s, bias)


def _keep(mask, xb):
    return jnp.where(mask, xb.astype(F32), 0.0).astype(BF16)


def _head_masks(rows, width, heads):
    lane = lax.broadcasted_iota(jnp.int32, (rows, width), 1)
    return [(lane >= HEAD_DIM * h) & (lane < HEAD_DIM * (h + 1)) for h in range(heads)]


def _mem_probs(qh, k):
    sc = _dot(qh, k, NT) * QK_SCALE
    e = jnp.exp(sc - jnp.max(sc, axis=-1, keepdims=True))
    return e / jnp.sum(e, axis=-1, keepdims=True)


def mem_fwd(proj, q_blk, mem_kv, layer, into, into_blk, name):
    s = proj.shape[0]
    n_mem = mem_kv.shape[0]
    tm = _tile(s, 512, 16)

    def body(q_ref, k_ref, v_ref, into_ref, o_ref):
        q = q_ref[...].astype(BF16)
        k, v = k_ref[...], v_ref[...]
        out = jnp.zeros((tm, MEM_W), F32)
        for hm in _head_masks(tm, MEM_W, MEM_HEADS):
            p = _mem_probs(_keep(hm, q), k)
            out = out + jnp.where(hm, _dot(p.astype(BF16), v), 0.0)
        o_ref[...] = out.astype(BF16)

    return pl.pallas_call(
        body, name=name, grid=(s // tm,),
        in_specs=[pl.BlockSpec((tm, MEM_W), lambda i: (i, q_blk)), pl.BlockSpec((n_mem, MEM_W), lambda i: (0, 2 * layer)),
                  pl.BlockSpec((n_mem, MEM_W), lambda i: (0, 2 * layer + 1)), pl.BlockSpec(memory_space=pl.ANY)],
        out_specs=pl.BlockSpec((tm, MEM_W), lambda i: (i, into_blk)),
        out_shape=jax.ShapeDtypeStruct(into.shape, BF16), input_output_aliases={3: 0},
        compiler_params=_params("arbitrary"),
    )(proj, mem_kv, mem_kv, into)


def mem_bwd(proj, q_blk, mem_kv, layer, dy, dy_blk, into, name):
    s = proj.shape[0]
    n_mem = mem_kv.shape[0]
    tm = _tile(s, 512, 16)

    def body(q_ref, k_ref, v_ref, dy_ref, into_ref, dq_ref, dk_ref, dv_ref):
        @pl.when(pl.program_id(0) == 0)
        def _():
            dk_ref[...] = jnp.zeros_like(dk_ref)
            dv_ref[...] = jnp.zeros_like(dv_ref)

        q = q_ref[...].astype(BF16)
        k, v = k_ref[...], v_ref[...]
        dy = dy_ref[...]
        dq = jnp.zeros((tm, MEM_W), F32)
        dk = jnp.zeros((n_mem, MEM_W), F32)
        dv = jnp.zeros((n_mem, MEM_W), F32)
        for hm in _head_masks(tm, MEM_W, MEM_HEADS):
            qh = _keep(hm, q)
            dyh = _keep(hm, dy)
            p = _mem_probs(qh, k)
            dp = _dot(dyh, v, NT)
            dv = dv + _dot(p.astype(BF16), dyh, TN)
            ds = (p * (dp - jnp.sum(dp * p, axis=-1, keepdims=True)) * QK_SCALE).astype(BF16)
            dq = dq + jnp.where(hm, _dot(ds, k), 0.0)
            dk = dk + _dot(ds, qh, TN)
        dq_ref[...] = dq.astype(BF16)
        dk_ref[...] += dk
        dv_ref[...] += dv

    const = lambda i: (0, 0)
    return pl.pallas_call(
        body, name=name, grid=(s // tm,),
        in_specs=[pl.BlockSpec((tm, MEM_W), lambda i: (i, q_blk)), pl.BlockSpec((n_mem, MEM_W), lambda i: (0, 2 * layer)),
                  pl.BlockSpec((n_mem, MEM_W), lambda i: (0, 2 * layer + 1)), pl.BlockSpec((tm, MEM_W), lambda i: (i, dy_blk)),
                  pl.BlockSpec(memory_space=pl.ANY)],
        out_specs=[pl.BlockSpec((tm, MEM_W), lambda i: (i, q_blk)), pl.BlockSpec((n_mem, MEM_W), const),
                   pl.BlockSpec((n_mem, MEM_W), const)],
        out_shape=[jax.ShapeDtypeStruct(into.shape, BF16), jax.ShapeDtypeStruct((n_mem, MEM_W), F32),
                   jax.ShapeDtypeStruct((n_mem, MEM_W), F32)],
        input_output_aliases={4: 0}, compiler_params=_params("arbitrary"),
    )(proj, mem_kv, mem_kv, dy, into)


SB_KEYS = 256
SB_SUB = SB_KEYS // SB_BLK
SB_QROWS = 256
SB_QB = SB_QROWS // SB_BLK
SB_CHAINS = 2 * SB_QB
SB_DEAD = -110.0


def _split(xf):
    hi = xf.astype(BF16)
    return hi, (xf - hi.astype(F32)).astype(BF16)


def _sb_consts():
    row = lax.bitwise_and(lax.broadcasted_iota(jnp.int32, (2 * SB_BLK, 2 * SB_BLK), 0), SB_BLK - 1)
    col = lax.broadcasted_iota(jnp.int32, (2 * SB_BLK, 2 * SB_BLK), 1)
    ones = col >= SB_BLK
    after2 = jnp.where(ones | (row > col), -1.0, 0.0).astype(BF16)
    from2 = jnp.where(ones | (row >= col), 1.0, 0.0).astype(BF16)
    r = lax.broadcasted_iota(jnp.int32, (SB_BLK, SB_BLK), 0)
    c = lax.broadcasted_iota(jnp.int32, (SB_BLK, SB_BLK), 1)
    return after2, from2, c - r, [c < HEAD_DIM, c >= HEAD_DIM]


def _suffix(xf, tri2):
    hi, lo = _split(xf)
    return _dot(jnp.concatenate([hi, lo], axis=1), tri2)


def _sb_logs(z, mask):
    softplus = jnp.maximum(z, 0.0) + jnp.log(1.0 + jnp.exp(-jnp.abs(z)))
    log_beta = z - softplus
    if mask is not None:
        softplus = jnp.where(mask, softplus, 0.0)
    return softplus, log_beta


def _sb_queries(q_ref, heads):
    q = q_ref[...].astype(F32) * QK_SCALE
    return [jnp.where(hm, q[r * SB_BLK:(r + 1) * SB_BLK], 0.0).astype(BF16) for r in range(SB_QB) for hm in heads]


def _sb_walk(i, block, state):
    places = SB_SUB // SB_QB
    assert places in (1, 2)
    own = lax.shift_right_logical(i * SB_QB, SB_SUB.bit_length() - 1)
    firsts = [[(v * SB_QB + r) * SB_BLK for r in range(SB_QB) for _ in range(2)] for v in range(places)]
    if places == 1:
        state = block(own, state, firsts[0])
    else:
        state = lax.cond(lax.bitwise_and(i, 1) == 0, lambda st: block(own, st, firsts[0]),
                         lambda st: block(own, st, firsts[1]), state)

    def live(carry):
        j, st = carry
        most = st[0][0]
        for run in st[0][1:]:
            most = jnp.maximum(most, run)
        return (j >= 0) & (jnp.max(most) > SB_DEAD)

    return lax.while_loop(live, lambda carry: (carry[0] - 1, block(carry[0], carry[1], None)), (own - 1, state))[1]


def _sb_tiles(first):
    out = []
    for c in reversed(range(SB_SUB)):
        for n in range(SB_CHAINS):
            if first is None or c * SB_BLK < first[n]:
                out.append((c, n, "before"))
            elif c * SB_BLK == first[n]:
                out.append((c, n, "diagonal"))
    return out


def _sb_heads_apart(stacked, heads, r):
    return jnp.where(heads[0], stacked[2 * r * SB_BLK:(2 * r + 1) * SB_BLK],
                     stacked[(2 * r + 1) * SB_BLK:(2 * r + 2) * SB_BLK])


def sb_fwd(proj, kv, name):
    s = proj.shape[0]
    assert s % SB_KEYS == 0 and SB_KEYS % SB_QROWS == 0

    def body(q_ref, k_ref, v_ref, o_ref, y_ref):
        after2, _, col_minus_row, heads = _sb_consts()
        q_all = jnp.concatenate(_sb_queries(q_ref, heads), axis=0)
        key_before_query = col_minus_row < 0

        def block(j, state, first):
            runs, acc = list(state[0]), state[1]
            rows = pl.ds(pl.multiple_of(j * SB_KEYS, SB_KEYS), SB_KEYS)
            kb, vb = k_ref[rows, :], v_ref[rows, :]
            z = _dot(q_all, kb, NT)
            pend = {}
            parts = [[jnp.zeros((SB_BLK, SB_BLK), BF16)] * SB_SUB for _ in range(SB_CHAINS)]
            for c, n, where in _sb_tiles(first):
                mask = key_before_query if where == "diagonal" else None
                softplus, lb = _sb_logs(z[n * SB_BLK:(n + 1) * SB_BLK, c * SB_BLK:(c + 1) * SB_BLK], mask)
                pend[c, n] = (lb, _suffix(softplus, after2), mask)
            for c, n, _ in _sb_tiles(first):
                lb, r, mask = pend.pop((c, n))
                a = jnp.exp(lb + r[:, :SB_BLK] + runs[n])
                if mask is not None:
                    a = jnp.where(mask, a, 0.0)
                parts[n][c] = a.astype(BF16)
                runs[n] = runs[n] + r[:, SB_BLK:]
            a_all = jnp.concatenate([jnp.concatenate(p, axis=1) for p in parts], axis=0)
            return tuple(runs), acc + _dot(a_all, vb)

        zero = jnp.zeros((SB_BLK, LANES), F32)
        state = _sb_walk(pl.program_id(1), block, ((zero,) * SB_CHAINS, jnp.zeros((SB_CHAINS * SB_BLK, LANES), F32)))
        for r in range(SB_QB):
            out = _sb_heads_apart(state[1], heads, r)
            o_ref[r * SB_BLK:(r + 1) * SB_BLK, :] = out
            y_ref[r * SB_BLK:(r + 1) * SB_BLK, :] = out.astype(BF16)

    pairs = SB_W // LANES
    block_spec = pl.BlockSpec((SB_QROWS, LANES), lambda p, i: (i, p))
    return pl.pallas_call(
        body, name=name, grid=(pairs, s // SB_QROWS),
        in_specs=[block_spec, pl.BlockSpec((s, LANES), lambda p, i: (0, p)),
                  pl.BlockSpec((s, LANES), lambda p, i: (0, pairs + p))],
        out_specs=[block_spec, block_spec],
        out_shape=[jax.ShapeDtypeStruct((s, SB_W), F32), jax.ShapeDtypeStruct((s, SB_W + MEM_W), BF16)],
        compiler_params=_params("arbitrary", "arbitrary"),
    )(proj, kv, kv)


def sb_bwd(proj, kv, out, dy, name):
    s = proj.shape[0]

    def body(q_ref, k_ref, v_ref, o_ref, do_ref, dq_ref, dk_ref, dv_ref):
        i = pl.program_id(1)

        @pl.when(i == 0)
        def _():
            dk_ref[...] = jnp.zeros_like(dk_ref)
            dv_ref[...] = jnp.zeros_like(dv_ref)

        after2, from2, col_minus_row, heads = _sb_consts()
        q_all = jnp.concatenate(_sb_queries(q_ref, heads), axis=0)
        key_before_query = col_minus_row < 0
        d_out = do_ref[...].astype(F32)
        prod = d_out * o_ref[...]
        dos, totals = [], []
        for r in range(SB_QB):
            rr = slice(r * SB_BLK, (r + 1) * SB_BLK)
            for hm in heads:
                dos.append(jnp.where(hm, d_out[rr], 0.0).astype(BF16))
                totals.append(jnp.broadcast_to(jnp.sum(jnp.where(hm, prod[rr], 0.0), axis=1, keepdims=True),
                                               (SB_BLK, SB_BLK)))
        do_all = jnp.concatenate(dos, axis=0)

        def block(j, state, first):
            runs, seens, dq = list(state[0]), list(state[1]), state[2]
            rows = pl.ds(pl.multiple_of(j * SB_KEYS, SB_KEYS), SB_KEYS)
            kb, vb = k_ref[rows, :], v_ref[rows, :]
            z = _dot(q_all, kb, NT)
            da = _dot(do_all, vb, NT)
            pend, pend2 = {}, {}
            a_parts = [[jnp.zeros((SB_BLK, SB_BLK), BF16)] * SB_SUB for _ in range(SB_CHAINS)]
            dz_parts = [[jnp.zeros((SB_BLK, SB_BLK), BF16)] * SB_SUB for _ in range(SB_CHAINS)]
            for c, n, where in _sb_tiles(first):
                mask = key_before_query if where == "diagonal" else None
                softplus, lb = _sb_logs(z[n * SB_BLK:(n + 1) * SB_BLK, c * SB_BLK:(c + 1) * SB_BLK], mask)
                pend[c, n] = (softplus, lb, _suffix(softplus, after2), mask)
            for c, n, _ in _sb_tiles(first):
                softplus, lb, r, mask = pend.pop((c, n))
                a = jnp.exp(lb + r[:, :SB_BLK] + runs[n])
                if mask is not None:
                    a = jnp.where(mask, a, 0.0)
                runs[n] = runs[n] + r[:, SB_BLK:]
                ab = a.astype(BF16)
                a_parts[n][c] = ab
                dl = ab.astype(F32) * da[n * SB_BLK:(n + 1) * SB_BLK, c * SB_BLK:(c + 1) * SB_BLK]
                pend2[c, n] = (softplus, lb, dl, _suffix(dl, from2), mask)
            for c, n, _ in _sb_tiles(first):
                softplus, lb, dl, r2, mask = pend2.pop((c, n))
                d_lom = totals[n] - (r2[:, :SB_BLK] + seens[n])
                if mask is not None:
                    d_lom = jnp.where(mask, d_lom, 0.0)
                seens[n] = seens[n] + r2[:, SB_BLK:]
                dz_parts[n][c] = (dl * jnp.exp(-softplus) - d_lom * jnp.exp(lb)).astype(BF16)
            a_all = jnp.concatenate([jnp.concatenate(p, axis=1) for p in a_parts], axis=0)
            dz_all = jnp.concatenate([jnp.concatenate(p, axis=1) for p in dz_parts], axis=0)
            dv_ref[rows, :] += _dot(a_all, do_all, TN)
            dk_ref[rows, :] += _dot(dz_all, q_all, TN)
            return tuple(runs), tuple(seens), dq + _dot(dz_all, kb)

        zero = jnp.zeros((SB_BLK, LANES), F32)
        state = _sb_walk(i, block, ((zero,) * SB_CHAINS, (zero,) * SB_CHAINS,
                                    jnp.zeros((SB_CHAINS * SB_BLK, LANES), F32)))
        for r in range(SB_QB):
            dq_ref[r * SB_BLK:(r + 1) * SB_BLK, :] = (_sb_heads_apart(state[2], heads, r) * QK_SCALE).astype(BF16)

    pairs = SB_W // LANES
    blk = lambda p, i: (i, p)
    col = lambda p, i: (0, p)
    return pl.pallas_call(
        body, name=name, grid=(pairs, s // SB_QROWS),
        in_specs=[pl.BlockSpec((SB_QROWS, LANES), blk), pl.BlockSpec((s, LANES), col),
                  pl.BlockSpec((s, LANES), lambda p, i: (0, pairs + p)), pl.BlockSpec((SB_QROWS, LANES), blk),
                  pl.BlockSpec((SB_QROWS, LANES), blk)],
        out_specs=[pl.BlockSpec((SB_QROWS, LANES), blk), pl.BlockSpec((s, LANES), col), pl.BlockSpec((s, LANES), col)],
        out_shape=[jax.ShapeDtypeStruct((s, SB_W + MEM_W), BF16), jax.ShapeDtypeStruct((s, SB_W), F32),
                   jax.ShapeDtypeStruct((s, SB_W), F32)],
        compiler_params=_params("arbitrary", "arbitrary"),
    )(proj, kv, kv, out, dy)


def final_loss(x, g, target, name):
    s, d = x.shape
    tm = _tile(s, 256, 8)

    def body(x_ref, g_ref, t_ref, loss_ref, dx_ref, dg_ref):
        @pl.when(pl.program_id(0) == 0)
        def _():
            loss_ref[...] = jnp.zeros_like(loss_ref)
            dg_ref[...] = jnp.zeros_like(dg_ref)

        xf = x_ref[...]
        r = _rms(xf)
        xhat = xf * r
        gain = g_ref[...]
        diff = xhat * gain - t_ref[...]
        sq = jnp.sum(jnp.sum(diff * diff, axis=1, keepdims=True), axis=0, keepdims=True)
        loss_ref[...] += jnp.broadcast_to(sq, loss_ref.shape)
        dy = diff * (1.0 / d)
        dg_ref[...] += jnp.sum(dy * xhat, axis=0, keepdims=True)
        dxhat = dy * gain
        dx_ref[...] = r * (dxhat - xhat * jnp.mean(dxhat * xhat, axis=-1, keepdims=True))

    row = lambda i: (i, 0)
    const = lambda i: (0, 0)
    return pl.pallas_call(
        body, name=name, grid=(s // tm,),
        in_specs=[pl.BlockSpec((tm, d), row), pl.BlockSpec((1, d), const), pl.BlockSpec((tm, d), row)],
        out_specs=[pl.BlockSpec((8, LANES), const), pl.BlockSpec((tm, d), row), pl.BlockSpec((1, d), const)],
        out_shape=[jax.ShapeDtypeStruct((8, LANES), F32), jax.ShapeDtypeStruct((s, d), F32), jax.ShapeDtypeStruct((1, d), F32)],
        compiler_params=_params("arbitrary"),
    )(x, g, target)


def adamw(w, parts, m, v, name):
    rows, cols = w.shape
    k = parts.shape[0]
    tr = _tile(rows, 512, 16)
    c1, c2 = 1.0 - ADAM_B1 ** ADAM_STEP, 1.0 - ADAM_B2 ** ADAM_STEP

    def body(w_ref, p_ref, m_ref, v_ref, g_ref, d_ref, nm_ref, nv_ref):
        grad = p_ref[0].astype(F32)
        for s in range(1, k):
            grad = grad + p_ref[s].astype(F32)
        nm = ADAM_B1 * m_ref[...] + (1.0 - ADAM_B1) * grad
        nv = ADAM_B2 * v_ref[...] + (1.0 - ADAM_B2) * (grad * grad)
        g_ref[...] = grad
        d_ref[...] = -ADAM_LR * ((nm / c1) / (jnp.sqrt(nv / c2) + ADAM_EPS) + ADAM_WD * w_ref[...])
        nm_ref[...] = nm
        nv_ref[...] = nv

    spec = pl.BlockSpec((tr, cols), lambda i: (i, 0))
    shape = jax.ShapeDtypeStruct((rows, cols), F32)
    return pl.pallas_call(
        body, name=name, grid=(rows // tr,),
        in_specs=[spec, pl.BlockSpec((k, tr, cols), lambda i: (0, i, 0)), spec, spec],
        out_specs=[spec] * 4, out_shape=[shape] * 4,
        compiler_params=_params("arbitrary"),
    )(w, parts, m, v)


SHARDED = {"ffn1_w_gate": 2, "ffn1_w_up": 2, "ffn1_w_down": 1, "ffn2_w_gate": 2, "ffn2_w_up": 2, "ffn2_w_down": 1,
           "w_mem_kv": 1, "a_w_in": 2, "a_w_out": 1, "w_kv": 1, "b_w_in": 1, "b_w_out": 1}
SMALL = ["ffn1_norm", "mix_norm", "ffn2_norm", "mem_norm", "kv_norm", "final_norm", "a_v_norm", "a_w_spatial", "a_b_spatial"]
WEIGHTS = ["ffn1_norm", "ffn1_w_gate", "ffn1_w_up", "ffn1_w_down", "mix_norm", "ffn2_norm", "ffn2_w_gate", "ffn2_w_up",
           "ffn2_w_down", "mem_norm", "w_mem_kv", "a_w_in", "a_v_norm", "a_w_spatial", "a_b_spatial", "a_w_out", "kv_norm",
           "w_kv", "b_w_in", "b_w_out", "final_norm"]


def _all_sum(parts, name):
    flat = jnp.concatenate([p.reshape(-1) for p in parts])
    pad = (-flat.size) % (16 * LANES)
    buf = jnp.pad(flat, (0, pad)).reshape(-1, LANES)
    total = sum_leading(exchange([buf], "all", True, name)[0], F32, name + "_sum").reshape(-1)
    out, off = [], 0
    for p in parts:
        out.append(total[off:off + p.size].reshape(p.shape))
        off += p.size
    return out


def _device_index():
    return 4 * lax.axis_index("x") + 2 * lax.axis_index("y") + lax.axis_index("c")


def kernel(x, mem, ffn1_norm, ffn1_w_gate, ffn1_w_up, ffn1_w_down, mix_norm, ffn2_norm, ffn2_w_gate, ffn2_w_up, ffn2_w_down, mem_norm, w_mem_kv, a_w_in, a_v_norm, a_w_spatial, a_b_spatial, a_w_out, kv_norm, w_kv, b_w_in, b_w_out, final_norm, loss_target, m_ffn1_norm, m_ffn1_w_gate, m_ffn1_w_up, m_ffn1_w_down, m_mix_norm, m_ffn2_norm, m_ffn2_w_gate, m_ffn2_w_up, m_ffn2_w_down, m_mem_norm, m_w_mem_kv, m_a_w_in, m_a_v_norm, m_a_w_spatial, m_a_b_spatial, m_a_w_out, m_kv_norm, m_w_kv, m_b_w_in, m_b_w_out, m_final_norm, v_ffn1_norm, v_ffn1_w_gate, v_ffn1_w_up, v_ffn1_w_down, v_mix_norm, v_ffn2_norm, v_ffn2_w_gate, v_ffn2_w_up, v_ffn2_w_down, v_mem_norm, v_w_mem_kv, v_a_w_in, v_a_v_norm, v_a_w_spatial, v_a_b_spatial, v_a_w_out, v_kv_norm, v_w_kv, v_b_w_in, v_b_w_out, v_final_norm):
    weights = dict(ffn1_norm=ffn1_norm, ffn1_w_gate=ffn1_w_gate, ffn1_w_up=ffn1_w_up, ffn1_w_down=ffn1_w_down, mix_norm=mix_norm, ffn2_norm=ffn2_norm, ffn2_w_gate=ffn2_w_gate, ffn2_w_up=ffn2_w_up, ffn2_w_down=ffn2_w_down, mem_norm=mem_norm, w_mem_kv=w_mem_kv, a_w_in=a_w_in, a_v_norm=a_v_norm, a_w_spatial=a_w_spatial, a_b_spatial=a_b_spatial, a_w_out=a_w_out, kv_norm=kv_norm, w_kv=w_kv, b_w_in=b_w_in, b_w_out=b_w_out, final_norm=final_norm)
    mom1 = dict(ffn1_norm=m_ffn1_norm, ffn1_w_gate=m_ffn1_w_gate, ffn1_w_up=m_ffn1_w_up, ffn1_w_down=m_ffn1_w_down, mix_norm=m_mix_norm, ffn2_norm=m_ffn2_norm, ffn2_w_gate=m_ffn2_w_gate, ffn2_w_up=m_ffn2_w_up, ffn2_w_down=m_ffn2_w_down, mem_norm=m_mem_norm, w_mem_kv=m_w_mem_kv, a_w_in=m_a_w_in, a_v_norm=m_a_v_norm, a_w_spatial=m_a_w_spatial, a_b_spatial=m_a_b_spatial, a_w_out=m_a_w_out, kv_norm=m_kv_norm, w_kv=m_w_kv, b_w_in=m_b_w_in, b_w_out=m_b_w_out, final_norm=m_final_norm)
    mom2 = dict(ffn1_norm=v_ffn1_norm, ffn1_w_gate=v_ffn1_w_gate, ffn1_w_up=v_ffn1_w_up, ffn1_w_down=v_ffn1_w_down, mix_norm=v_mix_norm, ffn2_norm=v_ffn2_norm, ffn2_w_gate=v_ffn2_w_gate, ffn2_w_up=v_ffn2_w_up, ffn2_w_down=v_ffn2_w_down, mem_norm=v_mem_norm, w_mem_kv=v_w_mem_kv, a_w_in=v_a_w_in, a_v_norm=v_a_v_norm, a_w_spatial=v_a_w_spatial, a_b_spatial=v_a_b_spatial, a_w_out=v_a_w_out, kv_norm=v_kv_norm, w_kv=v_w_kv, b_w_in=v_b_w_in, b_w_out=v_b_w_out, final_norm=v_final_norm)

    dev = _device_index()
    xs, mem_in, target = x[0], mem[0], loss_target[0]
    d_model = xs.shape[1]
    shards = {n: weights[n] for n in SHARDED}

    def mix_keys(l):
        w_in, w_out, idx = ("a_w_in", "a_w_out", l) if l < N_A else ("b_w_in", "b_w_out", l - N_A)
        return (w_in, idx), (w_out, idx)

    def ffn_keys(ffn):
        return ([], []) if ffn is None else ([(ffn[0] + "_w_gate", ffn[1]), (ffn[0] + "_w_up", ffn[1])],
                                             [(ffn[0] + "_w_down", ffn[1])])

    def ffn_after(f, l):
        return ("ffn2", l) if f == "ffn1" else (("ffn1", l + 1) if l + 1 < DEPTH else None)

    def cut_axis(key):
        return SHARDED[key[0]] - (0 if key[1] is None else 1)

    def block(key):
        return (shards[key[0]] if key[1] is None else shards[key[0]][key[1]]).astype(BF16)

    def carrying(call, keys, same_src, source, store):
        if not keys:
            return call(None)
        result, arrived = call(Side([source(k) for k in keys], same_src))
        store.update(zip(keys, arrived))
        return result

    first_gu, first_down = ffn_keys(("ffn1", 0))
    first = first_gu + first_down + [mix_keys(0)[0], ("w_mem_kv", None)]
    landed = dict(zip(first, exchange([block(k) for k in first], "all", True, "gather_first")))
    assembled = {}

    def whole(n, l=None):
        if (n, l) not in assembled:
            got = landed[n, l]
            if cut_axis((n, l)) == 0:
                assembled[n, l] = got.reshape((-1,) + got.shape[2:])
            else:
                pieces = [got[d] for d in range(N_DEV)]
                if n.endswith("_w_gate"):
                    pieces += [landed[n.replace("_w_gate", "_w_up"), l][d] for d in range(N_DEV)]
                assembled[n, l] = jnp.concatenate(pieces, axis=cut_axis((n, l)))
        return assembled[n, l]

    def whole_gu(f, l):
        return whole(f + "_w_gate", l)

    vn_width = a_v_norm.shape[1]
    a_v_full = _all_sum([lax.dynamic_update_slice(jnp.zeros((N_A, N_DEV * vn_width), F32), a_v_norm, (0, dev * vn_width))],
                        "gather_v_norm")[0]
    row = lambda v: v.reshape(1, -1)
    w_mem_cat = whole("w_mem_kv").transpose(1, 0, 2).reshape(d_model, -1)
    bias = [jnp.repeat(a_b_spatial[i].T, GM_P, axis=1) for i in range(N_A)]

    mem_kv, mem_h = norm_mm(mem_in, row(mem_norm), w_mem_cat, BF16, "mem_kv", emit_h=True)

    def ffn_fwd(xin, f, l):
        keys_gu, keys_down = ffn_keys(ffn_after(f, l))
        if ffn_after(f, l) == ("ffn1", N_A):
            keys_down = keys_down + [("w_kv", None)]
        gu = carrying(lambda side: norm_mm(xin, row(weights[f + "_norm"][l]), whole_gu(f, l), BF16, "ffn_gu", side=side),
                      keys_gu, True, block, landed)
        out = carrying(lambda side: swiglu_mm_res(gu, whole(f + "_w_down", l), xin, 0.5, "ffn_down", side=side),
                       keys_down, True, block, landed)
        return out, gu

    saved = []
    kv = x_kv = None
    cur = xs
    for l in range(DEPTH):
        st = {"x0": cur}
        if l == N_A:
            x_kv = cur
            kv = norm_mm(cur, row(kv_norm), whole("w_kv"), BF16, "kv_proj")
        st["x1"], st["gu1"] = ffn_fwd(cur, "ffn1", l)
        key_in, key_out = mix_keys(l)
        proj = carrying(lambda side: norm_mm(st["x1"], row(mix_norm[l]), whole(*key_in), F32 if l < N_A else BF16,
                                             "a_proj" if l < N_A else "b_proj", side=side), [key_out], True, block, landed)
        if l < N_A:
            y_tok = gmlp_fwd(proj, row(a_v_full[l]), a_w_spatial[l], bias[l], "gmlp_fwd")
            st["y"] = mem_fwd(proj, 2 * GM_W // MEM_W, mem_kv, l, y_tok, GM_W // MEM_W, "mem_fwd_a")
        else:
            st["sb_out"], y_tok = sb_fwd(proj, kv, "sb_fwd")
            st["y"] = mem_fwd(proj, SB_W // MEM_W, mem_kv, l, y_tok, SB_W // MEM_W, "mem_fwd_b")
        st["proj"] = proj
        st["x2"] = carrying(lambda side: mm_res(st["y"], whole(*key_out), st["x1"], 1.0, "mix_out", side=side),
                            [mix_keys(l + 1)[0]] if l + 1 < DEPTH else [], True, block, landed)
        cur, st["gu2"] = ffn_fwd(st["x2"], "ffn2", l)
        saved.append(st)

    loss_blk, dx, d_final = final_loss(cur, row(final_norm), target, "final_loss")
    loss = lax.psum(loss_blk[0, 0] * (0.5 / d_model), AXES)

    grads = {n: [None] * weights[n].shape[0] for n in WEIGHTS if weights[n].ndim >= 2 and n not in ("w_kv",)}
    grads["final_norm"] = d_final.reshape(-1)
    d_mem_kv = [None] * DEPTH
    d_kv = []

    summed = {}

    def pieces(key):
        g = (grads[key[0]] if key[1] is None else grads[key[0]][key[1]]).astype(BF16)
        axis = cut_axis(key)
        cut = g.reshape(g.shape[:axis] + (N_DEV, g.shape[axis] // N_DEV) + g.shape[axis + 1:])
        return jnp.moveaxis(cut, axis, 0)

    def ffn_bwd(dx, xin, gu, f, l):
        keys_gu, keys_down = ffn_keys(ffn_after(f, l))
        keys_mix = []
        if f == "ffn2" and l + 1 < DEPTH:
            keys_mix = list(mix_keys(l + 1)) + ([("w_kv", None)] if l + 1 == N_A else [])
        d_gu = carrying(lambda side: mm_nt_swiglu_bwd(dx, whole(f + "_w_down", l), gu, 0.5, "ffn_dgu", side=side),
                        keys_down, False, pieces, summed)
        dx_new, d_gain, h = carrying(
            lambda side: mm_nt_normbwd(d_gu, whole_gu(f, l), xin, row(weights[f + "_norm"][l]), dx, "ffn_dx", side=side),
            keys_gu, False, pieces, summed)
        d_wgu = carrying(lambda side: mm_tn(h, d_gu, 1.0, "ffn_dwgu", tb_target=1408, side=side),
                         keys_mix, False, pieces, summed)
        d_wdown = swiglu_mm_tn(gu, dx, 0.5, "ffn_dwdown")
        half = d_wgu.shape[1] // 2
        grads[f + "_w_gate"][l], grads[f + "_w_up"][l] = d_wgu[:, :half], d_wgu[:, half:]
        grads[f + "_w_down"][l] = d_wdown
        grads[f + "_norm"][l] = d_gain.reshape(-1)
        return dx_new

    for l in reversed(range(DEPTH)):
        st = saved[l]
        dx = ffn_bwd(dx, st["x2"], st["gu2"], "ffn2", l)
        proj = st["proj"]
        (key_in, idx), (key_out, _) = mix_keys(l)
        w_in, w_out = whole(key_in, idx), whole(key_out, idx)
        dy = mm_nt(dx, w_out, 1.0, "mix_dy")
        grads[key_out][idx] = mm_tn(st["y"], dx, 1.0, "mix_dwout", tb_target=1024)
        if l < N_A:
            d_uv, d_ws, d_bs, d_vgain = gmlp_bwd(proj, dy, row(a_v_full[l]), a_w_spatial[l], bias[l], "gmlp_bwd")
            grads["a_w_spatial"][l], grads["a_b_spatial"][l], grads["a_v_norm"][l] = d_ws, d_bs[:, :, 0], d_vgain.reshape(-1)
            d_proj, d_k, d_v = mem_bwd(proj, 2 * GM_W // MEM_W, mem_kv, l, dy, GM_W // MEM_W, d_uv, "mem_bwd_a")
        else:
            d_qsb, d_ksb, d_vsb = sb_bwd(proj, kv, st["sb_out"], dy, "sb_bwd")
            d_kv.append(jnp.concatenate([d_ksb, d_vsb], axis=1))
            d_proj, d_k, d_v = mem_bwd(proj, SB_W // MEM_W, mem_kv, l, dy, SB_W // MEM_W, d_qsb, "mem_bwd_b")
        d_mem_kv[l] = jnp.concatenate([d_k, d_v], axis=1)
        dx, d_gain, h = mm_nt_normbwd(d_proj, w_in, st["x1"], row(mix_norm[l]), dx, "mix_dx")
        grads["mix_norm"][l] = d_gain.reshape(-1)
        grads[key_in][idx] = mm_tn(h, d_proj, 1.0, "mix_dwin")
        dx = ffn_bwd(dx, st["x0"], st["gu1"], "ffn1", l)
        if l == N_A:
            d_kv_b = sum_leading(jnp.stack(d_kv), BF16, "kv_dsum")
            dx, d_gain, h = mm_nt_normbwd(d_kv_b, whole("w_kv"), x_kv, row(kv_norm), dx, "kv_dx")
            grads["kv_norm"] = d_gain.reshape(-1)
            grads["w_kv"] = mm_tn(h, d_kv_b, 1.0, "kv_dw")

    d_mem_all = jnp.concatenate(d_mem_kv, axis=1).astype(BF16)
    _, d_gain, _ = mm_nt_normbwd(d_mem_all, w_mem_cat, mem_in, row(mem_norm), None, "mem_dnorm")
    grads["mem_norm"] = d_gain.reshape(-1)
    d_wmem = mm_tn(mem_h, d_mem_all, 1.0, "mem_dw")
    grads["w_mem_kv"] = d_wmem.reshape(d_model, DEPTH, -1).transpose(1, 0, 2)

    last = first + [mix_keys(0)[1]]
    summed.update(zip(last, exchange([pieces(k) for k in last], "all", False, "scatter_last")))
    parts = {n: summed[n, None] if (n, None) in summed else
             jnp.stack([summed[n, i] for i in range(weights[n].shape[0])], axis=1) for n in SHARDED}
    grads = {n: (jnp.stack(g) if isinstance(g, list) else g) for n, g in grads.items()}
    for n, g in zip(SMALL, _all_sum([grads[n] for n in SMALL], "sum_small")):
        parts[n] = g[None]
    parts["a_v_norm"] = lax.dynamic_slice(parts["a_v_norm"], (0, 0, dev * vn_width), (1,) + a_v_norm.shape)

    reduced, deltas, new_m, new_v = {}, {}, {}, {}
    for n in WEIGHTS:
        w = weights[n]
        view = (lambda a: a.reshape(-1, a.shape[-1]))
        res = adamw(view(w), parts[n].reshape(parts[n].shape[0], -1, w.shape[-1]), view(mom1[n]), view(mom2[n]), "adamw")
        reduced[n], deltas[n], new_m[n], new_v[n] = [r.reshape(w.shape) for r in res]

    return (loss, dx[None], *[reduced[n] for n in WEIGHTS], *[deltas[n] for n in WEIGHTS],
            *[new_m[n] for n in WEIGHTS], *[new_v[n] for n in WEIGHTS])
```

```python
import jax
import jax.numpy as jnp
from jax import lax
from jax.experimental import pallas as pl
from jax.experimental.pallas import tpu as pltpu

F32, BF16 = jnp.float32, jnp.bfloat16
MESH_ID = pl.DeviceIdType.MESH
AXES = ("x", "y", "c")
N_DEV = 8

EPS = 1e-6
DEPTH, N_A = 4, 2
GM_W, GM_GROUPS, GM_P = 768, 6, 128
MEM_W, MEM_HEADS, HEAD_DIM = 256, 4, 64
SB_W, SB_BLK = 768, 128
LANES = 128
QK_SCALE = HEAD_DIM ** -0.5
GELU_C, GELU_A = 0.7978845608028654, 0.044715

ADAM_LR, ADAM_B1, ADAM_B2, ADAM_EPS, ADAM_WD, ADAM_STEP = 0.001, 0.9, 0.999, 1e-08, 0.01, 10

VMEM_LIMIT = 56 * 1024 * 1024

NT = (((1,), (1,)), ((), ()))
TN = (((0,), (0,)), ((), ()))


def _params(*sem):
    return pltpu.CompilerParams(dimension_semantics=sem, vmem_limit_bytes=VMEM_LIMIT)


def _tile(n, target, mult=LANES):
    best = None
    for t in range(mult, min(n, target) + 1, mult):
        if n % t == 0:
            best = t
    return best if best is not None else n


def _dot(a, b, dims=None):
    if dims is None:
        return jnp.dot(a, b, preferred_element_type=F32)
    return lax.dot_general(a, b, dims, preferred_element_type=F32)


def exchange(srcs, group, same_src, name, split=False):
    size = {"pair": 2, "quad": 4, "all": 8}[group]
    n = len(srcs)
    chunk_shapes = [tuple(s.shape) if same_src else tuple(s.shape[1:]) for s in srcs]
    pieces = [cs[0] if split else 1 for cs in chunk_shapes]
    n_dma = sum(pieces)

    def body(*refs):
        src_refs, out_refs = refs[:n], refs[n:2 * n]
        send_sems, recv_sems, local_sems = refs[2 * n:]
        x, y, c = lax.axis_index("x"), lax.axis_index("y"), lax.axis_index("c")
        if group == "pair":
            me, dev = c, lambda p: (x, y, p)
        elif group == "quad":
            me, dev = 2 * x + y, lambda p: (p // 2, p % 2, c)
        else:
            me, dev = 4 * x + 2 * y + c, lambda p: (p // 4, (p // 2) % 2, p % 2)

        def chunk(t, idx):
            return src_refs[t] if same_src else src_refs[t].at[idx]

        def copies(k, idx, slot, peer):
            out, w = [], k * n_dma
            for t in range(n):
                src, dst = chunk(t, idx), out_refs[t].at[slot]
                for s_ref, d_ref in ([(src.at[u], dst.at[u]) for u in range(pieces[t])] if split else [(src, dst)]):
                    out.append(pltpu.make_async_remote_copy(
                        src_ref=s_ref, dst_ref=d_ref, send_sem=send_sems.at[w], recv_sem=recv_sems.at[w],
                        device_id=dev(peer), device_id_type=MESH_ID))
                    w += 1
            return out

        local = [pltpu.make_async_copy(chunk(t, me), out_refs[t].at[me], local_sems.at[t]) for t in range(n)]
        for cp in local:
            cp.start()
        sends = []
        for k in range(1, size):
            peer = (me + k) % size
            sends += copies(k, peer, me, peer)
        for cp in sends:
            cp.start()
        for k in range(1, size):
            sender = (me + size - k) % size
            for cp in copies(k, me, sender, sender):
                cp.wait_recv()
        for cp in sends:
            cp.wait_send()
        for cp in local:
            cp.wait()

    hbm = pl.BlockSpec(memory_space=pltpu.HBM)
    return pl.pallas_call(
        body, name=name,
        out_shape=[jax.ShapeDtypeStruct((size,) + cs, s.dtype) for cs, s in zip(chunk_shapes, srcs)],
        in_specs=[hbm] * n, out_specs=[hbm] * n,
        scratch_shapes=[pltpu.SemaphoreType.DMA((size * n_dma,)), pltpu.SemaphoreType.DMA((size * n_dma,)),
                        pltpu.SemaphoreType.DMA((n,))],
    )(*srcs)


class Side:
    def __init__(self, srcs, same_src):
        self.srcs, self.same_src, self.n = list(srcs), same_src, len(srcs)
        self.chunk_shapes = [tuple(s.shape) if same_src else tuple(s.shape[1:]) for s in srcs]

    def out_shapes(self):
        return [jax.ShapeDtypeStruct((N_DEV,) + cs, s.dtype) for cs, s in zip(self.chunk_shapes, self.srcs)]

    def scratch(self):
        return [pltpu.SemaphoreType.DMA((N_DEV * self.n,)), pltpu.SemaphoreType.DMA((N_DEV * self.n,))]

    def _copies(self, src_refs, land_refs, send_sems, recv_sems, outgoing):
        me = 4 * lax.axis_index("x") + 2 * lax.axis_index("y") + lax.axis_index("c")
        out = []
        for k in range(1, N_DEV):
            peer = (me + k) % N_DEV if outgoing else (me + N_DEV - k) % N_DEV
            for t in range(self.n):
                src = src_refs[t] if self.same_src else src_refs[t].at[peer if outgoing else me]
                out.append(pltpu.make_async_remote_copy(
                    src_ref=src, dst_ref=land_refs[t].at[me if outgoing else peer],
                    send_sem=send_sems.at[k * self.n + t], recv_sem=recv_sems.at[k * self.n + t],
                    device_id=(peer // 4, (peer // 2) % 2, peer % 2), device_id_type=MESH_ID))
        return out

    def _own(self, src_refs, land_refs, send_sems):
        me = 4 * lax.axis_index("x") + 2 * lax.axis_index("y") + lax.axis_index("c")
        return [pltpu.make_async_copy(src_refs[t] if self.same_src else src_refs[t].at[me], land_refs[t].at[me],
                                      send_sems.at[t]) for t in range(self.n)]

    def start(self, src_refs, land_refs, send_sems, recv_sems):
        for cp in self._own(src_refs, land_refs, send_sems) + self._copies(src_refs, land_refs, send_sems, recv_sems, True):
            cp.start()

    def wait(self, src_refs, land_refs, send_sems, recv_sems):
        for cp in self._copies(src_refs, land_refs, send_sems, recv_sems, False):
            cp.wait_recv()
        for cp in self._copies(src_refs, land_refs, send_sems, recv_sems, True):
            cp.wait_send()
        for cp in self._own(src_refs, land_refs, send_sems):
            cp.wait()


def _call(body, side, name, grid, in_specs, out_specs, out_shape, scratch_shapes, dims, args):
    if side is None:
        res = pl.pallas_call(body, name=name, grid=grid, in_specs=in_specs, out_specs=out_specs, out_shape=out_shape,
                             scratch_shapes=scratch_shapes, compiler_params=_params(*dims))(*args)
        return list(res), []
    n_in, n_out, n_scr, ns = len(in_specs), len(out_specs), len(scratch_shapes), side.n

    def wrapped(*refs):
        ins, srcs = refs[:n_in], refs[n_in:n_in + ns]
        outs, lands = refs[n_in + ns:n_in + ns + n_out], refs[n_in + ns + n_out:n_in + 2 * ns + n_out]
        scratch, (send_sems, recv_sems) = refs[n_in + 2 * ns + n_out:n_in + 2 * ns + n_out + n_scr], refs[-2:]
        first, last = None, None
        for axis, steps in enumerate(grid):
            i = pl.program_id(axis)
            first = (i == 0) if first is None else first & (i == 0)
            last = (i == steps - 1) if last is None else last & (i == steps - 1)

        @pl.when(first)
        def _():
            side.start(srcs, lands, send_sems, recv_sems)

        body(*ins, *outs, *scratch)

        @pl.when(last)
        def _():
            side.wait(srcs, lands, send_sems, recv_sems)

    hbm = pl.BlockSpec(memory_space=pltpu.HBM)
    res = pl.pallas_call(
        wrapped, name=name, grid=grid, in_specs=list(in_specs) + [hbm] * ns, out_specs=list(out_specs) + [hbm] * ns,
        out_shape=list(out_shape) + side.out_shapes(), scratch_shapes=list(scratch_shapes) + side.scratch(),
        compiler_params=_params(*dims))(*args, *side.srcs)
    return list(res[:n_out]), list(res[n_out:])


def sum_leading(parts, out_dtype, name):
    k, rows, cols = parts.shape
    tr = _tile(rows, 512, 16)

    def body(p_ref, o_ref):
        acc = p_ref[0].astype(F32)
        for s in range(1, k):
            acc = acc + p_ref[s].astype(F32)
        o_ref[...] = acc.astype(o_ref.dtype)

    return pl.pallas_call(
        body, name=name, grid=(rows // tr,),
        in_specs=[pl.BlockSpec((k, tr, cols), lambda i: (0, i, 0))],
        out_specs=pl.BlockSpec((tr, cols), lambda i: (i, 0)),
        out_shape=jax.ShapeDtypeStruct((rows, cols), out_dtype),
        compiler_params=_params("arbitrary"),
    )(parts)


def _rms(xf):
    return lax.rsqrt(jnp.mean(xf * xf, axis=-1, keepdims=True) + EPS)


def norm_mm(x, g, w, out_dtype, name, emit_h=False, side=None):
    m, d = x.shape
    n = w.shape[1]
    tm, tn = _tile(m, 1024, 8), _tile(n, 1408)

    def body(x_ref, g_ref, w_ref, o_ref, *rest):
        h_ref = rest[-1]

        @pl.when(pl.program_id(1) == 0)
        def _():
            xf = x_ref[...]
            hb = ((xf * _rms(xf)) * g_ref[...]).astype(BF16)
            h_ref[...] = hb
            if emit_h:
                rest[0][...] = hb

        o_ref[...] = _dot(h_ref[...], w_ref[...]).astype(o_ref.dtype)

    out_shape = [jax.ShapeDtypeStruct((m, n), out_dtype)]
    out_specs = [pl.BlockSpec((tm, tn), lambda i, j: (i, j))]
    if emit_h:
        out_shape.append(jax.ShapeDtypeStruct((m, d), BF16))
        out_specs.append(pl.BlockSpec((tm, d), lambda i, j: (i, 0)))
    res, landed = _call(
        body, side, name, (m // tm, n // tn),
        [pl.BlockSpec((tm, d), lambda i, j: (i, 0)), pl.BlockSpec((1, d), lambda i, j: (0, 0)),
         pl.BlockSpec((d, tn), lambda i, j: (0, j))],
        out_specs, out_shape, [pltpu.VMEM((tm, d), BF16)], ("arbitrary", "arbitrary"), (x, g, w))
    out = res if emit_h else res[0]
    return out if side is None else (out, landed)


def mm_res(a, w, res, alpha, name, side=None):
    m, k = a.shape
    n = w.shape[1]
    tm, tn = _tile(m, 1024, 8), _tile(n, 1024)

    def body(a_ref, w_ref, r_ref, o_ref):
        o_ref[...] = r_ref[...] + alpha * _dot(a_ref[...], w_ref[...])

    out, landed = _call(
        body, side, name, (m // tm, n // tn),
        [pl.BlockSpec((tm, k), lambda i, j: (i, 0)), pl.BlockSpec((k, tn), lambda i, j: (0, j)),
         pl.BlockSpec((tm, tn), lambda i, j: (i, j))],
        [pl.BlockSpec((tm, tn), lambda i, j: (i, j))], [jax.ShapeDtypeStruct((m, n), F32)], [],
        ("arbitrary", "arbitrary"), (a, w, res))
    return out[0] if side is None else (out[0], landed)


def mm_nt(x, w, alpha, name):
    m, d = x.shape
    n = w.shape[0]
    tm, tn = _tile(m, 1024, 8), _tile(n, 1408)

    def body(x_ref, w_ref, o_ref, xb_ref):
        @pl.when(pl.program_id(1) == 0)
        def _():
            xb_ref[...] = x_ref[...].astype(BF16)

        o_ref[...] = (alpha * _dot(xb_ref[...], w_ref[...], NT)).astype(o_ref.dtype)

    return pl.pallas_call(
        body, name=name, grid=(m // tm, n // tn),
        in_specs=[pl.BlockSpec((tm, d), lambda i, j: (i, 0)), pl.BlockSpec((tn, d), lambda i, j: (j, 0))],
        out_specs=pl.BlockSpec((tm, tn), lambda i, j: (i, j)),
        out_shape=jax.ShapeDtypeStruct((m, n), BF16),
        scratch_shapes=[pltpu.VMEM((tm, d), BF16)],
        compiler_params=_params("arbitrary", "arbitrary"),
    )(x, w)


def mm_tn(a, b, alpha, name, ta_target=1024, tb_target=512, side=None):
    s, ka = a.shape
    nb = b.shape[1]
    ta, tb, ts = _tile(ka, ta_target), _tile(nb, tb_target), _tile(s, 1024, 16)
    steps = s // ts

    def body(a_ref, b_ref, o_ref, acc_ref):
        t = pl.program_id(2)

        @pl.when(t == 0)
        def _():
            acc_ref[...] = jnp.zeros_like(acc_ref)

        acc_ref[...] += _dot(a_ref[...].astype(BF16), b_ref[...].astype(BF16), TN)

        @pl.when(t == steps - 1)
        def _():
            o_ref[...] = (alpha * acc_ref[...]).astype(o_ref.dtype)

    res, landed = _call(
        body, side, name, (ka // ta, nb // tb, steps),
        [pl.BlockSpec((ts, ta), lambda i, j, t: (t, i)), pl.BlockSpec((ts, tb), lambda i, j, t: (t, j))],
        [pl.BlockSpec((ta, tb), lambda i, j, t: (i, j))], [jax.ShapeDtypeStruct((ka, nb), BF16)],
        [pltpu.VMEM((ta, tb), F32)], ("arbitrary", "arbitrary", "arbitrary"), (a, b))
    return res[0] if side is None else (res[0], landed)


def mm_nt_normbwd(dy, w, x, g, res, name, side=None):
    m, n = dy.shape
    d = w.shape[0]
    tm, tk = _tile(m, 1024, 8), _tile(n, 1408)
    steps = n // tk
    has_res = res is not None

    def body(*refs):
        if has_res:
            dy_ref, w_ref, x_ref, g_ref, r_ref, dx_ref, dg_ref, h_ref, acc_ref = refs
        else:
            dy_ref, w_ref, x_ref, g_ref, dx_ref, dg_ref, h_ref, acc_ref = refs
        i, t = pl.program_id(0), pl.program_id(1)

        @pl.when(t == 0)
        def _():
            acc_ref[...] = jnp.zeros_like(acc_ref)

        @pl.when((t == 0) & (i == 0))
        def _():
            dg_ref[...] = jnp.zeros_like(dg_ref)

        acc_ref[...] += _dot(dy_ref[...], w_ref[...], NT)

        @pl.when(t == steps - 1)
        def _():
            xf = x_ref[...]
            r = _rms(xf)
            xhat = xf * r
            dh = acc_ref[...]
            gain = g_ref[...]
            dg_ref[...] += jnp.sum(dh * xhat, axis=0, keepdims=True)
            dxhat = dh * gain
            dx = r * (dxhat - xhat * jnp.mean(dxhat * xhat, axis=-1, keepdims=True))
            dx_ref[...] = (r_ref[...] + dx) if has_res else dx
            h_ref[...] = (xhat * gain).astype(BF16)

    row = lambda i, t: (i, 0)
    in_specs = [pl.BlockSpec((tm, tk), lambda i, t: (i, t)), pl.BlockSpec((d, tk), lambda i, t: (0, t)),
                pl.BlockSpec((tm, d), row), pl.BlockSpec((1, d), lambda i, t: (0, 0))]
    args = [dy, w, x, g]
    if has_res:
        in_specs.append(pl.BlockSpec((tm, d), row))
        args.append(res)
    res, landed = _call(
        body, side, name, (m // tm, steps), in_specs,
        [pl.BlockSpec((tm, d), row), pl.BlockSpec((1, d), lambda i, t: (0, 0)), pl.BlockSpec((tm, d), row)],
        [jax.ShapeDtypeStruct((m, d), F32), jax.ShapeDtypeStruct((1, d), F32), jax.ShapeDtypeStruct((m, d), BF16)],
        [pltpu.VMEM((tm, d), F32)], ("arbitrary", "arbitrary"), args)
    return res if side is None else (res, landed)


def _sigmoid(z):
    return 1.0 / (1.0 + jnp.exp(-z))


def _swiglu(gate_b, up_b):
    gate = gate_b.astype(F32)
    return (gate * _sigmoid(gate) * up_b.astype(F32)).astype(BF16)


def swiglu_mm_res(gu, w, res, alpha, name, side=None):
    m, f2 = gu.shape
    f, n = w.shape
    tm, tc = _tile(m, 256, 16), _tile(f, 256)

    def body(gu_ref, w_ref, r_ref, o_ref):
        acc = jnp.zeros((tm, n), F32)
        for c0 in range(0, f, tc):
            act = _swiglu(gu_ref[:, c0:c0 + tc], gu_ref[:, f + c0:f + c0 + tc])
            acc = acc + _dot(act, w_ref[c0:c0 + tc, :])
        o_ref[...] = r_ref[...] + alpha * acc

    out, landed = _call(
        body, side, name, (m // tm,),
        [pl.BlockSpec((tm, f2), lambda i: (i, 0)), pl.BlockSpec((f, n), lambda i: (0, 0)),
         pl.BlockSpec((tm, n), lambda i: (i, 0))],
        [pl.BlockSpec((tm, n), lambda i: (i, 0))], [jax.ShapeDtypeStruct((m, n), F32)], [], ("arbitrary",), (gu, w, res))
    return out[0] if side is None else (out[0], landed)


def swiglu_mm_tn(gu, b, alpha, name, side=None):
    s, f2 = gu.shape
    f, n = f2 // 2, b.shape[1]
    ta, ts = _tile(f, 1408), _tile(s, 512, 16)
    steps, half = s // ts, f // ta

    def body(g_ref, u_ref, b_ref, o_ref, acc_ref):
        t = pl.program_id(1)

        @pl.when(t == 0)
        def _():
            acc_ref[...] = jnp.zeros_like(acc_ref)

        bb = b_ref[...].astype(BF16)
        for c0 in range(0, ta, LANES):
            acc_ref[c0:c0 + LANES, :] += _dot(_swiglu(g_ref[:, c0:c0 + LANES], u_ref[:, c0:c0 + LANES]), bb, TN)

        @pl.when(t == steps - 1)
        def _():
            o_ref[...] = (alpha * acc_ref[...]).astype(o_ref.dtype)

    res, landed = _call(
        body, side, name, (half, steps),
        [pl.BlockSpec((ts, ta), lambda i, t: (t, i)), pl.BlockSpec((ts, ta), lambda i, t: (t, half + i)),
         pl.BlockSpec((ts, n), lambda i, t: (t, 0))],
        [pl.BlockSpec((ta, n), lambda i, t: (i, 0))], [jax.ShapeDtypeStruct((f, n), BF16)],
        [pltpu.VMEM((ta, n), F32)], ("arbitrary", "arbitrary"), (gu, gu, b))
    return res[0] if side is None else (res[0], landed)


def mm_nt_swiglu_bwd(x, w, gu, alpha, name, side=None):
    m, d = x.shape
    f = w.shape[0]
    tm, tc = _tile(m, 256, 16), _tile(f, 256)

    def body(x_ref, w_ref, gu_ref, o_ref):
        xb = x_ref[...].astype(BF16)
        for c0 in range(0, f, tc):
            d_act = alpha * _dot(xb, w_ref[c0:c0 + tc, :], NT)
            gate, up = gu_ref[:, c0:c0 + tc].astype(F32), gu_ref[:, f + c0:f + c0 + tc].astype(F32)
            sg = _sigmoid(gate)
            o_ref[:, c0:c0 + tc] = (d_act * up * (sg * (1.0 + gate * (1.0 - sg)))).astype(BF16)
            o_ref[:, f + c0:f + c0 + tc] = (d_act * (gate * sg)).astype(BF16)

    res, landed = _call(
        body, side, name, (m // tm,),
        [pl.BlockSpec((tm, d), lambda i: (i, 0)), pl.BlockSpec((f, d), lambda i: (0, 0)),
         pl.BlockSpec((tm, 2 * f), lambda i: (i, 0))],
        [pl.BlockSpec((tm, 2 * f), lambda i: (i, 0))], [jax.ShapeDtypeStruct((m, 2 * f), BF16)], [], ("arbitrary",),
        (x, w, gu))
    return res[0] if side is None else (res[0], landed)


def _gelu(x):
    return 0.5 * x * (1.0 + jnp.tanh(GELU_C * (x + GELU_A * x * x * x)))


def _gelu_grad(x):
    t = jnp.tanh(GELU_C * (x + GELU_A * x * x * x))
    return 0.5 * (1.0 + t) + 0.5 * x * (1.0 - t * t) * (GELU_C * (1.0 + 3.0 * GELU_A * x * x))


def _chunk_mask():
    row = lax.broadcasted_iota(jnp.int32, (GM_P, GM_P), 0)
    col = lax.broadcasted_iota(jnp.int32, (GM_P, GM_P), 1)
    return (col < GM_P // 2) | (row >= GM_P // 2)


def gmlp_fwd(proj, gain, w_s, bias, name):
    s, pw = proj.shape
    tm = _tile(s, 256, GM_P)

    def body(p_ref, gain_ref, w_ref, b_ref, o_ref):
        mask = _chunk_mask()
        u = _gelu(p_ref[:, :GM_W])
        v = _gelu(p_ref[:, GM_W:2 * GM_W])
        vn = ((v * _rms(v)) * gain_ref[...]).astype(BF16)
        for g in range(GM_GROUPS):
            wg = jnp.where(mask, w_ref[g], 0.0).astype(BF16)
            cols = slice(g * GM_P, (g + 1) * GM_P)
            for n in range(tm // GM_P):
                rows = slice(n * GM_P, (n + 1) * GM_P)
                mixed = _dot(wg, vn[rows, cols]) + b_ref[:, cols]
                o_ref[rows, cols] = (u[rows, cols] * mixed).astype(BF16)

    return pl.pallas_call(
        body, name=name, grid=(s // tm,),
        in_specs=[pl.BlockSpec((tm, pw), lambda i: (i, 0)), pl.BlockSpec((1, GM_W), lambda i: (0, 0)),
                  pl.BlockSpec((GM_GROUPS, GM_P, GM_P), lambda i: (0, 0, 0)), pl.BlockSpec((GM_P, GM_W), lambda i: (0, 0))],
        out_specs=pl.BlockSpec((tm, GM_W), lambda i: (i, 0)),
        out_shape=jax.ShapeDtypeStruct((s, GM_W + MEM_W), BF16), compiler_params=_params("arbitrary"),
    )(proj, gain, w_s, bias)


def gmlp_bwd(proj, dy, gain, w_s, bias, name):
    s, pw = proj.shape
    dw_total = dy.shape[1]
    tm = _tile(s, 256, GM_P)

    def body(p_ref, dy_ref, gain_ref, w_ref, b_ref, dp_ref, dw_ref, db_ref, dgain_ref, dvn_ref):
        @pl.when(pl.program_id(0) == 0)
        def _():
            dw_ref[...] = jnp.zeros_like(dw_ref)
            db_ref[...] = jnp.zeros_like(db_ref)
            dgain_ref[...] = jnp.zeros_like(dgain_ref)

        mask = _chunk_mask()
        pu = p_ref[:, :GM_W]
        pv = p_ref[:, GM_W:2 * GM_W]
        u = _gelu(pu)
        v = _gelu(pv)
        r = _rms(v)
        vhat = v * r
        gain = gain_ref[...]
        vn = (vhat * gain).astype(BF16)
        gu_grad = _gelu_grad(pu)
        for g in range(GM_GROUPS):
            wg = jnp.where(mask, w_ref[g], 0.0).astype(BF16)
            cols = slice(g * GM_P, (g + 1) * GM_P)
            dw_acc = jnp.zeros((GM_P, GM_P), F32)
            db_acc = jnp.zeros((GM_P, 1), F32)
            for n in range(tm // GM_P):
                rows = slice(n * GM_P, (n + 1) * GM_P)
                dyb = dy_ref[rows, cols].astype(F32)
                vnb = vn[rows, cols]
                mixed = _dot(wg, vnb) + b_ref[:, cols]
                dmixed = dyb * u[rows, cols]
                dmb = dmixed.astype(BF16)
                dp_ref[rows, cols] = (dyb * mixed * gu_grad[rows, cols]).astype(BF16)
                dw_acc = dw_acc + _dot(dmb, vnb, NT)
                db_acc = db_acc + jnp.sum(dmixed, axis=1, keepdims=True)
                dvn_ref[rows, cols] = _dot(wg, dmb, TN)
            dw_ref[g] += jnp.where(mask, dw_acc, 0.0)
            db_ref[g] += jnp.broadcast_to(db_acc, (GM_P, GM_P))
        dvn = dvn_ref[...]
        dgain_ref[...] += jnp.sum(dvn * vhat, axis=0, keepdims=True)
        dvhat = dvn * gain
        dv = r * (dvhat - vhat * jnp.mean(dvhat * vhat, axis=-1, keepdims=True))
        dp_ref[:, GM_W:] = (dv * _gelu_grad(pv)).astype(BF16)

    const3 = lambda i: (0, 0, 0)
    return pl.pallas_call(
        body, name=name, grid=(s // tm,),
        in_specs=[pl.BlockSpec((tm, pw), lambda i: (i, 0)), pl.BlockSpec((tm, dw_total), lambda i: (i, 0)),
                  pl.BlockSpec((1, GM_W), lambda i: (0, 0)), pl.BlockSpec((GM_GROUPS, GM_P, GM_P), const3),
                  pl.BlockSpec((GM_P, GM_W), lambda i: (0, 0))],
        out_specs=[pl.BlockSpec((tm, 2 * GM_W), lambda i: (i, 0)), pl.BlockSpec((GM_GROUPS, GM_P, GM_P), const3),
                   pl.BlockSpec((GM_GROUPS, GM_P, GM_P), const3), pl.BlockSpec((1, GM_W), lambda i: (0, 0))],
        out_shape=[jax.ShapeDtypeStruct((s, pw), BF16), jax.ShapeDtypeStruct((GM_GROUPS, GM_P, GM_P), F32),
                   jax.ShapeDtypeStruct((GM_GROUPS, GM_P, GM_P), F32), jax.ShapeDtypeStruct((1, GM_W), F32)],
        scratch_shapes=[pltpu.VMEM((tm, GM_W), F32)],
        compiler_params=_params("arbitrary"),
    )(proj, dy, gain, w_s, bias)


def _keep(mask, xb):
    return jnp.where(mask, xb.astype(F32), 0.0).astype(BF16)


def _head_masks(rows, width, heads):
    lane = lax.broadcasted_iota(jnp.int32, (rows, width), 1)
    return [(lane >= HEAD_DIM * h) & (lane < HEAD_DIM * (h + 1)) for h in range(heads)]


def _mem_probs(qh, k):
    sc = _dot(qh, k, NT) * QK_SCALE
    e = jnp.exp(sc - jnp.max(sc, axis=-1, keepdims=True))
    return e / jnp.sum(e, axis=-1, keepdims=True)


def mem_fwd(proj, q_blk, mem_kv, layer, into, into_blk, name):
    s = proj.shape[0]
    n_mem = mem_kv.shape[0]
    tm = _tile(s, 512, 16)

    def body(q_ref, k_ref, v_ref, into_ref, o_ref):
        q = q_ref[...].astype(BF16)
        k, v = k_ref[...], v_ref[...]
        out = jnp.zeros((tm, MEM_W), F32)
        for hm in _head_masks(tm, MEM_W, MEM_HEADS):
            p = _mem_probs(_keep(hm, q), k)
            out = out + jnp.where(hm, _dot(p.astype(BF16), v), 0.0)
        o_ref[...] = out.astype(BF16)

    return pl.pallas_call(
        body, name=name, grid=(s // tm,),
        in_specs=[pl.BlockSpec((tm, MEM_W), lambda i: (i, q_blk)), pl.BlockSpec((n_mem, MEM_W), lambda i: (0, 2 * layer)),
                  pl.BlockSpec((n_mem, MEM_W), lambda i: (0, 2 * layer + 1)), pl.BlockSpec(memory_space=pl.ANY)],
        out_specs=pl.BlockSpec((tm, MEM_W), lambda i: (i, into_blk)),
        out_shape=jax.ShapeDtypeStruct(into.shape, BF16), input_output_aliases={3: 0},
        compiler_params=_params("arbitrary"),
    )(proj, mem_kv, mem_kv, into)


def mem_bwd(proj, q_blk, mem_kv, layer, dy, dy_blk, into, name):
    s = proj.shape[0]
    n_mem = mem_kv.shape[0]
    tm = _tile(s, 512, 16)

    def body(q_ref, k_ref, v_ref, dy_ref, into_ref, dq_ref, dk_ref, dv_ref):
        @pl.when(pl.program_id(0) == 0)
        def _():
            dk_ref[...] = jnp.zeros_like(dk_ref)
            dv_ref[...] = jnp.zeros_like(dv_ref)

        q = q_ref[...].astype(BF16)
        k, v = k_ref[...], v_ref[...]
        dy = dy_ref[...]
        dq = jnp.zeros((tm, MEM_W), F32)
        dk = jnp.zeros((n_mem, MEM_W), F32)
        dv = jnp.zeros((n_mem, MEM_W), F32)
        for hm in _head_masks(tm, MEM_W, MEM_HEADS):
            qh = _keep(hm, q)
            dyh = _keep(hm, dy)
            p = _mem_probs(qh, k)
            dp = _dot(dyh, v, NT)
            dv = dv + _dot(p.astype(BF16), dyh, TN)
            ds = (p * (dp - jnp.sum(dp * p, axis=-1, keepdims=True)) * QK_SCALE).astype(BF16)
            dq = dq + jnp.where(hm, _dot(ds, k), 0.0)
            dk = dk + _dot(ds, qh, TN)
        dq_ref[...] = dq.astype(BF16)
        dk_ref[...] += dk
        dv_ref[...] += dv

    const = lambda i: (0, 0)
    return pl.pallas_call(
        body, name=name, grid=(s // tm,),
        in_specs=[pl.BlockSpec((tm, MEM_W), lambda i: (i, q_blk)), pl.BlockSpec((n_mem, MEM_W), lambda i: (0, 2 * layer)),
                  pl.BlockSpec((n_mem, MEM_W), lambda i: (0, 2 * layer + 1)), pl.BlockSpec((tm, MEM_W), lambda i: (i, dy_blk)),
                  pl.BlockSpec(memory_space=pl.ANY)],
        out_specs=[pl.BlockSpec((tm, MEM_W), lambda i: (i, q_blk)), pl.BlockSpec((n_mem, MEM_W), const),
                   pl.BlockSpec((n_mem, MEM_W), const)],
        out_shape=[jax.ShapeDtypeStruct(into.shape, BF16), jax.ShapeDtypeStruct((n_mem, MEM_W), F32),
                   jax.ShapeDtypeStruct((n_mem, MEM_W), F32)],
        input_output_aliases={4: 0}, compiler_params=_params("arbitrary"),
    )(proj, mem_kv, mem_kv, dy, into)


SB_KEYS = 256
SB_SUB = SB_KEYS // SB_BLK
SB_QROWS = 256
SB_QB = SB_QROWS // SB_BLK
SB_CHAINS = 2 * SB_QB
SB_DEAD = -110.0


def _split(xf):
    hi = xf.astype(BF16)
    return hi, (xf - hi.astype(F32)).astype(BF16)


def _sb_consts():
    row = lax.bitwise_and(lax.broadcasted_iota(jnp.int32, (2 * SB_BLK, 2 * SB_BLK), 0), SB_BLK - 1)
    col = lax.broadcasted_iota(jnp.int32, (2 * SB_BLK, 2 * SB_BLK), 1)
    ones = col >= SB_BLK
    after2 = jnp.where(ones | (row > col), -1.0, 0.0).astype(BF16)
    from2 = jnp.where(ones | (row >= col), 1.0, 0.0).astype(BF16)
    r = lax.broadcasted_iota(jnp.int32, (SB_BLK, SB_BLK), 0)
    c = lax.broadcasted_iota(jnp.int32, (SB_BLK, SB_BLK), 1)
    return after2, from2, c - r, [c < HEAD_DIM, c >= HEAD_DIM]


def _suffix(xf, tri2):
    hi, lo = _split(xf)
    return _dot(jnp.concatenate([hi, lo], axis=1), tri2)


def _sb_logs(z, mask):
    softplus = jnp.maximum(z, 0.0) + jnp.log(1.0 + jnp.exp(-jnp.abs(z)))
    log_beta = z - softplus
    if mask is not None:
        softplus = jnp.where(mask, softplus, 0.0)
    return softplus, log_beta


def _sb_queries(q_ref, heads):
    q = q_ref[...].astype(F32) * QK_SCALE
    return [jnp.where(hm, q[r * SB_BLK:(r + 1) * SB_BLK], 0.0).astype(BF16) for r in range(SB_QB) for hm in heads]


def _sb_walk(i, block, state):
    places = SB_SUB // SB_QB
    assert places in (1, 2)
    own = lax.shift_right_logical(i * SB_QB, SB_SUB.bit_length() - 1)
    firsts = [[(v * SB_QB + r) * SB_BLK for r in range(SB_QB) for _ in range(2)] for v in range(places)]
    if places == 1:
        state = block(own, state, firsts[0])
    else:
        state = lax.cond(lax.bitwise_and(i, 1) == 0, lambda st: block(own, st, firsts[0]),
                         lambda st: block(own, st, firsts[1]), state)

    def live(carry):
        j, st = carry
        most = st[0][0]
        for run in st[0][1:]:
            most = jnp.maximum(most, run)
        return (j >= 0) & (jnp.max(most) > SB_DEAD)

    return lax.while_loop(live, lambda carry: (carry[0] - 1, block(carry[0], carry[1], None)), (own - 1, state))[1]


def _sb_tiles(first):
    out = []
    for c in reversed(range(SB_SUB)):
        for n in range(SB_CHAINS):
            if first is None or c * SB_BLK < first[n]:
                out.append((c, n, "before"))
            elif c * SB_BLK == first[n]:
                out.append((c, n, "diagonal"))
    return out


def _sb_heads_apart(stacked, heads, r):
    return jnp.where(heads[0], stacked[2 * r * SB_BLK:(2 * r + 1) * SB_BLK],
                     stacked[(2 * r + 1) * SB_BLK:(2 * r + 2) * SB_BLK])


def sb_fwd(proj, kv, name):
    s = proj.shape[0]
    assert s % SB_KEYS == 0 and SB_KEYS % SB_QROWS == 0

    def body(q_ref, k_ref, v_ref, o_ref, y_ref):
        after2, _, col_minus_row, heads = _sb_consts()
        q_all = jnp.concatenate(_sb_queries(q_ref, heads), axis=0)
        key_before_query = col_minus_row < 0

        def block(j, state, first):
            runs, acc = list(state[0]), state[1]
            rows = pl.ds(pl.multiple_of(j * SB_KEYS, SB_KEYS), SB_KEYS)
            kb, vb = k_ref[rows, :], v_ref[rows, :]
            z = _dot(q_all, kb, NT)
            pend = {}
            parts = [[jnp.zeros((SB_BLK, SB_BLK), BF16)] * SB_SUB for _ in range(SB_CHAINS)]
            for c, n, where in _sb_tiles(first):
                mask = key_before_query if where == "diagonal" else None
                softplus, lb = _sb_logs(z[n * SB_BLK:(n + 1) * SB_BLK, c * SB_BLK:(c + 1) * SB_BLK], mask)
                pend[c, n] = (lb, _suffix(softplus, after2), mask)
            for c, n, _ in _sb_tiles(first):
                lb, r, mask = pend.pop((c, n))
                a = jnp.exp(lb + r[:, :SB_BLK] + runs[n])
                if mask is not None:
                    a = jnp.where(mask, a, 0.0)
                parts[n][c] = a.astype(BF16)
                runs[n] = runs[n] + r[:, SB_BLK:]
            a_all = jnp.concatenate([jnp.concatenate(p, axis=1) for p in parts], axis=0)
            return tuple(runs), acc + _dot(a_all, vb)

        zero = jnp.zeros((SB_BLK, LANES), F32)
        state = _sb_walk(pl.program_id(1), block, ((zero,) * SB_CHAINS, jnp.zeros((SB_CHAINS * SB_BLK, LANES), F32)))
        for r in range(SB_QB):
            out = _sb_heads_apart(state[1], heads, r)
            o_ref[r * SB_BLK:(r + 1) * SB_BLK, :] = out
            y_ref[r * SB_BLK:(r + 1) * SB_BLK, :] = out.astype(BF16)

    pairs = SB_W // LANES
    block_spec = pl.BlockSpec((SB_QROWS, LANES), lambda p, i: (i, p))
    return pl.pallas_call(
        body, name=name, grid=(pairs, s // SB_QROWS),
        in_specs=[block_spec, pl.BlockSpec((s, LANES), lambda p, i: (0, p)),
                  pl.BlockSpec((s, LANES), lambda p, i: (0, pairs + p))],
        out_specs=[block_spec, block_spec],
        out_shape=[jax.ShapeDtypeStruct((s, SB_W), F32), jax.ShapeDtypeStruct((s, SB_W + MEM_W), BF16)],
        compiler_params=_params("arbitrary", "arbitrary"),
    )(proj, kv, kv)


def sb_bwd(proj, kv, out, dy, name):
    s = proj.shape[0]

    def body(q_ref, k_ref, v_ref, o_ref, do_ref, dq_ref, dk_ref, dv_ref):
        i = pl.program_id(1)

        @pl.when(i == 0)
        def _():
            dk_ref[...] = jnp.zeros_like(dk_ref)
            dv_ref[...] = jnp.zeros_like(dv_ref)

        after2, from2, col_minus_row, heads = _sb_consts()
        q_all = jnp.concatenate(_sb_queries(q_ref, heads), axis=0)
        key_before_query = col_minus_row < 0
        d_out = do_ref[...].astype(F32)
        prod = d_out * o_ref[...]
        dos, totals = [], []
        for r in range(SB_QB):
            rr = slice(r * SB_BLK, (r + 1) * SB_BLK)
            for hm in heads:
                dos.append(jnp.where(hm, d_out[rr], 0.0).astype(BF16))
                totals.append(jnp.broadcast_to(jnp.sum(jnp.where(hm, prod[rr], 0.0), axis=1, keepdims=True),
                                               (SB_BLK, SB_BLK)))
        do_all = jnp.concatenate(dos, axis=0)

        def block(j, state, first):
            runs, seens, dq = list(state[0]), list(state[1]), state[2]
            rows = pl.ds(pl.multiple_of(j * SB_KEYS, SB_KEYS), SB_KEYS)
            kb, vb = k_ref[rows, :], v_ref[rows, :]
            z = _dot(q_all, kb, NT)
            da = _dot(do_all, vb, NT)
            pend, pend2 = {}, {}
            a_parts = [[jnp.zeros((SB_BLK, SB_BLK), BF16)] * SB_SUB for _ in range(SB_CHAINS)]
            dz_parts = [[jnp.zeros((SB_BLK, SB_BLK), BF16)] * SB_SUB for _ in range(SB_CHAINS)]
            for c, n, where in _sb_tiles(first):
                mask = key_before_query if where == "diagonal" else None
                softplus, lb = _sb_logs(z[n * SB_BLK:(n + 1) * SB_BLK, c * SB_BLK:(c + 1) * SB_BLK], mask)
                pend[c, n] = (softplus, lb, _suffix(softplus, after2), mask)
            for c, n, _ in _sb_tiles(first):
                softplus, lb, r, mask = pend.pop((c, n))
                a = jnp.exp(lb + r[:, :SB_BLK] + runs[n])
                if mask is not None:
                    a = jnp.where(mask, a, 0.0)
                runs[n] = runs[n] + r[:, SB_BLK:]
                ab = a.astype(BF16)
                a_parts[n][c] = ab
                dl = ab.astype(F32) * da[n * SB_BLK:(n + 1) * SB_BLK, c * SB_BLK:(c + 1) * SB_BLK]
                pend2[c, n] = (softplus, lb, dl, _suffix(dl, from2), mask)
            for c, n, _ in _sb_tiles(first):
                softplus, lb, dl, r2, mask = pend2.pop((c, n))
                d_lom = totals[n] - (r2[:, :SB_BLK] + seens[n])
                if mask is not None:
                    d_lom = jnp.where(mask, d_lom, 0.0)
                seens[n] = seens[n] + r2[:, SB_BLK:]
                dz_parts[n][c] = (dl * jnp.exp(-softplus) - d_lom * jnp.exp(lb)).astype(BF16)
            a_all = jnp.concatenate([jnp.concatenate(p, axis=1) for p in a_parts], axis=0)
            dz_all = jnp.concatenate([jnp.concatenate(p, axis=1) for p in dz_parts], axis=0)
            dv_ref[rows, :] += _dot(a_all, do_all, TN)
            dk_ref[rows, :] += _dot(dz_all, q_all, TN)
            return tuple(runs), tuple(seens), dq + _dot(dz_all, kb)

        zero = jnp.zeros((SB_BLK, LANES), F32)
        state = _sb_walk(i, block, ((zero,) * SB_CHAINS, (zero,) * SB_CHAINS,
                                    jnp.zeros((SB_CHAINS * SB_BLK, LANES), F32)))
        for r in range(SB_QB):
            dq_ref[r * SB_BLK:(r + 1) * SB_BLK, :] = (_sb_heads_apart(state[2], heads, r) * QK_SCALE).astype(BF16)

    pairs = SB_W // LANES
    blk = lambda p, i: (i, p)
    col = lambda p, i: (0, p)
    return pl.pallas_call(
        body, name=name, grid=(pairs, s // SB_QROWS),
        in_specs=[pl.BlockSpec((SB_QROWS, LANES), blk), pl.BlockSpec((s, LANES), col),
                  pl.BlockSpec((s, LANES), lambda p, i: (0, pairs + p)), pl.BlockSpec((SB_QROWS, LANES), blk),
                  pl.BlockSpec((SB_QROWS, LANES), blk)],
        out_specs=[pl.BlockSpec((SB_QROWS, LANES), blk), pl.BlockSpec((s, LANES), col), pl.BlockSpec((s, LANES), col)],
        out_shape=[jax.ShapeDtypeStruct((s, SB_W + MEM_W), BF16), jax.ShapeDtypeStruct((s, SB_W), F32),
                   jax.ShapeDtypeStruct((s, SB_W), F32)],
        compiler_params=_params("arbitrary", "arbitrary"),
    )(proj, kv, kv, out, dy)


def final_loss(x, g, target, name):
    s, d = x.shape
    tm = _tile(s, 256, 8)

    def body(x_ref, g_ref, t_ref, loss_ref, dx_ref, dg_ref):
        @pl.when(pl.program_id(0) == 0)
        def _():
            loss_ref[...] = jnp.zeros_like(loss_ref)
            dg_ref[...] = jnp.zeros_like(dg_ref)

        xf = x_ref[...]
        r = _rms(xf)
        xhat = xf * r
        gain = g_ref[...]
        diff = xhat * gain - t_ref[...]
        sq = jnp.sum(jnp.sum(diff * diff, axis=1, keepdims=True), axis=0, keepdims=True)
        loss_ref[...] += jnp.broadcast_to(sq, loss_ref.shape)
        dy = diff * (1.0 / d)
        dg_ref[...] += jnp.sum(dy * xhat, axis=0, keepdims=True)
        dxhat = dy * gain
        dx_ref[...] = r * (dxhat - xhat * jnp.mean(dxhat * xhat, axis=-1, keepdims=True))

    row = lambda i: (i, 0)
    const = lambda i: (0, 0)
    return pl.pallas_call(
        body, name=name, grid=(s // tm,),
        in_specs=[pl.BlockSpec((tm, d), row), pl.BlockSpec((1, d), const), pl.BlockSpec((tm, d), row)],
        out_specs=[pl.BlockSpec((8, LANES), const), pl.BlockSpec((tm, d), row), pl.BlockSpec((1, d), const)],
        out_shape=[jax.ShapeDtypeStruct((8, LANES), F32), jax.ShapeDtypeStruct((s, d), F32), jax.ShapeDtypeStruct((1, d), F32)],
        compiler_params=_params("arbitrary"),
    )(x, g, target)


def adamw(w, parts, m, v, name):
    rows, cols = w.shape
    k = parts.shape[0]
    tr = _tile(rows, 512, 16)
    c1, c2 = 1.0 - ADAM_B1 ** ADAM_STEP, 1.0 - ADAM_B2 ** ADAM_STEP

    def body(w_ref, p_ref, m_ref, v_ref, g_ref, d_ref, nm_ref, nv_ref):
        grad = p_ref[0].astype(F32)
        for s in range(1, k):
            grad = grad + p_ref[s].astype(F32)
        nm = ADAM_B1 * m_ref[...] + (1.0 - ADAM_B1) * grad
        nv = ADAM_B2 * v_ref[...] + (1.0 - ADAM_B2) * (grad * grad)
        g_ref[...] = grad
        d_ref[...] = -ADAM_LR * ((nm / c1) / (jnp.sqrt(nv / c2) + ADAM_EPS) + ADAM_WD * w_ref[...])
        nm_ref[...] = nm
        nv_ref[...] = nv

    spec = pl.BlockSpec((tr, cols), lambda i: (i, 0))
    shape = jax.ShapeDtypeStruct((rows, cols), F32)
    return pl.pallas_call(
        body, name=name, grid=(rows // tr,),
        in_specs=[spec, pl.BlockSpec((k, tr, cols), lambda i: (0, i, 0)), spec, spec],
        out_specs=[spec] * 4, out_shape=[shape] * 4,
        compiler_params=_params("arbitrary"),
    )(w, parts, m, v)


SHARDED = {"ffn1_w_gate": 2, "ffn1_w_up": 2, "ffn1_w_down": 1, "ffn2_w_gate": 2, "ffn2_w_up": 2, "ffn2_w_down": 1,
           "w_mem_kv": 1, "a_w_in": 2, "a_w_out": 1, "w_kv": 1, "b_w_in": 1, "b_w_out": 1}
SMALL = ["ffn1_norm", "mix_norm", "ffn2_norm", "mem_norm", "kv_norm", "final_norm", "a_v_norm", "a_w_spatial", "a_b_spatial"]
WEIGHTS = ["ffn1_norm", "ffn1_w_gate", "ffn1_w_up", "ffn1_w_down", "mix_norm", "ffn2_norm", "ffn2_w_gate", "ffn2_w_up",
           "ffn2_w_down", "mem_norm", "w_mem_kv", "a_w_in", "a_v_norm", "a_w_spatial", "a_b_spatial", "a_w_out", "kv_norm",
           "w_kv", "b_w_in", "b_w_out", "final_norm"]


def _all_sum(parts, name):
    flat = jnp.concatenate([p.reshape(-1) for p in parts])
    pad = (-flat.size) % (16 * LANES)
    buf = jnp.pad(flat, (0, pad)).reshape(-1, LANES)
    total = sum_leading(exchange([buf], "all", True, name)[0], F32, name + "_sum").reshape(-1)
    out, off = [], 0
    for p in parts:
        out.append(total[off:off + p.size].reshape(p.shape))
        off += p.size
    return out


def _device_index():
    return 4 * lax.axis_index("x") + 2 * lax.axis_index("y") + lax.axis_index("c")


def kernel(x, mem, ffn1_norm, ffn1_w_gate, ffn1_w_up, ffn1_w_down, mix_norm, ffn2_norm, ffn2_w_gate, ffn2_w_up, ffn2_w_down, mem_norm, w_mem_kv, a_w_in, a_v_norm, a_w_spatial, a_b_spatial, a_w_out, kv_norm, w_kv, b_w_in, b_w_out, final_norm, loss_target, m_ffn1_norm, m_ffn1_w_gate, m_ffn1_w_up, m_ffn1_w_down, m_mix_norm, m_ffn2_norm, m_ffn2_w_gate, m_ffn2_w_up, m_ffn2_w_down, m_mem_norm, m_w_mem_kv, m_a_w_in, m_a_v_norm, m_a_w_spatial, m_a_b_spatial, m_a_w_out, m_kv_norm, m_w_kv, m_b_w_in, m_b_w_out, m_final_norm, v_ffn1_norm, v_ffn1_w_gate, v_ffn1_w_up, v_ffn1_w_down, v_mix_norm, v_ffn2_norm, v_ffn2_w_gate, v_ffn2_w_up, v_ffn2_w_down, v_mem_norm, v_w_mem_kv, v_a_w_in, v_a_v_norm, v_a_w_spatial, v_a_b_spatial, v_a_w_out, v_kv_norm, v_w_kv, v_b_w_in, v_b_w_out, v_final_norm):
    weights = dict(ffn1_norm=ffn1_norm, ffn1_w_gate=ffn1_w_gate, ffn1_w_up=ffn1_w_up, ffn1_w_down=ffn1_w_down, mix_norm=mix_norm, ffn2_norm=ffn2_norm, ffn2_w_gate=ffn2_w_gate, ffn2_w_up=ffn2_w_up, ffn2_w_down=ffn2_w_down, mem_norm=mem_norm, w_mem_kv=w_mem_kv, a_w_in=a_w_in, a_v_norm=a_v_norm, a_w_spatial=a_w_spatial, a_b_spatial=a_b_spatial, a_w_out=a_w_out, kv_norm=kv_norm, w_kv=w_kv, b_w_in=b_w_in, b_w_out=b_w_out, final_norm=final_norm)
    mom1 = dict(ffn1_norm=m_ffn1_norm, ffn1_w_gate=m_ffn1_w_gate, ffn1_w_up=m_ffn1_w_up, ffn1_w_down=m_ffn1_w_down, mix_norm=m_mix_norm, ffn2_norm=m_ffn2_norm, ffn2_w_gate=m_ffn2_w_gate, ffn2_w_up=m_ffn2_w_up, ffn2_w_down=m_ffn2_w_down, mem_norm=m_mem_norm, w_mem_kv=m_w_mem_kv, a_w_in=m_a_w_in, a_v_norm=m_a_v_norm, a_w_spatial=m_a_w_spatial, a_b_spatial=m_a_b_spatial, a_w_out=m_a_w_out, kv_norm=m_kv_norm, w_kv=m_w_kv, b_w_in=m_b_w_in, b_w_out=m_b_w_out, final_norm=m_final_norm)
    mom2 = dict(ffn1_norm=v_ffn1_norm, ffn1_w_gate=v_ffn1_w_gate, ffn1_w_up=v_ffn1_w_up, ffn1_w_down=v_ffn1_w_down, mix_norm=v_mix_norm, ffn2_norm=v_ffn2_norm, ffn2_w_gate=v_ffn2_w_gate, ffn2_w_up=v_ffn2_w_up, ffn2_w_down=v_ffn2_w_down, mem_norm=v_mem_norm, w_mem_kv=v_w_mem_kv, a_w_in=v_a_w_in, a_v_norm=v_a_v_norm, a_w_spatial=v_a_w_spatial, a_b_spatial=v_a_b_spatial, a_w_out=v_a_w_out, kv_norm=v_kv_norm, w_kv=v_w_kv, b_w_in=v_b_w_in, b_w_out=v_b_w_out, final_norm=v_final_norm)

    dev = _device_index()
    xs, mem_in, target = x[0], mem[0], loss_target[0]
    d_model = xs.shape[1]
    shards = {n: weights[n] for n in SHARDED}

    def mix_keys(l):
        w_in, w_out, idx = ("a_w_in", "a_w_out", l) if l < N_A else ("b_w_in", "b_w_out", l - N_A)
        return (w_in, idx), (w_out, idx)

    def ffn_keys(ffn):
        return ([], []) if ffn is None else ([(ffn[0] + "_w_gate", ffn[1]), (ffn[0] + "_w_up", ffn[1])],
                                             [(ffn[0] + "_w_down", ffn[1])])

    def ffn_after(f, l):
        return ("ffn2", l) if f == "ffn1" else (("ffn1", l + 1) if l + 1 < DEPTH else None)

    def cut_axis(key):
        return SHARDED[key[0]] - (0 if key[1] is None else 1)

    def block(key):
        return (shards[key[0]] if key[1] is None else shards[key[0]][key[1]]).astype(BF16)

    def carrying(call, keys, same_src, source, store):
        if not keys:
            return call(None)
        result, arrived = call(Side([source(k) for k in keys], same_src))
        store.update(zip(keys, arrived))
        return result

    first_gu, first_down = ffn_keys(("ffn1", 0))
    first = first_gu + first_down + [mix_keys(0)[0], ("w_mem_kv", None)]
    landed = dict(zip(first, exchange([block(k) for k in first], "all", True, "gather_first")))
    assembled = {}

    def whole(n, l=None):
        if (n, l) not in assembled:
            got = landed[n, l]
            if cut_axis((n, l)) == 0:
                assembled[n, l] = got.reshape((-1,) + got.shape[2:])
            else:
                pieces = [got[d] for d in range(N_DEV)]
                if n.endswith("_w_gate"):
                    pieces += [landed[n.replace("_w_gate", "_w_up"), l][d] for d in range(N_DEV)]
                assembled[n, l] = jnp.concatenate(pieces, axis=cut_axis((n, l)))
        return assembled[n, l]

    def whole_gu(f, l):
        return whole(f + "_w_gate", l)

    vn_width = a_v_norm.shape[1]
    a_v_full = _all_sum([lax.dynamic_update_slice(jnp.zeros((N_A, N_DEV * vn_width), F32), a_v_norm, (0, dev * vn_width))],
                        "gather_v_norm")[0]
    row = lambda v: v.reshape(1, -1)
    w_mem_cat = whole("w_mem_kv").transpose(1, 0, 2).reshape(d_model, -1)
    bias = [jnp.repeat(a_b_spatial[i].T, GM_P, axis=1) for i in range(N_A)]

    mem_kv, mem_h = norm_mm(mem_in, row(mem_norm), w_mem_cat, BF16, "mem_kv", emit_h=True)

    def ffn_fwd(xin, f, l):
        keys_gu, keys_down = ffn_keys(ffn_after(f, l))
        if ffn_after(f, l) == ("ffn1", N_A):
            keys_down = keys_down + [("w_kv", None)]
        gu = carrying(lambda side: norm_mm(xin, row(weights[f + "_norm"][l]), whole_gu(f, l), BF16, "ffn_gu", side=side),
                      keys_gu, True, block, landed)
        out = carrying(lambda side: swiglu_mm_res(gu, whole(f + "_w_down", l), xin, 0.5, "ffn_down", side=side),
                       keys_down, True, block, landed)
        return out, gu

    saved = []
    kv = x_kv = None
    cur = xs
    for l in range(DEPTH):
        st = {"x0": cur}
        if l == N_A:
            x_kv = cur
            kv = norm_mm(cur, row(kv_norm), whole("w_kv"), BF16, "kv_proj")
        st["x1"], st["gu1"] = ffn_fwd(cur, "ffn1", l)
        key_in, key_out = mix_keys(l)
        proj = carrying(lambda side: norm_mm(st["x1"], row(mix_norm[l]), whole(*key_in), F32 if l < N_A else BF16,
                                             "a_proj" if l < N_A else "b_proj", side=side), [key_out], True, block, landed)
        if l < N_A:
            y_tok = gmlp_fwd(proj, row(a_v_full[l]), a_w_spatial[l], bias[l], "gmlp_fwd")
            st["y"] = mem_fwd(proj, 2 * GM_W // MEM_W, mem_kv, l, y_tok, GM_W // MEM_W, "mem_fwd_a")
        else:
            st["sb_out"], y_tok = sb_fwd(proj, kv, "sb_fwd")
            st["y"] = mem_fwd(proj, SB_W // MEM_W, mem_kv, l, y_tok, SB_W // MEM_W, "mem_fwd_b")
        st["proj"] = proj
        st["x2"] = carrying(lambda side: mm_res(st["y"], whole(*key_out), st["x1"], 1.0, "mix_out", side=side),
                            [mix_keys(l + 1)[0]] if l + 1 < DEPTH else [], True, block, landed)
        cur, st["gu2"] = ffn_fwd(st["x2"], "ffn2", l)
        saved.append(st)

    loss_blk, dx, d_final = final_loss(cur, row(final_norm), target, "final_loss")
    loss = lax.psum(loss_blk[0, 0] * (0.5 / d_model), AXES)

    grads = {n: [None] * weights[n].shape[0] for n in WEIGHTS if weights[n].ndim >= 2 and n not in ("w_kv",)}
    grads["final_norm"] = d_final.reshape(-1)
    d_mem_kv = [None] * DEPTH
    d_kv = []

    summed = {}

    def pieces(key):
        g = (grads[key[0]] if key[1] is None else grads[key[0]][key[1]]).astype(BF16)
        axis = cut_axis(key)
        cut = g.reshape(g.shape[:axis] + (N_DEV, g.shape[axis] // N_DEV) + g.shape[axis + 1:])
        return jnp.moveaxis(cut, axis, 0)

    def ffn_bwd(dx, xin, gu, f, l):
        keys_gu, keys_down = ffn_keys(ffn_after(f, l))
        keys_mix = list(mix_keys(l)) if f == "ffn1" else ([("w_kv", None)] if l + 1 == N_A else [])
        d_gu = carrying(lambda side: mm_nt_swiglu_bwd(dx, whole(f + "_w_down", l), gu, 0.5, "ffn_dgu", side=side),
                        keys_down, False, pieces, summed)
        dx_new, d_gain, h = carrying(
            lambda side: mm_nt_normbwd(d_gu, whole_gu(f, l), xin, row(weights[f + "_norm"][l]), dx, "ffn_dx", side=side),
            keys_gu, False, pieces, summed)
        d_wgu = carrying(lambda side: mm_tn(h, d_gu, 1.0, "ffn_dwgu", tb_target=1408, side=side),
                         keys_mix, False, pieces, summed)
        d_wdown = swiglu_mm_tn(gu, dx, 0.5, "ffn_dwdown")
        half = d_wgu.shape[1] // 2
        grads[f + "_w_gate"][l], grads[f + "_w_up"][l] = d_wgu[:, :half], d_wgu[:, half:]
        grads[f + "_w_down"][l] = d_wdown
        grads[f + "_norm"][l] = d_gain.reshape(-1)
        return dx_new

    for l in reversed(range(DEPTH)):
        st = saved[l]
        dx = ffn_bwd(dx, st["x2"], st["gu2"], "ffn2", l)
        proj = st["proj"]
        (key_in, idx), (key_out, _) = mix_keys(l)
        w_in, w_out = whole(key_in, idx), whole(key_out, idx)
        dy = mm_nt(dx, w_out, 1.0, "mix_dy")
        grads[key_out][idx] = mm_tn(st["y"], dx, 1.0, "mix_dwout", tb_target=1024)
        if l < N_A:
            d_uv, d_ws, d_bs, d_vgain = gmlp_bwd(proj, dy, row(a_v_full[l]), a_w_spatial[l], bias[l], "gmlp_bwd")
            grads["a_w_spatial"][l], grads["a_b_spatial"][l], grads["a_v_norm"][l] = d_ws, d_bs[:, :, 0], d_vgain.reshape(-1)
            d_proj, d_k, d_v = mem_bwd(proj, 2 * GM_W // MEM_W, mem_kv, l, dy, GM_W // MEM_W, d_uv, "mem_bwd_a")
        else:
            d_qsb, d_ksb, d_vsb = sb_bwd(proj, kv, st["sb_out"], dy, "sb_bwd")
            d_kv.append(jnp.concatenate([d_ksb, d_vsb], axis=1))
            d_proj, d_k, d_v = mem_bwd(proj, SB_W // MEM_W, mem_kv, l, dy, SB_W // MEM_W, d_qsb, "mem_bwd_b")
        d_mem_kv[l] = jnp.concatenate([d_k, d_v], axis=1)
        dx, d_gain, h = mm_nt_normbwd(d_proj, w_in, st["x1"], row(mix_norm[l]), dx, "mix_dx")
        grads["mix_norm"][l] = d_gain.reshape(-1)
        grads[key_in][idx] = mm_tn(h, d_proj, 1.0, "mix_dwin")
        dx = ffn_bwd(dx, st["x0"], st["gu1"], "ffn1", l)
        if l == N_A:
            d_kv_b = sum_leading(jnp.stack(d_kv), BF16, "kv_dsum")
            dx, d_gain, h = mm_nt_normbwd(d_kv_b, whole("w_kv"), x_kv, row(kv_norm), dx, "kv_dx")
            grads["kv_norm"] = d_gain.reshape(-1)
            grads["w_kv"] = mm_tn(h, d_kv_b, 1.0, "kv_dw")

    d_mem_all = jnp.concatenate(d_mem_kv, axis=1).astype(BF16)
    _, d_gain, _ = mm_nt_normbwd(d_mem_all, w_mem_cat, mem_in, row(mem_norm), None, "mem_dnorm")
    grads["mem_norm"] = d_gain.reshape(-1)
    d_wmem = mm_tn(mem_h, d_mem_all, 1.0, "mem_dw")
    grads["w_mem_kv"] = d_wmem.reshape(d_model, DEPTH, -1).transpose(1, 0, 2)

    last = first_gu + first_down + [("w_mem_kv", None)]
    summed.update(zip(last, exchange([pieces(k) for k in last], "all", False, "scatter_last")))
    parts = {n: summed[n, None] if (n, None) in summed else
             jnp.stack([summed[n, i] for i in range(weights[n].shape[0])], axis=1) for n in SHARDED}
    grads = {n: (jnp.stack(g) if isinstance(g, list) else g) for n, g in grads.items()}
    for n, g in zip(SMALL, _all_sum([grads[n] for n in SMALL], "sum_small")):
        parts[n] = g[None]
    parts["a_v_norm"] = lax.dynamic_slice(parts["a_v_norm"], (0, 0, dev * vn_width), (1,) + a_v_norm.shape)

    reduced, deltas, new_m, new_v = {}, {}, {}, {}
    for n in WEIGHTS:
        w = weights[n]
        view = (lambda a: a.reshape(-1, a.shape[-1]))
        res = adamw(view(w), parts[n].reshape(parts[n].shape[0], -1, w.shape[-1]), view(mom1[n]), view(mom2[n]), "adamw")
        reduced[n], deltas[n], new_m[n], new_v[n] = [r.reshape(w.shape) for r in res]

    return (loss, dx[None], *[reduced[n] for n in WEIGHTS], *[deltas[n] for n in WEIGHTS],
            *[new_m[n] for n in WEIGHTS], *[new_v[n] for n in WEIGHTS])
```

```python
import jax
import jax.numpy as jnp
from jax import lax
from jax.experimental import pallas as pl
from jax.experimental.pallas import tpu as pltpu

F32, BF16 = jnp.float32, jnp.bfloat16
MESH_ID = pl.DeviceIdType.MESH
AXES = ("x", "y", "c")
N_DEV = 8

EPS = 1e-6
DEPTH, N_A = 4, 2
GM_W, GM_GROUPS, GM_P = 768, 6, 128
MEM_W, MEM_HEADS, HEAD_DIM = 256, 4, 64
SB_W, SB_BLK = 768, 128
LANES = 128
QK_SCALE = HEAD_DIM ** -0.5
GELU_C, GELU_A = 0.7978845608028654, 0.044715

ADAM_LR, ADAM_B1, ADAM_B2, ADAM_EPS, ADAM_WD, ADAM_STEP = 0.001, 0.9, 0.999, 1e-08, 0.01, 10

VMEM_LIMIT = 56 * 1024 * 1024

NT = (((1,), (1,)), ((), ()))
TN = (((0,), (0,)), ((), ()))


def _params(*sem):
    return pltpu.CompilerParams(dimension_semantics=sem, vmem_limit_bytes=VMEM_LIMIT)


def _tile(n, target, mult=LANES):
    best = None
    for t in range(mult, min(n, target) + 1, mult):
        if n % t == 0:
            best = t
    return best if best is not None else n


def _dot(a, b, dims=None):
    if dims is None:
        return jnp.dot(a, b, preferred_element_type=F32)
    return lax.dot_general(a, b, dims, preferred_element_type=F32)


def exchange(srcs, group, same_src, name, split=False):
    size = {"pair": 2, "quad": 4, "all": 8}[group]
    n = len(srcs)
    chunk_shapes = [tuple(s.shape) if same_src else tuple(s.shape[1:]) for s in srcs]
    pieces = [cs[0] if split else 1 for cs in chunk_shapes]
    n_dma = sum(pieces)

    def body(*refs):
        src_refs, out_refs = refs[:n], refs[n:2 * n]
        send_sems, recv_sems, local_sems = refs[2 * n:]
        x, y, c = lax.axis_index("x"), lax.axis_index("y"), lax.axis_index("c")
        if group == "pair":
            me, dev = c, lambda p: (x, y, p)
        elif group == "quad":
            me, dev = 2 * x + y, lambda p: (p // 2, p % 2, c)
        else:
            me, dev = 4 * x + 2 * y + c, lambda p: (p // 4, (p // 2) % 2, p % 2)

        def chunk(t, idx):
            return src_refs[t] if same_src else src_refs[t].at[idx]

        def copies(k, idx, slot, peer):
            out, w = [], k * n_dma
            for t in range(n):
                src, dst = chunk(t, idx), out_refs[t].at[slot]
                for s_ref, d_ref in ([(src.at[u], dst.at[u]) for u in range(pieces[t])] if split else [(src, dst)]):
                    out.append(pltpu.make_async_remote_copy(
                        src_ref=s_ref, dst_ref=d_ref, send_sem=send_sems.at[w], recv_sem=recv_sems.at[w],
                        device_id=dev(peer), device_id_type=MESH_ID))
                    w += 1
            return out

        local = [pltpu.make_async_copy(chunk(t, me), out_refs[t].at[me], local_sems.at[t]) for t in range(n)]
        for cp in local:
            cp.start()
        sends = []
        for k in range(1, size):
            peer = (me + k) % size
            sends += copies(k, peer, me, peer)
        for cp in sends:
            cp.start()
        for k in range(1, size):
            sender = (me + size - k) % size
            for cp in copies(k, me, sender, sender):
                cp.wait_recv()
        for cp in sends:
            cp.wait_send()
        for cp in local:
            cp.wait()

    hbm = pl.BlockSpec(memory_space=pltpu.HBM)
    return pl.pallas_call(
        body, name=name,
        out_shape=[jax.ShapeDtypeStruct((size,) + cs, s.dtype) for cs, s in zip(chunk_shapes, srcs)],
        in_specs=[hbm] * n, out_specs=[hbm] * n,
        scratch_shapes=[pltpu.SemaphoreType.DMA((size * n_dma,)), pltpu.SemaphoreType.DMA((size * n_dma,)),
                        pltpu.SemaphoreType.DMA((n,))],
    )(*srcs)


class Side:
    def __init__(self, srcs, same_src):
        self.srcs, self.same_src, self.n = list(srcs), same_src, len(srcs)
        self.chunk_shapes = [tuple(s.shape) if same_src else tuple(s.shape[1:]) for s in srcs]

    def out_shapes(self):
        return [jax.ShapeDtypeStruct((N_DEV,) + cs, s.dtype) for cs, s in zip(self.chunk_shapes, self.srcs)]

    def scratch(self):
        return [pltpu.SemaphoreType.DMA((N_DEV * self.n,)), pltpu.SemaphoreType.DMA((N_DEV * self.n,))]

    def _copies(self, src_refs, land_refs, send_sems, recv_sems, outgoing):
        me = 4 * lax.axis_index("x") + 2 * lax.axis_index("y") + lax.axis_index("c")
        out = []
        for k in range(1, N_DEV):
            peer = (me + k) % N_DEV if outgoing else (me + N_DEV - k) % N_DEV
            for t in range(self.n):
                src = src_refs[t] if self.same_src else src_refs[t].at[peer if outgoing else me]
                out.append(pltpu.make_async_remote_copy(
                    src_ref=src, dst_ref=land_refs[t].at[me if outgoing else peer],
                    send_sem=send_sems.at[k * self.n + t], recv_sem=recv_sems.at[k * self.n + t],
                    device_id=(peer // 4, (peer // 2) % 2, peer % 2), device_id_type=MESH_ID))
        return out

    def _own(self, src_refs, land_refs, send_sems):
        me = 4 * lax.axis_index("x") + 2 * lax.axis_index("y") + lax.axis_index("c")
        return [pltpu.make_async_copy(src_refs[t] if self.same_src else src_refs[t].at[me], land_refs[t].at[me],
                                      send_sems.at[t]) for t in range(self.n)]

    def start(self, src_refs, land_refs, send_sems, recv_sems):
        for cp in self._own(src_refs, land_refs, send_sems) + self._copies(src_refs, land_refs, send_sems, recv_sems, True):
            cp.start()

    def wait(self, src_refs, land_refs, send_sems, recv_sems):
        for cp in self._copies(src_refs, land_refs, send_sems, recv_sems, False):
            cp.wait_recv()
        for cp in self._copies(src_refs, land_refs, send_sems, recv_sems, True):
            cp.wait_send()
        for cp in self._own(src_refs, land_refs, send_sems):
            cp.wait()


def _call(body, side, name, grid, in_specs, out_specs, out_shape, scratch_shapes, dims, args):
    if side is None:
        res = pl.pallas_call(body, name=name, grid=grid, in_specs=in_specs, out_specs=out_specs, out_shape=out_shape,
                             scratch_shapes=scratch_shapes, compiler_params=_params(*dims))(*args)
        return list(res), []
    n_in, n_out, n_scr, ns = len(in_specs), len(out_specs), len(scratch_shapes), side.n

    def wrapped(*refs):
        ins, srcs = refs[:n_in], refs[n_in:n_in + ns]
        outs, lands = refs[n_in + ns:n_in + ns + n_out], refs[n_in + ns + n_out:n_in + 2 * ns + n_out]
        scratch, (send_sems, recv_sems) = refs[n_in + 2 * ns + n_out:n_in + 2 * ns + n_out + n_scr], refs[-2:]
        first, last = None, None
        for axis, steps in enumerate(grid):
            i = pl.program_id(axis)
            first = (i == 0) if first is None else first & (i == 0)
            last = (i == steps - 1) if last is None else last & (i == steps - 1)

        @pl.when(first)
        def _():
            side.start(srcs, lands, send_sems, recv_sems)

        body(*ins, *outs, *scratch)

        @pl.when(last)
        def _():
            side.wait(srcs, lands, send_sems, recv_sems)

    hbm = pl.BlockSpec(memory_space=pltpu.HBM)
    res = pl.pallas_call(
        wrapped, name=name, grid=grid, in_specs=list(in_specs) + [hbm] * ns, out_specs=list(out_specs) + [hbm] * ns,
        out_shape=list(out_shape) + side.out_shapes(), scratch_shapes=list(scratch_shapes) + side.scratch(),
        compiler_params=_params(*dims))(*args, *side.srcs)
    return list(res[:n_out]), list(res[n_out:])


def sum_leading(parts, out_dtype, name):
    k, rows, cols = parts.shape
    tr = _tile(rows, 512, 16)

    def body(p_ref, o_ref):
        acc = p_ref[0].astype(F32)
        for s in range(1, k):
            acc = acc + p_ref[s].astype(F32)
        o_ref[...] = acc.astype(o_ref.dtype)

    return pl.pallas_call(
        body, name=name, grid=(rows // tr,),
        in_specs=[pl.BlockSpec((k, tr, cols), lambda i: (0, i, 0))],
        out_specs=pl.BlockSpec((tr, cols), lambda i: (i, 0)),
        out_shape=jax.ShapeDtypeStruct((rows, cols), out_dtype),
        compiler_params=_params("arbitrary"),
    )(parts)


def _rms(xf):
    return lax.rsqrt(jnp.mean(xf * xf, axis=-1, keepdims=True) + EPS)


def norm_mm(x, g, w, out_dtype, name, emit_h=False, side=None):
    m, d = x.shape
    n = w.shape[1]
    tm, tn = _tile(m, 1024, 8), _tile(n, 1408)

    def body(x_ref, g_ref, w_ref, o_ref, *rest):
        h_ref = rest[-1]

        @pl.when(pl.program_id(1) == 0)
        def _():
            xf = x_ref[...]
            hb = ((xf * _rms(xf)) * g_ref[...]).astype(BF16)
            h_ref[...] = hb
            if emit_h:
                rest[0][...] = hb

        o_ref[...] = _dot(h_ref[...], w_ref[...]).astype(o_ref.dtype)

    out_shape = [jax.ShapeDtypeStruct((m, n), out_dtype)]
    out_specs = [pl.BlockSpec((tm, tn), lambda i, j: (i, j))]
    if emit_h:
        out_shape.append(jax.ShapeDtypeStruct((m, d), BF16))
        out_specs.append(pl.BlockSpec((tm, d), lambda i, j: (i, 0)))
    res, landed = _call(
        body, side, name, (m // tm, n // tn),
        [pl.BlockSpec((tm, d), lambda i, j: (i, 0)), pl.BlockSpec((1, d), lambda i, j: (0, 0)),
         pl.BlockSpec((d, tn), lambda i, j: (0, j))],
        out_specs, out_shape, [pltpu.VMEM((tm, d), BF16)], ("arbitrary", "arbitrary"), (x, g, w))
    out = res if emit_h else res[0]
    return out if side is None else (out, landed)


def mm_res(a, w, res, alpha, name, side=None):
    m, k = a.shape
    n = w.shape[1]
    tm, tn = _tile(m, 1024, 8), _tile(n, 1024)

    def body(a_ref, w_ref, r_ref, o_ref):
        o_ref[...] = r_ref[...] + alpha * _dot(a_ref[...], w_ref[...])

    out, landed = _call(
        body, side, name, (m // tm, n // tn),
        [pl.BlockSpec((tm, k), lambda i, j: (i, 0)), pl.BlockSpec((k, tn), lambda i, j: (0, j)),
         pl.BlockSpec((tm, tn), lambda i, j: (i, j))],
        [pl.BlockSpec((tm, tn), lambda i, j: (i, j))], [jax.ShapeDtypeStruct((m, n), F32)], [],
        ("arbitrary", "arbitrary"), (a, w, res))
    return out[0] if side is None else (out[0], landed)


def mm_nt(x, w, alpha, name):
    m, d = x.shape
    n = w.shape[0]
    tm, tn = _tile(m, 1024, 8), _tile(n, 1408)

    def body(x_ref, w_ref, o_ref, xb_ref):
        @pl.when(pl.program_id(1) == 0)
        def _():
            xb_ref[...] = x_ref[...].astype(BF16)

        o_ref[...] = (alpha * _dot(xb_ref[...], w_ref[...], NT)).astype(o_ref.dtype)

    return pl.pallas_call(
        body, name=name, grid=(m // tm, n // tn),
        in_specs=[pl.BlockSpec((tm, d), lambda i, j: (i, 0)), pl.BlockSpec((tn, d), lambda i, j: (j, 0))],
        out_specs=pl.BlockSpec((tm, tn), lambda i, j: (i, j)),
        out_shape=jax.ShapeDtypeStruct((m, n), BF16),
        scratch_shapes=[pltpu.VMEM((tm, d), BF16)],
        compiler_params=_params("arbitrary", "arbitrary"),
    )(x, w)


def mm_tn(a, b, alpha, name, ta_target=1024, tb_target=512, side=None):
    s, ka = a.shape
    nb = b.shape[1]
    ta, tb, ts = _tile(ka, ta_target), _tile(nb, tb_target), _tile(s, 1024, 16)
    steps = s // ts

    def body(a_ref, b_ref, o_ref, acc_ref):
        t = pl.program_id(2)

        @pl.when(t == 0)
        def _():
            acc_ref[...] = jnp.zeros_like(acc_ref)

        acc_ref[...] += _dot(a_ref[...].astype(BF16), b_ref[...].astype(BF16), TN)

        @pl.when(t == steps - 1)
        def _():
            o_ref[...] = (alpha * acc_ref[...]).astype(o_ref.dtype)

    res, landed = _call(
        body, side, name, (ka // ta, nb // tb, steps),
        [pl.BlockSpec((ts, ta), lambda i, j, t: (t, i)), pl.BlockSpec((ts, tb), lambda i, j, t: (t, j))],
        [pl.BlockSpec((ta, tb), lambda i, j, t: (i, j))], [jax.ShapeDtypeStruct((ka, nb), BF16)],
        [pltpu.VMEM((ta, tb), F32)], ("arbitrary", "arbitrary", "arbitrary"), (a, b))
    return res[0] if side is None else (res[0], landed)


def mm_nt_normbwd(dy, w, x, g, res, name, side=None):
    m, n = dy.shape
    d = w.shape[0]
    tm, tk = _tile(m, 1024, 8), _tile(n, 1408)
    steps = n // tk
    has_res = res is not None

    def body(*refs):
        if has_res:
            dy_ref, w_ref, x_ref, g_ref, r_ref, dx_ref, dg_ref, h_ref, acc_ref = refs
        else:
            dy_ref, w_ref, x_ref, g_ref, dx_ref, dg_ref, h_ref, acc_ref = refs
        i, t = pl.program_id(0), pl.program_id(1)

        @pl.when(t == 0)
        def _():
            acc_ref[...] = jnp.zeros_like(acc_ref)

        @pl.when((t == 0) & (i == 0))
        def _():
            dg_ref[...] = jnp.zeros_like(dg_ref)

        acc_ref[...] += _dot(dy_ref[...], w_ref[...], NT)

        @pl.when(t == steps - 1)
        def _():
            xf = x_ref[...]
            r = _rms(xf)
            xhat = xf * r
            dh = acc_ref[...]
            gain = g_ref[...]
            dg_ref[...] += jnp.sum(dh * xhat, axis=0, keepdims=True)
            dxhat = dh * gain
            dx = r * (dxhat - xhat * jnp.mean(dxhat * xhat, axis=-1, keepdims=True))
            dx_ref[...] = (r_ref[...] + dx) if has_res else dx
            h_ref[...] = (xhat * gain).astype(BF16)

    row = lambda i, t: (i, 0)
    in_specs = [pl.BlockSpec((tm, tk), lambda i, t: (i, t)), pl.BlockSpec((d, tk), lambda i, t: (0, t)),
                pl.BlockSpec((tm, d), row), pl.BlockSpec((1, d), lambda i, t: (0, 0))]
    args = [dy, w, x, g]
    if has_res:
        in_specs.append(pl.BlockSpec((tm, d), row))
        args.append(res)
    res, landed = _call(
        body, side, name, (m // tm, steps), in_specs,
        [pl.BlockSpec((tm, d), row), pl.BlockSpec((1, d), lambda i, t: (0, 0)), pl.BlockSpec((tm, d), row)],
        [jax.ShapeDtypeStruct((m, d), F32), jax.ShapeDtypeStruct((1, d), F32), jax.ShapeDtypeStruct((m, d), BF16)],
        [pltpu.VMEM((tm, d), F32)], ("arbitrary", "arbitrary"), args)
    return res if side is None else (res, landed)


def _sigmoid(z):
    return 1.0 / (1.0 + jnp.exp(-z))


def _swiglu(gate_b, up_b):
    gate = gate_b.astype(F32)
    return (gate * _sigmoid(gate) * up_b.astype(F32)).astype(BF16)


def swiglu_mm_res(gu, w, res, alpha, name, side=None):
    m, f2 = gu.shape
    f, n = w.shape
    tm, tc = _tile(m, 256, 16), _tile(f, 256)

    def body(gu_ref, w_ref, r_ref, o_ref):
        acc = jnp.zeros((tm, n), F32)
        for c0 in range(0, f, tc):
            act = _swiglu(gu_ref[:, c0:c0 + tc], gu_ref[:, f + c0:f + c0 + tc])
            acc = acc + _dot(act, w_ref[c0:c0 + tc, :])
        o_ref[...] = r_ref[...] + alpha * acc

    out, landed = _call(
        body, side, name, (m // tm,),
        [pl.BlockSpec((tm, f2), lambda i: (i, 0)), pl.BlockSpec((f, n), lambda i: (0, 0)),
         pl.BlockSpec((tm, n), lambda i: (i, 0))],
        [pl.BlockSpec((tm, n), lambda i: (i, 0))], [jax.ShapeDtypeStruct((m, n), F32)], [], ("arbitrary",), (gu, w, res))
    return out[0] if side is None else (out[0], landed)


def swiglu_mm_tn(gu, b, alpha, name, side=None):
    s, f2 = gu.shape
    f, n = f2 // 2, b.shape[1]
    ta, ts = _tile(f, 1408), _tile(s, 512, 16)
    steps, half = s // ts, f // ta

    def body(g_ref, u_ref, b_ref, o_ref, acc_ref):
        t = pl.program_id(1)

        @pl.when(t == 0)
        def _():
            acc_ref[...] = jnp.zeros_like(acc_ref)

        bb = b_ref[...].astype(BF16)
        for c0 in range(0, ta, LANES):
            acc_ref[c0:c0 + LANES, :] += _dot(_swiglu(g_ref[:, c0:c0 + LANES], u_ref[:, c0:c0 + LANES]), bb, TN)

        @pl.when(t == steps - 1)
        def _():
            o_ref[...] = (alpha * acc_ref[...]).astype(o_ref.dtype)

    res, landed = _call(
        body, side, name, (half, steps),
        [pl.BlockSpec((ts, ta), lambda i, t: (t, i)), pl.BlockSpec((ts, ta), lambda i, t: (t, half + i)),
         pl.BlockSpec((ts, n), lambda i, t: (t, 0))],
        [pl.BlockSpec((ta, n), lambda i, t: (i, 0))], [jax.ShapeDtypeStruct((f, n), BF16)],
        [pltpu.VMEM((ta, n), F32)], ("arbitrary", "arbitrary"), (gu, gu, b))
    return res[0] if side is None else (res[0], landed)


def mm_nt_swiglu_bwd(x, w, gu, alpha, name, side=None):
    m, d = x.shape
    f = w.shape[0]
    tm, tc = _tile(m, 256, 16), _tile(f, 256)

    def body(x_ref, w_ref, gu_ref, o_ref):
        xb = x_ref[...].astype(BF16)
        for c0 in range(0, f, tc):
            d_act = alpha * _dot(xb, w_ref[c0:c0 + tc, :], NT)
            gate, up = gu_ref[:, c0:c0 + tc].astype(F32), gu_ref[:, f + c0:f + c0 + tc].astype(F32)
            sg = _sigmoid(gate)
            o_ref[:, c0:c0 + tc] = (d_act * up * (sg * (1.0 + gate * (1.0 - sg)))).astype(BF16)
            o_ref[:, f + c0:f + c0 + tc] = (d_act * (gate * sg)).astype(BF16)

    res, landed = _call(
        body, side, name, (m // tm,),
        [pl.BlockSpec((tm, d), lambda i: (i, 0)), pl.BlockSpec((f, d), lambda i: (0, 0)),
         pl.BlockSpec((tm, 2 * f), lambda i: (i, 0))],
        [pl.BlockSpec((tm, 2 * f), lambda i: (i, 0))], [jax.ShapeDtypeStruct((m, 2 * f), BF16)], [], ("arbitrary",),
        (x, w, gu))
    return res[0] if side is None else (res[0], landed)


def _gelu(x):
    return 0.5 * x * (1.0 + jnp.tanh(GELU_C * (x + GELU_A * x * x * x)))


def _gelu_grad(x):
    t = jnp.tanh(GELU_C * (x + GELU_A * x * x * x))
    return 0.5 * (1.0 + t) + 0.5 * x * (1.0 - t * t) * (GELU_C * (1.0 + 3.0 * GELU_A * x * x))


def _chunk_mask():
    row = lax.broadcasted_iota(jnp.int32, (GM_P, GM_P), 0)
    col = lax.broadcasted_iota(jnp.int32, (GM_P, GM_P), 1)
    return (col < GM_P // 2) | (row >= GM_P // 2)


def gmlp_fwd(proj, gain, w_s, bias, name):
    s, pw = proj.shape
    tm = _tile(s, 256, GM_P)

    def body(p_ref, gain_ref, w_ref, b_ref, o_ref):
        mask = _chunk_mask()
        u = _gelu(p_ref[:, :GM_W])
        v = _gelu(p_ref[:, GM_W:2 * GM_W])
        vn = ((v * _rms(v)) * gain_ref[...]).astype(BF16)
        for g in range(GM_GROUPS):
            wg = jnp.where(mask, w_ref[g], 0.0).astype(BF16)
            cols = slice(g * GM_P, (g + 1) * GM_P)
            for n in range(tm // GM_P):
                rows = slice(n * GM_P, (n + 1) * GM_P)
                mixed = _dot(wg, vn[rows, cols]) + b_ref[:, cols]
                o_ref[rows, cols] = (u[rows, cols] * mixed).astype(BF16)

    return pl.pallas_call(
        body, name=name, grid=(s // tm,),
        in_specs=[pl.BlockSpec((tm, pw), lambda i: (i, 0)), pl.BlockSpec((1, GM_W), lambda i: (0, 0)),
                  pl.BlockSpec((GM_GROUPS, GM_P, GM_P), lambda i: (0, 0, 0)), pl.BlockSpec((GM_P, GM_W), lambda i: (0, 0))],
        out_specs=pl.BlockSpec((tm, GM_W), lambda i: (i, 0)),
        out_shape=jax.ShapeDtypeStruct((s, GM_W + MEM_W), BF16), compiler_params=_params("arbitrary"),
    )(proj, gain, w_s, bias)


def gmlp_bwd(proj, dy, gain, w_s, bias, name):
    s, pw = proj.shape
    dw_total = dy.shape[1]
    tm = _tile(s, 256, GM_P)

    def body(p_ref, dy_ref, gain_ref, w_ref, b_ref, dp_ref, dw_ref, db_ref, dgain_ref, dvn_ref):
        @pl.when(pl.program_id(0) == 0)
        def _():
            dw_ref[...] = jnp.zeros_like(dw_ref)
            db_ref[...] = jnp.zeros_like(db_ref)
            dgain_ref[...] = jnp.zeros_like(dgain_ref)

        mask = _chunk_mask()
        pu = p_ref[:, :GM_W]
        pv = p_ref[:, GM_W:2 * GM_W]
        u = _gelu(pu)
        v = _gelu(pv)
        r = _rms(v)
        vhat = v * r
        gain = gain_ref[...]
        vn = (vhat * gain).astype(BF16)
        gu_grad = _gelu_grad(pu)
        for g in range(GM_GROUPS):
            wg = jnp.where(mask, w_ref[g], 0.0).astype(BF16)
            cols = slice(g * GM_P, (g + 1) * GM_P)
            dw_acc = jnp.zeros((GM_P, GM_P), F32)
            db_acc = jnp.zeros((GM_P, 1), F32)
            for n in range(tm // GM_P):
                rows = slice(n * GM_P, (n + 1) * GM_P)
                dyb = dy_ref[rows, cols].astype(F32)
                vnb = vn[rows, cols]
                mixed = _dot(wg, vnb) + b_ref[:, cols]
                dmixed = dyb * u[rows, cols]
                dmb = dmixed.astype(BF16)
                dp_ref[rows, cols] = (dyb * mixed * gu_grad[rows, cols]).astype(BF16)
                dw_acc = dw_acc + _dot(dmb, vnb, NT)
                db_acc = db_acc + jnp.sum(dmixed, axis=1, keepdims=True)
                dvn_ref[rows, cols] = _dot(wg, dmb, TN)
            dw_ref[g] += jnp.where(mask, dw_acc, 0.0)
            db_ref[g] += jnp.broadcast_to(db_acc, (GM_P, GM_P))
        dvn = dvn_ref[...]
        dgain_ref[...] += jnp.sum(dvn * vhat, axis=0, keepdims=True)
        dvhat = dvn * gain
        dv = r * (dvhat - vhat * jnp.mean(dvhat * vhat, axis=-1, keepdims=True))
        dp_ref[:, GM_W:] = (dv * _gelu_grad(pv)).astype(BF16)

    const3 = lambda i: (0, 0, 0)
    return pl.pallas_call(
        body, name=name, grid=(s // tm,),
        in_specs=[pl.BlockSpec((tm, pw), lambda i: (i, 0)), pl.BlockSpec((tm, dw_total), lambda i: (i, 0)),
                  pl.BlockSpec((1, GM_W), lambda i: (0, 0)), pl.BlockSpec((GM_GROUPS, GM_P, GM_P), const3),
                  pl.BlockSpec((GM_P, GM_W), lambda i: (0, 0))],
        out_specs=[pl.BlockSpec((tm, 2 * GM_W), lambda i: (i, 0)), pl.BlockSpec((GM_GROUPS, GM_P, GM_P), const3),
                   pl.BlockSpec((GM_GROUPS, GM_P, GM_P), const3), pl.BlockSpec((1, GM_W), lambda i: (0, 0))],
        out_shape=[jax.ShapeDtypeStruct((s, pw), BF16), jax.ShapeDtypeStruct((GM_GROUPS, GM_P, GM_P), F32),
                   jax.ShapeDtypeStruct((GM_GROUPS, GM_P, GM_P), F32), jax.ShapeDtypeStruct((1, GM_W), F32)],
        scratch_shapes=[pltpu.VMEM((tm, GM_W), F32)],
        compiler_params=_params("arbitrary"),
    )(proj, dy, gain, w_s, bias)


def _keep(mask, xb):
    return jnp.where(mask, xb.astype(F32), 0.0).astype(BF16)


def _head_masks(rows, width, heads):
    lane = lax.broadcasted_iota(jnp.int32, (rows, width), 1)
    return [(lane >= HEAD_DIM * h) & (lane < HEAD_DIM * (h + 1)) for h in range(heads)]


def _mem_probs(qh, k):
    sc = _dot(qh, k, NT) * QK_SCALE
    e = jnp.exp(sc - jnp.max(sc, axis=-1, keepdims=True))
    return e / jnp.sum(e, axis=-1, keepdims=True)


def mem_fwd(proj, q_blk, mem_kv, layer, into, into_blk, name):
    s = proj.shape[0]
    n_mem = mem_kv.shape[0]
    tm = _tile(s, 512, 16)

    def body(q_ref, k_ref, v_ref, into_ref, o_ref):
        q = q_ref[...].astype(BF16)
        k, v = k_ref[...], v_ref[...]
        out = jnp.zeros((tm, MEM_W), F32)
        for hm in _head_masks(tm, MEM_W, MEM_HEADS):
            p = _mem_probs(_keep(hm, q), k)
            out = out + jnp.where(hm, _dot(p.astype(BF16), v), 0.0)
        o_ref[...] = out.astype(BF16)

    return pl.pallas_call(
        body, name=name, grid=(s // tm,),
        in_specs=[pl.BlockSpec((tm, MEM_W), lambda i: (i, q_blk)), pl.BlockSpec((n_mem, MEM_W), lambda i: (0, 2 * layer)),
                  pl.BlockSpec((n_mem, MEM_W), lambda i: (0, 2 * layer + 1)), pl.BlockSpec(memory_space=pl.ANY)],
        out_specs=pl.BlockSpec((tm, MEM_W), lambda i: (i, into_blk)),
        out_shape=jax.ShapeDtypeStruct(into.shape, BF16), input_output_aliases={3: 0},
        compiler_params=_params("arbitrary"),
    )(proj, mem_kv, mem_kv, into)


def mem_bwd(proj, q_blk, mem_kv, layer, dy, dy_blk, into, name):
    s = proj.shape[0]
    n_mem = mem_kv.shape[0]
    tm = _tile(s, 512, 16)

    def body(q_ref, k_ref, v_ref, dy_ref, into_ref, dq_ref, dk_ref, dv_ref):
        @pl.when(pl.program_id(0) == 0)
        def _():
            dk_ref[...] = jnp.zeros_like(dk_ref)
            dv_ref[...] = jnp.zeros_like(dv_ref)

        q = q_ref[...].astype(BF16)
        k, v = k_ref[...], v_ref[...]
        dy = dy_ref[...]
        dq = jnp.zeros((tm, MEM_W), F32)
        dk = jnp.zeros((n_mem, MEM_W), F32)
        dv = jnp.zeros((n_mem, MEM_W), F32)
        for hm in _head_masks(tm, MEM_W, MEM_HEADS):
            qh = _keep(hm, q)
            dyh = _keep(hm, dy)
            p = _mem_probs(qh, k)
            dp = _dot(dyh, v, NT)
            dv = dv + _dot(p.astype(BF16), dyh, TN)
            ds = (p * (dp - jnp.sum(dp * p, axis=-1, keepdims=True)) * QK_SCALE).astype(BF16)
            dq = dq + jnp.where(hm, _dot(ds, k), 0.0)
            dk = dk + _dot(ds, qh, TN)
        dq_ref[...] = dq.astype(BF16)
        dk_ref[...] += dk
        dv_ref[...] += dv

    const = lambda i: (0, 0)
    return pl.pallas_call(
        body, name=name, grid=(s // tm,),
        in_specs=[pl.BlockSpec((tm, MEM_W), lambda i: (i, q_blk)), pl.BlockSpec((n_mem, MEM_W), lambda i: (0, 2 * layer)),
                  pl.BlockSpec((n_mem, MEM_W), lambda i: (0, 2 * layer + 1)), pl.BlockSpec((tm, MEM_W), lambda i: (i, dy_blk)),
                  pl.BlockSpec(memory_space=pl.ANY)],
        out_specs=[pl.BlockSpec((tm, MEM_W), lambda i: (i, q_blk)), pl.BlockSpec((n_mem, MEM_W), const),
                   pl.BlockSpec((n_mem, MEM_W), const)],
        out_shape=[jax.ShapeDtypeStruct(into.shape, BF16), jax.ShapeDtypeStruct((n_mem, MEM_W), F32),
                   jax.ShapeDtypeStruct((n_mem, MEM_W), F32)],
        input_output_aliases={4: 0}, compiler_params=_params("arbitrary"),
    )(proj, mem_kv, mem_kv, dy, into)


SB_KEYS = 256
SB_SUB = SB_KEYS // SB_BLK
SB_QROWS = 256
SB_QB = SB_QROWS // SB_BLK
SB_CHAINS = 2 * SB_QB
SB_DEAD = -110.0


def _split(xf):
    hi = xf.astype(BF16)
    return hi, (xf - hi.astype(F32)).astype(BF16)


def _sb_consts():
    row = lax.bitwise_and(lax.broadcasted_iota(jnp.int32, (2 * SB_BLK, 2 * SB_BLK), 0), SB_BLK - 1)
    col = lax.broadcasted_iota(jnp.int32, (2 * SB_BLK, 2 * SB_BLK), 1)
    ones = col >= SB_BLK
    after2 = jnp.where(ones | (row > col), -1.0, 0.0).astype(BF16)
    from2 = jnp.where(ones | (row >= col), 1.0, 0.0).astype(BF16)
    r = lax.broadcasted_iota(jnp.int32, (SB_BLK, SB_BLK), 0)
    c = lax.broadcasted_iota(jnp.int32, (SB_BLK, SB_BLK), 1)
    return after2, from2, c - r, [c < HEAD_DIM, c >= HEAD_DIM]


def _suffix(xf, tri2):
    hi, lo = _split(xf)
    return _dot(jnp.concatenate([hi, lo], axis=1), tri2)


def _sb_logs(z, mask):
    softplus = jnp.maximum(z, 0.0) + jnp.log(1.0 + jnp.exp(-jnp.abs(z)))
    log_beta = z - softplus
    if mask is not None:
        softplus = jnp.where(mask, softplus, 0.0)
    return softplus, log_beta


def _sb_queries(q_ref, heads):
    q = q_ref[...].astype(F32) * QK_SCALE
    return [jnp.where(hm, q[r * SB_BLK:(r + 1) * SB_BLK], 0.0).astype(BF16) for r in range(SB_QB) for hm in heads]


def _sb_walk(i, block, state):
    places = SB_SUB // SB_QB
    assert places in (1, 2)
    own = lax.shift_right_logical(i * SB_QB, SB_SUB.bit_length() - 1)
    firsts = [[(v * SB_QB + r) * SB_BLK for r in range(SB_QB) for _ in range(2)] for v in range(places)]
    if places == 1:
        state = block(own, state, firsts[0])
    else:
        state = lax.cond(lax.bitwise_and(i, 1) == 0, lambda st: block(own, st, firsts[0]),
                         lambda st: block(own, st, firsts[1]), state)

    def live(carry):
        j, st = carry
        most = st[0][0]
        for run in st[0][1:]:
            most = jnp.maximum(most, run)
        return (j >= 0) & (jnp.max(most) > SB_DEAD)

    return lax.while_loop(live, lambda carry: (carry[0] - 1, block(carry[0], carry[1], None)), (own - 1, state))[1]


def _sb_tiles(first):
    out = []
    for c in reversed(range(SB_SUB)):
        for n in range(SB_CHAINS):
            if first is None or c * SB_BLK < first[n]:
                out.append((c, n, "before"))
            elif c * SB_BLK == first[n]:
                out.append((c, n, "diagonal"))
    return out


def _sb_heads_apart(stacked, heads, r):
    return jnp.where(heads[0], stacked[2 * r * SB_BLK:(2 * r + 1) * SB_BLK],
                     stacked[(2 * r + 1) * SB_BLK:(2 * r + 2) * SB_BLK])


def sb_fwd(proj, kv, name):
    s = proj.shape[0]
    assert s % SB_KEYS == 0 and SB_KEYS % SB_QROWS == 0

    def body(q_ref, k_ref, v_ref, o_ref, y_ref):
        after2, _, col_minus_row, heads = _sb_consts()
        q_all = jnp.concatenate(_sb_queries(q_ref, heads), axis=0)
        key_before_query = col_minus_row < 0

        def block(j, state, first):
            runs, acc = list(state[0]), state[1]
            rows = pl.ds(pl.multiple_of(j * SB_KEYS, SB_KEYS), SB_KEYS)
            kb, vb = k_ref[rows, :], v_ref[rows, :]
            z = _dot(q_all, kb, NT)
            pend = {}
            parts = [[jnp.zeros((SB_BLK, SB_BLK), BF16)] * SB_SUB for _ in range(SB_CHAINS)]
            for c, n, where in _sb_tiles(first):
                mask = key_before_query if where == "diagonal" else None
                softplus, lb = _sb_logs(z[n * SB_BLK:(n + 1) * SB_BLK, c * SB_BLK:(c + 1) * SB_BLK], mask)
                pend[c, n] = (lb, _suffix(softplus, after2), mask)
            for c, n, _ in _sb_tiles(first):
                lb, r, mask = pend.pop((c, n))
                a = jnp.exp(lb + r[:, :SB_BLK] + runs[n])
                if mask is not None:
                    a = jnp.where(mask, a, 0.0)
                parts[n][c] = a.astype(BF16)
                runs[n] = runs[n] + r[:, SB_BLK:]
            a_all = jnp.concatenate([jnp.concatenate(p, axis=1) for p in parts], axis=0)
            return tuple(runs), acc + _dot(a_all, vb)

        zero = jnp.zeros((SB_BLK, LANES), F32)
        state = _sb_walk(pl.program_id(1), block, ((zero,) * SB_CHAINS, jnp.zeros((SB_CHAINS * SB_BLK, LANES), F32)))
        for r in range(SB_QB):
            out = _sb_heads_apart(state[1], heads, r)
            o_ref[r * SB_BLK:(r + 1) * SB_BLK, :] = out
            y_ref[r * SB_BLK:(r + 1) * SB_BLK, :] = out.astype(BF16)

    pairs = SB_W // LANES
    block_spec = pl.BlockSpec((SB_QROWS, LANES), lambda p, i: (i, p))
    return pl.pallas_call(
        body, name=name, grid=(pairs, s // SB_QROWS),
        in_specs=[block_spec, pl.BlockSpec((s, LANES), lambda p, i: (0, p)),
                  pl.BlockSpec((s, LANES), lambda p, i: (0, pairs + p))],
        out_specs=[block_spec, block_spec],
        out_shape=[jax.ShapeDtypeStruct((s, SB_W), F32), jax.ShapeDtypeStruct((s, SB_W + MEM_W), BF16)],
        compiler_params=_params("arbitrary", "arbitrary"),
    )(proj, kv, kv)


def sb_bwd(proj, kv, out, dy, name):
    s = proj.shape[0]

    def body(q_ref, k_ref, v_ref, o_ref, do_ref, dq_ref, dk_ref, dv_ref):
        i = pl.program_id(1)

        @pl.when(i == 0)
        def _():
            dk_ref[...] = jnp.zeros_like(dk_ref)
            dv_ref[...] = jnp.zeros_like(dv_ref)

        after2, from2, col_minus_row, heads = _sb_consts()
        q_all = jnp.concatenate(_sb_queries(q_ref, heads), axis=0)
        key_before_query = col_minus_row < 0
        d_out = do_ref[...].astype(F32)
        prod = d_out * o_ref[...]
        dos, totals = [], []
        for r in range(SB_QB):
            rr = slice(r * SB_BLK, (r + 1) * SB_BLK)
            for hm in heads:
                dos.append(jnp.where(hm, d_out[rr], 0.0).astype(BF16))
                totals.append(jnp.broadcast_to(jnp.sum(jnp.where(hm, prod[rr], 0.0), axis=1, keepdims=True),
                                               (SB_BLK, SB_BLK)))
        do_all = jnp.concatenate(dos, axis=0)

        def block(j, state, first):
            runs, seens, dq = list(state[0]), list(state[1]), state[2]
            rows = pl.ds(pl.multiple_of(j * SB_KEYS, SB_KEYS), SB_KEYS)
            kb, vb = k_ref[rows, :], v_ref[rows, :]
            z = _dot(q_all, kb, NT)
            da = _dot(do_all, vb, NT)
            pend, pend2 = {}, {}
            a_parts = [[jnp.zeros((SB_BLK, SB_BLK), BF16)] * SB_SUB for _ in range(SB_CHAINS)]
            dz_parts = [[jnp.zeros((SB_BLK, SB_BLK), BF16)] * SB_SUB for _ in range(SB_CHAINS)]
            for c, n, where in _sb_tiles(first):
                mask = key_before_query if where == "diagonal" else None
                softplus, lb = _sb_logs(z[n * SB_BLK:(n + 1) * SB_BLK, c * SB_BLK:(c + 1) * SB_BLK], mask)
                pend[c, n] = (softplus, lb, _suffix(softplus, after2), mask)
            for c, n, _ in _sb_tiles(first):
                softplus, lb, r, mask = pend.pop((c, n))
                a = jnp.exp(lb + r[:, :SB_BLK] + runs[n])
                if mask is not None:
                    a = jnp.where(mask, a, 0.0)
                runs[n] = runs[n] + r[:, SB_BLK:]
                ab = a.astype(BF16)
                a_parts[n][c] = ab
                dl = ab.astype(F32) * da[n * SB_BLK:(n + 1) * SB_BLK, c * SB_BLK:(c + 1) * SB_BLK]
                pend2[c, n] = (softplus, lb, dl, _suffix(dl, from2), mask)
            for c, n, _ in _sb_tiles(first):
                softplus, lb, dl, r2, mask = pend2.pop((c, n))
                d_lom = totals[n] - (r2[:, :SB_BLK] + seens[n])
                if mask is not None:
                    d_lom = jnp.where(mask, d_lom, 0.0)
                seens[n] = seens[n] + r2[:, SB_BLK:]
                dz_parts[n][c] = (dl * jnp.exp(-softplus) - d_lom * jnp.exp(lb)).astype(BF16)
            a_all = jnp.concatenate([jnp.concatenate(p, axis=1) for p in a_parts], axis=0)
            dz_all = jnp.concatenate([jnp.concatenate(p, axis=1) for p in dz_parts], axis=0)
            dv_ref[rows, :] += _dot(a_all, do_all, TN)
            dk_ref[rows, :] += _dot(dz_all, q_all, TN)
            return tuple(runs), tuple(seens), dq + _dot(dz_all, kb)

        zero = jnp.zeros((SB_BLK, LANES), F32)
        state = _sb_walk(i, block, ((zero,) * SB_CHAINS, (zero,) * SB_CHAINS,
                                    jnp.zeros((SB_CHAINS * SB_BLK, LANES), F32)))
        for r in range(SB_QB):
            dq_ref[r * SB_BLK:(r + 1) * SB_BLK, :] = (_sb_heads_apart(state[2], heads, r) * QK_SCALE).astype(BF16)

    pairs = SB_W // LANES
    blk = lambda p, i: (i, p)
    col = lambda p, i: (0, p)
    return pl.pallas_call(
        body, name=name, grid=(pairs, s // SB_QROWS),
        in_specs=[pl.BlockSpec((SB_QROWS, LANES), blk), pl.BlockSpec((s, LANES), col),
                  pl.BlockSpec((s, LANES), lambda p, i: (0, pairs + p)), pl.BlockSpec((SB_QROWS, LANES), blk),
                  pl.BlockSpec((SB_QROWS, LANES), blk)],
        out_specs=[pl.BlockSpec((SB_QROWS, LANES), blk), pl.BlockSpec((s, LANES), col), pl.BlockSpec((s, LANES), col)],
        out_shape=[jax.ShapeDtypeStruct((s, SB_W + MEM_W), BF16), jax.ShapeDtypeStruct((s, SB_W), F32),
                   jax.ShapeDtypeStruct((s, SB_W), F32)],
        compiler_params=_params("arbitrary", "arbitrary"),
    )(proj, kv, kv, out, dy)


def final_loss(x, g, target, name):
    s, d = x.shape
    tm = _tile(s, 256, 8)

    def body(x_ref, g_ref, t_ref, loss_ref, dx_ref, dg_ref):
        @pl.when(pl.program_id(0) == 0)
        def _():
            loss_ref[...] = jnp.zeros_like(loss_ref)
            dg_ref[...] = jnp.zeros_like(dg_ref)

        xf = x_ref[...]
        r = _rms(xf)
        xhat = xf * r
        gain = g_ref[...]
        diff = xhat * gain - t_ref[...]
        sq = jnp.sum(jnp.sum(diff * diff, axis=1, keepdims=True), axis=0, keepdims=True)
        loss_ref[...] += jnp.broadcast_to(sq, loss_ref.shape)
        dy = diff * (1.0 / d)
        dg_ref[...] += jnp.sum(dy * xhat, axis=0, keepdims=True)
        dxhat = dy * gain
        dx_ref[...] = r * (dxhat - xhat * jnp.mean(dxhat * xhat, axis=-1, keepdims=True))

    row = lambda i: (i, 0)
    const = lambda i: (0, 0)
    return pl.pallas_call(
        body, name=name, grid=(s // tm,),
        in_specs=[pl.BlockSpec((tm, d), row), pl.BlockSpec((1, d), const), pl.BlockSpec((tm, d), row)],
        out_specs=[pl.BlockSpec((8, LANES), const), pl.BlockSpec((tm, d), row), pl.BlockSpec((1, d), const)],
        out_shape=[jax.ShapeDtypeStruct((8, LANES), F32), jax.ShapeDtypeStruct((s, d), F32), jax.ShapeDtypeStruct((1, d), F32)],
        compiler_params=_params("arbitrary"),
    )(x, g, target)


def adamw(w, parts, m, v, name):
    rows, cols = w.shape
    k = parts.shape[0]
    tr = _tile(rows, 512, 16)
    c1, c2 = 1.0 - ADAM_B1 ** ADAM_STEP, 1.0 - ADAM_B2 ** ADAM_STEP

    def body(w_ref, p_ref, m_ref, v_ref, g_ref, d_ref, nm_ref, nv_ref):
        grad = p_ref[0].astype(F32)
        for s in range(1, k):
            grad = grad + p_ref[s].astype(F32)
        nm = ADAM_B1 * m_ref[...] + (1.0 - ADAM_B1) * grad
        nv = ADAM_B2 * v_ref[...] + (1.0 - ADAM_B2) * (grad * grad)
        g_ref[...] = grad
        d_ref[...] = -ADAM_LR * ((nm / c1) / (jnp.sqrt(nv / c2) + ADAM_EPS) + ADAM_WD * w_ref[...])
        nm_ref[...] = nm
        nv_ref[...] = nv

    spec = pl.BlockSpec((tr, cols), lambda i: (i, 0))
    shape = jax.ShapeDtypeStruct((rows, cols), F32)
    return pl.pallas_call(
        body, name=name, grid=(rows // tr,),
        in_specs=[spec, pl.BlockSpec((k, tr, cols), lambda i: (0, i, 0)), spec, spec],
        out_specs=[spec] * 4, out_shape=[shape] * 4,
        compiler_params=_params("arbitrary"),
    )(w, parts, m, v)


SHARDED = {"ffn1_w_gate": 2, "ffn1_w_up": 2, "ffn1_w_down": 1, "ffn2_w_gate": 2, "ffn2_w_up": 2, "ffn2_w_down": 1,
           "w_mem_kv": 1, "a_w_in": 2, "a_w_out": 1, "w_kv": 1, "b_w_in": 1, "b_w_out": 1}
TRANSPOSED = ("ffn1_w_gate", "ffn1_w_up", "ffn2_w_gate", "ffn2_w_up")
SMALL = ["ffn1_norm", "mix_norm", "ffn2_norm", "mem_norm", "kv_norm", "final_norm", "a_v_norm", "a_w_spatial", "a_b_spatial"]
WEIGHTS = ["ffn1_norm", "ffn1_w_gate", "ffn1_w_up", "ffn1_w_down", "mix_norm", "ffn2_norm", "ffn2_w_gate", "ffn2_w_up",
           "ffn2_w_down", "mem_norm", "w_mem_kv", "a_w_in", "a_v_norm", "a_w_spatial", "a_b_spatial", "a_w_out", "kv_norm",
           "w_kv", "b_w_in", "b_w_out", "final_norm"]


def _all_sum(parts, name):
    flat = jnp.concatenate([p.reshape(-1) for p in parts])
    pad = (-flat.size) % (16 * LANES)
    buf = jnp.pad(flat, (0, pad)).reshape(-1, LANES)
    total = sum_leading(exchange([buf], "all", True, name)[0], F32, name + "_sum").reshape(-1)
    out, off = [], 0
    for p in parts:
        out.append(total[off:off + p.size].reshape(p.shape))
        off += p.size
    return out


def _device_index():
    return 4 * lax.axis_index("x") + 2 * lax.axis_index("y") + lax.axis_index("c")


def kernel(x, mem, ffn1_norm, ffn1_w_gate, ffn1_w_up, ffn1_w_down, mix_norm, ffn2_norm, ffn2_w_gate, ffn2_w_up, ffn2_w_down, mem_norm, w_mem_kv, a_w_in, a_v_norm, a_w_spatial, a_b_spatial, a_w_out, kv_norm, w_kv, b_w_in, b_w_out, final_norm, loss_target, m_ffn1_norm, m_ffn1_w_gate, m_ffn1_w_up, m_ffn1_w_down, m_mix_norm, m_ffn2_norm, m_ffn2_w_gate, m_ffn2_w_up, m_ffn2_w_down, m_mem_norm, m_w_mem_kv, m_a_w_in, m_a_v_norm, m_a_w_spatial, m_a_b_spatial, m_a_w_out, m_kv_norm, m_w_kv, m_b_w_in, m_b_w_out, m_final_norm, v_ffn1_norm, v_ffn1_w_gate, v_ffn1_w_up, v_ffn1_w_down, v_mix_norm, v_ffn2_norm, v_ffn2_w_gate, v_ffn2_w_up, v_ffn2_w_down, v_mem_norm, v_w_mem_kv, v_a_w_in, v_a_v_norm, v_a_w_spatial, v_a_b_spatial, v_a_w_out, v_kv_norm, v_w_kv, v_b_w_in, v_b_w_out, v_final_norm):
    weights = dict(ffn1_norm=ffn1_norm, ffn1_w_gate=ffn1_w_gate, ffn1_w_up=ffn1_w_up, ffn1_w_down=ffn1_w_down, mix_norm=mix_norm, ffn2_norm=ffn2_norm, ffn2_w_gate=ffn2_w_gate, ffn2_w_up=ffn2_w_up, ffn2_w_down=ffn2_w_down, mem_norm=mem_norm, w_mem_kv=w_mem_kv, a_w_in=a_w_in, a_v_norm=a_v_norm, a_w_spatial=a_w_spatial, a_b_spatial=a_b_spatial, a_w_out=a_w_out, kv_norm=kv_norm, w_kv=w_kv, b_w_in=b_w_in, b_w_out=b_w_out, final_norm=final_norm)
    mom1 = dict(ffn1_norm=m_ffn1_norm, ffn1_w_gate=m_ffn1_w_gate, ffn1_w_up=m_ffn1_w_up, ffn1_w_down=m_ffn1_w_down, mix_norm=m_mix_norm, ffn2_norm=m_ffn2_norm, ffn2_w_gate=m_ffn2_w_gate, ffn2_w_up=m_ffn2_w_up, ffn2_w_down=m_ffn2_w_down, mem_norm=m_mem_norm, w_mem_kv=m_w_mem_kv, a_w_in=m_a_w_in, a_v_norm=m_a_v_norm, a_w_spatial=m_a_w_spatial, a_b_spatial=m_a_b_spatial, a_w_out=m_a_w_out, kv_norm=m_kv_norm, w_kv=m_w_kv, b_w_in=m_b_w_in, b_w_out=m_b_w_out, final_norm=m_final_norm)
    mom2 = dict(ffn1_norm=v_ffn1_norm, ffn1_w_gate=v_ffn1_w_gate, ffn1_w_up=v_ffn1_w_up, ffn1_w_down=v_ffn1_w_down, mix_norm=v_mix_norm, ffn2_norm=v_ffn2_norm, ffn2_w_gate=v_ffn2_w_gate, ffn2_w_up=v_ffn2_w_up, ffn2_w_down=v_ffn2_w_down, mem_norm=v_mem_norm, w_mem_kv=v_w_mem_kv, a_w_in=v_a_w_in, a_v_norm=v_a_v_norm, a_w_spatial=v_a_w_spatial, a_b_spatial=v_a_b_spatial, a_w_out=v_a_w_out, kv_norm=v_kv_norm, w_kv=v_w_kv, b_w_in=v_b_w_in, b_w_out=v_b_w_out, final_norm=v_final_norm)

    dev = _device_index()
    xs, mem_in, target = x[0], mem[0], loss_target[0]
    d_model = xs.shape[1]
    shards = {n: weights[n] for n in SHARDED}

    def mix_keys(l):
        w_in, w_out, idx = ("a_w_in", "a_w_out", l) if l < N_A else ("b_w_in", "b_w_out", l - N_A)
        return (w_in, idx), (w_out, idx)

    def ffn_keys(ffn):
        return ([], []) if ffn is None else ([(ffn[0] + "_w_gate", ffn[1]), (ffn[0] + "_w_up", ffn[1])],
                                             [(ffn[0] + "_w_down", ffn[1])])

    def ffn_after(f, l):
        return ("ffn2", l) if f == "ffn1" else (("ffn1", l + 1) if l + 1 < DEPTH else None)

    def cut_axis(key):
        return SHARDED[key[0]] - (0 if key[1] is None else 1)

    def block(key):
        return (shards[key[0]] if key[1] is None else shards[key[0]][key[1]]).astype(BF16)

    def carrying(call, keys, same_src, source, store):
        if not keys:
            return call(None)
        result, arrived = call(Side([source(k) for k in keys], same_src))
        store.update(zip(keys, arrived))
        return result

    first_gu, first_down = ffn_keys(("ffn1", 0))
    first = first_gu + first_down + [mix_keys(0)[0], ("w_mem_kv", None)]
    landed = dict(zip(first, exchange([block(k) for k in first], "all", True, "gather_first")))
    assembled = {}

    def whole(n, l=None):
        if (n, l) not in assembled:
            got = landed[n, l]
            if cut_axis((n, l)) == 0:
                assembled[n, l] = got.reshape((-1,) + got.shape[2:])
            else:
                pieces = [got[d] for d in range(N_DEV)]
                if n.endswith("_w_gate"):
                    pieces += [landed[n.replace("_w_gate", "_w_up"), l][d] for d in range(N_DEV)]
                assembled[n, l] = jnp.concatenate(pieces, axis=cut_axis((n, l)))
        return assembled[n, l]

    def whole_gu(f, l):
        return whole(f + "_w_gate", l)

    vn_width = a_v_norm.shape[1]
    a_v_full = _all_sum([lax.dynamic_update_slice(jnp.zeros((N_A, N_DEV * vn_width), F32), a_v_norm, (0, dev * vn_width))],
                        "gather_v_norm")[0]
    row = lambda v: v.reshape(1, -1)
    w_mem_cat = whole("w_mem_kv").transpose(1, 0, 2).reshape(d_model, -1)
    bias = [jnp.repeat(a_b_spatial[i].T, GM_P, axis=1) for i in range(N_A)]

    mem_kv, mem_h = norm_mm(mem_in, row(mem_norm), w_mem_cat, BF16, "mem_kv", emit_h=True)

    def ffn_fwd(xin, f, l):
        keys_gu, keys_down = ffn_keys(ffn_after(f, l))
        if ffn_after(f, l) == ("ffn1", N_A):
            keys_down = keys_down + [("w_kv", None)]
        gu = carrying(lambda side: norm_mm(xin, row(weights[f + "_norm"][l]), whole_gu(f, l), BF16, "ffn_gu", side=side),
                      keys_gu, True, block, landed)
        out = carrying(lambda side: swiglu_mm_res(gu, whole(f + "_w_down", l), xin, 0.5, "ffn_down", side=side),
                       keys_down, True, block, landed)
        return out, gu

    saved = []
    kv = x_kv = None
    cur = xs
    for l in range(DEPTH):
        st = {"x0": cur}
        if l == N_A:
            x_kv = cur
            kv = norm_mm(cur, row(kv_norm), whole("w_kv"), BF16, "kv_proj")
        st["x1"], st["gu1"] = ffn_fwd(cur, "ffn1", l)
        key_in, key_out = mix_keys(l)
        proj = carrying(lambda side: norm_mm(st["x1"], row(mix_norm[l]), whole(*key_in), F32 if l < N_A else BF16,
                                             "a_proj" if l < N_A else "b_proj", side=side), [key_out], True, block, landed)
        if l < N_A:
            y_tok = gmlp_fwd(proj, row(a_v_full[l]), a_w_spatial[l], bias[l], "gmlp_fwd")
            st["y"] = mem_fwd(proj, 2 * GM_W // MEM_W, mem_kv, l, y_tok, GM_W // MEM_W, "mem_fwd_a")
        else:
            st["sb_out"], y_tok = sb_fwd(proj, kv, "sb_fwd")
            st["y"] = mem_fwd(proj, SB_W // MEM_W, mem_kv, l, y_tok, SB_W // MEM_W, "mem_fwd_b")
        st["proj"] = proj
        st["x2"] = carrying(lambda side: mm_res(st["y"], whole(*key_out), st["x1"], 1.0, "mix_out", side=side),
                            [mix_keys(l + 1)[0]] if l + 1 < DEPTH else [], True, block, landed)
        cur, st["gu2"] = ffn_fwd(st["x2"], "ffn2", l)
        saved.append(st)

    loss_blk, dx, d_final = final_loss(cur, row(final_norm), target, "final_loss")
    loss = lax.psum(loss_blk[0, 0] * (0.5 / d_model), AXES)

    grads = {n: [None] * weights[n].shape[0] for n in WEIGHTS if weights[n].ndim >= 2 and n not in ("w_kv",)}
    grads["final_norm"] = d_final.reshape(-1)
    d_mem_kv = [None] * DEPTH
    d_kv = []

    summed = {}

    def pieces(key):
        g = (grads[key[0]] if key[1] is None else grads[key[0]][key[1]]).astype(BF16)
        axis = 0 if key[0] in TRANSPOSED else cut_axis(key)
        cut = g.reshape(g.shape[:axis] + (N_DEV, g.shape[axis] // N_DEV) + g.shape[axis + 1:])
        return jnp.moveaxis(cut, axis, 0)

    def ffn_bwd(dx, xin, gu, f, l):
        keys_gu, keys_down = ffn_keys(ffn_after(f, l))
        keys_mix = list(mix_keys(l)) if f == "ffn1" else ([("w_kv", None)] if l + 1 == N_A else [])
        d_gu = carrying(lambda side: mm_nt_swiglu_bwd(dx, whole(f + "_w_down", l), gu, 0.5, "ffn_dgu", side=side),
                        keys_down, False, pieces, summed)
        dx_new, d_gain, h = carrying(
            lambda side: mm_nt_normbwd(d_gu, whole_gu(f, l), xin, row(weights[f + "_norm"][l]), dx, "ffn_dx", side=side),
            keys_gu, False, pieces, summed)
        d_wgu_t = carrying(lambda side: mm_tn(d_gu, h, 1.0, "ffn_dwgu", ta_target=1408, tb_target=1024, side=side),
                           keys_mix, False, pieces, summed)
        d_wdown = swiglu_mm_tn(gu, dx, 0.5, "ffn_dwdown")
        half = d_wgu_t.shape[0] // 2
        grads[f + "_w_gate"][l], grads[f + "_w_up"][l] = d_wgu_t[:half], d_wgu_t[half:]
        grads[f + "_w_down"][l] = d_wdown
        grads[f + "_norm"][l] = d_gain.reshape(-1)
        return dx_new

    for l in reversed(range(DEPTH)):
        st = saved[l]
        dx = ffn_bwd(dx, st["x2"], st["gu2"], "ffn2", l)
        proj = st["proj"]
        (key_in, idx), (key_out, _) = mix_keys(l)
        w_in, w_out = whole(key_in, idx), whole(key_out, idx)
        dy = mm_nt(dx, w_out, 1.0, "mix_dy")
        grads[key_out][idx] = mm_tn(st["y"], dx, 1.0, "mix_dwout", tb_target=1024)
        if l < N_A:
            d_uv, d_ws, d_bs, d_vgain = gmlp_bwd(proj, dy, row(a_v_full[l]), a_w_spatial[l], bias[l], "gmlp_bwd")
            grads["a_w_spatial"][l], grads["a_b_spatial"][l], grads["a_v_norm"][l] = d_ws, d_bs[:, :, 0], d_vgain.reshape(-1)
            d_proj, d_k, d_v = mem_bwd(proj, 2 * GM_W // MEM_W, mem_kv, l, dy, GM_W // MEM_W, d_uv, "mem_bwd_a")
        else:
            d_qsb, d_ksb, d_vsb = sb_bwd(proj, kv, st["sb_out"], dy, "sb_bwd")
            d_kv.append(jnp.concatenate([d_ksb, d_vsb], axis=1))
            d_proj, d_k, d_v = mem_bwd(proj, SB_W // MEM_W, mem_kv, l, dy, SB_W // MEM_W, d_qsb, "mem_bwd_b")
        d_mem_kv[l] = jnp.concatenate([d_k, d_v], axis=1)
        dx, d_gain, h = mm_nt_normbwd(d_proj, w_in, st["x1"], row(mix_norm[l]), dx, "mix_dx")
        grads["mix_norm"][l] = d_gain.reshape(-1)
        grads[key_in][idx] = mm_tn(h, d_proj, 1.0, "mix_dwin")
        dx = ffn_bwd(dx, st["x0"], st["gu1"], "ffn1", l)
        if l == N_A:
            d_kv_b = sum_leading(jnp.stack(d_kv), BF16, "kv_dsum")
            dx, d_gain, h = mm_nt_normbwd(d_kv_b, whole("w_kv"), x_kv, row(kv_norm), dx, "kv_dx")
            grads["kv_norm"] = d_gain.reshape(-1)
            grads["w_kv"] = mm_tn(h, d_kv_b, 1.0, "kv_dw")

    d_mem_all = jnp.concatenate(d_mem_kv, axis=1).astype(BF16)
    _, d_gain, _ = mm_nt_normbwd(d_mem_all, w_mem_cat, mem_in, row(mem_norm), None, "mem_dnorm")
    grads["mem_norm"] = d_gain.reshape(-1)
    d_wmem = mm_tn(mem_h, d_mem_all, 1.0, "mem_dw")
    grads["w_mem_kv"] = d_wmem.reshape(d_model, DEPTH, -1).transpose(1, 0, 2)

    last = first_gu + first_down + [("w_mem_kv", None)]
    summed.update(zip(last, exchange([pieces(k) for k in last], "all", False, "scatter_last")))
    parts = {n: summed[n, None] if (n, None) in summed else
             jnp.stack([summed[n, i] for i in range(weights[n].shape[0])], axis=1) for n in SHARDED}
    grads = {n: (jnp.stack(g) if isinstance(g, list) else g) for n, g in grads.items()}
    for n, g in zip(SMALL, _all_sum([grads[n] for n in SMALL], "sum_small")):
        parts[n] = g[None]
    parts["a_v_norm"] = lax.dynamic_slice(parts["a_v_norm"], (0, 0, dev * vn_width), (1,) + a_v_norm.shape)

    reduced, deltas, new_m, new_v = {}, {}, {}, {}
    for n in WEIGHTS:
        turn = (lambda a: jnp.swapaxes(a, -1, -2)) if n in TRANSPOSED else (lambda a: a)
        w = turn(weights[n])
        view = (lambda a: turn(a).reshape(-1, w.shape[-1]))
        res = adamw(view(weights[n]), parts[n].reshape(parts[n].shape[0], -1, w.shape[-1]), view(mom1[n]), view(mom2[n]),
                    "adamw")
        reduced[n], deltas[n], new_m[n], new_v[n] = [turn(r.reshape(w.shape)) for r in res]

    return (loss, dx[None], *[reduced[n] for n in WEIGHTS], *[deltas[n] for n in WEIGHTS],
            *[new_m[n] for n in WEIGHTS], *[new_v[n] for n in WEIGHTS])
```

```python
import jax
import jax.numpy as jnp
from jax import lax
from jax.experimental import pallas as pl
from jax.experimental.pallas import tpu as pltpu

F32, BF16 = jnp.float32, jnp.bfloat16
MESH_ID = pl.DeviceIdType.MESH
AXES = ("x", "y", "c")
N_DEV = 8

EPS = 1e-6
DEPTH, N_A = 4, 2
GM_W, GM_GROUPS, GM_P = 768, 6, 128
MEM_W, MEM_HEADS, HEAD_DIM = 256, 4, 64
SB_W, SB_BLK = 768, 128
LANES = 128
QK_SCALE = HEAD_DIM ** -0.5
GELU_C, GELU_A = 0.7978845608028654, 0.044715

ADAM_LR, ADAM_B1, ADAM_B2, ADAM_EPS, ADAM_WD, ADAM_STEP = 0.001, 0.9, 0.999, 1e-08, 0.01, 10

VMEM_LIMIT = 56 * 1024 * 1024

NT = (((1,), (1,)), ((), ()))
TN = (((0,), (0,)), ((), ()))


def _params(*sem):
    return pltpu.CompilerParams(dimension_semantics=sem, vmem_limit_bytes=VMEM_LIMIT)


def _tile(n, target, mult=LANES):
    best = None
    for t in range(mult, min(n, target) + 1, mult):
        if n % t == 0:
            best = t
    return best if best is not None else n


def _dot(a, b, dims=None):
    if dims is None:
        return jnp.dot(a, b, preferred_element_type=F32)
    return lax.dot_general(a, b, dims, preferred_element_type=F32)


def exchange(srcs, group, same_src, name, split=False):
    size = {"pair": 2, "quad": 4, "all": 8}[group]
    n = len(srcs)
    chunk_shapes = [tuple(s.shape) if same_src else tuple(s.shape[1:]) for s in srcs]
    pieces = [cs[0] if split else 1 for cs in chunk_shapes]
    n_dma = sum(pieces)

    def body(*refs):
        src_refs, out_refs = refs[:n], refs[n:2 * n]
        send_sems, recv_sems, local_sems = refs[2 * n:]
        x, y, c = lax.axis_index("x"), lax.axis_index("y"), lax.axis_index("c")
        if group == "pair":
            me, dev = c, lambda p: (x, y, p)
        elif group == "quad":
            me, dev = 2 * x + y, lambda p: (p // 2, p % 2, c)
        else:
            me, dev = 4 * x + 2 * y + c, lambda p: (p // 4, (p // 2) % 2, p % 2)

        def chunk(t, idx):
            return src_refs[t] if same_src else src_refs[t].at[idx]

        def copies(k, idx, slot, peer):
            out, w = [], k * n_dma
            for t in range(n):
                src, dst = chunk(t, idx), out_refs[t].at[slot]
                for s_ref, d_ref in ([(src.at[u], dst.at[u]) for u in range(pieces[t])] if split else [(src, dst)]):
                    out.append(pltpu.make_async_remote_copy(
                        src_ref=s_ref, dst_ref=d_ref, send_sem=send_sems.at[w], recv_sem=recv_sems.at[w],
                        device_id=dev(peer), device_id_type=MESH_ID))
                    w += 1
            return out

        local = [pltpu.make_async_copy(chunk(t, me), out_refs[t].at[me], local_sems.at[t]) for t in range(n)]
        for cp in local:
            cp.start()
        sends = []
        for k in range(1, size):
            peer = (me + k) % size
            sends += copies(k, peer, me, peer)
        for cp in sends:
            cp.start()
        for k in range(1, size):
            sender = (me + size - k) % size
            for cp in copies(k, me, sender, sender):
                cp.wait_recv()
        for cp in sends:
            cp.wait_send()
        for cp in local:
            cp.wait()

    hbm = pl.BlockSpec(memory_space=pltpu.HBM)
    return pl.pallas_call(
        body, name=name,
        out_shape=[jax.ShapeDtypeStruct((size,) + cs, s.dtype) for cs, s in zip(chunk_shapes, srcs)],
        in_specs=[hbm] * n, out_specs=[hbm] * n,
        scratch_shapes=[pltpu.SemaphoreType.DMA((size * n_dma,)), pltpu.SemaphoreType.DMA((size * n_dma,)),
                        pltpu.SemaphoreType.DMA((n,))],
    )(*srcs)


class Side:
    def __init__(self, srcs, same_src):
        self.srcs, self.same_src, self.n = list(srcs), same_src, len(srcs)
        self.chunk_shapes = [tuple(s.shape) if same_src else tuple(s.shape[1:]) for s in srcs]

    def out_shapes(self):
        return [jax.ShapeDtypeStruct((N_DEV,) + cs, s.dtype) for cs, s in zip(self.chunk_shapes, self.srcs)]

    def scratch(self):
        return [pltpu.SemaphoreType.DMA((N_DEV * self.n,)), pltpu.SemaphoreType.DMA((N_DEV * self.n,))]

    def _copies(self, src_refs, land_refs, send_sems, recv_sems, outgoing):
        me = 4 * lax.axis_index("x") + 2 * lax.axis_index("y") + lax.axis_index("c")
        out = []
        for k in range(1, N_DEV):
            peer = (me + k) % N_DEV if outgoing else (me + N_DEV - k) % N_DEV
            for t in range(self.n):
                src = src_refs[t] if self.same_src else src_refs[t].at[peer if outgoing else me]
                out.append(pltpu.make_async_remote_copy(
                    src_ref=src, dst_ref=land_refs[t].at[me if outgoing else peer],
                    send_sem=send_sems.at[k * self.n + t], recv_sem=recv_sems.at[k * self.n + t],
                    device_id=(peer // 4, (peer // 2) % 2, peer % 2), device_id_type=MESH_ID))
        return out

    def _own(self, src_refs, land_refs, send_sems):
        me = 4 * lax.axis_index("x") + 2 * lax.axis_index("y") + lax.axis_index("c")
        return [pltpu.make_async_copy(src_refs[t] if self.same_src else src_refs[t].at[me], land_refs[t].at[me],
                                      send_sems.at[t]) for t in range(self.n)]

    def start(self, src_refs, land_refs, send_sems, recv_sems):
        for cp in self._own(src_refs, land_refs, send_sems) + self._copies(src_refs, land_refs, send_sems, recv_sems, True):
            cp.start()

    def wait(self, src_refs, land_refs, send_sems, recv_sems):
        for cp in self._copies(src_refs, land_refs, send_sems, recv_sems, False):
            cp.wait_recv()
        for cp in self._copies(src_refs, land_refs, send_sems, recv_sems, True):
            cp.wait_send()
        for cp in self._own(src_refs, land_refs, send_sems):
            cp.wait()


def _call(body, side, name, grid, in_specs, out_specs, out_shape, scratch_shapes, dims, args):
    if side is None:
        res = pl.pallas_call(body, name=name, grid=grid, in_specs=in_specs, out_specs=out_specs, out_shape=out_shape,
                             scratch_shapes=scratch_shapes, compiler_params=_params(*dims))(*args)
        return list(res), []
    n_in, n_out, n_scr, ns = len(in_specs), len(out_specs), len(scratch_shapes), side.n

    def wrapped(*refs):
        ins, srcs = refs[:n_in], refs[n_in:n_in + ns]
        outs, lands = refs[n_in + ns:n_in + ns + n_out], refs[n_in + ns + n_out:n_in + 2 * ns + n_out]
        scratch, (send_sems, recv_sems) = refs[n_in + 2 * ns + n_out:n_in + 2 * ns + n_out + n_scr], refs[-2:]
        first, last = None, None
        for axis, steps in enumerate(grid):
            i = pl.program_id(axis)
            first = (i == 0) if first is None else first & (i == 0)
            last = (i == steps - 1) if last is None else last & (i == steps - 1)

        @pl.when(first)
        def _():
            side.start(srcs, lands, send_sems, recv_sems)

        body(*ins, *outs, *scratch)

        @pl.when(last)
        def _():
            side.wait(srcs, lands, send_sems, recv_sems)

    hbm = pl.BlockSpec(memory_space=pltpu.HBM)
    res = pl.pallas_call(
        wrapped, name=name, grid=grid, in_specs=list(in_specs) + [hbm] * ns, out_specs=list(out_specs) + [hbm] * ns,
        out_shape=list(out_shape) + side.out_shapes(), scratch_shapes=list(scratch_shapes) + side.scratch(),
        compiler_params=_params(*dims))(*args, *side.srcs)
    return list(res[:n_out]), list(res[n_out:])


def sum_leading(parts, out_dtype, name):
    k, rows, cols = parts.shape
    tr = _tile(rows, 512, 16)

    def body(p_ref, o_ref):
        acc = p_ref[0].astype(F32)
        for s in range(1, k):
            acc = acc + p_ref[s].astype(F32)
        o_ref[...] = acc.astype(o_ref.dtype)

    return pl.pallas_call(
        body, name=name, grid=(rows // tr,),
        in_specs=[pl.BlockSpec((k, tr, cols), lambda i: (0, i, 0))],
        out_specs=pl.BlockSpec((tr, cols), lambda i: (i, 0)),
        out_shape=jax.ShapeDtypeStruct((rows, cols), out_dtype),
        compiler_params=_params("arbitrary"),
    )(parts)


def _rms(xf):
    return lax.rsqrt(jnp.mean(xf * xf, axis=-1, keepdims=True) + EPS)


def norm_mm(x, g, w, out_dtype, name, emit_h=False, side=None, w_rows=False):
    m, d = x.shape
    n = w.shape[0] if w_rows else w.shape[1]
    tm, tn = _tile(m, 1024, 8), _tile(n, 1408)

    def body(x_ref, g_ref, w_ref, o_ref, *rest):
        h_ref = rest[-1]

        @pl.when(pl.program_id(1) == 0)
        def _():
            xf = x_ref[...]
            hb = ((xf * _rms(xf)) * g_ref[...]).astype(BF16)
            h_ref[...] = hb
            if emit_h:
                rest[0][...] = hb

        o_ref[...] = _dot(h_ref[...], w_ref[...], NT if w_rows else None).astype(o_ref.dtype)

    out_shape = [jax.ShapeDtypeStruct((m, n), out_dtype)]
    out_specs = [pl.BlockSpec((tm, tn), lambda i, j: (i, j))]
    if emit_h:
        out_shape.append(jax.ShapeDtypeStruct((m, d), BF16))
        out_specs.append(pl.BlockSpec((tm, d), lambda i, j: (i, 0)))
    res, landed = _call(
        body, side, name, (m // tm, n // tn),
        [pl.BlockSpec((tm, d), lambda i, j: (i, 0)), pl.BlockSpec((1, d), lambda i, j: (0, 0)),
         pl.BlockSpec((tn, d), lambda i, j: (j, 0)) if w_rows else pl.BlockSpec((d, tn), lambda i, j: (0, j))],
        out_specs, out_shape, [pltpu.VMEM((tm, d), BF16)], ("arbitrary", "arbitrary"), (x, g, w))
    out = res if emit_h else res[0]
    return out if side is None else (out, landed)


def mm_res(a, w, res, alpha, name, side=None):
    m, k = a.shape
    n = w.shape[1]
    tm, tn = _tile(m, 1024, 8), _tile(n, 1024)

    def body(a_ref, w_ref, r_ref, o_ref):
        o_ref[...] = r_ref[...] + alpha * _dot(a_ref[...], w_ref[...])

    out, landed = _call(
        body, side, name, (m // tm, n // tn),
        [pl.BlockSpec((tm, k), lambda i, j: (i, 0)), pl.BlockSpec((k, tn), lambda i, j: (0, j)),
         pl.BlockSpec((tm, tn), lambda i, j: (i, j))],
        [pl.BlockSpec((tm, tn), lambda i, j: (i, j))], [jax.ShapeDtypeStruct((m, n), F32)], [],
        ("arbitrary", "arbitrary"), (a, w, res))
    return out[0] if side is None else (out[0], landed)


def mm_nt(x, w, alpha, name):
    m, d = x.shape
    n = w.shape[0]
    tm, tn = _tile(m, 1024, 8), _tile(n, 1408)

    def body(x_ref, w_ref, o_ref, xb_ref):
        @pl.when(pl.program_id(1) == 0)
        def _():
            xb_ref[...] = x_ref[...].astype(BF16)

        o_ref[...] = (alpha * _dot(xb_ref[...], w_ref[...], NT)).astype(o_ref.dtype)

    return pl.pallas_call(
        body, name=name, grid=(m // tm, n // tn),
        in_specs=[pl.BlockSpec((tm, d), lambda i, j: (i, 0)), pl.BlockSpec((tn, d), lambda i, j: (j, 0))],
        out_specs=pl.BlockSpec((tm, tn), lambda i, j: (i, j)),
        out_shape=jax.ShapeDtypeStruct((m, n), BF16),
        scratch_shapes=[pltpu.VMEM((tm, d), BF16)],
        compiler_params=_params("arbitrary", "arbitrary"),
    )(x, w)


def mm_tn(a, b, alpha, name, ta_target=1024, tb_target=512, side=None):
    s, ka = a.shape
    nb = b.shape[1]
    ta, tb, ts = _tile(ka, ta_target), _tile(nb, tb_target), _tile(s, 1024, 16)
    steps = s // ts

    def body(a_ref, b_ref, o_ref, acc_ref):
        t = pl.program_id(2)

        @pl.when(t == 0)
        def _():
            acc_ref[...] = jnp.zeros_like(acc_ref)

        acc_ref[...] += _dot(a_ref[...].astype(BF16), b_ref[...].astype(BF16), TN)

        @pl.when(t == steps - 1)
        def _():
            o_ref[...] = (alpha * acc_ref[...]).astype(o_ref.dtype)

    res, landed = _call(
        body, side, name, (ka // ta, nb // tb, steps),
        [pl.BlockSpec((ts, ta), lambda i, j, t: (t, i)), pl.BlockSpec((ts, tb), lambda i, j, t: (t, j))],
        [pl.BlockSpec((ta, tb), lambda i, j, t: (i, j))], [jax.ShapeDtypeStruct((ka, nb), BF16)],
        [pltpu.VMEM((ta, tb), F32)], ("arbitrary", "arbitrary", "arbitrary"), (a, b))
    return res[0] if side is None else (res[0], landed)


def mm_nt_normbwd(dy, w, x, g, res, name, side=None, w_rows=False):
    m, n = dy.shape
    d = w.shape[1] if w_rows else w.shape[0]
    tm, tk = _tile(m, 1024, 8), _tile(n, 1408)
    steps = n // tk
    has_res = res is not None

    def body(*refs):
        if has_res:
            dy_ref, w_ref, x_ref, g_ref, r_ref, dx_ref, dg_ref, h_ref, acc_ref = refs
        else:
            dy_ref, w_ref, x_ref, g_ref, dx_ref, dg_ref, h_ref, acc_ref = refs
        i, t = pl.program_id(0), pl.program_id(1)

        @pl.when(t == 0)
        def _():
            acc_ref[...] = jnp.zeros_like(acc_ref)

        @pl.when((t == 0) & (i == 0))
        def _():
            dg_ref[...] = jnp.zeros_like(dg_ref)

        acc_ref[...] += _dot(dy_ref[...], w_ref[...], None if w_rows else NT)

        @pl.when(t == steps - 1)
        def _():
            xf = x_ref[...]
            r = _rms(xf)
            xhat = xf * r
            dh = acc_ref[...]
            gain = g_ref[...]
            dg_ref[...] += jnp.sum(dh * xhat, axis=0, keepdims=True)
            dxhat = dh * gain
            dx = r * (dxhat - xhat * jnp.mean(dxhat * xhat, axis=-1, keepdims=True))
            dx_ref[...] = (r_ref[...] + dx) if has_res else dx
            h_ref[...] = (xhat * gain).astype(BF16)

    row = lambda i, t: (i, 0)
    in_specs = [pl.BlockSpec((tm, tk), lambda i, t: (i, t)),
                pl.BlockSpec((tk, d), lambda i, t: (t, 0)) if w_rows else pl.BlockSpec((d, tk), lambda i, t: (0, t)),
                pl.BlockSpec((tm, d), row), pl.BlockSpec((1, d), lambda i, t: (0, 0))]
    args = [dy, w, x, g]
    if has_res:
        in_specs.append(pl.BlockSpec((tm, d), row))
        args.append(res)
    res, landed = _call(
        body, side, name, (m // tm, steps), in_specs,
        [pl.BlockSpec((tm, d), row), pl.BlockSpec((1, d), lambda i, t: (0, 0)), pl.BlockSpec((tm, d), row)],
        [jax.ShapeDtypeStruct((m, d), F32), jax.ShapeDtypeStruct((1, d), F32), jax.ShapeDtypeStruct((m, d), BF16)],
        [pltpu.VMEM((tm, d), F32)], ("arbitrary", "arbitrary"), args)
    return res if side is None else (res, landed)


def _sigmoid(z):
    return 1.0 / (1.0 + jnp.exp(-z))


def _swiglu(gate_b, up_b):
    gate = gate_b.astype(F32)
    return (gate * _sigmoid(gate) * up_b.astype(F32)).astype(BF16)


def swiglu_mm_res(gu, w, res, alpha, name, side=None):
    m, f2 = gu.shape
    f, n = w.shape
    tm, tc = _tile(m, 256, 16), _tile(f, 256)

    def body(gu_ref, w_ref, r_ref, o_ref):
        acc = jnp.zeros((tm, n), F32)
        for c0 in range(0, f, tc):
            act = _swiglu(gu_ref[:, c0:c0 + tc], gu_ref[:, f + c0:f + c0 + tc])
            acc = acc + _dot(act, w_ref[c0:c0 + tc, :])
        o_ref[...] = r_ref[...] + alpha * acc

    out, landed = _call(
        body, side, name, (m // tm,),
        [pl.BlockSpec((tm, f2), lambda i: (i, 0)), pl.BlockSpec((f, n), lambda i: (0, 0)),
         pl.BlockSpec((tm, n), lambda i: (i, 0))],
        [pl.BlockSpec((tm, n), lambda i: (i, 0))], [jax.ShapeDtypeStruct((m, n), F32)], [], ("arbitrary",), (gu, w, res))
    return out[0] if side is None else (out[0], landed)


def swiglu_mm_tn(gu, b, alpha, name, side=None):
    s, f2 = gu.shape
    f, n = f2 // 2, b.shape[1]
    ta, ts = _tile(f, 1408), _tile(s, 512, 16)
    steps, half = s // ts, f // ta

    def body(g_ref, u_ref, b_ref, o_ref, acc_ref):
        t = pl.program_id(1)

        @pl.when(t == 0)
        def _():
            acc_ref[...] = jnp.zeros_like(acc_ref)

        bb = b_ref[...].astype(BF16)
        for c0 in range(0, ta, LANES):
            acc_ref[c0:c0 + LANES, :] += _dot(_swiglu(g_ref[:, c0:c0 + LANES], u_ref[:, c0:c0 + LANES]), bb, TN)

        @pl.when(t == steps - 1)
        def _():
            o_ref[...] = (alpha * acc_ref[...]).astype(o_ref.dtype)

    res, landed = _call(
        body, side, name, (half, steps),
        [pl.BlockSpec((ts, ta), lambda i, t: (t, i)), pl.BlockSpec((ts, ta), lambda i, t: (t, half + i)),
         pl.BlockSpec((ts, n), lambda i, t: (t, 0))],
        [pl.BlockSpec((ta, n), lambda i, t: (i, 0))], [jax.ShapeDtypeStruct((f, n), BF16)],
        [pltpu.VMEM((ta, n), F32)], ("arbitrary", "arbitrary"), (gu, gu, b))
    return res[0] if side is None else (res[0], landed)


def mm_nt_swiglu_bwd(x, w, gu, alpha, name, side=None):
    m, d = x.shape
    f = w.shape[0]
    tm, tc = _tile(m, 256, 16), _tile(f, 256)

    def body(x_ref, w_ref, gu_ref, o_ref):
        xb = x_ref[...].astype(BF16)
        for c0 in range(0, f, tc):
            d_act = alpha * _dot(xb, w_ref[c0:c0 + tc, :], NT)
            gate, up = gu_ref[:, c0:c0 + tc].astype(F32), gu_ref[:, f + c0:f + c0 + tc].astype(F32)
            sg = _sigmoid(gate)
            o_ref[:, c0:c0 + tc] = (d_act * up * (sg * (1.0 + gate * (1.0 - sg)))).astype(BF16)
            o_ref[:, f + c0:f + c0 + tc] = (d_act * (gate * sg)).astype(BF16)

    res, landed = _call(
        body, side, name, (m // tm,),
        [pl.BlockSpec((tm, d), lambda i: (i, 0)), pl.BlockSpec((f, d), lambda i: (0, 0)),
         pl.BlockSpec((tm, 2 * f), lambda i: (i, 0))],
        [pl.BlockSpec((tm, 2 * f), lambda i: (i, 0))], [jax.ShapeDtypeStruct((m, 2 * f), BF16)], [], ("arbitrary",),
        (x, w, gu))
    return res[0] if side is None else (res[0], landed)


def _gelu(x):
    return 0.5 * x * (1.0 + jnp.tanh(GELU_C * (x + GELU_A * x * x * x)))


def _gelu_grad(x):
    t = jnp.tanh(GELU_C * (x + GELU_A * x * x * x))
    return 0.5 * (1.0 + t) + 0.5 * x * (1.0 - t * t) * (GELU_C * (1.0 + 3.0 * GELU_A * x * x))


def _chunk_mask():
    row = lax.broadcasted_iota(jnp.int32, (GM_P, GM_P), 0)
    col = lax.broadcasted_iota(jnp.int32, (GM_P, GM_P), 1)
    return (col < GM_P // 2) | (row >= GM_P // 2)


def gmlp_fwd(proj, gain, w_s, bias, name):
    s, pw = proj.shape
    tm = _tile(s, 256, GM_P)

    def body(p_ref, gain_ref, w_ref, b_ref, o_ref):
        mask = _chunk_mask()
        u = _gelu(p_ref[:, :GM_W])
        v = _gelu(p_ref[:, GM_W:2 * GM_W])
        vn = ((v * _rms(v)) * gain_ref[...]).astype(BF16)
        for g in range(GM_GROUPS):
            wg = jnp.where(mask, w_ref[g], 0.0).astype(BF16)
            cols = slice(g * GM_P, (g + 1) * GM_P)
            for n in range(tm // GM_P):
                rows = slice(n * GM_P, (n + 1) * GM_P)
                mixed = _dot(wg, vn[rows, cols]) + b_ref[:, cols]
                o_ref[rows, cols] = (u[rows, cols] * mixed).astype(BF16)

    return pl.pallas_call(
        body, name=name, grid=(s // tm,),
        in_specs=[pl.BlockSpec((tm, pw), lambda i: (i, 0)), pl.BlockSpec((1, GM_W), lambda i: (0, 0)),
                  pl.BlockSpec((GM_GROUPS, GM_P, GM_P), lambda i: (0, 0, 0)), pl.BlockSpec((GM_P, GM_W), lambda i: (0, 0))],
        out_specs=pl.BlockSpec((tm, GM_W), lambda i: (i, 0)),
        out_shape=jax.ShapeDtypeStruct((s, GM_W + MEM_W), BF16), compiler_params=_params("arbitrary"),
    )(proj, gain, w_s, bias)


def gmlp_bwd(proj, dy, gain, w_s, bias, name):
    s, pw = proj.shape
    dw_total = dy.shape[1]
    tm = _tile(s, 256, GM_P)

    def body(p_ref, dy_ref, gain_ref, w_ref, b_ref, dp_ref, dw_ref, db_ref, dgain_ref, dvn_ref):
        @pl.when(pl.program_id(0) == 0)
        def _():
            dw_ref[...] = jnp.zeros_like(dw_ref)
            db_ref[...] = jnp.zeros_like(db_ref)
            dgain_ref[...] = jnp.zeros_like(dgain_ref)

        mask = _chunk_mask()
        pu = p_ref[:, :GM_W]
        pv = p_ref[:, GM_W:2 * GM_W]
        u = _gelu(pu)
        v = _gelu(pv)
        r = _rms(v)
        vhat = v * r
        gain = gain_ref[...]
        vn = (vhat * gain).astype(BF16)
        gu_grad = _gelu_grad(pu)
        for g in range(GM_GROUPS):
            wg = jnp.where(mask, w_ref[g], 0.0).astype(BF16)
            cols = slice(g * GM_P, (g + 1) * GM_P)
            dw_acc = jnp.zeros((GM_P, GM_P), F32)
            db_acc = jnp.zeros((GM_P, 1), F32)
            for n in range(tm // GM_P):
                rows = slice(n * GM_P, (n + 1) * GM_P)
                dyb = dy_ref[rows, cols].astype(F32)
                vnb = vn[rows, cols]
                mixed = _dot(wg, vnb) + b_ref[:, cols]
                dmixed = dyb * u[rows, cols]
                dmb = dmixed.astype(BF16)
                dp_ref[rows, cols] = (dyb * mixed * gu_grad[rows, cols]).astype(BF16)
                dw_acc = dw_acc + _dot(dmb, vnb, NT)
                db_acc = db_acc + jnp.sum(dmixed, axis=1, keepdims=True)
                dvn_ref[rows, cols] = _dot(wg, dmb, TN)
            dw_ref[g] += jnp.where(mask, dw_acc, 0.0)
            db_ref[g] += jnp.broadcast_to(db_acc, (GM_P, GM_P))
        dvn = dvn_ref[...]
        dgain_ref[...] += jnp.sum(dvn * vhat, axis=0, keepdims=True)
        dvhat = dvn * gain
        dv = r * (dvhat - vhat * jnp.mean(dvhat * vhat, axis=-1, keepdims=True))
        dp_ref[:, GM_W:] = (dv * _gelu_grad(pv)).astype(BF16)

    const3 = lambda i: (0, 0, 0)
    return pl.pallas_call(
        body, name=name, grid=(s // tm,),
        in_specs=[pl.BlockSpec((tm, pw), lambda i: (i, 0)), pl.BlockSpec((tm, dw_total), lambda i: (i, 0)),
                  pl.BlockSpec((1, GM_W), lambda i: (0, 0)), pl.BlockSpec((GM_GROUPS, GM_P, GM_P), const3),
                  pl.BlockSpec((GM_P, GM_W), lambda i: (0, 0))],
        out_specs=[pl.BlockSpec((tm, 2 * GM_W), lambda i: (i, 0)), pl.BlockSpec((GM_GROUPS, GM_P, GM_P), const3),
                   pl.BlockSpec((GM_GROUPS, GM_P, GM_P), const3), pl.BlockSpec((1, GM_W), lambda i: (0, 0))],
        out_shape=[jax.ShapeDtypeStruct((s, pw), BF16), jax.ShapeDtypeStruct((GM_GROUPS, GM_P, GM_P), F32),
                   jax.ShapeDtypeStruct((GM_GROUPS, GM_P, GM_P), F32), jax.ShapeDtypeStruct((1, GM_W), F32)],
        scratch_shapes=[pltpu.VMEM((tm, GM_W), F32)],
        compiler_params=_params("arbitrary"),
    )(proj, dy, gain, w_s, bias)


def _keep(mask, xb):
    return jnp.where(mask, xb.astype(F32), 0.0).astype(BF16)


def _head_masks(rows, width, heads):
    lane = lax.broadcasted_iota(jnp.int32, (rows, width), 1)
    return [(lane >= HEAD_DIM * h) & (lane < HEAD_DIM * (h + 1)) for h in range(heads)]


def _mem_probs(qh, k):
    sc = _dot(qh, k, NT) * QK_SCALE
    e = jnp.exp(sc - jnp.max(sc, axis=-1, keepdims=True))
    return e / jnp.sum(e, axis=-1, keepdims=True)


def mem_fwd(proj, q_blk, mem_kv, layer, into, into_blk, name):
    s = proj.shape[0]
    n_mem = mem_kv.shape[0]
    tm = _tile(s, 512, 16)

    def body(q_ref, k_ref, v_ref, into_ref, o_ref):
        q = q_ref[...].astype(BF16)
        k, v = k_ref[...], v_ref[...]
        out = jnp.zeros((tm, MEM_W), F32)
        for hm in _head_masks(tm, MEM_W, MEM_HEADS):
            p = _mem_probs(_keep(hm, q), k)
            out = out + jnp.where(hm, _dot(p.astype(BF16), v), 0.0)
        o_ref[...] = out.astype(BF16)

    return pl.pallas_call(
        body, name=name, grid=(s // tm,),
        in_specs=[pl.BlockSpec((tm, MEM_W), lambda i: (i, q_blk)), pl.BlockSpec((n_mem, MEM_W), lambda i: (0, 2 * layer)),
                  pl.BlockSpec((n_mem, MEM_W), lambda i: (0, 2 * layer + 1)), pl.BlockSpec(memory_space=pl.ANY)],
        out_specs=pl.BlockSpec((tm, MEM_W), lambda i: (i, into_blk)),
        out_shape=jax.ShapeDtypeStruct(into.shape, BF16), input_output_aliases={3: 0},
        compiler_params=_params("arbitrary"),
    )(proj, mem_kv, mem_kv, into)


def mem_bwd(proj, q_blk, mem_kv, layer, dy, dy_blk, into, name):
    s = proj.shape[0]
    n_mem = mem_kv.shape[0]
    tm = _tile(s, 512, 16)

    def body(q_ref, k_ref, v_ref, dy_ref, into_ref, dq_ref, dk_ref, dv_ref):
        @pl.when(pl.program_id(0) == 0)
        def _():
            dk_ref[...] = jnp.zeros_like(dk_ref)
            dv_ref[...] = jnp.zeros_like(dv_ref)

        q = q_ref[...].astype(BF16)
        k, v = k_ref[...], v_ref[...]
        dy = dy_ref[...]
        dq = jnp.zeros((tm, MEM_W), F32)
        dk = jnp.zeros((n_mem, MEM_W), F32)
        dv = jnp.zeros((n_mem, MEM_W), F32)
        for hm in _head_masks(tm, MEM_W, MEM_HEADS):
            qh = _keep(hm, q)
            dyh = _keep(hm, dy)
            p = _mem_probs(qh, k)
            dp = _dot(dyh, v, NT)
            dv = dv + _dot(p.astype(BF16), dyh, TN)
            ds = (p * (dp - jnp.sum(dp * p, axis=-1, keepdims=True)) * QK_SCALE).astype(BF16)
            dq = dq + jnp.where(hm, _dot(ds, k), 0.0)
            dk = dk + _dot(ds, qh, TN)
        dq_ref[...] = dq.astype(BF16)
        dk_ref[...] += dk
        dv_ref[...] += dv

    const = lambda i: (0, 0)
    return pl.pallas_call(
        body, name=name, grid=(s // tm,),
        in_specs=[pl.BlockSpec((tm, MEM_W), lambda i: (i, q_blk)), pl.BlockSpec((n_mem, MEM_W), lambda i: (0, 2 * layer)),
                  pl.BlockSpec((n_mem, MEM_W), lambda i: (0, 2 * layer + 1)), pl.BlockSpec((tm, MEM_W), lambda i: (i, dy_blk)),
                  pl.BlockSpec(memory_space=pl.ANY)],
        out_specs=[pl.BlockSpec((tm, MEM_W), lambda i: (i, q_blk)), pl.BlockSpec((n_mem, MEM_W), const),
                   pl.BlockSpec((n_mem, MEM_W), const)],
        out_shape=[jax.ShapeDtypeStruct(into.shape, BF16), jax.ShapeDtypeStruct((n_mem, MEM_W), F32),
                   jax.ShapeDtypeStruct((n_mem, MEM_W), F32)],
        input_output_aliases={4: 0}, compiler_params=_params("arbitrary"),
    )(proj, mem_kv, mem_kv, dy, into)


SB_KEYS = 256
SB_SUB = SB_KEYS // SB_BLK
SB_QROWS = 256
SB_QB = SB_QROWS // SB_BLK
SB_CHAINS = 2 * SB_QB
SB_DEAD = -110.0


def _split(xf):
    hi = xf.astype(BF16)
    return hi, (xf - hi.astype(F32)).astype(BF16)


def _sb_consts():
    row = lax.bitwise_and(lax.broadcasted_iota(jnp.int32, (2 * SB_BLK, 2 * SB_BLK), 0), SB_BLK - 1)
    col = lax.broadcasted_iota(jnp.int32, (2 * SB_BLK, 2 * SB_BLK), 1)
    ones = col >= SB_BLK
    after2 = jnp.where(ones | (row > col), -1.0, 0.0).astype(BF16)
    from2 = jnp.where(ones | (row >= col), 1.0, 0.0).astype(BF16)
    r = lax.broadcasted_iota(jnp.int32, (SB_BLK, SB_BLK), 0)
    c = lax.broadcasted_iota(jnp.int32, (SB_BLK, SB_BLK), 1)
    return after2, from2, c - r, [c < HEAD_DIM, c >= HEAD_DIM]


def _suffix(xf, tri2):
    hi, lo = _split(xf)
    return _dot(jnp.concatenate([hi, lo], axis=1), tri2)


def _sb_logs(z, mask):
    softplus = jnp.maximum(z, 0.0) + jnp.log(1.0 + jnp.exp(-jnp.abs(z)))
    log_beta = z - softplus
    if mask is not None:
        softplus = jnp.where(mask, softplus, 0.0)
    return softplus, log_beta


def _sb_queries(q_ref, heads):
    q = q_ref[...].astype(F32) * QK_SCALE
    return [jnp.where(hm, q[r * SB_BLK:(r + 1) * SB_BLK], 0.0).astype(BF16) for r in range(SB_QB) for hm in heads]


def _sb_walk(i, block, state):
    places = SB_SUB // SB_QB
    assert places in (1, 2)
    own = lax.shift_right_logical(i * SB_QB, SB_SUB.bit_length() - 1)
    firsts = [[(v * SB_QB + r) * SB_BLK for r in range(SB_QB) for _ in range(2)] for v in range(places)]
    if places == 1:
        state = block(own, state, firsts[0])
    else:
        state = lax.cond(lax.bitwise_and(i, 1) == 0, lambda st: block(own, st, firsts[0]),
                         lambda st: block(own, st, firsts[1]), state)

    def live(carry):
        j, st = carry
        most = st[0][0]
        for run in st[0][1:]:
            most = jnp.maximum(most, run)
        return (j >= 0) & (jnp.max(most) > SB_DEAD)

    return lax.while_loop(live, lambda carry: (carry[0] - 1, block(carry[0], carry[1], None)), (own - 1, state))[1]


def _sb_tiles(first):
    out = []
    for c in reversed(range(SB_SUB)):
        for n in range(SB_CHAINS):
            if first is None or c * SB_BLK < first[n]:
                out.append((c, n, "before"))
            elif c * SB_BLK == first[n]:
                out.append((c, n, "diagonal"))
    return out


def _sb_heads_apart(stacked, heads, r):
    return jnp.where(heads[0], stacked[2 * r * SB_BLK:(2 * r + 1) * SB_BLK],
                     stacked[(2 * r + 1) * SB_BLK:(2 * r + 2) * SB_BLK])


def sb_fwd(proj, kv, name):
    s = proj.shape[0]
    assert s % SB_KEYS == 0 and SB_KEYS % SB_QROWS == 0

    def body(q_ref, k_ref, v_ref, o_ref, y_ref):
        after2, _, col_minus_row, heads = _sb_consts()
        q_all = jnp.concatenate(_sb_queries(q_ref, heads), axis=0)
        key_before_query = col_minus_row < 0

        def block(j, state, first):
            runs, acc = list(state[0]), state[1]
            rows = pl.ds(pl.multiple_of(j * SB_KEYS, SB_KEYS), SB_KEYS)
            kb, vb = k_ref[rows, :], v_ref[rows, :]
            z = _dot(q_all, kb, NT)
            pend = {}
            parts = [[jnp.zeros((SB_BLK, SB_BLK), BF16)] * SB_SUB for _ in range(SB_CHAINS)]
            for c, n, where in _sb_tiles(first):
                mask = key_before_query if where == "diagonal" else None
                softplus, lb = _sb_logs(z[n * SB_BLK:(n + 1) * SB_BLK, c * SB_BLK:(c + 1) * SB_BLK], mask)
                pend[c, n] = (lb, _suffix(softplus, after2), mask)
            for c, n, _ in _sb_tiles(first):
                lb, r, mask = pend.pop((c, n))
                a = jnp.exp(lb + r[:, :SB_BLK] + runs[n])
                if mask is not None:
                    a = jnp.where(mask, a, 0.0)
                parts[n][c] = a.astype(BF16)
                runs[n] = runs[n] + r[:, SB_BLK:]
            a_all = jnp.concatenate([jnp.concatenate(p, axis=1) for p in parts], axis=0)
            return tuple(runs), acc + _dot(a_all, vb)

        zero = jnp.zeros((SB_BLK, LANES), F32)
        state = _sb_walk(pl.program_id(1), block, ((zero,) * SB_CHAINS, jnp.zeros((SB_CHAINS * SB_BLK, LANES), F32)))
        for r in range(SB_QB):
            out = _sb_heads_apart(state[1], heads, r)
            o_ref[r * SB_BLK:(r + 1) * SB_BLK, :] = out
            y_ref[r * SB_BLK:(r + 1) * SB_BLK, :] = out.astype(BF16)

    pairs = SB_W // LANES
    block_spec = pl.BlockSpec((SB_QROWS, LANES), lambda p, i: (i, p))
    return pl.pallas_call(
        body, name=name, grid=(pairs, s // SB_QROWS),
        in_specs=[block_spec, pl.BlockSpec((s, LANES), lambda p, i: (0, p)),
                  pl.BlockSpec((s, LANES), lambda p, i: (0, pairs + p))],
        out_specs=[block_spec, block_spec],
        out_shape=[jax.ShapeDtypeStruct((s, SB_W), F32), jax.ShapeDtypeStruct((s, SB_W + MEM_W), BF16)],
        compiler_params=_params("arbitrary", "arbitrary"),
    )(proj, kv, kv)


def sb_bwd(proj, kv, out, dy, name):
    s = proj.shape[0]

    def body(q_ref, k_ref, v_ref, o_ref, do_ref, dq_ref, dk_ref, dv_ref):
        i = pl.program_id(1)

        @pl.when(i == 0)
        def _():
            dk_ref[...] = jnp.zeros_like(dk_ref)
            dv_ref[...] = jnp.zeros_like(dv_ref)

        after2, from2, col_minus_row, heads = _sb_consts()
        q_all = jnp.concatenate(_sb_queries(q_ref, heads), axis=0)
        key_before_query = col_minus_row < 0
        d_out = do_ref[...].astype(F32)
        prod = d_out * o_ref[...]
        dos, totals = [], []
        for r in range(SB_QB):
            rr = slice(r * SB_BLK, (r + 1) * SB_BLK)
            for hm in heads:
                dos.append(jnp.where(hm, d_out[rr], 0.0).astype(BF16))
                totals.append(jnp.broadcast_to(jnp.sum(jnp.where(hm, prod[rr], 0.0), axis=1, keepdims=True),
                                               (SB_BLK, SB_BLK)))
        do_all = jnp.concatenate(dos, axis=0)

        def block(j, state, first):
            runs, seens, dq = list(state[0]), list(state[1]), state[2]
            rows = pl.ds(pl.multiple_of(j * SB_KEYS, SB_KEYS), SB_KEYS)
            kb, vb = k_ref[rows, :], v_ref[rows, :]
            z = _dot(q_all, kb, NT)
            da = _dot(do_all, vb, NT)
            pend, pend2 = {}, {}
            a_parts = [[jnp.zeros((SB_BLK, SB_BLK), BF16)] * SB_SUB for _ in range(SB_CHAINS)]
            dz_parts = [[jnp.zeros((SB_BLK, SB_BLK), BF16)] * SB_SUB for _ in range(SB_CHAINS)]
            for c, n, where in _sb_tiles(first):
                mask = key_before_query if where == "diagonal" else None
                softplus, lb = _sb_logs(z[n * SB_BLK:(n + 1) * SB_BLK, c * SB_BLK:(c + 1) * SB_BLK], mask)
                pend[c, n] = (softplus, lb, _suffix(softplus, after2), mask)
            for c, n, _ in _sb_tiles(first):
                softplus, lb, r, mask = pend.pop((c, n))
                a = jnp.exp(lb + r[:, :SB_BLK] + runs[n])
                if mask is not None:
                    a = jnp.where(mask, a, 0.0)
                runs[n] = runs[n] + r[:, SB_BLK:]
                ab = a.astype(BF16)
                a_parts[n][c] = ab
                dl = ab.astype(F32) * da[n * SB_BLK:(n + 1) * SB_BLK, c * SB_BLK:(c + 1) * SB_BLK]
                pend2[c, n] = (softplus, lb, dl, _suffix(dl, from2), mask)
            for c, n, _ in _sb_tiles(first):
                softplus, lb, dl, r2, mask = pend2.pop((c, n))
                d_lom = totals[n] - (r2[:, :SB_BLK] + seens[n])
                if mask is not None:
                    d_lom = jnp.where(mask, d_lom, 0.0)
                seens[n] = seens[n] + r2[:, SB_BLK:]
                dz_parts[n][c] = (dl * jnp.exp(-softplus) - d_lom * jnp.exp(lb)).astype(BF16)
            a_all = jnp.concatenate([jnp.concatenate(p, axis=1) for p in a_parts], axis=0)
            dz_all = jnp.concatenate([jnp.concatenate(p, axis=1) for p in dz_parts], axis=0)
            dv_ref[rows, :] += _dot(a_all, do_all, TN)
            dk_ref[rows, :] += _dot(dz_all, q_all, TN)
            return tuple(runs), tuple(seens), dq + _dot(dz_all, kb)

        zero = jnp.zeros((SB_BLK, LANES), F32)
        state = _sb_walk(i, block, ((zero,) * SB_CHAINS, (zero,) * SB_CHAINS,
                                    jnp.zeros((SB_CHAINS * SB_BLK, LANES), F32)))
        for r in range(SB_QB):
            dq_ref[r * SB_BLK:(r + 1) * SB_BLK, :] = (_sb_heads_apart(state[2], heads, r) * QK_SCALE).astype(BF16)

    pairs = SB_W // LANES
    blk = lambda p, i: (i, p)
    col = lambda p, i: (0, p)
    return pl.pallas_call(
        body, name=name, grid=(pairs, s // SB_QROWS),
        in_specs=[pl.BlockSpec((SB_QROWS, LANES), blk), pl.BlockSpec((s, LANES), col),
                  pl.BlockSpec((s, LANES), lambda p, i: (0, pairs + p)), pl.BlockSpec((SB_QROWS, LANES), blk),
                  pl.BlockSpec((SB_QROWS, LANES), blk)],
        out_specs=[pl.BlockSpec((SB_QROWS, LANES), blk), pl.BlockSpec((s, LANES), col), pl.BlockSpec((s, LANES), col)],
        out_shape=[jax.ShapeDtypeStruct((s, SB_W + MEM_W), BF16), jax.ShapeDtypeStruct((s, SB_W), F32),
                   jax.ShapeDtypeStruct((s, SB_W), F32)],
        compiler_params=_params("arbitrary", "arbitrary"),
    )(proj, kv, kv, out, dy)


def final_loss(x, g, target, name):
    s, d = x.shape
    tm = _tile(s, 256, 8)

    def body(x_ref, g_ref, t_ref, loss_ref, dx_ref, dg_ref):
        @pl.when(pl.program_id(0) == 0)
        def _():
            loss_ref[...] = jnp.zeros_like(loss_ref)
            dg_ref[...] = jnp.zeros_like(dg_ref)

        xf = x_ref[...]
        r = _rms(xf)
        xhat = xf * r
        gain = g_ref[...]
        diff = xhat * gain - t_ref[...]
        sq = jnp.sum(jnp.sum(diff * diff, axis=1, keepdims=True), axis=0, keepdims=True)
        loss_ref[...] += jnp.broadcast_to(sq, loss_ref.shape)
        dy = diff * (1.0 / d)
        dg_ref[...] += jnp.sum(dy * xhat, axis=0, keepdims=True)
        dxhat = dy * gain
        dx_ref[...] = r * (dxhat - xhat * jnp.mean(dxhat * xhat, axis=-1, keepdims=True))

    row = lambda i: (i, 0)
    const = lambda i: (0, 0)
    return pl.pallas_call(
        body, name=name, grid=(s // tm,),
        in_specs=[pl.BlockSpec((tm, d), row), pl.BlockSpec((1, d), const), pl.BlockSpec((tm, d), row)],
        out_specs=[pl.BlockSpec((8, LANES), const), pl.BlockSpec((tm, d), row), pl.BlockSpec((1, d), const)],
        out_shape=[jax.ShapeDtypeStruct((8, LANES), F32), jax.ShapeDtypeStruct((s, d), F32), jax.ShapeDtypeStruct((1, d), F32)],
        compiler_params=_params("arbitrary"),
    )(x, g, target)


def adamw(w, parts, m, v, name):
    rows, cols = w.shape
    k = parts.shape[0]
    tr = _tile(rows, 512, 16)
    c1, c2 = 1.0 - ADAM_B1 ** ADAM_STEP, 1.0 - ADAM_B2 ** ADAM_STEP

    def body(w_ref, p_ref, m_ref, v_ref, g_ref, d_ref, nm_ref, nv_ref):
        grad = p_ref[0].astype(F32)
        for s in range(1, k):
            grad = grad + p_ref[s].astype(F32)
        nm = ADAM_B1 * m_ref[...] + (1.0 - ADAM_B1) * grad
        nv = ADAM_B2 * v_ref[...] + (1.0 - ADAM_B2) * (grad * grad)
        g_ref[...] = grad
        d_ref[...] = -ADAM_LR * ((nm / c1) / (jnp.sqrt(nv / c2) + ADAM_EPS) + ADAM_WD * w_ref[...])
        nm_ref[...] = nm
        nv_ref[...] = nv

    spec = pl.BlockSpec((tr, cols), lambda i: (i, 0))
    shape = jax.ShapeDtypeStruct((rows, cols), F32)
    return pl.pallas_call(
        body, name=name, grid=(rows // tr,),
        in_specs=[spec, pl.BlockSpec((k, tr, cols), lambda i: (0, i, 0)), spec, spec],
        out_specs=[spec] * 4, out_shape=[shape] * 4,
        compiler_params=_params("arbitrary"),
    )(w, parts, m, v)


SHARDED = {"ffn1_w_gate": 2, "ffn1_w_up": 2, "ffn1_w_down": 1, "ffn2_w_gate": 2, "ffn2_w_up": 2, "ffn2_w_down": 1,
           "w_mem_kv": 1, "a_w_in": 2, "a_w_out": 1, "w_kv": 1, "b_w_in": 1, "b_w_out": 1}
TRANSPOSED = ("ffn1_w_gate", "ffn1_w_up", "ffn2_w_gate", "ffn2_w_up")
SMALL = ["ffn1_norm", "mix_norm", "ffn2_norm", "mem_norm", "kv_norm", "final_norm", "a_v_norm", "a_w_spatial", "a_b_spatial"]
WEIGHTS = ["ffn1_norm", "ffn1_w_gate", "ffn1_w_up", "ffn1_w_down", "mix_norm", "ffn2_norm", "ffn2_w_gate", "ffn2_w_up",
           "ffn2_w_down", "mem_norm", "w_mem_kv", "a_w_in", "a_v_norm", "a_w_spatial", "a_b_spatial", "a_w_out", "kv_norm",
           "w_kv", "b_w_in", "b_w_out", "final_norm"]


def _all_sum(parts, name):
    flat = jnp.concatenate([p.reshape(-1) for p in parts])
    pad = (-flat.size) % (16 * LANES)
    buf = jnp.pad(flat, (0, pad)).reshape(-1, LANES)
    total = sum_leading(exchange([buf], "all", True, name)[0], F32, name + "_sum").reshape(-1)
    out, off = [], 0
    for p in parts:
        out.append(total[off:off + p.size].reshape(p.shape))
        off += p.size
    return out


def _device_index():
    return 4 * lax.axis_index("x") + 2 * lax.axis_index("y") + lax.axis_index("c")


def kernel(x, mem, ffn1_norm, ffn1_w_gate, ffn1_w_up, ffn1_w_down, mix_norm, ffn2_norm, ffn2_w_gate, ffn2_w_up, ffn2_w_down, mem_norm, w_mem_kv, a_w_in, a_v_norm, a_w_spatial, a_b_spatial, a_w_out, kv_norm, w_kv, b_w_in, b_w_out, final_norm, loss_target, m_ffn1_norm, m_ffn1_w_gate, m_ffn1_w_up, m_ffn1_w_down, m_mix_norm, m_ffn2_norm, m_ffn2_w_gate, m_ffn2_w_up, m_ffn2_w_down, m_mem_norm, m_w_mem_kv, m_a_w_in, m_a_v_norm, m_a_w_spatial, m_a_b_spatial, m_a_w_out, m_kv_norm, m_w_kv, m_b_w_in, m_b_w_out, m_final_norm, v_ffn1_norm, v_ffn1_w_gate, v_ffn1_w_up, v_ffn1_w_down, v_mix_norm, v_ffn2_norm, v_ffn2_w_gate, v_ffn2_w_up, v_ffn2_w_down, v_mem_norm, v_w_mem_kv, v_a_w_in, v_a_v_norm, v_a_w_spatial, v_a_b_spatial, v_a_w_out, v_kv_norm, v_w_kv, v_b_w_in, v_b_w_out, v_final_norm):
    weights = dict(ffn1_norm=ffn1_norm, ffn1_w_gate=ffn1_w_gate, ffn1_w_up=ffn1_w_up, ffn1_w_down=ffn1_w_down, mix_norm=mix_norm, ffn2_norm=ffn2_norm, ffn2_w_gate=ffn2_w_gate, ffn2_w_up=ffn2_w_up, ffn2_w_down=ffn2_w_down, mem_norm=mem_norm, w_mem_kv=w_mem_kv, a_w_in=a_w_in, a_v_norm=a_v_norm, a_w_spatial=a_w_spatial, a_b_spatial=a_b_spatial, a_w_out=a_w_out, kv_norm=kv_norm, w_kv=w_kv, b_w_in=b_w_in, b_w_out=b_w_out, final_norm=final_norm)
    mom1 = dict(ffn1_norm=m_ffn1_norm, ffn1_w_gate=m_ffn1_w_gate, ffn1_w_up=m_ffn1_w_up, ffn1_w_down=m_ffn1_w_down, mix_norm=m_mix_norm, ffn2_norm=m_ffn2_norm, ffn2_w_gate=m_ffn2_w_gate, ffn2_w_up=m_ffn2_w_up, ffn2_w_down=m_ffn2_w_down, mem_norm=m_mem_norm, w_mem_kv=m_w_mem_kv, a_w_in=m_a_w_in, a_v_norm=m_a_v_norm, a_w_spatial=m_a_w_spatial, a_b_spatial=m_a_b_spatial, a_w_out=m_a_w_out, kv_norm=m_kv_norm, w_kv=m_w_kv, b_w_in=m_b_w_in, b_w_out=m_b_w_out, final_norm=m_final_norm)
    mom2 = dict(ffn1_norm=v_ffn1_norm, ffn1_w_gate=v_ffn1_w_gate, ffn1_w_up=v_ffn1_w_up, ffn1_w_down=v_ffn1_w_down, mix_norm=v_mix_norm, ffn2_norm=v_ffn2_norm, ffn2_w_gate=v_ffn2_w_gate, ffn2_w_up=v_ffn2_w_up, ffn2_w_down=v_ffn2_w_down, mem_norm=v_mem_norm, w_mem_kv=v_w_mem_kv, a_w_in=v_a_w_in, a_v_norm=v_a_v_norm, a_w_spatial=v_a_w_spatial, a_b_spatial=v_a_b_spatial, a_w_out=v_a_w_out, kv_norm=v_kv_norm, w_kv=v_w_kv, b_w_in=v_b_w_in, b_w_out=v_b_w_out, final_norm=v_final_norm)

    dev = _device_index()
    xs, mem_in, target = x[0], mem[0], loss_target[0]
    d_model = xs.shape[1]
    shards = {n: weights[n] for n in SHARDED}

    def mix_keys(l):
        w_in, w_out, idx = ("a_w_in", "a_w_out", l) if l < N_A else ("b_w_in", "b_w_out", l - N_A)
        return (w_in, idx), (w_out, idx)

    def ffn_keys(ffn):
        return ([], []) if ffn is None else ([(ffn[0] + "_w_gate", ffn[1]), (ffn[0] + "_w_up", ffn[1])],
                                             [(ffn[0] + "_w_down", ffn[1])])

    def ffn_after(f, l):
        return ("ffn2", l) if f == "ffn1" else (("ffn1", l + 1) if l + 1 < DEPTH else None)

    def cut_axis(key):
        return SHARDED[key[0]] - (0 if key[1] is None else 1)

    def block(key):
        b = (shards[key[0]] if key[1] is None else shards[key[0]][key[1]]).astype(BF16)
        return jnp.swapaxes(b, 0, 1) if key[0] in TRANSPOSED else b

    def carrying(call, keys, same_src, source, store):
        if not keys:
            return call(None)
        result, arrived = call(Side([source(k) for k in keys], same_src))
        store.update(zip(keys, arrived))
        return result

    first_gu, first_down = ffn_keys(("ffn1", 0))
    first = first_gu + first_down + [mix_keys(0)[0], ("w_mem_kv", None)]
    landed = dict(zip(first, exchange([block(k) for k in first], "all", True, "gather_first")))
    assembled = {}

    def whole(n, l=None):
        if (n, l) not in assembled:
            got = landed[n, l]
            if n.endswith("_w_gate"):
                up = landed[n.replace("_w_gate", "_w_up"), l]
                assembled[n, l] = jnp.concatenate([got.reshape((-1,) + got.shape[2:]), up.reshape((-1,) + up.shape[2:])])
            elif cut_axis((n, l)) == 0:
                assembled[n, l] = got.reshape((-1,) + got.shape[2:])
            else:
                assembled[n, l] = jnp.concatenate([got[d] for d in range(N_DEV)], axis=cut_axis((n, l)))
        return assembled[n, l]

    def whole_gu(f, l):
        return whole(f + "_w_gate", l)

    vn_width = a_v_norm.shape[1]
    a_v_full = _all_sum([lax.dynamic_update_slice(jnp.zeros((N_A, N_DEV * vn_width), F32), a_v_norm, (0, dev * vn_width))],
                        "gather_v_norm")[0]
    row = lambda v: v.reshape(1, -1)
    w_mem_cat = whole("w_mem_kv").transpose(1, 0, 2).reshape(d_model, -1)
    bias = [jnp.repeat(a_b_spatial[i].T, GM_P, axis=1) for i in range(N_A)]

    mem_kv, mem_h = norm_mm(mem_in, row(mem_norm), w_mem_cat, BF16, "mem_kv", emit_h=True)

    def ffn_fwd(xin, f, l):
        keys_gu, keys_down = ffn_keys(ffn_after(f, l))
        if ffn_after(f, l) == ("ffn1", N_A):
            keys_down = keys_down + [("w_kv", None)]
        gu = carrying(lambda side: norm_mm(xin, row(weights[f + "_norm"][l]), whole_gu(f, l), BF16, "ffn_gu", side=side,
                                             w_rows=True),
                      keys_gu, True, block, landed)
        out = carrying(lambda side: swiglu_mm_res(gu, whole(f + "_w_down", l), xin, 0.5, "ffn_down", side=side),
                       keys_down, True, block, landed)
        return out, gu

    saved = []
    kv = x_kv = None
    cur = xs
    for l in range(DEPTH):
        st = {"x0": cur}
        if l == N_A:
            x_kv = cur
            kv = norm_mm(cur, row(kv_norm), whole("w_kv"), BF16, "kv_proj")
        st["x1"], st["gu1"] = ffn_fwd(cur, "ffn1", l)
        key_in, key_out = mix_keys(l)
        proj = carrying(lambda side: norm_mm(st["x1"], row(mix_norm[l]), whole(*key_in), F32 if l < N_A else BF16,
                                             "a_proj" if l < N_A else "b_proj", side=side), [key_out], True, block, landed)
        if l < N_A:
            y_tok = gmlp_fwd(proj, row(a_v_full[l]), a_w_spatial[l], bias[l], "gmlp_fwd")
            st["y"] = mem_fwd(proj, 2 * GM_W // MEM_W, mem_kv, l, y_tok, GM_W // MEM_W, "mem_fwd_a")
        else:
            st["sb_out"], y_tok = sb_fwd(proj, kv, "sb_fwd")
            st["y"] = mem_fwd(proj, SB_W // MEM_W, mem_kv, l, y_tok, SB_W // MEM_W, "mem_fwd_b")
        st["proj"] = proj
        st["x2"] = carrying(lambda side: mm_res(st["y"], whole(*key_out), st["x1"], 1.0, "mix_out", side=side),
                            [mix_keys(l + 1)[0]] if l + 1 < DEPTH else [], True, block, landed)
        cur, st["gu2"] = ffn_fwd(st["x2"], "ffn2", l)
        saved.append(st)

    loss_blk, dx, d_final = final_loss(cur, row(final_norm), target, "final_loss")
    loss = lax.psum(loss_blk[0, 0] * (0.5 / d_model), AXES)

    grads = {n: [None] * weights[n].shape[0] for n in WEIGHTS if weights[n].ndim >= 2 and n not in ("w_kv",)}
    grads["final_norm"] = d_final.reshape(-1)
    d_mem_kv = [None] * DEPTH
    d_kv = []

    summed = {}

    def pieces(key):
        g = (grads[key[0]] if key[1] is None else grads[key[0]][key[1]]).astype(BF16)
        axis = 0 if key[0] in TRANSPOSED else cut_axis(key)
        cut = g.reshape(g.shape[:axis] + (N_DEV, g.shape[axis] // N_DEV) + g.shape[axis + 1:])
        return jnp.moveaxis(cut, axis, 0)

    def ffn_bwd(dx, xin, gu, f, l):
        keys_gu, keys_down = ffn_keys(ffn_after(f, l))
        keys_mix = list(mix_keys(l)) if f == "ffn1" else ([("w_kv", None)] if l + 1 == N_A else [])
        d_gu = carrying(lambda side: mm_nt_swiglu_bwd(dx, whole(f + "_w_down", l), gu, 0.5, "ffn_dgu", side=side),
                        keys_down, False, pieces, summed)
        dx_new, d_gain, h = carrying(
            lambda side: mm_nt_normbwd(d_gu, whole_gu(f, l), xin, row(weights[f + "_norm"][l]), dx, "ffn_dx", side=side,
                                       w_rows=True),
            keys_gu, False, pieces, summed)
        d_wgu_t = carrying(lambda side: mm_tn(d_gu, h, 1.0, "ffn_dwgu", ta_target=1408, tb_target=1024, side=side),
                           keys_mix, False, pieces, summed)
        d_wdown = swiglu_mm_tn(gu, dx, 0.5, "ffn_dwdown")
        half = d_wgu_t.shape[0] // 2
        grads[f + "_w_gate"][l], grads[f + "_w_up"][l] = d_wgu_t[:half], d_wgu_t[half:]
        grads[f + "_w_down"][l] = d_wdown
        grads[f + "_norm"][l] = d_gain.reshape(-1)
        return dx_new

    for l in reversed(range(DEPTH)):
        st = saved[l]
        dx = ffn_bwd(dx, st["x2"], st["gu2"], "ffn2", l)
        proj = st["proj"]
        (key_in, idx), (key_out, _) = mix_keys(l)
        w_in, w_out = whole(key_in, idx), whole(key_out, idx)
        dy = mm_nt(dx, w_out, 1.0, "mix_dy")
        grads[key_out][idx] = mm_tn(st["y"], dx, 1.0, "mix_dwout", tb_target=1024)
        if l < N_A:
            d_uv, d_ws, d_bs, d_vgain = gmlp_bwd(proj, dy, row(a_v_full[l]), a_w_spatial[l], bias[l], "gmlp_bwd")
            grads["a_w_spatial"][l], grads["a_b_spatial"][l], grads["a_v_norm"][l] = d_ws, d_bs[:, :, 0], d_vgain.reshape(-1)
            d_proj, d_k, d_v = mem_bwd(proj, 2 * GM_W // MEM_W, mem_kv, l, dy, GM_W // MEM_W, d_uv, "mem_bwd_a")
        else:
            d_qsb, d_ksb, d_vsb = sb_bwd(proj, kv, st["sb_out"], dy, "sb_bwd")
            d_kv.append(jnp.concatenate([d_ksb, d_vsb], axis=1))
            d_proj, d_k, d_v = mem_bwd(proj, SB_W // MEM_W, mem_kv, l, dy, SB_W // MEM_W, d_qsb, "mem_bwd_b")
        d_mem_kv[l] = jnp.concatenate([d_k, d_v], axis=1)
        dx, d_gain, h = mm_nt_normbwd(d_proj, w_in, st["x1"], row(mix_norm[l]), dx, "mix_dx")
        grads["mix_norm"][l] = d_gain.reshape(-1)
        grads[key_in][idx] = mm_tn(h, d_proj, 1.0, "mix_dwin")
        dx = ffn_bwd(dx, st["x0"], st["gu1"], "ffn1", l)
        if l == N_A:
            d_kv_b = sum_leading(jnp.stack(d_kv), BF16, "kv_dsum")
            dx, d_gain, h = mm_nt_normbwd(d_kv_b, whole("w_kv"), x_kv, row(kv_norm), dx, "kv_dx")
            grads["kv_norm"] = d_gain.reshape(-1)
            grads["w_kv"] = mm_tn(h, d_kv_b, 1.0, "kv_dw")

    d_mem_all = jnp.concatenate(d_mem_kv, axis=1).astype(BF16)
    _, d_gain, _ = mm_nt_normbwd(d_mem_all, w_mem_cat, mem_in, row(mem_norm), None, "mem_dnorm")
    grads["mem_norm"] = d_gain.reshape(-1)
    d_wmem = mm_tn(mem_h, d_mem_all, 1.0, "mem_dw")
    grads["w_mem_kv"] = d_wmem.reshape(d_model, DEPTH, -1).transpose(1, 0, 2)

    last = first_gu + first_down + [("w_mem_kv", None)]
    summed.update(zip(last, exchange([pieces(k) for k in last], "all", False, "scatter_last")))
    parts = {n: summed[n, None] if (n, None) in summed else
             jnp.stack([summed[n, i] for i in range(weights[n].shape[0])], axis=1) for n in SHARDED}
    grads = {n: (jnp.stack(g) if isinstance(g, list) else g) for n, g in grads.items()}
    for n, g in zip(SMALL, _all_sum([grads[n] for n in SMALL], "sum_small")):
        parts[n] = g[None]
    parts["a_v_norm"] = lax.dynamic_slice(parts["a_v_norm"], (0, 0, dev * vn_width), (1,) + a_v_norm.shape)

    reduced, deltas, new_m, new_v = {}, {}, {}, {}
    for n in WEIGHTS:
        turn = (lambda a: jnp.swapaxes(a, -1, -2)) if n in TRANSPOSED else (lambda a: a)
        w = turn(weights[n])
        view = (lambda a: turn(a).reshape(-1, w.shape[-1]))
        res = adamw(view(weights[n]), parts[n].reshape(parts[n].shape[0], -1, w.shape[-1]), view(mom1[n]), view(mom2[n]),
                    "adamw")
        reduced[n], deltas[n], new_m[n], new_v[n] = [turn(r.reshape(w.shape)) for r in res]

    return (loss, dx[None], *[reduced[n] for n in WEIGHTS], *[deltas[n] for n in WEIGHTS],
            *[new_m[n] for n in WEIGHTS], *[new_v[n] for n in WEIGHTS])
```

```python
import jax
import jax.numpy as jnp
from jax import lax
from jax.experimental import pallas as pl
from jax.experimental.pallas import tpu as pltpu

F32, BF16 = jnp.float32, jnp.bfloat16
MESH_ID = pl.DeviceIdType.MESH
AXES = ("x", "y", "c")
N_DEV = 8

EPS = 1e-6
DEPTH, N_A = 4, 2
GM_W, GM_GROUPS, GM_P = 768, 6, 128
MEM_W, MEM_HEADS, HEAD_DIM = 256, 4, 64
SB_W, SB_BLK = 768, 128
LANES = 128
QK_SCALE = HEAD_DIM ** -0.5
GELU_C, GELU_A = 0.7978845608028654, 0.044715

ADAM_LR, ADAM_B1, ADAM_B2, ADAM_EPS, ADAM_WD, ADAM_STEP = 0.001, 0.9, 0.999, 1e-08, 0.01, 10

VMEM_LIMIT = 56 * 1024 * 1024

NT = (((1,), (1,)), ((), ()))
TN = (((0,), (0,)), ((), ()))


def _params(*sem):
    return pltpu.CompilerParams(dimension_semantics=sem, vmem_limit_bytes=VMEM_LIMIT)


def _tile(n, target, mult=LANES):
    best = None
    for t in range(mult, min(n, target) + 1, mult):
        if n % t == 0:
            best = t
    return best if best is not None else n


def _dot(a, b, dims=None):
    if dims is None:
        return jnp.dot(a, b, preferred_element_type=F32)
    return lax.dot_general(a, b, dims, preferred_element_type=F32)


def exchange(srcs, group, same_src, name, split=False):
    size = {"pair": 2, "quad": 4, "all": 8}[group]
    n = len(srcs)
    chunk_shapes = [tuple(s.shape) if same_src else tuple(s.shape[1:]) for s in srcs]
    pieces = [cs[0] if split else 1 for cs in chunk_shapes]
    n_dma = sum(pieces)

    def body(*refs):
        src_refs, out_refs = refs[:n], refs[n:2 * n]
        send_sems, recv_sems, local_sems = refs[2 * n:]
        x, y, c = lax.axis_index("x"), lax.axis_index("y"), lax.axis_index("c")
        if group == "pair":
            me, dev = c, lambda p: (x, y, p)
        elif group == "quad":
            me, dev = 2 * x + y, lambda p: (p // 2, p % 2, c)
        else:
            me, dev = 4 * x + 2 * y + c, lambda p: (p // 4, (p // 2) % 2, p % 2)

        def chunk(t, idx):
            return src_refs[t] if same_src else src_refs[t].at[idx]

        def copies(k, idx, slot, peer):
            out, w = [], k * n_dma
            for t in range(n):
                src, dst = chunk(t, idx), out_refs[t].at[slot]
                for s_ref, d_ref in ([(src.at[u], dst.at[u]) for u in range(pieces[t])] if split else [(src, dst)]):
                    out.append(pltpu.make_async_remote_copy(
                        src_ref=s_ref, dst_ref=d_ref, send_sem=send_sems.at[w], recv_sem=recv_sems.at[w],
                        device_id=dev(peer), device_id_type=MESH_ID))
                    w += 1
            return out

        local = [pltpu.make_async_copy(chunk(t, me), out_refs[t].at[me], local_sems.at[t]) for t in range(n)]
        for cp in local:
            cp.start()
        sends = []
        for k in range(1, size):
            peer = (me + k) % size
            sends += copies(k, peer, me, peer)
        for cp in sends:
            cp.start()
        for k in range(1, size):
            sender = (me + size - k) % size
            for cp in copies(k, me, sender, sender):
                cp.wait_recv()
        for cp in sends:
            cp.wait_send()
        for cp in local:
            cp.wait()

    hbm = pl.BlockSpec(memory_space=pltpu.HBM)
    return pl.pallas_call(
        body, name=name,
        out_shape=[jax.ShapeDtypeStruct((size,) + cs, s.dtype) for cs, s in zip(chunk_shapes, srcs)],
        in_specs=[hbm] * n, out_specs=[hbm] * n,
        scratch_shapes=[pltpu.SemaphoreType.DMA((size * n_dma,)), pltpu.SemaphoreType.DMA((size * n_dma,)),
                        pltpu.SemaphoreType.DMA((n,))],
    )(*srcs)


class Side:
    def __init__(self, srcs, same_src):
        self.srcs, self.same_src, self.n = list(srcs), same_src, len(srcs)
        self.chunk_shapes = [tuple(s.shape) if same_src else tuple(s.shape[1:]) for s in srcs]

    def out_shapes(self):
        return [jax.ShapeDtypeStruct((N_DEV,) + cs, s.dtype) for cs, s in zip(self.chunk_shapes, self.srcs)]

    def scratch(self):
        return [pltpu.SemaphoreType.DMA((N_DEV * self.n,)), pltpu.SemaphoreType.DMA((N_DEV * self.n,))]

    def _copies(self, src_refs, land_refs, send_sems, recv_sems, outgoing):
        me = 4 * lax.axis_index("x") + 2 * lax.axis_index("y") + lax.axis_index("c")
        out = []
        for k in range(1, N_DEV):
            peer = (me + k) % N_DEV if outgoing else (me + N_DEV - k) % N_DEV
            for t in range(self.n):
                src = src_refs[t] if self.same_src else src_refs[t].at[peer if outgoing else me]
                out.append(pltpu.make_async_remote_copy(
                    src_ref=src, dst_ref=land_refs[t].at[me if outgoing else peer],
                    send_sem=send_sems.at[k * self.n + t], recv_sem=recv_sems.at[k * self.n + t],
                    device_id=(peer // 4, (peer // 2) % 2, peer % 2), device_id_type=MESH_ID))
        return out

    def _own(self, src_refs, land_refs, send_sems):
        me = 4 * lax.axis_index("x") + 2 * lax.axis_index("y") + lax.axis_index("c")
        return [pltpu.make_async_copy(src_refs[t] if self.same_src else src_refs[t].at[me], land_refs[t].at[me],
                                      send_sems.at[t]) for t in range(self.n)]

    def start(self, src_refs, land_refs, send_sems, recv_sems):
        for cp in self._own(src_refs, land_refs, send_sems) + self._copies(src_refs, land_refs, send_sems, recv_sems, True):
            cp.start()

    def wait(self, src_refs, land_refs, send_sems, recv_sems):
        for cp in self._copies(src_refs, land_refs, send_sems, recv_sems, False):
            cp.wait_recv()
        for cp in self._copies(src_refs, land_refs, send_sems, recv_sems, True):
            cp.wait_send()
        for cp in self._own(src_refs, land_refs, send_sems):
            cp.wait()


def _call(body, side, name, grid, in_specs, out_specs, out_shape, scratch_shapes, dims, args):
    if side is None:
        res = pl.pallas_call(body, name=name, grid=grid, in_specs=in_specs, out_specs=out_specs, out_shape=out_shape,
                             scratch_shapes=scratch_shapes, compiler_params=_params(*dims))(*args)
        return list(res), []
    n_in, n_out, n_scr, ns = len(in_specs), len(out_specs), len(scratch_shapes), side.n

    def wrapped(*refs):
        ins, srcs = refs[:n_in], refs[n_in:n_in + ns]
        outs, lands = refs[n_in + ns:n_in + ns + n_out], refs[n_in + ns + n_out:n_in + 2 * ns + n_out]
        scratch, (send_sems, recv_sems) = refs[n_in + 2 * ns + n_out:n_in + 2 * ns + n_out + n_scr], refs[-2:]
        first, last = None, None
        for axis, steps in enumerate(grid):
            i = pl.program_id(axis)
            first = (i == 0) if first is None else first & (i == 0)
            last = (i == steps - 1) if last is None else last & (i == steps - 1)

        @pl.when(first)
        def _():
            side.start(srcs, lands, send_sems, recv_sems)

        body(*ins, *outs, *scratch)

        @pl.when(last)
        def _():
            side.wait(srcs, lands, send_sems, recv_sems)

    hbm = pl.BlockSpec(memory_space=pltpu.HBM)
    res = pl.pallas_call(
        wrapped, name=name, grid=grid, in_specs=list(in_specs) + [hbm] * ns, out_specs=list(out_specs) + [hbm] * ns,
        out_shape=list(out_shape) + side.out_shapes(), scratch_shapes=list(scratch_shapes) + side.scratch(),
        compiler_params=_params(*dims))(*args, *side.srcs)
    return list(res[:n_out]), list(res[n_out:])


def sum_leading(parts, out_dtype, name):
    k, rows, cols = parts.shape
    tr = _tile(rows, 512, 16)

    def body(p_ref, o_ref):
        acc = p_ref[0].astype(F32)
        for s in range(1, k):
            acc = acc + p_ref[s].astype(F32)
        o_ref[...] = acc.astype(o_ref.dtype)

    return pl.pallas_call(
        body, name=name, grid=(rows // tr,),
        in_specs=[pl.BlockSpec((k, tr, cols), lambda i: (0, i, 0))],
        out_specs=pl.BlockSpec((tr, cols), lambda i: (i, 0)),
        out_shape=jax.ShapeDtypeStruct((rows, cols), out_dtype),
        compiler_params=_params("arbitrary"),
    )(parts)


def _rms(xf):
    return lax.rsqrt(jnp.mean(xf * xf, axis=-1, keepdims=True) + EPS)


def norm_mm(x, g, w, out_dtype, name, emit_h=False, side=None, w_rows=False):
    m, d = x.shape
    n = w.shape[0] if w_rows else w.shape[1]
    tm, tn = _tile(m, 1024, 8), _tile(n, 1408)

    def body(x_ref, g_ref, w_ref, o_ref, *rest):
        h_ref = rest[-1]

        @pl.when(pl.program_id(1) == 0)
        def _():
            xf = x_ref[...]
            hb = ((xf * _rms(xf)) * g_ref[...]).astype(BF16)
            h_ref[...] = hb
            if emit_h:
                rest[0][...] = hb

        o_ref[...] = _dot(h_ref[...], w_ref[...], NT if w_rows else None).astype(o_ref.dtype)

    out_shape = [jax.ShapeDtypeStruct((m, n), out_dtype)]
    out_specs = [pl.BlockSpec((tm, tn), lambda i, j: (i, j))]
    if emit_h:
        out_shape.append(jax.ShapeDtypeStruct((m, d), BF16))
        out_specs.append(pl.BlockSpec((tm, d), lambda i, j: (i, 0)))
    res, landed = _call(
        body, side, name, (m // tm, n // tn),
        [pl.BlockSpec((tm, d), lambda i, j: (i, 0)), pl.BlockSpec((1, d), lambda i, j: (0, 0)),
         pl.BlockSpec((tn, d), lambda i, j: (j, 0)) if w_rows else pl.BlockSpec((d, tn), lambda i, j: (0, j))],
        out_specs, out_shape, [pltpu.VMEM((tm, d), BF16)], ("arbitrary", "arbitrary"), (x, g, w))
    out = res if emit_h else res[0]
    return out if side is None else (out, landed)


def mm_res(a, w, res, alpha, name, side=None):
    m, k = a.shape
    n = w.shape[1]
    tm, tn = _tile(m, 1024, 8), _tile(n, 1024)

    def body(a_ref, w_ref, r_ref, o_ref):
        o_ref[...] = r_ref[...] + alpha * _dot(a_ref[...], w_ref[...])

    out, landed = _call(
        body, side, name, (m // tm, n // tn),
        [pl.BlockSpec((tm, k), lambda i, j: (i, 0)), pl.BlockSpec((k, tn), lambda i, j: (0, j)),
         pl.BlockSpec((tm, tn), lambda i, j: (i, j))],
        [pl.BlockSpec((tm, tn), lambda i, j: (i, j))], [jax.ShapeDtypeStruct((m, n), F32)], [],
        ("arbitrary", "arbitrary"), (a, w, res))
    return out[0] if side is None else (out[0], landed)


def mm_nt(x, w, alpha, name):
    m, d = x.shape
    n = w.shape[0]
    tm, tn = _tile(m, 1024, 8), _tile(n, 1408)

    def body(x_ref, w_ref, o_ref, xb_ref):
        @pl.when(pl.program_id(1) == 0)
        def _():
            xb_ref[...] = x_ref[...].astype(BF16)

        o_ref[...] = (alpha * _dot(xb_ref[...], w_ref[...], NT)).astype(o_ref.dtype)

    return pl.pallas_call(
        body, name=name, grid=(m // tm, n // tn),
        in_specs=[pl.BlockSpec((tm, d), lambda i, j: (i, 0)), pl.BlockSpec((tn, d), lambda i, j: (j, 0))],
        out_specs=pl.BlockSpec((tm, tn), lambda i, j: (i, j)),
        out_shape=jax.ShapeDtypeStruct((m, n), BF16),
        scratch_shapes=[pltpu.VMEM((tm, d), BF16)],
        compiler_params=_params("arbitrary", "arbitrary"),
    )(x, w)


def mm_tn(a, b, alpha, name, ta_target=1024, tb_target=512, side=None):
    s, ka = a.shape
    nb = b.shape[1]
    ta, tb, ts = _tile(ka, ta_target), _tile(nb, tb_target), _tile(s, 1024, 16)
    steps = s // ts

    def body(a_ref, b_ref, o_ref, acc_ref):
        t = pl.program_id(2)

        @pl.when(t == 0)
        def _():
            acc_ref[...] = jnp.zeros_like(acc_ref)

        acc_ref[...] += _dot(a_ref[...].astype(BF16), b_ref[...].astype(BF16), TN)

        @pl.when(t == steps - 1)
        def _():
            o_ref[...] = (alpha * acc_ref[...]).astype(o_ref.dtype)

    res, landed = _call(
        body, side, name, (ka // ta, nb // tb, steps),
        [pl.BlockSpec((ts, ta), lambda i, j, t: (t, i)), pl.BlockSpec((ts, tb), lambda i, j, t: (t, j))],
        [pl.BlockSpec((ta, tb), lambda i, j, t: (i, j))], [jax.ShapeDtypeStruct((ka, nb), BF16)],
        [pltpu.VMEM((ta, tb), F32)], ("arbitrary", "arbitrary", "arbitrary"), (a, b))
    return res[0] if side is None else (res[0], landed)


def mm_nt_normbwd(dy, w, x, g, res, name, side=None, w_rows=False):
    m, n = dy.shape
    d = w.shape[1] if w_rows else w.shape[0]
    tm, tk = _tile(m, 1024, 8), _tile(n, 1408)
    steps = n // tk
    has_res = res is not None

    def body(*refs):
        if has_res:
            dy_ref, w_ref, x_ref, g_ref, r_ref, dx_ref, dg_ref, h_ref, acc_ref = refs
        else:
            dy_ref, w_ref, x_ref, g_ref, dx_ref, dg_ref, h_ref, acc_ref = refs
        i, t = pl.program_id(0), pl.program_id(1)

        @pl.when(t == 0)
        def _():
            acc_ref[...] = jnp.zeros_like(acc_ref)

        @pl.when((t == 0) & (i == 0))
        def _():
            dg_ref[...] = jnp.zeros_like(dg_ref)

        acc_ref[...] += _dot(dy_ref[...], w_ref[...], None if w_rows else NT)

        @pl.when(t == steps - 1)
        def _():
            xf = x_ref[...]
            r = _rms(xf)
            xhat = xf * r
            dh = acc_ref[...]
            gain = g_ref[...]
            dg_ref[...] += jnp.sum(dh * xhat, axis=0, keepdims=True)
            dxhat = dh * gain
            dx = r * (dxhat - xhat * jnp.mean(dxhat * xhat, axis=-1, keepdims=True))
            dx_ref[...] = (r_ref[...] + dx) if has_res else dx
            h_ref[...] = (xhat * gain).astype(BF16)

    row = lambda i, t: (i, 0)
    in_specs = [pl.BlockSpec((tm, tk), lambda i, t: (i, t)),
                pl.BlockSpec((tk, d), lambda i, t: (t, 0)) if w_rows else pl.BlockSpec((d, tk), lambda i, t: (0, t)),
                pl.BlockSpec((tm, d), row), pl.BlockSpec((1, d), lambda i, t: (0, 0))]
    args = [dy, w, x, g]
    if has_res:
        in_specs.append(pl.BlockSpec((tm, d), row))
        args.append(res)
    res, landed = _call(
        body, side, name, (m // tm, steps), in_specs,
        [pl.BlockSpec((tm, d), row), pl.BlockSpec((1, d), lambda i, t: (0, 0)), pl.BlockSpec((tm, d), row)],
        [jax.ShapeDtypeStruct((m, d), F32), jax.ShapeDtypeStruct((1, d), F32), jax.ShapeDtypeStruct((m, d), BF16)],
        [pltpu.VMEM((tm, d), F32)], ("arbitrary", "arbitrary"), args)
    return res if side is None else (res, landed)


def _sigmoid(z):
    return 1.0 / (1.0 + jnp.exp(-z))


def _swiglu(gate_b, up_b):
    gate = gate_b.astype(F32)
    return (gate * _sigmoid(gate) * up_b.astype(F32)).astype(BF16)


def swiglu_mm_res(gu, w, res, alpha, name, side=None):
    m, f2 = gu.shape
    f, n = w.shape
    tm, tc = _tile(m, 256, 16), _tile(f, 256)

    def body(gu_ref, w_ref, r_ref, o_ref):
        acc = jnp.zeros((tm, n), F32)
        for c0 in range(0, f, tc):
            act = _swiglu(gu_ref[:, c0:c0 + tc], gu_ref[:, f + c0:f + c0 + tc])
            acc = acc + _dot(act, w_ref[c0:c0 + tc, :])
        o_ref[...] = r_ref[...] + alpha * acc

    out, landed = _call(
        body, side, name, (m // tm,),
        [pl.BlockSpec((tm, f2), lambda i: (i, 0)), pl.BlockSpec((f, n), lambda i: (0, 0)),
         pl.BlockSpec((tm, n), lambda i: (i, 0))],
        [pl.BlockSpec((tm, n), lambda i: (i, 0))], [jax.ShapeDtypeStruct((m, n), F32)], [], ("arbitrary",), (gu, w, res))
    return out[0] if side is None else (out[0], landed)


def swiglu_mm_tn(gu, b, alpha, name, side=None):
    s, f2 = gu.shape
    f, n = f2 // 2, b.shape[1]
    ta, ts = _tile(f, 1408), _tile(s, 512, 16)
    steps, half = s // ts, f // ta

    def body(g_ref, u_ref, b_ref, o_ref, acc_ref):
        t = pl.program_id(1)

        @pl.when(t == 0)
        def _():
            acc_ref[...] = jnp.zeros_like(acc_ref)

        bb = b_ref[...].astype(BF16)
        for c0 in range(0, ta, LANES):
            acc_ref[c0:c0 + LANES, :] += _dot(_swiglu(g_ref[:, c0:c0 + LANES], u_ref[:, c0:c0 + LANES]), bb, TN)

        @pl.when(t == steps - 1)
        def _():
            o_ref[...] = (alpha * acc_ref[...]).astype(o_ref.dtype)

    res, landed = _call(
        body, side, name, (half, steps),
        [pl.BlockSpec((ts, ta), lambda i, t: (t, i)), pl.BlockSpec((ts, ta), lambda i, t: (t, half + i)),
         pl.BlockSpec((ts, n), lambda i, t: (t, 0))],
        [pl.BlockSpec((ta, n), lambda i, t: (i, 0))], [jax.ShapeDtypeStruct((f, n), BF16)],
        [pltpu.VMEM((ta, n), F32)], ("arbitrary", "arbitrary"), (gu, gu, b))
    return res[0] if side is None else (res[0], landed)


def mm_nt_swiglu_bwd(x, w, gu, alpha, name, side=None):
    m, d = x.shape
    f = w.shape[0]
    tm, tc = _tile(m, 256, 16), _tile(f, 256)

    def body(x_ref, w_ref, gu_ref, o_ref):
        xb = x_ref[...].astype(BF16)
        for c0 in range(0, f, tc):
            d_act = alpha * _dot(xb, w_ref[c0:c0 + tc, :], NT)
            gate, up = gu_ref[:, c0:c0 + tc].astype(F32), gu_ref[:, f + c0:f + c0 + tc].astype(F32)
            sg = _sigmoid(gate)
            o_ref[:, c0:c0 + tc] = (d_act * up * (sg * (1.0 + gate * (1.0 - sg)))).astype(BF16)
            o_ref[:, f + c0:f + c0 + tc] = (d_act * (gate * sg)).astype(BF16)

    res, landed = _call(
        body, side, name, (m // tm,),
        [pl.BlockSpec((tm, d), lambda i: (i, 0)), pl.BlockSpec((f, d), lambda i: (0, 0)),
         pl.BlockSpec((tm, 2 * f), lambda i: (i, 0))],
        [pl.BlockSpec((tm, 2 * f), lambda i: (i, 0))], [jax.ShapeDtypeStruct((m, 2 * f), BF16)], [], ("arbitrary",),
        (x, w, gu))
    return res[0] if side is None else (res[0], landed)


def _gelu(x):
    return 0.5 * x * (1.0 + jnp.tanh(GELU_C * (x + GELU_A * x * x * x)))


def _gelu_grad(x):
    t = jnp.tanh(GELU_C * (x + GELU_A * x * x * x))
    return 0.5 * (1.0 + t) + 0.5 * x * (1.0 - t * t) * (GELU_C * (1.0 + 3.0 * GELU_A * x * x))


def _chunk_mask():
    row = lax.broadcasted_iota(jnp.int32, (GM_P, GM_P), 0)
    col = lax.broadcasted_iota(jnp.int32, (GM_P, GM_P), 1)
    return (col < GM_P // 2) | (row >= GM_P // 2)


def gmlp_fwd(proj, gain, w_s, bias, name):
    s, pw = proj.shape
    tm = _tile(s, 256, GM_P)

    def body(p_ref, gain_ref, w_ref, b_ref, o_ref):
        mask = _chunk_mask()
        u = _gelu(p_ref[:, :GM_W])
        v = _gelu(p_ref[:, GM_W:2 * GM_W])
        vn = ((v * _rms(v)) * gain_ref[...]).astype(BF16)
        for g in range(GM_GROUPS):
            wg = jnp.where(mask, w_ref[g], 0.0).astype(BF16)
            cols = slice(g * GM_P, (g + 1) * GM_P)
            for n in range(tm // GM_P):
                rows = slice(n * GM_P, (n + 1) * GM_P)
                mixed = _dot(wg, vn[rows, cols]) + b_ref[:, cols]
                o_ref[rows, cols] = (u[rows, cols] * mixed).astype(BF16)

    return pl.pallas_call(
        body, name=name, grid=(s // tm,),
        in_specs=[pl.BlockSpec((tm, pw), lambda i: (i, 0)), pl.BlockSpec((1, GM_W), lambda i: (0, 0)),
                  pl.BlockSpec((GM_GROUPS, GM_P, GM_P), lambda i: (0, 0, 0)), pl.BlockSpec((GM_P, GM_W), lambda i: (0, 0))],
        out_specs=pl.BlockSpec((tm, GM_W), lambda i: (i, 0)),
        out_shape=jax.ShapeDtypeStruct((s, GM_W + MEM_W), BF16), compiler_params=_params("arbitrary"),
    )(proj, gain, w_s, bias)


def gmlp_bwd(proj, dy, gain, w_s, bias, name):
    s, pw = proj.shape
    dw_total = dy.shape[1]
    tm = _tile(s, 256, GM_P)

    def body(p_ref, dy_ref, gain_ref, w_ref, b_ref, dp_ref, dw_ref, db_ref, dgain_ref, dvn_ref):
        @pl.when(pl.program_id(0) == 0)
        def _():
            dw_ref[...] = jnp.zeros_like(dw_ref)
            db_ref[...] = jnp.zeros_like(db_ref)
            dgain_ref[...] = jnp.zeros_like(dgain_ref)

        mask = _chunk_mask()
        pu = p_ref[:, :GM_W]
        pv = p_ref[:, GM_W:2 * GM_W]
        u = _gelu(pu)
        v = _gelu(pv)
        r = _rms(v)
        vhat = v * r
        gain = gain_ref[...]
        vn = (vhat * gain).astype(BF16)
        gu_grad = _gelu_grad(pu)
        for g in range(GM_GROUPS):
            wg = jnp.where(mask, w_ref[g], 0.0).astype(BF16)
            cols = slice(g * GM_P, (g + 1) * GM_P)
            dw_acc = jnp.zeros((GM_P, GM_P), F32)
            db_acc = jnp.zeros((GM_P, 1), F32)
            for n in range(tm // GM_P):
                rows = slice(n * GM_P, (n + 1) * GM_P)
                dyb = dy_ref[rows, cols].astype(F32)
                vnb = vn[rows, cols]
                mixed = _dot(wg, vnb) + b_ref[:, cols]
                dmixed = dyb * u[rows, cols]
                dmb = dmixed.astype(BF16)
                dp_ref[rows, cols] = (dyb * mixed * gu_grad[rows, cols]).astype(BF16)
                dw_acc = dw_acc + _dot(dmb, vnb, NT)
                db_acc = db_acc + jnp.sum(dmixed, axis=1, keepdims=True)
                dvn_ref[rows, cols] = _dot(wg, dmb, TN)
            dw_ref[g] += jnp.where(mask, dw_acc, 0.0)
            db_ref[g] += jnp.broadcast_to(db_acc, (GM_P, GM_P))
        dvn = dvn_ref[...]
        dgain_ref[...] += jnp.sum(dvn * vhat, axis=0, keepdims=True)
        dvhat = dvn * gain
        dv = r * (dvhat - vhat * jnp.mean(dvhat * vhat, axis=-1, keepdims=True))
        dp_ref[:, GM_W:] = (dv * _gelu_grad(pv)).astype(BF16)

    const3 = lambda i: (0, 0, 0)
    return pl.pallas_call(
        body, name=name, grid=(s // tm,),
        in_specs=[pl.BlockSpec((tm, pw), lambda i: (i, 0)), pl.BlockSpec((tm, dw_total), lambda i: (i, 0)),
                  pl.BlockSpec((1, GM_W), lambda i: (0, 0)), pl.BlockSpec((GM_GROUPS, GM_P, GM_P), const3),
                  pl.BlockSpec((GM_P, GM_W), lambda i: (0, 0))],
        out_specs=[pl.BlockSpec((tm, 2 * GM_W), lambda i: (i, 0)), pl.BlockSpec((GM_GROUPS, GM_P, GM_P), const3),
                   pl.BlockSpec((GM_GROUPS, GM_P, GM_P), const3), pl.BlockSpec((1, GM_W), lambda i: (0, 0))],
        out_shape=[jax.ShapeDtypeStruct((s, pw), BF16), jax.ShapeDtypeStruct((GM_GROUPS, GM_P, GM_P), F32),
                   jax.ShapeDtypeStruct((GM_GROUPS, GM_P, GM_P), F32), jax.ShapeDtypeStruct((1, GM_W), F32)],
        scratch_shapes=[pltpu.VMEM((tm, GM_W), F32)],
        compiler_params=_params("arbitrary"),
    )(proj, dy, gain, w_s, bias)


def _keep(mask, xb):
    return jnp.where(mask, xb.astype(F32), 0.0).astype(BF16)


def _head_masks(rows, width, heads):
    lane = lax.broadcasted_iota(jnp.int32, (rows, width), 1)
    return [(lane >= HEAD_DIM * h) & (lane < HEAD_DIM * (h + 1)) for h in range(heads)]


def _mem_probs(qh, k):
    sc = _dot(qh, k, NT) * QK_SCALE
    e = jnp.exp(sc - jnp.max(sc, axis=-1, keepdims=True))
    return e / jnp.sum(e, axis=-1, keepdims=True)


def mem_fwd(proj, q_blk, mem_kv, layer, into, into_blk, name):
    s = proj.shape[0]
    n_mem = mem_kv.shape[0]
    tm = _tile(s, 512, 16)

    def body(q_ref, k_ref, v_ref, into_ref, o_ref):
        q = q_ref[...].astype(BF16)
        k, v = k_ref[...], v_ref[...]
        out = jnp.zeros((tm, MEM_W), F32)
        for hm in _head_masks(tm, MEM_W, MEM_HEADS):
            p = _mem_probs(_keep(hm, q), k)
            out = out + jnp.where(hm, _dot(p.astype(BF16), v), 0.0)
        o_ref[...] = out.astype(BF16)

    return pl.pallas_call(
        body, name=name, grid=(s // tm,),
        in_specs=[pl.BlockSpec((tm, MEM_W), lambda i: (i, q_blk)), pl.BlockSpec((n_mem, MEM_W), lambda i: (0, 2 * layer)),
                  pl.BlockSpec((n_mem, MEM_W), lambda i: (0, 2 * layer + 1)), pl.BlockSpec(memory_space=pl.ANY)],
        out_specs=pl.BlockSpec((tm, MEM_W), lambda i: (i, into_blk)),
        out_shape=jax.ShapeDtypeStruct(into.shape, BF16), input_output_aliases={3: 0},
        compiler_params=_params("arbitrary"),
    )(proj, mem_kv, mem_kv, into)


def mem_bwd(proj, q_blk, mem_kv, layer, dy, dy_blk, into, name):
    s = proj.shape[0]
    n_mem = mem_kv.shape[0]
    tm = _tile(s, 512, 16)

    def body(q_ref, k_ref, v_ref, dy_ref, into_ref, dq_ref, dk_ref, dv_ref):
        @pl.when(pl.program_id(0) == 0)
        def _():
            dk_ref[...] = jnp.zeros_like(dk_ref)
            dv_ref[...] = jnp.zeros_like(dv_ref)

        q = q_ref[...].astype(BF16)
        k, v = k_ref[...], v_ref[...]
        dy = dy_ref[...]
        dq = jnp.zeros((tm, MEM_W), F32)
        dk = jnp.zeros((n_mem, MEM_W), F32)
        dv = jnp.zeros((n_mem, MEM_W), F32)
        for hm in _head_masks(tm, MEM_W, MEM_HEADS):
            qh = _keep(hm, q)
            dyh = _keep(hm, dy)
            p = _mem_probs(qh, k)
            dp = _dot(dyh, v, NT)
            dv = dv + _dot(p.astype(BF16), dyh, TN)
            ds = (p * (dp - jnp.sum(dp * p, axis=-1, keepdims=True)) * QK_SCALE).astype(BF16)
            dq = dq + jnp.where(hm, _dot(ds, k), 0.0)
            dk = dk + _dot(ds, qh, TN)
        dq_ref[...] = dq.astype(BF16)
        dk_ref[...] += dk
        dv_ref[...] += dv

    const = lambda i: (0, 0)
    return pl.pallas_call(
        body, name=name, grid=(s // tm,),
        in_specs=[pl.BlockSpec((tm, MEM_W), lambda i: (i, q_blk)), pl.BlockSpec((n_mem, MEM_W), lambda i: (0, 2 * layer)),
                  pl.BlockSpec((n_mem, MEM_W), lambda i: (0, 2 * layer + 1)), pl.BlockSpec((tm, MEM_W), lambda i: (i, dy_blk)),
                  pl.BlockSpec(memory_space=pl.ANY)],
        out_specs=[pl.BlockSpec((tm, MEM_W), lambda i: (i, q_blk)), pl.BlockSpec((n_mem, MEM_W), const),
                   pl.BlockSpec((n_mem, MEM_W), const)],
        out_shape=[jax.ShapeDtypeStruct(into.shape, BF16), jax.ShapeDtypeStruct((n_mem, MEM_W), F32),
                   jax.ShapeDtypeStruct((n_mem, MEM_W), F32)],
        input_output_aliases={4: 0}, compiler_params=_params("arbitrary"),
    )(proj, mem_kv, mem_kv, dy, into)


SB_KEYS = 256
SB_SUB = SB_KEYS // SB_BLK
SB_QROWS = 256
SB_QB = SB_QROWS // SB_BLK
SB_CHAINS = 2 * SB_QB
SB_DEAD = -110.0


def _split(xf):
    hi = xf.astype(BF16)
    return hi, (xf - hi.astype(F32)).astype(BF16)


def _sb_consts():
    row = lax.bitwise_and(lax.broadcasted_iota(jnp.int32, (2 * SB_BLK, 2 * SB_BLK), 0), SB_BLK - 1)
    col = lax.broadcasted_iota(jnp.int32, (2 * SB_BLK, 2 * SB_BLK), 1)
    ones = col >= SB_BLK
    after2 = jnp.where(ones | (row > col), -1.0, 0.0).astype(BF16)
    from2 = jnp.where(ones | (row >= col), 1.0, 0.0).astype(BF16)
    r = lax.broadcasted_iota(jnp.int32, (SB_BLK, SB_BLK), 0)
    c = lax.broadcasted_iota(jnp.int32, (SB_BLK, SB_BLK), 1)
    return after2, from2, c - r, [c < HEAD_DIM, c >= HEAD_DIM]


def _suffix(xf, tri2):
    hi, lo = _split(xf)
    return _dot(jnp.concatenate([hi, lo], axis=1), tri2)


def _sb_logs(z, mask):
    softplus = jnp.maximum(z, 0.0) + jnp.log(1.0 + jnp.exp(-jnp.abs(z)))
    log_beta = z - softplus
    if mask is not None:
        softplus = jnp.where(mask, softplus, 0.0)
    return softplus, log_beta


def _sb_queries(q_ref, heads):
    q = q_ref[...].astype(F32) * QK_SCALE
    return [jnp.where(hm, q[r * SB_BLK:(r + 1) * SB_BLK], 0.0).astype(BF16) for r in range(SB_QB) for hm in heads]


def _sb_walk(i, block, state):
    places = SB_SUB // SB_QB
    assert places in (1, 2)
    own = lax.shift_right_logical(i * SB_QB, SB_SUB.bit_length() - 1)
    firsts = [[(v * SB_QB + r) * SB_BLK for r in range(SB_QB) for _ in range(2)] for v in range(places)]
    if places == 1:
        state = block(own, state, firsts[0])
    else:
        state = lax.cond(lax.bitwise_and(i, 1) == 0, lambda st: block(own, st, firsts[0]),
                         lambda st: block(own, st, firsts[1]), state)

    def live(carry):
        j, st = carry
        most = st[0][0]
        for run in st[0][1:]:
            most = jnp.maximum(most, run)
        return (j >= 0) & (jnp.max(most) > SB_DEAD)

    return lax.while_loop(live, lambda carry: (carry[0] - 1, block(carry[0], carry[1], None)), (own - 1, state))[1]


def _sb_tiles(first):
    out = []
    for c in reversed(range(SB_SUB)):
        for n in range(SB_CHAINS):
            if first is None or c * SB_BLK < first[n]:
                out.append((c, n, "before"))
            elif c * SB_BLK == first[n]:
                out.append((c, n, "diagonal"))
    return out


def _sb_heads_apart(stacked, heads, r):
    return jnp.where(heads[0], stacked[2 * r * SB_BLK:(2 * r + 1) * SB_BLK],
                     stacked[(2 * r + 1) * SB_BLK:(2 * r + 2) * SB_BLK])


def sb_fwd(proj, kv, name):
    s = proj.shape[0]
    assert s % SB_KEYS == 0 and SB_KEYS % SB_QROWS == 0

    def body(q_ref, k_ref, v_ref, o_ref, y_ref):
        after2, _, col_minus_row, heads = _sb_consts()
        q_all = jnp.concatenate(_sb_queries(q_ref, heads), axis=0)
        key_before_query = col_minus_row < 0

        def block(j, state, first):
            runs, acc = list(state[0]), state[1]
            rows = pl.ds(pl.multiple_of(j * SB_KEYS, SB_KEYS), SB_KEYS)
            kb, vb = k_ref[rows, :], v_ref[rows, :]
            z = _dot(q_all, kb, NT)
            pend = {}
            parts = [[jnp.zeros((SB_BLK, SB_BLK), BF16)] * SB_SUB for _ in range(SB_CHAINS)]
            for c, n, where in _sb_tiles(first):
                mask = key_before_query if where == "diagonal" else None
                softplus, lb = _sb_logs(z[n * SB_BLK:(n + 1) * SB_BLK, c * SB_BLK:(c + 1) * SB_BLK], mask)
                pend[c, n] = (lb, _suffix(softplus, after2), mask)
            for c, n, _ in _sb_tiles(first):
                lb, r, mask = pend.pop((c, n))
                a = jnp.exp(lb + r[:, :SB_BLK] + runs[n])
                if mask is not None:
                    a = jnp.where(mask, a, 0.0)
                parts[n][c] = a.astype(BF16)
                runs[n] = runs[n] + r[:, SB_BLK:]
            a_all = jnp.concatenate([jnp.concatenate(p, axis=1) for p in parts], axis=0)
            return tuple(runs), acc + _dot(a_all, vb)

        zero = jnp.zeros((SB_BLK, LANES), F32)
        state = _sb_walk(pl.program_id(1), block, ((zero,) * SB_CHAINS, jnp.zeros((SB_CHAINS * SB_BLK, LANES), F32)))
        for r in range(SB_QB):
            out = _sb_heads_apart(state[1], heads, r)
            o_ref[r * SB_BLK:(r + 1) * SB_BLK, :] = out
            y_ref[r * SB_BLK:(r + 1) * SB_BLK, :] = out.astype(BF16)

    pairs = SB_W // LANES
    block_spec = pl.BlockSpec((SB_QROWS, LANES), lambda p, i: (i, p))
    return pl.pallas_call(
        body, name=name, grid=(pairs, s // SB_QROWS),
        in_specs=[block_spec, pl.BlockSpec((s, LANES), lambda p, i: (0, p)),
                  pl.BlockSpec((s, LANES), lambda p, i: (0, pairs + p))],
        out_specs=[block_spec, block_spec],
        out_shape=[jax.ShapeDtypeStruct((s, SB_W), F32), jax.ShapeDtypeStruct((s, SB_W + MEM_W), BF16)],
        compiler_params=_params("arbitrary", "arbitrary"),
    )(proj, kv, kv)


def sb_bwd(proj, kv, out, dy, name):
    s = proj.shape[0]

    def body(q_ref, k_ref, v_ref, o_ref, do_ref, dq_ref, dk_ref, dv_ref):
        i = pl.program_id(1)

        @pl.when(i == 0)
        def _():
            dk_ref[...] = jnp.zeros_like(dk_ref)
            dv_ref[...] = jnp.zeros_like(dv_ref)

        after2, from2, col_minus_row, heads = _sb_consts()
        q_all = jnp.concatenate(_sb_queries(q_ref, heads), axis=0)
        key_before_query = col_minus_row < 0
        d_out = do_ref[...].astype(F32)
        prod = d_out * o_ref[...]
        dos, totals = [], []
        for r in range(SB_QB):
            rr = slice(r * SB_BLK, (r + 1) * SB_BLK)
            for hm in heads:
                dos.append(jnp.where(hm, d_out[rr], 0.0).astype(BF16))
                totals.append(jnp.broadcast_to(jnp.sum(jnp.where(hm, prod[rr], 0.0), axis=1, keepdims=True),
                                               (SB_BLK, SB_BLK)))
        do_all = jnp.concatenate(dos, axis=0)

        def block(j, state, first):
            runs, seens, dq = list(state[0]), list(state[1]), state[2]
            rows = pl.ds(pl.multiple_of(j * SB_KEYS, SB_KEYS), SB_KEYS)
            kb, vb = k_ref[rows, :], v_ref[rows, :]
            z = _dot(q_all, kb, NT)
            da = _dot(do_all, vb, NT)
            pend, pend2 = {}, {}
            a_parts = [[jnp.zeros((SB_BLK, SB_BLK), BF16)] * SB_SUB for _ in range(SB_CHAINS)]
            dz_parts = [[jnp.zeros((SB_BLK, SB_BLK), BF16)] * SB_SUB for _ in range(SB_CHAINS)]
            for c, n, where in _sb_tiles(first):
                mask = key_before_query if where == "diagonal" else None
                softplus, lb = _sb_logs(z[n * SB_BLK:(n + 1) * SB_BLK, c * SB_BLK:(c + 1) * SB_BLK], mask)
                pend[c, n] = (softplus, lb, _suffix(softplus, after2), mask)
            for c, n, _ in _sb_tiles(first):
                softplus, lb, r, mask = pend.pop((c, n))
                a = jnp.exp(lb + r[:, :SB_BLK] + runs[n])
                if mask is not None:
                    a = jnp.where(mask, a, 0.0)
                runs[n] = runs[n] + r[:, SB_BLK:]
                ab = a.astype(BF16)
                a_parts[n][c] = ab
                dl = ab.astype(F32) * da[n * SB_BLK:(n + 1) * SB_BLK, c * SB_BLK:(c + 1) * SB_BLK]
                pend2[c, n] = (softplus, lb, dl, _suffix(dl, from2), mask)
            for c, n, _ in _sb_tiles(first):
                softplus, lb, dl, r2, mask = pend2.pop((c, n))
                d_lom = totals[n] - (r2[:, :SB_BLK] + seens[n])
                if mask is not None:
                    d_lom = jnp.where(mask, d_lom, 0.0)
                seens[n] = seens[n] + r2[:, SB_BLK:]
                dz_parts[n][c] = (dl * jnp.exp(-softplus) - d_lom * jnp.exp(lb)).astype(BF16)
            a_all = jnp.concatenate([jnp.concatenate(p, axis=1) for p in a_parts], axis=0)
            dz_all = jnp.concatenate([jnp.concatenate(p, axis=1) for p in dz_parts], axis=0)
            dv_ref[rows, :] += _dot(a_all, do_all, TN)
            dk_ref[rows, :] += _dot(dz_all, q_all, TN)
            return tuple(runs), tuple(seens), dq + _dot(dz_all, kb)

        zero = jnp.zeros((SB_BLK, LANES), F32)
        state = _sb_walk(i, block, ((zero,) * SB_CHAINS, (zero,) * SB_CHAINS,
                                    jnp.zeros((SB_CHAINS * SB_BLK, LANES), F32)))
        for r in range(SB_QB):
            dq_ref[r * SB_BLK:(r + 1) * SB_BLK, :] = (_sb_heads_apart(state[2], heads, r) * QK_SCALE).astype(BF16)

    pairs = SB_W // LANES
    blk = lambda p, i: (i, p)
    col = lambda p, i: (0, p)
    return pl.pallas_call(
        body, name=name, grid=(pairs, s // SB_QROWS),
        in_specs=[pl.BlockSpec((SB_QROWS, LANES), blk), pl.BlockSpec((s, LANES), col),
                  pl.BlockSpec((s, LANES), lambda p, i: (0, pairs + p)), pl.BlockSpec((SB_QROWS, LANES), blk),
                  pl.BlockSpec((SB_QROWS, LANES), blk)],
        out_specs=[pl.BlockSpec((SB_QROWS, LANES), blk), pl.BlockSpec((s, LANES), col), pl.BlockSpec((s, LANES), col)],
        out_shape=[jax.ShapeDtypeStruct((s, SB_W + MEM_W), BF16), jax.ShapeDtypeStruct((s, SB_W), F32),
                   jax.ShapeDtypeStruct((s, SB_W), F32)],
        compiler_params=_params("arbitrary", "arbitrary"),
    )(proj, kv, kv, out, dy)


def final_loss(x, g, target, name):
    s, d = x.shape
    tm = _tile(s, 256, 8)

    def body(x_ref, g_ref, t_ref, loss_ref, dx_ref, dg_ref):
        @pl.when(pl.program_id(0) == 0)
        def _():
            loss_ref[...] = jnp.zeros_like(loss_ref)
            dg_ref[...] = jnp.zeros_like(dg_ref)

        xf = x_ref[...]
        r = _rms(xf)
        xhat = xf * r
        gain = g_ref[...]
        diff = xhat * gain - t_ref[...]
        sq = jnp.sum(jnp.sum(diff * diff, axis=1, keepdims=True), axis=0, keepdims=True)
        loss_ref[...] += jnp.broadcast_to(sq, loss_ref.shape)
        dy = diff * (1.0 / d)
        dg_ref[...] += jnp.sum(dy * xhat, axis=0, keepdims=True)
        dxhat = dy * gain
        dx_ref[...] = r * (dxhat - xhat * jnp.mean(dxhat * xhat, axis=-1, keepdims=True))

    row = lambda i: (i, 0)
    const = lambda i: (0, 0)
    return pl.pallas_call(
        body, name=name, grid=(s // tm,),
        in_specs=[pl.BlockSpec((tm, d), row), pl.BlockSpec((1, d), const), pl.BlockSpec((tm, d), row)],
        out_specs=[pl.BlockSpec((8, LANES), const), pl.BlockSpec((tm, d), row), pl.BlockSpec((1, d), const)],
        out_shape=[jax.ShapeDtypeStruct((8, LANES), F32), jax.ShapeDtypeStruct((s, d), F32), jax.ShapeDtypeStruct((1, d), F32)],
        compiler_params=_params("arbitrary"),
    )(x, g, target)


def adamw(w, parts, m, v, name, side=None):
    rows, cols = w.shape
    k = parts.shape[0]
    tr = _tile(rows, 512, 16)
    c1, c2 = 1.0 - ADAM_B1 ** ADAM_STEP, 1.0 - ADAM_B2 ** ADAM_STEP

    def body(w_ref, p_ref, m_ref, v_ref, g_ref, d_ref, nm_ref, nv_ref):
        grad = p_ref[0].astype(F32)
        for s in range(1, k):
            grad = grad + p_ref[s].astype(F32)
        nm = ADAM_B1 * m_ref[...] + (1.0 - ADAM_B1) * grad
        nv = ADAM_B2 * v_ref[...] + (1.0 - ADAM_B2) * (grad * grad)
        g_ref[...] = grad
        d_ref[...] = -ADAM_LR * ((nm / c1) / (jnp.sqrt(nv / c2) + ADAM_EPS) + ADAM_WD * w_ref[...])
        nm_ref[...] = nm
        nv_ref[...] = nv

    spec = pl.BlockSpec((tr, cols), lambda i: (i, 0))
    shape = jax.ShapeDtypeStruct((rows, cols), F32)
    res, landed = _call(body, side, name, (rows // tr,), [spec, pl.BlockSpec((k, tr, cols), lambda i: (0, i, 0)), spec, spec],
                        [spec] * 4, [shape] * 4, [], ("arbitrary",), (w, parts, m, v))
    return res if side is None else (res, landed)


SHARDED = {"ffn1_w_gate": 2, "ffn1_w_up": 2, "ffn1_w_down": 1, "ffn2_w_gate": 2, "ffn2_w_up": 2, "ffn2_w_down": 1,
           "w_mem_kv": 1, "a_w_in": 2, "a_w_out": 1, "w_kv": 1, "b_w_in": 1, "b_w_out": 1}
TRANSPOSED = ("ffn1_w_gate", "ffn1_w_up", "ffn2_w_gate", "ffn2_w_up")
SMALL = ["ffn1_norm", "mix_norm", "ffn2_norm", "mem_norm", "kv_norm", "final_norm", "a_v_norm", "a_w_spatial", "a_b_spatial"]
WEIGHTS = ["ffn1_norm", "ffn1_w_gate", "ffn1_w_up", "ffn1_w_down", "mix_norm", "ffn2_norm", "ffn2_w_gate", "ffn2_w_up",
           "ffn2_w_down", "mem_norm", "w_mem_kv", "a_w_in", "a_v_norm", "a_w_spatial", "a_b_spatial", "a_w_out", "kv_norm",
           "w_kv", "b_w_in", "b_w_out", "final_norm"]


def _all_sum(parts, name):
    flat = jnp.concatenate([p.reshape(-1) for p in parts])
    pad = (-flat.size) % (16 * LANES)
    buf = jnp.pad(flat, (0, pad)).reshape(-1, LANES)
    total = sum_leading(exchange([buf], "all", True, name)[0], F32, name + "_sum").reshape(-1)
    out, off = [], 0
    for p in parts:
        out.append(total[off:off + p.size].reshape(p.shape))
        off += p.size
    return out


def _device_index():
    return 4 * lax.axis_index("x") + 2 * lax.axis_index("y") + lax.axis_index("c")


def kernel(x, mem, ffn1_norm, ffn1_w_gate, ffn1_w_up, ffn1_w_down, mix_norm, ffn2_norm, ffn2_w_gate, ffn2_w_up, ffn2_w_down, mem_norm, w_mem_kv, a_w_in, a_v_norm, a_w_spatial, a_b_spatial, a_w_out, kv_norm, w_kv, b_w_in, b_w_out, final_norm, loss_target, m_ffn1_norm, m_ffn1_w_gate, m_ffn1_w_up, m_ffn1_w_down, m_mix_norm, m_ffn2_norm, m_ffn2_w_gate, m_ffn2_w_up, m_ffn2_w_down, m_mem_norm, m_w_mem_kv, m_a_w_in, m_a_v_norm, m_a_w_spatial, m_a_b_spatial, m_a_w_out, m_kv_norm, m_w_kv, m_b_w_in, m_b_w_out, m_final_norm, v_ffn1_norm, v_ffn1_w_gate, v_ffn1_w_up, v_ffn1_w_down, v_mix_norm, v_ffn2_norm, v_ffn2_w_gate, v_ffn2_w_up, v_ffn2_w_down, v_mem_norm, v_w_mem_kv, v_a_w_in, v_a_v_norm, v_a_w_spatial, v_a_b_spatial, v_a_w_out, v_kv_norm, v_w_kv, v_b_w_in, v_b_w_out, v_final_norm):
    weights = dict(ffn1_norm=ffn1_norm, ffn1_w_gate=ffn1_w_gate, ffn1_w_up=ffn1_w_up, ffn1_w_down=ffn1_w_down, mix_norm=mix_norm, ffn2_norm=ffn2_norm, ffn2_w_gate=ffn2_w_gate, ffn2_w_up=ffn2_w_up, ffn2_w_down=ffn2_w_down, mem_norm=mem_norm, w_mem_kv=w_mem_kv, a_w_in=a_w_in, a_v_norm=a_v_norm, a_w_spatial=a_w_spatial, a_b_spatial=a_b_spatial, a_w_out=a_w_out, kv_norm=kv_norm, w_kv=w_kv, b_w_in=b_w_in, b_w_out=b_w_out, final_norm=final_norm)
    mom1 = dict(ffn1_norm=m_ffn1_norm, ffn1_w_gate=m_ffn1_w_gate, ffn1_w_up=m_ffn1_w_up, ffn1_w_down=m_ffn1_w_down, mix_norm=m_mix_norm, ffn2_norm=m_ffn2_norm, ffn2_w_gate=m_ffn2_w_gate, ffn2_w_up=m_ffn2_w_up, ffn2_w_down=m_ffn2_w_down, mem_norm=m_mem_norm, w_mem_kv=m_w_mem_kv, a_w_in=m_a_w_in, a_v_norm=m_a_v_norm, a_w_spatial=m_a_w_spatial, a_b_spatial=m_a_b_spatial, a_w_out=m_a_w_out, kv_norm=m_kv_norm, w_kv=m_w_kv, b_w_in=m_b_w_in, b_w_out=m_b_w_out, final_norm=m_final_norm)
    mom2 = dict(ffn1_norm=v_ffn1_norm, ffn1_w_gate=v_ffn1_w_gate, ffn1_w_up=v_ffn1_w_up, ffn1_w_down=v_ffn1_w_down, mix_norm=v_mix_norm, ffn2_norm=v_ffn2_norm, ffn2_w_gate=v_ffn2_w_gate, ffn2_w_up=v_ffn2_w_up, ffn2_w_down=v_ffn2_w_down, mem_norm=v_mem_norm, w_mem_kv=v_w_mem_kv, a_w_in=v_a_w_in, a_v_norm=v_a_v_norm, a_w_spatial=v_a_w_spatial, a_b_spatial=v_a_b_spatial, a_w_out=v_a_w_out, kv_norm=v_kv_norm, w_kv=v_w_kv, b_w_in=v_b_w_in, b_w_out=v_b_w_out, final_norm=v_final_norm)

    dev = _device_index()
    xs, mem_in, target = x[0], mem[0], loss_target[0]
    d_model = xs.shape[1]
    shards = {n: weights[n] for n in SHARDED}

    def mix_keys(l):
        w_in, w_out, idx = ("a_w_in", "a_w_out", l) if l < N_A else ("b_w_in", "b_w_out", l - N_A)
        return (w_in, idx), (w_out, idx)

    def ffn_keys(ffn):
        return ([], []) if ffn is None else ([(ffn[0] + "_w_gate", ffn[1]), (ffn[0] + "_w_up", ffn[1])],
                                             [(ffn[0] + "_w_down", ffn[1])])

    def ffn_after(f, l):
        return ("ffn2", l) if f == "ffn1" else (("ffn1", l + 1) if l + 1 < DEPTH else None)

    def cut_axis(key):
        return SHARDED[key[0]] - (0 if key[1] is None else 1)

    def block(key):
        b = (shards[key[0]] if key[1] is None else shards[key[0]][key[1]]).astype(BF16)
        return jnp.swapaxes(b, 0, 1) if key[0] in TRANSPOSED else b

    def carrying(call, keys, same_src, source, store):
        if not keys:
            return call(None)
        result, arrived = call(Side([source(k) for k in keys], same_src))
        store.update(zip(keys, arrived))
        return result

    first_gu, first_down = ffn_keys(("ffn1", 0))
    first = first_gu + first_down + [mix_keys(0)[0], ("w_mem_kv", None)]
    landed = dict(zip(first, exchange([block(k) for k in first], "all", True, "gather_first")))
    assembled = {}

    def whole(n, l=None):
        if (n, l) not in assembled:
            got = landed[n, l]
            if n.endswith("_w_gate"):
                up = landed[n.replace("_w_gate", "_w_up"), l]
                assembled[n, l] = jnp.concatenate([got.reshape((-1,) + got.shape[2:]), up.reshape((-1,) + up.shape[2:])])
            elif cut_axis((n, l)) == 0:
                assembled[n, l] = got.reshape((-1,) + got.shape[2:])
            else:
                assembled[n, l] = jnp.concatenate([got[d] for d in range(N_DEV)], axis=cut_axis((n, l)))
        return assembled[n, l]

    def whole_gu(f, l):
        return whole(f + "_w_gate", l)

    vn_width = a_v_norm.shape[1]
    a_v_full = _all_sum([lax.dynamic_update_slice(jnp.zeros((N_A, N_DEV * vn_width), F32), a_v_norm, (0, dev * vn_width))],
                        "gather_v_norm")[0]
    row = lambda v: v.reshape(1, -1)
    w_mem_cat = whole("w_mem_kv").transpose(1, 0, 2).reshape(d_model, -1)
    bias = [jnp.repeat(a_b_spatial[i].T, GM_P, axis=1) for i in range(N_A)]

    mem_kv, mem_h = norm_mm(mem_in, row(mem_norm), w_mem_cat, BF16, "mem_kv", emit_h=True)

    def ffn_fwd(xin, f, l):
        keys_gu, keys_down = ffn_keys(ffn_after(f, l))
        if ffn_after(f, l) == ("ffn1", N_A):
            keys_down = keys_down + [("w_kv", None)]
        gu = carrying(lambda side: norm_mm(xin, row(weights[f + "_norm"][l]), whole_gu(f, l), BF16, "ffn_gu", side=side,
                                             w_rows=True),
                      keys_gu, True, block, landed)
        out = carrying(lambda side: swiglu_mm_res(gu, whole(f + "_w_down", l), xin, 0.5, "ffn_down", side=side),
                       keys_down, True, block, landed)
        return out, gu

    saved = []
    kv = x_kv = None
    cur = xs
    for l in range(DEPTH):
        st = {"x0": cur}
        if l == N_A:
            x_kv = cur
            kv = norm_mm(cur, row(kv_norm), whole("w_kv"), BF16, "kv_proj")
        st["x1"], st["gu1"] = ffn_fwd(cur, "ffn1", l)
        key_in, key_out = mix_keys(l)
        proj = carrying(lambda side: norm_mm(st["x1"], row(mix_norm[l]), whole(*key_in), F32 if l < N_A else BF16,
                                             "a_proj" if l < N_A else "b_proj", side=side), [key_out], True, block, landed)
        if l < N_A:
            y_tok = gmlp_fwd(proj, row(a_v_full[l]), a_w_spatial[l], bias[l], "gmlp_fwd")
            st["y"] = mem_fwd(proj, 2 * GM_W // MEM_W, mem_kv, l, y_tok, GM_W // MEM_W, "mem_fwd_a")
        else:
            st["sb_out"], y_tok = sb_fwd(proj, kv, "sb_fwd")
            st["y"] = mem_fwd(proj, SB_W // MEM_W, mem_kv, l, y_tok, SB_W // MEM_W, "mem_fwd_b")
        st["proj"] = proj
        st["x2"] = carrying(lambda side: mm_res(st["y"], whole(*key_out), st["x1"], 1.0, "mix_out", side=side),
                            [mix_keys(l + 1)[0]] if l + 1 < DEPTH else [], True, block, landed)
        cur, st["gu2"] = ffn_fwd(st["x2"], "ffn2", l)
        saved.append(st)

    loss_blk, dx, d_final = final_loss(cur, row(final_norm), target, "final_loss")
    loss = lax.psum(loss_blk[0, 0] * (0.5 / d_model), AXES)

    grads = {n: [None] * weights[n].shape[0] for n in WEIGHTS if weights[n].ndim >= 2 and n not in ("w_kv",)}
    grads["final_norm"] = d_final.reshape(-1)
    d_mem_kv = [None] * DEPTH
    d_kv = []

    summed = {}

    def pieces(key):
        g = (grads[key[0]] if key[1] is None else grads[key[0]][key[1]]).astype(BF16)
        axis = 0 if key[0] in TRANSPOSED else cut_axis(key)
        cut = g.reshape(g.shape[:axis] + (N_DEV, g.shape[axis] // N_DEV) + g.shape[axis + 1:])
        return jnp.moveaxis(cut, axis, 0)

    def ffn_bwd(dx, xin, gu, f, l):
        keys_gu, keys_down = ffn_keys(ffn_after(f, l))
        keys_mix = list(mix_keys(l)) if f == "ffn1" else ([("w_kv", None)] if l + 1 == N_A else [])
        d_gu = carrying(lambda side: mm_nt_swiglu_bwd(dx, whole(f + "_w_down", l), gu, 0.5, "ffn_dgu", side=side),
                        keys_down, False, pieces, summed)
        dx_new, d_gain, h = carrying(
            lambda side: mm_nt_normbwd(d_gu, whole_gu(f, l), xin, row(weights[f + "_norm"][l]), dx, "ffn_dx", side=side,
                                       w_rows=True),
            keys_gu, False, pieces, summed)
        d_wgu_t = carrying(lambda side: mm_tn(d_gu, h, 1.0, "ffn_dwgu", ta_target=1408, tb_target=1024, side=side),
                           keys_mix, False, pieces, summed)
        d_wdown = swiglu_mm_tn(gu, dx, 0.5, "ffn_dwdown")
        half = d_wgu_t.shape[0] // 2
        grads[f + "_w_gate"][l], grads[f + "_w_up"][l] = d_wgu_t[:half], d_wgu_t[half:]
        grads[f + "_w_down"][l] = d_wdown
        grads[f + "_norm"][l] = d_gain.reshape(-1)
        return dx_new

    for l in reversed(range(DEPTH)):
        st = saved[l]
        dx = ffn_bwd(dx, st["x2"], st["gu2"], "ffn2", l)
        proj = st["proj"]
        (key_in, idx), (key_out, _) = mix_keys(l)
        w_in, w_out = whole(key_in, idx), whole(key_out, idx)
        dy = mm_nt(dx, w_out, 1.0, "mix_dy")
        grads[key_out][idx] = mm_tn(st["y"], dx, 1.0, "mix_dwout", tb_target=1024)
        if l < N_A:
            d_uv, d_ws, d_bs, d_vgain = gmlp_bwd(proj, dy, row(a_v_full[l]), a_w_spatial[l], bias[l], "gmlp_bwd")
            grads["a_w_spatial"][l], grads["a_b_spatial"][l], grads["a_v_norm"][l] = d_ws, d_bs[:, :, 0], d_vgain.reshape(-1)
            d_proj, d_k, d_v = mem_bwd(proj, 2 * GM_W // MEM_W, mem_kv, l, dy, GM_W // MEM_W, d_uv, "mem_bwd_a")
        else:
            d_qsb, d_ksb, d_vsb = sb_bwd(proj, kv, st["sb_out"], dy, "sb_bwd")
            d_kv.append(jnp.concatenate([d_ksb, d_vsb], axis=1))
            d_proj, d_k, d_v = mem_bwd(proj, SB_W // MEM_W, mem_kv, l, dy, SB_W // MEM_W, d_qsb, "mem_bwd_b")
        d_mem_kv[l] = jnp.concatenate([d_k, d_v], axis=1)
        dx, d_gain, h = mm_nt_normbwd(d_proj, w_in, st["x1"], row(mix_norm[l]), dx, "mix_dx")
        grads["mix_norm"][l] = d_gain.reshape(-1)
        grads[key_in][idx] = mm_tn(h, d_proj, 1.0, "mix_dwin")
        dx = ffn_bwd(dx, st["x0"], st["gu1"], "ffn1", l)
        if l == N_A:
            d_kv_b = sum_leading(jnp.stack(d_kv), BF16, "kv_dsum")
            dx, d_gain, h = mm_nt_normbwd(d_kv_b, whole("w_kv"), x_kv, row(kv_norm), dx, "kv_dx")
            grads["kv_norm"] = d_gain.reshape(-1)
            grads["w_kv"] = mm_tn(h, d_kv_b, 1.0, "kv_dw")

    d_mem_all = jnp.concatenate(d_mem_kv, axis=1).astype(BF16)
    _, d_gain, _ = mm_nt_normbwd(d_mem_all, w_mem_cat, mem_in, row(mem_norm), None, "mem_dnorm")
    grads["mem_norm"] = d_gain.reshape(-1)
    d_wmem = mm_tn(mem_h, d_mem_all, 1.0, "mem_dw")
    grads["w_mem_kv"] = d_wmem.reshape(d_model, DEPTH, -1).transpose(1, 0, 2)

    last_rides = {"ffn2_w_gate": first_gu[:1], "ffn2_w_up": first_gu[1:], "ffn2_w_down": first_down + [("w_mem_kv", None)]}
    grads_small = {n: (jnp.stack(grads[n]) if isinstance(grads[n], list) else grads[n]) for n in SMALL}
    small = dict(zip(SMALL, _all_sum([grads_small[n] for n in SMALL], "sum_small")))
    small["a_v_norm"] = lax.dynamic_slice(small["a_v_norm"], (0, dev * vn_width), a_v_norm.shape)

    def parts_of(n):
        if n in small:
            return small[n][None]
        return summed[n, None] if (n, None) in summed else jnp.stack(
            [summed[n, i] for i in range(weights[n].shape[0])], axis=1)

    reduced, deltas, new_m, new_v = {}, {}, {}, {}
    for n in list(last_rides) + [n for n in WEIGHTS if n not in last_rides]:
        turn = (lambda a: jnp.swapaxes(a, -1, -2)) if n in TRANSPOSED else (lambda a: a)
        w = turn(weights[n])
        view = (lambda a: turn(a).reshape(-1, w.shape[-1]))
        parts = parts_of(n)
        res = carrying(lambda side: adamw(view(weights[n]), parts.reshape(parts.shape[0], -1, w.shape[-1]), view(mom1[n]),
                                          view(mom2[n]), "adamw", side=side), last_rides.get(n, []), False, pieces, summed)
        reduced[n], deltas[n], new_m[n], new_v[n] = [turn(r.reshape(w.shape)) for r in res]

    return (loss, dx[None], *[reduced[n] for n in WEIGHTS], *[deltas[n] for n in WEIGHTS],
            *[new_m[n] for n in WEIGHTS], *[new_v[n] for n in WEIGHTS])
```
